```python
import jax, jax.numpy as jnp
from jax import lax
import numpy as np

D_MODEL = 1024
BATCH = 32
SEQ = 2048
DEPTH = 4

N_MIXERS = 2
N_HEADS = 16
N_KV_HEADS = 2
HEAD_DIM = 64
GROUP = N_HEADS // N_KV_HEADS
WINDOW = 128
BLOCK = 128
SPAN = BLOCK + WINDOW
Q_DIM = N_HEADS * HEAD_DIM
KV_DIM = N_KV_HEADS * HEAD_DIM
QKV_DIM = Q_DIM + 2 * KV_DIM
CONV_WIDTH = 31
CONV_DIM = D_MODEL
D_FF = 4 * D_MODEL
N_MOD = 6
EPS = 1e-6
N_ATTN_LAYERS = (DEPTH + 1) // 2
N_CONV_LAYERS = DEPTH // 2

kernel_name = "interleaved_swa_sink_conformer_conv_adaln"


def alibi_slopes(n_heads):
    return jnp.asarray(np.array([2.0 ** (-8.0 * (h + 1) / n_heads) for h in range(n_heads)], dtype=np.float32))


def rmsnorm(x, g):
    xf = x.astype(jnp.float32)
    r = lax.rsqrt(jnp.mean(xf * xf, axis=-1, keepdims=True) + EPS)
    return (xf * r * g.astype(jnp.float32)).astype(x.dtype)


def layernorm(x, g, b):
    xf = x.astype(jnp.float32)
    mu = jnp.mean(xf, axis=-1, keepdims=True)
    var = jnp.mean(jnp.square(xf - mu), axis=-1, keepdims=True)
    y = (xf - mu) * lax.rsqrt(var + EPS) * g.astype(jnp.float32) + b.astype(jnp.float32)
    return y.astype(x.dtype)


def modulate(h, shift, scale):
    return h * (1 + scale[:, None, :]) + shift[:, None, :]


def sliding_window_attention(h, w_qkv, b_qkv, w_o, b_o, sinks):
    B, S, _ = h.shape
    qkv = h @ w_qkv + b_qkv
    q, k, v = jnp.split(qkv, [Q_DIM, Q_DIM + KV_DIM], axis=-1)
    q = q.reshape(B, S, N_KV_HEADS, GROUP, HEAD_DIM) * (HEAD_DIM ** -0.5)
    k = k.reshape(B, S, N_KV_HEADS, HEAD_DIM)
    v = v.reshape(B, S, N_KV_HEADS, HEAD_DIM)
    pad = ((0, 0), (WINDOW, 0), (0, 0), (0, 0))
    k_pad = jnp.pad(k, pad)
    v_pad = jnp.pad(v, pad)
    q_idx = jnp.arange(BLOCK)[:, None] + WINDOW
    k_idx = jnp.arange(SPAN)[None, :]
    dist = q_idx - k_idx
    band = (dist >= 0) & (dist < WINDOW)
    slopes = alibi_slopes(N_HEADS).reshape(N_KV_HEADS, GROUP)
    alibi = -slopes[:, :, None, None] * dist.astype(jnp.float32)[None, None]
    sink = sinks.astype(jnp.float32).reshape(N_KV_HEADS, GROUP)[None, :, :, None, None]

    def one_block(i):
        start = i * BLOCK
        qb = lax.dynamic_slice_in_dim(q, start, BLOCK, axis=1)
        kb = lax.dynamic_slice_in_dim(k_pad, start, SPAN, axis=1)
        vb = lax.dynamic_slice_in_dim(v_pad, start, SPAN, axis=1)
        s = jnp.einsum('bqkgd,bskd->bkgqs', qb, kb).astype(jnp.float32) + alibi
        mask = band & ((start - WINDOW + k_idx) >= 0)
        s = jnp.where(mask[None, None, None], s, -jnp.inf)
        m = jnp.maximum(jnp.max(s, axis=-1, keepdims=True), sink)
        p = jnp.exp(s - m)
        denom = jnp.sum(p, axis=-1, keepdims=True) + jnp.exp(sink - m)
        p = (p / denom).astype(vb.dtype)
        return jnp.einsum('bkgqs,bskd->bqkgd', p, vb)

    out = lax.map(one_block, jnp.arange(S // BLOCK))
    out = jnp.moveaxis(out, 0, 1).reshape(B, S, Q_DIM)
    return out @ w_o + b_o


def conformer_conv(h, w_pw1, b_pw1, w_dw, b_dw, ln_g, ln_b, w_pw2, b_pw2):
    u = h @ w_pw1 + b_pw1
    a, g = jnp.split(u, 2, axis=-1)
    u = a * jax.nn.sigmoid(g)
    u = jnp.pad(u, ((0, 0), (CONV_WIDTH - 1, 0), (0, 0)))
    u = lax.conv_general_dilated(u, w_dw[:, None, :].astype(u.dtype), window_strides=(1,), padding='VALID',
                                 dimension_numbers=('NWC', 'WIO', 'NWC'),
                                 feature_group_count=CONV_DIM) + b_dw
    u = jax.nn.silu(layernorm(u, ln_g, ln_b))
    return u @ w_pw2 + b_pw2


def squared_relu_mlp(h, w_up, w_down):
    return jnp.square(jax.nn.relu(h @ w_up)) @ w_down


def _fwd_setup_inputs(seed: int = 0) -> dict:
    key = jax.random.key(seed)
    ks = jax.random.split(key, 32)
    D = D_MODEL
    nrm = lambda k, shape, s: jax.random.normal(k, shape, jnp.float32) * s
    return {
        "x": nrm(ks[0], (BATCH, SEQ, D), 1.0),
        "c": nrm(ks[1], (BATCH, D), 1.0),
        "w_mod": nrm(ks[2], (DEPTH, D, N_MOD * D), 0.5 * D ** -0.5),
        "b_mod": nrm(ks[3], (DEPTH, N_MOD * D), 0.01),
        "norm_mix": 1.0 + nrm(ks[4], (DEPTH, D), 0.02),
        "norm_mlp": 1.0 + nrm(ks[5], (DEPTH, D), 0.02),
        "w_qkv": nrm(ks[6], (N_ATTN_LAYERS, D, QKV_DIM), D ** -0.5),
        "b_qkv": nrm(ks[7], (N_ATTN_LAYERS, QKV_DIM), 0.01),
        "w_o": nrm(ks[8], (N_ATTN_LAYERS, Q_DIM, D), Q_DIM ** -0.5),
        "b_o": nrm(ks[9], (N_ATTN_LAYERS, D), 0.01),
        "sinks": nrm(ks[10], (N_ATTN_LAYERS, N_HEADS), 0.5),
        "w_pw1": nrm(ks[11], (N_CONV_LAYERS, D, 2 * CONV_DIM), D ** -0.5),
        "b_pw1": nrm(ks[12], (N_CONV_LAYERS, 2 * CONV_DIM), 0.01),
        "w_dw": nrm(ks[13], (N_CONV_LAYERS, CONV_WIDTH, CONV_DIM), CONV_WIDTH ** -0.5),
        "b_dw": nrm(ks[14], (N_CONV_LAYERS, CONV_DIM), 0.01),
        "conv_ln_g": 1.0 + nrm(ks[15], (N_CONV_LAYERS, CONV_DIM), 0.02),
        "conv_ln_b": nrm(ks[16], (N_CONV_LAYERS, CONV_DIM), 0.01),
        "w_pw2": nrm(ks[17], (N_CONV_LAYERS, CONV_DIM, D), CONV_DIM ** -0.5),
        "b_pw2": nrm(ks[18], (N_CONV_LAYERS, D), 0.01),
        "w_up": nrm(ks[19], (DEPTH, D, D_FF), D ** -0.5),
        "w_down": nrm(ks[20], (DEPTH, D_FF, D), D_FF ** -0.5),
        "final_norm": 1.0 + nrm(ks[21], (D,), 0.02),
    }


def _fwd_reference(x, c, w_mod, b_mod, norm_mix, norm_mlp, w_qkv, b_qkv, w_o, b_o, sinks,
              w_pw1, b_pw1, w_dw, b_dw, conv_ln_g, conv_ln_b, w_pw2, b_pw2,
              w_up, w_down, final_norm):
    cs = jax.nn.silu(c)
    for i in range(DEPTH):
        mod = cs @ w_mod[i] + b_mod[i]
        sh1, sc1, g1, sh2, sc2, g2 = jnp.split(mod, N_MOD, axis=-1)
        h = modulate(rmsnorm(x, norm_mix[i]), sh1, sc1)
        j = i // N_MIXERS
        if i % N_MIXERS == 0:
            y = sliding_window_attention(h, w_qkv[j], b_qkv[j], w_o[j], b_o[j], sinks[j])
        else:
            y = conformer_conv(h, w_pw1[j], b_pw1[j], w_dw[j], b_dw[j], conv_ln_g[j], conv_ln_b[j],
                               w_pw2[j], b_pw2[j])
        x = x + g1[:, None, :] * y
        h = modulate(rmsnorm(x, norm_mlp[i]), sh2, sc2)
        x = x + g2[:, None, :] * squared_relu_mlp(h, w_up[i], w_down[i])
    return rmsnorm(x, final_norm)


import jax as _jax
import jax.numpy as _jnp

TWIN_FORMAT = 'train_step'
FWD_PARAMS = ['x', 'c', 'w_mod', 'b_mod', 'norm_mix', 'norm_mlp', 'w_qkv', 'b_qkv', 'w_o', 'b_o', 'sinks', 'w_pw1', 'b_pw1', 'w_dw', 'b_dw', 'conv_ln_g', 'conv_ln_b', 'w_pw2', 'b_pw2', 'w_up', 'w_down', 'final_norm']
TWIN_WEIGHTS = ['w_mod', 'b_mod', 'norm_mix', 'norm_mlp', 'w_qkv', 'b_qkv', 'w_o', 'b_o', 'sinks', 'w_pw1', 'b_pw1', 'w_dw', 'b_dw', 'conv_ln_g', 'conv_ln_b', 'w_pw2', 'b_pw2', 'w_up', 'w_down', 'final_norm']
TWIN_DIFF_INPUT = 'x'
TWIN_INPUTS = ['x', 'c', 'w_mod', 'b_mod', 'norm_mix', 'norm_mlp', 'w_qkv', 'b_qkv', 'w_o', 'b_o', 'sinks', 'w_pw1', 'b_pw1', 'w_dw', 'b_dw', 'conv_ln_g', 'conv_ln_b', 'w_pw2', 'b_pw2', 'w_up', 'w_down', 'final_norm', 'loss_target', 'm_w_mod', 'm_b_mod', 'm_norm_mix', 'm_norm_mlp', 'm_w_qkv', 'm_b_qkv', 'm_w_o', 'm_b_o', 'm_sinks', 'm_w_pw1', 'm_b_pw1', 'm_w_dw', 'm_b_dw', 'm_conv_ln_g', 'm_conv_ln_b', 'm_w_pw2', 'm_b_pw2', 'm_w_up', 'm_w_down', 'm_final_norm', 'v_w_mod', 'v_b_mod', 'v_norm_mix', 'v_norm_mlp', 'v_w_qkv', 'v_b_qkv', 'v_w_o', 'v_b_o', 'v_sinks', 'v_w_pw1', 'v_b_pw1', 'v_w_dw', 'v_b_dw', 'v_conv_ln_g', 'v_conv_ln_b', 'v_w_pw2', 'v_b_pw2', 'v_w_up', 'v_w_down', 'v_final_norm']
TWIN_OUTPUTS = ['loss', 'grad_x', 'grad_w_mod', 'grad_b_mod', 'grad_norm_mix', 'grad_norm_mlp', 'grad_w_qkv', 'grad_b_qkv', 'grad_w_o', 'grad_b_o', 'grad_sinks', 'grad_w_pw1', 'grad_b_pw1', 'grad_w_dw', 'grad_b_dw', 'grad_conv_ln_g', 'grad_conv_ln_b', 'grad_w_pw2', 'grad_b_pw2', 'grad_w_up', 'grad_w_down', 'grad_final_norm', 'delta_w_mod', 'delta_b_mod', 'delta_norm_mix', 'delta_norm_mlp', 'delta_w_qkv', 'delta_b_qkv', 'delta_w_o', 'delta_b_o', 'delta_sinks', 'delta_w_pw1', 'delta_b_pw1', 'delta_w_dw', 'delta_b_dw', 'delta_conv_ln_g', 'delta_conv_ln_b', 'delta_w_pw2', 'delta_b_pw2', 'delta_w_up', 'delta_w_down', 'delta_final_norm', 'new_m_w_mod', 'new_m_b_mod', 'new_m_norm_mix', 'new_m_norm_mlp', 'new_m_w_qkv', 'new_m_b_qkv', 'new_m_w_o', 'new_m_b_o', 'new_m_sinks', 'new_m_w_pw1', 'new_m_b_pw1', 'new_m_w_dw', 'new_m_b_dw', 'new_m_conv_ln_g', 'new_m_conv_ln_b', 'new_m_w_pw2', 'new_m_b_pw2', 'new_m_w_up', 'new_m_w_down', 'new_m_final_norm', 'new_v_w_mod', 'new_v_b_mod', 'new_v_norm_mix', 'new_v_norm_mlp', 'new_v_w_qkv', 'new_v_b_qkv', 'new_v_w_o', 'new_v_b_o', 'new_v_sinks', 'new_v_w_pw1', 'new_v_b_pw1', 'new_v_w_dw', 'new_v_b_dw', 'new_v_conv_ln_g', 'new_v_conv_ln_b', 'new_v_w_pw2', 'new_v_b_pw2', 'new_v_w_up', 'new_v_w_down', 'new_v_final_norm']
TWIN_LEAF_KINDS = {'loss': 'loss', 'grad_x': 'grad_x', 'grad_w_mod': 'grad_w', 'grad_b_mod': 'grad_w', 'grad_norm_mix': 'grad_w', 'grad_norm_mlp': 'grad_w', 'grad_w_qkv': 'grad_w', 'grad_b_qkv': 'grad_w', 'grad_w_o': 'grad_w', 'grad_b_o': 'grad_w', 'grad_sinks': 'grad_w', 'grad_w_pw1': 'grad_w', 'grad_b_pw1': 'grad_w', 'grad_w_dw': 'grad_w', 'grad_b_dw': 'grad_w', 'grad_conv_ln_g': 'grad_w', 'grad_conv_ln_b': 'grad_w', 'grad_w_pw2': 'grad_w', 'grad_b_pw2': 'grad_w', 'grad_w_up': 'grad_w', 'grad_w_down': 'grad_w', 'grad_final_norm': 'grad_w', 'delta_w_mod': 'delta_w', 'delta_b_mod': 'delta_w', 'delta_norm_mix': 'delta_w', 'delta_norm_mlp': 'delta_w', 'delta_w_qkv': 'delta_w', 'delta_b_qkv': 'delta_w', 'delta_w_o': 'delta_w', 'delta_b_o': 'delta_w', 'delta_sinks': 'delta_w', 'delta_w_pw1': 'delta_w', 'delta_b_pw1': 'delta_w', 'delta_w_dw': 'delta_w', 'delta_b_dw': 'delta_w', 'delta_conv_ln_g': 'delta_w', 'delta_conv_ln_b': 'delta_w', 'delta_w_pw2': 'delta_w', 'delta_b_pw2': 'delta_w', 'delta_w_up': 'delta_w', 'delta_w_down': 'delta_w', 'delta_final_norm': 'delta_w', 'new_m_w_mod': 'new_m', 'new_m_b_mod': 'new_m', 'new_m_norm_mix': 'new_m', 'new_m_norm_mlp': 'new_m', 'new_m_w_qkv': 'new_m', 'new_m_b_qkv': 'new_m', 'new_m_w_o': 'new_m', 'new_m_b_o': 'new_m', 'new_m_sinks': 'new_m', 'new_m_w_pw1': 'new_m', 'new_m_b_pw1': 'new_m', 'new_m_w_dw': 'new_m', 'new_m_b_dw': 'new_m', 'new_m_conv_ln_g': 'new_m', 'new_m_conv_ln_b': 'new_m', 'new_m_w_pw2': 'new_m', 'new_m_b_pw2': 'new_m', 'new_m_w_up': 'new_m', 'new_m_w_down': 'new_m', 'new_m_final_norm': 'new_m', 'new_v_w_mod': 'new_v', 'new_v_b_mod': 'new_v', 'new_v_norm_mix': 'new_v', 'new_v_norm_mlp': 'new_v', 'new_v_w_qkv': 'new_v', 'new_v_b_qkv': 'new_v', 'new_v_w_o': 'new_v', 'new_v_b_o': 'new_v', 'new_v_sinks': 'new_v', 'new_v_w_pw1': 'new_v', 'new_v_b_pw1': 'new_v', 'new_v_w_dw': 'new_v', 'new_v_b_dw': 'new_v', 'new_v_conv_ln_g': 'new_v', 'new_v_conv_ln_b': 'new_v', 'new_v_w_pw2': 'new_v', 'new_v_b_pw2': 'new_v', 'new_v_w_up': 'new_v', 'new_v_w_down': 'new_v', 'new_v_final_norm': 'new_v'}


def _forward(args):
    return _fwd_reference(*[args[k] for k in FWD_PARAMS])


def _output_shape():
    out = _jax.eval_shape(lambda: _forward(_fwd_setup_inputs(0)))
    return out.shape, out.dtype

N_MICROBATCH = 1
ADAM_LR = 0.001
ADAM_B1 = 0.9
ADAM_B2 = 0.999
ADAM_EPS = 1e-08
ADAM_WD = 0.01
ADAM_STEP = 10
PER_EXAMPLE_BATCH_AXIS = {'x': 0, 'c': 0, 'loss_target': 0}
SHARED_INPUTS = []
_WEIGHT_DTYPES = {'w_mod': _jnp.float32, 'b_mod': _jnp.float32, 'norm_mix': _jnp.float32, 'norm_mlp': _jnp.float32, 'w_qkv': _jnp.float32, 'b_qkv': _jnp.float32, 'w_o': _jnp.float32, 'b_o': _jnp.float32, 'sinks': _jnp.float32, 'w_pw1': _jnp.float32, 'b_pw1': _jnp.float32, 'w_dw': _jnp.float32, 'b_dw': _jnp.float32, 'conv_ln_g': _jnp.float32, 'conv_ln_b': _jnp.float32, 'w_pw2': _jnp.float32, 'b_pw2': _jnp.float32, 'w_up': _jnp.float32, 'w_down': _jnp.float32, 'final_norm': _jnp.float32}
MOMENT_SCALE = {'w_mod': 1.065567e-01, 'b_mod': 1.778175e-01, 'norm_mix': 4.183615e-02, 'norm_mlp': 1.084373e-01, 'w_qkv': 4.178416e-02, 'b_qkv': 7.002594e-02, 'w_o': 3.622896e-02, 'b_o': 7.829887e-02, 'sinks': 7.394752e-02, 'w_pw1': 3.199547e-02, 'b_pw1': 3.175358e-02, 'w_dw': 4.087525e-02, 'b_dw': 8.011172e-02, 'conv_ln_g': 4.749657e-02, 'conv_ln_b': 4.524105e-02, 'w_pw2': 4.146321e-02, 'b_pw2': 7.880051e-02, 'w_up': 5.474066e-02, 'w_down': 1.007092e-01, 'final_norm': 6.516366e+01}


def _to_microbatches(a, axis):
    t = _jnp.moveaxis(a, axis, 0)
    t = t.reshape((N_MICROBATCH, t.shape[0] // N_MICROBATCH) + t.shape[1:])
    return _jnp.moveaxis(t, 1, axis + 1)


def setup_inputs(seed: int = 0) -> dict:
    inp = _fwd_setup_inputs(seed)
    key = _jax.random.fold_in(_jax.random.key(seed), 7919)
    shape, _ = _output_shape()
    out = dict(inp)
    out["loss_target"] = _jax.random.normal(_jax.random.fold_in(key, 0), shape, _jnp.float32)
    for i, name in enumerate(TWIN_WEIGHTS):
        w = inp[name].astype(_jnp.float32)
        if MOMENT_SCALE is None:
            s = _jnp.sqrt(_jnp.mean(_jnp.square(w)) + 1e-30)
        else:
            s = MOMENT_SCALE[name]
        km, kv = _jax.random.split(_jax.random.fold_in(key, i + 1))
        out[name] = w
        out["m_" + name] = s * _jax.random.normal(km, w.shape, _jnp.float32)
        out["v_" + name] = (s * s) * _jax.random.uniform(kv, w.shape, _jnp.float32, 0.5, 1.5)
    if N_MICROBATCH > 1:
        for name, axis in PER_EXAMPLE_BATCH_AXIS.items():
            out[name] = _to_microbatches(out[name], axis)
    return {'x': out['x'], 'c': out['c'], 'w_mod': out['w_mod'], 'b_mod': out['b_mod'], 'norm_mix': out['norm_mix'], 'norm_mlp': out['norm_mlp'], 'w_qkv': out['w_qkv'], 'b_qkv': out['b_qkv'], 'w_o': out['w_o'], 'b_o': out['b_o'], 'sinks': out['sinks'], 'w_pw1': out['w_pw1'], 'b_pw1': out['b_pw1'], 'w_dw': out['w_dw'], 'b_dw': out['b_dw'], 'conv_ln_g': out['conv_ln_g'], 'conv_ln_b': out['conv_ln_b'], 'w_pw2': out['w_pw2'], 'b_pw2': out['b_pw2'], 'w_up': out['w_up'], 'w_down': out['w_down'], 'final_norm': out['final_norm'], 'loss_target': out['loss_target'], 'm_w_mod': out['m_w_mod'], 'm_b_mod': out['m_b_mod'], 'm_norm_mix': out['m_norm_mix'], 'm_norm_mlp': out['m_norm_mlp'], 'm_w_qkv': out['m_w_qkv'], 'm_b_qkv': out['m_b_qkv'], 'm_w_o': out['m_w_o'], 'm_b_o': out['m_b_o'], 'm_sinks': out['m_sinks'], 'm_w_pw1': out['m_w_pw1'], 'm_b_pw1': out['m_b_pw1'], 'm_w_dw': out['m_w_dw'], 'm_b_dw': out['m_b_dw'], 'm_conv_ln_g': out['m_conv_ln_g'], 'm_conv_ln_b': out['m_conv_ln_b'], 'm_w_pw2': out['m_w_pw2'], 'm_b_pw2': out['m_b_pw2'], 'm_w_up': out['m_w_up'], 'm_w_down': out['m_w_down'], 'm_final_norm': out['m_final_norm'], 'v_w_mod': out['v_w_mod'], 'v_b_mod': out['v_b_mod'], 'v_norm_mix': out['v_norm_mix'], 'v_norm_mlp': out['v_norm_mlp'], 'v_w_qkv': out['v_w_qkv'], 'v_b_qkv': out['v_b_qkv'], 'v_w_o': out['v_w_o'], 'v_b_o': out['v_b_o'], 'v_sinks': out['v_sinks'], 'v_w_pw1': out['v_w_pw1'], 'v_b_pw1': out['v_b_pw1'], 'v_w_dw': out['v_w_dw'], 'v_b_dw': out['v_b_dw'], 'v_conv_ln_g': out['v_conv_ln_g'], 'v_conv_ln_b': out['v_conv_ln_b'], 'v_w_pw2': out['v_w_pw2'], 'v_b_pw2': out['v_b_pw2'], 'v_w_up': out['v_w_up'], 'v_w_down': out['v_w_down'], 'v_final_norm': out['v_final_norm']}


def _loss(weights, diff, rest, loss_target):
    with _jax.named_scope("forward"):
        args = {**rest, TWIN_DIFF_INPUT: diff, **{k: w.astype(_WEIGHT_DTYPES[k]) for k, w in weights.items()}}
        y = _forward(args)
    with _jax.named_scope("loss_head"):
        err = _jnp.square(y.astype(_jnp.float32) - loss_target)
        return 0.5 * _jnp.sum(_jnp.mean(err, axis=-1)) if err.ndim else 0.5 * err


def _adamw(w, g, m, v):
    m = ADAM_B1 * m + (1.0 - ADAM_B1) * g
    v = ADAM_B2 * v + (1.0 - ADAM_B2) * _jnp.square(g)
    m_hat = m / (1.0 - ADAM_B1 ** ADAM_STEP)
    v_hat = v / (1.0 - ADAM_B2 ** ADAM_STEP)
    delta = -ADAM_LR * (m_hat / (_jnp.sqrt(v_hat) + ADAM_EPS) + ADAM_WD * w)
    return delta, m, v


def reference(x, c, w_mod, b_mod, norm_mix, norm_mlp, w_qkv, b_qkv, w_o, b_o, sinks, w_pw1, b_pw1, w_dw, b_dw, conv_ln_g, conv_ln_b, w_pw2, b_pw2, w_up, w_down, final_norm, loss_target, m_w_mod, m_b_mod, m_norm_mix, m_norm_mlp, m_w_qkv, m_b_qkv, m_w_o, m_b_o, m_sinks, m_w_pw1, m_b_pw1, m_w_dw, m_b_dw, m_conv_ln_g, m_conv_ln_b, m_w_pw2, m_b_pw2, m_w_up, m_w_down, m_final_norm, v_w_mod, v_b_mod, v_norm_mix, v_norm_mlp, v_w_qkv, v_b_qkv, v_w_o, v_b_o, v_sinks, v_w_pw1, v_b_pw1, v_w_dw, v_b_dw, v_conv_ln_g, v_conv_ln_b, v_w_pw2, v_b_pw2, v_w_up, v_w_down, v_final_norm):
    given = dict(x=x, c=c, w_mod=w_mod, b_mod=b_mod, norm_mix=norm_mix, norm_mlp=norm_mlp, w_qkv=w_qkv, b_qkv=b_qkv, w_o=w_o, b_o=b_o, sinks=sinks, w_pw1=w_pw1, b_pw1=b_pw1, w_dw=w_dw, b_dw=b_dw, conv_ln_g=conv_ln_g, conv_ln_b=conv_ln_b, w_pw2=w_pw2, b_pw2=b_pw2, w_up=w_up, w_down=w_down, final_norm=final_norm, loss_target=loss_target, m_w_mod=m_w_mod, m_b_mod=m_b_mod, m_norm_mix=m_norm_mix, m_norm_mlp=m_norm_mlp, m_w_qkv=m_w_qkv, m_b_qkv=m_b_qkv, m_w_o=m_w_o, m_b_o=m_b_o, m_sinks=m_sinks, m_w_pw1=m_w_pw1, m_b_pw1=m_b_pw1, m_w_dw=m_w_dw, m_b_dw=m_b_dw, m_conv_ln_g=m_conv_ln_g, m_conv_ln_b=m_conv_ln_b, m_w_pw2=m_w_pw2, m_b_pw2=m_b_pw2, m_w_up=m_w_up, m_w_down=m_w_down, m_final_norm=m_final_norm, v_w_mod=v_w_mod, v_b_mod=v_b_mod, v_norm_mix=v_norm_mix, v_norm_mlp=v_norm_mlp, v_w_qkv=v_w_qkv, v_b_qkv=v_b_qkv, v_w_o=v_w_o, v_b_o=v_b_o, v_sinks=v_sinks, v_w_pw1=v_w_pw1, v_b_pw1=v_b_pw1, v_w_dw=v_w_dw, v_b_dw=v_b_dw, v_conv_ln_g=v_conv_ln_g, v_conv_ln_b=v_conv_ln_b, v_w_pw2=v_w_pw2, v_b_pw2=v_b_pw2, v_w_up=v_w_up, v_w_down=v_w_down, v_final_norm=v_final_norm)
    weights = {n: given[n] for n in TWIN_WEIGHTS}
    shared = {n: given[n] for n in SHARED_INPUTS}
    per_example = {n: given[n] for n in ['x', 'c']}
    grad_fn = _jax.value_and_grad(_loss, argnums=(0, 1))

    def one_microbatch(ex, loss_target):
        ex = dict(ex)
        diff = ex.pop(TWIN_DIFF_INPUT)
        return grad_fn(weights, diff, {**shared, **ex}, loss_target)

    if N_MICROBATCH == 1:
        loss, (grad_w, grad_x) = one_microbatch(per_example, given["loss_target"])
    else:
        def body(carry, xs):
            loss_sum, grad_sum = carry
            l_k, (gw_k, gx_k) = one_microbatch(xs[0], xs[1])
            with _jax.named_scope("update"):
                return (loss_sum + l_k, _jax.tree.map(_jnp.add, grad_sum, gw_k)), gx_k

        init = (_jnp.zeros((), _jnp.float32), _jax.tree.map(_jnp.zeros_like, weights))
        (loss, grad_w), grad_x = _jax.lax.scan(body, init, (per_example, given["loss_target"]))
    with _jax.named_scope("update"):
        delta_w, new_m, new_v = {}, {}, {}
        for n in TWIN_WEIGHTS:
            delta_w[n], new_m[n], new_v[n] = _adamw(weights[n], grad_w[n], given["m_" + n], given["v_" + n])
    return (loss, grad_x, *[grad_w[n] for n in TWIN_WEIGHTS], *[delta_w[n] for n in TWIN_WEIGHTS],
            *[new_m[n] for n in TWIN_WEIGHTS], *[new_v[n] for n in TWIN_WEIGHTS])
```

```python
import functools

import jax
import jax.numpy as jnp
from jax import lax
from jax.experimental import pallas as pl
from jax.experimental.pallas import tpu as pltpu

F32, BF16 = jnp.float32, jnp.bfloat16
MESH = pl.DeviceIdType.MESH
N_CHIPS = 4
N_DEV = 8
LANES = 128
SUBLANES = 8
VMEM_LIMIT = 48 * 1024 * 1024

NORM_EPS = 1e-6
HEAD_DIM = 64
N_KV_HEADS = 2
WINDOW = 128
CONV_WIDTH = 31
CONV_HALO = 32
CONV_ROWS = 32
N_MOD = 6

ADAM_LR, ADAM_B1, ADAM_B2, ADAM_EPS, ADAM_WD, ADAM_STEP = 0.001, 0.9, 0.999, 1e-08, 0.01, 10

HBM_SPEC = pl.BlockSpec(memory_space=pltpu.HBM)
VMEM_SPEC = pl.BlockSpec(memory_space=pltpu.VMEM)
SMEM_SPEC = pl.BlockSpec(memory_space=pltpu.SMEM)


def _params(*sem):
    return pltpu.CompilerParams(dimension_semantics=sem or None, vmem_limit_bytes=VMEM_LIMIT)


def _row_tile(rows, width_bytes, target=1 << 20):
    t = rows
    while t % 2 == 0 and t > SUBLANES and t * width_bytes > target:
        t //= 2
    return t


def _mm(a, b, *, name, nt=False, tm=512, tn=512, epi="plain", out_dtype=BF16,
        bias=None, act=None, resid=None, gate=None, seq=None):
    m, k = a.shape
    n = b.shape[0] if nt else b.shape[1]
    tm, tn = min(tm, m), min(tn, n)
    assert m % tm == 0 and n % tn == 0
    dims = (((1,), (1,)), ((), ())) if nt else (((1,), (0,)), ((), ()))
    tile = pl.BlockSpec((tm, tn), lambda j, i: (i, j))
    operands = [a, b]
    in_specs = [pl.BlockSpec((tm, k), lambda j, i: (i, 0)),
                pl.BlockSpec((tn, k), lambda j, i: (j, 0)) if nt else pl.BlockSpec((k, tn), lambda j, i: (0, j))]
    if bias is not None:
        operands.append(bias.reshape(1, n))
        in_specs.append(pl.BlockSpec((1, tn), lambda j, i: (0, j)))
    if epi == "dact":
        operands.append(act)
        in_specs.append(tile)
    if epi == "resid":
        assert seq % tm == 0
        operands += [resid, gate]
        in_specs += [tile, pl.BlockSpec((None, 1, tn), lambda j, i: (i * tm // seq, 0, j))]
        out_shape = (jax.ShapeDtypeStruct((m, n), BF16), jax.ShapeDtypeStruct((m, n), F32))
        out_specs = (tile, tile)
    else:
        out_shape = jax.ShapeDtypeStruct((m, n), out_dtype)
        out_specs = tile

    def body(*refs):
        it = iter(refs)
        a_ref, b_ref = next(it), next(it)
        acc = lax.dot_general(a_ref[...], b_ref[...], dims, preferred_element_type=F32)
        if bias is not None:
            acc = acc + next(it)[...]
        if epi == "plain":
            next(it)[...] = acc.astype(out_dtype)
        elif epi == "relu2":
            r = jnp.maximum(acc, 0.0)
            next(it)[...] = (r * r).astype(out_dtype)
        elif epi == "dact":
            act_ref = next(it)
            next(it)[...] = (acc * (2.0 * jnp.sqrt(act_ref[...].astype(F32)))).astype(out_dtype)
        else:
            resid_ref, gate_ref = next(it), next(it)
            y_ref, x_ref = next(it), next(it)
            y_ref[...] = acc.astype(BF16)
            x_ref[...] = resid_ref[...] + gate_ref[...] * acc

    return pl.pallas_call(body, out_shape=out_shape, grid=(n // tn, m // tm), in_specs=in_specs,
                          out_specs=out_specs, name=name,
                          compiler_params=_params("parallel", "parallel"))(*operands)


def _mm_tn(a, b, *, name, tm=1024, tn=1024, tk=512, col_shards=None):
    t, m = a.shape
    n = b.shape[1]
    tm, tk = min(tm, m), min(tk, t)
    if col_shards is None:
        tn = min(tn, n)
        out_shape = jax.ShapeDtypeStruct((m, n), F32)
        out_spec = pl.BlockSpec((tm, tn), lambda i, j, k: (i, j))
    else:
        per = n // col_shards
        tn = min(tn, per)
        assert per % tn == 0
        out_shape = jax.ShapeDtypeStruct((col_shards, m, per), F32)
        out_spec = pl.BlockSpec((None, tm, tn), lambda i, j, k: (j // (per // tn), i, j % (per // tn)))
    assert m % tm == 0 and n % tn == 0 and t % tk == 0

    def body(a_ref, b_ref, o_ref):
        @pl.when(pl.program_id(2) == 0)
        def _():
            o_ref[...] = jnp.zeros_like(o_ref)

        o_ref[...] += lax.dot_general(a_ref[...], b_ref[...], (((0,), (0,)), ((), ())),
                                      preferred_element_type=F32)

    return pl.pallas_call(body, out_shape=out_shape, grid=(m // tm, n // tn, t // tk),
                          in_specs=[pl.BlockSpec((tk, tm), lambda i, j, k: (k, i)),
                                    pl.BlockSpec((tk, tn), lambda i, j, k: (k, j))],
                          out_specs=out_spec, name=name,
                          compiler_params=_params("parallel", "parallel", "arbitrary"))(a, b)


def _normmod(x, gamma, sc, sh, *, name):
    bsz, s, d = x.shape
    ts = _row_tile(s, d * 4, 2 << 20)

    def body(x_ref, g_ref, sc_ref, sh_ref, o_ref):
        xf = x_ref[...]
        r = lax.rsqrt(jnp.mean(xf * xf, axis=-1, keepdims=True) + NORM_EPS)
        o_ref[...] = (xf * r * g_ref[...] * (1.0 + sc_ref[...]) + sh_ref[...]).astype(BF16)

    row = pl.BlockSpec((None, ts, d), lambda b, i: (b, i, 0))
    per_ex = pl.BlockSpec((None, 1, d), lambda b, i: (b, 0, 0))
    return pl.pallas_call(body, out_shape=jax.ShapeDtypeStruct(x.shape, BF16), grid=(bsz, s // ts),
                          in_specs=[row, pl.BlockSpec((1, d), lambda b, i: (0, 0)), per_ex, per_ex],
                          out_specs=row, name=name, compiler_params=_params("parallel", "parallel"))(x, gamma, sc, sh)


def _normmod_bwd(x, dh, dres, gamma, sc, *, name):
    bsz, s, d = x.shape
    ts = _row_tile(s, d * 4, 1 << 20)

    def body(x_ref, dh_ref, dres_ref, g_ref, sc_ref, dx_ref, p_ref, dsh_ref):
        xf = x_ref[...]
        r = lax.rsqrt(jnp.mean(xf * xf, axis=-1, keepdims=True) + NORM_EPS)
        xhat = xf * r
        dh_v = dh_ref[...]
        dxhat = dh_v * (g_ref[...] * (1.0 + sc_ref[...]))
        dx_ref[...] = dres_ref[...] + r * (dxhat - xhat * jnp.mean(dxhat * xhat, axis=-1, keepdims=True))

        @pl.when(pl.program_id(1) == 0)
        def _():
            p_ref[...] = jnp.zeros_like(p_ref)
            dsh_ref[...] = jnp.zeros_like(dsh_ref)

        p_ref[...] += jnp.sum(dh_v * xhat, axis=0, keepdims=True)
        dsh_ref[...] += jnp.sum(dh_v, axis=0, keepdims=True)

    row = pl.BlockSpec((None, ts, d), lambda b, i: (b, i, 0))
    per_ex = pl.BlockSpec((None, 1, d), lambda b, i: (b, 0, 0))
    vec = jax.ShapeDtypeStruct((bsz, 1, d), F32)
    return pl.pallas_call(body, out_shape=(jax.ShapeDtypeStruct(x.shape, F32), vec, vec), grid=(bsz, s // ts),
                          in_specs=[row, row, row, pl.BlockSpec((1, d), lambda b, i: (0, 0)), per_ex],
                          out_specs=(row, per_ex, per_ex), name=name,
                          compiler_params=_params("parallel", "arbitrary"))(x, dh, dres, gamma, sc)


def _gate_bwd(dx, y, gate, *, name):
    bsz, s, d = dx.shape
    ts = _row_tile(s, d * 4, 2 << 20)

    def body(dx_ref, y_ref, g_ref, dy_ref, dg_ref, sdx_ref):
        dxv = dx_ref[...]
        dy_ref[...] = (g_ref[...] * dxv).astype(BF16)

        @pl.when(pl.program_id(1) == 0)
        def _():
            dg_ref[...] = jnp.zeros_like(dg_ref)
            sdx_ref[...] = jnp.zeros_like(sdx_ref)

        dg_ref[...] += jnp.sum(dxv * y_ref[...].astype(F32), axis=0, keepdims=True)
        sdx_ref[...] += jnp.sum(dxv, axis=0, keepdims=True)

    row = pl.BlockSpec((None, ts, d), lambda b, i: (b, i, 0))
    per_ex = pl.BlockSpec((None, 1, d), lambda b, i: (b, 0, 0))
    vec = jax.ShapeDtypeStruct((bsz, 1, d), F32)
    return pl.pallas_call(body, out_shape=(jax.ShapeDtypeStruct(dx.shape, BF16), vec, vec), grid=(bsz, s // ts),
                          in_specs=[row, row, per_ex], out_specs=(row, per_ex, per_ex), name=name,
                          compiler_params=_params("parallel", "arbitrary"))(dx, y, gate)


def _loss_head(x, target, gamma, *, name):
    bsz, s, d = x.shape
    ts = _row_tile(s, d * 4, 1 << 20)

    def body(x_ref, t_ref, g_ref, dx_ref, loss_ref, dg_ref):
        xf = x_ref[...]
        r = lax.rsqrt(jnp.mean(xf * xf, axis=-1, keepdims=True) + NORM_EPS)
        xhat = xf * r
        err = xhat * g_ref[...] - t_ref[...]
        dy = err * (1.0 / d)
        dxhat = dy * g_ref[...]
        dx_ref[...] = r * (dxhat - xhat * jnp.mean(dxhat * xhat, axis=-1, keepdims=True))

        @pl.when((pl.program_id(0) == 0) & (pl.program_id(1) == 0))
        def _():
            loss_ref[...] = jnp.zeros_like(loss_ref)
            dg_ref[...] = jnp.zeros_like(dg_ref)

        loss_ref[...] += jnp.sum(err * err, axis=0, keepdims=True)
        dg_ref[...] += jnp.sum(dy * xhat, axis=0, keepdims=True)

    row = pl.BlockSpec((None, ts, d), lambda b, i: (b, i, 0))
    one = pl.BlockSpec((1, d), lambda b, i: (0, 0))
    vec = jax.ShapeDtypeStruct((1, d), F32)
    return pl.pallas_call(body, out_shape=(jax.ShapeDtypeStruct(x.shape, F32), vec, vec), grid=(bsz, s // ts),
                          in_specs=[row, row, one], out_specs=(row, one, one), name=name,
                          compiler_params=_params("arbitrary", "arbitrary"))(x, target, gamma)


def _alibi_slope(h, n_heads):
    return 2.0 ** (-8.0 * (h + 1) / n_heads)


def _attn_masks(first_block):
    qi = lax.broadcasted_iota(jnp.int32, (WINDOW, 2 * WINDOW), 0)
    ki = lax.broadcasted_iota(jnp.int32, (WINDOW, 2 * WINDOW), 1)
    dist = qi + WINDOW - ki
    first_key = jnp.where(first_block, WINDOW, 0)
    valid = (dist >= 0) & (dist < WINDOW) & (ki >= first_key)
    return dist.astype(F32), valid


def _dup_halves(span, kv, left):
    f = span.astype(F32)
    rolled = pltpu.roll(f, HEAD_DIM, axis=1)
    out = jnp.where(left, f, rolled) if kv == 0 else jnp.where(left, rolled, f)
    return out.astype(BF16)


def _attn_probs(s, h, n_heads, distf, valid, sink):
    s = s * (HEAD_DIM ** -0.5) - _alibi_slope(h, n_heads) * distf
    s = jnp.where(valid, s, -1e30)
    m = jnp.maximum(jnp.max(s, axis=-1, keepdims=True), sink)
    e = jnp.exp(s - m)
    e_sink = jnp.exp(sink - m)
    inv = 1.0 / (jnp.sum(e, axis=-1, keepdims=True) + e_sink)
    return e * inv, e_sink * inv


def _attn_specs(d, n_blocks, clamp):
    kcol = d // LANES
    cur = (lambda i: jnp.minimum(i, n_blocks - 1)) if clamp else (lambda i: i)
    prev = lambda i: jnp.maximum(cur(i) - 1, 0)
    kv = lambda col, blk: pl.BlockSpec((None, WINDOW, LANES), lambda b, i: (b, blk(i), col))
    return [pl.BlockSpec((None, WINDOW, d), lambda b, i: (b, cur(i), 0)),
            kv(kcol, prev), kv(kcol, cur), kv(kcol + 1, prev), kv(kcol + 1, cur)]


def _attn_fwd(qkv, sinks, *, name):
    bsz, s, qkv_dim = qkv.shape
    d = qkv_dim - 2 * N_KV_HEADS * HEAD_DIM
    n_heads = d // HEAD_DIM
    group = n_heads // N_KV_HEADS
    pairs = group // 2
    n_blocks = s // WINDOW

    def body(q_ref, kp_ref, kc_ref, vp_ref, vc_ref, sink_ref, o_ref):
        left = lax.broadcasted_iota(jnp.int32, (1, LANES), 1) < HEAD_DIM
        distf, valid = _attn_masks(pl.program_id(1) == 0)
        kspan = jnp.concatenate([kp_ref[...], kc_ref[...]], axis=0)
        vspan = jnp.concatenate([vp_ref[...], vc_ref[...]], axis=0)
        for kv in range(N_KV_HEADS):
            kdup, vdup = _dup_halves(kspan, kv, left), _dup_halves(vspan, kv, left)
            res = []
            for par in range(2):
                keep = left if par == 0 else jnp.logical_not(left)
                cols = [pl.ds((kv * pairs + p) * LANES, LANES) for p in range(pairs)]
                lhs = jnp.concatenate([jnp.where(keep, q_ref[:, cl], jnp.zeros((), BF16)) for cl in cols], axis=0)
                sc = lax.dot_general(lhs, kdup, (((1,), (1,)), ((), ())), preferred_element_type=F32)
                probs = []
                for p in range(pairs):
                    h = kv * group + 2 * p + par
                    pr, _ = _attn_probs(sc[p * WINDOW:(p + 1) * WINDOW], h, n_heads, distf, valid, sink_ref[h])
                    probs.append(pr.astype(BF16))
                res.append(jnp.dot(jnp.concatenate(probs, axis=0), vdup, preferred_element_type=F32))
            for p in range(pairs):
                rows = slice(p * WINDOW, (p + 1) * WINDOW)
                o_ref[:, pl.ds((kv * pairs + p) * LANES, LANES)] = jnp.where(left, res[0][rows], res[1][rows]).astype(BF16)

    return pl.pallas_call(body, out_shape=jax.ShapeDtypeStruct((bsz, s, d), BF16), grid=(bsz, n_blocks),
                          in_specs=_attn_specs(d, n_blocks, False) + [SMEM_SPEC],
                          out_specs=pl.BlockSpec((None, WINDOW, d), lambda b, i: (b, i, 0)), name=name,
                          compiler_params=_params("parallel", "parallel"))(qkv, qkv, qkv, qkv, qkv, sinks)


def _attn_bwd(qkv, do, sinks, *, name):
    bsz, s, qkv_dim = qkv.shape
    d = qkv_dim - 2 * N_KV_HEADS * HEAD_DIM
    n_heads = d // HEAD_DIM
    group = n_heads // N_KV_HEADS
    pairs = group // 2
    n_blocks = s // WINDOW
    tn_dims = (((0,), (0,)), ((), ()))

    def body(q_ref, kp_ref, kc_ref, vp_ref, vc_ref, do_ref, sink_ref, dqkv_ref, colsum_ref, dsink_ref,
             dq_prev, dk_carry, dv_carry):
        b, i = pl.program_id(0), pl.program_id(1)
        left = lax.broadcasted_iota(jnp.int32, (1, LANES), 1) < HEAD_DIM

        @pl.when((b == 0) & (i == 0))
        def _():
            colsum_ref[...] = jnp.zeros_like(colsum_ref)
            dsink_ref[...] = jnp.zeros_like(dsink_ref)

        @pl.when(i == 0)
        def _():
            dqkv_ref[...] = jnp.zeros_like(dqkv_ref)
            dk_carry[...] = jnp.zeros_like(dk_carry)
            dv_carry[...] = jnp.zeros_like(dv_carry)

        @pl.when(i > 0)
        def _():
            dq_v = dq_prev[...]
            dqkv_ref[:, pl.ds(0, d)] = dq_v.astype(BF16)
            colsum_ref[:, pl.ds(0, d)] += jnp.sum(dq_v, axis=0, keepdims=True)

        @pl.when(i < n_blocks)
        def _():
            distf, valid = _attn_masks(i == 0)
            kspan = jnp.concatenate([kp_ref[...], kc_ref[...]], axis=0)
            vspan = jnp.concatenate([vp_ref[...], vc_ref[...]], axis=0)
            dk_blk, dv_blk = [], []
            for kv in range(N_KV_HEADS):
                kdup, vdup = _dup_halves(kspan, kv, left), _dup_halves(vspan, kv, left)
                dq_res, dk_sum, dv_sum = [], None, None
                for par in range(2):
                    keep = left if par == 0 else jnp.logical_not(left)
                    cols = [pl.ds((kv * pairs + p) * LANES, LANES) for p in range(pairs)]
                    zero = jnp.zeros((), BF16)
                    lhs = jnp.concatenate([jnp.where(keep, q_ref[:, cl], zero) for cl in cols], axis=0)
                    dol = jnp.concatenate([jnp.where(keep, do_ref[:, cl], zero) for cl in cols], axis=0)
                    sc = lax.dot_general(lhs, kdup, (((1,), (1,)), ((), ())), preferred_element_type=F32)
                    dp = lax.dot_general(dol, vdup, (((1,), (1,)), ((), ())), preferred_element_type=F32)
                    probs, dscores = [], []
                    for p in range(pairs):
                        h = kv * group + 2 * p + par
                        rows = slice(p * WINDOW, (p + 1) * WINDOW)
                        pr, p_sink = _attn_probs(sc[rows], h, n_heads, distf, valid, sink_ref[h])
                        delta = jnp.sum(pr * dp[rows], axis=-1, keepdims=True)
                        dscores.append((pr * (dp[rows] - delta) * (HEAD_DIM ** -0.5)).astype(BF16))
                        probs.append(pr.astype(BF16))
                        dsink_ref[pl.ds(h, 1), :] += jnp.zeros((1, LANES), F32) - jnp.sum(p_sink * delta)
                    ds_all = jnp.concatenate(dscores, axis=0)
                    p_all = jnp.concatenate(probs, axis=0)
                    dq_res.append(jnp.dot(ds_all, kdup, preferred_element_type=F32))
                    dk_par = lax.dot_general(ds_all, lhs, tn_dims, preferred_element_type=F32)
                    dv_par = lax.dot_general(p_all, dol, tn_dims, preferred_element_type=F32)
                    dk_sum = dk_par if dk_sum is None else dk_sum + dk_par
                    dv_sum = dv_par if dv_sum is None else dv_sum + dv_par
                for p in range(pairs):
                    rows = slice(p * WINDOW, (p + 1) * WINDOW)
                    dq_prev[:, pl.ds((kv * pairs + p) * LANES, LANES)] = jnp.where(left, dq_res[0][rows], dq_res[1][rows])
                dk_blk.append(dk_sum + pltpu.roll(dk_sum, HEAD_DIM, axis=1))
                dv_blk.append(dv_sum + pltpu.roll(dv_sum, HEAD_DIM, axis=1))
            dk_span = jnp.where(left, dk_blk[0], dk_blk[1])
            dv_span = jnp.where(left, dv_blk[0], dv_blk[1])
            dk_done = dk_carry[...] + dk_span[:WINDOW]
            dv_done = dv_carry[...] + dv_span[:WINDOW]
            dk_carry[...] = dk_span[WINDOW:]
            dv_carry[...] = dv_span[WINDOW:]

            @pl.when(i > 0)
            def _():
                dqkv_ref[:, pl.ds(d, LANES)] = dk_done.astype(BF16)
                dqkv_ref[:, pl.ds(d + LANES, LANES)] = dv_done.astype(BF16)
                colsum_ref[:, pl.ds(d, LANES)] += jnp.sum(dk_done, axis=0, keepdims=True)
                colsum_ref[:, pl.ds(d + LANES, LANES)] += jnp.sum(dv_done, axis=0, keepdims=True)

        @pl.when(i == n_blocks)
        def _():
            dk_done, dv_done = dk_carry[...], dv_carry[...]
            dqkv_ref[:, pl.ds(d, LANES)] = dk_done.astype(BF16)
            dqkv_ref[:, pl.ds(d + LANES, LANES)] = dv_done.astype(BF16)
            colsum_ref[:, pl.ds(d, LANES)] += jnp.sum(dk_done, axis=0, keepdims=True)
            colsum_ref[:, pl.ds(d + LANES, LANES)] += jnp.sum(dv_done, axis=0, keepdims=True)

    do_spec = pl.BlockSpec((None, WINDOW, d), lambda b, i: (b, jnp.minimum(i, n_blocks - 1), 0))
    out_shape = (jax.ShapeDtypeStruct((bsz, s, qkv_dim), BF16), jax.ShapeDtypeStruct((1, qkv_dim), F32),
                 jax.ShapeDtypeStruct((n_heads, LANES), F32))
    out_specs = (pl.BlockSpec((None, WINDOW, qkv_dim), lambda b, i: (b, jnp.maximum(i - 1, 0), 0)),
                 pl.BlockSpec((1, qkv_dim), lambda b, i: (0, 0)),
                 pl.BlockSpec((n_heads, LANES), lambda b, i: (0, 0)))
    return pl.pallas_call(body, out_shape=out_shape, grid=(bsz, n_blocks + 1),
                          in_specs=_attn_specs(d, n_blocks, True) + [do_spec, SMEM_SPEC], out_specs=out_specs,
                          scratch_shapes=[pltpu.VMEM((WINDOW, d), F32), pltpu.VMEM((WINDOW, LANES), F32),
                                          pltpu.VMEM((WINDOW, LANES), F32)],
                          name=name, compiler_params=_params("arbitrary", "arbitrary"))(qkv, qkv, qkv, qkv, qkv, do, sinks)


def _conv_tile(s):
    return min(256, s)


def _halo_specs(ts, width, s):
    per = ts // CONV_HALO
    prev = pl.BlockSpec((None, CONV_HALO, width), lambda b, i: (b, jnp.maximum(i * per - 1, 0), 0))
    nxt = pl.BlockSpec((None, CONV_HALO, width), lambda b, i: (b, jnp.minimum((i + 1) * per, s // CONV_HALO - 1), 0))
    cur = pl.BlockSpec((None, ts, width), lambda b, i: (b, i, 0))
    return prev, cur, nxt


def _glu(u, d):
    return u[:, :d] * jax.nn.sigmoid(u[:, d:])


def _taps(w_ref, buf, out_ref, ts, d, offset):
    for r0 in range(0, ts, CONV_ROWS):
        for l0 in range(0, d, LANES):
            lanes = pl.ds(l0, LANES)
            acc = jnp.zeros((CONV_ROWS, LANES), F32)
            for j in range(CONV_WIDTH):
                acc = acc + w_ref[pl.ds(j, 1), lanes] * buf[pl.ds(r0 + offset(j), CONV_ROWS), lanes]
            out_ref[pl.ds(r0, CONV_ROWS), lanes] = acc


def _conv_fwd(u, w_dw, b_dw, ln_g, ln_b, *, name):
    bsz, s, d2 = u.shape
    d = d2 // 2
    ts = _conv_tile(s)

    def body(up_ref, uc_ref, w_ref, bdw_ref, g_ref, b_ref, z_ref, v_ref, gbuf):
        halo = _glu(up_ref[...], d)
        gbuf[pl.ds(0, CONV_HALO), :] = jnp.where(pl.program_id(1) > 0, halo, 0.0)
        gbuf[pl.ds(CONV_HALO, ts), :] = _glu(uc_ref[...], d)
        _taps(w_ref, gbuf, v_ref, ts, d, lambda j: CONV_HALO - (CONV_WIDTH - 1) + j)
        v = v_ref[...] + bdw_ref[...]
        v_ref[...] = v
        mu = jnp.mean(v, axis=-1, keepdims=True)
        cen = v - mu
        rstd = lax.rsqrt(jnp.mean(cen * cen, axis=-1, keepdims=True) + NORM_EPS)
        ln = cen * rstd * g_ref[...] + b_ref[...]
        z_ref[...] = (ln * jax.nn.sigmoid(ln)).astype(BF16)

    prev, cur, _ = _halo_specs(ts, d2, s)
    one = pl.BlockSpec((1, d), lambda b, i: (0, 0))
    row = pl.BlockSpec((None, ts, d), lambda b, i: (b, i, 0))
    return pl.pallas_call(body, out_shape=(jax.ShapeDtypeStruct((bsz, s, d), BF16), jax.ShapeDtypeStruct((bsz, s, d), F32)),
                          grid=(bsz, s // ts),
                          in_specs=[prev, cur, pl.BlockSpec((CONV_HALO, d), lambda b, i: (0, 0)), one, one, one],
                          out_specs=(row, row), scratch_shapes=[pltpu.VMEM((ts + CONV_HALO, d), F32)], name=name,
                          compiler_params=_params("parallel", "parallel"))(u, u, w_dw, b_dw, ln_g, ln_b)


def _conv_bwd_ln(dz, v, ln_g, ln_b, *, name):
    bsz, s, d = v.shape
    ts = _row_tile(s, d * 4, 1 << 20)

    def body(dz_ref, v_ref, g_ref, b_ref, dv_ref, dg_ref, db_ref, dbdw_ref):
        v_v = v_ref[...]
        mu = jnp.mean(v_v, axis=-1, keepdims=True)
        cen = v_v - mu
        rstd = lax.rsqrt(jnp.mean(cen * cen, axis=-1, keepdims=True) + NORM_EPS)
        vhat = cen * rstd
        ln = vhat * g_ref[...] + b_ref[...]
        sig = jax.nn.sigmoid(ln)
        dln = dz_ref[...] * (sig * (1.0 + ln * (1.0 - sig)))
        dvhat = dln * g_ref[...]
        dv = rstd * (dvhat - jnp.mean(dvhat, axis=-1, keepdims=True)
                     - vhat * jnp.mean(dvhat * vhat, axis=-1, keepdims=True))
        dv_ref[...] = dv

        @pl.when((pl.program_id(0) == 0) & (pl.program_id(1) == 0))
        def _():
            dg_ref[...] = jnp.zeros_like(dg_ref)
            db_ref[...] = jnp.zeros_like(db_ref)
            dbdw_ref[...] = jnp.zeros_like(dbdw_ref)

        dg_ref[...] += jnp.sum(dln * vhat, axis=0, keepdims=True)
        db_ref[...] += jnp.sum(dln, axis=0, keepdims=True)
        dbdw_ref[...] += jnp.sum(dv, axis=0, keepdims=True)

    row = pl.BlockSpec((None, ts, d), lambda b, i: (b, i, 0))
    one = pl.BlockSpec((1, d), lambda b, i: (0, 0))
    vec = jax.ShapeDtypeStruct((1, d), F32)
    return pl.pallas_call(body, out_shape=(jax.ShapeDtypeStruct(v.shape, F32), vec, vec, vec), grid=(bsz, s // ts),
                          in_specs=[row, row, one, one], out_specs=(row, one, one, one), name=name,
                          compiler_params=_params("arbitrary", "arbitrary"))(dz, v, ln_g, ln_b)


def _conv_bwd_taps(dv, u, w_dw, *, name):
    bsz, s, d = dv.shape
    ts = _conv_tile(s)
    n_tiles = s // ts

    def body(dvc_ref, dvn_ref, up_ref, uc_ref, w_ref, du_ref, dbu_ref, dw_ref, gbuf, dvbuf, dglu):
        i = pl.program_id(1)

        @pl.when((pl.program_id(0) == 0) & (i == 0))
        def _():
            dbu_ref[...] = jnp.zeros_like(dbu_ref)
            dw_ref[...] = jnp.zeros_like(dw_ref)

        gbuf[pl.ds(0, CONV_HALO), :] = jnp.where(i > 0, _glu(up_ref[...], d), 0.0)
        gbuf[pl.ds(CONV_HALO, ts), :] = _glu(uc_ref[...], d)
        dvbuf[pl.ds(0, ts), :] = dvc_ref[...]
        dvbuf[pl.ds(ts, CONV_HALO), :] = jnp.where(i < n_tiles - 1, dvn_ref[...], 0.0)
        _taps(w_ref, dvbuf, dglu, ts, d, lambda j: CONV_WIDTH - 1 - j)
        for l0 in range(0, d, LANES):
            lanes = pl.ds(l0, LANES)
            for j in range(CONV_WIDTH):
                acc = jnp.zeros((SUBLANES, LANES), F32)
                for r0 in range(0, ts, CONV_ROWS):
                    prod = dvbuf[pl.ds(r0, CONV_ROWS), lanes] * gbuf[pl.ds(r0 + CONV_HALO - (CONV_WIDTH - 1) + j, CONV_ROWS), lanes]
                    for k in range(0, CONV_ROWS, SUBLANES):
                        acc = acc + prod[k:k + SUBLANES]
                dw_ref[j, :, lanes] += acc
        u_v = uc_ref[...]
        a, sig = u_v[:, :d], jax.nn.sigmoid(u_v[:, d:])
        dg_v = dglu[...]
        da = dg_v * sig
        dgate = dg_v * a * sig * (1.0 - sig)
        du_ref[:, pl.ds(0, d)] = da.astype(BF16)
        du_ref[:, pl.ds(d, d)] = dgate.astype(BF16)
        dbu_ref[:, pl.ds(0, d)] += jnp.sum(da, axis=0, keepdims=True)
        dbu_ref[:, pl.ds(d, d)] += jnp.sum(dgate, axis=0, keepdims=True)

    _, dv_cur, dv_next = _halo_specs(ts, d, s)
    u_prev, u_cur, _ = _halo_specs(ts, 2 * d, s)
    out_shape = (jax.ShapeDtypeStruct((bsz, s, 2 * d), BF16), jax.ShapeDtypeStruct((1, 2 * d), F32),
                 jax.ShapeDtypeStruct((CONV_HALO, SUBLANES, d), F32))
    out_specs = (pl.BlockSpec((None, ts, 2 * d), lambda b, i: (b, i, 0)), pl.BlockSpec((1, 2 * d), lambda b, i: (0, 0)),
                 pl.BlockSpec((CONV_HALO, SUBLANES, d), lambda b, i: (0, 0, 0)))
    return pl.pallas_call(body, out_shape=out_shape, grid=(bsz, n_tiles),
                          in_specs=[dv_cur, dv_next, u_prev, u_cur, pl.BlockSpec((CONV_HALO, d), lambda b, i: (0, 0))],
                          out_specs=out_specs,
                          scratch_shapes=[pltpu.VMEM((ts + CONV_HALO, d), F32), pltpu.VMEM((ts + CONV_HALO, d), F32),
                                          pltpu.VMEM((ts, d), F32)],
                          name=name, compiler_params=_params("arbitrary", "arbitrary"))(dv, dv, u, u, w_dw)


def _mod_fwd(c_all, w_mod, b_mod, *, name):
    n_layers, d, n = w_mod.shape
    rows = c_all.shape[0]

    def body(c_ref, w_ref, b_ref, o_ref):
        cv = c_ref[...]
        cs = (cv * jax.nn.sigmoid(cv)).astype(BF16)
        o_ref[...] = jnp.dot(cs, w_ref[...].astype(BF16), preferred_element_type=F32) + b_ref[...]

    return pl.pallas_call(body, out_shape=jax.ShapeDtypeStruct((n_layers, rows, n), F32), grid=(n_layers,),
                          in_specs=[pl.BlockSpec((rows, d), lambda l: (0, 0)), pl.BlockSpec((None, d, n), lambda l: (l, 0, 0)),
                                    pl.BlockSpec((None, 1, n), lambda l: (l, 0, 0))],
                          out_specs=pl.BlockSpec((None, rows, n), lambda l: (l, 0, 0)), name=name,
                          compiler_params=_params("parallel"))(c_all, w_mod, b_mod)


def _mod_bwd(c_all, dmod, *, name):
    n_layers, rows, n = dmod.shape
    d = c_all.shape[1]

    def body(c_ref, g_ref, o_ref):
        cv = c_ref[...]
        cs = (cv * jax.nn.sigmoid(cv)).astype(BF16)
        o_ref[...] = lax.dot_general(cs, g_ref[...].astype(BF16), (((0,), (0,)), ((), ())), preferred_element_type=F32)

    return pl.pallas_call(body, out_shape=jax.ShapeDtypeStruct((n_layers, d, n), F32), grid=(n_layers,),
                          in_specs=[pl.BlockSpec((rows, d), lambda l: (0, 0)), pl.BlockSpec((None, rows, n), lambda l: (l, 0, 0))],
                          out_specs=pl.BlockSpec((None, d, n), lambda l: (l, 0, 0)), name=name,
                          compiler_params=_params("parallel"))(c_all, dmod)


def _add_half(g, recv, my_c, *, name):
    p, _, r, cdim = g.shape
    tr = _row_tile(r, cdim * 4)

    def body(c_ref, g_ref, r_ref, o_ref):
        o_ref[...] = g_ref[...] + r_ref[...]

    grid_spec = pltpu.PrefetchScalarGridSpec(
        num_scalar_prefetch=1, grid=(p, r // tr),
        in_specs=[pl.BlockSpec((None, None, tr, cdim), lambda q, i, c_ref: (q, c_ref[0], i, 0)),
                  pl.BlockSpec((None, None, tr, cdim), lambda q, i, c_ref: (q, 0, i, 0))],
        out_specs=pl.BlockSpec((None, tr, cdim), lambda q, i, c_ref: (q, i, 0)))
    return pl.pallas_call(body, out_shape=jax.ShapeDtypeStruct((p, r, cdim), F32), grid_spec=grid_spec, name=name,
                          compiler_params=_params("parallel", "parallel"))(my_c, g, recv)


def _add_pieces(chip, recv, my_q, *, name):
    _, r, cdim = chip.shape
    tr = _row_tile(r, cdim * 4)

    def body(q_ref, own_ref, r0_ref, r1_ref, r2_ref, o_ref):
        o_ref[...] = ((own_ref[...] + r0_ref[...]) + r1_ref[...]) + r2_ref[...]

    piece = lambda k: pl.BlockSpec((None, tr, cdim), lambda i, q_ref: (k, i, 0))
    grid_spec = pltpu.PrefetchScalarGridSpec(
        num_scalar_prefetch=1, grid=(r // tr,),
        in_specs=[pl.BlockSpec((None, tr, cdim), lambda i, q_ref: (q_ref[0], i, 0)), piece(0), piece(1), piece(2)],
        out_specs=pl.BlockSpec((tr, cdim), lambda i, q_ref: (i, 0)))
    return pl.pallas_call(body, out_shape=jax.ShapeDtypeStruct((r, cdim), F32), grid_spec=grid_spec, name=name,
                          compiler_params=_params("parallel"))(my_q, chip, recv, recv, recv)


def _sum_devices(parts, dmod, *, name):
    def body(p_ref, m_ref, o_ref, b_ref):
        acc = p_ref[0]
        for k in range(1, N_DEV):
            acc = acc + p_ref[k]
        o_ref[...] = acc
        tot = None
        for k in range(N_DEV):
            for e in range(dmod.shape[2]):
                tot = m_ref[k, :, e] if tot is None else tot + m_ref[k, :, e]
        b_ref[...] = tot

    out_shape = (jax.ShapeDtypeStruct(parts.shape[1:], F32),
                 jax.ShapeDtypeStruct((dmod.shape[1],) + dmod.shape[3:], F32))
    return pl.pallas_call(body, out_shape=out_shape, in_specs=[VMEM_SPEC, VMEM_SPEC], out_specs=(VMEM_SPEC, VMEM_SPEC),
                          name=name, compiler_params=_params())(parts, dmod)


def _adamw(w, g, m, v, *, name):
    r, cdim = w.shape
    tr = _row_tile(r, cdim * 4, 1 << 19)

    def body(w_ref, g_ref, m_ref, v_ref, d_ref, nm_ref, nv_ref):
        gv = g_ref[...]
        nm = ADAM_B1 * m_ref[...] + (1.0 - ADAM_B1) * gv
        nv = ADAM_B2 * v_ref[...] + (1.0 - ADAM_B2) * (gv * gv)
        m_hat = nm / (1.0 - ADAM_B1 ** ADAM_STEP)
        v_hat = nv / (1.0 - ADAM_B2 ** ADAM_STEP)
        d_ref[...] = -ADAM_LR * (m_hat / (jnp.sqrt(v_hat) + ADAM_EPS) + ADAM_WD * w_ref[...])
        nm_ref[...] = nm
        nv_ref[...] = nv

    row = pl.BlockSpec((tr, cdim), lambda i: (i, 0))
    shape = jax.ShapeDtypeStruct((r, cdim), F32)
    return pl.pallas_call(body, out_shape=(shape, shape, shape), grid=(r // tr,), in_specs=[row] * 4,
                          out_specs=(row, row, row), name=name, compiler_params=_params("parallel"))(w, g, m, v)


def _position():
    x, y, c = lax.axis_index("x"), lax.axis_index("y"), lax.axis_index("c")
    return x, y, c, 2 * x + y


def _peer(kind, x, y, c):
    return {"c": (x, y, 1 - c), "x": (1 - x, y, c), "y": (x, 1 - y, c), "xy": (1 - x, 1 - y, c)}[kind]


def _peer_shard(kind, x, y):
    px, py, _ = _peer(kind, x, y, 0)
    return 2 * px + py


CHIP_KINDS = ("x", "y", "xy")


def _all_gather_small(block, *, name):
    m_per, n = block.shape

    def body(x_ref, out_ref, send_sems, recv_sems, local_sem):
        x, y, c, _ = _position()
        me, sibling = (x, y, c), (x, y, 1 - c)
        chips = [_peer(k, x, y, c)[:2] for k in CHIP_KINDS]

        def rows(px, py, pc):
            return out_ref.at[pl.ds((4 * px + 2 * py + pc) * m_per, m_per), :]

        def copy(k, blk, to, src=None):
            return pltpu.make_async_remote_copy(src_ref=rows(*blk) if src is None else src, dst_ref=rows(*blk),
                                                send_sem=send_sems.at[k], recv_sem=recv_sems.at[k],
                                                device_id=to, device_id_type=MESH)

        mine = pltpu.make_async_copy(x_ref, rows(*me), local_sem)
        mine.start()
        first = [copy(0, me, sibling, src=x_ref)]
        first += [copy(1 + j, me, (*chip, c), src=x_ref) for j, chip in enumerate(chips)]
        for cp in first:
            cp.start()
        passed = [copy(4 + j, (*chip, c), sibling) for j, chip in enumerate(chips)]
        for j, chip in enumerate(chips):
            copy(1 + j, (*chip, c), me).wait_recv()
            passed[j].start()
        copy(0, sibling, me).wait_recv()
        for j, chip in enumerate(chips):
            copy(4 + j, (*chip, 1 - c), me).wait_recv()
        for cp in first + passed:
            cp.wait_send()
        mine.wait()

    return pl.pallas_call(body, out_shape=jax.ShapeDtypeStruct((N_DEV * m_per, n), block.dtype),
                          in_specs=[VMEM_SPEC], out_specs=VMEM_SPEC,
                          scratch_shapes=[pltpu.SemaphoreType.DMA((7,)), pltpu.SemaphoreType.DMA((7,)),
                                          pltpu.SemaphoreType.DMA],
                          name=name, compiler_params=_params())(block)


def _comm(operands, out_shapes, copies, local_copies, *, name):
    n_in, n_out = len(operands), len(out_shapes)

    def body(*refs):
        ins, outs = refs[:n_in], refs[n_in:n_in + n_out]
        send_sems, recv_sems, local_sems = refs[n_in + n_out:]
        pos = _position()
        x, y, c, _ = pos
        sends, lands = [], []
        for k, (src, dst, landing, kind, _) in enumerate(copies):
            common = dict(send_sem=send_sems.at[k], recv_sem=recv_sems.at[k], device_id=_peer(kind, x, y, c),
                          device_id_type=MESH)
            sends.append(pltpu.make_async_remote_copy(src_ref=src(ins, outs, pos), dst_ref=dst(ins, outs, pos), **common))
            lands.append(pltpu.make_async_remote_copy(src_ref=src(ins, outs, pos), dst_ref=landing(ins, outs, pos), **common))
        locals_ = [pltpu.make_async_copy(src(ins, outs, pos), dst(ins, outs, pos), local_sems.at[k])
                   for k, (src, dst) in enumerate(local_copies)]
        for cp in locals_:
            cp.start()
        for k, cp in enumerate(copies):
            if cp[4] is None:
                sends[k].start()
        arrived = set()
        for k, cp in enumerate(copies):
            if cp[4] is not None:
                if cp[4] not in arrived:
                    lands[cp[4]].wait_recv()
                    arrived.add(cp[4])
                sends[k].start()
        for k in range(len(copies)):
            if k not in arrived:
                lands[k].wait_recv()
        for cp in sends:
            cp.wait_send()
        for cp in locals_:
            cp.wait()

    return pl.pallas_call(body, out_shape=out_shapes, in_specs=[HBM_SPEC] * n_in, out_specs=[HBM_SPEC] * n_out,
                          scratch_shapes=[pltpu.SemaphoreType.DMA((len(copies),)), pltpu.SemaphoreType.DMA((len(copies),)),
                                          pltpu.SemaphoreType.DMA((max(len(local_copies), 1),))],
                          name=name, compiler_params=_params())(*operands)


def _gather_weights(shards, axes, *, name):
    out_shapes, copies, local_copies = [], [], []
    for a, (shard, axis) in enumerate(zip(shards, axes)):
        shape = list(shard.shape)
        half = shape[2] // 2
        shape[axis] = N_CHIPS
        out_shapes.append(jax.ShapeDtypeStruct(tuple(shape), shard.dtype))

        def sl(ref, q, h, axis=axis, half=half, whole=False):
            idx = [slice(None)] * 4
            if q is not None:
                idx[axis] = pl.ds(q, 1)
            if not whole:
                idx[2] = pl.ds(h * half, half)
            return ref.at[tuple(idx)]

        local_copies.append((lambda ins, outs, pos, a=a: ins[a],
                             lambda ins, outs, pos, a=a, sl=sl: sl(outs[a], pos[3], 0, whole=True)))
        first = len(copies)
        for kind in CHIP_KINDS:
            copies.append((lambda ins, outs, pos, a=a, sl=sl: sl(ins[a], None, pos[2]),
                           lambda ins, outs, pos, a=a, sl=sl: sl(outs[a], pos[3], pos[2]),
                           lambda ins, outs, pos, a=a, sl=sl, kind=kind: sl(outs[a], _peer_shard(kind, pos[0], pos[1]), pos[2]),
                           kind, None))
        for j, kind in enumerate(CHIP_KINDS):
            copies.append((lambda ins, outs, pos, a=a, sl=sl, kind=kind: sl(outs[a], _peer_shard(kind, pos[0], pos[1]), pos[2]),
                           lambda ins, outs, pos, a=a, sl=sl, kind=kind: sl(outs[a], _peer_shard(kind, pos[0], pos[1]), pos[2]),
                           lambda ins, outs, pos, a=a, sl=sl, kind=kind: sl(outs[a], _peer_shard(kind, pos[0], pos[1]), 1 - pos[2]),
                           "c", first + j))
    return _comm(shards, out_shapes, copies, local_copies, name=name)


def _reduce_scatter(grads, stack, my_c, my_q, *, name):
    n = len(grads)
    views = [g.reshape(N_CHIPS, 2, g.shape[1] // 2, g.shape[2]) for g in grads]
    recv = _comm(views, [jax.ShapeDtypeStruct((N_CHIPS, 1) + v.shape[2:], F32) for v in views],
                 [(lambda ins, outs, pos, a=a: ins[a].at[:, pl.ds(1 - pos[2], 1)],
                   lambda ins, outs, pos, a=a: outs[a], lambda ins, outs, pos, a=a: outs[a], "c", None) for a in range(n)],
                 [], name=name + "_swap")
    chip = [_add_half(v, r, my_c, name=name + "_chipsum") for v, r in zip(views, recv)]
    copies = []
    for a in range(n):
        for j, kind in enumerate(CHIP_KINDS):
            copies.append((lambda ins, outs, pos, a=a, kind=kind: ins[a].at[pl.ds(_peer_shard(kind, pos[0], pos[1]), 1)],
                           lambda ins, outs, pos, a=a, j=j: outs[a].at[pl.ds(j, 1)],
                           lambda ins, outs, pos, a=a, j=j: outs[a].at[pl.ds(j, 1)], kind, None))
    recv = _comm(chip, [jax.ShapeDtypeStruct((3,) + ch.shape[1:], F32) for ch in chip], copies, [], name=name + "_scatter")
    fin = [_add_pieces(ch, r, my_q, name=name + "_total") for ch, r in zip(chip, recv)]
    fin4 = [f.reshape((1, 1) + f.shape) for f in fin]
    out_shapes, copies, local_copies = [], [], []
    for gi, group in enumerate(stack):
        out_shapes.append(jax.ShapeDtypeStruct((len(group), 2) + fin[group[0]].shape, F32))
        for li, a in enumerate(group):
            here = lambda ins, outs, pos, gi=gi, li=li: outs[gi].at[pl.ds(li, 1), pl.ds(pos[2], 1)]
            there = lambda ins, outs, pos, gi=gi, li=li: outs[gi].at[pl.ds(li, 1), pl.ds(1 - pos[2], 1)]
            src = lambda ins, outs, pos, a=a: ins[a]
            local_copies.append((src, here))
            copies.append((src, here, there, "c", None))
    full = _comm(fin4, out_shapes, copies, local_copies, name=name + "_share")
    return [f.reshape(f.shape[0], 2 * f.shape[2], f.shape[3]) for f in full]


def _pack(arrays, width):
    flat = jnp.concatenate([a.reshape(-1).astype(F32) for a in arrays])
    rows = -(-flat.shape[0] // width)
    rows = -(-rows // SUBLANES) * SUBLANES
    return jnp.pad(flat, (0, rows * width - flat.shape[0])).reshape(rows, width)


def _unpack(packed, shapes):
    flat, out, off = packed.reshape(-1), [], 0
    for shp in shapes:
        size = 1
        for dim in shp:
            size *= dim
        out.append(flat[off:off + size].reshape(shp))
        off += size
    return out


def _adamw_packed(ws, gs, ms, vs, width, *, name):
    shapes = [w.shape for w in ws]
    res = _adamw(_pack(ws, width), _pack(gs, width), _pack(ms, width), _pack(vs, width), name=name)
    return [_unpack(r, shapes) for r in res]


def kernel(x, c, w_mod, b_mod, norm_mix, norm_mlp, w_qkv, b_qkv, w_o, b_o, sinks, w_pw1, b_pw1, w_dw, b_dw, conv_ln_g, conv_ln_b, w_pw2, b_pw2, w_up, w_down, final_norm, loss_target, m_w_mod, m_b_mod, m_norm_mix, m_norm_mlp, m_w_qkv, m_b_qkv, m_w_o, m_b_o, m_sinks, m_w_pw1, m_b_pw1, m_w_dw, m_b_dw, m_conv_ln_g, m_conv_ln_b, m_w_pw2, m_b_pw2, m_w_up, m_w_down, m_final_norm, v_w_mod, v_b_mod, v_norm_mix, v_norm_mlp, v_w_qkv, v_b_qkv, v_w_o, v_b_o, v_sinks, v_w_pw1, v_b_pw1, v_w_dw, v_b_dw, v_conv_ln_g, v_conv_ln_b, v_w_pw2, v_b_pw2, v_w_up, v_w_down, v_final_norm):
    bsz, s, d = x.shape
    t = bsz * s
    depth = w_mod.shape[0]
    n_attn, n_conv = w_qkv.shape[0], w_pw1.shape[0]
    n_heads = d // HEAD_DIM
    qkv_dim = d + 2 * N_KV_HEADS * HEAD_DIM
    mx, my, mc, mq = _position()
    me = 4 * mx + 2 * my + mc
    my_c = jnp.reshape(mc, (1,)).astype(jnp.int32)
    my_q = jnp.reshape(mq, (1,)).astype(jnp.int32)

    small_sharded = [b_pw1, w_dw, b_dw, conv_ln_g, conv_ln_b, b_pw2]
    c_pad = jnp.pad(c, ((0, SUBLANES - bsz), (0, 0)))
    gathered = _all_gather_small(jnp.concatenate([c_pad, _pack(small_sharded, d)], axis=0), name="gather_c")
    gathered = gathered.reshape(N_DEV, -1, d)
    c_all = gathered[:, :bsz].reshape(N_DEV * bsz, d)
    per_chip = [_unpack(gathered[2 * q, SUBLANES:], [a.shape for a in small_sharded]) for q in range(N_CHIPS)]
    b_pw1_f, w_dw_f, b_dw_f, ln_g_f, ln_b_f, b_pw2_f = [jnp.concatenate([per_chip[q][k] for q in range(N_CHIPS)], axis=-1)
                                                           for k in range(len(small_sharded))]
    w_dw_f = jnp.pad(w_dw_f, ((0, 0), (0, CONV_HALO - CONV_WIDTH), (0, 0)))

    n_mod = w_mod.shape[2]
    b_mod_cols = lax.dynamic_slice_in_dim(b_mod, mq * n_mod, n_mod, axis=1).reshape(depth, 1, n_mod)
    mod_part = _mod_fwd(c_all, w_mod, b_mod_cols, name="mod_fwd")
    mod_all = _all_gather_small(mod_part.reshape(depth * N_DEV * bsz, n_mod), name="gather_mod")
    mod_all = mod_all.reshape(N_DEV, depth, N_DEV * bsz, n_mod)[0::2]
    mod = lax.dynamic_slice_in_dim(mod_all, me * bsz, bsz, axis=2)
    mod = jnp.transpose(mod, (1, 2, 0, 3)).reshape(depth, bsz, N_MOD, 1, d)
    mods = [[mod[i][:, k] for k in range(N_MOD)] for i in range(depth)]

    as_col = lambda w: w.astype(BF16)[None]
    as_row = lambda w: w.astype(BF16)[:, None]
    g_qkv, g_o, g_pw1, g_pw2, g_up, g_down = _gather_weights(
        [as_col(w_qkv), as_row(w_o), as_col(w_pw1), as_row(w_pw2), as_col(w_up), as_row(w_down)],
        [0, 1, 0, 1, 0, 1], name="gather_weights")
    cols = lambda g: jnp.transpose(g, (1, 2, 0, 3)).reshape(g.shape[1], g.shape[2], -1)
    rows = lambda g: g.reshape(g.shape[0], -1, g.shape[3])
    wq, wo, wpw1, wpw2, wup, wdown = cols(g_qkv), rows(g_o), cols(g_pw1), rows(g_pw2), cols(g_up), rows(g_down)

    saved = []
    xc = x
    for i in range(depth):
        j = i // 2
        sh1, sc1, g1, sh2, sc2, g2 = mods[i]
        h1 = _normmod(xc, norm_mix[i][None], sc1, sh1, name="normmod")
        if i % 2 == 0:
            qkv = _mm(h1.reshape(t, d), wq[j], bias=b_qkv[j], tn=qkv_dim, name="mm_qkv").reshape(bsz, s, qkv_dim)
            mix = _attn_fwd(qkv, sinks[j], name="attn_fwd")
            y1, x1 = _mm(mix.reshape(t, d), wo[j], bias=b_o[j], epi="resid", resid=xc.reshape(t, d), gate=g1, seq=s,
                         tn=d, name="mm_out")
            extra = (qkv, mix)
        else:
            u = _mm(h1.reshape(t, d), wpw1[j], bias=b_pw1_f[j], out_dtype=F32, tn=d, name="mm_pw1").reshape(bsz, s, 2 * d)
            mix, conv_v = _conv_fwd(u, w_dw_f[j], b_dw_f[j][None], ln_g_f[j][None], ln_b_f[j][None], name="conv_fwd")
            y1, x1 = _mm(mix.reshape(t, d), wpw2[j], bias=b_pw2_f[j], epi="resid", resid=xc.reshape(t, d), gate=g1, seq=s,
                         tn=d, name="mm_out")
            extra = (u, mix, conv_v)
        x1 = x1.reshape(bsz, s, d)
        h2 = _normmod(x1, norm_mlp[i][None], sc2, sh2, name="normmod")
        act = _mm(h2.reshape(t, d), wup[i], epi="relu2", tn=d, name="mm_up")
        y2, x2 = _mm(act, wdown[i], epi="resid", resid=x1.reshape(t, d), gate=g2, seq=s, name="mm_down")
        saved.append((xc, h1, extra, y1, x1, h2, act, y2))
        xc = x2.reshape(bsz, s, d)

    dx, loss_cols, d_final = _loss_head(xc, loss_target, final_norm[None], name="loss_head")
    loss = lax.psum(0.5 / d * jnp.sum(loss_cols), ("x", "y", "c"))

    big = {}
    dmods, small = [None] * depth, {}
    for i in reversed(range(depth)):
        j = i // 2
        xin, h1, extra, y1, x1, h2, act, y2 = saved[i]
        sh1, sc1, g1, sh2, sc2, g2 = mods[i]
        dyb, dg2, _ = _gate_bwd(dx, y2.reshape(bsz, s, d), g2, name="gate_bwd")
        dyb = dyb.reshape(t, d)
        big["down", i] = _mm_tn(act, dyb, name="dw_down").reshape(N_CHIPS, -1, d)
        dup = _mm(dyb, wdown[i], nt=True, epi="dact", act=act, tn=d, name="mm_dact")
        big["up", i] = _mm_tn(h2.reshape(t, d), dup, col_shards=N_CHIPS, name="dw_up")
        dh2 = _mm(dup, wup[i], nt=True, out_dtype=F32, name="mm_dh2").reshape(bsz, s, d)
        dx1, p2, dsh2 = _normmod_bwd(x1, dh2, dx, norm_mlp[i][None], sc2, name="normmod_bwd")
        dyb, dg1, sdx = _gate_bwd(dx1, y1.reshape(bsz, s, d), g1, name="gate_bwd")
        dyb = dyb.reshape(t, d)
        d_bias_out = jnp.sum(g1 * sdx, axis=(0, 1))
        if i % 2 == 0:
            qkv, mix = extra
            small["b_o", j] = d_bias_out
            big["o", j] = _mm_tn(mix.reshape(t, d), dyb, name="dw_sq").reshape(N_CHIPS, -1, d)
            dmix = _mm(dyb, wo[j], nt=True, tn=d, name="mm_dmix").reshape(bsz, s, d)
            dqkv, d_bqkv, d_sink = _attn_bwd(qkv, dmix, sinks[j], name="attn_bwd")
            small["b_qkv", j], small["sinks", j] = d_bqkv[0], d_sink[:, 0]
            dqkv = dqkv.reshape(t, qkv_dim)
            dwq = _mm_tn(h1.reshape(t, d), dqkv, tn=qkv_dim, name="dw_qkv")
            big["qkv", j] = jnp.transpose(dwq.reshape(d, N_CHIPS, -1), (1, 0, 2))
            dh1 = _mm(dqkv, wq[j], nt=True, out_dtype=F32, tn=d, name="mm_dh1a")
        else:
            u, mix, conv_v = extra
            small["b_pw2", j] = d_bias_out
            big["pw2", j] = _mm_tn(mix.reshape(t, d), dyb, name="dw_sq").reshape(N_CHIPS, -1, d)
            dz = _mm(dyb, wpw2[j], nt=True, out_dtype=F32, tn=d, name="mm_dz").reshape(bsz, s, d)
            dv, d_lng, d_lnb, d_bdw = _conv_bwd_ln(dz, conv_v, ln_g_f[j][None], ln_b_f[j][None], name="conv_bwd_ln")
            du, d_bpw1, d_wdw = _conv_bwd_taps(dv, u, w_dw_f[j], name="conv_bwd_taps")
            small["ln_g", j], small["ln_b", j], small["b_dw", j] = d_lng[0], d_lnb[0], d_bdw[0]
            small["b_pw1", j], small["w_dw", j] = d_bpw1[0], jnp.sum(d_wdw[:CONV_WIDTH], axis=1)
            du = du.reshape(t, 2 * d)
            big["pw1", j] = _mm_tn(h1.reshape(t, d), du, col_shards=N_CHIPS, name="dw_pw1")
            dh1 = _mm(du, wpw1[j], nt=True, out_dtype=F32, tn=d, name="mm_dh1c")
        dx, p1, dsh1 = _normmod_bwd(xin, dh1.reshape(bsz, s, d), dx1, norm_mix[i][None], sc1, name="normmod_bwd")
        small["norm_mix", i] = jnp.sum((1.0 + sc1) * p1, axis=(0, 1))
        small["norm_mlp", i] = jnp.sum((1.0 + sc2) * p2, axis=(0, 1))
        dmods[i] = jnp.concatenate([dsh1, norm_mix[i] * p1, dg1, dsh2, norm_mlp[i] * p2, dg2], axis=1)
    grad_x = dx

    small_names = ([("norm_mix", i) for i in range(depth)] + [("norm_mlp", i) for i in range(depth)]
                   + [(nm, j) for nm in ("b_qkv", "b_o", "sinks") for j in range(n_attn)]
                   + [(nm, j) for nm in ("b_pw1", "w_dw", "b_dw", "ln_g", "ln_b", "b_pw2") for j in range(n_conv)])
    small_list = [small[k] for k in small_names] + [d_final[0]]
    small_pack = _pack(small_list, d)
    dmod_rows = jnp.stack(dmods).reshape(depth * bsz * N_MOD, d)
    n_small = small_pack.shape[0]
    gathered = _all_gather_small(jnp.concatenate([small_pack, _pack([dmod_rows], d)], axis=0), name="gather_small")
    gathered = gathered.reshape(N_DEV, -1, d)
    dmod_all = gathered[:, n_small:n_small + depth * bsz * N_MOD].reshape(N_DEV, depth, bsz, N_MOD, d)
    small_sum, g_b_mod = _sum_devices(gathered[:, :n_small], dmod_all, name="sum_devices")
    small_tot = dict(zip(small_names + ["final_norm"], _unpack(small_sum, [a.shape for a in small_list])))
    stacked = lambda nm, count: jnp.stack([small_tot[nm, k] for k in range(count)])
    g_norm_mix, g_norm_mlp = stacked("norm_mix", depth), stacked("norm_mlp", depth)
    g_b_qkv, g_b_o, g_sinks = stacked("b_qkv", n_attn), stacked("b_o", n_attn), stacked("sinks", n_attn)
    g_final = small_tot["final_norm"]
    g_b_mod = g_b_mod.reshape(depth, N_MOD * d)
    shard_cols = lambda g: lax.dynamic_slice_in_dim(g, mq * (g.shape[-1] // N_CHIPS), g.shape[-1] // N_CHIPS, axis=g.ndim - 1)
    g_b_pw1, g_w_dw, g_b_dw, g_ln_g, g_ln_b, g_b_pw2 = [shard_cols(stacked(nm, n_conv))
                                                        for nm in ("b_pw1", "w_dw", "b_dw", "ln_g", "ln_b", "b_pw2")]

    dmod_cols = jnp.transpose(dmod_all, (1, 0, 2, 3, 4)).reshape(depth, N_DEV * bsz, N_MOD * d)
    dmod_cols = lax.dynamic_slice_in_dim(dmod_cols, mq * n_mod, n_mod, axis=2)
    g_w_mod = _mod_bwd(c_all, dmod_cols, name="mod_bwd")

    order = [("qkv", n_attn), ("o", n_attn), ("pw1", n_conv), ("pw2", n_conv), ("up", depth), ("down", depth)]
    grads, stack = [], []
    for nm, count in order:
        stack.append(list(range(len(grads), len(grads) + count)))
        grads += [big[nm, k] for k in range(count)]
    g_w_qkv, g_w_o, g_w_pw1, g_w_pw2, g_w_up, g_w_down = _reduce_scatter(grads, stack, my_c, my_q, name="rs")

    def adam(w, g, m, v, name):
        two_d = lambda a: a.reshape(-1, a.shape[-1])
        return [r.reshape(w.shape) for r in _adamw(two_d(w), two_d(g), two_d(m), two_d(v), name=name)]

    results = {}
    for nm, w, g, m, v in (("w_mod", w_mod, g_w_mod, m_w_mod, v_w_mod), ("w_qkv", w_qkv, g_w_qkv, m_w_qkv, v_w_qkv),
                           ("w_o", w_o, g_w_o, m_w_o, v_w_o), ("w_pw1", w_pw1, g_w_pw1, m_w_pw1, v_w_pw1),
                           ("w_pw2", w_pw2, g_w_pw2, m_w_pw2, v_w_pw2), ("w_up", w_up, g_w_up, m_w_up, v_w_up),
                           ("w_down", w_down, g_w_down, m_w_down, v_w_down)):
        results[nm] = (g,) + tuple(adam(w, g, m, v, "adamw"))
    small_w = dict(b_mod=(b_mod, g_b_mod, m_b_mod, v_b_mod), norm_mix=(norm_mix, g_norm_mix, m_norm_mix, v_norm_mix),
                   norm_mlp=(norm_mlp, g_norm_mlp, m_norm_mlp, v_norm_mlp), b_qkv=(b_qkv, g_b_qkv, m_b_qkv, v_b_qkv),
                   b_o=(b_o, g_b_o, m_b_o, v_b_o), sinks=(sinks, g_sinks, m_sinks, v_sinks),
                   b_pw1=(b_pw1, g_b_pw1, m_b_pw1, v_b_pw1), w_dw=(w_dw, g_w_dw, m_w_dw, v_w_dw),
                   b_dw=(b_dw, g_b_dw, m_b_dw, v_b_dw), conv_ln_g=(conv_ln_g, g_ln_g, m_conv_ln_g, v_conv_ln_g),
                   conv_ln_b=(conv_ln_b, g_ln_b, m_conv_ln_b, v_conv_ln_b), b_pw2=(b_pw2, g_b_pw2, m_b_pw2, v_b_pw2),
                   final_norm=(final_norm, g_final, m_final_norm, v_final_norm))
    names = list(small_w)
    deltas, new_ms, new_vs = _adamw_packed(*[[small_w[nm][k] for nm in names] for k in range(4)], d, name="adamw_small")
    for k, nm in enumerate(names):
        results[nm] = (small_w[nm][1], deltas[k], new_ms[k], new_vs[k])

    weight_order = ["w_mod", "b_mod", "norm_mix", "norm_mlp", "w_qkv", "b_qkv", "w_o", "b_o", "sinks", "w_pw1", "b_pw1",
                    "w_dw", "b_dw", "conv_ln_g", "conv_ln_b", "w_pw2", "b_pw2", "w_up", "w_down", "final_norm"]
    return (loss, grad_x, *[results[nm][0] for nm in weight_order], *[results[nm][1] for nm in weight_order],
            *[results[nm][2] for nm in weight_order], *[results[nm][3] for nm in weight_order])
```

```python
import functools

import jax
import jax.numpy as jnp
from jax import lax
from jax.experimental import pallas as pl
from jax.experimental.pallas import tpu as pltpu

F32, BF16 = jnp.float32, jnp.bfloat16
MESH = pl.DeviceIdType.MESH
N_CHIPS = 4
N_DEV = 8
LANES = 128
SUBLANES = 8
VMEM_LIMIT = 48 * 1024 * 1024

NORM_EPS = 1e-6
HEAD_DIM = 64
N_KV_HEADS = 2
WINDOW = 128
CONV_WIDTH = 31
CONV_HALO = 32
CONV_ROWS = 32
N_MOD = 6

ADAM_LR, ADAM_B1, ADAM_B2, ADAM_EPS, ADAM_WD, ADAM_STEP = 0.001, 0.9, 0.999, 1e-08, 0.01, 10

HBM_SPEC = pl.BlockSpec(memory_space=pltpu.HBM)
VMEM_SPEC = pl.BlockSpec(memory_space=pltpu.VMEM)
SMEM_SPEC = pl.BlockSpec(memory_space=pltpu.SMEM)


def _params(*sem):
    return pltpu.CompilerParams(dimension_semantics=sem or None, vmem_limit_bytes=VMEM_LIMIT)


def _row_tile(rows, width_bytes, target=1 << 20):
    t = rows
    while t % 2 == 0 and t > SUBLANES and t * width_bytes > target:
        t //= 2
    return t


CHIP_KINDS = ("x", "y", "xy")


def _position():
    x, y, c = lax.axis_index("x"), lax.axis_index("y"), lax.axis_index("c")
    return x, y, c, 2 * x + y


def _peer(kind, x, y, c):
    return {"c": (x, y, 1 - c), "x": (1 - x, y, c), "y": (x, 1 - y, c), "xy": (1 - x, 1 - y, c)}[kind]


def _peer_shard(kind, x, y):
    px, py, _ = _peer(kind, x, y, 0)
    return 2 * px + py


class _Plan:
    def __init__(self, link, cost, operands, out_shapes, copies, aliases=None, then=None):
        self.link, self.cost = link, cost
        self.operands, self.out_shapes, self.copies = list(operands), list(out_shapes), list(copies)
        self.aliases, self.then = dict(aliases or {}), then


def _merge(plans):
    operands, out_shapes, copies, aliases, thens = [], [], [], {}, []
    for p in plans:
        i0, o0 = len(operands), len(out_shapes)
        i1, o1 = i0 + len(p.operands), o0 + len(p.out_shapes)

        def shifted(f, i0=i0, i1=i1, o0=o0, o1=o1):
            return lambda ins, outs, pos: f(ins[i0:i1], outs[o0:o1], pos)

        copies += [(shifted(src), shifted(dst), shifted(land), kind) for src, dst, land, kind in p.copies]
        aliases.update({i0 + k: o0 + v for k, v in p.aliases.items()})
        operands += p.operands
        out_shapes += p.out_shapes
        thens.append((p.then, o0, o1))

    def then(outs):
        for f, o0, o1 in thens:
            if f is not None:
                f(outs[o0:o1])

    return _Plan("mixed", sum(p.cost for p in plans), operands, out_shapes, copies, aliases, then)


def _call(body, *, name, out_shape, operands, grid=(), in_specs=(), out_specs=(), scratch_shapes=(), sem=(), comm=None):
    single = not isinstance(out_shape, (tuple, list))
    out_shape = [out_shape] if single else list(out_shape)
    out_specs = [out_specs] if single else list(out_specs)
    if comm is None:
        res = pl.pallas_call(body, out_shape=out_shape, grid=grid, in_specs=list(in_specs), out_specs=out_specs,
                             scratch_shapes=list(scratch_shapes), name=name, compiler_params=_params(*sem))(*operands)
        return res[0] if single else res
    n_in, n_out, n_scr = len(operands), len(out_shape), len(scratch_shapes)
    c_in, c_out, n_cp = len(comm.operands), len(comm.out_shapes), len(comm.copies)

    def wrapped(*refs):
        ins, refs = refs[:n_in], refs[n_in:]
        cins, refs = refs[:c_in], refs[c_in:]
        outs, refs = refs[:n_out], refs[n_out:]
        couts, refs = refs[:c_out], refs[c_out:]
        scr, (send_sems, recv_sems) = refs[:n_scr], refs[n_scr:]

        def descriptors():
            pos = _position()
            sends, lands = [], []
            for k, (src, dst, landing, kind) in enumerate(comm.copies):
                common = dict(send_sem=send_sems.at[k], recv_sem=recv_sems.at[k],
                              device_id=_peer(kind, *pos[:3]), device_id_type=MESH)
                sends.append(pltpu.make_async_remote_copy(src_ref=src(cins, couts, pos), dst_ref=dst(cins, couts, pos), **common))
                lands.append(pltpu.make_async_remote_copy(src_ref=src(cins, couts, pos), dst_ref=landing(cins, couts, pos), **common))
            return sends, lands

        def start():
            for cp in descriptors()[0]:
                cp.start()

        def finish():
            sends, lands = descriptors()
            for cp in lands:
                cp.wait_recv()
            for cp in sends:
                cp.wait_send()

        if not grid:
            start()
            body(*ins, *outs, *scr)
            finish()
        else:
            ids = [pl.program_id(ax) for ax in range(len(grid))]
            first, last = ids[0] == 0, ids[0] == grid[0] - 1
            for ax in range(1, len(grid)):
                first, last = first & (ids[ax] == 0), last & (ids[ax] == grid[ax] - 1)
            pl.when(first)(start)
            body(*ins, *outs, *scr)
            pl.when(last)(finish)

    res = pl.pallas_call(wrapped, out_shape=out_shape + comm.out_shapes, grid=grid,
                         in_specs=list(in_specs) + [HBM_SPEC] * c_in, out_specs=out_specs + [HBM_SPEC] * c_out,
                         scratch_shapes=list(scratch_shapes) + [pltpu.SemaphoreType.DMA((n_cp,)), pltpu.SemaphoreType.DMA((n_cp,))],
                         input_output_aliases={n_in + k: n_out + v for k, v in comm.aliases.items()},
                         name=name, compiler_params=_params(*["arbitrary"] * len(grid)))(*operands, *comm.operands)
    if comm.then is not None:
        comm.then(res[n_out:])
    return res[0] if single else res[:n_out]


def _exchange(plan, *, name):
    _call(lambda: None, name=name, out_shape=[], operands=[], comm=plan)


def _mm(a, b, *, name, nt=False, tm=512, tn=512, epi="plain", out_dtype=BF16,
        bias=None, act=None, resid=None, gate=None, seq=None, comm=None):
    m, k = a.shape
    n = b.shape[0] if nt else b.shape[1]
    tm, tn = min(tm, m), min(tn, n)
    assert m % tm == 0 and n % tn == 0
    dims = (((1,), (1,)), ((), ())) if nt else (((1,), (0,)), ((), ()))
    tile = pl.BlockSpec((tm, tn), lambda j, i: (i, j))
    operands = [a, b]
    in_specs = [pl.BlockSpec((tm, k), lambda j, i: (i, 0)),
                pl.BlockSpec((tn, k), lambda j, i: (j, 0)) if nt else pl.BlockSpec((k, tn), lambda j, i: (0, j))]
    if bias is not None:
        operands.append(bias.reshape(1, n))
        in_specs.append(pl.BlockSpec((1, tn), lambda j, i: (0, j)))
    if epi == "dact":
        operands.append(act)
        in_specs.append(tile)
    if epi == "resid":
        assert seq % tm == 0
        operands += [resid, gate]
        in_specs += [tile, pl.BlockSpec((None, 1, tn), lambda j, i: (i * tm // seq, 0, j))]
        out_shape = (jax.ShapeDtypeStruct((m, n), BF16), jax.ShapeDtypeStruct((m, n), F32))
        out_specs = (tile, tile)
    else:
        out_shape = jax.ShapeDtypeStruct((m, n), out_dtype)
        out_specs = tile

    def body(*refs):
        it = iter(refs)
        a_ref, b_ref = next(it), next(it)
        acc = lax.dot_general(a_ref[...], b_ref[...], dims, preferred_element_type=F32)
        if bias is not None:
            acc = acc + next(it)[...]
        if epi == "plain":
            next(it)[...] = acc.astype(out_dtype)
        elif epi == "relu2":
            r = jnp.maximum(acc, 0.0)
            next(it)[...] = (r * r).astype(out_dtype)
        elif epi == "dact":
            act_ref = next(it)
            next(it)[...] = (acc * (2.0 * jnp.sqrt(act_ref[...].astype(F32)))).astype(out_dtype)
        else:
            resid_ref, gate_ref = next(it), next(it)
            y_ref, x_ref = next(it), next(it)
            y_ref[...] = acc.astype(BF16)
            x_ref[...] = resid_ref[...] + gate_ref[...] * acc

    return _call(body, name=name, out_shape=out_shape, operands=operands, grid=(n // tn, m // tm), in_specs=in_specs,
                 out_specs=out_specs, sem=("parallel", "parallel"), comm=comm)


def _mm_tn(a, b, *, name, tm=1024, tn=1024, tk=512, col_shards=None, comm=None):
    t, m = a.shape
    n = b.shape[1]
    tm, tk = min(tm, m), min(tk, t)
    if col_shards is None:
        tn = min(tn, n)
        out_shape = jax.ShapeDtypeStruct((m, n), F32)
        out_spec = pl.BlockSpec((tm, tn), lambda i, j, k: (i, j))
    else:
        per = n // col_shards
        tn = min(tn, per)
        assert per % tn == 0
        out_shape = jax.ShapeDtypeStruct((col_shards, m, per), F32)
        out_spec = pl.BlockSpec((None, tm, tn), lambda i, j, k: (j // (per // tn), i, j % (per // tn)))
    assert m % tm == 0 and n % tn == 0 and t % tk == 0

    def body(a_ref, b_ref, o_ref):
        @pl.when(pl.program_id(2) == 0)
        def _():
            o_ref[...] = jnp.zeros_like(o_ref)

        o_ref[...] += lax.dot_general(a_ref[...], b_ref[...], (((0,), (0,)), ((), ())),
                                      preferred_element_type=F32)

    return _call(body, name=name, out_shape=out_shape, operands=[a, b], grid=(m // tm, n // tn, t // tk),
                 in_specs=[pl.BlockSpec((tk, tm), lambda i, j, k: (k, i)), pl.BlockSpec((tk, tn), lambda i, j, k: (k, j))],
                 out_specs=out_spec, sem=("parallel", "parallel", "arbitrary"), comm=comm)


def _normmod(x, gamma, sc, sh, *, name):
    bsz, s, d = x.shape
    ts = _row_tile(s, d * 4, 2 << 20)

    def body(x_ref, g_ref, sc_ref, sh_ref, o_ref):
        xf = x_ref[...]
        r = lax.rsqrt(jnp.mean(xf * xf, axis=-1, keepdims=True) + NORM_EPS)
        o_ref[...] = (xf * r * g_ref[...] * (1.0 + sc_ref[...]) + sh_ref[...]).astype(BF16)

    row = pl.BlockSpec((None, ts, d), lambda b, i: (b, i, 0))
    per_ex = pl.BlockSpec((None, 1, d), lambda b, i: (b, 0, 0))
    return pl.pallas_call(body, out_shape=jax.ShapeDtypeStruct(x.shape, BF16), grid=(bsz, s // ts),
                          in_specs=[row, pl.BlockSpec((1, d), lambda b, i: (0, 0)), per_ex, per_ex],
                          out_specs=row, name=name, compiler_params=_params("parallel", "parallel"))(x, gamma, sc, sh)


def _normmod_bwd(x, dh, dres, gamma, sc, *, name):
    bsz, s, d = x.shape
    ts = _row_tile(s, d * 4, 1 << 20)

    def body(x_ref, dh_ref, dres_ref, g_ref, sc_ref, dx_ref, p_ref, dsh_ref):
        xf = x_ref[...]
        r = lax.rsqrt(jnp.mean(xf * xf, axis=-1, keepdims=True) + NORM_EPS)
        xhat = xf * r
        dh_v = dh_ref[...]
        dxhat = dh_v * (g_ref[...] * (1.0 + sc_ref[...]))
        dx_ref[...] = dres_ref[...] + r * (dxhat - xhat * jnp.mean(dxhat * xhat, axis=-1, keepdims=True))

        @pl.when(pl.program_id(1) == 0)
        def _():
            p_ref[...] = jnp.zeros_like(p_ref)
            dsh_ref[...] = jnp.zeros_like(dsh_ref)

        p_ref[...] += jnp.sum(dh_v * xhat, axis=0, keepdims=True)
        dsh_ref[...] += jnp.sum(dh_v, axis=0, keepdims=True)

    row = pl.BlockSpec((None, ts, d), lambda b, i: (b, i, 0))
    per_ex = pl.BlockSpec((None, 1, d), lambda b, i: (b, 0, 0))
    vec = jax.ShapeDtypeStruct((bsz, 1, d), F32)
    return pl.pallas_call(body, out_shape=(jax.ShapeDtypeStruct(x.shape, F32), vec, vec), grid=(bsz, s // ts),
                          in_specs=[row, row, row, pl.BlockSpec((1, d), lambda b, i: (0, 0)), per_ex],
                          out_specs=(row, per_ex, per_ex), name=name,
                          compiler_params=_params("parallel", "arbitrary"))(x, dh, dres, gamma, sc)


def _gate_bwd(dx, y, gate, *, name):
    bsz, s, d = dx.shape
    ts = _row_tile(s, d * 4, 2 << 20)

    def body(dx_ref, y_ref, g_ref, dy_ref, dg_ref, sdx_ref):
        dxv = dx_ref[...]
        dy_ref[...] = (g_ref[...] * dxv).astype(BF16)

        @pl.when(pl.program_id(1) == 0)
        def _():
            dg_ref[...] = jnp.zeros_like(dg_ref)
            sdx_ref[...] = jnp.zeros_like(sdx_ref)

        dg_ref[...] += jnp.sum(dxv * y_ref[...].astype(F32), axis=0, keepdims=True)
        sdx_ref[...] += jnp.sum(dxv, axis=0, keepdims=True)

    row = pl.BlockSpec((None, ts, d), lambda b, i: (b, i, 0))
    per_ex = pl.BlockSpec((None, 1, d), lambda b, i: (b, 0, 0))
    vec = jax.ShapeDtypeStruct((bsz, 1, d), F32)
    return pl.pallas_call(body, out_shape=(jax.ShapeDtypeStruct(dx.shape, BF16), vec, vec), grid=(bsz, s // ts),
                          in_specs=[row, row, per_ex], out_specs=(row, per_ex, per_ex), name=name,
                          compiler_params=_params("parallel", "arbitrary"))(dx, y, gate)


def _loss_head(x, target, gamma, *, name):
    bsz, s, d = x.shape
    ts = _row_tile(s, d * 4, 1 << 20)

    def body(x_ref, t_ref, g_ref, dx_ref, loss_ref, dg_ref):
        xf = x_ref[...]
        r = lax.rsqrt(jnp.mean(xf * xf, axis=-1, keepdims=True) + NORM_EPS)
        xhat = xf * r
        err = xhat * g_ref[...] - t_ref[...]
        dy = err * (1.0 / d)
        dxhat = dy * g_ref[...]
        dx_ref[...] = r * (dxhat - xhat * jnp.mean(dxhat * xhat, axis=-1, keepdims=True))

        @pl.when((pl.program_id(0) == 0) & (pl.program_id(1) == 0))
        def _():
            loss_ref[...] = jnp.zeros_like(loss_ref)
            dg_ref[...] = jnp.zeros_like(dg_ref)

        loss_ref[...] += jnp.sum(err * err, axis=0, keepdims=True)
        dg_ref[...] += jnp.sum(dy * xhat, axis=0, keepdims=True)

    row = pl.BlockSpec((None, ts, d), lambda b, i: (b, i, 0))
    one = pl.BlockSpec((1, d), lambda b, i: (0, 0))
    vec = jax.ShapeDtypeStruct((1, d), F32)
    return pl.pallas_call(body, out_shape=(jax.ShapeDtypeStruct(x.shape, F32), vec, vec), grid=(bsz, s // ts),
                          in_specs=[row, row, one], out_specs=(row, one, one), name=name,
                          compiler_params=_params("arbitrary", "arbitrary"))(x, target, gamma)


def _alibi_slope(h, n_heads):
    return 2.0 ** (-8.0 * (h + 1) / n_heads)


def _attn_masks(first_block):
    qi = lax.broadcasted_iota(jnp.int32, (WINDOW, 2 * WINDOW), 0)
    ki = lax.broadcasted_iota(jnp.int32, (WINDOW, 2 * WINDOW), 1)
    dist = qi + WINDOW - ki
    first_key = jnp.where(first_block, WINDOW, 0)
    valid = (dist >= 0) & (dist < WINDOW) & (ki >= first_key)
    return dist.astype(F32), valid


def _dup_halves(span, kv, left):
    f = span.astype(F32)
    rolled = pltpu.roll(f, HEAD_DIM, axis=1)
    out = jnp.where(left, f, rolled) if kv == 0 else jnp.where(left, rolled, f)
    return out.astype(BF16)


def _attn_probs(s, h, n_heads, distf, valid, sink):
    s = s * (HEAD_DIM ** -0.5) - _alibi_slope(h, n_heads) * distf
    s = jnp.where(valid, s, -1e30)
    m = jnp.maximum(jnp.max(s, axis=-1, keepdims=True), sink)
    e = jnp.exp(s - m)
    e_sink = jnp.exp(sink - m)
    inv = 1.0 / (jnp.sum(e, axis=-1, keepdims=True) + e_sink)
    return e * inv, e_sink * inv


def _attn_specs(d, n_blocks, clamp):
    kcol = d // LANES
    cur = (lambda i: jnp.minimum(i, n_blocks - 1)) if clamp else (lambda i: i)
    prev = lambda i: jnp.maximum(cur(i) - 1, 0)
    kv = lambda col, blk: pl.BlockSpec((None, WINDOW, LANES), lambda b, i: (b, blk(i), col))
    return [pl.BlockSpec((None, WINDOW, d), lambda b, i: (b, cur(i), 0)),
            kv(kcol, prev), kv(kcol, cur), kv(kcol + 1, prev), kv(kcol + 1, cur)]


def _attn_fwd(qkv, sinks, *, name):
    bsz, s, qkv_dim = qkv.shape
    d = qkv_dim - 2 * N_KV_HEADS * HEAD_DIM
    n_heads = d // HEAD_DIM
    group = n_heads // N_KV_HEADS
    pairs = group // 2
    n_blocks = s // WINDOW

    def body(q_ref, kp_ref, kc_ref, vp_ref, vc_ref, sink_ref, o_ref):
        left = lax.broadcasted_iota(jnp.int32, (1, LANES), 1) < HEAD_DIM
        distf, valid = _attn_masks(pl.program_id(1) == 0)
        kspan = jnp.concatenate([kp_ref[...], kc_ref[...]], axis=0)
        vspan = jnp.concatenate([vp_ref[...], vc_ref[...]], axis=0)
        for kv in range(N_KV_HEADS):
            kdup, vdup = _dup_halves(kspan, kv, left), _dup_halves(vspan, kv, left)
            res = []
            for par in range(2):
                keep = left if par == 0 else jnp.logical_not(left)
                cols = [pl.ds((kv * pairs + p) * LANES, LANES) for p in range(pairs)]
                lhs = jnp.concatenate([jnp.where(keep, q_ref[:, cl], jnp.zeros((), BF16)) for cl in cols], axis=0)
                sc = lax.dot_general(lhs, kdup, (((1,), (1,)), ((), ())), preferred_element_type=F32)
                probs = []
                for p in range(pairs):
                    h = kv * group + 2 * p + par
                    pr, _ = _attn_probs(sc[p * WINDOW:(p + 1) * WINDOW], h, n_heads, distf, valid, sink_ref[h])
                    probs.append(pr.astype(BF16))
                res.append(jnp.dot(jnp.concatenate(probs, axis=0), vdup, preferred_element_type=F32))
            for p in range(pairs):
                rows = slice(p * WINDOW, (p + 1) * WINDOW)
                o_ref[:, pl.ds((kv * pairs + p) * LANES, LANES)] = jnp.where(left, res[0][rows], res[1][rows]).astype(BF16)

    return pl.pallas_call(body, out_shape=jax.ShapeDtypeStruct((bsz, s, d), BF16), grid=(bsz, n_blocks),
                          in_specs=_attn_specs(d, n_blocks, False) + [SMEM_SPEC],
                          out_specs=pl.BlockSpec((None, WINDOW, d), lambda b, i: (b, i, 0)), name=name,
                          compiler_params=_params("parallel", "parallel"))(qkv, qkv, qkv, qkv, qkv, sinks)


def _attn_bwd(qkv, do, sinks, *, name):
    bsz, s, qkv_dim = qkv.shape
    d = qkv_dim - 2 * N_KV_HEADS * HEAD_DIM
    n_heads = d // HEAD_DIM
    group = n_heads // N_KV_HEADS
    pairs = group // 2
    n_blocks = s // WINDOW
    tn_dims = (((0,), (0,)), ((), ()))

    def body(q_ref, kp_ref, kc_ref, vp_ref, vc_ref, do_ref, sink_ref, dqkv_ref, colsum_ref, dsink_ref,
             dq_prev, dk_carry, dv_carry):
        b, i = pl.program_id(0), pl.program_id(1)
        left = lax.broadcasted_iota(jnp.int32, (1, LANES), 1) < HEAD_DIM

        @pl.when((b == 0) & (i == 0))
        def _():
            colsum_ref[...] = jnp.zeros_like(colsum_ref)
            dsink_ref[...] = jnp.zeros_like(dsink_ref)

        @pl.when(i == 0)
        def _():
            dqkv_ref[...] = jnp.zeros_like(dqkv_ref)
            dk_carry[...] = jnp.zeros_like(dk_carry)
            dv_carry[...] = jnp.zeros_like(dv_carry)

        @pl.when(i > 0)
        def _():
            dq_v = dq_prev[...]
            dqkv_ref[:, pl.ds(0, d)] = dq_v.astype(BF16)
            colsum_ref[:, pl.ds(0, d)] += jnp.sum(dq_v, axis=0, keepdims=True)

        @pl.when(i < n_blocks)
        def _():
            distf, valid = _attn_masks(i == 0)
            kspan = jnp.concatenate([kp_ref[...], kc_ref[...]], axis=0)
            vspan = jnp.concatenate([vp_ref[...], vc_ref[...]], axis=0)
            dk_blk, dv_blk = [], []
            for kv in range(N_KV_HEADS):
                kdup, vdup = _dup_halves(kspan, kv, left), _dup_halves(vspan, kv, left)
                dq_res, dk_sum, dv_sum = [], None, None
                for par in range(2):
                    keep = left if par == 0 else jnp.logical_not(left)
                    cols = [pl.ds((kv * pairs + p) * LANES, LANES) for p in range(pairs)]
                    zero = jnp.zeros((), BF16)
                    lhs = jnp.concatenate([jnp.where(keep, q_ref[:, cl], zero) for cl in cols], axis=0)
                    dol = jnp.concatenate([jnp.where(keep, do_ref[:, cl], zero) for cl in cols], axis=0)
                    sc = lax.dot_general(lhs, kdup, (((1,), (1,)), ((), ())), preferred_element_type=F32)
                    dp = lax.dot_general(dol, vdup, (((1,), (1,)), ((), ())), preferred_element_type=F32)
                    probs, dscores = [], []
                    for p in range(pairs):
                        h = kv * group + 2 * p + par
                        rows = slice(p * WINDOW, (p + 1) * WINDOW)
                        pr, p_sink = _attn_probs(sc[rows], h, n_heads, distf, valid, sink_ref[h])
                        delta = jnp.sum(pr * dp[rows], axis=-1, keepdims=True)
                        dscores.append((pr * (dp[rows] - delta) * (HEAD_DIM ** -0.5)).astype(BF16))
                        probs.append(pr.astype(BF16))
                        dsink_ref[pl.ds(h, 1), :] += jnp.zeros((1, LANES), F32) - jnp.sum(p_sink * delta)
                    ds_all = jnp.concatenate(dscores, axis=0)
                    p_all = jnp.concatenate(probs, axis=0)
                    dq_res.append(jnp.dot(ds_all, kdup, preferred_element_type=F32))
                    dk_par = lax.dot_general(ds_all, lhs, tn_dims, preferred_element_type=F32)
                    dv_par = lax.dot_general(p_all, dol, tn_dims, preferred_element_type=F32)
                    dk_sum = dk_par if dk_sum is None else dk_sum + dk_par
                    dv_sum = dv_par if dv_sum is None else dv_sum + dv_par
                for p in range(pairs):
                    rows = slice(p * WINDOW, (p + 1) * WINDOW)
                    dq_prev[:, pl.ds((kv * pairs + p) * LANES, LANES)] = jnp.where(left, dq_res[0][rows], dq_res[1][rows])
                dk_blk.append(dk_sum + pltpu.roll(dk_sum, HEAD_DIM, axis=1))
                dv_blk.append(dv_sum + pltpu.roll(dv_sum, HEAD_DIM, axis=1))
            dk_span = jnp.where(left, dk_blk[0], dk_blk[1])
            dv_span = jnp.where(left, dv_blk[0], dv_blk[1])
            dk_done = dk_carry[...] + dk_span[:WINDOW]
            dv_done = dv_carry[...] + dv_span[:WINDOW]
            dk_carry[...] = dk_span[WINDOW:]
            dv_carry[...] = dv_span[WINDOW:]

            @pl.when(i > 0)
            def _():
                dqkv_ref[:, pl.ds(d, LANES)] = dk_done.astype(BF16)
                dqkv_ref[:, pl.ds(d + LANES, LANES)] = dv_done.astype(BF16)
                colsum_ref[:, pl.ds(d, LANES)] += jnp.sum(dk_done, axis=0, keepdims=True)
                colsum_ref[:, pl.ds(d + LANES, LANES)] += jnp.sum(dv_done, axis=0, keepdims=True)

        @pl.when(i == n_blocks)
        def _():
            dk_done, dv_done = dk_carry[...], dv_carry[...]
            dqkv_ref[:, pl.ds(d, LANES)] = dk_done.astype(BF16)
            dqkv_ref[:, pl.ds(d + LANES, LANES)] = dv_done.astype(BF16)
            colsum_ref[:, pl.ds(d, LANES)] += jnp.sum(dk_done, axis=0, keepdims=True)
            colsum_ref[:, pl.ds(d + LANES, LANES)] += jnp.sum(dv_done, axis=0, keepdims=True)

    do_spec = pl.BlockSpec((None, WINDOW, d), lambda b, i: (b, jnp.minimum(i, n_blocks - 1), 0))
    out_shape = (jax.ShapeDtypeStruct((bsz, s, qkv_dim), BF16), jax.ShapeDtypeStruct((1, qkv_dim), F32),
                 jax.ShapeDtypeStruct((n_heads, LANES), F32))
    out_specs = (pl.BlockSpec((None, WINDOW, qkv_dim), lambda b, i: (b, jnp.maximum(i - 1, 0), 0)),
                 pl.BlockSpec((1, qkv_dim), lambda b, i: (0, 0)),
                 pl.BlockSpec((n_heads, LANES), lambda b, i: (0, 0)))
    return pl.pallas_call(body, out_shape=out_shape, grid=(bsz, n_blocks + 1),
                          in_specs=_attn_specs(d, n_blocks, True) + [do_spec, SMEM_SPEC], out_specs=out_specs,
                          scratch_shapes=[pltpu.VMEM((WINDOW, d), F32), pltpu.VMEM((WINDOW, LANES), F32),
                                          pltpu.VMEM((WINDOW, LANES), F32)],
                          name=name, compiler_params=_params("arbitrary", "arbitrary"))(qkv, qkv, qkv, qkv, qkv, do, sinks)


def _conv_tile(s):
    return min(256, s)


def _halo_specs(ts, width, s):
    per = ts // CONV_HALO
    prev = pl.BlockSpec((None, CONV_HALO, width), lambda b, i: (b, jnp.maximum(i * per - 1, 0), 0))
    nxt = pl.BlockSpec((None, CONV_HALO, width), lambda b, i: (b, jnp.minimum((i + 1) * per, s // CONV_HALO - 1), 0))
    cur = pl.BlockSpec((None, ts, width), lambda b, i: (b, i, 0))
    return prev, cur, nxt


def _glu(u, d):
    return u[:, :d] * jax.nn.sigmoid(u[:, d:])


def _taps(w_ref, buf, out_ref, ts, d, offset):
    for r0 in range(0, ts, CONV_ROWS):
        for l0 in range(0, d, LANES):
            lanes = pl.ds(l0, LANES)
            acc = jnp.zeros((CONV_ROWS, LANES), F32)
            for j in range(CONV_WIDTH):
                acc = acc + w_ref[pl.ds(j, 1), lanes] * buf[pl.ds(r0 + offset(j), CONV_ROWS), lanes]
            out_ref[pl.ds(r0, CONV_ROWS), lanes] = acc


def _conv_fwd(u, w_dw, b_dw, ln_g, ln_b, *, name):
    bsz, s, d2 = u.shape
    d = d2 // 2
    ts = _conv_tile(s)

    def body(up_ref, uc_ref, w_ref, bdw_ref, g_ref, b_ref, z_ref, v_ref, gbuf):
        halo = _glu(up_ref[...], d)
        gbuf[pl.ds(0, CONV_HALO), :] = jnp.where(pl.program_id(1) > 0, halo, 0.0)
        gbuf[pl.ds(CONV_HALO, ts), :] = _glu(uc_ref[...], d)
        _taps(w_ref, gbuf, v_ref, ts, d, lambda j: CONV_HALO - (CONV_WIDTH - 1) + j)
        v = v_ref[...] + bdw_ref[...]
        v_ref[...] = v
        mu = jnp.mean(v, axis=-1, keepdims=True)
        cen = v - mu
        rstd = lax.rsqrt(jnp.mean(cen * cen, axis=-1, keepdims=True) + NORM_EPS)
        ln = cen * rstd * g_ref[...] + b_ref[...]
        z_ref[...] = (ln * jax.nn.sigmoid(ln)).astype(BF16)

    prev, cur, _ = _halo_specs(ts, d2, s)
    one = pl.BlockSpec((1, d), lambda b, i: (0, 0))
    row = pl.BlockSpec((None, ts, d), lambda b, i: (b, i, 0))
    return pl.pallas_call(body, out_shape=(jax.ShapeDtypeStruct((bsz, s, d), BF16), jax.ShapeDtypeStruct((bsz, s, d), F32)),
                          grid=(bsz, s // ts),
                          in_specs=[prev, cur, pl.BlockSpec((CONV_HALO, d), lambda b, i: (0, 0)), one, one, one],
                          out_specs=(row, row), scratch_shapes=[pltpu.VMEM((ts + CONV_HALO, d), F32)], name=name,
                          compiler_params=_params("parallel", "parallel"))(u, u, w_dw, b_dw, ln_g, ln_b)


def _conv_bwd_ln(dz, v, ln_g, ln_b, *, name):
    bsz, s, d = v.shape
    ts = _row_tile(s, d * 4, 1 << 20)

    def body(dz_ref, v_ref, g_ref, b_ref, dv_ref, dg_ref, db_ref, dbdw_ref):
        v_v = v_ref[...]
        mu = jnp.mean(v_v, axis=-1, keepdims=True)
        cen = v_v - mu
        rstd = lax.rsqrt(jnp.mean(cen * cen, axis=-1, keepdims=True) + NORM_EPS)
        vhat = cen * rstd
        ln = vhat * g_ref[...] + b_ref[...]
        sig = jax.nn.sigmoid(ln)
        dln = dz_ref[...] * (sig * (1.0 + ln * (1.0 - sig)))
        dvhat = dln * g_ref[...]
        dv = rstd * (dvhat - jnp.mean(dvhat, axis=-1, keepdims=True)
                     - vhat * jnp.mean(dvhat * vhat, axis=-1, keepdims=True))
        dv_ref[...] = dv

        @pl.when((pl.program_id(0) == 0) & (pl.program_id(1) == 0))
        def _():
            dg_ref[...] = jnp.zeros_like(dg_ref)
            db_ref[...] = jnp.zeros_like(db_ref)
            dbdw_ref[...] = jnp.zeros_like(dbdw_ref)

        dg_ref[...] += jnp.sum(dln * vhat, axis=0, keepdims=True)
        db_ref[...] += jnp.sum(dln, axis=0, keepdims=True)
        dbdw_ref[...] += jnp.sum(dv, axis=0, keepdims=True)

    row = pl.BlockSpec((None, ts, d), lambda b, i: (b, i, 0))
    one = pl.BlockSpec((1, d), lambda b, i: (0, 0))
    vec = jax.ShapeDtypeStruct((1, d), F32)
    return pl.pallas_call(body, out_shape=(jax.ShapeDtypeStruct(v.shape, F32), vec, vec, vec), grid=(bsz, s // ts),
                          in_specs=[row, row, one, one], out_specs=(row, one, one, one), name=name,
                          compiler_params=_params("arbitrary", "arbitrary"))(dz, v, ln_g, ln_b)


def _conv_bwd_taps(dv, u, w_dw, *, name):
    bsz, s, d = dv.shape
    ts = _conv_tile(s)
    n_tiles = s // ts

    def body(dvc_ref, dvn_ref, up_ref, uc_ref, w_ref, du_ref, dbu_ref, dw_ref, gbuf, dvbuf, dglu):
        i = pl.program_id(1)

        @pl.when((pl.program_id(0) == 0) & (i == 0))
        def _():
            dbu_ref[...] = jnp.zeros_like(dbu_ref)
            dw_ref[...] = jnp.zeros_like(dw_ref)

        gbuf[pl.ds(0, CONV_HALO), :] = jnp.where(i > 0, _glu(up_ref[...], d), 0.0)
        gbuf[pl.ds(CONV_HALO, ts), :] = _glu(uc_ref[...], d)
        dvbuf[pl.ds(0, ts), :] = dvc_ref[...]
        dvbuf[pl.ds(ts, CONV_HALO), :] = jnp.where(i < n_tiles - 1, dvn_ref[...], 0.0)
        _taps(w_ref, dvbuf, dglu, ts, d, lambda j: CONV_WIDTH - 1 - j)
        for l0 in range(0, d, LANES):
            lanes = pl.ds(l0, LANES)
            for j in range(CONV_WIDTH):
                acc = jnp.zeros((SUBLANES, LANES), F32)
                for r0 in range(0, ts, CONV_ROWS):
                    prod = dvbuf[pl.ds(r0, CONV_ROWS), lanes] * gbuf[pl.ds(r0 + CONV_HALO - (CONV_WIDTH - 1) + j, CONV_ROWS), lanes]
                    for k in range(0, CONV_ROWS, SUBLANES):
                        acc = acc + prod[k:k + SUBLANES]
                dw_ref[j, :, lanes] += acc
        u_v = uc_ref[...]
        a, sig = u_v[:, :d], jax.nn.sigmoid(u_v[:, d:])
        dg_v = dglu[...]
        da = dg_v * sig
        dgate = dg_v * a * sig * (1.0 - sig)
        du_ref[:, pl.ds(0, d)] = da.astype(BF16)
        du_ref[:, pl.ds(d, d)] = dgate.astype(BF16)
        dbu_ref[:, pl.ds(0, d)] += jnp.sum(da, axis=0, keepdims=True)
        dbu_ref[:, pl.ds(d, d)] += jnp.sum(dgate, axis=0, keepdims=True)

    _, dv_cur, dv_next = _halo_specs(ts, d, s)
    u_prev, u_cur, _ = _halo_specs(ts, 2 * d, s)
    out_shape = (jax.ShapeDtypeStruct((bsz, s, 2 * d), BF16), jax.ShapeDtypeStruct((1, 2 * d), F32),
                 jax.ShapeDtypeStruct((CONV_HALO, SUBLANES, d), F32))
    out_specs = (pl.BlockSpec((None, ts, 2 * d), lambda b, i: (b, i, 0)), pl.BlockSpec((1, 2 * d), lambda b, i: (0, 0)),
                 pl.BlockSpec((CONV_HALO, SUBLANES, d), lambda b, i: (0, 0, 0)))
    return pl.pallas_call(body, out_shape=out_shape, grid=(bsz, n_tiles),
                          in_specs=[dv_cur, dv_next, u_prev, u_cur, pl.BlockSpec((CONV_HALO, d), lambda b, i: (0, 0))],
                          out_specs=out_specs,
                          scratch_shapes=[pltpu.VMEM((ts + CONV_HALO, d), F32), pltpu.VMEM((ts + CONV_HALO, d), F32),
                                          pltpu.VMEM((ts, d), F32)],
                          name=name, compiler_params=_params("arbitrary", "arbitrary"))(dv, dv, u, u, w_dw)


def _mod_fwd(c_all, w_mod, b_mod, *, name):
    n_layers, d, n = w_mod.shape
    rows = c_all.shape[0]

    def body(c_ref, w_ref, b_ref, o_ref):
        cv = c_ref[...]
        cs = (cv * jax.nn.sigmoid(cv)).astype(BF16)
        o_ref[...] = jnp.dot(cs, w_ref[...].astype(BF16), preferred_element_type=F32) + b_ref[...]

    return pl.pallas_call(body, out_shape=jax.ShapeDtypeStruct((n_layers, rows, n), F32), grid=(n_layers,),
                          in_specs=[pl.BlockSpec((rows, d), lambda l: (0, 0)), pl.BlockSpec((None, d, n), lambda l: (l, 0, 0)),
                                    pl.BlockSpec((None, 1, n), lambda l: (l, 0, 0))],
                          out_specs=pl.BlockSpec((None, rows, n), lambda l: (l, 0, 0)), name=name,
                          compiler_params=_params("parallel"))(c_all, w_mod, b_mod)


def _mod_bwd(c_all, dmod, *, name):
    n_layers, rows, n = dmod.shape
    d = c_all.shape[1]

    def body(c_ref, g_ref, o_ref):
        cv = c_ref[...]
        cs = (cv * jax.nn.sigmoid(cv)).astype(BF16)
        o_ref[...] = lax.dot_general(cs, g_ref[...].astype(BF16), (((0,), (0,)), ((), ())), preferred_element_type=F32)

    return pl.pallas_call(body, out_shape=jax.ShapeDtypeStruct((n_layers, d, n), F32), grid=(n_layers,),
                          in_specs=[pl.BlockSpec((rows, d), lambda l: (0, 0)), pl.BlockSpec((None, rows, n), lambda l: (l, 0, 0))],
                          out_specs=pl.BlockSpec((None, d, n), lambda l: (l, 0, 0)), name=name,
                          compiler_params=_params("parallel"))(c_all, dmod)


def _add_half(g, recv, my_c, *, name):
    p, _, r, cdim = g.shape
    tr = _row_tile(r, cdim * 4)

    def body(c_ref, g_ref, r_ref, o_ref):
        o_ref[...] = (g_ref[...] + r_ref[...]).astype(BF16)

    grid_spec = pltpu.PrefetchScalarGridSpec(
        num_scalar_prefetch=1, grid=(p, r // tr),
        in_specs=[pl.BlockSpec((None, None, tr, cdim), lambda q, i, c_ref: (q, c_ref[0], i, 0)),
                  pl.BlockSpec((None, None, tr, cdim), lambda q, i, c_ref: (q, 0, i, 0))],
        out_specs=pl.BlockSpec((None, tr, cdim), lambda q, i, c_ref: (q, i, 0)))
    return pl.pallas_call(body, out_shape=jax.ShapeDtypeStruct((p, r, cdim), BF16), grid_spec=grid_spec, name=name,
                          compiler_params=_params("parallel", "parallel"))(my_c, g, recv)


def _add_pieces(chip, recv, my_qc, stacked, layer, n_layers, *, name):
    _, r, cdim = chip.shape
    tr = _row_tile(r, cdim * 4)

    def body(qc_ref, own_ref, r0_ref, r1_ref, r2_ref, *rest):
        f32 = lambda ref: ref[...].astype(F32)
        rest[-1][...] = ((f32(own_ref) + f32(r0_ref)) + f32(r1_ref)) + f32(r2_ref)

    piece = lambda k: pl.BlockSpec((None, tr, cdim), lambda i, qc_ref: (k, i, 0))
    in_specs = [pl.BlockSpec((None, tr, cdim), lambda i, qc_ref: (qc_ref[0], i, 0)), piece(0), piece(1), piece(2)]
    operands = [my_qc, chip, recv, recv, recv]
    aliases = {}
    if stacked is not None:
        in_specs.append(pl.BlockSpec(memory_space=pl.ANY))
        operands.append(stacked)
        aliases = {len(operands) - 1: 0}
    grid_spec = pltpu.PrefetchScalarGridSpec(
        num_scalar_prefetch=1, grid=(r // tr,), in_specs=in_specs,
        out_specs=pl.BlockSpec((None, None, tr, cdim), lambda i, qc_ref: (layer, qc_ref[1], i, 0)))
    return pl.pallas_call(body, out_shape=jax.ShapeDtypeStruct((n_layers, 2, r, cdim), F32), grid_spec=grid_spec,
                          input_output_aliases=aliases, name=name, compiler_params=_params("parallel"))(*operands)


def _sum_devices(parts, dmod, *, name):
    def body(p_ref, m_ref, o_ref, b_ref):
        acc = p_ref[0]
        for k in range(1, N_DEV):
            acc = acc + p_ref[k]
        o_ref[...] = acc
        tot = None
        for k in range(N_DEV):
            for e in range(dmod.shape[2]):
                tot = m_ref[k, :, e] if tot is None else tot + m_ref[k, :, e]
        b_ref[...] = tot

    out_shape = (jax.ShapeDtypeStruct(parts.shape[1:], F32),
                 jax.ShapeDtypeStruct((dmod.shape[1],) + dmod.shape[3:], F32))
    return pl.pallas_call(body, out_shape=out_shape, in_specs=[VMEM_SPEC, VMEM_SPEC], out_specs=(VMEM_SPEC, VMEM_SPEC),
                          name=name, compiler_params=_params())(parts, dmod)


def _adamw(w, g, m, v, *, name):
    r, cdim = w.shape
    tr = _row_tile(r, cdim * 4, 1 << 19)

    def body(w_ref, g_ref, m_ref, v_ref, d_ref, nm_ref, nv_ref):
        gv = g_ref[...]
        nm = ADAM_B1 * m_ref[...] + (1.0 - ADAM_B1) * gv
        nv = ADAM_B2 * v_ref[...] + (1.0 - ADAM_B2) * (gv * gv)
        m_hat = nm / (1.0 - ADAM_B1 ** ADAM_STEP)
        v_hat = nv / (1.0 - ADAM_B2 ** ADAM_STEP)
        d_ref[...] = -ADAM_LR * (m_hat / (jnp.sqrt(v_hat) + ADAM_EPS) + ADAM_WD * w_ref[...])
        nm_ref[...] = nm
        nv_ref[...] = nv

    row = pl.BlockSpec((tr, cdim), lambda i: (i, 0))
    shape = jax.ShapeDtypeStruct((r, cdim), F32)
    return pl.pallas_call(body, out_shape=(shape, shape, shape), grid=(r // tr,), in_specs=[row] * 4,
                          out_specs=(row, row, row), name=name, compiler_params=_params("parallel"))(w, g, m, v)


def _all_gather_small(block, *, name):
    m_per, n = block.shape

    def body(x_ref, out_ref, send_sems, recv_sems, local_sem):
        x, y, c, _ = _position()
        me, sibling = (x, y, c), (x, y, 1 - c)
        chips = [_peer(k, x, y, c)[:2] for k in CHIP_KINDS]

        def rows(px, py, pc):
            return out_ref.at[pl.ds((4 * px + 2 * py + pc) * m_per, m_per), :]

        def copy(k, blk, to, src=None):
            return pltpu.make_async_remote_copy(src_ref=rows(*blk) if src is None else src, dst_ref=rows(*blk),
                                                send_sem=send_sems.at[k], recv_sem=recv_sems.at[k],
                                                device_id=to, device_id_type=MESH)

        mine = pltpu.make_async_copy(x_ref, rows(*me), local_sem)
        mine.start()
        first = [copy(0, me, sibling, src=x_ref)]
        first += [copy(1 + j, me, (*chip, c), src=x_ref) for j, chip in enumerate(chips)]
        for cp in first:
            cp.start()
        passed = [copy(4 + j, (*chip, c), sibling) for j, chip in enumerate(chips)]
        for j, chip in enumerate(chips):
            copy(1 + j, (*chip, c), me).wait_recv()
            passed[j].start()
        copy(0, sibling, me).wait_recv()
        for j, chip in enumerate(chips):
            copy(4 + j, (*chip, 1 - c), me).wait_recv()
        for cp in first + passed:
            cp.wait_send()
        mine.wait()

    return pl.pallas_call(body, out_shape=jax.ShapeDtypeStruct((N_DEV * m_per, n), block.dtype),
                          in_specs=[VMEM_SPEC], out_specs=VMEM_SPEC,
                          scratch_shapes=[pltpu.SemaphoreType.DMA((7,)), pltpu.SemaphoreType.DMA((7,)),
                                          pltpu.SemaphoreType.DMA],
                          name=name, compiler_params=_params())(block)


def _pack(arrays, width):
    flat = jnp.concatenate([a.reshape(-1).astype(F32) for a in arrays])
    rows = -(-flat.shape[0] // width)
    rows = -(-rows // SUBLANES) * SUBLANES
    return jnp.pad(flat, (0, rows * width - flat.shape[0])).reshape(rows, width)


def _unpack(packed, shapes):
    flat, out, off = packed.reshape(-1), [], 0
    for shp in shapes:
        size = 1
        for dim in shp:
            size *= dim
        out.append(flat[off:off + size].reshape(shp))
        off += size
    return out


def _adamw_packed(ws, gs, ms, vs, width, *, name):
    shapes = [w.shape for w in ws]
    res = _adamw(_pack(ws, width), _pack(gs, width), _pack(ms, width), _pack(vs, width), name=name)
    return [_unpack(r, shapes) for r in res]


MB = float(1 << 20)


def _nbytes(shape, dtype):
    size = jnp.dtype(dtype).itemsize
    for dim in shape:
        size *= dim
    return size


def _gather_plans(shard, my_q, on_ready):
    r, cdim = shard.shape
    half = r // 2
    buf = lax.dynamic_update_slice(jnp.zeros((N_CHIPS, r, cdim), shard.dtype), shard[None], (my_q, 0, 0))
    rows = lambda ref, q, h: ref.at[pl.ds(q, 1), pl.ds(h * half, half)]
    there = lambda kind, pos: _peer_shard(kind, pos[0], pos[1])
    cost = 3 * _nbytes((half, cdim), shard.dtype) / MB

    def forward(outs):
        copies = [(lambda ins, outs, pos, kind=kind: rows(ins[0], there(kind, pos), pos[2]),
                   lambda ins, outs, pos, kind=kind: rows(outs[0], there(kind, pos), pos[2]),
                   lambda ins, outs, pos, kind=kind: rows(outs[0], there(kind, pos), 1 - pos[2]), "c") for kind in CHIP_KINDS]
        return _Plan("d2d", cost, [outs[0]], [jax.ShapeDtypeStruct(buf.shape, buf.dtype)], copies, {0: 0},
                     lambda done: on_ready(done[0]))

    copies = [(lambda ins, outs, pos: ins[0].at[:, pl.ds(pos[2] * half, half)],
               lambda ins, outs, pos: rows(outs[0], pos[3], pos[2]),
               lambda ins, outs, pos, kind=kind: rows(outs[0], there(kind, pos), pos[2]), kind) for kind in CHIP_KINDS]
    return _Plan("ici", cost, [shard[None], buf], [jax.ShapeDtypeStruct(buf.shape, buf.dtype)], copies, {1: 0}), forward


class _Exchanges:
    def __init__(self):
        self.queue = []

    def add(self, plan, front=False):
        if front:
            self.queue.insert(0, plan)
        else:
            self.queue.append(plan)

    def take(self, budget_mb):
        chosen, spent = [p for p in self.queue if p.link == "d2d"], 0.0
        for p in self.queue:
            if p.link == "ici" and (spent == 0.0 or spent + p.cost <= budget_mb):
                chosen.append(p)
                spent += p.cost
        if not chosen:
            return None
        self.queue = [p for p in self.queue if all(p is not ch for ch in chosen)]
        return _merge(chosen)

    def flush(self, budget_mb, until=lambda: False):
        while self.queue and not until():
            _exchange(self.take(budget_mb), name="exchange")


def kernel(x, c, w_mod, b_mod, norm_mix, norm_mlp, w_qkv, b_qkv, w_o, b_o, sinks, w_pw1, b_pw1, w_dw, b_dw, conv_ln_g, conv_ln_b, w_pw2, b_pw2, w_up, w_down, final_norm, loss_target, m_w_mod, m_b_mod, m_norm_mix, m_norm_mlp, m_w_qkv, m_b_qkv, m_w_o, m_b_o, m_sinks, m_w_pw1, m_b_pw1, m_w_dw, m_b_dw, m_conv_ln_g, m_conv_ln_b, m_w_pw2, m_b_pw2, m_w_up, m_w_down, m_final_norm, v_w_mod, v_b_mod, v_norm_mix, v_norm_mlp, v_w_qkv, v_b_qkv, v_w_o, v_b_o, v_sinks, v_w_pw1, v_b_pw1, v_w_dw, v_b_dw, v_conv_ln_g, v_conv_ln_b, v_w_pw2, v_b_pw2, v_w_up, v_w_down, v_final_norm):
    bsz, s, d = x.shape
    t = bsz * s
    depth = w_mod.shape[0]
    n_attn, n_conv = w_qkv.shape[0], w_pw1.shape[0]
    qkv_dim = d + 2 * N_KV_HEADS * HEAD_DIM
    mx, my, mc, mq = _position()
    me = 4 * mx + 2 * my + mc
    my_c = jnp.reshape(mc, (1,)).astype(jnp.int32)
    my_qc = jnp.stack([mq, mc]).astype(jnp.int32)
    pending = _Exchanges()
    SMALL, LARGE = 3.0, 6.5

    weights = {}
    order = []
    for i in range(depth):
        j = i // 2
        order += ([(("qkv", j), w_qkv[j], True), (("o", j), w_o[j], False)] if i % 2 == 0 else
                  [(("pw1", j), w_pw1[j], True), (("pw2", j), w_pw2[j], False)])
        order += [(("up", i), w_up[i], True), (("down", i), w_down[i], False)]
    for key, shard, by_cols in order:
        def ready(buf, key=key, by_cols=by_cols):
            weights[key] = jnp.transpose(buf, (1, 0, 2)).reshape(buf.shape[1], -1) if by_cols else buf.reshape(-1, buf.shape[2])
        ici, forward = _gather_plans(shard.astype(BF16), mq, ready)
        ici.then = lambda outs, forward=forward: pending.add(forward(outs), front=True)
        pending.add(ici)

    def weight(key):
        pending.flush(SMALL, until=lambda: key in weights)
        return weights[key]

    small_sharded = [b_pw1, w_dw, b_dw, conv_ln_g, conv_ln_b, b_pw2]
    c_pad = jnp.pad(c, ((0, SUBLANES - bsz), (0, 0)))
    gathered = _all_gather_small(jnp.concatenate([c_pad, _pack(small_sharded, d)], axis=0), name="gather_c")
    gathered = gathered.reshape(N_DEV, -1, d)
    c_all = gathered[:, :bsz].reshape(N_DEV * bsz, d)
    per_chip = [_unpack(gathered[2 * q, SUBLANES:], [a.shape for a in small_sharded]) for q in range(N_CHIPS)]
    b_pw1_f, w_dw_f, b_dw_f, ln_g_f, ln_b_f, b_pw2_f = [jnp.concatenate([per_chip[q][k] for q in range(N_CHIPS)], axis=-1)
                                                           for k in range(len(small_sharded))]
    w_dw_f = jnp.pad(w_dw_f, ((0, 0), (0, CONV_HALO - CONV_WIDTH), (0, 0)))

    n_mod = w_mod.shape[2]
    b_mod_cols = lax.dynamic_slice_in_dim(b_mod, mq * n_mod, n_mod, axis=1).reshape(depth, 1, n_mod)
    mod_part = _mod_fwd(c_all, w_mod, b_mod_cols, name="mod_fwd")
    mod_all = _all_gather_small(mod_part.reshape(depth * N_DEV * bsz, n_mod), name="gather_mod")
    mod_all = mod_all.reshape(N_DEV, depth, N_DEV * bsz, n_mod)[0::2]
    mod = lax.dynamic_slice_in_dim(mod_all, me * bsz, bsz, axis=2)
    mod = jnp.transpose(mod, (1, 2, 0, 3)).reshape(depth, bsz, N_MOD, 1, d)
    mods = [[mod[i][:, k] for k in range(N_MOD)] for i in range(depth)]

    saved = []
    xc = x
    for i in range(depth):
        j = i // 2
        sh1, sc1, g1, sh2, sc2, g2 = mods[i]
        h1 = _normmod(xc, norm_mix[i][None], sc1, sh1, name="normmod")
        if i % 2 == 0:
            wq, wo = weight(("qkv", j)), weight(("o", j))
            qkv = _mm(h1.reshape(t, d), wq, bias=b_qkv[j], tn=qkv_dim, name="mm_qkv",
                      comm=pending.take(SMALL)).reshape(bsz, s, qkv_dim)
            mix = _attn_fwd(qkv, sinks[j], name="attn_fwd")
            y1, x1 = _mm(mix.reshape(t, d), wo, bias=b_o[j], epi="resid", resid=xc.reshape(t, d), gate=g1, seq=s,
                         tn=d, name="mm_out", comm=pending.take(SMALL))
            extra = (qkv, mix)
        else:
            wp1, wp2 = weight(("pw1", j)), weight(("pw2", j))
            u = _mm(h1.reshape(t, d), wp1, bias=b_pw1_f[j], out_dtype=F32, tn=d, name="mm_pw1",
                    comm=pending.take(SMALL)).reshape(bsz, s, 2 * d)
            mix, conv_v = _conv_fwd(u, w_dw_f[j], b_dw_f[j][None], ln_g_f[j][None], ln_b_f[j][None], name="conv_fwd")
            y1, x1 = _mm(mix.reshape(t, d), wp2, bias=b_pw2_f[j], epi="resid", resid=xc.reshape(t, d), gate=g1, seq=s,
                         tn=d, name="mm_out", comm=pending.take(SMALL))
            extra = (u, mix, conv_v)
        x1 = x1.reshape(bsz, s, d)
        h2 = _normmod(x1, norm_mlp[i][None], sc2, sh2, name="normmod")
        act = _mm(h2.reshape(t, d), weight(("up", i)), epi="relu2", tn=d, name="mm_up", comm=pending.take(LARGE))
        y2, x2 = _mm(act, weight(("down", i)), epi="resid", resid=x1.reshape(t, d), gate=g2, seq=s, name="mm_down",
                     comm=pending.take(LARGE))
        saved.append((xc, h1, extra, y1, x1, h2, act, y2))
        xc = x2.reshape(bsz, s, d)
    pending.flush(LARGE)

    dx, loss_cols, d_final = _loss_head(xc, loss_target, final_norm[None], name="loss_head")
    loss = lax.psum(0.5 / d * jnp.sum(loss_cols), ("x", "y", "c"))

    totals = {}

    def reduce_scatter(name, layer, n_layers, grad):
        view = grad.reshape(N_CHIPS, 2, grad.shape[1] // 2, grad.shape[2])
        half_shape = (N_CHIPS, 1) + view.shape[2:]

        def scatter(outs):
            chip = _add_half(view, outs[0], my_c, name="rs_chipsum")
            copies = [(lambda ins, outs, pos, kind=kind: ins[0].at[pl.ds(_peer_shard(kind, pos[0], pos[1]), 1)],
                       lambda ins, outs, pos, k=k: outs[0].at[pl.ds(k, 1)],
                       lambda ins, outs, pos, k=k: outs[0].at[pl.ds(k, 1)], kind) for k, kind in enumerate(CHIP_KINDS)]

            def total(outs):
                totals[name] = _add_pieces(chip, outs[0], my_qc, totals.get(name), layer, n_layers, name="rs_total")

            pending.add(_Plan("ici", 3 * _nbytes(chip.shape[1:], BF16) / MB, [chip],
                              [jax.ShapeDtypeStruct((3,) + chip.shape[1:], BF16)], copies, then=total))

        pending.add(_Plan("d2d", _nbytes(half_shape, F32) / MB, [view], [jax.ShapeDtypeStruct(half_shape, F32)],
                          [(lambda ins, outs, pos: ins[0].at[:, pl.ds(1 - pos[2], 1)],
                            lambda ins, outs, pos: outs[0], lambda ins, outs, pos: outs[0], "c")], then=scatter))

    dmods, small = [None] * depth, {}
    for i in reversed(range(depth)):
        j = i // 2
        xin, h1, extra, y1, x1, h2, act, y2 = saved[i]
        sh1, sc1, g1, sh2, sc2, g2 = mods[i]
        dyb, dg2, _ = _gate_bwd(dx, y2.reshape(bsz, s, d), g2, name="gate_bwd")
        dyb = dyb.reshape(t, d)
        reduce_scatter("down", i, depth,
                       _mm_tn(act, dyb, name="dw_down", comm=pending.take(LARGE)).reshape(N_CHIPS, -1, d))
        dup = _mm(dyb, weights["down", i], nt=True, epi="dact", act=act, tn=d, name="mm_dact", comm=pending.take(LARGE))
        reduce_scatter("up", i, depth,
                       _mm_tn(h2.reshape(t, d), dup, col_shards=N_CHIPS, name="dw_up", comm=pending.take(LARGE)))
        dh2 = _mm(dup, weights["up", i], nt=True, out_dtype=F32, name="mm_dh2", comm=pending.take(LARGE)).reshape(bsz, s, d)
        dx1, p2, dsh2 = _normmod_bwd(x1, dh2, dx, norm_mlp[i][None], sc2, name="normmod_bwd")
        dyb, dg1, sdx = _gate_bwd(dx1, y1.reshape(bsz, s, d), g1, name="gate_bwd")
        dyb = dyb.reshape(t, d)
        d_bias_out = jnp.sum(g1 * sdx, axis=(0, 1))
        if i % 2 == 0:
            qkv, mix = extra
            small["b_o", j] = d_bias_out
            reduce_scatter("o", j, n_attn,
                           _mm_tn(mix.reshape(t, d), dyb, name="dw_sq", comm=pending.take(SMALL)).reshape(N_CHIPS, -1, d))
            dmix = _mm(dyb, weights["o", j], nt=True, tn=d, name="mm_dmix", comm=pending.take(SMALL)).reshape(bsz, s, d)
            dqkv, d_bqkv, d_sink = _attn_bwd(qkv, dmix, sinks[j], name="attn_bwd")
            small["b_qkv", j], small["sinks", j] = d_bqkv[0], d_sink[:, 0]
            dqkv = dqkv.reshape(t, qkv_dim)
            dwq = _mm_tn(h1.reshape(t, d), dqkv, tn=qkv_dim, name="dw_qkv", comm=pending.take(SMALL))
            reduce_scatter("qkv", j, n_attn, jnp.transpose(dwq.reshape(d, N_CHIPS, -1), (1, 0, 2)))
            dh1 = _mm(dqkv, weights["qkv", j], nt=True, out_dtype=F32, tn=d, name="mm_dh1a", comm=pending.take(SMALL))
        else:
            u, mix, conv_v = extra
            small["b_pw2", j] = d_bias_out
            reduce_scatter("pw2", j, n_conv,
                           _mm_tn(mix.reshape(t, d), dyb, name="dw_sq", comm=pending.take(SMALL)).reshape(N_CHIPS, -1, d))
            dz = _mm(dyb, weights["pw2", j], nt=True, out_dtype=F32, tn=d, name="mm_dz", comm=pending.take(SMALL)).reshape(bsz, s, d)
            dv, d_lng, d_lnb, d_bdw = _conv_bwd_ln(dz, conv_v, ln_g_f[j][None], ln_b_f[j][None], name="conv_bwd_ln")
            du, d_bpw1, d_wdw = _conv_bwd_taps(dv, u, w_dw_f[j], name="conv_bwd_taps")
            small["ln_g", j], small["ln_b", j], small["b_dw", j] = d_lng[0], d_lnb[0], d_bdw[0]
            small["b_pw1", j], small["w_dw", j] = d_bpw1[0], jnp.sum(d_wdw[:CONV_WIDTH], axis=1)
            du = du.reshape(t, 2 * d)
            reduce_scatter("pw1", j, n_conv,
                           _mm_tn(h1.reshape(t, d), du, col_shards=N_CHIPS, name="dw_pw1", comm=pending.take(SMALL)))
            dh1 = _mm(du, weights["pw1", j], nt=True, out_dtype=F32, tn=d, name="mm_dh1c", comm=pending.take(SMALL))
        dx, p1, dsh1 = _normmod_bwd(xin, dh1.reshape(bsz, s, d), dx1, norm_mix[i][None], sc1, name="normmod_bwd")
        small["norm_mix", i] = jnp.sum((1.0 + sc1) * p1, axis=(0, 1))
        small["norm_mlp", i] = jnp.sum((1.0 + sc2) * p2, axis=(0, 1))
        dmods[i] = jnp.concatenate([dsh1, norm_mix[i] * p1, dg1, dsh2, norm_mlp[i] * p2, dg2], axis=1)
    grad_x = dx

    small_names = ([("norm_mix", i) for i in range(depth)] + [("norm_mlp", i) for i in range(depth)]
                   + [(nm, j) for nm in ("b_qkv", "b_o", "sinks") for j in range(n_attn)]
                   + [(nm, j) for nm in ("b_pw1", "w_dw", "b_dw", "ln_g", "ln_b", "b_pw2") for j in range(n_conv)])
    small_list = [small[k] for k in small_names] + [d_final[0]]
    small_pack = _pack(small_list, d)
    dmod_rows = jnp.stack(dmods).reshape(depth * bsz * N_MOD, d)
    n_small = small_pack.shape[0]
    gathered = _all_gather_small(jnp.concatenate([small_pack, _pack([dmod_rows], d)], axis=0), name="gather_small")
    gathered = gathered.reshape(N_DEV, -1, d)
    dmod_all = gathered[:, n_small:n_small + depth * bsz * N_MOD].reshape(N_DEV, depth, bsz, N_MOD, d)
    small_sum, g_b_mod = _sum_devices(gathered[:, :n_small], dmod_all, name="sum_devices")
    small_tot = dict(zip(small_names + ["final_norm"], _unpack(small_sum, [a.shape for a in small_list])))
    stacked = lambda nm, count: jnp.stack([small_tot[nm, k] for k in range(count)])
    g_norm_mix, g_norm_mlp = stacked("norm_mix", depth), stacked("norm_mlp", depth)
    g_b_qkv, g_b_o, g_sinks = stacked("b_qkv", n_attn), stacked("b_o", n_attn), stacked("sinks", n_attn)
    g_final = small_tot["final_norm"]
    g_b_mod = g_b_mod.reshape(depth, N_MOD * d)
    shard_cols = lambda g: lax.dynamic_slice_in_dim(g, mq * (g.shape[-1] // N_CHIPS), g.shape[-1] // N_CHIPS, axis=g.ndim - 1)
    g_b_pw1, g_w_dw, g_b_dw, g_ln_g, g_ln_b, g_b_pw2 = [shard_cols(stacked(nm, n_conv))
                                                        for nm in ("b_pw1", "w_dw", "b_dw", "ln_g", "ln_b", "b_pw2")]

    dmod_cols = jnp.transpose(dmod_all, (1, 0, 2, 3, 4)).reshape(depth, N_DEV * bsz, N_MOD * d)
    dmod_cols = lax.dynamic_slice_in_dim(dmod_cols, mq * n_mod, n_mod, axis=2)
    g_w_mod = _mod_bwd(c_all, dmod_cols, name="mod_bwd")

    pending.flush(LARGE)
    names = ["qkv", "o", "pw1", "pw2", "up", "down"]
    bufs = [totals[nm] for nm in names]
    copies = []
    for a, buf in enumerate(bufs):
        for layer in range(buf.shape[0]):
            half = lambda ref, h, layer=layer: ref.at[pl.ds(layer, 1), pl.ds(h, 1)]
            copies.append((lambda ins, outs, pos, a=a, half=half: half(ins[a], pos[2]),
                           lambda ins, outs, pos, a=a, half=half: half(outs[a], pos[2]),
                           lambda ins, outs, pos, a=a, half=half: half(outs[a], 1 - pos[2]), "c"))
    shared = {}
    _exchange(_Plan("d2d", 0.0, bufs, [jax.ShapeDtypeStruct(b.shape, F32) for b in bufs], copies,
                    {a: a for a in range(len(bufs))}, lambda outs: shared.update(zip(names, outs))), name="rs_share")
    g_w_qkv, g_w_o, g_w_pw1, g_w_pw2, g_w_up, g_w_down = [
        shared[nm].reshape(shared[nm].shape[0], -1, shared[nm].shape[3]) for nm in names]

    def adam(w, g, m, v, name):
        two_d = lambda a: a.reshape(-1, a.shape[-1])
        return [r.reshape(w.shape) for r in _adamw(two_d(w), two_d(g), two_d(m), two_d(v), name=name)]

    results = {}
    for nm, w, g, m, v in (("w_mod", w_mod, g_w_mod, m_w_mod, v_w_mod), ("w_qkv", w_qkv, g_w_qkv, m_w_qkv, v_w_qkv),
                           ("w_o", w_o, g_w_o, m_w_o, v_w_o), ("w_pw1", w_pw1, g_w_pw1, m_w_pw1, v_w_pw1),
                           ("w_pw2", w_pw2, g_w_pw2, m_w_pw2, v_w_pw2), ("w_up", w_up, g_w_up, m_w_up, v_w_up),
                           ("w_down", w_down, g_w_down, m_w_down, v_w_down)):
        results[nm] = (g,) + tuple(adam(w, g, m, v, "adamw"))
    small_w = dict(b_mod=(b_mod, g_b_mod, m_b_mod, v_b_mod), norm_mix=(norm_mix, g_norm_mix, m_norm_mix, v_norm_mix),
                   norm_mlp=(norm_mlp, g_norm_mlp, m_norm_mlp, v_norm_mlp), b_qkv=(b_qkv, g_b_qkv, m_b_qkv, v_b_qkv),
                   b_o=(b_o, g_b_o, m_b_o, v_b_o), sinks=(sinks, g_sinks, m_sinks, v_sinks),
                   b_pw1=(b_pw1, g_b_pw1, m_b_pw1, v_b_pw1), w_dw=(w_dw, g_w_dw, m_w_dw, v_w_dw),
                   b_dw=(b_dw, g_b_dw, m_b_dw, v_b_dw), conv_ln_g=(conv_ln_g, g_ln_g, m_conv_ln_g, v_conv_ln_g),
                   conv_ln_b=(conv_ln_b, g_ln_b, m_conv_ln_b, v_conv_ln_b), b_pw2=(b_pw2, g_b_pw2, m_b_pw2, v_b_pw2),
                   final_norm=(final_norm, g_final, m_final_norm, v_final_norm))
    names = list(small_w)
    deltas, new_ms, new_vs = _adamw_packed(*[[small_w[nm][k] for nm in names] for k in range(4)], d, name="adamw_small")
    for k, nm in enumerate(names):
        results[nm] = (small_w[nm][1], deltas[k], new_ms[k], new_vs[k])

    weight_order = ["w_mod", "b_mod", "norm_mix", "norm_mlp", "w_qkv", "b_qkv", "w_o", "b_o", "sinks", "w_pw1", "b_pw1",
                    "w_dw", "b_dw", "conv_ln_g", "conv_ln_b", "w_pw2", "b_pw2", "w_up", "w_down", "final_norm"]
    return (loss, grad_x, *[results[nm][0] for nm in weight_order], *[results[nm][1] for nm in weight_order],
            *[results[nm][2] for nm in weight_order], *[results[nm][3] for nm in weight_order])
```

```python
import functools

import jax
import jax.numpy as jnp
from jax import lax
from jax.experimental import pallas as pl
from jax.experimental.pallas import tpu as pltpu

F32, BF16 = jnp.float32, jnp.bfloat16
MESH = pl.DeviceIdType.MESH
N_CHIPS = 4
N_DEV = 8
LANES = 128
SUBLANES = 8
VMEM_LIMIT = 48 * 1024 * 1024

NORM_EPS = 1e-6
HEAD_DIM = 64
N_KV_HEADS = 2
WINDOW = 128
CONV_WIDTH = 31
CONV_HALO = 32
CONV_ROWS = 32
N_MOD = 6

ADAM_LR, ADAM_B1, ADAM_B2, ADAM_EPS, ADAM_WD, ADAM_STEP = 0.001, 0.9, 0.999, 1e-08, 0.01, 10

HBM_SPEC = pl.BlockSpec(memory_space=pltpu.HBM)
VMEM_SPEC = pl.BlockSpec(memory_space=pltpu.VMEM)
SMEM_SPEC = pl.BlockSpec(memory_space=pltpu.SMEM)


def _params(*sem):
    return pltpu.CompilerParams(dimension_semantics=sem or None, vmem_limit_bytes=VMEM_LIMIT)


def _row_tile(rows, width_bytes, target=1 << 20):
    t = rows
    while t % 2 == 0 and t > SUBLANES and t * width_bytes > target:
        t //= 2
    return t


CHIP_KINDS = ("x", "y", "xy")


def _position():
    x, y, c = lax.axis_index("x"), lax.axis_index("y"), lax.axis_index("c")
    return x, y, c, 2 * x + y


def _peer(kind, x, y, c):
    return {"c": (x, y, 1 - c), "x": (1 - x, y, c), "y": (x, 1 - y, c), "xy": (1 - x, 1 - y, c)}[kind]


def _peer_shard(kind, x, y):
    px, py, _ = _peer(kind, x, y, 0)
    return 2 * px + py


class _Plan:
    def __init__(self, link, cost, operands, out_shapes, copies, aliases=None, then=None):
        self.link, self.cost = link, cost
        self.operands, self.out_shapes, self.copies = list(operands), list(out_shapes), list(copies)
        self.aliases, self.then = dict(aliases or {}), then


def _merge(plans):
    operands, out_shapes, copies, aliases, thens = [], [], [], {}, []
    for p in plans:
        i0, o0 = len(operands), len(out_shapes)
        i1, o1 = i0 + len(p.operands), o0 + len(p.out_shapes)

        def shifted(f, i0=i0, i1=i1, o0=o0, o1=o1):
            return lambda ins, outs, pos: f(ins[i0:i1], outs[o0:o1], pos)

        copies += [(shifted(src), shifted(dst), shifted(land), kind) for src, dst, land, kind in p.copies]
        aliases.update({i0 + k: o0 + v for k, v in p.aliases.items()})
        operands += p.operands
        out_shapes += p.out_shapes
        thens.append((p.then, o0, o1))

    def then(outs):
        for f, o0, o1 in thens:
            if f is not None:
                f(outs[o0:o1])

    return _Plan("mixed", sum(p.cost for p in plans), operands, out_shapes, copies, aliases, then)


def _call(body, *, name, out_shape, operands, grid=(), in_specs=(), out_specs=(), scratch_shapes=(), sem=(), comm=None):
    single = not isinstance(out_shape, (tuple, list))
    out_shape = [out_shape] if single else list(out_shape)
    out_specs = [out_specs] if single else list(out_specs)
    if comm is None:
        res = pl.pallas_call(body, out_shape=out_shape, grid=grid, in_specs=list(in_specs), out_specs=out_specs,
                             scratch_shapes=list(scratch_shapes), name=name, compiler_params=_params(*sem))(*operands)
        return res[0] if single else res
    n_in, n_out, n_scr = len(operands), len(out_shape), len(scratch_shapes)
    c_in, c_out, n_cp = len(comm.operands), len(comm.out_shapes), len(comm.copies)

    def wrapped(*refs):
        ins, refs = refs[:n_in], refs[n_in:]
        cins, refs = refs[:c_in], refs[c_in:]
        outs, refs = refs[:n_out], refs[n_out:]
        couts, refs = refs[:c_out], refs[c_out:]
        scr, (send_sems, recv_sems) = refs[:n_scr], refs[n_scr:]

        def descriptors():
            pos = _position()
            sends, lands = [], []
            for k, (src, dst, landing, kind) in enumerate(comm.copies):
                common = dict(send_sem=send_sems.at[k], recv_sem=recv_sems.at[k],
                              device_id=_peer(kind, *pos[:3]), device_id_type=MESH)
                sends.append(pltpu.make_async_remote_copy(src_ref=src(cins, couts, pos), dst_ref=dst(cins, couts, pos), **common))
                lands.append(pltpu.make_async_remote_copy(src_ref=src(cins, couts, pos), dst_ref=landing(cins, couts, pos), **common))
            return sends, lands

        def start():
            for cp in descriptors()[0]:
                cp.start()

        def finish():
            sends, lands = descriptors()
            for cp in lands:
                cp.wait_recv()
            for cp in sends:
                cp.wait_send()

        if not grid:
            start()
            body(*ins, *outs, *scr)
            finish()
        else:
            ids = [pl.program_id(ax) for ax in range(len(grid))]
            first, last = ids[0] == 0, ids[0] == grid[0] - 1
            for ax in range(1, len(grid)):
                first, last = first & (ids[ax] == 0), last & (ids[ax] == grid[ax] - 1)
            pl.when(first)(start)
            body(*ins, *outs, *scr)
            pl.when(last)(finish)

    res = pl.pallas_call(wrapped, out_shape=out_shape + comm.out_shapes, grid=grid,
                         in_specs=list(in_specs) + [HBM_SPEC] * c_in, out_specs=out_specs + [HBM_SPEC] * c_out,
                         scratch_shapes=list(scratch_shapes) + [pltpu.SemaphoreType.DMA((n_cp,)), pltpu.SemaphoreType.DMA((n_cp,))],
                         input_output_aliases={n_in + k: n_out + v for k, v in comm.aliases.items()},
                         name=name, compiler_params=_params(*["arbitrary"] * len(grid)))(*operands, *comm.operands)
    if comm.then is not None:
        comm.then(res[n_out:])
    return res[0] if single else res[:n_out]


def _exchange(plan, *, name):
    _call(lambda: None, name=name, out_shape=[], operands=[], comm=plan)


def _mm(a, b, *, name, nt=False, tm=1024, tn=512, epi="plain", out_dtype=BF16,
        bias=None, act=None, resid=None, gate=None, seq=None, comm=None):
    m, k = a.shape
    n = b.shape[0] if nt else b.shape[1]
    tm, tn = min(tm, m, seq or m), min(tn, n)
    assert m % tm == 0 and n % tn == 0
    dims = (((1,), (1,)), ((), ())) if nt else (((1,), (0,)), ((), ()))
    tile = pl.BlockSpec((tm, tn), lambda j, i: (i, j))
    operands = [a, b]
    in_specs = [pl.BlockSpec((tm, k), lambda j, i: (i, 0)),
                pl.BlockSpec((tn, k), lambda j, i: (j, 0)) if nt else pl.BlockSpec((k, tn), lambda j, i: (0, j))]
    if bias is not None:
        operands.append(bias.reshape(1, n))
        in_specs.append(pl.BlockSpec((1, tn), lambda j, i: (0, j)))
    if epi == "dact":
        operands.append(act)
        in_specs.append(tile)
    if epi == "resid":
        assert seq % tm == 0
        operands += [resid, gate]
        in_specs += [tile, pl.BlockSpec((None, 1, tn), lambda j, i: (i * tm // seq, 0, j))]
        out_shape = (jax.ShapeDtypeStruct((m, n), BF16), jax.ShapeDtypeStruct((m, n), F32))
        out_specs = (tile, tile)
    else:
        out_shape = jax.ShapeDtypeStruct((m, n), out_dtype)
        out_specs = tile

    def body(*refs):
        it = iter(refs)
        a_ref, b_ref = next(it), next(it)
        acc = lax.dot_general(a_ref[...], b_ref[...], dims, preferred_element_type=F32)
        if bias is not None:
            acc = acc + next(it)[...]
        if epi == "plain":
            next(it)[...] = acc.astype(out_dtype)
        elif epi == "relu2":
            r = jnp.maximum(acc, 0.0)
            next(it)[...] = (r * r).astype(out_dtype)
        elif epi == "dact":
            act_ref = next(it)
            next(it)[...] = (acc * (2.0 * jnp.sqrt(act_ref[...].astype(F32)))).astype(out_dtype)
        else:
            resid_ref, gate_ref = next(it), next(it)
            y_ref, x_ref = next(it), next(it)
            y_ref[...] = acc.astype(BF16)
            x_ref[...] = resid_ref[...] + gate_ref[...] * acc

    return _call(body, name=name, out_shape=out_shape, operands=operands, grid=(n // tn, m // tm), in_specs=in_specs,
                 out_specs=out_specs, sem=("parallel", "parallel"), comm=comm)


def _mm_tn(a, b, *, name, tm=1024, tn=1024, tk=2048, col_shards=None, comm=None):
    t, m = a.shape
    n = b.shape[1]
    tm, tk = min(tm, m), min(tk, t)
    if col_shards is None:
        tn = min(tn, n)
        out_shape = jax.ShapeDtypeStruct((m, n), F32)
        out_spec = pl.BlockSpec((tm, tn), lambda i, j, k: (i, j))
    else:
        per = n // col_shards
        tn = min(tn, per)
        assert per % tn == 0
        out_shape = jax.ShapeDtypeStruct((col_shards, m, per), F32)
        out_spec = pl.BlockSpec((None, tm, tn), lambda i, j, k: (j // (per // tn), i, j % (per // tn)))
    assert m % tm == 0 and n % tn == 0 and t % tk == 0

    def body(a_ref, b_ref, o_ref):
        @pl.when(pl.program_id(2) == 0)
        def _():
            o_ref[...] = jnp.zeros_like(o_ref)

        o_ref[...] += lax.dot_general(a_ref[...], b_ref[...], (((0,), (0,)), ((), ())),
                                      preferred_element_type=F32)

    return _call(body, name=name, out_shape=out_shape, operands=[a, b], grid=(m // tm, n // tn, t // tk),
                 in_specs=[pl.BlockSpec((tk, tm), lambda i, j, k: (k, i)), pl.BlockSpec((tk, tn), lambda i, j, k: (k, j))],
                 out_specs=out_spec, sem=("parallel", "parallel", "arbitrary"), comm=comm)


def _normmod(x, gamma, sc, sh, *, name):
    bsz, s, d = x.shape
    ts = _row_tile(s, d * 4, 2 << 20)

    def body(x_ref, g_ref, sc_ref, sh_ref, o_ref):
        xf = x_ref[...]
        r = lax.rsqrt(jnp.mean(xf * xf, axis=-1, keepdims=True) + NORM_EPS)
        o_ref[...] = (xf * r * g_ref[...] * (1.0 + sc_ref[...]) + sh_ref[...]).astype(BF16)

    row = pl.BlockSpec((None, ts, d), lambda b, i: (b, i, 0))
    per_ex = pl.BlockSpec((None, 1, d), lambda b, i: (b, 0, 0))
    return pl.pallas_call(body, out_shape=jax.ShapeDtypeStruct(x.shape, BF16), grid=(bsz, s // ts),
                          in_specs=[row, pl.BlockSpec((1, d), lambda b, i: (0, 0)), per_ex, per_ex],
                          out_specs=row, name=name, compiler_params=_params("parallel", "parallel"))(x, gamma, sc, sh)


def _gate_grads(dxv, y_ref, gate_ref, dy_ref, dg_ref, sdx_ref):
    dy_ref[...] = (gate_ref[...] * dxv).astype(BF16)

    @pl.when(pl.program_id(1) == 0)
    def _():
        dg_ref[...] = jnp.zeros_like(dg_ref)
        sdx_ref[...] = jnp.zeros_like(sdx_ref)

    dg_ref[...] += jnp.sum(dxv * y_ref[...].astype(F32), axis=0, keepdims=True)
    sdx_ref[...] += jnp.sum(dxv, axis=0, keepdims=True)


def _normmod_bwd(x, dh, dres, gamma, sc, producer=None, *, name):
    bsz, s, d = x.shape
    ts = _row_tile(s, d * 4, 1 << 20)

    def body(x_ref, dh_ref, dres_ref, g_ref, sc_ref, *rest):
        dx_ref, p_ref, dsh_ref = rest[-6:-3] if producer else rest
        xf = x_ref[...]
        r = lax.rsqrt(jnp.mean(xf * xf, axis=-1, keepdims=True) + NORM_EPS)
        xhat = xf * r
        dh_v = dh_ref[...]
        dxhat = dh_v * (g_ref[...] * (1.0 + sc_ref[...]))
        dxv = dres_ref[...] + r * (dxhat - xhat * jnp.mean(dxhat * xhat, axis=-1, keepdims=True))
        dx_ref[...] = dxv

        @pl.when(pl.program_id(1) == 0)
        def _():
            p_ref[...] = jnp.zeros_like(p_ref)
            dsh_ref[...] = jnp.zeros_like(dsh_ref)

        p_ref[...] += jnp.sum(dh_v * xhat, axis=0, keepdims=True)
        dsh_ref[...] += jnp.sum(dh_v, axis=0, keepdims=True)
        if producer:
            _gate_grads(dxv, rest[0], rest[1], *rest[-3:])

    row = pl.BlockSpec((None, ts, d), lambda b, i: (b, i, 0))
    per_ex = pl.BlockSpec((None, 1, d), lambda b, i: (b, 0, 0))
    vec = jax.ShapeDtypeStruct((bsz, 1, d), F32)
    gate_in, gate_out, gate_specs = (), (), ()
    if producer:
        gate_in, gate_out, gate_specs = (row, per_ex), (jax.ShapeDtypeStruct(x.shape, BF16), vec, vec), (row, per_ex, per_ex)
    return pl.pallas_call(body, out_shape=(jax.ShapeDtypeStruct(x.shape, F32), vec, vec) + gate_out, grid=(bsz, s // ts),
                          in_specs=[row, row, row, pl.BlockSpec((1, d), lambda b, i: (0, 0)), per_ex, *gate_in],
                          out_specs=(row, per_ex, per_ex) + gate_specs, name=name,
                          compiler_params=_params("parallel", "arbitrary"))(x, dh, dres, gamma, sc, *(producer or ()))


def _loss_head(x, target, gamma, producer, *, name):
    bsz, s, d = x.shape
    ts = _row_tile(s, d * 4, 1 << 20)

    def body(x_ref, t_ref, g_ref, y_ref, gate_ref, dx_ref, loss_ref, dg_ref, *gate_refs):
        xf = x_ref[...]
        r = lax.rsqrt(jnp.mean(xf * xf, axis=-1, keepdims=True) + NORM_EPS)
        xhat = xf * r
        err = xhat * g_ref[...] - t_ref[...]
        dy = err * (1.0 / d)
        dxhat = dy * g_ref[...]
        dxv = r * (dxhat - xhat * jnp.mean(dxhat * xhat, axis=-1, keepdims=True))
        dx_ref[...] = dxv

        @pl.when((pl.program_id(0) == 0) & (pl.program_id(1) == 0))
        def _():
            loss_ref[...] = jnp.zeros_like(loss_ref)
            dg_ref[...] = jnp.zeros_like(dg_ref)

        loss_ref[...] += jnp.sum(err * err, axis=0, keepdims=True)
        dg_ref[...] += jnp.sum(dy * xhat, axis=0, keepdims=True)
        _gate_grads(dxv, y_ref, gate_ref, *gate_refs)

    row = pl.BlockSpec((None, ts, d), lambda b, i: (b, i, 0))
    one = pl.BlockSpec((1, d), lambda b, i: (0, 0))
    per_ex = pl.BlockSpec((None, 1, d), lambda b, i: (b, 0, 0))
    vec, ex_vec = jax.ShapeDtypeStruct((1, d), F32), jax.ShapeDtypeStruct((bsz, 1, d), F32)
    return pl.pallas_call(body, out_shape=(jax.ShapeDtypeStruct(x.shape, F32), vec, vec,
                                           jax.ShapeDtypeStruct(x.shape, BF16), ex_vec, ex_vec), grid=(bsz, s // ts),
                          in_specs=[row, row, one, row, per_ex], out_specs=(row, one, one, row, per_ex, per_ex), name=name,
                          compiler_params=_params("arbitrary", "arbitrary"))(x, target, gamma, *producer)


def _alibi_slope(h, n_heads):
    return 2.0 ** (-8.0 * (h + 1) / n_heads)


def _attn_masks(first_block):
    qi = lax.broadcasted_iota(jnp.int32, (WINDOW, 2 * WINDOW), 0)
    ki = lax.broadcasted_iota(jnp.int32, (WINDOW, 2 * WINDOW), 1)
    dist = qi + WINDOW - ki
    first_key = jnp.where(first_block, WINDOW, 0)
    valid = (dist >= 0) & (dist < WINDOW) & (ki >= first_key)
    return dist.astype(F32), valid


def _dup_halves(span, kv, left):
    f = span.astype(F32)
    rolled = pltpu.roll(f, HEAD_DIM, axis=1)
    out = jnp.where(left, f, rolled) if kv == 0 else jnp.where(left, rolled, f)
    return out.astype(BF16)


def _attn_probs(s, h, n_heads, distf, valid, sink):
    s = s * (HEAD_DIM ** -0.5) - _alibi_slope(h, n_heads) * distf
    s = jnp.where(valid, s, -1e30)
    m = jnp.maximum(jnp.max(s, axis=-1, keepdims=True), sink)
    e = jnp.exp(s - m)
    e_sink = jnp.exp(sink - m)
    inv = 1.0 / (jnp.sum(e, axis=-1, keepdims=True) + e_sink)
    return e * inv, e_sink * inv


def _attn_specs(d, n_blocks, clamp):
    kcol = d // LANES
    cur = (lambda i: jnp.minimum(i, n_blocks - 1)) if clamp else (lambda i: i)
    prev = lambda i: jnp.maximum(cur(i) - 1, 0)
    kv = lambda col, blk: pl.BlockSpec((None, WINDOW, LANES), lambda b, i: (b, blk(i), col))
    return [pl.BlockSpec((None, WINDOW, d), lambda b, i: (b, cur(i), 0)),
            kv(kcol, prev), kv(kcol, cur), kv(kcol + 1, prev), kv(kcol + 1, cur)]


def _attn_fwd(qkv, sinks, *, name):
    bsz, s, qkv_dim = qkv.shape
    d = qkv_dim - 2 * N_KV_HEADS * HEAD_DIM
    n_heads = d // HEAD_DIM
    group = n_heads // N_KV_HEADS
    pairs = group // 2
    n_blocks = s // WINDOW

    def body(q_ref, kp_ref, kc_ref, vp_ref, vc_ref, sink_ref, o_ref):
        left = lax.broadcasted_iota(jnp.int32, (1, LANES), 1) < HEAD_DIM
        distf, valid = _attn_masks(pl.program_id(1) == 0)
        kspan = jnp.concatenate([kp_ref[...], kc_ref[...]], axis=0)
        vspan = jnp.concatenate([vp_ref[...], vc_ref[...]], axis=0)
        for kv in range(N_KV_HEADS):
            kdup, vdup = _dup_halves(kspan, kv, left), _dup_halves(vspan, kv, left)
            res = []
            for par in range(2):
                keep = left if par == 0 else jnp.logical_not(left)
                cols = [pl.ds((kv * pairs + p) * LANES, LANES) for p in range(pairs)]
                lhs = jnp.concatenate([jnp.where(keep, q_ref[:, cl], jnp.zeros((), BF16)) for cl in cols], axis=0)
                sc = lax.dot_general(lhs, kdup, (((1,), (1,)), ((), ())), preferred_element_type=F32)
                probs = []
                for p in range(pairs):
                    h = kv * group + 2 * p + par
                    pr, _ = _attn_probs(sc[p * WINDOW:(p + 1) * WINDOW], h, n_heads, distf, valid, sink_ref[h])
                    probs.append(pr.astype(BF16))
                res.append(jnp.dot(jnp.concatenate(probs, axis=0), vdup, preferred_element_type=F32))
            for p in range(pairs):
                rows = slice(p * WINDOW, (p + 1) * WINDOW)
                o_ref[:, pl.ds((kv * pairs + p) * LANES, LANES)] = jnp.where(left, res[0][rows], res[1][rows]).astype(BF16)

    return pl.pallas_call(body, out_shape=jax.ShapeDtypeStruct((bsz, s, d), BF16), grid=(bsz, n_blocks),
                          in_specs=_attn_specs(d, n_blocks, False) + [SMEM_SPEC],
                          out_specs=pl.BlockSpec((None, WINDOW, d), lambda b, i: (b, i, 0)), name=name,
                          compiler_params=_params("parallel", "parallel"))(qkv, qkv, qkv, qkv, qkv, sinks)


def _attn_bwd(qkv, do, sinks, *, name):
    bsz, s, qkv_dim = qkv.shape
    d = qkv_dim - 2 * N_KV_HEADS * HEAD_DIM
    n_heads = d // HEAD_DIM
    group = n_heads // N_KV_HEADS
    pairs = group // 2
    n_blocks = s // WINDOW
    tn_dims = (((0,), (0,)), ((), ()))

    def body(q_ref, kp_ref, kc_ref, vp_ref, vc_ref, do_ref, sink_ref, dqkv_ref, colsum_ref, dsink_ref,
             dq_prev, dk_carry, dv_carry):
        b, i = pl.program_id(0), pl.program_id(1)
        left = lax.broadcasted_iota(jnp.int32, (1, LANES), 1) < HEAD_DIM

        @pl.when((b == 0) & (i == 0))
        def _():
            colsum_ref[...] = jnp.zeros_like(colsum_ref)
            dsink_ref[...] = jnp.zeros_like(dsink_ref)

        @pl.when(i == 0)
        def _():
            dqkv_ref[...] = jnp.zeros_like(dqkv_ref)
            dk_carry[...] = jnp.zeros_like(dk_carry)
            dv_carry[...] = jnp.zeros_like(dv_carry)

        @pl.when(i > 0)
        def _():
            dq_v = dq_prev[...]
            dqkv_ref[:, pl.ds(0, d)] = dq_v.astype(BF16)
            colsum_ref[:, pl.ds(0, d)] += jnp.sum(dq_v, axis=0, keepdims=True)

        @pl.when(i < n_blocks)
        def _():
            distf, valid = _attn_masks(i == 0)
            kspan = jnp.concatenate([kp_ref[...], kc_ref[...]], axis=0)
            vspan = jnp.concatenate([vp_ref[...], vc_ref[...]], axis=0)
            dk_blk, dv_blk = [], []
            for kv in range(N_KV_HEADS):
                kdup, vdup = _dup_halves(kspan, kv, left), _dup_halves(vspan, kv, left)
                dq_res, dk_sum, dv_sum = [], None, None
                for par in range(2):
                    keep = left if par == 0 else jnp.logical_not(left)
                    cols = [pl.ds((kv * pairs + p) * LANES, LANES) for p in range(pairs)]
                    zero = jnp.zeros((), BF16)
                    lhs = jnp.concatenate([jnp.where(keep, q_ref[:, cl], zero) for cl in cols], axis=0)
                    dol = jnp.concatenate([jnp.where(keep, do_ref[:, cl], zero) for cl in cols], axis=0)
                    sc = lax.dot_general(lhs, kdup, (((1,), (1,)), ((), ())), preferred_element_type=F32)
                    dp = lax.dot_general(dol, vdup, (((1,), (1,)), ((), ())), preferred_element_type=F32)
                    probs, dscores = [], []
                    for p in range(pairs):
                        h = kv * group + 2 * p + par
                        rows = slice(p * WINDOW, (p + 1) * WINDOW)
                        pr, p_sink = _attn_probs(sc[rows], h, n_heads, distf, valid, sink_ref[h])
                        delta = jnp.sum(pr * dp[rows], axis=-1, keepdims=True)
                        dscores.append((pr * (dp[rows] - delta) * (HEAD_DIM ** -0.5)).astype(BF16))
                        probs.append(pr.astype(BF16))
                        dsink_ref[pl.ds(h, 1), :] += jnp.zeros((1, LANES), F32) - jnp.sum(p_sink * delta)
                    ds_all = jnp.concatenate(dscores, axis=0)
                    p_all = jnp.concatenate(probs, axis=0)
                    dq_res.append(jnp.dot(ds_all, kdup, preferred_element_type=F32))
                    dk_par = lax.dot_general(ds_all, lhs, tn_dims, preferred_element_type=F32)
                    dv_par = lax.dot_general(p_all, dol, tn_dims, preferred_element_type=F32)
                    dk_sum = dk_par if dk_sum is None else dk_sum + dk_par
                    dv_sum = dv_par if dv_sum is None else dv_sum + dv_par
                for p in range(pairs):
                    rows = slice(p * WINDOW, (p + 1) * WINDOW)
                    dq_prev[:, pl.ds((kv * pairs + p) * LANES, LANES)] = jnp.where(left, dq_res[0][rows], dq_res[1][rows])
                dk_blk.append(dk_sum + pltpu.roll(dk_sum, HEAD_DIM, axis=1))
                dv_blk.append(dv_sum + pltpu.roll(dv_sum, HEAD_DIM, axis=1))
            dk_span = jnp.where(left, dk_blk[0], dk_blk[1])
            dv_span = jnp.where(left, dv_blk[0], dv_blk[1])
            dk_done = dk_carry[...] + dk_span[:WINDOW]
            dv_done = dv_carry[...] + dv_span[:WINDOW]
            dk_carry[...] = dk_span[WINDOW:]
            dv_carry[...] = dv_span[WINDOW:]

            @pl.when(i > 0)
            def _():
                dqkv_ref[:, pl.ds(d, LANES)] = dk_done.astype(BF16)
                dqkv_ref[:, pl.ds(d + LANES, LANES)] = dv_done.astype(BF16)
                colsum_ref[:, pl.ds(d, LANES)] += jnp.sum(dk_done, axis=0, keepdims=True)
                colsum_ref[:, pl.ds(d + LANES, LANES)] += jnp.sum(dv_done, axis=0, keepdims=True)

        @pl.when(i == n_blocks)
        def _():
            dk_done, dv_done = dk_carry[...], dv_carry[...]
            dqkv_ref[:, pl.ds(d, LANES)] = dk_done.astype(BF16)
            dqkv_ref[:, pl.ds(d + LANES, LANES)] = dv_done.astype(BF16)
            colsum_ref[:, pl.ds(d, LANES)] += jnp.sum(dk_done, axis=0, keepdims=True)
            colsum_ref[:, pl.ds(d + LANES, LANES)] += jnp.sum(dv_done, axis=0, keepdims=True)

    do_spec = pl.BlockSpec((None, WINDOW, d), lambda b, i: (b, jnp.minimum(i, n_blocks - 1), 0))
    out_shape = (jax.ShapeDtypeStruct((bsz, s, qkv_dim), BF16), jax.ShapeDtypeStruct((1, qkv_dim), F32),
                 jax.ShapeDtypeStruct((n_heads, LANES), F32))
    out_specs = (pl.BlockSpec((None, WINDOW, qkv_dim), lambda b, i: (b, jnp.maximum(i - 1, 0), 0)),
                 pl.BlockSpec((1, qkv_dim), lambda b, i: (0, 0)),
                 pl.BlockSpec((n_heads, LANES), lambda b, i: (0, 0)))
    return pl.pallas_call(body, out_shape=out_shape, grid=(bsz, n_blocks + 1),
                          in_specs=_attn_specs(d, n_blocks, True) + [do_spec, SMEM_SPEC], out_specs=out_specs,
                          scratch_shapes=[pltpu.VMEM((WINDOW, d), F32), pltpu.VMEM((WINDOW, LANES), F32),
                                          pltpu.VMEM((WINDOW, LANES), F32)],
                          name=name, compiler_params=_params("arbitrary", "arbitrary"))(qkv, qkv, qkv, qkv, qkv, do, sinks)


def _conv_tile(s):
    return min(256, s)


def _halo_specs(ts, width, s):
    per = ts // CONV_HALO
    prev = pl.BlockSpec((None, CONV_HALO, width), lambda b, i: (b, jnp.maximum(i * per - 1, 0), 0))
    nxt = pl.BlockSpec((None, CONV_HALO, width), lambda b, i: (b, jnp.minimum((i + 1) * per, s // CONV_HALO - 1), 0))
    cur = pl.BlockSpec((None, ts, width), lambda b, i: (b, i, 0))
    return prev, cur, nxt


def _glu(u, d):
    return u[:, :d] * jax.nn.sigmoid(u[:, d:])


def _store_shifted(shifted, value):
    rows = value.shape[0]
    shifted[0] = value
    for b in range(1, SUBLANES):
        shifted[b] = pltpu.roll(value, rows - b, axis=0)


def _at(shifted, row, n_rows, lanes):
    return shifted[row % SUBLANES, pl.ds(row - row % SUBLANES, n_rows), lanes]


def _taps(w_ref, shifted, out_ref, ts, d, offset):
    for r0 in range(0, ts, CONV_ROWS):
        for l0 in range(0, d, LANES):
            lanes = pl.ds(l0, LANES)
            acc = jnp.zeros((CONV_ROWS, LANES), F32)
            for j in range(CONV_WIDTH):
                acc = acc + w_ref[pl.ds(j, 1), lanes] * _at(shifted, r0 + offset(j), CONV_ROWS, lanes)
            out_ref[pl.ds(r0, CONV_ROWS), lanes] = acc


def _conv_fwd(u, w_dw, b_dw, ln_g, ln_b, *, name):
    bsz, s, d2 = u.shape
    d = d2 // 2
    ts = _conv_tile(s)

    def body(up_ref, uc_ref, w_ref, bdw_ref, g_ref, b_ref, z_ref, v_ref, gbuf):
        halo = jnp.where(pl.program_id(1) > 0, _glu(up_ref[...], d), 0.0)
        _store_shifted(gbuf, jnp.concatenate([halo, _glu(uc_ref[...], d)], axis=0))
        _taps(w_ref, gbuf, v_ref, ts, d, lambda j: CONV_HALO - (CONV_WIDTH - 1) + j)
        v = v_ref[...] + bdw_ref[...]
        v_ref[...] = v
        mu = jnp.mean(v, axis=-1, keepdims=True)
        cen = v - mu
        rstd = lax.rsqrt(jnp.mean(cen * cen, axis=-1, keepdims=True) + NORM_EPS)
        ln = cen * rstd * g_ref[...] + b_ref[...]
        z_ref[...] = (ln * jax.nn.sigmoid(ln)).astype(BF16)

    prev, cur, _ = _halo_specs(ts, d2, s)
    one = pl.BlockSpec((1, d), lambda b, i: (0, 0))
    row = pl.BlockSpec((None, ts, d), lambda b, i: (b, i, 0))
    return pl.pallas_call(body, out_shape=(jax.ShapeDtypeStruct((bsz, s, d), BF16), jax.ShapeDtypeStruct((bsz, s, d), F32)),
                          grid=(bsz, s // ts),
                          in_specs=[prev, cur, pl.BlockSpec((CONV_HALO, d), lambda b, i: (0, 0)), one, one, one],
                          out_specs=(row, row), scratch_shapes=[pltpu.VMEM((SUBLANES, ts + CONV_HALO, d), F32)], name=name,
                          compiler_params=_params("parallel", "parallel"))(u, u, w_dw, b_dw, ln_g, ln_b)


def _conv_bwd_ln(dz, v, ln_g, ln_b, *, name):
    bsz, s, d = v.shape
    ts = _row_tile(s, d * 4, 1 << 20)

    def body(dz_ref, v_ref, g_ref, b_ref, dv_ref, dg_ref, db_ref, dbdw_ref):
        v_v = v_ref[...]
        mu = jnp.mean(v_v, axis=-1, keepdims=True)
        cen = v_v - mu
        rstd = lax.rsqrt(jnp.mean(cen * cen, axis=-1, keepdims=True) + NORM_EPS)
        vhat = cen * rstd
        ln = vhat * g_ref[...] + b_ref[...]
        sig = jax.nn.sigmoid(ln)
        dln = dz_ref[...] * (sig * (1.0 + ln * (1.0 - sig)))
        dvhat = dln * g_ref[...]
        dv = rstd * (dvhat - jnp.mean(dvhat, axis=-1, keepdims=True)
                     - vhat * jnp.mean(dvhat * vhat, axis=-1, keepdims=True))
        dv_ref[...] = dv

        @pl.when((pl.program_id(0) == 0) & (pl.program_id(1) == 0))
        def _():
            dg_ref[...] = jnp.zeros_like(dg_ref)
            db_ref[...] = jnp.zeros_like(db_ref)
            dbdw_ref[...] = jnp.zeros_like(dbdw_ref)

        dg_ref[...] += jnp.sum(dln * vhat, axis=0, keepdims=True)
        db_ref[...] += jnp.sum(dln, axis=0, keepdims=True)
        dbdw_ref[...] += jnp.sum(dv, axis=0, keepdims=True)

    row = pl.BlockSpec((None, ts, d), lambda b, i: (b, i, 0))
    one = pl.BlockSpec((1, d), lambda b, i: (0, 0))
    vec = jax.ShapeDtypeStruct((1, d), F32)
    return pl.pallas_call(body, out_shape=(jax.ShapeDtypeStruct(v.shape, F32), vec, vec, vec), grid=(bsz, s // ts),
                          in_specs=[row, row, one, one], out_specs=(row, one, one, one), name=name,
                          compiler_params=_params("arbitrary", "arbitrary"))(dz, v, ln_g, ln_b)


def _conv_bwd_taps(dv, u, w_dw, *, name):
    bsz, s, d = dv.shape
    ts = _conv_tile(s)
    n_tiles = s // ts

    def body(dvc_ref, dvn_ref, up_ref, uc_ref, w_ref, du_ref, dbu_ref, dw_ref, gbuf, dvbuf, dglu):
        i = pl.program_id(1)

        @pl.when((pl.program_id(0) == 0) & (i == 0))
        def _():
            dbu_ref[...] = jnp.zeros_like(dbu_ref)
            dw_ref[...] = jnp.zeros_like(dw_ref)

        halo = jnp.where(i > 0, _glu(up_ref[...], d), 0.0)
        _store_shifted(gbuf, jnp.concatenate([halo, _glu(uc_ref[...], d)], axis=0))
        ahead = jnp.where(i < n_tiles - 1, dvn_ref[...], 0.0)
        _store_shifted(dvbuf, jnp.concatenate([dvc_ref[...], ahead], axis=0))
        _taps(w_ref, dvbuf, dglu, ts, d, lambda j: CONV_WIDTH - 1 - j)
        for l0 in range(0, d, LANES):
            lanes = pl.ds(l0, LANES)
            for j in range(CONV_WIDTH):
                acc = jnp.zeros((SUBLANES, LANES), F32)
                for r0 in range(0, ts, CONV_ROWS):
                    prod = dvc_ref[pl.ds(r0, CONV_ROWS), lanes] * _at(gbuf, r0 + CONV_HALO - (CONV_WIDTH - 1) + j, CONV_ROWS, lanes)
                    for k in range(0, CONV_ROWS, SUBLANES):
                        acc = acc + prod[k:k + SUBLANES]
                dw_ref[j, :, lanes] += acc
        u_v = uc_ref[...]
        a, sig = u_v[:, :d], jax.nn.sigmoid(u_v[:, d:])
        dg_v = dglu[...]
        da = dg_v * sig
        dgate = dg_v * a * sig * (1.0 - sig)
        du_ref[:, pl.ds(0, d)] = da.astype(BF16)
        du_ref[:, pl.ds(d, d)] = dgate.astype(BF16)
        dbu_ref[:, pl.ds(0, d)] += jnp.sum(da, axis=0, keepdims=True)
        dbu_ref[:, pl.ds(d, d)] += jnp.sum(dgate, axis=0, keepdims=True)

    _, dv_cur, dv_next = _halo_specs(ts, d, s)
    u_prev, u_cur, _ = _halo_specs(ts, 2 * d, s)
    out_shape = (jax.ShapeDtypeStruct((bsz, s, 2 * d), BF16), jax.ShapeDtypeStruct((1, 2 * d), F32),
                 jax.ShapeDtypeStruct((CONV_HALO, SUBLANES, d), F32))
    out_specs = (pl.BlockSpec((None, ts, 2 * d), lambda b, i: (b, i, 0)), pl.BlockSpec((1, 2 * d), lambda b, i: (0, 0)),
                 pl.BlockSpec((CONV_HALO, SUBLANES, d), lambda b, i: (0, 0, 0)))
    return pl.pallas_call(body, out_shape=out_shape, grid=(bsz, n_tiles),
                          in_specs=[dv_cur, dv_next, u_prev, u_cur, pl.BlockSpec((CONV_HALO, d), lambda b, i: (0, 0))],
                          out_specs=out_specs,
                          scratch_shapes=[pltpu.VMEM((SUBLANES, ts + CONV_HALO, d), F32),
                                          pltpu.VMEM((SUBLANES, ts + CONV_HALO, d), F32), pltpu.VMEM((ts, d), F32)],
                          name=name, compiler_params=_params("arbitrary", "arbitrary"))(dv, dv, u, u, w_dw)


def _mod_fwd(c_all, w_mod, b_mod, *, name):
    n_layers, d, n = w_mod.shape
    rows = c_all.shape[0]

    def body(c_ref, w_ref, b_ref, o_ref):
        cv = c_ref[...]
        cs = (cv * jax.nn.sigmoid(cv)).astype(BF16)
        o_ref[...] = jnp.dot(cs, w_ref[...].astype(BF16), preferred_element_type=F32) + b_ref[...]

    return pl.pallas_call(body, out_shape=jax.ShapeDtypeStruct((n_layers, rows, n), F32), grid=(n_layers,),
                          in_specs=[pl.BlockSpec((rows, d), lambda l: (0, 0)), pl.BlockSpec((None, d, n), lambda l: (l, 0, 0)),
                                    pl.BlockSpec((None, 1, n), lambda l: (l, 0, 0))],
                          out_specs=pl.BlockSpec((None, rows, n), lambda l: (l, 0, 0)), name=name,
                          compiler_params=_params("parallel"))(c_all, w_mod, b_mod)


def _mod_bwd(c_all, dmod, *, name):
    n_layers, rows, n = dmod.shape
    d = c_all.shape[1]

    def body(c_ref, g_ref, o_ref):
        cv = c_ref[...]
        cs = (cv * jax.nn.sigmoid(cv)).astype(BF16)
        o_ref[...] = lax.dot_general(cs, g_ref[...].astype(BF16), (((0,), (0,)), ((), ())), preferred_element_type=F32)

    return pl.pallas_call(body, out_shape=jax.ShapeDtypeStruct((n_layers, d, n), F32), grid=(n_layers,),
                          in_specs=[pl.BlockSpec((rows, d), lambda l: (0, 0)), pl.BlockSpec((None, rows, n), lambda l: (l, 0, 0))],
                          out_specs=pl.BlockSpec((None, d, n), lambda l: (l, 0, 0)), name=name,
                          compiler_params=_params("parallel"))(c_all, dmod)


def _add_half(g, recv, my_c, *, name):
    p, _, r, cdim = g.shape
    tr = _row_tile(r, cdim * 4)

    def body(c_ref, g_ref, r_ref, o_ref):
        o_ref[...] = (g_ref[...] + r_ref[...]).astype(BF16)

    grid_spec = pltpu.PrefetchScalarGridSpec(
        num_scalar_prefetch=1, grid=(p, r // tr),
        in_specs=[pl.BlockSpec((None, None, tr, cdim), lambda q, i, c_ref: (q, c_ref[0], i, 0)),
                  pl.BlockSpec((None, None, tr, cdim), lambda q, i, c_ref: (q, 0, i, 0))],
        out_specs=pl.BlockSpec((None, tr, cdim), lambda q, i, c_ref: (q, i, 0)))
    return pl.pallas_call(body, out_shape=jax.ShapeDtypeStruct((p, r, cdim), BF16), grid_spec=grid_spec, name=name,
                          compiler_params=_params("parallel", "parallel"))(my_c, g, recv)


def _add_pieces(chip, recv, my_qc, stacked, layer, n_layers, *, name):
    _, r, cdim = chip.shape
    tr = _row_tile(r, cdim * 4)

    def body(qc_ref, own_ref, r0_ref, r1_ref, r2_ref, *rest):
        f32 = lambda ref: ref[...].astype(F32)
        rest[-1][...] = ((f32(own_ref) + f32(r0_ref)) + f32(r1_ref)) + f32(r2_ref)

    piece = lambda k: pl.BlockSpec((None, tr, cdim), lambda i, qc_ref: (k, i, 0))
    in_specs = [pl.BlockSpec((None, tr, cdim), lambda i, qc_ref: (qc_ref[0], i, 0)), piece(0), piece(1), piece(2)]
    operands = [my_qc, chip, recv, recv, recv]
    aliases = {}
    if stacked is not None:
        in_specs.append(pl.BlockSpec(memory_space=pl.ANY))
        operands.append(stacked)
        aliases = {len(operands) - 1: 0}
    grid_spec = pltpu.PrefetchScalarGridSpec(
        num_scalar_prefetch=1, grid=(r // tr,), in_specs=in_specs,
        out_specs=pl.BlockSpec((None, None, tr, cdim), lambda i, qc_ref: (layer, qc_ref[1], i, 0)))
    return pl.pallas_call(body, out_shape=jax.ShapeDtypeStruct((n_layers, 2, r, cdim), F32), grid_spec=grid_spec,
                          input_output_aliases=aliases, name=name, compiler_params=_params("parallel"))(*operands)


def _sum_devices(parts, dmod, *, name):
    def body(p_ref, m_ref, o_ref, b_ref):
        acc = p_ref[0]
        for k in range(1, N_DEV):
            acc = acc + p_ref[k]
        o_ref[...] = acc
        tot = None
        for k in range(N_DEV):
            for e in range(dmod.shape[2]):
                tot = m_ref[k, :, e] if tot is None else tot + m_ref[k, :, e]
        b_ref[...] = tot

    out_shape = (jax.ShapeDtypeStruct(parts.shape[1:], F32),
                 jax.ShapeDtypeStruct((dmod.shape[1],) + dmod.shape[3:], F32))
    return pl.pallas_call(body, out_shape=out_shape, in_specs=[VMEM_SPEC, VMEM_SPEC], out_specs=(VMEM_SPEC, VMEM_SPEC),
                          name=name, compiler_params=_params())(parts, dmod)


def _adamw(w, g, m, v, *, name):
    r, cdim = w.shape
    tr = _row_tile(r, cdim * 4, 1 << 19)

    def body(w_ref, g_ref, m_ref, v_ref, d_ref, nm_ref, nv_ref):
        gv = g_ref[...]
        nm = ADAM_B1 * m_ref[...] + (1.0 - ADAM_B1) * gv
        nv = ADAM_B2 * v_ref[...] + (1.0 - ADAM_B2) * (gv * gv)
        m_hat = nm / (1.0 - ADAM_B1 ** ADAM_STEP)
        v_hat = nv / (1.0 - ADAM_B2 ** ADAM_STEP)
        d_ref[...] = -ADAM_LR * (m_hat / (jnp.sqrt(v_hat) + ADAM_EPS) + ADAM_WD * w_ref[...])
        nm_ref[...] = nm
        nv_ref[...] = nv

    row = pl.BlockSpec((tr, cdim), lambda i: (i, 0))
    shape = jax.ShapeDtypeStruct((r, cdim), F32)
    return pl.pallas_call(body, out_shape=(shape, shape, shape), grid=(r // tr,), in_specs=[row] * 4,
                          out_specs=(row, row, row), name=name, compiler_params=_params("parallel"))(w, g, m, v)


def _all_gather_small(block, *, name):
    m_per, n = block.shape

    def body(x_ref, out_ref, send_sems, recv_sems, local_sem):
        x, y, c, _ = _position()
        me, sibling = (x, y, c), (x, y, 1 - c)
        chips = [_peer(k, x, y, c)[:2] for k in CHIP_KINDS]

        def rows(px, py, pc):
            return out_ref.at[pl.ds((4 * px + 2 * py + pc) * m_per, m_per), :]

        def copy(k, blk, to, src=None):
            return pltpu.make_async_remote_copy(src_ref=rows(*blk) if src is None else src, dst_ref=rows(*blk),
                                                send_sem=send_sems.at[k], recv_sem=recv_sems.at[k],
                                                device_id=to, device_id_type=MESH)

        mine = pltpu.make_async_copy(x_ref, rows(*me), local_sem)
        mine.start()
        first = [copy(0, me, sibling, src=x_ref)]
        first += [copy(1 + j, me, (*chip, c), src=x_ref) for j, chip in enumerate(chips)]
        for cp in first:
            cp.start()
        passed = [copy(4 + j, (*chip, c), sibling) for j, chip in enumerate(chips)]
        for j, chip in enumerate(chips):
            copy(1 + j, (*chip, c), me).wait_recv()
            passed[j].start()
        copy(0, sibling, me).wait_recv()
        for j, chip in enumerate(chips):
            copy(4 + j, (*chip, 1 - c), me).wait_recv()
        for cp in first + passed:
            cp.wait_send()
        mine.wait()

    return pl.pallas_call(body, out_shape=jax.ShapeDtypeStruct((N_DEV * m_per, n), block.dtype),
                          in_specs=[VMEM_SPEC], out_specs=VMEM_SPEC,
                          scratch_shapes=[pltpu.SemaphoreType.DMA((7,)), pltpu.SemaphoreType.DMA((7,)),
                                          pltpu.SemaphoreType.DMA],
                          name=name, compiler_params=_params())(block)


def _pack(arrays, width):
    flat = jnp.concatenate([a.reshape(-1).astype(F32) for a in arrays])
    rows = -(-flat.shape[0] // width)
    rows = -(-rows // SUBLANES) * SUBLANES
    return jnp.pad(flat, (0, rows * width - flat.shape[0])).reshape(rows, width)


def _unpack(packed, shapes):
    flat, out, off = packed.reshape(-1), [], 0
    for shp in shapes:
        size = 1
        for dim in shp:
            size *= dim
        out.append(flat[off:off + size].reshape(shp))
        off += size
    return out


def _adamw_packed(ws, gs, ms, vs, width, *, name):
    shapes = [w.shape for w in ws]
    res = _adamw(_pack(ws, width), _pack(gs, width), _pack(ms, width), _pack(vs, width), name=name)
    return [_unpack(r, shapes) for r in res]


MB = float(1 << 20)


def _nbytes(shape, dtype):
    size = jnp.dtype(dtype).itemsize
    for dim in shape:
        size *= dim
    return size


def _gather_plans(shard, on_ready):
    r, cdim = shard.shape
    half = r // 2
    buf = jax.ShapeDtypeStruct((N_CHIPS, r, cdim), shard.dtype)
    rows = lambda ref, q, h: ref.at[pl.ds(q, 1), pl.ds(h * half, half)]
    there = lambda kind, pos: _peer_shard(kind, pos[0], pos[1])
    cost = 3 * _nbytes((half, cdim), shard.dtype) / MB

    def forward(outs):
        copies = [(lambda ins, outs, pos, kind=kind: rows(ins[0], there(kind, pos), pos[2]),
                   lambda ins, outs, pos, kind=kind: rows(outs[0], there(kind, pos), pos[2]),
                   lambda ins, outs, pos, kind=kind: rows(outs[0], there(kind, pos), 1 - pos[2]), "c") for kind in CHIP_KINDS]
        return _Plan("d2d", cost, [outs[0]], [jax.ShapeDtypeStruct(buf.shape, buf.dtype)], copies, {0: 0},
                     lambda done: on_ready(done[0]))

    copies = [(lambda ins, outs, pos: ins[0].at[:, pl.ds(pos[2] * half, half)],
               lambda ins, outs, pos: rows(outs[0], pos[3], pos[2]),
               lambda ins, outs, pos, kind=kind: rows(outs[0], there(kind, pos), pos[2]), kind) for kind in CHIP_KINDS]
    own = lambda ins, outs, pos: outs[0].at[pl.ds(pos[3], 1)]
    copies.append((lambda ins, outs, pos: ins[0], own, own, "c"))
    return _Plan("ici", cost, [shard[None]], [buf], copies), forward


class _Exchanges:
    def __init__(self):
        self.queue = []

    def add(self, plan, front=False):
        if front:
            self.queue.insert(0, plan)
        else:
            self.queue.append(plan)

    def take(self, budget_mb, at_least_one=False):
        chosen, spent = [p for p in self.queue if p.link == "d2d"], 0.0
        for p in self.queue:
            if p.link == "ici" and (spent + p.cost <= budget_mb or (at_least_one and spent == 0.0)):
                chosen.append(p)
                spent += p.cost
        if not chosen:
            return None
        self.queue = [p for p in self.queue if all(p is not ch for ch in chosen)]
        return _merge(chosen)

    def flush(self, budget_mb, until=lambda: False):
        while self.queue and not until():
            _exchange(self.take(budget_mb, at_least_one=True), name="exchange")


def kernel(x, c, w_mod, b_mod, norm_mix, norm_mlp, w_qkv, b_qkv, w_o, b_o, sinks, w_pw1, b_pw1, w_dw, b_dw, conv_ln_g, conv_ln_b, w_pw2, b_pw2, w_up, w_down, final_norm, loss_target, m_w_mod, m_b_mod, m_norm_mix, m_norm_mlp, m_w_qkv, m_b_qkv, m_w_o, m_b_o, m_sinks, m_w_pw1, m_b_pw1, m_w_dw, m_b_dw, m_conv_ln_g, m_conv_ln_b, m_w_pw2, m_b_pw2, m_w_up, m_w_down, m_final_norm, v_w_mod, v_b_mod, v_norm_mix, v_norm_mlp, v_w_qkv, v_b_qkv, v_w_o, v_b_o, v_sinks, v_w_pw1, v_b_pw1, v_w_dw, v_b_dw, v_conv_ln_g, v_conv_ln_b, v_w_pw2, v_b_pw2, v_w_up, v_w_down, v_final_norm):
    bsz, s, d = x.shape
    t = bsz * s
    depth = w_mod.shape[0]
    n_attn, n_conv = w_qkv.shape[0], w_pw1.shape[0]
    qkv_dim = d + 2 * N_KV_HEADS * HEAD_DIM
    mx, my, mc, mq = _position()
    me = 4 * mx + 2 * my + mc
    my_c = jnp.reshape(mc, (1,)).astype(jnp.int32)
    my_qc = jnp.stack([mq, mc]).astype(jnp.int32)
    pending = _Exchanges()
    SMALL, MEDIUM, LARGE = 2.5, 3.5, 6.5

    weights = {}
    order = []
    for i in range(depth):
        j = i // 2
        order += ([(("qkv", j), w_qkv[j], True), (("o", j), w_o[j], False)] if i % 2 == 0 else
                  [(("pw1", j), w_pw1[j], True), (("pw2", j), w_pw2[j], False)])
        order += [(("up", i), w_up[i], True), (("down", i), w_down[i], False)]
    for key, shard, by_cols in order:
        def ready(buf, key=key, by_cols=by_cols):
            weights[key] = jnp.transpose(buf, (1, 0, 2)).reshape(buf.shape[1], -1) if by_cols else buf.reshape(-1, buf.shape[2])
        ici, forward = _gather_plans(shard.astype(BF16), ready)
        ici.then = lambda outs, forward=forward: pending.add(forward(outs), front=True)
        pending.add(ici)

    def weight(key):
        pending.flush(SMALL, until=lambda: key in weights)
        return weights[key]

    small_sharded = [b_pw1, w_dw, b_dw, conv_ln_g, conv_ln_b, b_pw2]
    c_pad = jnp.pad(c, ((0, SUBLANES - bsz), (0, 0)))
    gathered = _all_gather_small(jnp.concatenate([c_pad, _pack(small_sharded, d)], axis=0), name="gather_c")
    gathered = gathered.reshape(N_DEV, -1, d)
    c_all = gathered[:, :bsz].reshape(N_DEV * bsz, d)
    per_chip = [_unpack(gathered[2 * q, SUBLANES:], [a.shape for a in small_sharded]) for q in range(N_CHIPS)]
    b_pw1_f, w_dw_f, b_dw_f, ln_g_f, ln_b_f, b_pw2_f = [jnp.concatenate([per_chip[q][k] for q in range(N_CHIPS)], axis=-1)
                                                           for k in range(len(small_sharded))]
    w_dw_f = jnp.pad(w_dw_f, ((0, 0), (0, CONV_HALO - CONV_WIDTH), (0, 0)))

    n_mod = w_mod.shape[2]
    b_mod_cols = lax.dynamic_slice_in_dim(b_mod, mq * n_mod, n_mod, axis=1).reshape(depth, 1, n_mod)
    mod_part = _mod_fwd(c_all, w_mod, b_mod_cols, name="mod_fwd")
    mod_all = _all_gather_small(mod_part.reshape(depth * N_DEV * bsz, n_mod), name="gather_mod")
    mod_all = mod_all.reshape(N_DEV, depth, N_DEV * bsz, n_mod)[0::2]
    mod = lax.dynamic_slice_in_dim(mod_all, me * bsz, bsz, axis=2)
    mod = jnp.transpose(mod, (1, 2, 0, 3)).reshape(depth, bsz, N_MOD, 1, d)
    mods = [[mod[i][:, k] for k in range(N_MOD)] for i in range(depth)]

    saved = []
    xc = x
    for i in range(depth):
        j = i // 2
        sh1, sc1, g1, sh2, sc2, g2 = mods[i]
        h1 = _normmod(xc, norm_mix[i][None], sc1, sh1, name="normmod")
        if i % 2 == 0:
            wq, wo = weight(("qkv", j)), weight(("o", j))
            qkv = _mm(h1.reshape(t, d), wq, bias=b_qkv[j], tn=qkv_dim, name="mm_qkv",
                      comm=pending.take(SMALL, True)).reshape(bsz, s, qkv_dim)
            mix = _attn_fwd(qkv, sinks[j], name="attn_fwd")
            y1, x1 = _mm(mix.reshape(t, d), wo, bias=b_o[j], epi="resid", resid=xc.reshape(t, d), gate=g1, seq=s,
                         tn=d, name="mm_out", comm=pending.take(SMALL, True))
            extra = (qkv, mix)
        else:
            wp1, wp2 = weight(("pw1", j)), weight(("pw2", j))
            u = _mm(h1.reshape(t, d), wp1, bias=b_pw1_f[j], out_dtype=F32, tn=d, name="mm_pw1",
                    comm=pending.take(MEDIUM, True)).reshape(bsz, s, 2 * d)
            mix, conv_v = _conv_fwd(u, w_dw_f[j], b_dw_f[j][None], ln_g_f[j][None], ln_b_f[j][None], name="conv_fwd")
            y1, x1 = _mm(mix.reshape(t, d), wp2, bias=b_pw2_f[j], epi="resid", resid=xc.reshape(t, d), gate=g1, seq=s,
                         tn=d, name="mm_out", comm=pending.take(SMALL, True))
            extra = (u, mix, conv_v)
        x1 = x1.reshape(bsz, s, d)
        h2 = _normmod(x1, norm_mlp[i][None], sc2, sh2, name="normmod")
        act = _mm(h2.reshape(t, d), weight(("up", i)), epi="relu2", tn=d, name="mm_up", comm=pending.take(LARGE, True))
        y2, x2 = _mm(act, weight(("down", i)), epi="resid", resid=x1.reshape(t, d), gate=g2, seq=s, name="mm_down",
                     comm=pending.take(LARGE, True))
        saved.append((xc, h1, extra, y1, x1, h2, act, y2))
        xc = x2.reshape(bsz, s, d)
    pending.flush(LARGE)

    last_y2, last_g2 = saved[-1][7].reshape(bsz, s, d), mods[-1][5]
    dx, loss_cols, d_final, dyb, dg2, _ = _loss_head(xc, loss_target, final_norm[None], (last_y2, last_g2), name="loss_head")
    loss = lax.psum(0.5 / d * jnp.sum(loss_cols), ("x", "y", "c"))

    totals = {}

    def reduce_scatter(name, layer, n_layers, grad):
        view = grad.reshape(N_CHIPS, 2, grad.shape[1] // 2, grad.shape[2])
        half_shape = (N_CHIPS, 1) + view.shape[2:]

        def scatter(outs):
            chip = _add_half(view, outs[0], my_c, name="rs_chipsum")
            copies = [(lambda ins, outs, pos, kind=kind: ins[0].at[pl.ds(_peer_shard(kind, pos[0], pos[1]), 1)],
                       lambda ins, outs, pos, k=k: outs[0].at[pl.ds(k, 1)],
                       lambda ins, outs, pos, k=k: outs[0].at[pl.ds(k, 1)], kind) for k, kind in enumerate(CHIP_KINDS)]

            def total(outs):
                totals[name] = _add_pieces(chip, outs[0], my_qc, totals.get(name), layer, n_layers, name="rs_total")

            pending.add(_Plan("ici", 3 * _nbytes(chip.shape[1:], BF16) / MB, [chip],
                              [jax.ShapeDtypeStruct((3,) + chip.shape[1:], BF16)], copies, then=total))

        pending.add(_Plan("d2d", _nbytes(half_shape, F32) / MB, [view], [jax.ShapeDtypeStruct(half_shape, F32)],
                          [(lambda ins, outs, pos: ins[0].at[:, pl.ds(1 - pos[2], 1)],
                            lambda ins, outs, pos: outs[0], lambda ins, outs, pos: outs[0], "c")], then=scatter))

    dmods, small = [None] * depth, {}
    for i in reversed(range(depth)):
        j = i // 2
        xin, h1, extra, y1, x1, h2, act, y2 = saved[i]
        sh1, sc1, g1, sh2, sc2, g2 = mods[i]
        dyb = dyb.reshape(t, d)
        reduce_scatter("down", i, depth,
                       _mm_tn(act, dyb, name="dw_down", comm=pending.take(LARGE)).reshape(N_CHIPS, -1, d))
        dup = _mm(dyb, weights["down", i], nt=True, epi="dact", act=act, tn=d, name="mm_dact", comm=pending.take(LARGE))
        reduce_scatter("up", i, depth,
                       _mm_tn(h2.reshape(t, d), dup, col_shards=N_CHIPS, name="dw_up", comm=pending.take(LARGE)))
        dh2 = _mm(dup, weights["up", i], nt=True, out_dtype=F32, name="mm_dh2", comm=pending.take(LARGE)).reshape(bsz, s, d)
        dx1, p2, dsh2, dyb, dg1, sdx = _normmod_bwd(x1, dh2, dx, norm_mlp[i][None], sc2, (y1.reshape(bsz, s, d), g1),
                                                    name="normmod_bwd")
        dyb = dyb.reshape(t, d)
        d_bias_out = jnp.sum(g1 * sdx, axis=(0, 1))
        if i % 2 == 0:
            qkv, mix = extra
            small["b_o", j] = d_bias_out
            reduce_scatter("o", j, n_attn,
                           _mm_tn(mix.reshape(t, d), dyb, name="dw_sq", comm=pending.take(SMALL)).reshape(N_CHIPS, -1, d))
            dmix = _mm(dyb, weights["o", j], nt=True, tn=d, name="mm_dmix", comm=pending.take(SMALL)).reshape(bsz, s, d)
            dqkv, d_bqkv, d_sink = _attn_bwd(qkv, dmix, sinks[j], name="attn_bwd")
            small["b_qkv", j], small["sinks", j] = d_bqkv[0], d_sink[:, 0]
            dqkv = dqkv.reshape(t, qkv_dim)
            dwq = _mm_tn(h1.reshape(t, d), dqkv, tn=qkv_dim, name="dw_qkv", comm=pending.take(SMALL))
            reduce_scatter("qkv", j, n_attn, jnp.transpose(dwq.reshape(d, N_CHIPS, -1), (1, 0, 2)))
            dh1 = _mm(dqkv, weights["qkv", j], nt=True, out_dtype=F32, tn=d, name="mm_dh1a", comm=pending.take(SMALL))
        else:
            u, mix, conv_v = extra
            small["b_pw2", j] = d_bias_out
            reduce_scatter("pw2", j, n_conv,
                           _mm_tn(mix.reshape(t, d), dyb, name="dw_sq", comm=pending.take(SMALL)).reshape(N_CHIPS, -1, d))
            dz = _mm(dyb, weights["pw2", j], nt=True, out_dtype=F32, tn=d, name="mm_dz", comm=pending.take(SMALL)).reshape(bsz, s, d)
            dv, d_lng, d_lnb, d_bdw = _conv_bwd_ln(dz, conv_v, ln_g_f[j][None], ln_b_f[j][None], name="conv_bwd_ln")
            du, d_bpw1, d_wdw = _conv_bwd_taps(dv, u, w_dw_f[j], name="conv_bwd_taps")
            small["ln_g", j], small["ln_b", j], small["b_dw", j] = d_lng[0], d_lnb[0], d_bdw[0]
            small["b_pw1", j], small["w_dw", j] = d_bpw1[0], jnp.sum(d_wdw[:CONV_WIDTH], axis=1)
            du = du.reshape(t, 2 * d)
            reduce_scatter("pw1", j, n_conv,
                           _mm_tn(h1.reshape(t, d), du, col_shards=N_CHIPS, name="dw_pw1", comm=pending.take(MEDIUM)))
            dh1 = _mm(du, weights["pw1", j], nt=True, out_dtype=F32, tn=d, name="mm_dh1c", comm=pending.take(MEDIUM))
        below = (saved[i - 1][7].reshape(bsz, s, d), mods[i - 1][5]) if i > 0 else None
        dx, p1, dsh1, *gate_grads = _normmod_bwd(xin, dh1.reshape(bsz, s, d), dx1, norm_mix[i][None], sc1, below,
                                                 name="normmod_bwd")
        small["norm_mix", i] = jnp.sum((1.0 + sc1) * p1, axis=(0, 1))
        small["norm_mlp", i] = jnp.sum((1.0 + sc2) * p2, axis=(0, 1))
        dmods[i] = jnp.concatenate([dsh1, norm_mix[i] * p1, dg1, dsh2, norm_mlp[i] * p2, dg2], axis=1)
        if i > 0:
            dyb, dg2, _ = gate_grads
    grad_x = dx

    small_names = ([("norm_mix", i) for i in range(depth)] + [("norm_mlp", i) for i in range(depth)]
                   + [(nm, j) for nm in ("b_qkv", "b_o", "sinks") for j in range(n_attn)]
                   + [(nm, j) for nm in ("b_pw1", "w_dw", "b_dw", "ln_g", "ln_b", "b_pw2") for j in range(n_conv)])
    small_list = [small[k] for k in small_names] + [d_final[0]]
    small_pack = _pack(small_list, d)
    dmod_rows = jnp.stack(dmods).reshape(depth * bsz * N_MOD, d)
    n_small = small_pack.shape[0]
    gathered = _all_gather_small(jnp.concatenate([small_pack, _pack([dmod_rows], d)], axis=0), name="gather_small")
    gathered = gathered.reshape(N_DEV, -1, d)
    dmod_all = gathered[:, n_small:n_small + depth * bsz * N_MOD].reshape(N_DEV, depth, bsz, N_MOD, d)
    small_sum, g_b_mod = _sum_devices(gathered[:, :n_small], dmod_all, name="sum_devices")
    small_tot = dict(zip(small_names + ["final_norm"], _unpack(small_sum, [a.shape for a in small_list])))
    stacked = lambda nm, count: jnp.stack([small_tot[nm, k] for k in range(count)])
    g_norm_mix, g_norm_mlp = stacked("norm_mix", depth), stacked("norm_mlp", depth)
    g_b_qkv, g_b_o, g_sinks = stacked("b_qkv", n_attn), stacked("b_o", n_attn), stacked("sinks", n_attn)
    g_final = small_tot["final_norm"]
    g_b_mod = g_b_mod.reshape(depth, N_MOD * d)
    shard_cols = lambda g: lax.dynamic_slice_in_dim(g, mq * (g.shape[-1] // N_CHIPS), g.shape[-1] // N_CHIPS, axis=g.ndim - 1)
    g_b_pw1, g_w_dw, g_b_dw, g_ln_g, g_ln_b, g_b_pw2 = [shard_cols(stacked(nm, n_conv))
                                                        for nm in ("b_pw1", "w_dw", "b_dw", "ln_g", "ln_b", "b_pw2")]

    dmod_cols = jnp.transpose(dmod_all, (1, 0, 2, 3, 4)).reshape(depth, N_DEV * bsz, N_MOD * d)
    dmod_cols = lax.dynamic_slice_in_dim(dmod_cols, mq * n_mod, n_mod, axis=2)
    g_w_mod = _mod_bwd(c_all, dmod_cols, name="mod_bwd")

    pending.flush(LARGE)
    names = ["qkv", "o", "pw1", "pw2", "up", "down"]
    bufs = [totals[nm] for nm in names]
    copies = []
    for a, buf in enumerate(bufs):
        for layer in range(buf.shape[0]):
            half = lambda ref, h, layer=layer: ref.at[pl.ds(layer, 1), pl.ds(h, 1)]
            copies.append((lambda ins, outs, pos, a=a, half=half: half(ins[a], pos[2]),
                           lambda ins, outs, pos, a=a, half=half: half(outs[a], pos[2]),
                           lambda ins, outs, pos, a=a, half=half: half(outs[a], 1 - pos[2]), "c"))
    shared = {}
    _exchange(_Plan("d2d", 0.0, bufs, [jax.ShapeDtypeStruct(b.shape, F32) for b in bufs], copies,
                    {a: a for a in range(len(bufs))}, lambda outs: shared.update(zip(names, outs))), name="rs_share")
    g_w_qkv, g_w_o, g_w_pw1, g_w_pw2, g_w_up, g_w_down = [
        shared[nm].reshape(shared[nm].shape[0], -1, shared[nm].shape[3]) for nm in names]

    def adam(w, g, m, v, name):
        two_d = lambda a: a.reshape(-1, a.shape[-1])
        return [r.reshape(w.shape) for r in _adamw(two_d(w), two_d(g), two_d(m), two_d(v), name=name)]

    results = {}
    for nm, w, g, m, v in (("w_mod", w_mod, g_w_mod, m_w_mod, v_w_mod), ("w_qkv", w_qkv, g_w_qkv, m_w_qkv, v_w_qkv),
                           ("w_o", w_o, g_w_o, m_w_o, v_w_o), ("w_pw1", w_pw1, g_w_pw1, m_w_pw1, v_w_pw1),
                           ("w_pw2", w_pw2, g_w_pw2, m_w_pw2, v_w_pw2), ("w_up", w_up, g_w_up, m_w_up, v_w_up),
                           ("w_down", w_down, g_w_down, m_w_down, v_w_down)):
        results[nm] = (g,) + tuple(adam(w, g, m, v, "adamw"))
    small_w = dict(b_mod=(b_mod, g_b_mod, m_b_mod, v_b_mod), norm_mix=(norm_mix, g_norm_mix, m_norm_mix, v_norm_mix),
                   norm_mlp=(norm_mlp, g_norm_mlp, m_norm_mlp, v_norm_mlp), b_qkv=(b_qkv, g_b_qkv, m_b_qkv, v_b_qkv),
                   b_o=(b_o, g_b_o, m_b_o, v_b_o), sinks=(sinks, g_sinks, m_sinks, v_sinks),
                   b_pw1=(b_pw1, g_b_pw1, m_b_pw1, v_b_pw1), w_dw=(w_dw, g_w_dw, m_w_dw, v_w_dw),
                   b_dw=(b_dw, g_b_dw, m_b_dw, v_b_dw), conv_ln_g=(conv_ln_g, g_ln_g, m_conv_ln_g, v_conv_ln_g),
                   conv_ln_b=(conv_ln_b, g_ln_b, m_conv_ln_b, v_conv_ln_b), b_pw2=(b_pw2, g_b_pw2, m_b_pw2, v_b_pw2),
                   final_norm=(final_norm, g_final, m_final_norm, v_final_norm))
    names = list(small_w)
    deltas, new_ms, new_vs = _adamw_packed(*[[small_w[nm][k] for nm in names] for k in range(4)], d, name="adamw_small")
    for k, nm in enumerate(names):
        results[nm] = (small_w[nm][1], deltas[k], new_ms[k], new_vs[k])

    weight_order = ["w_mod", "b_mod", "norm_mix", "norm_mlp", "w_qkv", "b_qkv", "w_o", "b_o", "sinks", "w_pw1", "b_pw1",
                    "w_dw", "b_dw", "conv_ln_g", "conv_ln_b", "w_pw2", "b_pw2", "w_up", "w_down", "final_norm"]
    return (loss, grad_x, *[results[nm][0] for nm in weight_order], *[results[nm][1] for nm in weight_order],
            *[results[nm][2] for nm in weight_order], *[results[nm][3] for nm in weight_order])
```

```python
import functools

import jax
import jax.numpy as jnp
from jax import lax
from jax.experimental import pallas as pl
from jax.experimental.pallas import tpu as pltpu

F32, BF16 = jnp.float32, jnp.bfloat16
MESH = pl.DeviceIdType.MESH
N_CHIPS = 4
N_DEV = 8
LANES = 128
SUBLANES = 8
VMEM_LIMIT = 48 * 1024 * 1024

NORM_EPS = 1e-6
HEAD_DIM = 64
N_KV_HEADS = 2
WINDOW = 128
CONV_WIDTH = 31
CONV_HALO = 32
CONV_ROWS = 32
N_MOD = 6

ADAM_LR, ADAM_B1, ADAM_B2, ADAM_EPS, ADAM_WD, ADAM_STEP = 0.001, 0.9, 0.999, 1e-08, 0.01, 10

HBM_SPEC = pl.BlockSpec(memory_space=pltpu.HBM)
VMEM_SPEC = pl.BlockSpec(memory_space=pltpu.VMEM)
SMEM_SPEC = pl.BlockSpec(memory_space=pltpu.SMEM)


def _params(*sem):
    return pltpu.CompilerParams(dimension_semantics=sem or None, vmem_limit_bytes=VMEM_LIMIT)


def _row_tile(rows, width_bytes, target=1 << 20):
    t = rows
    while t % 2 == 0 and t > SUBLANES and t * width_bytes > target:
        t //= 2
    return t


CHIP_KINDS = ("x", "y", "xy")


def _position():
    x, y, c = lax.axis_index("x"), lax.axis_index("y"), lax.axis_index("c")
    return x, y, c, 2 * x + y


def _peer(kind, x, y, c):
    return {"c": (x, y, 1 - c), "x": (1 - x, y, c), "y": (x, 1 - y, c), "xy": (1 - x, 1 - y, c)}[kind]


def _peer_shard(kind, x, y):
    px, py, _ = _peer(kind, x, y, 0)
    return 2 * px + py


class _Plan:
    def __init__(self, link, cost, operands, out_shapes, copies, aliases=None, then=None):
        self.link, self.cost = link, cost
        self.operands, self.out_shapes, self.copies = list(operands), list(out_shapes), list(copies)
        self.aliases, self.then = dict(aliases or {}), then


def _merge(plans):
    operands, out_shapes, copies, aliases, thens = [], [], [], {}, []
    for p in plans:
        i0, o0 = len(operands), len(out_shapes)
        i1, o1 = i0 + len(p.operands), o0 + len(p.out_shapes)

        def shifted(f, i0=i0, i1=i1, o0=o0, o1=o1):
            return lambda ins, outs, pos: f(ins[i0:i1], outs[o0:o1], pos)

        copies += [(shifted(src), shifted(dst), shifted(land), kind) for src, dst, land, kind in p.copies]
        aliases.update({i0 + k: o0 + v for k, v in p.aliases.items()})
        operands += p.operands
        out_shapes += p.out_shapes
        thens.append((p.then, o0, o1))

    def then(outs):
        for f, o0, o1 in thens:
            if f is not None:
                f(outs[o0:o1])

    return _Plan("mixed", sum(p.cost for p in plans), operands, out_shapes, copies, aliases, then)


def _call(body, *, name, out_shape, operands, grid=(), in_specs=(), out_specs=(), scratch_shapes=(), sem=(), comm=None):
    single = not isinstance(out_shape, (tuple, list))
    out_shape = [out_shape] if single else list(out_shape)
    out_specs = [out_specs] if single else list(out_specs)
    if comm is None:
        res = pl.pallas_call(body, out_shape=out_shape, grid=grid, in_specs=list(in_specs), out_specs=out_specs,
                             scratch_shapes=list(scratch_shapes), name=name, compiler_params=_params(*sem))(*operands)
        return res[0] if single else res
    n_in, n_out, n_scr = len(operands), len(out_shape), len(scratch_shapes)
    c_in, c_out, n_cp = len(comm.operands), len(comm.out_shapes), len(comm.copies)

    def wrapped(*refs):
        ins, refs = refs[:n_in], refs[n_in:]
        cins, refs = refs[:c_in], refs[c_in:]
        outs, refs = refs[:n_out], refs[n_out:]
        couts, refs = refs[:c_out], refs[c_out:]
        scr, (send_sems, recv_sems) = refs[:n_scr], refs[n_scr:]

        def descriptors():
            pos = _position()
            sends, lands = [], []
            for k, (src, dst, landing, kind) in enumerate(comm.copies):
                common = dict(send_sem=send_sems.at[k], recv_sem=recv_sems.at[k],
                              device_id=_peer(kind, *pos[:3]), device_id_type=MESH)
                sends.append(pltpu.make_async_remote_copy(src_ref=src(cins, couts, pos), dst_ref=dst(cins, couts, pos), **common))
                lands.append(pltpu.make_async_remote_copy(src_ref=src(cins, couts, pos), dst_ref=landing(cins, couts, pos), **common))
            return sends, lands

        def start():
            for cp in descriptors()[0]:
                cp.start()

        def finish():
            sends, lands = descriptors()
            for cp in lands:
                cp.wait_recv()
            for cp in sends:
                cp.wait_send()

        if not grid:
            start()
            body(*ins, *outs, *scr)
            finish()
        else:
            ids = [pl.program_id(ax) for ax in range(len(grid))]
            first, last = ids[0] == 0, ids[0] == grid[0] - 1
            for ax in range(1, len(grid)):
                first, last = first & (ids[ax] == 0), last & (ids[ax] == grid[ax] - 1)
            pl.when(first)(start)
            body(*ins, *outs, *scr)
            pl.when(last)(finish)

    res = pl.pallas_call(wrapped, out_shape=out_shape + comm.out_shapes, grid=grid,
                         in_specs=list(in_specs) + [HBM_SPEC] * c_in, out_specs=out_specs + [HBM_SPEC] * c_out,
                         scratch_shapes=list(scratch_shapes) + [pltpu.SemaphoreType.DMA((n_cp,)), pltpu.SemaphoreType.DMA((n_cp,))],
                         input_output_aliases={n_in + k: n_out + v for k, v in comm.aliases.items()},
                         name=name, compiler_params=_params(*["arbitrary"] * len(grid)))(*operands, *comm.operands)
    if comm.then is not None:
        comm.then(res[n_out:])
    return res[0] if single else res[:n_out]


def _exchange(plan, *, name):
    _call(lambda: None, name=name, out_shape=[], operands=[], comm=plan)


def _mm(a, b, *, name, nt=False, tm=1024, tn=512, epi="plain", out_dtype=BF16,
        bias=None, act=None, resid=None, gate=None, seq=None, comm=None):
    m, k = a.shape
    n = b.shape[0] if nt else b.shape[1]
    tm, tn = min(tm, m, seq or m), min(tn, n)
    assert m % tm == 0 and n % tn == 0
    dims = (((1,), (1,)), ((), ())) if nt else (((1,), (0,)), ((), ()))
    tile = pl.BlockSpec((tm, tn), lambda j, i: (i, j))
    operands = [a, b]
    in_specs = [pl.BlockSpec((tm, k), lambda j, i: (i, 0)),
                pl.BlockSpec((tn, k), lambda j, i: (j, 0)) if nt else pl.BlockSpec((k, tn), lambda j, i: (0, j))]
    if bias is not None:
        operands.append(bias.reshape(1, n))
        in_specs.append(pl.BlockSpec((1, tn), lambda j, i: (0, j)))
    if epi == "dact":
        operands.append(act)
        in_specs.append(tile)
    if epi == "resid":
        assert seq % tm == 0
        operands += [resid, gate]
        in_specs += [tile, pl.BlockSpec((None, 1, tn), lambda j, i: (i * tm // seq, 0, j))]
        out_shape = (jax.ShapeDtypeStruct((m, n), BF16), jax.ShapeDtypeStruct((m, n), F32))
        out_specs = (tile, tile)
    elif epi == "relu2":
        out_shape = (jax.ShapeDtypeStruct((m, n), BF16), jax.ShapeDtypeStruct((m, n), BF16))
        out_specs = (tile, tile)
    else:
        out_shape = jax.ShapeDtypeStruct((m, n), out_dtype)
        out_specs = tile

    def body(*refs):
        it = iter(refs)
        a_ref, b_ref = next(it), next(it)
        acc = lax.dot_general(a_ref[...], b_ref[...], dims, preferred_element_type=F32)
        if bias is not None:
            acc = acc + next(it)[...]
        if epi == "plain":
            next(it)[...] = acc.astype(out_dtype)
        elif epi == "relu2":
            r = jnp.maximum(acc, 0.0)
            next(it)[...] = (r * r).astype(BF16)
            next(it)[...] = (2.0 * r).astype(BF16)
        elif epi == "dact":
            slope_ref = next(it)
            next(it)[...] = (acc * slope_ref[...].astype(F32)).astype(out_dtype)
        else:
            resid_ref, gate_ref = next(it), next(it)
            y_ref, x_ref = next(it), next(it)
            y_ref[...] = acc.astype(BF16)
            x_ref[...] = resid_ref[...] + gate_ref[...] * acc

    return _call(body, name=name, out_shape=out_shape, operands=operands, grid=(n // tn, m // tm), in_specs=in_specs,
                 out_specs=out_specs, sem=("parallel", "parallel"), comm=comm)


def _mm_tn(a, b, *, name, tm=1024, tn=1024, tk=2048, col_shards=None, comm=None):
    t, m = a.shape
    n = b.shape[1]
    tm, tk = min(tm, m), min(tk, t)
    if col_shards is None:
        tn = min(tn, n)
        out_shape = jax.ShapeDtypeStruct((m, n), F32)
        out_spec = pl.BlockSpec((tm, tn), lambda i, j, k: (i, j))
    else:
        per = n // col_shards
        tn = min(tn, per)
        assert per % tn == 0
        out_shape = jax.ShapeDtypeStruct((col_shards, m, per), F32)
        out_spec = pl.BlockSpec((None, tm, tn), lambda i, j, k: (j // (per // tn), i, j % (per // tn)))
    assert m % tm == 0 and n % tn == 0 and t % tk == 0

    def body(a_ref, b_ref, o_ref):
        @pl.when(pl.program_id(2) == 0)
        def _():
            o_ref[...] = jnp.zeros_like(o_ref)

        o_ref[...] += lax.dot_general(a_ref[...], b_ref[...], (((0,), (0,)), ((), ())),
                                      preferred_element_type=F32)

    return _call(body, name=name, out_shape=out_shape, operands=[a, b], grid=(m // tm, n // tn, t // tk),
                 in_specs=[pl.BlockSpec((tk, tm), lambda i, j, k: (k, i)), pl.BlockSpec((tk, tn), lambda i, j, k: (k, j))],
                 out_specs=out_spec, sem=("parallel", "parallel", "arbitrary"), comm=comm)


def _normmod(x, gamma, sc, sh, *, name):
    bsz, s, d = x.shape
    ts = _row_tile(s, d * 4, 2 << 20)

    def body(x_ref, g_ref, sc_ref, sh_ref, o_ref):
        xf = x_ref[...]
        r = lax.rsqrt(jnp.mean(xf * xf, axis=-1, keepdims=True) + NORM_EPS)
        o_ref[...] = (xf * r * g_ref[...] * (1.0 + sc_ref[...]) + sh_ref[...]).astype(BF16)

    row = pl.BlockSpec((None, ts, d), lambda b, i: (b, i, 0))
    per_ex = pl.BlockSpec((None, 1, d), lambda b, i: (b, 0, 0))
    return pl.pallas_call(body, out_shape=jax.ShapeDtypeStruct(x.shape, BF16), grid=(bsz, s // ts),
                          in_specs=[row, pl.BlockSpec((1, d), lambda b, i: (0, 0)), per_ex, per_ex],
                          out_specs=row, name=name, compiler_params=_params("parallel", "parallel"))(x, gamma, sc, sh)


def _gate_grads(dxv, y_ref, gate_ref, dy_ref, dg_ref, sdx_ref):
    dy_ref[...] = (gate_ref[...] * dxv).astype(BF16)

    @pl.when(pl.program_id(1) == 0)
    def _():
        dg_ref[...] = jnp.zeros_like(dg_ref)
        sdx_ref[...] = jnp.zeros_like(sdx_ref)

    dg_ref[...] += jnp.sum(dxv * y_ref[...].astype(F32), axis=0, keepdims=True)
    sdx_ref[...] += jnp.sum(dxv, axis=0, keepdims=True)


def _normmod_bwd(x, dh, dres, gamma, sc, producer=None, *, name):
    bsz, s, d = x.shape
    ts = _row_tile(s, d * 4, 1 << 20)

    def body(x_ref, dh_ref, dres_ref, g_ref, sc_ref, *rest):
        dx_ref, p_ref, dsh_ref = rest[-6:-3] if producer else rest
        xf = x_ref[...]
        r = lax.rsqrt(jnp.mean(xf * xf, axis=-1, keepdims=True) + NORM_EPS)
        xhat = xf * r
        dh_v = dh_ref[...]
        dxhat = dh_v * (g_ref[...] * (1.0 + sc_ref[...]))
        dxv = dres_ref[...] + r * (dxhat - xhat * jnp.mean(dxhat * xhat, axis=-1, keepdims=True))
        dx_ref[...] = dxv

        @pl.when(pl.program_id(1) == 0)
        def _():
            p_ref[...] = jnp.zeros_like(p_ref)
            dsh_ref[...] = jnp.zeros_like(dsh_ref)

        p_ref[...] += jnp.sum(dh_v * xhat, axis=0, keepdims=True)
        dsh_ref[...] += jnp.sum(dh_v, axis=0, keepdims=True)
        if producer:
            _gate_grads(dxv, rest[0], rest[1], *rest[-3:])

    row = pl.BlockSpec((None, ts, d), lambda b, i: (b, i, 0))
    per_ex = pl.BlockSpec((None, 1, d), lambda b, i: (b, 0, 0))
    vec = jax.ShapeDtypeStruct((bsz, 1, d), F32)
    gate_in, gate_out, gate_specs = (), (), ()
    if producer:
        gate_in, gate_out, gate_specs = (row, per_ex), (jax.ShapeDtypeStruct(x.shape, BF16), vec, vec), (row, per_ex, per_ex)
    return pl.pallas_call(body, out_shape=(jax.ShapeDtypeStruct(x.shape, F32), vec, vec) + gate_out, grid=(bsz, s // ts),
                          in_specs=[row, row, row, pl.BlockSpec((1, d), lambda b, i: (0, 0)), per_ex, *gate_in],
                          out_specs=(row, per_ex, per_ex) + gate_specs, name=name,
                          compiler_params=_params("parallel", "arbitrary"))(x, dh, dres, gamma, sc, *(producer or ()))


def _loss_head(x, target, gamma, producer, *, name):
    bsz, s, d = x.shape
    ts = _row_tile(s, d * 4, 1 << 20)

    def body(x_ref, t_ref, g_ref, y_ref, gate_ref, dx_ref, loss_ref, dg_ref, *gate_refs):
        xf = x_ref[...]
        r = lax.rsqrt(jnp.mean(xf * xf, axis=-1, keepdims=True) + NORM_EPS)
        xhat = xf * r
        err = xhat * g_ref[...] - t_ref[...]
        dy = err * (1.0 / d)
        dxhat = dy * g_ref[...]
        dxv = r * (dxhat - xhat * jnp.mean(dxhat * xhat, axis=-1, keepdims=True))
        dx_ref[...] = dxv

        @pl.when((pl.program_id(0) == 0) & (pl.program_id(1) == 0))
        def _():
            loss_ref[...] = jnp.zeros_like(loss_ref)
            dg_ref[...] = jnp.zeros_like(dg_ref)

        loss_ref[...] += jnp.sum(err * err, axis=0, keepdims=True)
        dg_ref[...] += jnp.sum(dy * xhat, axis=0, keepdims=True)
        _gate_grads(dxv, y_ref, gate_ref, *gate_refs)

    row = pl.BlockSpec((None, ts, d), lambda b, i: (b, i, 0))
    one = pl.BlockSpec((1, d), lambda b, i: (0, 0))
    per_ex = pl.BlockSpec((None, 1, d), lambda b, i: (b, 0, 0))
    vec, ex_vec = jax.ShapeDtypeStruct((1, d), F32), jax.ShapeDtypeStruct((bsz, 1, d), F32)
    return pl.pallas_call(body, out_shape=(jax.ShapeDtypeStruct(x.shape, F32), vec, vec,
                                           jax.ShapeDtypeStruct(x.shape, BF16), ex_vec, ex_vec), grid=(bsz, s // ts),
                          in_specs=[row, row, one, row, per_ex], out_specs=(row, one, one, row, per_ex, per_ex), name=name,
                          compiler_params=_params("arbitrary", "arbitrary"))(x, target, gamma, *producer)


def _alibi_slope(h, n_heads):
    return 2.0 ** (-8.0 * (h + 1) / n_heads)


def _attn_masks(first_block):
    qi = lax.broadcasted_iota(jnp.int32, (WINDOW, 2 * WINDOW), 0)
    ki = lax.broadcasted_iota(jnp.int32, (WINDOW, 2 * WINDOW), 1)
    dist = qi + WINDOW - ki
    first_key = jnp.where(first_block, WINDOW, 0)
    valid = (dist >= 0) & (dist < WINDOW) & (ki >= first_key)
    return dist.astype(F32), valid


def _dup_halves(span, kv, left):
    f = span.astype(F32)
    rolled = pltpu.roll(f, HEAD_DIM, axis=1)
    out = jnp.where(left, f, rolled) if kv == 0 else jnp.where(left, rolled, f)
    return out.astype(BF16)


def _attn_probs(s, h, n_heads, distf, valid, sink):
    s = s * (HEAD_DIM ** -0.5) - _alibi_slope(h, n_heads) * distf
    s = jnp.where(valid, s, -1e30)
    m = jnp.maximum(jnp.max(s, axis=-1, keepdims=True), sink)
    e = jnp.exp(s - m)
    e_sink = jnp.exp(sink - m)
    inv = 1.0 / (jnp.sum(e, axis=-1, keepdims=True) + e_sink)
    return e * inv, e_sink * inv


def _attn_specs(d, n_blocks, clamp):
    kcol = d // LANES
    cur = (lambda i: jnp.minimum(i, n_blocks - 1)) if clamp else (lambda i: i)
    prev = lambda i: jnp.maximum(cur(i) - 1, 0)
    kv = lambda col, blk: pl.BlockSpec((None, WINDOW, LANES), lambda b, i: (b, blk(i), col))
    return [pl.BlockSpec((None, WINDOW, d), lambda b, i: (b, cur(i), 0)),
            kv(kcol, prev), kv(kcol, cur), kv(kcol + 1, prev), kv(kcol + 1, cur)]


def _attn_fwd(qkv, sinks, *, name):
    bsz, s, qkv_dim = qkv.shape
    d = qkv_dim - 2 * N_KV_HEADS * HEAD_DIM
    n_heads = d // HEAD_DIM
    group = n_heads // N_KV_HEADS
    pairs = group // 2
    n_blocks = s // WINDOW

    def body(q_ref, kp_ref, kc_ref, vp_ref, vc_ref, sink_ref, o_ref):
        left = lax.broadcasted_iota(jnp.int32, (1, LANES), 1) < HEAD_DIM
        distf, valid = _attn_masks(pl.program_id(1) == 0)
        kspan = jnp.concatenate([kp_ref[...], kc_ref[...]], axis=0)
        vspan = jnp.concatenate([vp_ref[...], vc_ref[...]], axis=0)
        for kv in range(N_KV_HEADS):
            kdup, vdup = _dup_halves(kspan, kv, left), _dup_halves(vspan, kv, left)
            res = []
            for par in range(2):
                keep = left if par == 0 else jnp.logical_not(left)
                cols = [pl.ds((kv * pairs + p) * LANES, LANES) for p in range(pairs)]
                lhs = jnp.concatenate([jnp.where(keep, q_ref[:, cl], jnp.zeros((), BF16)) for cl in cols], axis=0)
                sc = lax.dot_general(lhs, kdup, (((1,), (1,)), ((), ())), preferred_element_type=F32)
                probs = []
                for p in range(pairs):
                    h = kv * group + 2 * p + par
                    pr, _ = _attn_probs(sc[p * WINDOW:(p + 1) * WINDOW], h, n_heads, distf, valid, sink_ref[h])
                    probs.append(pr.astype(BF16))
                res.append(jnp.dot(jnp.concatenate(probs, axis=0), vdup, preferred_element_type=F32))
            for p in range(pairs):
                rows = slice(p * WINDOW, (p + 1) * WINDOW)
                o_ref[:, pl.ds((kv * pairs + p) * LANES, LANES)] = jnp.where(left, res[0][rows], res[1][rows]).astype(BF16)

    return pl.pallas_call(body, out_shape=jax.ShapeDtypeStruct((bsz, s, d), BF16), grid=(bsz, n_blocks),
                          in_specs=_attn_specs(d, n_blocks, False) + [SMEM_SPEC],
                          out_specs=pl.BlockSpec((None, WINDOW, d), lambda b, i: (b, i, 0)), name=name,
                          compiler_params=_params("parallel", "parallel"))(qkv, qkv, qkv, qkv, qkv, sinks)


def _attn_bwd(qkv, do, sinks, *, name):
    bsz, s, qkv_dim = qkv.shape
    d = qkv_dim - 2 * N_KV_HEADS * HEAD_DIM
    n_heads = d // HEAD_DIM
    group = n_heads // N_KV_HEADS
    pairs = group // 2
    n_blocks = s // WINDOW
    tn_dims = (((0,), (0,)), ((), ()))

    def body(q_ref, kp_ref, kc_ref, vp_ref, vc_ref, do_ref, sink_ref, dqkv_ref, colsum_ref, dsink_ref,
             dq_prev, dk_carry, dv_carry):
        b, i = pl.program_id(0), pl.program_id(1)
        left = lax.broadcasted_iota(jnp.int32, (1, LANES), 1) < HEAD_DIM

        @pl.when((b == 0) & (i == 0))
        def _():
            colsum_ref[...] = jnp.zeros_like(colsum_ref)
            dsink_ref[...] = jnp.zeros_like(dsink_ref)

        @pl.when(i == 0)
        def _():
            dqkv_ref[...] = jnp.zeros_like(dqkv_ref)
            dk_carry[...] = jnp.zeros_like(dk_carry)
            dv_carry[...] = jnp.zeros_like(dv_carry)

        @pl.when(i > 0)
        def _():
            dq_v = dq_prev[...]
            dqkv_ref[:, pl.ds(0, d)] = dq_v.astype(BF16)
            colsum_ref[:, pl.ds(0, d)] += jnp.sum(dq_v, axis=0, keepdims=True)

        @pl.when(i < n_blocks)
        def _():
            distf, valid = _attn_masks(i == 0)
            kspan = jnp.concatenate([kp_ref[...], kc_ref[...]], axis=0)
            vspan = jnp.concatenate([vp_ref[...], vc_ref[...]], axis=0)
            dk_blk, dv_blk = [], []
            for kv in range(N_KV_HEADS):
                kdup, vdup = _dup_halves(kspan, kv, left), _dup_halves(vspan, kv, left)
                dq_res, dk_sum, dv_sum = [], None, None
                for par in range(2):
                    keep = left if par == 0 else jnp.logical_not(left)
                    cols = [pl.ds((kv * pairs + p) * LANES, LANES) for p in range(pairs)]
                    zero = jnp.zeros((), BF16)
                    lhs = jnp.concatenate([jnp.where(keep, q_ref[:, cl], zero) for cl in cols], axis=0)
                    dol = jnp.concatenate([jnp.where(keep, do_ref[:, cl], zero) for cl in cols], axis=0)
                    sc = lax.dot_general(lhs, kdup, (((1,), (1,)), ((), ())), preferred_element_type=F32)
                    dp = lax.dot_general(dol, vdup, (((1,), (1,)), ((), ())), preferred_element_type=F32)
                    probs, dscores = [], []
                    for p in range(pairs):
                        h = kv * group + 2 * p + par
                        rows = slice(p * WINDOW, (p + 1) * WINDOW)
                        pr, p_sink = _attn_probs(sc[rows], h, n_heads, distf, valid, sink_ref[h])
                        delta = jnp.sum(pr * dp[rows], axis=-1, keepdims=True)
                        dscores.append((pr * (dp[rows] - delta) * (HEAD_DIM ** -0.5)).astype(BF16))
                        probs.append(pr.astype(BF16))
                        dsink_ref[pl.ds(h, 1), :] += jnp.zeros((1, LANES), F32) - jnp.sum(p_sink * delta)
                    ds_all = jnp.concatenate(dscores, axis=0)
                    p_all = jnp.concatenate(probs, axis=0)
                    dq_res.append(jnp.dot(ds_all, kdup, preferred_element_type=F32))
                    dk_par = lax.dot_general(ds_all, lhs, tn_dims, preferred_element_type=F32)
                    dv_par = lax.dot_general(p_all, dol, tn_dims, preferred_element_type=F32)
                    dk_sum = dk_par if dk_sum is None else dk_sum + dk_par
                    dv_sum = dv_par if dv_sum is None else dv_sum + dv_par
                for p in range(pairs):
                    rows = slice(p * WINDOW, (p + 1) * WINDOW)
                    dq_prev[:, pl.ds((kv * pairs + p) * LANES, LANES)] = jnp.where(left, dq_res[0][rows], dq_res[1][rows])
                dk_blk.append(dk_sum + pltpu.roll(dk_sum, HEAD_DIM, axis=1))
                dv_blk.append(dv_sum + pltpu.roll(dv_sum, HEAD_DIM, axis=1))
            dk_span = jnp.where(left, dk_blk[0], dk_blk[1])
            dv_span = jnp.where(left, dv_blk[0], dv_blk[1])
            dk_done = dk_carry[...] + dk_span[:WINDOW]
            dv_done = dv_carry[...] + dv_span[:WINDOW]
            dk_carry[...] = dk_span[WINDOW:]
            dv_carry[...] = dv_span[WINDOW:]

            @pl.when(i > 0)
            def _():
                dqkv_ref[:, pl.ds(d, LANES)] = dk_done.astype(BF16)
                dqkv_ref[:, pl.ds(d + LANES, LANES)] = dv_done.astype(BF16)
                colsum_ref[:, pl.ds(d, LANES)] += jnp.sum(dk_done, axis=0, keepdims=True)
                colsum_ref[:, pl.ds(d + LANES, LANES)] += jnp.sum(dv_done, axis=0, keepdims=True)

        @pl.when(i == n_blocks)
        def _():
            dk_done, dv_done = dk_carry[...], dv_carry[...]
            dqkv_ref[:, pl.ds(d, LANES)] = dk_done.astype(BF16)
            dqkv_ref[:, pl.ds(d + LANES, LANES)] = dv_done.astype(BF16)
            colsum_ref[:, pl.ds(d, LANES)] += jnp.sum(dk_done, axis=0, keepdims=True)
            colsum_ref[:, pl.ds(d + LANES, LANES)] += jnp.sum(dv_done, axis=0, keepdims=True)

    do_spec = pl.BlockSpec((None, WINDOW, d), lambda b, i: (b, jnp.minimum(i, n_blocks - 1), 0))
    out_shape = (jax.ShapeDtypeStruct((bsz, s, qkv_dim), BF16), jax.ShapeDtypeStruct((1, qkv_dim), F32),
                 jax.ShapeDtypeStruct((n_heads, LANES), F32))
    out_specs = (pl.BlockSpec((None, WINDOW, qkv_dim), lambda b, i: (b, jnp.maximum(i - 1, 0), 0)),
                 pl.BlockSpec((1, qkv_dim), lambda b, i: (0, 0)),
                 pl.BlockSpec((n_heads, LANES), lambda b, i: (0, 0)))
    return pl.pallas_call(body, out_shape=out_shape, grid=(bsz, n_blocks + 1),
                          in_specs=_attn_specs(d, n_blocks, True) + [do_spec, SMEM_SPEC], out_specs=out_specs,
                          scratch_shapes=[pltpu.VMEM((WINDOW, d), F32), pltpu.VMEM((WINDOW, LANES), F32),
                                          pltpu.VMEM((WINDOW, LANES), F32)],
                          name=name, compiler_params=_params("arbitrary", "arbitrary"))(qkv, qkv, qkv, qkv, qkv, do, sinks)


def _conv_tile(s):
    return min(256, s)


def _halo_specs(ts, width, s):
    per = ts // CONV_HALO
    prev = pl.BlockSpec((None, CONV_HALO, width), lambda b, i: (b, jnp.maximum(i * per - 1, 0), 0))
    nxt = pl.BlockSpec((None, CONV_HALO, width), lambda b, i: (b, jnp.minimum((i + 1) * per, s // CONV_HALO - 1), 0))
    cur = pl.BlockSpec((None, ts, width), lambda b, i: (b, i, 0))
    return prev, cur, nxt


def _glu(u, d):
    return u[:, :d] * jax.nn.sigmoid(u[:, d:])


def _store_shifted(shifted, value):
    rows = value.shape[0]
    shifted[0] = value
    for b in range(1, SUBLANES):
        shifted[b] = pltpu.roll(value, rows - b, axis=0)


def _at(shifted, row, n_rows, lanes):
    return shifted[row % SUBLANES, pl.ds(row - row % SUBLANES, n_rows), lanes]


def _taps(w_ref, shifted, out_ref, ts, d, offset):
    for r0 in range(0, ts, CONV_ROWS):
        for l0 in range(0, d, LANES):
            lanes = pl.ds(l0, LANES)
            acc = jnp.zeros((CONV_ROWS, LANES), F32)
            for j in range(CONV_WIDTH):
                acc = acc + w_ref[pl.ds(j, 1), lanes] * _at(shifted, r0 + offset(j), CONV_ROWS, lanes)
            out_ref[pl.ds(r0, CONV_ROWS), lanes] = acc


def _conv_fwd(u, w_dw, b_dw, ln_g, ln_b, *, name):
    bsz, s, d2 = u.shape
    d = d2 // 2
    ts = _conv_tile(s)

    def body(up_ref, uc_ref, w_ref, bdw_ref, g_ref, b_ref, z_ref, v_ref, gbuf):
        halo = jnp.where(pl.program_id(1) > 0, _glu(up_ref[...], d), 0.0)
        _store_shifted(gbuf, jnp.concatenate([halo, _glu(uc_ref[...], d)], axis=0))
        _taps(w_ref, gbuf, v_ref, ts, d, lambda j: CONV_HALO - (CONV_WIDTH - 1) + j)
        v = v_ref[...] + bdw_ref[...]
        v_ref[...] = v
        mu = jnp.mean(v, axis=-1, keepdims=True)
        cen = v - mu
        rstd = lax.rsqrt(jnp.mean(cen * cen, axis=-1, keepdims=True) + NORM_EPS)
        ln = cen * rstd * g_ref[...] + b_ref[...]
        z_ref[...] = (ln * jax.nn.sigmoid(ln)).astype(BF16)

    prev, cur, _ = _halo_specs(ts, d2, s)
    one = pl.BlockSpec((1, d), lambda b, i: (0, 0))
    row = pl.BlockSpec((None, ts, d), lambda b, i: (b, i, 0))
    return pl.pallas_call(body, out_shape=(jax.ShapeDtypeStruct((bsz, s, d), BF16), jax.ShapeDtypeStruct((bsz, s, d), F32)),
                          grid=(bsz, s // ts),
                          in_specs=[prev, cur, pl.BlockSpec((CONV_HALO, d), lambda b, i: (0, 0)), one, one, one],
                          out_specs=(row, row), scratch_shapes=[pltpu.VMEM((SUBLANES, ts + CONV_HALO, d), F32)], name=name,
                          compiler_params=_params("parallel", "parallel"))(u, u, w_dw, b_dw, ln_g, ln_b)


def _conv_bwd_ln(dz, v, ln_g, ln_b, *, name):
    bsz, s, d = v.shape
    ts = _row_tile(s, d * 4, 1 << 20)

    def body(dz_ref, v_ref, g_ref, b_ref, dv_ref, dg_ref, db_ref, dbdw_ref):
        v_v = v_ref[...]
        mu = jnp.mean(v_v, axis=-1, keepdims=True)
        cen = v_v - mu
        rstd = lax.rsqrt(jnp.mean(cen * cen, axis=-1, keepdims=True) + NORM_EPS)
        vhat = cen * rstd
        ln = vhat * g_ref[...] + b_ref[...]
        sig = jax.nn.sigmoid(ln)
        dln = dz_ref[...] * (sig * (1.0 + ln * (1.0 - sig)))
        dvhat = dln * g_ref[...]
        dv = rstd * (dvhat - jnp.mean(dvhat, axis=-1, keepdims=True)
                     - vhat * jnp.mean(dvhat * vhat, axis=-1, keepdims=True))
        dv_ref[...] = dv

        @pl.when((pl.program_id(0) == 0) & (pl.program_id(1) == 0))
        def _():
            dg_ref[...] = jnp.zeros_like(dg_ref)
            db_ref[...] = jnp.zeros_like(db_ref)
            dbdw_ref[...] = jnp.zeros_like(dbdw_ref)

        dg_ref[...] += jnp.sum(dln * vhat, axis=0, keepdims=True)
        db_ref[...] += jnp.sum(dln, axis=0, keepdims=True)
        dbdw_ref[...] += jnp.sum(dv, axis=0, keepdims=True)

    row = pl.BlockSpec((None, ts, d), lambda b, i: (b, i, 0))
    one = pl.BlockSpec((1, d), lambda b, i: (0, 0))
    vec = jax.ShapeDtypeStruct((1, d), F32)
    return pl.pallas_call(body, out_shape=(jax.ShapeDtypeStruct(v.shape, F32), vec, vec, vec), grid=(bsz, s // ts),
                          in_specs=[row, row, one, one], out_specs=(row, one, one, one), name=name,
                          compiler_params=_params("arbitrary", "arbitrary"))(dz, v, ln_g, ln_b)


def _conv_bwd_taps(dv, u, w_dw, *, name):
    bsz, s, d = dv.shape
    ts = _conv_tile(s)
    n_tiles = s // ts

    def body(dvc_ref, dvn_ref, up_ref, uc_ref, w_ref, du_ref, dbu_ref, dw_ref, gbuf, dvbuf, dglu):
        i = pl.program_id(1)

        @pl.when((pl.program_id(0) == 0) & (i == 0))
        def _():
            dbu_ref[...] = jnp.zeros_like(dbu_ref)
            dw_ref[...] = jnp.zeros_like(dw_ref)

        halo = jnp.where(i > 0, _glu(up_ref[...], d), 0.0)
        _store_shifted(gbuf, jnp.concatenate([halo, _glu(uc_ref[...], d)], axis=0))
        ahead = jnp.where(i < n_tiles - 1, dvn_ref[...], 0.0)
        _store_shifted(dvbuf, jnp.concatenate([dvc_ref[...], ahead], axis=0))
        _taps(w_ref, dvbuf, dglu, ts, d, lambda j: CONV_WIDTH - 1 - j)
        for l0 in range(0, d, LANES):
            lanes = pl.ds(l0, LANES)
            for j in range(CONV_WIDTH):
                acc = jnp.zeros((SUBLANES, LANES), F32)
                for r0 in range(0, ts, CONV_ROWS):
                    prod = dvc_ref[pl.ds(r0, CONV_ROWS), lanes] * _at(gbuf, r0 + CONV_HALO - (CONV_WIDTH - 1) + j, CONV_ROWS, lanes)
                    for k in range(0, CONV_ROWS, SUBLANES):
                        acc = acc + prod[k:k + SUBLANES]
                dw_ref[j, :, lanes] += acc
        u_v = uc_ref[...]
        a, sig = u_v[:, :d], jax.nn.sigmoid(u_v[:, d:])
        dg_v = dglu[...]
        da = dg_v * sig
        dgate = dg_v * a * sig * (1.0 - sig)
        du_ref[:, pl.ds(0, d)] = da.astype(BF16)
        du_ref[:, pl.ds(d, d)] = dgate.astype(BF16)
        dbu_ref[:, pl.ds(0, d)] += jnp.sum(da, axis=0, keepdims=True)
        dbu_ref[:, pl.ds(d, d)] += jnp.sum(dgate, axis=0, keepdims=True)

    _, dv_cur, dv_next = _halo_specs(ts, d, s)
    u_prev, u_cur, _ = _halo_specs(ts, 2 * d, s)
    out_shape = (jax.ShapeDtypeStruct((bsz, s, 2 * d), BF16), jax.ShapeDtypeStruct((1, 2 * d), F32),
                 jax.ShapeDtypeStruct((CONV_HALO, SUBLANES, d), F32))
    out_specs = (pl.BlockSpec((None, ts, 2 * d), lambda b, i: (b, i, 0)), pl.BlockSpec((1, 2 * d), lambda b, i: (0, 0)),
                 pl.BlockSpec((CONV_HALO, SUBLANES, d), lambda b, i: (0, 0, 0)))
    return pl.pallas_call(body, out_shape=out_shape, grid=(bsz, n_tiles),
                          in_specs=[dv_cur, dv_next, u_prev, u_cur, pl.BlockSpec((CONV_HALO, d), lambda b, i: (0, 0))],
                          out_specs=out_specs,
                          scratch_shapes=[pltpu.VMEM((SUBLANES, ts + CONV_HALO, d), F32),
                                          pltpu.VMEM((SUBLANES, ts + CONV_HALO, d), F32), pltpu.VMEM((ts, d), F32)],
                          name=name, compiler_params=_params("arbitrary", "arbitrary"))(dv, dv, u, u, w_dw)


def _mod_fwd(c_all, w_mod, b_mod, *, name):
    n_layers, d, n = w_mod.shape
    rows = c_all.shape[0]

    def body(c_ref, w_ref, b_ref, o_ref):
        cv = c_ref[...]
        cs = (cv * jax.nn.sigmoid(cv)).astype(BF16)
        o_ref[...] = jnp.dot(cs, w_ref[...].astype(BF16), preferred_element_type=F32) + b_ref[...]

    return pl.pallas_call(body, out_shape=jax.ShapeDtypeStruct((n_layers, rows, n), F32), grid=(n_layers,),
                          in_specs=[pl.BlockSpec((rows, d), lambda l: (0, 0)), pl.BlockSpec((None, d, n), lambda l: (l, 0, 0)),
                                    pl.BlockSpec((None, 1, n), lambda l: (l, 0, 0))],
                          out_specs=pl.BlockSpec((None, rows, n), lambda l: (l, 0, 0)), name=name,
                          compiler_params=_params("parallel"))(c_all, w_mod, b_mod)


def _mod_bwd(c_all, dmod, *, name):
    n_layers, rows, n = dmod.shape
    d = c_all.shape[1]

    def body(c_ref, g_ref, o_ref):
        cv = c_ref[...]
        cs = (cv * jax.nn.sigmoid(cv)).astype(BF16)
        o_ref[...] = lax.dot_general(cs, g_ref[...].astype(BF16), (((0,), (0,)), ((), ())), preferred_element_type=F32)

    return pl.pallas_call(body, out_shape=jax.ShapeDtypeStruct((n_layers, d, n), F32), grid=(n_layers,),
                          in_specs=[pl.BlockSpec((rows, d), lambda l: (0, 0)), pl.BlockSpec((None, rows, n), lambda l: (l, 0, 0))],
                          out_specs=pl.BlockSpec((None, d, n), lambda l: (l, 0, 0)), name=name,
                          compiler_params=_params("parallel"))(c_all, dmod)


def _add_half(g, recv, my_c, *, name):
    p, _, r, cdim = g.shape
    tr = _row_tile(r, cdim * 4)

    def body(c_ref, g_ref, r_ref, o_ref):
        o_ref[...] = (g_ref[...] + r_ref[...]).astype(BF16)

    grid_spec = pltpu.PrefetchScalarGridSpec(
        num_scalar_prefetch=1, grid=(p, r // tr),
        in_specs=[pl.BlockSpec((None, None, tr, cdim), lambda q, i, c_ref: (q, c_ref[0], i, 0)),
                  pl.BlockSpec((None, None, tr, cdim), lambda q, i, c_ref: (q, 0, i, 0))],
        out_specs=pl.BlockSpec((None, tr, cdim), lambda q, i, c_ref: (q, i, 0)))
    return pl.pallas_call(body, out_shape=jax.ShapeDtypeStruct((p, r, cdim), BF16), grid_spec=grid_spec, name=name,
                          compiler_params=_params("parallel", "parallel"))(my_c, g, recv)


def _add_pieces(chip, recv, my_qc, stacked, layer, n_layers, *, name):
    _, r, cdim = chip.shape
    tr = _row_tile(r, cdim * 4)

    def body(qc_ref, own_ref, r0_ref, r1_ref, r2_ref, *rest):
        f32 = lambda ref: ref[...].astype(F32)
        rest[-1][...] = ((f32(own_ref) + f32(r0_ref)) + f32(r1_ref)) + f32(r2_ref)

    piece = lambda k: pl.BlockSpec((None, tr, cdim), lambda i, qc_ref: (k, i, 0))
    in_specs = [pl.BlockSpec((None, tr, cdim), lambda i, qc_ref: (qc_ref[0], i, 0)), piece(0), piece(1), piece(2)]
    operands = [my_qc, chip, recv, recv, recv]
    aliases = {}
    if stacked is not None:
        in_specs.append(pl.BlockSpec(memory_space=pl.ANY))
        operands.append(stacked)
        aliases = {len(operands) - 1: 0}
    grid_spec = pltpu.PrefetchScalarGridSpec(
        num_scalar_prefetch=1, grid=(r // tr,), in_specs=in_specs,
        out_specs=pl.BlockSpec((None, None, tr, cdim), lambda i, qc_ref: (layer, qc_ref[1], i, 0)))
    return pl.pallas_call(body, out_shape=jax.ShapeDtypeStruct((n_layers, 2, r, cdim), F32), grid_spec=grid_spec,
                          input_output_aliases=aliases, name=name, compiler_params=_params("parallel"))(*operands)


def _sum_devices(parts, dmod, *, name):
    def body(p_ref, m_ref, o_ref, b_ref):
        acc = p_ref[0]
        for k in range(1, N_DEV):
            acc = acc + p_ref[k]
        o_ref[...] = acc
        tot = None
        for k in range(N_DEV):
            for e in range(dmod.shape[2]):
                tot = m_ref[k, :, e] if tot is None else tot + m_ref[k, :, e]
        b_ref[...] = tot

    out_shape = (jax.ShapeDtypeStruct(parts.shape[1:], F32),
                 jax.ShapeDtypeStruct((dmod.shape[1],) + dmod.shape[3:], F32))
    return pl.pallas_call(body, out_shape=out_shape, in_specs=[VMEM_SPEC, VMEM_SPEC], out_specs=(VMEM_SPEC, VMEM_SPEC),
                          name=name, compiler_params=_params())(parts, dmod)


def _adamw(w, g, m, v, *, name, comm=None):
    r, cdim = w.shape
    tr = _row_tile(r, cdim * 4, 1 << 20)

    def body(w_ref, g_ref, m_ref, v_ref, d_ref, nm_ref, nv_ref):
        gv = g_ref[...]
        nm = ADAM_B1 * m_ref[...] + (1.0 - ADAM_B1) * gv
        nv = ADAM_B2 * v_ref[...] + (1.0 - ADAM_B2) * (gv * gv)
        m_hat = nm / (1.0 - ADAM_B1 ** ADAM_STEP)
        v_hat = nv / (1.0 - ADAM_B2 ** ADAM_STEP)
        d_ref[...] = -ADAM_LR * (m_hat / (jnp.sqrt(v_hat) + ADAM_EPS) + ADAM_WD * w_ref[...])
        nm_ref[...] = nm
        nv_ref[...] = nv

    row = pl.BlockSpec((tr, cdim), lambda i: (i, 0))
    shape = jax.ShapeDtypeStruct((r, cdim), F32)
    return _call(body, name=name, out_shape=(shape, shape, shape), operands=[w, g, m, v], grid=(r // tr,),
                 in_specs=[row] * 4, out_specs=(row, row, row), sem=("parallel",), comm=comm)


def _all_gather_small(block, *, name, comm=None):
    m_per, n = block.shape

    def body(x_ref, out_ref, send_sems, recv_sems, local_sem):
        x, y, c, _ = _position()
        me, sibling = (x, y, c), (x, y, 1 - c)
        chips = [_peer(k, x, y, c)[:2] for k in CHIP_KINDS]

        def rows(px, py, pc):
            return out_ref.at[pl.ds((4 * px + 2 * py + pc) * m_per, m_per), :]

        def copy(k, blk, to, src=None):
            return pltpu.make_async_remote_copy(src_ref=rows(*blk) if src is None else src, dst_ref=rows(*blk),
                                                send_sem=send_sems.at[k], recv_sem=recv_sems.at[k],
                                                device_id=to, device_id_type=MESH)

        mine = pltpu.make_async_copy(x_ref, rows(*me), local_sem)
        mine.start()
        first = [copy(0, me, sibling, src=x_ref)]
        first += [copy(1 + j, me, (*chip, c), src=x_ref) for j, chip in enumerate(chips)]
        for cp in first:
            cp.start()
        passed = [copy(4 + j, (*chip, c), sibling) for j, chip in enumerate(chips)]
        for j, chip in enumerate(chips):
            copy(1 + j, (*chip, c), me).wait_recv()
            passed[j].start()
        copy(0, sibling, me).wait_recv()
        for j, chip in enumerate(chips):
            copy(4 + j, (*chip, 1 - c), me).wait_recv()
        for cp in first + passed:
            cp.wait_send()
        mine.wait()

    return _call(body, name=name, out_shape=jax.ShapeDtypeStruct((N_DEV * m_per, n), block.dtype), operands=[block],
                 in_specs=[VMEM_SPEC], out_specs=VMEM_SPEC,
                 scratch_shapes=[pltpu.SemaphoreType.DMA((7,)), pltpu.SemaphoreType.DMA((7,)), pltpu.SemaphoreType.DMA],
                 comm=comm)


def _pack(arrays, width):
    flat = jnp.concatenate([a.reshape(-1).astype(F32) for a in arrays])
    rows = -(-flat.shape[0] // width)
    rows = -(-rows // SUBLANES) * SUBLANES
    return jnp.pad(flat, (0, rows * width - flat.shape[0])).reshape(rows, width)


def _unpack(packed, shapes):
    flat, out, off = packed.reshape(-1), [], 0
    for shp in shapes:
        size = 1
        for dim in shp:
            size *= dim
        out.append(flat[off:off + size].reshape(shp))
        off += size
    return out


def _adamw_packed(ws, gs, ms, vs, width, *, name):
    shapes = [w.shape for w in ws]
    res = _adamw(_pack(ws, width), _pack(gs, width), _pack(ms, width), _pack(vs, width), name=name)
    return [_unpack(r, shapes) for r in res]


MB = float(1 << 20)


def _nbytes(shape, dtype):
    size = jnp.dtype(dtype).itemsize
    for dim in shape:
        size *= dim
    return size


def _gather_plans(shard, on_ready):
    r, cdim = shard.shape
    half = r // 2
    buf = jax.ShapeDtypeStruct((N_CHIPS, r, cdim), shard.dtype)
    rows = lambda ref, q, h: ref.at[pl.ds(q, 1), pl.ds(h * half, half)]
    there = lambda kind, pos: _peer_shard(kind, pos[0], pos[1])
    cost = 3 * _nbytes((half, cdim), shard.dtype) / MB

    def forward(outs):
        copies = [(lambda ins, outs, pos, kind=kind: rows(ins[0], there(kind, pos), pos[2]),
                   lambda ins, outs, pos, kind=kind: rows(outs[0], there(kind, pos), pos[2]),
                   lambda ins, outs, pos, kind=kind: rows(outs[0], there(kind, pos), 1 - pos[2]), "c") for kind in CHIP_KINDS]
        return _Plan("d2d", cost, [outs[0]], [jax.ShapeDtypeStruct(buf.shape, buf.dtype)], copies, {0: 0},
                     lambda done: on_ready(done[0]))

    copies = [(lambda ins, outs, pos: ins[0].at[:, pl.ds(pos[2] * half, half)],
               lambda ins, outs, pos: rows(outs[0], pos[3], pos[2]),
               lambda ins, outs, pos, kind=kind: rows(outs[0], there(kind, pos), pos[2]), kind) for kind in CHIP_KINDS]
    own = lambda ins, outs, pos: outs[0].at[pl.ds(pos[3], 1)]
    copies.append((lambda ins, outs, pos: ins[0], own, own, "c"))
    return _Plan("ici", cost, [shard[None]], [buf], copies), forward


class _Exchanges:
    def __init__(self):
        self.queue = []

    def add(self, plan, front=False):
        if front:
            self.queue.insert(0, plan)
        else:
            self.queue.append(plan)

    def take(self, budget_mb, at_least_one=False):
        chosen, spent = [p for p in self.queue if p.link == "d2d"], 0.0
        for p in self.queue:
            if p.link == "ici" and (spent + p.cost <= budget_mb or (at_least_one and spent == 0.0)):
                chosen.append(p)
                spent += p.cost
        if not chosen:
            return None
        self.queue = [p for p in self.queue if all(p is not ch for ch in chosen)]
        return _merge(chosen)

    def flush(self, budget_mb, until=lambda: False):
        while self.queue and not until():
            _exchange(self.take(budget_mb, at_least_one=True), name="exchange")


def kernel(x, c, w_mod, b_mod, norm_mix, norm_mlp, w_qkv, b_qkv, w_o, b_o, sinks, w_pw1, b_pw1, w_dw, b_dw, conv_ln_g, conv_ln_b, w_pw2, b_pw2, w_up, w_down, final_norm, loss_target, m_w_mod, m_b_mod, m_norm_mix, m_norm_mlp, m_w_qkv, m_b_qkv, m_w_o, m_b_o, m_sinks, m_w_pw1, m_b_pw1, m_w_dw, m_b_dw, m_conv_ln_g, m_conv_ln_b, m_w_pw2, m_b_pw2, m_w_up, m_w_down, m_final_norm, v_w_mod, v_b_mod, v_norm_mix, v_norm_mlp, v_w_qkv, v_b_qkv, v_w_o, v_b_o, v_sinks, v_w_pw1, v_b_pw1, v_w_dw, v_b_dw, v_conv_ln_g, v_conv_ln_b, v_w_pw2, v_b_pw2, v_w_up, v_w_down, v_final_norm):
    bsz, s, d = x.shape
    t = bsz * s
    depth = w_mod.shape[0]
    n_attn, n_conv = w_qkv.shape[0], w_pw1.shape[0]
    qkv_dim = d + 2 * N_KV_HEADS * HEAD_DIM
    mx, my, mc, mq = _position()
    me = 4 * mx + 2 * my + mc
    my_c = jnp.reshape(mc, (1,)).astype(jnp.int32)
    my_qc = jnp.stack([mq, mc]).astype(jnp.int32)
    pending = _Exchanges()
    SMALL, MEDIUM, LARGE = 2.5, 3.5, 6.5

    weights = {}
    order = []
    for i in range(depth):
        j = i // 2
        order += ([(("qkv", j), w_qkv[j], True), (("o", j), w_o[j], False)] if i % 2 == 0 else
                  [(("pw1", j), w_pw1[j], True), (("pw2", j), w_pw2[j], False)])
        order += [(("up", i), w_up[i], True), (("down", i), w_down[i], False)]
    for key, shard, by_cols in order:
        def ready(buf, key=key, by_cols=by_cols):
            weights[key] = jnp.transpose(buf, (1, 0, 2)).reshape(buf.shape[1], -1) if by_cols else buf.reshape(-1, buf.shape[2])
        ici, forward = _gather_plans(shard.astype(BF16), ready)
        ici.then = lambda outs, forward=forward: pending.add(forward(outs), front=True)
        pending.add(ici)

    def weight(key):
        pending.flush(SMALL, until=lambda: key in weights)
        return weights[key]

    small_sharded = [b_pw1, w_dw, b_dw, conv_ln_g, conv_ln_b, b_pw2]
    c_pad = jnp.pad(c, ((0, SUBLANES - bsz), (0, 0)))
    gathered = _all_gather_small(jnp.concatenate([c_pad, _pack(small_sharded, d)], axis=0), name="gather_c",
                                 comm=pending.take(SMALL, True))
    gathered = gathered.reshape(N_DEV, -1, d)
    c_all = gathered[:, :bsz].reshape(N_DEV * bsz, d)
    per_chip = [_unpack(gathered[2 * q, SUBLANES:], [a.shape for a in small_sharded]) for q in range(N_CHIPS)]
    b_pw1_f, w_dw_f, b_dw_f, ln_g_f, ln_b_f, b_pw2_f = [jnp.concatenate([per_chip[q][k] for q in range(N_CHIPS)], axis=-1)
                                                           for k in range(len(small_sharded))]
    w_dw_f = jnp.pad(w_dw_f, ((0, 0), (0, CONV_HALO - CONV_WIDTH), (0, 0)))

    n_mod = w_mod.shape[2]
    b_mod_cols = lax.dynamic_slice_in_dim(b_mod, mq * n_mod, n_mod, axis=1).reshape(depth, 1, n_mod)
    mod_part = _mod_fwd(c_all, w_mod, b_mod_cols, name="mod_fwd")
    mod_all = _all_gather_small(mod_part.reshape(depth * N_DEV * bsz, n_mod), name="gather_mod",
                                comm=pending.take(MEDIUM, True))
    mod_all = mod_all.reshape(N_DEV, depth, N_DEV * bsz, n_mod)[0::2]
    mod = lax.dynamic_slice_in_dim(mod_all, me * bsz, bsz, axis=2)
    mod = jnp.transpose(mod, (1, 2, 0, 3)).reshape(depth, bsz, N_MOD, 1, d)
    mods = [[mod[i][:, k] for k in range(N_MOD)] for i in range(depth)]

    saved = []
    xc = x
    for i in range(depth):
        j = i // 2
        sh1, sc1, g1, sh2, sc2, g2 = mods[i]
        h1 = _normmod(xc, norm_mix[i][None], sc1, sh1, name="normmod")
        if i % 2 == 0:
            wq, wo = weight(("qkv", j)), weight(("o", j))
            qkv = _mm(h1.reshape(t, d), wq, bias=b_qkv[j], tn=qkv_dim, name="mm_qkv",
                      comm=pending.take(SMALL, True)).reshape(bsz, s, qkv_dim)
            mix = _attn_fwd(qkv, sinks[j], name="attn_fwd")
            y1, x1 = _mm(mix.reshape(t, d), wo, bias=b_o[j], epi="resid", resid=xc.reshape(t, d), gate=g1, seq=s,
                         tn=d, name="mm_out", comm=pending.take(SMALL, True))
            extra = (qkv, mix)
        else:
            wp1, wp2 = weight(("pw1", j)), weight(("pw2", j))
            u = _mm(h1.reshape(t, d), wp1, bias=b_pw1_f[j], out_dtype=F32, tn=d, name="mm_pw1",
                    comm=pending.take(MEDIUM, True)).reshape(bsz, s, 2 * d)
            mix, conv_v = _conv_fwd(u, w_dw_f[j], b_dw_f[j][None], ln_g_f[j][None], ln_b_f[j][None], name="conv_fwd")
            y1, x1 = _mm(mix.reshape(t, d), wp2, bias=b_pw2_f[j], epi="resid", resid=xc.reshape(t, d), gate=g1, seq=s,
                         tn=d, name="mm_out", comm=pending.take(SMALL, True))
            extra = (u, mix, conv_v)
        x1 = x1.reshape(bsz, s, d)
        h2 = _normmod(x1, norm_mlp[i][None], sc2, sh2, name="normmod")
        act, slope = _mm(h2.reshape(t, d), weight(("up", i)), epi="relu2", tn=d, name="mm_up", comm=pending.take(LARGE, True))
        y2, x2 = _mm(act, weight(("down", i)), epi="resid", resid=x1.reshape(t, d), gate=g2, seq=s, name="mm_down",
                     comm=pending.take(LARGE, True))
        saved.append((xc, h1, extra, y1, x1, h2, act, y2, slope))
        xc = x2.reshape(bsz, s, d)
    pending.flush(LARGE)

    last_y2, last_g2 = saved[-1][7].reshape(bsz, s, d), mods[-1][5]
    dx, loss_cols, d_final, dyb, dg2, _ = _loss_head(xc, loss_target, final_norm[None], (last_y2, last_g2), name="loss_head")
    loss = lax.psum(0.5 / d * jnp.sum(loss_cols), ("x", "y", "c"))

    totals = {}

    def reduce_scatter(name, layer, n_layers, grad):
        view = grad.reshape(N_CHIPS, 2, grad.shape[1] // 2, grad.shape[2])
        half_shape = (N_CHIPS, 1) + view.shape[2:]

        def scatter(outs):
            chip = _add_half(view, outs[0], my_c, name="rs_chipsum")
            copies = [(lambda ins, outs, pos, kind=kind: ins[0].at[pl.ds(_peer_shard(kind, pos[0], pos[1]), 1)],
                       lambda ins, outs, pos, k=k: outs[0].at[pl.ds(k, 1)],
                       lambda ins, outs, pos, k=k: outs[0].at[pl.ds(k, 1)], kind) for k, kind in enumerate(CHIP_KINDS)]

            def total(outs):
                totals[name] = _add_pieces(chip, outs[0], my_qc, totals.get(name), layer, n_layers, name="rs_total")

            pending.add(_Plan("ici", 3 * _nbytes(chip.shape[1:], BF16) / MB, [chip],
                              [jax.ShapeDtypeStruct((3,) + chip.shape[1:], BF16)], copies, then=total))

        pending.add(_Plan("d2d", _nbytes(half_shape, F32) / MB, [view], [jax.ShapeDtypeStruct(half_shape, F32)],
                          [(lambda ins, outs, pos: ins[0].at[:, pl.ds(1 - pos[2], 1)],
                            lambda ins, outs, pos: outs[0], lambda ins, outs, pos: outs[0], "c")], then=scatter))

    dmods, small = [None] * depth, {}
    for i in reversed(range(depth)):
        j = i // 2
        xin, h1, extra, y1, x1, h2, act, _, slope = saved[i]
        sh1, sc1, g1, sh2, sc2, g2 = mods[i]
        dyb = dyb.reshape(t, d)
        reduce_scatter("down", i, depth,
                       _mm_tn(act, dyb, name="dw_down", comm=pending.take(LARGE)).reshape(N_CHIPS, -1, d))
        dup = _mm(dyb, weights["down", i], nt=True, epi="dact", act=slope, tn=d, name="mm_dact", comm=pending.take(LARGE))
        reduce_scatter("up", i, depth,
                       _mm_tn(h2.reshape(t, d), dup, col_shards=N_CHIPS, name="dw_up", comm=pending.take(LARGE)))
        dh2 = _mm(dup, weights["up", i], nt=True, out_dtype=F32, name="mm_dh2", comm=pending.take(LARGE)).reshape(bsz, s, d)
        dx1, p2, dsh2, dyb, dg1, sdx = _normmod_bwd(x1, dh2, dx, norm_mlp[i][None], sc2, (y1.reshape(bsz, s, d), g1),
                                                    name="normmod_bwd")
        dyb = dyb.reshape(t, d)
        d_bias_out = jnp.sum(g1 * sdx, axis=(0, 1))
        if i % 2 == 0:
            qkv, mix = extra
            small["b_o", j] = d_bias_out
            reduce_scatter("o", j, n_attn,
                           _mm_tn(mix.reshape(t, d), dyb, name="dw_sq", comm=pending.take(SMALL)).reshape(N_CHIPS, -1, d))
            dmix = _mm(dyb, weights["o", j], nt=True, tn=d, name="mm_dmix", comm=pending.take(SMALL)).reshape(bsz, s, d)
            dqkv, d_bqkv, d_sink = _attn_bwd(qkv, dmix, sinks[j], name="attn_bwd")
            small["b_qkv", j], small["sinks", j] = d_bqkv[0], d_sink[:, 0]
            dqkv = dqkv.reshape(t, qkv_dim)
            dwq = _mm_tn(h1.reshape(t, d), dqkv, tn=qkv_dim, name="dw_qkv", comm=pending.take(SMALL))
            reduce_scatter("qkv", j, n_attn, jnp.transpose(dwq.reshape(d, N_CHIPS, -1), (1, 0, 2)))
            dh1 = _mm(dqkv, weights["qkv", j], nt=True, out_dtype=F32, tn=d, name="mm_dh1a", comm=pending.take(SMALL))
        else:
            u, mix, conv_v = extra
            small["b_pw2", j] = d_bias_out
            reduce_scatter("pw2", j, n_conv,
                           _mm_tn(mix.reshape(t, d), dyb, name="dw_sq", comm=pending.take(SMALL)).reshape(N_CHIPS, -1, d))
            dz = _mm(dyb, weights["pw2", j], nt=True, out_dtype=F32, tn=d, name="mm_dz", comm=pending.take(SMALL)).reshape(bsz, s, d)
            dv, d_lng, d_lnb, d_bdw = _conv_bwd_ln(dz, conv_v, ln_g_f[j][None], ln_b_f[j][None], name="conv_bwd_ln")
            du, d_bpw1, d_wdw = _conv_bwd_taps(dv, u, w_dw_f[j], name="conv_bwd_taps")
            small["ln_g", j], small["ln_b", j], small["b_dw", j] = d_lng[0], d_lnb[0], d_bdw[0]
            small["b_pw1", j], small["w_dw", j] = d_bpw1[0], jnp.sum(d_wdw[:CONV_WIDTH], axis=1)
            du = du.reshape(t, 2 * d)
            reduce_scatter("pw1", j, n_conv,
                           _mm_tn(h1.reshape(t, d), du, col_shards=N_CHIPS, name="dw_pw1", comm=pending.take(MEDIUM)))
            dh1 = _mm(du, weights["pw1", j], nt=True, out_dtype=F32, tn=d, name="mm_dh1c", comm=pending.take(MEDIUM))
        below = (saved[i - 1][7].reshape(bsz, s, d), mods[i - 1][5]) if i > 0 else None
        dx, p1, dsh1, *gate_grads = _normmod_bwd(xin, dh1.reshape(bsz, s, d), dx1, norm_mix[i][None], sc1, below,
                                                 name="normmod_bwd")
        small["norm_mix", i] = jnp.sum((1.0 + sc1) * p1, axis=(0, 1))
        small["norm_mlp", i] = jnp.sum((1.0 + sc2) * p2, axis=(0, 1))
        dmods[i] = jnp.concatenate([dsh1, norm_mix[i] * p1, dg1, dsh2, norm_mlp[i] * p2, dg2], axis=1)
        if i > 0:
            dyb, dg2, _ = gate_grads
    grad_x = dx

    small_names = ([("norm_mix", i) for i in range(depth)] + [("norm_mlp", i) for i in range(depth)]
                   + [(nm, j) for nm in ("b_qkv", "b_o", "sinks") for j in range(n_attn)]
                   + [(nm, j) for nm in ("b_pw1", "w_dw", "b_dw", "ln_g", "ln_b", "b_pw2") for j in range(n_conv)])
    small_list = [small[k] for k in small_names] + [d_final[0]]
    small_pack = _pack(small_list, d)
    dmod_rows = jnp.stack(dmods).reshape(depth * bsz * N_MOD, d)
    n_small = small_pack.shape[0]
    gathered = _all_gather_small(jnp.concatenate([small_pack, _pack([dmod_rows], d)], axis=0), name="gather_small",
                                 comm=pending.take(LARGE, True))
    gathered = gathered.reshape(N_DEV, -1, d)
    dmod_all = gathered[:, n_small:n_small + depth * bsz * N_MOD].reshape(N_DEV, depth, bsz, N_MOD, d)
    small_sum, g_b_mod = _sum_devices(gathered[:, :n_small], dmod_all, name="sum_devices")
    small_tot = dict(zip(small_names + ["final_norm"], _unpack(small_sum, [a.shape for a in small_list])))
    stacked = lambda nm, count: jnp.stack([small_tot[nm, k] for k in range(count)])
    g_norm_mix, g_norm_mlp = stacked("norm_mix", depth), stacked("norm_mlp", depth)
    g_b_qkv, g_b_o, g_sinks = stacked("b_qkv", n_attn), stacked("b_o", n_attn), stacked("sinks", n_attn)
    g_final = small_tot["final_norm"]
    g_b_mod = g_b_mod.reshape(depth, N_MOD * d)
    shard_cols = lambda g: lax.dynamic_slice_in_dim(g, mq * (g.shape[-1] // N_CHIPS), g.shape[-1] // N_CHIPS, axis=g.ndim - 1)
    g_b_pw1, g_w_dw, g_b_dw, g_ln_g, g_ln_b, g_b_pw2 = [shard_cols(stacked(nm, n_conv))
                                                        for nm in ("b_pw1", "w_dw", "b_dw", "ln_g", "ln_b", "b_pw2")]

    dmod_cols = jnp.transpose(dmod_all, (1, 0, 2, 3, 4)).reshape(depth, N_DEV * bsz, N_MOD * d)
    dmod_cols = lax.dynamic_slice_in_dim(dmod_cols, mq * n_mod, n_mod, axis=2)
    g_w_mod = _mod_bwd(c_all, dmod_cols, name="mod_bwd")

    def adam(w, g, m, v, name, comm=None):
        two_d = lambda a: a.reshape(-1, a.shape[-1])
        return [r.reshape(w.shape) for r in _adamw(two_d(w), two_d(g), two_d(m), two_d(v), name=name, comm=comm)]

    results = {"w_mod": (g_w_mod,) + tuple(adam(w_mod, g_w_mod, m_w_mod, v_w_mod, "adamw", pending.take(LARGE, True)))}

    pending.flush(LARGE)
    names = ["qkv", "o", "pw1", "pw2", "up", "down"]
    bufs = [totals[nm] for nm in names]
    copies = []
    for a, buf in enumerate(bufs):
        for layer in range(buf.shape[0]):
            half = lambda ref, h, layer=layer: ref.at[pl.ds(layer, 1), pl.ds(h, 1)]
            copies.append((lambda ins, outs, pos, a=a, half=half: half(ins[a], pos[2]),
                           lambda ins, outs, pos, a=a, half=half: half(outs[a], pos[2]),
                           lambda ins, outs, pos, a=a, half=half: half(outs[a], 1 - pos[2]), "c"))
    shared = {}
    _exchange(_Plan("d2d", 0.0, bufs, [jax.ShapeDtypeStruct(b.shape, F32) for b in bufs], copies,
                    {a: a for a in range(len(bufs))}, lambda outs: shared.update(zip(names, outs))), name="rs_share")
    g_w_qkv, g_w_o, g_w_pw1, g_w_pw2, g_w_up, g_w_down = [
        shared[nm].reshape(shared[nm].shape[0], -1, shared[nm].shape[3]) for nm in names]

    for nm, w, g, m, v in (("w_qkv", w_qkv, g_w_qkv, m_w_qkv, v_w_qkv),
                           ("w_o", w_o, g_w_o, m_w_o, v_w_o), ("w_pw1", w_pw1, g_w_pw1, m_w_pw1, v_w_pw1),
                           ("w_pw2", w_pw2, g_w_pw2, m_w_pw2, v_w_pw2), ("w_up", w_up, g_w_up, m_w_up, v_w_up),
                           ("w_down", w_down, g_w_down, m_w_down, v_w_down)):
        results[nm] = (g,) + tuple(adam(w, g, m, v, "adamw"))
    small_w = dict(b_mod=(b_mod, g_b_mod, m_b_mod, v_b_mod), norm_mix=(norm_mix, g_norm_mix, m_norm_mix, v_norm_mix),
                   norm_mlp=(norm_mlp, g_norm_mlp, m_norm_mlp, v_norm_mlp), b_qkv=(b_qkv, g_b_qkv, m_b_qkv, v_b_qkv),
                   b_o=(b_o, g_b_o, m_b_o, v_b_o), sinks=(sinks, g_sinks, m_sinks, v_sinks),
                   b_pw1=(b_pw1, g_b_pw1, m_b_pw1, v_b_pw1), w_dw=(w_dw, g_w_dw, m_w_dw, v_w_dw),
                   b_dw=(b_dw, g_b_dw, m_b_dw, v_b_dw), conv_ln_g=(conv_ln_g, g_ln_g, m_conv_ln_g, v_conv_ln_g),
                   conv_ln_b=(conv_ln_b, g_ln_b, m_conv_ln_b, v_conv_ln_b), b_pw2=(b_pw2, g_b_pw2, m_b_pw2, v_b_pw2),
                   final_norm=(final_norm, g_final, m_final_norm, v_final_norm))
    names = list(small_w)
    deltas, new_ms, new_vs = _adamw_packed(*[[small_w[nm][k] for nm in names] for k in range(4)], d, name="adamw_small")
    for k, nm in enumerate(names):
        results[nm] = (small_w[nm][1], deltas[k], new_ms[k], new_vs[k])

    weight_order = ["w_mod", "b_mod", "norm_mix", "norm_mlp", "w_qkv", "b_qkv", "w_o", "b_o", "sinks", "w_pw1", "b_pw1",
                    "w_dw", "b_dw", "conv_ln_g", "conv_ln_b", "w_pw2", "b_pw2", "w_up", "w_down", "final_norm"]
    return (loss, grad_x, *[results[nm][0] for nm in weight_order], *[results[nm][1] for nm in weight_order],
            *[results[nm][2] for nm in weight_order], *[results[nm][3] for nm in weight_order])
```

```python
import functools

import jax
import jax.numpy as jnp
from jax import lax
from jax.experimental import pallas as pl
from jax.experimental.pallas import tpu as pltpu

F32, BF16 = jnp.float32, jnp.bfloat16
MESH = pl.DeviceIdType.MESH
N_CHIPS = 4
N_DEV = 8
LANES = 128
SUBLANES = 8
VMEM_LIMIT = 48 * 1024 * 1024

NORM_EPS = 1e-6
HEAD_DIM = 64
N_KV_HEADS = 2
WINDOW = 128
CONV_WIDTH = 31
CONV_HALO = 32
CONV_ROWS = 32
N_MOD = 6

ADAM_LR, ADAM_B1, ADAM_B2, ADAM_EPS, ADAM_WD, ADAM_STEP = 0.001, 0.9, 0.999, 1e-08, 0.01, 10

HBM_SPEC = pl.BlockSpec(memory_space=pltpu.HBM)
VMEM_SPEC = pl.BlockSpec(memory_space=pltpu.VMEM)
SMEM_SPEC = pl.BlockSpec(memory_space=pltpu.SMEM)


def _params(*sem):
    return pltpu.CompilerParams(dimension_semantics=sem or None, vmem_limit_bytes=VMEM_LIMIT)


def _row_tile(rows, width_bytes, target=1 << 20):
    t = rows
    while t % 2 == 0 and t > SUBLANES and t * width_bytes > target:
        t //= 2
    return t


CHIP_KINDS = ("x", "y", "xy")


def _position():
    x, y, c = lax.axis_index("x"), lax.axis_index("y"), lax.axis_index("c")
    return x, y, c, 2 * x + y


def _peer(kind, x, y, c):
    return {"c": (x, y, 1 - c), "x": (1 - x, y, c), "y": (x, 1 - y, c), "xy": (1 - x, 1 - y, c)}[kind]


def _peer_shard(kind, x, y):
    px, py, _ = _peer(kind, x, y, 0)
    return 2 * px + py


class _Plan:
    def __init__(self, link, cost, operands, out_shapes, copies, aliases=None, then=None):
        self.link, self.cost = link, cost
        self.operands, self.out_shapes, self.copies = list(operands), list(out_shapes), list(copies)
        self.aliases, self.then = dict(aliases or {}), then


def _merge(plans):
    operands, out_shapes, copies, aliases, thens = [], [], [], {}, []
    for p in plans:
        i0, o0 = len(operands), len(out_shapes)
        i1, o1 = i0 + len(p.operands), o0 + len(p.out_shapes)

        def shifted(f, i0=i0, i1=i1, o0=o0, o1=o1):
            return lambda ins, outs, pos: f(ins[i0:i1], outs[o0:o1], pos)

        copies += [(shifted(src), shifted(dst), shifted(land), kind) for src, dst, land, kind in p.copies]
        aliases.update({i0 + k: o0 + v for k, v in p.aliases.items()})
        operands += p.operands
        out_shapes += p.out_shapes
        thens.append((p.then, o0, o1))

    def then(outs):
        for f, o0, o1 in thens:
            if f is not None:
                f(outs[o0:o1])

    return _Plan("mixed", sum(p.cost for p in plans), operands, out_shapes, copies, aliases, then)


def _call(body, *, name, out_shape, operands, grid=(), in_specs=(), out_specs=(), scratch_shapes=(), sem=(), comm=None):
    single = not isinstance(out_shape, (tuple, list))
    out_shape = [out_shape] if single else list(out_shape)
    out_specs = [out_specs] if single else list(out_specs)
    if comm is None:
        res = pl.pallas_call(body, out_shape=out_shape, grid=grid, in_specs=list(in_specs), out_specs=out_specs,
                             scratch_shapes=list(scratch_shapes), name=name, compiler_params=_params(*sem))(*operands)
        return res[0] if single else res
    n_in, n_out, n_scr = len(operands), len(out_shape), len(scratch_shapes)
    c_in, c_out, n_cp = len(comm.operands), len(comm.out_shapes), len(comm.copies)

    def wrapped(*refs):
        ins, refs = refs[:n_in], refs[n_in:]
        cins, refs = refs[:c_in], refs[c_in:]
        outs, refs = refs[:n_out], refs[n_out:]
        couts, refs = refs[:c_out], refs[c_out:]
        scr, (send_sems, recv_sems) = refs[:n_scr], refs[n_scr:]

        def descriptors():
            pos = _position()
            sends, lands = [], []
            for k, (src, dst, landing, kind) in enumerate(comm.copies):
                common = dict(send_sem=send_sems.at[k], recv_sem=recv_sems.at[k],
                              device_id=_peer(kind, *pos[:3]), device_id_type=MESH)
                sends.append(pltpu.make_async_remote_copy(src_ref=src(cins, couts, pos), dst_ref=dst(cins, couts, pos), **common))
                lands.append(pltpu.make_async_remote_copy(src_ref=src(cins, couts, pos), dst_ref=landing(cins, couts, pos), **common))
            return sends, lands

        def start():
            for cp in descriptors()[0]:
                cp.start()

        def finish():
            sends, lands = descriptors()
            for cp in lands:
                cp.wait_recv()
            for cp in sends:
                cp.wait_send()

        if not grid:
            start()
            body(*ins, *outs, *scr)
            finish()
        else:
            ids = [pl.program_id(ax) for ax in range(len(grid))]
            first, last = ids[0] == 0, ids[0] == grid[0] - 1
            for ax in range(1, len(grid)):
                first, last = first & (ids[ax] == 0), last & (ids[ax] == grid[ax] - 1)
            pl.when(first)(start)
            body(*ins, *outs, *scr)
            pl.when(last)(finish)

    res = pl.pallas_call(wrapped, out_shape=out_shape + comm.out_shapes, grid=grid,
                         in_specs=list(in_specs) + [HBM_SPEC] * c_in, out_specs=out_specs + [HBM_SPEC] * c_out,
                         scratch_shapes=list(scratch_shapes) + [pltpu.SemaphoreType.DMA((n_cp,)), pltpu.SemaphoreType.DMA((n_cp,))],
                         input_output_aliases={n_in + k: n_out + v for k, v in comm.aliases.items()},
                         name=name, compiler_params=_params(*["arbitrary"] * len(grid)))(*operands, *comm.operands)
    if comm.then is not None:
        comm.then(res[n_out:])
    return res[0] if single else res[:n_out]


def _exchange(plan, *, name):
    _call(lambda: None, name=name, out_shape=[], operands=[], comm=plan)


def _mm(a, b, *, name, nt=False, tm=1024, tn=512, epi="plain", out_dtype=BF16,
        bias=None, act=None, resid=None, gate=None, seq=None, norm=None, comm=None):
    m, k = a.shape
    n = b.shape[0] if nt else b.shape[1]
    tm, tn = min(tm, m, seq or m), min(tn, n)
    assert m % tm == 0 and n % tn == 0
    dims = (((1,), (1,)), ((), ())) if nt else (((1,), (0,)), ((), ()))
    tile = pl.BlockSpec((tm, tn), lambda j, i: (i, j))
    operands = [a, b]
    in_specs = [pl.BlockSpec((tm, k), lambda j, i: (i, 0)),
                pl.BlockSpec((tn, k), lambda j, i: (j, 0)) if nt else pl.BlockSpec((k, tn), lambda j, i: (0, j))]
    if bias is not None:
        operands.append(bias.reshape(1, n))
        in_specs.append(pl.BlockSpec((1, tn), lambda j, i: (0, j)))
    if epi == "dact":
        operands.append(act)
        in_specs.append(tile)
    if epi == "resid":
        assert seq % tm == 0
        per_ex = pl.BlockSpec((None, 1, tn), lambda j, i: (i * tm // seq, 0, j))
        operands += [resid, gate]
        in_specs += [tile, per_ex]
        out_shape = (jax.ShapeDtypeStruct((m, n), BF16), jax.ShapeDtypeStruct((m, n), F32))
        out_specs = (tile, tile)
        if norm is not None:
            assert tn == n
            operands += list(norm)
            in_specs += [pl.BlockSpec((1, tn), lambda j, i: (0, j)), per_ex, per_ex]
            out_shape += (jax.ShapeDtypeStruct((m, n), BF16),)
            out_specs += (tile,)
    elif epi == "relu2":
        out_shape = (jax.ShapeDtypeStruct((m, n), BF16), jax.ShapeDtypeStruct((m, n), BF16))
        out_specs = (tile, tile)
    else:
        out_shape = jax.ShapeDtypeStruct((m, n), out_dtype)
        out_specs = tile

    def body(*refs):
        it = iter(refs)
        a_ref, b_ref = next(it), next(it)
        acc = lax.dot_general(a_ref[...], b_ref[...], dims, preferred_element_type=F32)
        if bias is not None:
            acc = acc + next(it)[...]
        if epi == "plain":
            next(it)[...] = acc.astype(out_dtype)
        elif epi == "relu2":
            r = jnp.maximum(acc, 0.0)
            next(it)[...] = (r * r).astype(BF16)
            next(it)[...] = (2.0 * r).astype(BF16)
        elif epi == "dact":
            slope_ref = next(it)
            next(it)[...] = (acc * slope_ref[...].astype(F32)).astype(out_dtype)
        else:
            resid_ref, gate_ref = next(it), next(it)
            norm_refs = [next(it) for _ in (norm or ())]
            y_ref, x_ref = next(it), next(it)
            y_ref[...] = acc.astype(BF16)
            x_new = resid_ref[...] + gate_ref[...] * acc
            x_ref[...] = x_new
            if norm is not None:
                g_ref, sc_ref, sh_ref = norm_refs
                r = lax.rsqrt(jnp.mean(x_new * x_new, axis=-1, keepdims=True) + NORM_EPS)
                next(it)[...] = (x_new * r * g_ref[...] * (1.0 + sc_ref[...]) + sh_ref[...]).astype(BF16)

    return _call(body, name=name, out_shape=out_shape, operands=operands, grid=(n // tn, m // tm), in_specs=in_specs,
                 out_specs=out_specs, sem=("parallel", "parallel"), comm=comm)


def _mm_tn(a, b, *, name, tm=1024, tn=1024, tk=2048, col_shards=None, comm=None):
    t, m = a.shape
    n = b.shape[1]
    tm, tk = min(tm, m), min(tk, t)
    if col_shards is None:
        tn = min(tn, n)
        out_shape = jax.ShapeDtypeStruct((m, n), F32)
        out_spec = pl.BlockSpec((tm, tn), lambda i, j, k: (i, j))
    else:
        per = n // col_shards
        tn = min(tn, per)
        assert per % tn == 0
        out_shape = jax.ShapeDtypeStruct((col_shards, m, per), F32)
        out_spec = pl.BlockSpec((None, tm, tn), lambda i, j, k: (j // (per // tn), i, j % (per // tn)))
    assert m % tm == 0 and n % tn == 0 and t % tk == 0

    def body(a_ref, b_ref, o_ref):
        @pl.when(pl.program_id(2) == 0)
        def _():
            o_ref[...] = jnp.zeros_like(o_ref)

        o_ref[...] += lax.dot_general(a_ref[...], b_ref[...], (((0,), (0,)), ((), ())),
                                      preferred_element_type=F32)

    return _call(body, name=name, out_shape=out_shape, operands=[a, b], grid=(m // tm, n // tn, t // tk),
                 in_specs=[pl.BlockSpec((tk, tm), lambda i, j, k: (k, i)), pl.BlockSpec((tk, tn), lambda i, j, k: (k, j))],
                 out_specs=out_spec, sem=("parallel", "parallel", "arbitrary"), comm=comm)


def _normmod(x, gamma, sc, sh, *, name):
    bsz, s, d = x.shape
    ts = _row_tile(s, d * 4, 2 << 20)

    def body(x_ref, g_ref, sc_ref, sh_ref, o_ref):
        xf = x_ref[...]
        r = lax.rsqrt(jnp.mean(xf * xf, axis=-1, keepdims=True) + NORM_EPS)
        o_ref[...] = (xf * r * g_ref[...] * (1.0 + sc_ref[...]) + sh_ref[...]).astype(BF16)

    row = pl.BlockSpec((None, ts, d), lambda b, i: (b, i, 0))
    per_ex = pl.BlockSpec((None, 1, d), lambda b, i: (b, 0, 0))
    return pl.pallas_call(body, out_shape=jax.ShapeDtypeStruct(x.shape, BF16), grid=(bsz, s // ts),
                          in_specs=[row, pl.BlockSpec((1, d), lambda b, i: (0, 0)), per_ex, per_ex],
                          out_specs=row, name=name, compiler_params=_params("parallel", "parallel"))(x, gamma, sc, sh)


def _gate_grads(dxv, y_ref, gate_ref, dy_ref, dg_ref, sdx_ref):
    dy_ref[...] = (gate_ref[...] * dxv).astype(BF16)

    @pl.when(pl.program_id(1) == 0)
    def _():
        dg_ref[...] = jnp.zeros_like(dg_ref)
        sdx_ref[...] = jnp.zeros_like(sdx_ref)

    dg_ref[...] += jnp.sum(dxv * y_ref[...].astype(F32), axis=0, keepdims=True)
    sdx_ref[...] += jnp.sum(dxv, axis=0, keepdims=True)


def _normmod_bwd(x, dh, dres, gamma, sc, producer=None, *, name):
    bsz, s, d = x.shape
    ts = _row_tile(s, d * 4, 1 << 20)

    def body(x_ref, dh_ref, dres_ref, g_ref, sc_ref, *rest):
        dx_ref, p_ref, dsh_ref = rest[-6:-3] if producer else rest
        xf = x_ref[...]
        r = lax.rsqrt(jnp.mean(xf * xf, axis=-1, keepdims=True) + NORM_EPS)
        xhat = xf * r
        dh_v = dh_ref[...]
        dxhat = dh_v * (g_ref[...] * (1.0 + sc_ref[...]))
        dxv = dres_ref[...] + r * (dxhat - xhat * jnp.mean(dxhat * xhat, axis=-1, keepdims=True))
        dx_ref[...] = dxv

        @pl.when(pl.program_id(1) == 0)
        def _():
            p_ref[...] = jnp.zeros_like(p_ref)
            dsh_ref[...] = jnp.zeros_like(dsh_ref)

        p_ref[...] += jnp.sum(dh_v * xhat, axis=0, keepdims=True)
        dsh_ref[...] += jnp.sum(dh_v, axis=0, keepdims=True)
        if producer:
            _gate_grads(dxv, rest[0], rest[1], *rest[-3:])

    row = pl.BlockSpec((None, ts, d), lambda b, i: (b, i, 0))
    per_ex = pl.BlockSpec((None, 1, d), lambda b, i: (b, 0, 0))
    vec = jax.ShapeDtypeStruct((bsz, 1, d), F32)
    gate_in, gate_out, gate_specs = (), (), ()
    if producer:
        gate_in, gate_out, gate_specs = (row, per_ex), (jax.ShapeDtypeStruct(x.shape, BF16), vec, vec), (row, per_ex, per_ex)
    return pl.pallas_call(body, out_shape=(jax.ShapeDtypeStruct(x.shape, F32), vec, vec) + gate_out, grid=(bsz, s // ts),
                          in_specs=[row, row, row, pl.BlockSpec((1, d), lambda b, i: (0, 0)), per_ex, *gate_in],
                          out_specs=(row, per_ex, per_ex) + gate_specs, name=name,
                          compiler_params=_params("parallel", "arbitrary"))(x, dh, dres, gamma, sc, *(producer or ()))


def _loss_head(x, target, gamma, producer, *, name):
    bsz, s, d = x.shape
    ts = _row_tile(s, d * 4, 1 << 20)

    def body(x_ref, t_ref, g_ref, y_ref, gate_ref, dx_ref, loss_ref, dg_ref, *gate_refs):
        xf = x_ref[...]
        r = lax.rsqrt(jnp.mean(xf * xf, axis=-1, keepdims=True) + NORM_EPS)
        xhat = xf * r
        err = xhat * g_ref[...] - t_ref[...]
        dy = err * (1.0 / d)
        dxhat = dy * g_ref[...]
        dxv = r * (dxhat - xhat * jnp.mean(dxhat * xhat, axis=-1, keepdims=True))
        dx_ref[...] = dxv

        @pl.when((pl.program_id(0) == 0) & (pl.program_id(1) == 0))
        def _():
            loss_ref[...] = jnp.zeros_like(loss_ref)
            dg_ref[...] = jnp.zeros_like(dg_ref)

        loss_ref[...] += jnp.sum(err * err, axis=0, keepdims=True)
        dg_ref[...] += jnp.sum(dy * xhat, axis=0, keepdims=True)
        _gate_grads(dxv, y_ref, gate_ref, *gate_refs)

    row = pl.BlockSpec((None, ts, d), lambda b, i: (b, i, 0))
    one = pl.BlockSpec((1, d), lambda b, i: (0, 0))
    per_ex = pl.BlockSpec((None, 1, d), lambda b, i: (b, 0, 0))
    vec, ex_vec = jax.ShapeDtypeStruct((1, d), F32), jax.ShapeDtypeStruct((bsz, 1, d), F32)
    return pl.pallas_call(body, out_shape=(jax.ShapeDtypeStruct(x.shape, F32), vec, vec,
                                           jax.ShapeDtypeStruct(x.shape, BF16), ex_vec, ex_vec), grid=(bsz, s // ts),
                          in_specs=[row, row, one, row, per_ex], out_specs=(row, one, one, row, per_ex, per_ex), name=name,
                          compiler_params=_params("arbitrary", "arbitrary"))(x, target, gamma, *producer)


def _alibi_slope(h, n_heads):
    return 2.0 ** (-8.0 * (h + 1) / n_heads)


def _attn_masks(first_block):
    qi = lax.broadcasted_iota(jnp.int32, (WINDOW, 2 * WINDOW), 0)
    ki = lax.broadcasted_iota(jnp.int32, (WINDOW, 2 * WINDOW), 1)
    dist = qi + WINDOW - ki
    first_key = jnp.where(first_block, WINDOW, 0)
    valid = (dist >= 0) & (dist < WINDOW) & (ki >= first_key)
    return dist.astype(F32), valid


def _dup_halves(span, kv, left):
    f = span.astype(F32)
    rolled = pltpu.roll(f, HEAD_DIM, axis=1)
    out = jnp.where(left, f, rolled) if kv == 0 else jnp.where(left, rolled, f)
    return out.astype(BF16)


def _attn_probs(s, h, n_heads, distf, valid, sink):
    s = s * (HEAD_DIM ** -0.5) - _alibi_slope(h, n_heads) * distf
    s = jnp.where(valid, s, -1e30)
    m = jnp.maximum(jnp.max(s, axis=-1, keepdims=True), sink)
    e = jnp.exp(s - m)
    e_sink = jnp.exp(sink - m)
    inv = 1.0 / (jnp.sum(e, axis=-1, keepdims=True) + e_sink)
    return e * inv, e_sink * inv


def _attn_specs(d, n_blocks, clamp):
    kcol = d // LANES
    cur = (lambda i: jnp.minimum(i, n_blocks - 1)) if clamp else (lambda i: i)
    prev = lambda i: jnp.maximum(cur(i) - 1, 0)
    kv = lambda col, blk: pl.BlockSpec((None, WINDOW, LANES), lambda b, i: (b, blk(i), col))
    return [pl.BlockSpec((None, WINDOW, d), lambda b, i: (b, cur(i), 0)),
            kv(kcol, prev), kv(kcol, cur), kv(kcol + 1, prev), kv(kcol + 1, cur)]


def _attn_fwd(qkv, sinks, *, name, comm=None):
    bsz, s, qkv_dim = qkv.shape
    d = qkv_dim - 2 * N_KV_HEADS * HEAD_DIM
    n_heads = d // HEAD_DIM
    group = n_heads // N_KV_HEADS
    pairs = group // 2
    n_blocks = s // WINDOW

    def body(q_ref, kp_ref, kc_ref, vp_ref, vc_ref, sink_ref, o_ref):
        left = lax.broadcasted_iota(jnp.int32, (1, LANES), 1) < HEAD_DIM
        distf, valid = _attn_masks(pl.program_id(1) == 0)
        kspan = jnp.concatenate([kp_ref[...], kc_ref[...]], axis=0)
        vspan = jnp.concatenate([vp_ref[...], vc_ref[...]], axis=0)
        for kv in range(N_KV_HEADS):
            kdup, vdup = _dup_halves(kspan, kv, left), _dup_halves(vspan, kv, left)
            res = []
            for par in range(2):
                keep = left if par == 0 else jnp.logical_not(left)
                cols = [pl.ds((kv * pairs + p) * LANES, LANES) for p in range(pairs)]
                lhs = jnp.concatenate([jnp.where(keep, q_ref[:, cl], jnp.zeros((), BF16)) for cl in cols], axis=0)
                sc = lax.dot_general(lhs, kdup, (((1,), (1,)), ((), ())), preferred_element_type=F32)
                probs = []
                for p in range(pairs):
                    h = kv * group + 2 * p + par
                    pr, _ = _attn_probs(sc[p * WINDOW:(p + 1) * WINDOW], h, n_heads, distf, valid, sink_ref[h])
                    probs.append(pr.astype(BF16))
                res.append(jnp.dot(jnp.concatenate(probs, axis=0), vdup, preferred_element_type=F32))
            for p in range(pairs):
                rows = slice(p * WINDOW, (p + 1) * WINDOW)
                o_ref[:, pl.ds((kv * pairs + p) * LANES, LANES)] = jnp.where(left, res[0][rows], res[1][rows]).astype(BF16)

    return _call(body, name=name, out_shape=jax.ShapeDtypeStruct((bsz, s, d), BF16), operands=[qkv] * 5 + [sinks],
                 grid=(bsz, n_blocks), in_specs=_attn_specs(d, n_blocks, False) + [SMEM_SPEC],
                 out_specs=pl.BlockSpec((None, WINDOW, d), lambda b, i: (b, i, 0)), sem=("parallel", "parallel"), comm=comm)


def _attn_bwd(qkv, do, sinks, *, name, comm=None):
    bsz, s, qkv_dim = qkv.shape
    d = qkv_dim - 2 * N_KV_HEADS * HEAD_DIM
    n_heads = d // HEAD_DIM
    group = n_heads // N_KV_HEADS
    pairs = group // 2
    n_blocks = s // WINDOW
    tn_dims = (((0,), (0,)), ((), ()))

    def body(q_ref, kp_ref, kc_ref, vp_ref, vc_ref, do_ref, sink_ref, dqkv_ref, colsum_ref, dsink_ref,
             dq_prev, dk_carry, dv_carry):
        b, i = pl.program_id(0), pl.program_id(1)
        left = lax.broadcasted_iota(jnp.int32, (1, LANES), 1) < HEAD_DIM

        @pl.when((b == 0) & (i == 0))
        def _():
            colsum_ref[...] = jnp.zeros_like(colsum_ref)
            dsink_ref[...] = jnp.zeros_like(dsink_ref)

        @pl.when(i == 0)
        def _():
            dqkv_ref[...] = jnp.zeros_like(dqkv_ref)
            dk_carry[...] = jnp.zeros_like(dk_carry)
            dv_carry[...] = jnp.zeros_like(dv_carry)

        @pl.when(i > 0)
        def _():
            dq_v = dq_prev[...]
            dqkv_ref[:, pl.ds(0, d)] = dq_v.astype(BF16)
            colsum_ref[:, pl.ds(0, d)] += jnp.sum(dq_v, axis=0, keepdims=True)

        @pl.when(i < n_blocks)
        def _():
            distf, valid = _attn_masks(i == 0)
            kspan = jnp.concatenate([kp_ref[...], kc_ref[...]], axis=0)
            vspan = jnp.concatenate([vp_ref[...], vc_ref[...]], axis=0)
            dk_blk, dv_blk = [], []
            for kv in range(N_KV_HEADS):
                kdup, vdup = _dup_halves(kspan, kv, left), _dup_halves(vspan, kv, left)
                dq_res, dk_sum, dv_sum = [], None, None
                for par in range(2):
                    keep = left if par == 0 else jnp.logical_not(left)
                    cols = [pl.ds((kv * pairs + p) * LANES, LANES) for p in range(pairs)]
                    zero = jnp.zeros((), BF16)
                    lhs = jnp.concatenate([jnp.where(keep, q_ref[:, cl], zero) for cl in cols], axis=0)
                    dol = jnp.concatenate([jnp.where(keep, do_ref[:, cl], zero) for cl in cols], axis=0)
                    sc = lax.dot_general(lhs, kdup, (((1,), (1,)), ((), ())), preferred_element_type=F32)
                    dp = lax.dot_general(dol, vdup, (((1,), (1,)), ((), ())), preferred_element_type=F32)
                    probs, dscores = [], []
                    for p in range(pairs):
                        h = kv * group + 2 * p + par
                        rows = slice(p * WINDOW, (p + 1) * WINDOW)
                        pr, p_sink = _attn_probs(sc[rows], h, n_heads, distf, valid, sink_ref[h])
                        delta = jnp.sum(pr * dp[rows], axis=-1, keepdims=True)
                        dscores.append((pr * (dp[rows] - delta) * (HEAD_DIM ** -0.5)).astype(BF16))
                        probs.append(pr.astype(BF16))
                        dsink_ref[pl.ds(h, 1), :] += jnp.zeros((1, LANES), F32) - jnp.sum(p_sink * delta)
                    ds_all = jnp.concatenate(dscores, axis=0)
                    p_all = jnp.concatenate(probs, axis=0)
                    dq_res.append(jnp.dot(ds_all, kdup, preferred_element_type=F32))
                    dk_par = lax.dot_general(ds_all, lhs, tn_dims, preferred_element_type=F32)
                    dv_par = lax.dot_general(p_all, dol, tn_dims, preferred_element_type=F32)
                    dk_sum = dk_par if dk_sum is None else dk_sum + dk_par
                    dv_sum = dv_par if dv_sum is None else dv_sum + dv_par
                for p in range(pairs):
                    rows = slice(p * WINDOW, (p + 1) * WINDOW)
                    dq_prev[:, pl.ds((kv * pairs + p) * LANES, LANES)] = jnp.where(left, dq_res[0][rows], dq_res[1][rows])
                dk_blk.append(dk_sum + pltpu.roll(dk_sum, HEAD_DIM, axis=1))
                dv_blk.append(dv_sum + pltpu.roll(dv_sum, HEAD_DIM, axis=1))
            dk_span = jnp.where(left, dk_blk[0], dk_blk[1])
            dv_span = jnp.where(left, dv_blk[0], dv_blk[1])
            dk_done = dk_carry[...] + dk_span[:WINDOW]
            dv_done = dv_carry[...] + dv_span[:WINDOW]
            dk_carry[...] = dk_span[WINDOW:]
            dv_carry[...] = dv_span[WINDOW:]

            @pl.when(i > 0)
            def _():
                dqkv_ref[:, pl.ds(d, LANES)] = dk_done.astype(BF16)
                dqkv_ref[:, pl.ds(d + LANES, LANES)] = dv_done.astype(BF16)
                colsum_ref[:, pl.ds(d, LANES)] += jnp.sum(dk_done, axis=0, keepdims=True)
                colsum_ref[:, pl.ds(d + LANES, LANES)] += jnp.sum(dv_done, axis=0, keepdims=True)

        @pl.when(i == n_blocks)
        def _():
            dk_done, dv_done = dk_carry[...], dv_carry[...]
            dqkv_ref[:, pl.ds(d, LANES)] = dk_done.astype(BF16)
            dqkv_ref[:, pl.ds(d + LANES, LANES)] = dv_done.astype(BF16)
            colsum_ref[:, pl.ds(d, LANES)] += jnp.sum(dk_done, axis=0, keepdims=True)
            colsum_ref[:, pl.ds(d + LANES, LANES)] += jnp.sum(dv_done, axis=0, keepdims=True)

    do_spec = pl.BlockSpec((None, WINDOW, d), lambda b, i: (b, jnp.minimum(i, n_blocks - 1), 0))
    out_shape = (jax.ShapeDtypeStruct((bsz, s, qkv_dim), BF16), jax.ShapeDtypeStruct((1, qkv_dim), F32),
                 jax.ShapeDtypeStruct((n_heads, LANES), F32))
    out_specs = (pl.BlockSpec((None, WINDOW, qkv_dim), lambda b, i: (b, jnp.maximum(i - 1, 0), 0)),
                 pl.BlockSpec((1, qkv_dim), lambda b, i: (0, 0)),
                 pl.BlockSpec((n_heads, LANES), lambda b, i: (0, 0)))
    return _call(body, name=name, out_shape=out_shape, operands=[qkv] * 5 + [do, sinks], grid=(bsz, n_blocks + 1),
                 in_specs=_attn_specs(d, n_blocks, True) + [do_spec, SMEM_SPEC], out_specs=out_specs,
                 scratch_shapes=[pltpu.VMEM((WINDOW, d), F32), pltpu.VMEM((WINDOW, LANES), F32), pltpu.VMEM((WINDOW, LANES), F32)],
                 sem=("arbitrary", "arbitrary"), comm=comm)


def _conv_tile(s):
    return min(256, s)


def _halo_specs(ts, width, s):
    per = ts // CONV_HALO
    prev = pl.BlockSpec((None, CONV_HALO, width), lambda b, i: (b, jnp.maximum(i * per - 1, 0), 0))
    nxt = pl.BlockSpec((None, CONV_HALO, width), lambda b, i: (b, jnp.minimum((i + 1) * per, s // CONV_HALO - 1), 0))
    cur = pl.BlockSpec((None, ts, width), lambda b, i: (b, i, 0))
    return prev, cur, nxt


def _glu(u, d):
    return u[:, :d] * jax.nn.sigmoid(u[:, d:])


def _store_shifted(shifted, value):
    rows = value.shape[0]
    shifted[0] = value
    for b in range(1, SUBLANES):
        shifted[b] = pltpu.roll(value, rows - b, axis=0)


def _at(shifted, row, n_rows, lanes):
    return shifted[row % SUBLANES, pl.ds(row - row % SUBLANES, n_rows), lanes]


def _taps(w_ref, shifted, out_ref, ts, d, offset):
    for r0 in range(0, ts, CONV_ROWS):
        for l0 in range(0, d, LANES):
            lanes = pl.ds(l0, LANES)
            acc = jnp.zeros((CONV_ROWS, LANES), F32)
            for j in range(CONV_WIDTH):
                acc = acc + w_ref[pl.ds(j, 1), lanes] * _at(shifted, r0 + offset(j), CONV_ROWS, lanes)
            out_ref[pl.ds(r0, CONV_ROWS), lanes] = acc


def _conv_fwd(u, w_dw, b_dw, ln_g, ln_b, *, name, comm=None):
    bsz, s, d2 = u.shape
    d = d2 // 2
    ts = _conv_tile(s)

    def body(up_ref, uc_ref, w_ref, bdw_ref, g_ref, b_ref, z_ref, v_ref, gbuf):
        halo = jnp.where(pl.program_id(1) > 0, _glu(up_ref[...], d), 0.0)
        _store_shifted(gbuf, jnp.concatenate([halo, _glu(uc_ref[...], d)], axis=0))
        _taps(w_ref, gbuf, v_ref, ts, d, lambda j: CONV_HALO - (CONV_WIDTH - 1) + j)
        v = v_ref[...] + bdw_ref[...]
        v_ref[...] = v
        mu = jnp.mean(v, axis=-1, keepdims=True)
        cen = v - mu
        rstd = lax.rsqrt(jnp.mean(cen * cen, axis=-1, keepdims=True) + NORM_EPS)
        ln = cen * rstd * g_ref[...] + b_ref[...]
        z_ref[...] = (ln * jax.nn.sigmoid(ln)).astype(BF16)

    prev, cur, _ = _halo_specs(ts, d2, s)
    one = pl.BlockSpec((1, d), lambda b, i: (0, 0))
    row = pl.BlockSpec((None, ts, d), lambda b, i: (b, i, 0))
    return _call(body, name=name, out_shape=(jax.ShapeDtypeStruct((bsz, s, d), BF16), jax.ShapeDtypeStruct((bsz, s, d), F32)),
                 operands=[u, u, w_dw, b_dw, ln_g, ln_b], grid=(bsz, s // ts),
                 in_specs=[prev, cur, pl.BlockSpec((CONV_HALO, d), lambda b, i: (0, 0)), one, one, one],
                 out_specs=(row, row), scratch_shapes=[pltpu.VMEM((SUBLANES, ts + CONV_HALO, d), F32)],
                 sem=("parallel", "parallel"), comm=comm)


def _conv_bwd_ln(dz, v, ln_g, ln_b, *, name):
    bsz, s, d = v.shape
    ts = _row_tile(s, d * 4, 1 << 20)

    def body(dz_ref, v_ref, g_ref, b_ref, dv_ref, dg_ref, db_ref, dbdw_ref):
        v_v = v_ref[...]
        mu = jnp.mean(v_v, axis=-1, keepdims=True)
        cen = v_v - mu
        rstd = lax.rsqrt(jnp.mean(cen * cen, axis=-1, keepdims=True) + NORM_EPS)
        vhat = cen * rstd
        ln = vhat * g_ref[...] + b_ref[...]
        sig = jax.nn.sigmoid(ln)
        dln = dz_ref[...] * (sig * (1.0 + ln * (1.0 - sig)))
        dvhat = dln * g_ref[...]
        dv = rstd * (dvhat - jnp.mean(dvhat, axis=-1, keepdims=True)
                     - vhat * jnp.mean(dvhat * vhat, axis=-1, keepdims=True))
        dv_ref[...] = dv

        @pl.when((pl.program_id(0) == 0) & (pl.program_id(1) == 0))
        def _():
            dg_ref[...] = jnp.zeros_like(dg_ref)
            db_ref[...] = jnp.zeros_like(db_ref)
            dbdw_ref[...] = jnp.zeros_like(dbdw_ref)

        dg_ref[...] += jnp.sum(dln * vhat, axis=0, keepdims=True)
        db_ref[...] += jnp.sum(dln, axis=0, keepdims=True)
        dbdw_ref[...] += jnp.sum(dv, axis=0, keepdims=True)

    row = pl.BlockSpec((None, ts, d), lambda b, i: (b, i, 0))
    one = pl.BlockSpec((1, d), lambda b, i: (0, 0))
    vec = jax.ShapeDtypeStruct((1, d), F32)
    return pl.pallas_call(body, out_shape=(jax.ShapeDtypeStruct(v.shape, F32), vec, vec, vec), grid=(bsz, s // ts),
                          in_specs=[row, row, one, one], out_specs=(row, one, one, one), name=name,
                          compiler_params=_params("arbitrary", "arbitrary"))(dz, v, ln_g, ln_b)


def _conv_bwd_taps(dv, u, w_dw, *, name, comm=None):
    bsz, s, d = dv.shape
    ts = _conv_tile(s)
    n_tiles = s // ts

    def body(dvc_ref, dvn_ref, up_ref, uc_ref, w_ref, du_ref, dbu_ref, dw_ref, gbuf, dvbuf, dglu):
        i = pl.program_id(1)

        @pl.when((pl.program_id(0) == 0) & (i == 0))
        def _():
            dbu_ref[...] = jnp.zeros_like(dbu_ref)
            dw_ref[...] = jnp.zeros_like(dw_ref)

        halo = jnp.where(i > 0, _glu(up_ref[...], d), 0.0)
        _store_shifted(gbuf, jnp.concatenate([halo, _glu(uc_ref[...], d)], axis=0))
        ahead = jnp.where(i < n_tiles - 1, dvn_ref[...], 0.0)
        _store_shifted(dvbuf, jnp.concatenate([dvc_ref[...], ahead], axis=0))
        _taps(w_ref, dvbuf, dglu, ts, d, lambda j: CONV_WIDTH - 1 - j)
        for l0 in range(0, d, LANES):
            lanes = pl.ds(l0, LANES)
            for j in range(CONV_WIDTH):
                acc = jnp.zeros((SUBLANES, LANES), F32)
                for r0 in range(0, ts, CONV_ROWS):
                    prod = dvc_ref[pl.ds(r0, CONV_ROWS), lanes] * _at(gbuf, r0 + CONV_HALO - (CONV_WIDTH - 1) + j, CONV_ROWS, lanes)
                    for k in range(0, CONV_ROWS, SUBLANES):
                        acc = acc + prod[k:k + SUBLANES]
                dw_ref[j, :, lanes] += acc
        u_v = uc_ref[...]
        a, sig = u_v[:, :d], jax.nn.sigmoid(u_v[:, d:])
        dg_v = dglu[...]
        da = dg_v * sig
        dgate = dg_v * a * sig * (1.0 - sig)
        du_ref[:, pl.ds(0, d)] = da.astype(BF16)
        du_ref[:, pl.ds(d, d)] = dgate.astype(BF16)
        dbu_ref[:, pl.ds(0, d)] += jnp.sum(da, axis=0, keepdims=True)
        dbu_ref[:, pl.ds(d, d)] += jnp.sum(dgate, axis=0, keepdims=True)

    _, dv_cur, dv_next = _halo_specs(ts, d, s)
    u_prev, u_cur, _ = _halo_specs(ts, 2 * d, s)
    out_shape = (jax.ShapeDtypeStruct((bsz, s, 2 * d), BF16), jax.ShapeDtypeStruct((1, 2 * d), F32),
                 jax.ShapeDtypeStruct((CONV_HALO, SUBLANES, d), F32))
    out_specs = (pl.BlockSpec((None, ts, 2 * d), lambda b, i: (b, i, 0)), pl.BlockSpec((1, 2 * d), lambda b, i: (0, 0)),
                 pl.BlockSpec((CONV_HALO, SUBLANES, d), lambda b, i: (0, 0, 0)))
    return _call(body, name=name, out_shape=out_shape, operands=[dv, dv, u, u, w_dw], grid=(bsz, n_tiles),
                 in_specs=[dv_cur, dv_next, u_prev, u_cur, pl.BlockSpec((CONV_HALO, d), lambda b, i: (0, 0))],
                 out_specs=out_specs,
                 scratch_shapes=[pltpu.VMEM((SUBLANES, ts + CONV_HALO, d), F32),
                                 pltpu.VMEM((SUBLANES, ts + CONV_HALO, d), F32), pltpu.VMEM((ts, d), F32)],
                 sem=("arbitrary", "arbitrary"), comm=comm)


def _mod_fwd(c_all, w_mod, b_mod, *, name):
    n_layers, d, n = w_mod.shape
    rows = c_all.shape[0]

    def body(c_ref, w_ref, b_ref, o_ref):
        cv = c_ref[...]
        cs = (cv * jax.nn.sigmoid(cv)).astype(BF16)
        o_ref[...] = jnp.dot(cs, w_ref[...].astype(BF16), preferred_element_type=F32) + b_ref[...]

    return pl.pallas_call(body, out_shape=jax.ShapeDtypeStruct((n_layers, rows, n), F32), grid=(n_layers,),
                          in_specs=[pl.BlockSpec((rows, d), lambda l: (0, 0)), pl.BlockSpec((None, d, n), lambda l: (l, 0, 0)),
                                    pl.BlockSpec((None, 1, n), lambda l: (l, 0, 0))],
                          out_specs=pl.BlockSpec((None, rows, n), lambda l: (l, 0, 0)), name=name,
                          compiler_params=_params("parallel"))(c_all, w_mod, b_mod)


def _mod_bwd(c_all, dmod, *, name):
    n_layers, rows, n = dmod.shape
    d = c_all.shape[1]

    def body(c_ref, g_ref, o_ref):
        cv = c_ref[...]
        cs = (cv * jax.nn.sigmoid(cv)).astype(BF16)
        o_ref[...] = lax.dot_general(cs, g_ref[...].astype(BF16), (((0,), (0,)), ((), ())), preferred_element_type=F32)

    return pl.pallas_call(body, out_shape=jax.ShapeDtypeStruct((n_layers, d, n), F32), grid=(n_layers,),
                          in_specs=[pl.BlockSpec((rows, d), lambda l: (0, 0)), pl.BlockSpec((None, rows, n), lambda l: (l, 0, 0))],
                          out_specs=pl.BlockSpec((None, d, n), lambda l: (l, 0, 0)), name=name,
                          compiler_params=_params("parallel"))(c_all, dmod)


def _add_half(g, recv, my_c, *, name):
    p, _, r, cdim = g.shape
    tr = _row_tile(r, cdim * 4)

    def body(c_ref, g_ref, r_ref, o_ref):
        o_ref[...] = (g_ref[...] + r_ref[...]).astype(BF16)

    grid_spec = pltpu.PrefetchScalarGridSpec(
        num_scalar_prefetch=1, grid=(p, r // tr),
        in_specs=[pl.BlockSpec((None, None, tr, cdim), lambda q, i, c_ref: (q, c_ref[0], i, 0)),
                  pl.BlockSpec((None, None, tr, cdim), lambda q, i, c_ref: (q, 0, i, 0))],
        out_specs=pl.BlockSpec((None, tr, cdim), lambda q, i, c_ref: (q, i, 0)))
    return pl.pallas_call(body, out_shape=jax.ShapeDtypeStruct((p, r, cdim), BF16), grid_spec=grid_spec, name=name,
                          compiler_params=_params("parallel", "parallel"))(my_c, g, recv)


def _add_pieces(chip, recv, my_qc, stacked, layer, n_layers, *, name):
    _, r, cdim = chip.shape
    tr = _row_tile(r, cdim * 4)

    def body(qc_ref, own_ref, r0_ref, r1_ref, r2_ref, *rest):
        f32 = lambda ref: ref[...].astype(F32)
        rest[-1][...] = ((f32(own_ref) + f32(r0_ref)) + f32(r1_ref)) + f32(r2_ref)

    piece = lambda k: pl.BlockSpec((None, tr, cdim), lambda i, qc_ref: (k, i, 0))
    in_specs = [pl.BlockSpec((None, tr, cdim), lambda i, qc_ref: (qc_ref[0], i, 0)), piece(0), piece(1), piece(2)]
    operands = [my_qc, chip, recv, recv, recv]
    aliases = {}
    if stacked is not None:
        in_specs.append(pl.BlockSpec(memory_space=pl.ANY))
        operands.append(stacked)
        aliases = {len(operands) - 1: 0}
    grid_spec = pltpu.PrefetchScalarGridSpec(
        num_scalar_prefetch=1, grid=(r // tr,), in_specs=in_specs,
        out_specs=pl.BlockSpec((None, None, tr, cdim), lambda i, qc_ref: (layer, qc_ref[1], i, 0)))
    return pl.pallas_call(body, out_shape=jax.ShapeDtypeStruct((n_layers, 2, r, cdim), F32), grid_spec=grid_spec,
                          input_output_aliases=aliases, name=name, compiler_params=_params("parallel"))(*operands)


def _sum_devices(parts, dmod, *, name):
    def body(p_ref, m_ref, o_ref, b_ref):
        acc = p_ref[0]
        for k in range(1, N_DEV):
            acc = acc + p_ref[k]
        o_ref[...] = acc
        tot = None
        for k in range(N_DEV):
            for e in range(dmod.shape[2]):
                tot = m_ref[k, :, e] if tot is None else tot + m_ref[k, :, e]
        b_ref[...] = tot

    out_shape = (jax.ShapeDtypeStruct(parts.shape[1:], F32),
                 jax.ShapeDtypeStruct((dmod.shape[1],) + dmod.shape[3:], F32))
    return pl.pallas_call(body, out_shape=out_shape, in_specs=[VMEM_SPEC, VMEM_SPEC], out_specs=(VMEM_SPEC, VMEM_SPEC),
                          name=name, compiler_params=_params())(parts, dmod)


def _adamw(w, g, m, v, *, name, comm=None):
    r, cdim = w.shape
    tr = _row_tile(r, cdim * 4, 1 << 20)

    def body(w_ref, g_ref, m_ref, v_ref, d_ref, nm_ref, nv_ref):
        gv = g_ref[...]
        nm = ADAM_B1 * m_ref[...] + (1.0 - ADAM_B1) * gv
        nv = ADAM_B2 * v_ref[...] + (1.0 - ADAM_B2) * (gv * gv)
        m_hat = nm / (1.0 - ADAM_B1 ** ADAM_STEP)
        v_hat = nv / (1.0 - ADAM_B2 ** ADAM_STEP)
        d_ref[...] = -ADAM_LR * (m_hat / (jnp.sqrt(v_hat) + ADAM_EPS) + ADAM_WD * w_ref[...])
        nm_ref[...] = nm
        nv_ref[...] = nv

    row = pl.BlockSpec((tr, cdim), lambda i: (i, 0))
    shape = jax.ShapeDtypeStruct((r, cdim), F32)
    return _call(body, name=name, out_shape=(shape, shape, shape), operands=[w, g, m, v], grid=(r // tr,),
                 in_specs=[row] * 4, out_specs=(row, row, row), sem=("parallel",), comm=comm)


def _all_gather_small(block, *, name, comm=None):
    m_per, n = block.shape

    def body(x_ref, out_ref, send_sems, recv_sems, local_sem):
        x, y, c, _ = _position()
        me, sibling = (x, y, c), (x, y, 1 - c)
        chips = [_peer(k, x, y, c)[:2] for k in CHIP_KINDS]

        def rows(px, py, pc):
            return out_ref.at[pl.ds((4 * px + 2 * py + pc) * m_per, m_per), :]

        def copy(k, blk, to, src=None):
            return pltpu.make_async_remote_copy(src_ref=rows(*blk) if src is None else src, dst_ref=rows(*blk),
                                                send_sem=send_sems.at[k], recv_sem=recv_sems.at[k],
                                                device_id=to, device_id_type=MESH)

        mine = pltpu.make_async_copy(x_ref, rows(*me), local_sem)
        mine.start()
        first = [copy(0, me, sibling, src=x_ref)]
        first += [copy(1 + j, me, (*chip, c), src=x_ref) for j, chip in enumerate(chips)]
        for cp in first:
            cp.start()
        passed = [copy(4 + j, (*chip, c), sibling) for j, chip in enumerate(chips)]
        for j, chip in enumerate(chips):
            copy(1 + j, (*chip, c), me).wait_recv()
            passed[j].start()
        copy(0, sibling, me).wait_recv()
        for j, chip in enumerate(chips):
            copy(4 + j, (*chip, 1 - c), me).wait_recv()
        for cp in first + passed:
            cp.wait_send()
        mine.wait()

    return _call(body, name=name, out_shape=jax.ShapeDtypeStruct((N_DEV * m_per, n), block.dtype), operands=[block],
                 in_specs=[VMEM_SPEC], out_specs=VMEM_SPEC,
                 scratch_shapes=[pltpu.SemaphoreType.DMA((7,)), pltpu.SemaphoreType.DMA((7,)), pltpu.SemaphoreType.DMA],
                 comm=comm)


def _pack(arrays, width):
    flat = jnp.concatenate([a.reshape(-1).astype(F32) for a in arrays])
    rows = -(-flat.shape[0] // width)
    rows = -(-rows // SUBLANES) * SUBLANES
    return jnp.pad(flat, (0, rows * width - flat.shape[0])).reshape(rows, width)


def _unpack(packed, shapes):
    flat, out, off = packed.reshape(-1), [], 0
    for shp in shapes:
        size = 1
        for dim in shp:
            size *= dim
        out.append(flat[off:off + size].reshape(shp))
        off += size
    return out


def _adamw_packed(ws, gs, ms, vs, width, *, name):
    shapes = [w.shape for w in ws]
    res = _adamw(_pack(ws, width), _pack(gs, width), _pack(ms, width), _pack(vs, width), name=name)
    return [_unpack(r, shapes) for r in res]


MB = float(1 << 20)


def _nbytes(shape, dtype):
    size = jnp.dtype(dtype).itemsize
    for dim in shape:
        size *= dim
    return size


def _gather_plans(shard, on_ready):
    r, cdim = shard.shape
    half = r // 2
    buf = jax.ShapeDtypeStruct((N_CHIPS, r, cdim), shard.dtype)
    rows = lambda ref, q, h: ref.at[pl.ds(q, 1), pl.ds(h * half, half)]
    there = lambda kind, pos: _peer_shard(kind, pos[0], pos[1])
    cost = 3 * _nbytes((half, cdim), shard.dtype) / MB

    def forward(outs):
        copies = [(lambda ins, outs, pos, kind=kind: rows(ins[0], there(kind, pos), pos[2]),
                   lambda ins, outs, pos, kind=kind: rows(outs[0], there(kind, pos), pos[2]),
                   lambda ins, outs, pos, kind=kind: rows(outs[0], there(kind, pos), 1 - pos[2]), "c") for kind in CHIP_KINDS]
        return _Plan("d2d", cost, [outs[0]], [jax.ShapeDtypeStruct(buf.shape, buf.dtype)], copies, {0: 0},
                     lambda done: on_ready(done[0]))

    copies = [(lambda ins, outs, pos: ins[0].at[:, pl.ds(pos[2] * half, half)],
               lambda ins, outs, pos: rows(outs[0], pos[3], pos[2]),
               lambda ins, outs, pos, kind=kind: rows(outs[0], there(kind, pos), pos[2]), kind) for kind in CHIP_KINDS]
    own = lambda ins, outs, pos: outs[0].at[pl.ds(pos[3], 1)]
    copies.append((lambda ins, outs, pos: ins[0], own, own, "c"))
    return _Plan("ici", cost, [shard[None]], [buf], copies), forward


class _Exchanges:
    def __init__(self):
        self.queue = []

    def add(self, plan, front=False):
        if front:
            self.queue.insert(0, plan)
        else:
            self.queue.append(plan)

    def take(self, budget_mb, at_least_one=False):
        chosen, spent = [p for p in self.queue if p.link == "d2d"], 0.0
        for p in self.queue:
            if p.link == "ici" and (spent + p.cost <= budget_mb or (at_least_one and spent == 0.0)):
                chosen.append(p)
                spent += p.cost
        if not chosen:
            return None
        self.queue = [p for p in self.queue if all(p is not ch for ch in chosen)]
        return _merge(chosen)

    def flush(self, budget_mb, until=lambda: False):
        while self.queue and not until():
            _exchange(self.take(budget_mb, at_least_one=True), name="exchange")


def kernel(x, c, w_mod, b_mod, norm_mix, norm_mlp, w_qkv, b_qkv, w_o, b_o, sinks, w_pw1, b_pw1, w_dw, b_dw, conv_ln_g, conv_ln_b, w_pw2, b_pw2, w_up, w_down, final_norm, loss_target, m_w_mod, m_b_mod, m_norm_mix, m_norm_mlp, m_w_qkv, m_b_qkv, m_w_o, m_b_o, m_sinks, m_w_pw1, m_b_pw1, m_w_dw, m_b_dw, m_conv_ln_g, m_conv_ln_b, m_w_pw2, m_b_pw2, m_w_up, m_w_down, m_final_norm, v_w_mod, v_b_mod, v_norm_mix, v_norm_mlp, v_w_qkv, v_b_qkv, v_w_o, v_b_o, v_sinks, v_w_pw1, v_b_pw1, v_w_dw, v_b_dw, v_conv_ln_g, v_conv_ln_b, v_w_pw2, v_b_pw2, v_w_up, v_w_down, v_final_norm):
    bsz, s, d = x.shape
    t = bsz * s
    depth = w_mod.shape[0]
    n_attn, n_conv = w_qkv.shape[0], w_pw1.shape[0]
    qkv_dim = d + 2 * N_KV_HEADS * HEAD_DIM
    mx, my, mc, mq = _position()
    me = 4 * mx + 2 * my + mc
    my_c = jnp.reshape(mc, (1,)).astype(jnp.int32)
    my_qc = jnp.stack([mq, mc]).astype(jnp.int32)
    pending = _Exchanges()
    SMALL, MEDIUM, LARGE = 2.5, 3.5, 6.5

    weights = {}
    order = []
    for i in range(depth):
        j = i // 2
        order += ([(("qkv", j), w_qkv[j], True), (("o", j), w_o[j], False)] if i % 2 == 0 else
                  [(("pw1", j), w_pw1[j], True), (("pw2", j), w_pw2[j], False)])
        order += [(("up", i), w_up[i], True), (("down", i), w_down[i], False)]
    for key, shard, by_cols in order:
        def ready(buf, key=key, by_cols=by_cols):
            weights[key] = jnp.transpose(buf, (1, 0, 2)).reshape(buf.shape[1], -1) if by_cols else buf.reshape(-1, buf.shape[2])
        ici, forward = _gather_plans(shard.astype(BF16), ready)
        ici.then = lambda outs, forward=forward: pending.add(forward(outs), front=True)
        pending.add(ici)

    def weight(key):
        pending.flush(SMALL, until=lambda: key in weights)
        return weights[key]

    small_sharded = [b_pw1, w_dw, b_dw, conv_ln_g, conv_ln_b, b_pw2]
    c_pad = jnp.pad(c, ((0, SUBLANES - bsz), (0, 0)))
    gathered = _all_gather_small(jnp.concatenate([c_pad, _pack(small_sharded, d)], axis=0), name="gather_c",
                                 comm=pending.take(SMALL, True))
    gathered = gathered.reshape(N_DEV, -1, d)
    c_all = gathered[:, :bsz].reshape(N_DEV * bsz, d)
    per_chip = [_unpack(gathered[2 * q, SUBLANES:], [a.shape for a in small_sharded]) for q in range(N_CHIPS)]
    b_pw1_f, w_dw_f, b_dw_f, ln_g_f, ln_b_f, b_pw2_f = [jnp.concatenate([per_chip[q][k] for q in range(N_CHIPS)], axis=-1)
                                                           for k in range(len(small_sharded))]
    w_dw_f = jnp.pad(w_dw_f, ((0, 0), (0, CONV_HALO - CONV_WIDTH), (0, 0)))

    n_mod = w_mod.shape[2]
    b_mod_cols = lax.dynamic_slice_in_dim(b_mod, mq * n_mod, n_mod, axis=1).reshape(depth, 1, n_mod)
    mod_part = _mod_fwd(c_all, w_mod, b_mod_cols, name="mod_fwd")
    mod_all = _all_gather_small(mod_part.reshape(depth * N_DEV * bsz, n_mod), name="gather_mod",
                                comm=pending.take(0.0))
    mod_all = mod_all.reshape(N_DEV, depth, N_DEV * bsz, n_mod)[0::2]
    mod = lax.dynamic_slice_in_dim(mod_all, me * bsz, bsz, axis=2)
    mod = jnp.transpose(mod, (1, 2, 0, 3)).reshape(depth, bsz, N_MOD, 1, d)
    mods = [[mod[i][:, k] for k in range(N_MOD)] for i in range(depth)]

    saved = []
    xc = x
    h1 = _normmod(xc, norm_mix[0][None], mods[0][1], mods[0][0], name="normmod")
    for i in range(depth):
        j = i // 2
        sh1, sc1, g1, sh2, sc2, g2 = mods[i]
        mlp_norm = (norm_mlp[i][None], sc2, sh2)
        if i % 2 == 0:
            wq, wo = weight(("qkv", j)), weight(("o", j))
            qkv = _mm(h1.reshape(t, d), wq, bias=b_qkv[j], tn=qkv_dim, name="mm_qkv",
                      comm=pending.take(SMALL, True)).reshape(bsz, s, qkv_dim)
            mix = _attn_fwd(qkv, sinks[j], name="attn_fwd", comm=pending.take(LARGE, True))
            y1, x1, h2 = _mm(mix.reshape(t, d), wo, bias=b_o[j], epi="resid", resid=xc.reshape(t, d), gate=g1, seq=s,
                             norm=mlp_norm, tn=d, name="mm_out", comm=pending.take(SMALL, True))
            extra = (qkv, mix)
        else:
            wp1, wp2 = weight(("pw1", j)), weight(("pw2", j))
            u = _mm(h1.reshape(t, d), wp1, bias=b_pw1_f[j], out_dtype=F32, tn=d, name="mm_pw1",
                    comm=pending.take(MEDIUM, True)).reshape(bsz, s, 2 * d)
            mix, conv_v = _conv_fwd(u, w_dw_f[j], b_dw_f[j][None], ln_g_f[j][None], ln_b_f[j][None], name="conv_fwd",
                                    comm=pending.take(LARGE, True))
            y1, x1, h2 = _mm(mix.reshape(t, d), wp2, bias=b_pw2_f[j], epi="resid", resid=xc.reshape(t, d), gate=g1, seq=s,
                             norm=mlp_norm, tn=d, name="mm_out", comm=pending.take(SMALL, True))
            extra = (u, mix, conv_v)
        act, slope = _mm(h2, weight(("up", i)), epi="relu2", tm=512, tn=2 * d, name="mm_up", comm=pending.take(LARGE, True))
        mix_norm = (norm_mix[i + 1][None], mods[i + 1][1], mods[i + 1][0]) if i + 1 < depth else None
        y2, x2, *h_next = _mm(act, weight(("down", i)), epi="resid", resid=x1, gate=g2, seq=s, norm=mix_norm,
                              tm=512, tn=d, name="mm_down", comm=pending.take(LARGE, True))
        saved.append((xc, h1, extra, y1, x1.reshape(bsz, s, d), h2, act, y2, slope))
        xc = x2.reshape(bsz, s, d)
        h1 = h_next[0] if h_next else None
    pending.flush(LARGE)

    last_y2, last_g2 = saved[-1][7].reshape(bsz, s, d), mods[-1][5]
    dx, loss_cols, d_final, dyb, dg2, _ = _loss_head(xc, loss_target, final_norm[None], (last_y2, last_g2), name="loss_head")
    loss = lax.psum(0.5 / d * jnp.sum(loss_cols), ("x", "y", "c"))

    totals = {}

    def reduce_scatter(name, layer, n_layers, grad):
        view = grad.reshape(N_CHIPS, 2, grad.shape[1] // 2, grad.shape[2])
        half_shape = (N_CHIPS, 1) + view.shape[2:]

        def scatter(outs):
            chip = _add_half(view, outs[0], my_c, name="rs_chipsum")
            copies = [(lambda ins, outs, pos, kind=kind: ins[0].at[pl.ds(_peer_shard(kind, pos[0], pos[1]), 1)],
                       lambda ins, outs, pos, k=k: outs[0].at[pl.ds(k, 1)],
                       lambda ins, outs, pos, k=k: outs[0].at[pl.ds(k, 1)], kind) for k, kind in enumerate(CHIP_KINDS)]

            def total(outs):
                totals[name] = _add_pieces(chip, outs[0], my_qc, totals.get(name), layer, n_layers, name="rs_total")

            pending.add(_Plan("ici", 3 * _nbytes(chip.shape[1:], BF16) / MB, [chip],
                              [jax.ShapeDtypeStruct((3,) + chip.shape[1:], BF16)], copies, then=total))

        pending.add(_Plan("d2d", _nbytes(half_shape, F32) / MB, [view], [jax.ShapeDtypeStruct(half_shape, F32)],
                          [(lambda ins, outs, pos: ins[0].at[:, pl.ds(1 - pos[2], 1)],
                            lambda ins, outs, pos: outs[0], lambda ins, outs, pos: outs[0], "c")], then=scatter))

    dmods, small = [None] * depth, {}
    for i in reversed(range(depth)):
        j = i // 2
        xin, h1, extra, y1, x1, h2, act, _, slope = saved[i]
        sh1, sc1, g1, sh2, sc2, g2 = mods[i]
        dyb = dyb.reshape(t, d)
        reduce_scatter("down", i, depth,
                       _mm_tn(act, dyb, name="dw_down", comm=pending.take(LARGE)).reshape(N_CHIPS, -1, d))
        dup = _mm(dyb, weights["down", i], nt=True, epi="dact", act=slope, tm=512, tn=2 * d, name="mm_dact",
                  comm=pending.take(LARGE))
        reduce_scatter("up", i, depth,
                       _mm_tn(h2.reshape(t, d), dup, col_shards=N_CHIPS, name="dw_up", comm=pending.take(LARGE)))
        dh2 = _mm(dup, weights["up", i], nt=True, out_dtype=F32, tm=512, tn=d, name="mm_dh2",
                  comm=pending.take(LARGE)).reshape(bsz, s, d)
        dx1, p2, dsh2, dyb, dg1, sdx = _normmod_bwd(x1, dh2, dx, norm_mlp[i][None], sc2, (y1.reshape(bsz, s, d), g1),
                                                    name="normmod_bwd")
        dyb = dyb.reshape(t, d)
        d_bias_out = jnp.sum(g1 * sdx, axis=(0, 1))
        if i % 2 == 0:
            qkv, mix = extra
            small["b_o", j] = d_bias_out
            reduce_scatter("o", j, n_attn,
                           _mm_tn(mix.reshape(t, d), dyb, name="dw_sq", comm=pending.take(SMALL)).reshape(N_CHIPS, -1, d))
            dmix = _mm(dyb, weights["o", j], nt=True, tn=d, name="mm_dmix", comm=pending.take(SMALL)).reshape(bsz, s, d)
            dqkv, d_bqkv, d_sink = _attn_bwd(qkv, dmix, sinks[j], name="attn_bwd", comm=pending.take(2 * LARGE))
            small["b_qkv", j], small["sinks", j] = d_bqkv[0], d_sink[:, 0]
            dqkv = dqkv.reshape(t, qkv_dim)
            dwq = _mm_tn(h1.reshape(t, d), dqkv, tn=qkv_dim, name="dw_qkv", comm=pending.take(SMALL))
            reduce_scatter("qkv", j, n_attn, jnp.transpose(dwq.reshape(d, N_CHIPS, -1), (1, 0, 2)))
            dh1 = _mm(dqkv, weights["qkv", j], nt=True, out_dtype=F32, tn=d, name="mm_dh1a", comm=pending.take(SMALL))
        else:
            u, mix, conv_v = extra
            small["b_pw2", j] = d_bias_out
            reduce_scatter("pw2", j, n_conv,
                           _mm_tn(mix.reshape(t, d), dyb, name="dw_sq", comm=pending.take(SMALL)).reshape(N_CHIPS, -1, d))
            dz = _mm(dyb, weights["pw2", j], nt=True, out_dtype=F32, tn=d, name="mm_dz", comm=pending.take(SMALL)).reshape(bsz, s, d)
            dv, d_lng, d_lnb, d_bdw = _conv_bwd_ln(dz, conv_v, ln_g_f[j][None], ln_b_f[j][None], name="conv_bwd_ln")
            du, d_bpw1, d_wdw = _conv_bwd_taps(dv, u, w_dw_f[j], name="conv_bwd_taps", comm=pending.take(2 * LARGE))
            small["ln_g", j], small["ln_b", j], small["b_dw", j] = d_lng[0], d_lnb[0], d_bdw[0]
            small["b_pw1", j], small["w_dw", j] = d_bpw1[0], jnp.sum(d_wdw[:CONV_WIDTH], axis=1)
            du = du.reshape(t, 2 * d)
            reduce_scatter("pw1", j, n_conv,
                           _mm_tn(h1.reshape(t, d), du, col_shards=N_CHIPS, name="dw_pw1", comm=pending.take(MEDIUM)))
            dh1 = _mm(du, weights["pw1", j], nt=True, out_dtype=F32, tn=d, name="mm_dh1c", comm=pending.take(MEDIUM))
        below = (saved[i - 1][7].reshape(bsz, s, d), mods[i - 1][5]) if i > 0 else None
        dx, p1, dsh1, *gate_grads = _normmod_bwd(xin, dh1.reshape(bsz, s, d), dx1, norm_mix[i][None], sc1, below,
                                                 name="normmod_bwd")
        small["norm_mix", i] = jnp.sum((1.0 + sc1) * p1, axis=(0, 1))
        small["norm_mlp", i] = jnp.sum((1.0 + sc2) * p2, axis=(0, 1))
        dmods[i] = jnp.concatenate([dsh1, norm_mix[i] * p1, dg1, dsh2, norm_mlp[i] * p2, dg2], axis=1)
        if i > 0:
            dyb, dg2, _ = gate_grads
    grad_x = dx

    small_names = ([("norm_mix", i) for i in range(depth)] + [("norm_mlp", i) for i in range(depth)]
                   + [(nm, j) for nm in ("b_qkv", "b_o", "sinks") for j in range(n_attn)]
                   + [(nm, j) for nm in ("b_pw1", "w_dw", "b_dw", "ln_g", "ln_b", "b_pw2") for j in range(n_conv)])
    small_list = [small[k] for k in small_names] + [d_final[0]]
    small_pack = _pack(small_list, d)
    dmod_rows = jnp.stack(dmods).reshape(depth * bsz * N_MOD, d)
    n_small = small_pack.shape[0]
    gathered = _all_gather_small(jnp.concatenate([small_pack, _pack([dmod_rows], d)], axis=0), name="gather_small",
                                 comm=pending.take(LARGE, True))
    gathered = gathered.reshape(N_DEV, -1, d)
    dmod_all = gathered[:, n_small:n_small + depth * bsz * N_MOD].reshape(N_DEV, depth, bsz, N_MOD, d)
    small_sum, g_b_mod = _sum_devices(gathered[:, :n_small], dmod_all, name="sum_devices")
    small_tot = dict(zip(small_names + ["final_norm"], _unpack(small_sum, [a.shape for a in small_list])))
    stacked = lambda nm, count: jnp.stack([small_tot[nm, k] for k in range(count)])
    g_norm_mix, g_norm_mlp = stacked("norm_mix", depth), stacked("norm_mlp", depth)
    g_b_qkv, g_b_o, g_sinks = stacked("b_qkv", n_attn), stacked("b_o", n_attn), stacked("sinks", n_attn)
    g_final = small_tot["final_norm"]
    g_b_mod = g_b_mod.reshape(depth, N_MOD * d)
    shard_cols = lambda g: lax.dynamic_slice_in_dim(g, mq * (g.shape[-1] // N_CHIPS), g.shape[-1] // N_CHIPS, axis=g.ndim - 1)
    g_b_pw1, g_w_dw, g_b_dw, g_ln_g, g_ln_b, g_b_pw2 = [shard_cols(stacked(nm, n_conv))
                                                        for nm in ("b_pw1", "w_dw", "b_dw", "ln_g", "ln_b", "b_pw2")]

    dmod_cols = jnp.transpose(dmod_all, (1, 0, 2, 3, 4)).reshape(depth, N_DEV * bsz, N_MOD * d)
    dmod_cols = lax.dynamic_slice_in_dim(dmod_cols, mq * n_mod, n_mod, axis=2)
    g_w_mod = _mod_bwd(c_all, dmod_cols, name="mod_bwd")

    def adam(w, g, m, v, name, comm=None):
        two_d = lambda a: a.reshape(-1, a.shape[-1])
        return [r.reshape(w.shape) for r in _adamw(two_d(w), two_d(g), two_d(m), two_d(v), name=name, comm=comm)]

    results = {"w_mod": (g_w_mod,) + tuple(adam(w_mod, g_w_mod, m_w_mod, v_w_mod, "adamw", pending.take(LARGE, True)))}

    pending.flush(LARGE)
    names = ["qkv", "o", "pw1", "pw2", "up", "down"]
    bufs = [totals[nm] for nm in names]
    copies = []
    for a, buf in enumerate(bufs):
        for layer in range(buf.shape[0]):
            half = lambda ref, h, layer=layer: ref.at[pl.ds(layer, 1), pl.ds(h, 1)]
            copies.append((lambda ins, outs, pos, a=a, half=half: half(ins[a], pos[2]),
                           lambda ins, outs, pos, a=a, half=half: half(outs[a], pos[2]),
                           lambda ins, outs, pos, a=a, half=half: half(outs[a], 1 - pos[2]), "c"))
    shared = {}
    _exchange(_Plan("d2d", 0.0, bufs, [jax.ShapeDtypeStruct(b.shape, F32) for b in bufs], copies,
                    {a: a for a in range(len(bufs))}, lambda outs: shared.update(zip(names, outs))), name="rs_share")
    g_w_qkv, g_w_o, g_w_pw1, g_w_pw2, g_w_up, g_w_down = [
        shared[nm].reshape(shared[nm].shape[0], -1, shared[nm].shape[3]) for nm in names]

    for nm, w, g, m, v in (("w_qkv", w_qkv, g_w_qkv, m_w_qkv, v_w_qkv),
                           ("w_o", w_o, g_w_o, m_w_o, v_w_o), ("w_pw1", w_pw1, g_w_pw1, m_w_pw1, v_w_pw1),
                           ("w_pw2", w_pw2, g_w_pw2, m_w_pw2, v_w_pw2), ("w_up", w_up, g_w_up, m_w_up, v_w_up),
                           ("w_down", w_down, g_w_down, m_w_down, v_w_down)):
        results[nm] = (g,) + tuple(adam(w, g, m, v, "adamw"))
    small_w = dict(b_mod=(b_mod, g_b_mod, m_b_mod, v_b_mod), norm_mix=(norm_mix, g_norm_mix, m_norm_mix, v_norm_mix),
                   norm_mlp=(norm_mlp, g_norm_mlp, m_norm_mlp, v_norm_mlp), b_qkv=(b_qkv, g_b_qkv, m_b_qkv, v_b_qkv),
                   b_o=(b_o, g_b_o, m_b_o, v_b_o), sinks=(sinks, g_sinks, m_sinks, v_sinks),
                   b_pw1=(b_pw1, g_b_pw1, m_b_pw1, v_b_pw1), w_dw=(w_dw, g_w_dw, m_w_dw, v_w_dw),
                   b_dw=(b_dw, g_b_dw, m_b_dw, v_b_dw), conv_ln_g=(conv_ln_g, g_ln_g, m_conv_ln_g, v_conv_ln_g),
                   conv_ln_b=(conv_ln_b, g_ln_b, m_conv_ln_b, v_conv_ln_b), b_pw2=(b_pw2, g_b_pw2, m_b_pw2, v_b_pw2),
                   final_norm=(final_norm, g_final, m_final_norm, v_final_norm))
    names = list(small_w)
    deltas, new_ms, new_vs = _adamw_packed(*[[small_w[nm][k] for nm in names] for k in range(4)], d, name="adamw_small")
    for k, nm in enumerate(names):
        results[nm] = (small_w[nm][1], deltas[k], new_ms[k], new_vs[k])

    weight_order = ["w_mod", "b_mod", "norm_mix", "norm_mlp", "w_qkv", "b_qkv", "w_o", "b_o", "sinks", "w_pw1", "b_pw1",
                    "w_dw", "b_dw", "conv_ln_g", "conv_ln_b", "w_pw2", "b_pw2", "w_up", "w_down", "final_norm"]
    return (loss, grad_x, *[results[nm][0] for nm in weight_order], *[results[nm][1] for nm in weight_order],
            *[results[nm][2] for nm in weight_order], *[results[nm][3] for nm in weight_order])
```

```python
import functools

import jax
import jax.numpy as jnp
from jax import lax
from jax.experimental import pallas as pl
from jax.experimental.pallas import tpu as pltpu

F32, BF16 = jnp.float32, jnp.bfloat16
MESH = pl.DeviceIdType.MESH
N_CHIPS = 4
N_DEV = 8
LANES = 128
SUBLANES = 8
VMEM_LIMIT = 48 * 1024 * 1024

NORM_EPS = 1e-6
HEAD_DIM = 64
N_KV_HEADS = 2
WINDOW = 128
CONV_WIDTH = 31
CONV_HALO = 32
CONV_ROWS = 32
N_MOD = 6

ADAM_LR, ADAM_B1, ADAM_B2, ADAM_EPS, ADAM_WD, ADAM_STEP = 0.001, 0.9, 0.999, 1e-08, 0.01, 10

HBM_SPEC = pl.BlockSpec(memory_space=pltpu.HBM)
VMEM_SPEC = pl.BlockSpec(memory_space=pltpu.VMEM)
SMEM_SPEC = pl.BlockSpec(memory_space=pltpu.SMEM)


def _params(*sem):
    return pltpu.CompilerParams(dimension_semantics=sem or None, vmem_limit_bytes=VMEM_LIMIT)


def _row_tile(rows, width_bytes, target=1 << 20):
    t = rows
    while t % 2 == 0 and t > SUBLANES and t * width_bytes > target:
        t //= 2
    return t


CHIP_KINDS = ("x", "y", "xy")


def _position():
    x, y, c = lax.axis_index("x"), lax.axis_index("y"), lax.axis_index("c")
    return x, y, c, 2 * x + y


def _peer(kind, x, y, c):
    return {"c": (x, y, 1 - c), "x": (1 - x, y, c), "y": (x, 1 - y, c), "xy": (1 - x, 1 - y, c)}[kind]


def _peer_shard(kind, x, y):
    px, py, _ = _peer(kind, x, y, 0)
    return 2 * px + py


class _Plan:
    def __init__(self, link, cost, operands, out_shapes, copies, aliases=None, then=None):
        self.link, self.cost = link, cost
        self.operands, self.out_shapes, self.copies = list(operands), list(out_shapes), list(copies)
        self.aliases, self.then = dict(aliases or {}), then


def _merge(plans):
    operands, out_shapes, copies, aliases, thens = [], [], [], {}, []
    for p in plans:
        i0, o0 = len(operands), len(out_shapes)
        i1, o1 = i0 + len(p.operands), o0 + len(p.out_shapes)

        def shifted(f, i0=i0, i1=i1, o0=o0, o1=o1):
            return lambda ins, outs, pos: f(ins[i0:i1], outs[o0:o1], pos)

        copies += [(shifted(src), shifted(dst), shifted(land), kind) for src, dst, land, kind in p.copies]
        aliases.update({i0 + k: o0 + v for k, v in p.aliases.items()})
        operands += p.operands
        out_shapes += p.out_shapes
        thens.append((p.then, o0, o1))

    def then(outs):
        for f, o0, o1 in thens:
            if f is not None:
                f(outs[o0:o1])

    return _Plan("mixed", sum(p.cost for p in plans), operands, out_shapes, copies, aliases, then)


def _call(body, *, name, out_shape, operands, grid=(), in_specs=(), out_specs=(), scratch_shapes=(), sem=(), comm=None):
    single = not isinstance(out_shape, (tuple, list))
    out_shape = [out_shape] if single else list(out_shape)
    out_specs = [out_specs] if single else list(out_specs)
    if comm is None:
        res = pl.pallas_call(body, out_shape=out_shape, grid=grid, in_specs=list(in_specs), out_specs=out_specs,
                             scratch_shapes=list(scratch_shapes), name=name, compiler_params=_params(*sem))(*operands)
        return res[0] if single else res
    n_in, n_out, n_scr = len(operands), len(out_shape), len(scratch_shapes)
    c_in, c_out, n_cp = len(comm.operands), len(comm.out_shapes), len(comm.copies)

    def wrapped(*refs):
        ins, refs = refs[:n_in], refs[n_in:]
        cins, refs = refs[:c_in], refs[c_in:]
        outs, refs = refs[:n_out], refs[n_out:]
        couts, refs = refs[:c_out], refs[c_out:]
        scr, (send_sems, recv_sems) = refs[:n_scr], refs[n_scr:]

        def descriptors():
            pos = _position()
            sends, lands = [], []
            for k, (src, dst, landing, kind) in enumerate(comm.copies):
                common = dict(send_sem=send_sems.at[k], recv_sem=recv_sems.at[k],
                              device_id=_peer(kind, *pos[:3]), device_id_type=MESH)
                sends.append(pltpu.make_async_remote_copy(src_ref=src(cins, couts, pos), dst_ref=dst(cins, couts, pos), **common))
                lands.append(pltpu.make_async_remote_copy(src_ref=src(cins, couts, pos), dst_ref=landing(cins, couts, pos), **common))
            return sends, lands

        def start():
            for cp in descriptors()[0]:
                cp.start()

        def finish():
            sends, lands = descriptors()
            for cp in lands:
                cp.wait_recv()
            for cp in sends:
                cp.wait_send()

        if not grid:
            start()
            body(*ins, *outs, *scr)
            finish()
        else:
            ids = [pl.program_id(ax) for ax in range(len(grid))]
            first, last = ids[0] == 0, ids[0] == grid[0] - 1
            for ax in range(1, len(grid)):
                first, last = first & (ids[ax] == 0), last & (ids[ax] == grid[ax] - 1)
            pl.when(first)(start)
            body(*ins, *outs, *scr)
            pl.when(last)(finish)

    res = pl.pallas_call(wrapped, out_shape=out_shape + comm.out_shapes, grid=grid,
                         in_specs=list(in_specs) + [HBM_SPEC] * c_in, out_specs=out_specs + [HBM_SPEC] * c_out,
                         scratch_shapes=list(scratch_shapes) + [pltpu.SemaphoreType.DMA((n_cp,)), pltpu.SemaphoreType.DMA((n_cp,))],
                         input_output_aliases={n_in + k: n_out + v for k, v in comm.aliases.items()},
                         name=name, compiler_params=_params(*["arbitrary"] * len(grid)))(*operands, *comm.operands)
    if comm.then is not None:
        comm.then(res[n_out:])
    return res[0] if single else res[:n_out]


def _exchange(plan, *, name):
    _call(lambda: None, name=name, out_shape=[], operands=[], comm=plan)


def _mm(a, b, *, name, nt=False, tm=1024, tn=512, epi="plain", out_dtype=BF16,
        bias=None, act=None, resid=None, gate=None, seq=None, norm=None, comm=None):
    m, k = a.shape
    n = b.shape[0] if nt else b.shape[1]
    tm, tn = min(tm, m, seq or m), min(tn, n)
    assert m % tm == 0 and n % tn == 0
    dims = (((1,), (1,)), ((), ())) if nt else (((1,), (0,)), ((), ()))
    tile = pl.BlockSpec((tm, tn), lambda j, i: (i, j))
    operands = [a, b]
    in_specs = [pl.BlockSpec((tm, k), lambda j, i: (i, 0)),
                pl.BlockSpec((tn, k), lambda j, i: (j, 0)) if nt else pl.BlockSpec((k, tn), lambda j, i: (0, j))]
    if bias is not None:
        operands.append(bias.reshape(1, n))
        in_specs.append(pl.BlockSpec((1, tn), lambda j, i: (0, j)))
    if epi == "dact":
        operands.append(act)
        in_specs.append(tile)
    if epi == "resid":
        assert seq % tm == 0
        per_ex = pl.BlockSpec((None, 1, tn), lambda j, i: (i * tm // seq, 0, j))
        operands += [resid, gate]
        in_specs += [tile, per_ex]
        out_shape = (jax.ShapeDtypeStruct((m, n), BF16), jax.ShapeDtypeStruct((m, n), F32))
        out_specs = (tile, tile)
        if norm is not None:
            assert tn == n
            operands += list(norm)
            in_specs += [pl.BlockSpec((1, tn), lambda j, i: (0, j)), per_ex, per_ex]
            out_shape += (jax.ShapeDtypeStruct((m, n), BF16),)
            out_specs += (tile,)
    elif epi == "relu2":
        out_shape = (jax.ShapeDtypeStruct((m, n), BF16), jax.ShapeDtypeStruct((m, n), BF16))
        out_specs = (tile, tile)
    else:
        out_shape = jax.ShapeDtypeStruct((m, n), out_dtype)
        out_specs = tile

    def body(*refs):
        it = iter(refs)
        a_ref, b_ref = next(it), next(it)
        acc = lax.dot_general(a_ref[...], b_ref[...], dims, preferred_element_type=F32)
        if bias is not None:
            acc = acc + next(it)[...]
        if epi == "plain":
            next(it)[...] = acc.astype(out_dtype)
        elif epi == "relu2":
            r = jnp.maximum(acc, 0.0)
            next(it)[...] = (r * r).astype(BF16)
            next(it)[...] = (2.0 * r).astype(BF16)
        elif epi == "dact":
            slope_ref = next(it)
            next(it)[...] = (acc * slope_ref[...].astype(F32)).astype(out_dtype)
        else:
            resid_ref, gate_ref = next(it), next(it)
            norm_refs = [next(it) for _ in (norm or ())]
            y_ref, x_ref = next(it), next(it)
            y_ref[...] = acc.astype(BF16)
            x_new = resid_ref[...] + gate_ref[...] * acc
            x_ref[...] = x_new
            if norm is not None:
                g_ref, sc_ref, sh_ref = norm_refs
                r = lax.rsqrt(jnp.mean(x_new * x_new, axis=-1, keepdims=True) + NORM_EPS)
                next(it)[...] = (x_new * r * g_ref[...] * (1.0 + sc_ref[...]) + sh_ref[...]).astype(BF16)

    return _call(body, name=name, out_shape=out_shape, operands=operands, grid=(n // tn, m // tm), in_specs=in_specs,
                 out_specs=out_specs, sem=("parallel", "parallel"), comm=comm)


def _mm_tn(a, b, *, name, tm=1024, tn=1024, tk=2048, col_shards=None, comm=None):
    t, m = a.shape
    n = b.shape[1]
    tm, tk = min(tm, m), min(tk, t)
    if col_shards is None:
        tn = min(tn, n)
        out_shape = jax.ShapeDtypeStruct((m, n), F32)
        out_spec = pl.BlockSpec((tm, tn), lambda i, j, k: (i, j))
    else:
        per = n // col_shards
        tn = min(tn, per)
        assert per % tn == 0
        out_shape = jax.ShapeDtypeStruct((col_shards, m, per), F32)
        out_spec = pl.BlockSpec((None, tm, tn), lambda i, j, k: (j // (per // tn), i, j % (per // tn)))
    assert m % tm == 0 and n % tn == 0 and t % tk == 0

    def body(a_ref, b_ref, o_ref):
        @pl.when(pl.program_id(2) == 0)
        def _():
            o_ref[...] = jnp.zeros_like(o_ref)

        o_ref[...] += lax.dot_general(a_ref[...], b_ref[...], (((0,), (0,)), ((), ())),
                                      preferred_element_type=F32)

    return _call(body, name=name, out_shape=out_shape, operands=[a, b], grid=(m // tm, n // tn, t // tk),
                 in_specs=[pl.BlockSpec((tk, tm), lambda i, j, k: (k, i)), pl.BlockSpec((tk, tn), lambda i, j, k: (k, j))],
                 out_specs=out_spec, sem=("parallel", "parallel", "arbitrary"), comm=comm)


def _normmod(x, gamma, sc, sh, *, name):
    bsz, s, d = x.shape
    ts = _row_tile(s, d * 4, 2 << 20)

    def body(x_ref, g_ref, sc_ref, sh_ref, o_ref):
        xf = x_ref[...]
        r = lax.rsqrt(jnp.mean(xf * xf, axis=-1, keepdims=True) + NORM_EPS)
        o_ref[...] = (xf * r * g_ref[...] * (1.0 + sc_ref[...]) + sh_ref[...]).astype(BF16)

    row = pl.BlockSpec((None, ts, d), lambda b, i: (b, i, 0))
    per_ex = pl.BlockSpec((None, 1, d), lambda b, i: (b, 0, 0))
    return pl.pallas_call(body, out_shape=jax.ShapeDtypeStruct(x.shape, BF16), grid=(bsz, s // ts),
                          in_specs=[row, pl.BlockSpec((1, d), lambda b, i: (0, 0)), per_ex, per_ex],
                          out_specs=row, name=name, compiler_params=_params("parallel", "parallel"))(x, gamma, sc, sh)


def _gate_grads(dxv, y_ref, gate_ref, dy_ref, dg_ref, sdx_ref):
    dy_ref[...] = (gate_ref[...] * dxv).astype(BF16)

    @pl.when(pl.program_id(1) == 0)
    def _():
        dg_ref[...] = jnp.zeros_like(dg_ref)
        sdx_ref[...] = jnp.zeros_like(sdx_ref)

    dg_ref[...] += jnp.sum(dxv * y_ref[...].astype(F32), axis=0, keepdims=True)
    sdx_ref[...] += jnp.sum(dxv, axis=0, keepdims=True)


def _normmod_bwd(x, dh, dres, gamma, sc, producer=None, *, name):
    bsz, s, d = x.shape
    ts = _row_tile(s, d * 4, 1 << 20)

    def body(x_ref, dh_ref, dres_ref, g_ref, sc_ref, *rest):
        dx_ref, p_ref, dsh_ref = rest[-6:-3] if producer else rest
        xf = x_ref[...]
        r = lax.rsqrt(jnp.mean(xf * xf, axis=-1, keepdims=True) + NORM_EPS)
        xhat = xf * r
        dh_v = dh_ref[...].astype(F32)
        dxhat = dh_v * (g_ref[...] * (1.0 + sc_ref[...]))
        dxv = dres_ref[...] + r * (dxhat - xhat * jnp.mean(dxhat * xhat, axis=-1, keepdims=True))
        dx_ref[...] = dxv

        @pl.when(pl.program_id(1) == 0)
        def _():
            p_ref[...] = jnp.zeros_like(p_ref)
            dsh_ref[...] = jnp.zeros_like(dsh_ref)

        p_ref[...] += jnp.sum(dh_v * xhat, axis=0, keepdims=True)
        dsh_ref[...] += jnp.sum(dh_v, axis=0, keepdims=True)
        if producer:
            _gate_grads(dxv, rest[0], rest[1], *rest[-3:])

    row = pl.BlockSpec((None, ts, d), lambda b, i: (b, i, 0))
    per_ex = pl.BlockSpec((None, 1, d), lambda b, i: (b, 0, 0))
    vec = jax.ShapeDtypeStruct((bsz, 1, d), F32)
    gate_in, gate_out, gate_specs = (), (), ()
    if producer:
        gate_in, gate_out, gate_specs = (row, per_ex), (jax.ShapeDtypeStruct(x.shape, BF16), vec, vec), (row, per_ex, per_ex)
    return pl.pallas_call(body, out_shape=(jax.ShapeDtypeStruct(x.shape, F32), vec, vec) + gate_out, grid=(bsz, s // ts),
                          in_specs=[row, row, row, pl.BlockSpec((1, d), lambda b, i: (0, 0)), per_ex, *gate_in],
                          out_specs=(row, per_ex, per_ex) + gate_specs, name=name,
                          compiler_params=_params("parallel", "arbitrary"))(x, dh, dres, gamma, sc, *(producer or ()))


def _loss_head(x, target, gamma, producer, *, name):
    bsz, s, d = x.shape
    ts = _row_tile(s, d * 4, 1 << 20)

    def body(x_ref, t_ref, g_ref, y_ref, gate_ref, dx_ref, loss_ref, dg_ref, *gate_refs):
        xf = x_ref[...]
        r = lax.rsqrt(jnp.mean(xf * xf, axis=-1, keepdims=True) + NORM_EPS)
        xhat = xf * r
        err = xhat * g_ref[...] - t_ref[...]
        dy = err * (1.0 / d)
        dxhat = dy * g_ref[...]
        dxv = r * (dxhat - xhat * jnp.mean(dxhat * xhat, axis=-1, keepdims=True))
        dx_ref[...] = dxv

        @pl.when((pl.program_id(0) == 0) & (pl.program_id(1) == 0))
        def _():
            loss_ref[...] = jnp.zeros_like(loss_ref)
            dg_ref[...] = jnp.zeros_like(dg_ref)

        loss_ref[...] += jnp.sum(err * err, axis=0, keepdims=True)
        dg_ref[...] += jnp.sum(dy * xhat, axis=0, keepdims=True)
        _gate_grads(dxv, y_ref, gate_ref, *gate_refs)

    row = pl.BlockSpec((None, ts, d), lambda b, i: (b, i, 0))
    one = pl.BlockSpec((1, d), lambda b, i: (0, 0))
    per_ex = pl.BlockSpec((None, 1, d), lambda b, i: (b, 0, 0))
    vec, ex_vec = jax.ShapeDtypeStruct((1, d), F32), jax.ShapeDtypeStruct((bsz, 1, d), F32)
    return pl.pallas_call(body, out_shape=(jax.ShapeDtypeStruct(x.shape, F32), vec, vec,
                                           jax.ShapeDtypeStruct(x.shape, BF16), ex_vec, ex_vec), grid=(bsz, s // ts),
                          in_specs=[row, row, one, row, per_ex], out_specs=(row, one, one, row, per_ex, per_ex), name=name,
                          compiler_params=_params("arbitrary", "arbitrary"))(x, target, gamma, *producer)


def _alibi_slope(h, n_heads):
    return 2.0 ** (-8.0 * (h + 1) / n_heads)


def _attn_masks(first_block):
    qi = lax.broadcasted_iota(jnp.int32, (WINDOW, 2 * WINDOW), 0)
    ki = lax.broadcasted_iota(jnp.int32, (WINDOW, 2 * WINDOW), 1)
    dist = qi + WINDOW - ki
    first_key = jnp.where(first_block, WINDOW, 0)
    valid = (dist >= 0) & (dist < WINDOW) & (ki >= first_key)
    return dist.astype(F32), valid


def _dup_halves(span, kv, left):
    f = span.astype(F32)
    rolled = pltpu.roll(f, HEAD_DIM, axis=1)
    out = jnp.where(left, f, rolled) if kv == 0 else jnp.where(left, rolled, f)
    return out.astype(BF16)


def _attn_probs(s, h, n_heads, distf, valid, sink):
    s = s * (HEAD_DIM ** -0.5) - _alibi_slope(h, n_heads) * distf
    s = jnp.where(valid, s, -1e30)
    m = jnp.maximum(jnp.max(s, axis=-1, keepdims=True), sink)
    e = jnp.exp(s - m)
    e_sink = jnp.exp(sink - m)
    inv = 1.0 / (jnp.sum(e, axis=-1, keepdims=True) + e_sink)
    return e * inv, e_sink * inv


def _attn_specs(d, n_blocks, clamp):
    kcol = d // LANES
    cur = (lambda i: jnp.minimum(i, n_blocks - 1)) if clamp else (lambda i: i)
    prev = lambda i: jnp.maximum(cur(i) - 1, 0)
    kv = lambda col, blk: pl.BlockSpec((None, WINDOW, LANES), lambda b, i: (b, blk(i), col))
    return [pl.BlockSpec((None, WINDOW, d), lambda b, i: (b, cur(i), 0)),
            kv(kcol, prev), kv(kcol, cur), kv(kcol + 1, prev), kv(kcol + 1, cur)]


def _attn_fwd(qkv, sinks, *, name, comm=None):
    bsz, s, qkv_dim = qkv.shape
    d = qkv_dim - 2 * N_KV_HEADS * HEAD_DIM
    n_heads = d // HEAD_DIM
    group = n_heads // N_KV_HEADS
    pairs = group // 2
    n_blocks = s // WINDOW

    def body(q_ref, kp_ref, kc_ref, vp_ref, vc_ref, sink_ref, o_ref):
        left = lax.broadcasted_iota(jnp.int32, (1, LANES), 1) < HEAD_DIM
        distf, valid = _attn_masks(pl.program_id(1) == 0)
        kspan = jnp.concatenate([kp_ref[...], kc_ref[...]], axis=0)
        vspan = jnp.concatenate([vp_ref[...], vc_ref[...]], axis=0)
        for kv in range(N_KV_HEADS):
            kdup, vdup = _dup_halves(kspan, kv, left), _dup_halves(vspan, kv, left)
            res = []
            for par in range(2):
                keep = left if par == 0 else jnp.logical_not(left)
                cols = [pl.ds((kv * pairs + p) * LANES, LANES) for p in range(pairs)]
                lhs = jnp.concatenate([jnp.where(keep, q_ref[:, cl], jnp.zeros((), BF16)) for cl in cols], axis=0)
                sc = lax.dot_general(lhs, kdup, (((1,), (1,)), ((), ())), preferred_element_type=F32)
                probs = []
                for p in range(pairs):
                    h = kv * group + 2 * p + par
                    pr, _ = _attn_probs(sc[p * WINDOW:(p + 1) * WINDOW], h, n_heads, distf, valid, sink_ref[h])
                    probs.append(pr.astype(BF16))
                res.append(jnp.dot(jnp.concatenate(probs, axis=0), vdup, preferred_element_type=F32))
            for p in range(pairs):
                rows = slice(p * WINDOW, (p + 1) * WINDOW)
                o_ref[:, pl.ds((kv * pairs + p) * LANES, LANES)] = jnp.where(left, res[0][rows], res[1][rows]).astype(BF16)

    return _call(body, name=name, out_shape=jax.ShapeDtypeStruct((bsz, s, d), BF16), operands=[qkv] * 5 + [sinks],
                 grid=(bsz, n_blocks), in_specs=_attn_specs(d, n_blocks, False) + [SMEM_SPEC],
                 out_specs=pl.BlockSpec((None, WINDOW, d), lambda b, i: (b, i, 0)), sem=("parallel", "parallel"), comm=comm)


def _attn_bwd(qkv, do, sinks, *, name, comm=None):
    bsz, s, qkv_dim = qkv.shape
    d = qkv_dim - 2 * N_KV_HEADS * HEAD_DIM
    n_heads = d // HEAD_DIM
    group = n_heads // N_KV_HEADS
    pairs = group // 2
    n_blocks = s // WINDOW
    tn_dims = (((0,), (0,)), ((), ()))

    def body(q_ref, kp_ref, kc_ref, vp_ref, vc_ref, do_ref, sink_ref, dqkv_ref, colsum_ref, dsink_ref,
             dq_prev, dk_carry, dv_carry):
        b, i = pl.program_id(0), pl.program_id(1)
        left = lax.broadcasted_iota(jnp.int32, (1, LANES), 1) < HEAD_DIM

        @pl.when((b == 0) & (i == 0))
        def _():
            colsum_ref[...] = jnp.zeros_like(colsum_ref)
            dsink_ref[...] = jnp.zeros_like(dsink_ref)

        @pl.when(i == 0)
        def _():
            dqkv_ref[...] = jnp.zeros_like(dqkv_ref)
            dk_carry[...] = jnp.zeros_like(dk_carry)
            dv_carry[...] = jnp.zeros_like(dv_carry)

        @pl.when(i > 0)
        def _():
            dq_v = dq_prev[...]
            dqkv_ref[:, pl.ds(0, d)] = dq_v.astype(BF16)
            colsum_ref[:, pl.ds(0, d)] += jnp.sum(dq_v, axis=0, keepdims=True)

        @pl.when(i < n_blocks)
        def _():
            distf, valid = _attn_masks(i == 0)
            kspan = jnp.concatenate([kp_ref[...], kc_ref[...]], axis=0)
            vspan = jnp.concatenate([vp_ref[...], vc_ref[...]], axis=0)
            dk_blk, dv_blk = [], []
            for kv in range(N_KV_HEADS):
                kdup, vdup = _dup_halves(kspan, kv, left), _dup_halves(vspan, kv, left)
                dq_res, dk_sum, dv_sum = [], None, None
                for par in range(2):
                    keep = left if par == 0 else jnp.logical_not(left)
                    cols = [pl.ds((kv * pairs + p) * LANES, LANES) for p in range(pairs)]
                    zero = jnp.zeros((), BF16)
                    lhs = jnp.concatenate([jnp.where(keep, q_ref[:, cl], zero) for cl in cols], axis=0)
                    dol = jnp.concatenate([jnp.where(keep, do_ref[:, cl], zero) for cl in cols], axis=0)
                    sc = lax.dot_general(lhs, kdup, (((1,), (1,)), ((), ())), preferred_element_type=F32)
                    dp = lax.dot_general(dol, vdup, (((1,), (1,)), ((), ())), preferred_element_type=F32)
                    probs, dscores = [], []
                    for p in range(pairs):
                        h = kv * group + 2 * p + par
                        rows = slice(p * WINDOW, (p + 1) * WINDOW)
                        pr, p_sink = _attn_probs(sc[rows], h, n_heads, distf, valid, sink_ref[h])
                        delta = jnp.sum(pr * dp[rows], axis=-1, keepdims=True)
                        dscores.append((pr * (dp[rows] - delta) * (HEAD_DIM ** -0.5)).astype(BF16))
                        probs.append(pr.astype(BF16))
                        dsink_ref[pl.ds(h, 1), :] += jnp.zeros((1, LANES), F32) - jnp.sum(p_sink * delta)
                    ds_all = jnp.concatenate(dscores, axis=0)
                    p_all = jnp.concatenate(probs, axis=0)
                    dq_res.append(jnp.dot(ds_all, kdup, preferred_element_type=F32))
                    dk_par = lax.dot_general(ds_all, lhs, tn_dims, preferred_element_type=F32)
                    dv_par = lax.dot_general(p_all, dol, tn_dims, preferred_element_type=F32)
                    dk_sum = dk_par if dk_sum is None else dk_sum + dk_par
                    dv_sum = dv_par if dv_sum is None else dv_sum + dv_par
                for p in range(pairs):
                    rows = slice(p * WINDOW, (p + 1) * WINDOW)
                    dq_prev[:, pl.ds((kv * pairs + p) * LANES, LANES)] = jnp.where(left, dq_res[0][rows], dq_res[1][rows])
                dk_blk.append(dk_sum + pltpu.roll(dk_sum, HEAD_DIM, axis=1))
                dv_blk.append(dv_sum + pltpu.roll(dv_sum, HEAD_DIM, axis=1))
            dk_span = jnp.where(left, dk_blk[0], dk_blk[1])
            dv_span = jnp.where(left, dv_blk[0], dv_blk[1])
            dk_done = dk_carry[...] + dk_span[:WINDOW]
            dv_done = dv_carry[...] + dv_span[:WINDOW]
            dk_carry[...] = dk_span[WINDOW:]
            dv_carry[...] = dv_span[WINDOW:]

            @pl.when(i > 0)
            def _():
                dqkv_ref[:, pl.ds(d, LANES)] = dk_done.astype(BF16)
                dqkv_ref[:, pl.ds(d + LANES, LANES)] = dv_done.astype(BF16)
                colsum_ref[:, pl.ds(d, LANES)] += jnp.sum(dk_done, axis=0, keepdims=True)
                colsum_ref[:, pl.ds(d + LANES, LANES)] += jnp.sum(dv_done, axis=0, keepdims=True)

        @pl.when(i == n_blocks)
        def _():
            dk_done, dv_done = dk_carry[...], dv_carry[...]
            dqkv_ref[:, pl.ds(d, LANES)] = dk_done.astype(BF16)
            dqkv_ref[:, pl.ds(d + LANES, LANES)] = dv_done.astype(BF16)
            colsum_ref[:, pl.ds(d, LANES)] += jnp.sum(dk_done, axis=0, keepdims=True)
            colsum_ref[:, pl.ds(d + LANES, LANES)] += jnp.sum(dv_done, axis=0, keepdims=True)

    do_spec = pl.BlockSpec((None, WINDOW, d), lambda b, i: (b, jnp.minimum(i, n_blocks - 1), 0))
    out_shape = (jax.ShapeDtypeStruct((bsz, s, qkv_dim), BF16), jax.ShapeDtypeStruct((1, qkv_dim), F32),
                 jax.ShapeDtypeStruct((n_heads, LANES), F32))
    out_specs = (pl.BlockSpec((None, WINDOW, qkv_dim), lambda b, i: (b, jnp.maximum(i - 1, 0), 0)),
                 pl.BlockSpec((1, qkv_dim), lambda b, i: (0, 0)),
                 pl.BlockSpec((n_heads, LANES), lambda b, i: (0, 0)))
    return _call(body, name=name, out_shape=out_shape, operands=[qkv] * 5 + [do, sinks], grid=(bsz, n_blocks + 1),
                 in_specs=_attn_specs(d, n_blocks, True) + [do_spec, SMEM_SPEC], out_specs=out_specs,
                 scratch_shapes=[pltpu.VMEM((WINDOW, d), F32), pltpu.VMEM((WINDOW, LANES), F32), pltpu.VMEM((WINDOW, LANES), F32)],
                 sem=("arbitrary", "arbitrary"), comm=comm)


def _conv_tile(s):
    return min(256, s)


def _halo_specs(ts, width, s):
    per = ts // CONV_HALO
    prev = pl.BlockSpec((None, CONV_HALO, width), lambda b, i: (b, jnp.maximum(i * per - 1, 0), 0))
    nxt = pl.BlockSpec((None, CONV_HALO, width), lambda b, i: (b, jnp.minimum((i + 1) * per, s // CONV_HALO - 1), 0))
    cur = pl.BlockSpec((None, ts, width), lambda b, i: (b, i, 0))
    return prev, cur, nxt


def _glu(u, d):
    return u[:, :d] * jax.nn.sigmoid(u[:, d:])


def _store_shifted(shifted, value):
    rows = value.shape[0]
    shifted[0] = value
    for b in range(1, SUBLANES):
        shifted[b] = pltpu.roll(value, rows - b, axis=0)


def _at(shifted, base, offset, n_rows, lanes):
    return shifted[offset % SUBLANES, pl.ds(base + (offset - offset % SUBLANES), n_rows), lanes]


def _taps(w_ref, shifted, out_ref, ts, d, offset):
    for r0 in range(0, ts, CONV_ROWS):
        for l0 in range(0, d, LANES):
            lanes = pl.ds(l0, LANES)
            acc = jnp.zeros((CONV_ROWS, LANES), F32)
            for j in range(CONV_WIDTH):
                acc = acc + w_ref[pl.ds(j, 1), lanes] * _at(shifted, r0, offset(j), CONV_ROWS, lanes)
            out_ref[pl.ds(r0, CONV_ROWS), lanes] = acc


def _conv_fwd(u, w_dw, b_dw, ln_g, ln_b, *, name, comm=None):
    bsz, s, d2 = u.shape
    d = d2 // 2
    ts = _conv_tile(s)

    def body(up_ref, uc_ref, w_ref, bdw_ref, g_ref, b_ref, z_ref, v_ref, gbuf):
        halo = jnp.where(pl.program_id(1) > 0, _glu(up_ref[...], d), 0.0)
        _store_shifted(gbuf, jnp.concatenate([halo, _glu(uc_ref[...], d)], axis=0))
        _taps(w_ref, gbuf, v_ref, ts, d, lambda j: CONV_HALO - (CONV_WIDTH - 1) + j)
        v = v_ref[...] + bdw_ref[...]
        v_ref[...] = v
        mu = jnp.mean(v, axis=-1, keepdims=True)
        cen = v - mu
        rstd = lax.rsqrt(jnp.mean(cen * cen, axis=-1, keepdims=True) + NORM_EPS)
        ln = cen * rstd * g_ref[...] + b_ref[...]
        z_ref[...] = (ln * jax.nn.sigmoid(ln)).astype(BF16)

    prev, cur, _ = _halo_specs(ts, d2, s)
    one = pl.BlockSpec((1, d), lambda b, i: (0, 0))
    row = pl.BlockSpec((None, ts, d), lambda b, i: (b, i, 0))
    return _call(body, name=name, out_shape=(jax.ShapeDtypeStruct((bsz, s, d), BF16), jax.ShapeDtypeStruct((bsz, s, d), F32)),
                 operands=[u, u, w_dw, b_dw, ln_g, ln_b], grid=(bsz, s // ts),
                 in_specs=[prev, cur, pl.BlockSpec((CONV_HALO, d), lambda b, i: (0, 0)), one, one, one],
                 out_specs=(row, row), scratch_shapes=[pltpu.VMEM((SUBLANES, ts + CONV_HALO, d), F32)],
                 sem=("parallel", "parallel"), comm=comm)


def _conv_bwd_ln(dz, v, ln_g, ln_b, *, name):
    bsz, s, d = v.shape
    ts = _row_tile(s, d * 4, 1 << 20)

    def body(dz_ref, v_ref, g_ref, b_ref, dv_ref, dg_ref, db_ref, dbdw_ref):
        v_v = v_ref[...]
        mu = jnp.mean(v_v, axis=-1, keepdims=True)
        cen = v_v - mu
        rstd = lax.rsqrt(jnp.mean(cen * cen, axis=-1, keepdims=True) + NORM_EPS)
        vhat = cen * rstd
        ln = vhat * g_ref[...] + b_ref[...]
        sig = jax.nn.sigmoid(ln)
        dln = dz_ref[...] * (sig * (1.0 + ln * (1.0 - sig)))
        dvhat = dln * g_ref[...]
        dv = rstd * (dvhat - jnp.mean(dvhat, axis=-1, keepdims=True)
                     - vhat * jnp.mean(dvhat * vhat, axis=-1, keepdims=True))
        dv_ref[...] = dv

        @pl.when((pl.program_id(0) == 0) & (pl.program_id(1) == 0))
        def _():
            dg_ref[...] = jnp.zeros_like(dg_ref)
            db_ref[...] = jnp.zeros_like(db_ref)
            dbdw_ref[...] = jnp.zeros_like(dbdw_ref)

        dg_ref[...] += jnp.sum(dln * vhat, axis=0, keepdims=True)
        db_ref[...] += jnp.sum(dln, axis=0, keepdims=True)
        dbdw_ref[...] += jnp.sum(dv, axis=0, keepdims=True)

    row = pl.BlockSpec((None, ts, d), lambda b, i: (b, i, 0))
    one = pl.BlockSpec((1, d), lambda b, i: (0, 0))
    vec = jax.ShapeDtypeStruct((1, d), F32)
    return pl.pallas_call(body, out_shape=(jax.ShapeDtypeStruct(v.shape, F32), vec, vec, vec), grid=(bsz, s // ts),
                          in_specs=[row, row, one, one], out_specs=(row, one, one, one), name=name,
                          compiler_params=_params("arbitrary", "arbitrary"))(dz, v, ln_g, ln_b)


def _conv_bwd_taps(dv, u, w_dw, *, name, comm=None):
    bsz, s, d = dv.shape
    ts = _conv_tile(s)
    n_tiles = s // ts

    def body(dvc_ref, dvn_ref, up_ref, uc_ref, w_ref, du_ref, dbu_ref, dw_ref, gbuf, dvbuf, dglu):
        i = pl.program_id(1)

        @pl.when((pl.program_id(0) == 0) & (i == 0))
        def _():
            dbu_ref[...] = jnp.zeros_like(dbu_ref)
            dw_ref[...] = jnp.zeros_like(dw_ref)

        halo = jnp.where(i > 0, _glu(up_ref[...], d), 0.0)
        _store_shifted(gbuf, jnp.concatenate([halo, _glu(uc_ref[...], d)], axis=0))
        ahead = jnp.where(i < n_tiles - 1, dvn_ref[...], 0.0)
        _store_shifted(dvbuf, jnp.concatenate([dvc_ref[...], ahead], axis=0))
        _taps(w_ref, dvbuf, dglu, ts, d, lambda j: CONV_WIDTH - 1 - j)
        for l0 in range(0, d, LANES):
            lanes = pl.ds(l0, LANES)
            for j in range(CONV_WIDTH):
                acc = jnp.zeros((SUBLANES, LANES), F32)
                for r0 in range(0, ts, CONV_ROWS):
                    prod = dvc_ref[pl.ds(r0, CONV_ROWS), lanes] * _at(gbuf, r0, CONV_HALO - (CONV_WIDTH - 1) + j, CONV_ROWS, lanes)
                    for k in range(0, CONV_ROWS, SUBLANES):
                        acc = acc + prod[k:k + SUBLANES]
                dw_ref[j, :, lanes] += acc
        u_v = uc_ref[...]
        a, sig = u_v[:, :d], jax.nn.sigmoid(u_v[:, d:])
        dg_v = dglu[...]
        da = dg_v * sig
        dgate = dg_v * a * sig * (1.0 - sig)
        du_ref[:, pl.ds(0, d)] = da.astype(BF16)
        du_ref[:, pl.ds(d, d)] = dgate.astype(BF16)
        dbu_ref[:, pl.ds(0, d)] += jnp.sum(da, axis=0, keepdims=True)
        dbu_ref[:, pl.ds(d, d)] += jnp.sum(dgate, axis=0, keepdims=True)

    _, dv_cur, dv_next = _halo_specs(ts, d, s)
    u_prev, u_cur, _ = _halo_specs(ts, 2 * d, s)
    out_shape = (jax.ShapeDtypeStruct((bsz, s, 2 * d), BF16), jax.ShapeDtypeStruct((1, 2 * d), F32),
                 jax.ShapeDtypeStruct((CONV_HALO, SUBLANES, d), F32))
    out_specs = (pl.BlockSpec((None, ts, 2 * d), lambda b, i: (b, i, 0)), pl.BlockSpec((1, 2 * d), lambda b, i: (0, 0)),
                 pl.BlockSpec((CONV_HALO, SUBLANES, d), lambda b, i: (0, 0, 0)))
    return _call(body, name=name, out_shape=out_shape, operands=[dv, dv, u, u, w_dw], grid=(bsz, n_tiles),
                 in_specs=[dv_cur, dv_next, u_prev, u_cur, pl.BlockSpec((CONV_HALO, d), lambda b, i: (0, 0))],
                 out_specs=out_specs,
                 scratch_shapes=[pltpu.VMEM((SUBLANES, ts + CONV_HALO, d), F32),
                                 pltpu.VMEM((SUBLANES, ts + CONV_HALO, d), F32), pltpu.VMEM((ts, d), F32)],
                 sem=("arbitrary", "arbitrary"), comm=comm)


def _mod_fwd(c_all, w_mod, b_mod, *, name):
    n_layers, d, n = w_mod.shape
    rows = c_all.shape[0]

    def body(c_ref, w_ref, b_ref, o_ref):
        cv = c_ref[...]
        cs = (cv * jax.nn.sigmoid(cv)).astype(BF16)
        o_ref[...] = jnp.dot(cs, w_ref[...].astype(BF16), preferred_element_type=F32) + b_ref[...]

    return pl.pallas_call(body, out_shape=jax.ShapeDtypeStruct((n_layers, rows, n), F32), grid=(n_layers,),
                          in_specs=[pl.BlockSpec((rows, d), lambda l: (0, 0)), pl.BlockSpec((None, d, n), lambda l: (l, 0, 0)),
                                    pl.BlockSpec((None, 1, n), lambda l: (l, 0, 0))],
                          out_specs=pl.BlockSpec((None, rows, n), lambda l: (l, 0, 0)), name=name,
                          compiler_params=_params("parallel"))(c_all, w_mod, b_mod)


def _mod_bwd(c_all, dmod, *, name):
    n_layers, rows, n = dmod.shape
    d = c_all.shape[1]

    def body(c_ref, g_ref, o_ref):
        cv = c_ref[...]
        cs = (cv * jax.nn.sigmoid(cv)).astype(BF16)
        o_ref[...] = lax.dot_general(cs, g_ref[...].astype(BF16), (((0,), (0,)), ((), ())), preferred_element_type=F32)

    return pl.pallas_call(body, out_shape=jax.ShapeDtypeStruct((n_layers, d, n), F32), grid=(n_layers,),
                          in_specs=[pl.BlockSpec((rows, d), lambda l: (0, 0)), pl.BlockSpec((None, rows, n), lambda l: (l, 0, 0))],
                          out_specs=pl.BlockSpec((None, d, n), lambda l: (l, 0, 0)), name=name,
                          compiler_params=_params("parallel"))(c_all, dmod)


def _add_half(g, recv, my_c, *, name):
    p, _, r, cdim = g.shape
    tr = _row_tile(r, cdim * 4)

    def body(c_ref, g_ref, r_ref, o_ref):
        o_ref[...] = (g_ref[...] + r_ref[...]).astype(BF16)

    grid_spec = pltpu.PrefetchScalarGridSpec(
        num_scalar_prefetch=1, grid=(p, r // tr),
        in_specs=[pl.BlockSpec((None, None, tr, cdim), lambda q, i, c_ref: (q, c_ref[0], i, 0)),
                  pl.BlockSpec((None, None, tr, cdim), lambda q, i, c_ref: (q, 0, i, 0))],
        out_specs=pl.BlockSpec((None, tr, cdim), lambda q, i, c_ref: (q, i, 0)))
    return pl.pallas_call(body, out_shape=jax.ShapeDtypeStruct((p, r, cdim), BF16), grid_spec=grid_spec, name=name,
                          compiler_params=_params("parallel", "parallel"))(my_c, g, recv)


def _add_pieces(chip, recv, my_qc, stacked, layer, n_layers, *, name):
    _, r, cdim = chip.shape
    tr = _row_tile(r, cdim * 4)

    def body(qc_ref, own_ref, r0_ref, r1_ref, r2_ref, *rest):
        f32 = lambda ref: ref[...].astype(F32)
        rest[-1][...] = ((f32(own_ref) + f32(r0_ref)) + f32(r1_ref)) + f32(r2_ref)

    piece = lambda k: pl.BlockSpec((None, tr, cdim), lambda i, qc_ref: (k, i, 0))
    in_specs = [pl.BlockSpec((None, tr, cdim), lambda i, qc_ref: (qc_ref[0], i, 0)), piece(0), piece(1), piece(2)]
    operands = [my_qc, chip, recv, recv, recv]
    aliases = {}
    if stacked is not None:
        in_specs.append(pl.BlockSpec(memory_space=pl.ANY))
        operands.append(stacked)
        aliases = {len(operands) - 1: 0}
    grid_spec = pltpu.PrefetchScalarGridSpec(
        num_scalar_prefetch=1, grid=(r // tr,), in_specs=in_specs,
        out_specs=pl.BlockSpec((None, None, tr, cdim), lambda i, qc_ref: (layer, qc_ref[1], i, 0)))
    return pl.pallas_call(body, out_shape=jax.ShapeDtypeStruct((n_layers, 2, r, cdim), F32), grid_spec=grid_spec,
                          input_output_aliases=aliases, name=name, compiler_params=_params("parallel"))(*operands)


def _sum_devices(parts, dmod, *, name):
    def body(p_ref, m_ref, o_ref, b_ref):
        acc = p_ref[0]
        for k in range(1, N_DEV):
            acc = acc + p_ref[k]
        o_ref[...] = acc
        tot = None
        for k in range(N_DEV):
            for e in range(dmod.shape[2]):
                tot = m_ref[k, :, e] if tot is None else tot + m_ref[k, :, e]
        b_ref[...] = tot

    out_shape = (jax.ShapeDtypeStruct(parts.shape[1:], F32),
                 jax.ShapeDtypeStruct((dmod.shape[1],) + dmod.shape[3:], F32))
    return pl.pallas_call(body, out_shape=out_shape, in_specs=[VMEM_SPEC, VMEM_SPEC], out_specs=(VMEM_SPEC, VMEM_SPEC),
                          name=name, compiler_params=_params())(parts, dmod)


def _adamw(w, g, m, v, *, name, comm=None):
    r, cdim = w.shape
    tr = _row_tile(r, cdim * 4, 1 << 20)

    def body(w_ref, g_ref, m_ref, v_ref, d_ref, nm_ref, nv_ref):
        gv = g_ref[...]
        nm = ADAM_B1 * m_ref[...] + (1.0 - ADAM_B1) * gv
        nv = ADAM_B2 * v_ref[...] + (1.0 - ADAM_B2) * (gv * gv)
        m_hat = nm / (1.0 - ADAM_B1 ** ADAM_STEP)
        v_hat = nv / (1.0 - ADAM_B2 ** ADAM_STEP)
        d_ref[...] = -ADAM_LR * (m_hat / (jnp.sqrt(v_hat) + ADAM_EPS) + ADAM_WD * w_ref[...])
        nm_ref[...] = nm
        nv_ref[...] = nv

    row = pl.BlockSpec((tr, cdim), lambda i: (i, 0))
    shape = jax.ShapeDtypeStruct((r, cdim), F32)
    return _call(body, name=name, out_shape=(shape, shape, shape), operands=[w, g, m, v], grid=(r // tr,),
                 in_specs=[row] * 4, out_specs=(row, row, row), sem=("parallel",), comm=comm)


def _all_gather_small(block, *, name, comm=None):
    m_per, n = block.shape

    def body(x_ref, out_ref, send_sems, recv_sems, local_sem):
        x, y, c, _ = _position()
        me, sibling = (x, y, c), (x, y, 1 - c)
        chips = [_peer(k, x, y, c)[:2] for k in CHIP_KINDS]

        def rows(px, py, pc):
            return out_ref.at[pl.ds((4 * px + 2 * py + pc) * m_per, m_per), :]

        def copy(k, blk, to, src=None):
            return pltpu.make_async_remote_copy(src_ref=rows(*blk) if src is None else src, dst_ref=rows(*blk),
                                                send_sem=send_sems.at[k], recv_sem=recv_sems.at[k],
                                                device_id=to, device_id_type=MESH)

        mine = pltpu.make_async_copy(x_ref, rows(*me), local_sem)
        mine.start()
        first = [copy(0, me, sibling, src=x_ref)]
        first += [copy(1 + j, me, (*chip, c), src=x_ref) for j, chip in enumerate(chips)]
        for cp in first:
            cp.start()
        passed = [copy(4 + j, (*chip, c), sibling) for j, chip in enumerate(chips)]
        for j, chip in enumerate(chips):
            copy(1 + j, (*chip, c), me).wait_recv()
            passed[j].start()
        copy(0, sibling, me).wait_recv()
        for j, chip in enumerate(chips):
            copy(4 + j, (*chip, 1 - c), me).wait_recv()
        for cp in first + passed:
            cp.wait_send()
        mine.wait()

    return _call(body, name=name, out_shape=jax.ShapeDtypeStruct((N_DEV * m_per, n), block.dtype), operands=[block],
                 in_specs=[VMEM_SPEC], out_specs=VMEM_SPEC,
                 scratch_shapes=[pltpu.SemaphoreType.DMA((7,)), pltpu.SemaphoreType.DMA((7,)), pltpu.SemaphoreType.DMA],
                 comm=comm)


def _pack(arrays, width):
    flat = jnp.concatenate([a.reshape(-1).astype(F32) for a in arrays])
    rows = -(-flat.shape[0] // width)
    rows = -(-rows // SUBLANES) * SUBLANES
    return jnp.pad(flat, (0, rows * width - flat.shape[0])).reshape(rows, width)


def _unpack(packed, shapes):
    flat, out, off = packed.reshape(-1), [], 0
    for shp in shapes:
        size = 1
        for dim in shp:
            size *= dim
        out.append(flat[off:off + size].reshape(shp))
        off += size
    return out


def _adamw_packed(ws, gs, ms, vs, width, *, name):
    shapes = [w.shape for w in ws]
    res = _adamw(_pack(ws, width), _pack(gs, width), _pack(ms, width), _pack(vs, width), name=name)
    return [_unpack(r, shapes) for r in res]


MB = float(1 << 20)


def _nbytes(shape, dtype):
    size = jnp.dtype(dtype).itemsize
    for dim in shape:
        size *= dim
    return size


def _gather_plans(shard, on_ready):
    r, cdim = shard.shape
    half = r // 2
    buf = jax.ShapeDtypeStruct((N_CHIPS, r, cdim), shard.dtype)
    rows = lambda ref, q, h: ref.at[pl.ds(q, 1), pl.ds(h * half, half)]
    there = lambda kind, pos: _peer_shard(kind, pos[0], pos[1])
    cost = 3 * _nbytes((half, cdim), shard.dtype) / MB

    def forward(outs):
        copies = [(lambda ins, outs, pos, kind=kind: rows(ins[0], there(kind, pos), pos[2]),
                   lambda ins, outs, pos, kind=kind: rows(outs[0], there(kind, pos), pos[2]),
                   lambda ins, outs, pos, kind=kind: rows(outs[0], there(kind, pos), 1 - pos[2]), "c") for kind in CHIP_KINDS]
        return _Plan("d2d", cost, [outs[0]], [jax.ShapeDtypeStruct(buf.shape, buf.dtype)], copies, {0: 0},
                     lambda done: on_ready(done[0]))

    copies = [(lambda ins, outs, pos: ins[0].at[:, pl.ds(pos[2] * half, half)],
               lambda ins, outs, pos: rows(outs[0], pos[3], pos[2]),
               lambda ins, outs, pos, kind=kind: rows(outs[0], there(kind, pos), pos[2]), kind) for kind in CHIP_KINDS]
    own = lambda ins, outs, pos: outs[0].at[pl.ds(pos[3], 1)]
    copies.append((lambda ins, outs, pos: ins[0], own, own, "c"))
    return _Plan("ici", cost, [shard[None]], [buf], copies), forward


class _Exchanges:
    def __init__(self):
        self.queue = []

    def add(self, plan, front=False):
        if front:
            self.queue.insert(0, plan)
        else:
            self.queue.append(plan)

    def take(self, budget_mb, at_least_one=False):
        chosen, spent = [p for p in self.queue if p.link == "d2d"], 0.0
        for p in self.queue:
            if p.link == "ici" and (spent + p.cost <= budget_mb or (at_least_one and spent == 0.0)):
                chosen.append(p)
                spent += p.cost
        if not chosen:
            return None
        self.queue = [p for p in self.queue if all(p is not ch for ch in chosen)]
        return _merge(chosen)

    def flush(self, budget_mb, until=lambda: False):
        while self.queue and not until():
            _exchange(self.take(budget_mb, at_least_one=True), name="exchange")


def kernel(x, c, w_mod, b_mod, norm_mix, norm_mlp, w_qkv, b_qkv, w_o, b_o, sinks, w_pw1, b_pw1, w_dw, b_dw, conv_ln_g, conv_ln_b, w_pw2, b_pw2, w_up, w_down, final_norm, loss_target, m_w_mod, m_b_mod, m_norm_mix, m_norm_mlp, m_w_qkv, m_b_qkv, m_w_o, m_b_o, m_sinks, m_w_pw1, m_b_pw1, m_w_dw, m_b_dw, m_conv_ln_g, m_conv_ln_b, m_w_pw2, m_b_pw2, m_w_up, m_w_down, m_final_norm, v_w_mod, v_b_mod, v_norm_mix, v_norm_mlp, v_w_qkv, v_b_qkv, v_w_o, v_b_o, v_sinks, v_w_pw1, v_b_pw1, v_w_dw, v_b_dw, v_conv_ln_g, v_conv_ln_b, v_w_pw2, v_b_pw2, v_w_up, v_w_down, v_final_norm):
    bsz, s, d = x.shape
    t = bsz * s
    depth = w_mod.shape[0]
    n_attn, n_conv = w_qkv.shape[0], w_pw1.shape[0]
    qkv_dim = d + 2 * N_KV_HEADS * HEAD_DIM
    mx, my, mc, mq = _position()
    me = 4 * mx + 2 * my + mc
    my_c = jnp.reshape(mc, (1,)).astype(jnp.int32)
    my_qc = jnp.stack([mq, mc]).astype(jnp.int32)
    pending = _Exchanges()
    SMALL, MEDIUM, LARGE = 2.5, 3.5, 6.5

    weights = {}
    order = []
    for i in range(depth):
        j = i // 2
        order += ([(("qkv", j), w_qkv[j], True), (("o", j), w_o[j], False)] if i % 2 == 0 else
                  [(("pw1", j), w_pw1[j], True), (("pw2", j), w_pw2[j], False)])
        order += [(("up", i), w_up[i], True), (("down", i), w_down[i], False)]
    for key, shard, by_cols in order:
        def ready(buf, key=key, by_cols=by_cols):
            weights[key] = jnp.transpose(buf, (1, 0, 2)).reshape(buf.shape[1], -1) if by_cols else buf.reshape(-1, buf.shape[2])
        ici, forward = _gather_plans(shard.astype(BF16), ready)
        ici.then = lambda outs, forward=forward: pending.add(forward(outs), front=True)
        pending.add(ici)

    def weight(key):
        pending.flush(SMALL, until=lambda: key in weights)
        return weights[key]

    small_sharded = [b_pw1, w_dw, b_dw, conv_ln_g, conv_ln_b, b_pw2]
    c_pad = jnp.pad(c, ((0, SUBLANES - bsz), (0, 0)))
    gathered = _all_gather_small(jnp.concatenate([c_pad, _pack(small_sharded, d)], axis=0), name="gather_c",
                                 comm=pending.take(SMALL, True))
    gathered = gathered.reshape(N_DEV, -1, d)
    c_all = gathered[:, :bsz].reshape(N_DEV * bsz, d)
    per_chip = [_unpack(gathered[2 * q, SUBLANES:], [a.shape for a in small_sharded]) for q in range(N_CHIPS)]
    b_pw1_f, w_dw_f, b_dw_f, ln_g_f, ln_b_f, b_pw2_f = [jnp.concatenate([per_chip[q][k] for q in range(N_CHIPS)], axis=-1)
                                                           for k in range(len(small_sharded))]
    w_dw_f = jnp.pad(w_dw_f, ((0, 0), (0, CONV_HALO - CONV_WIDTH), (0, 0)))

    n_mod = w_mod.shape[2]
    b_mod_cols = lax.dynamic_slice_in_dim(b_mod, mq * n_mod, n_mod, axis=1).reshape(depth, 1, n_mod)
    mod_part = _mod_fwd(c_all, w_mod, b_mod_cols, name="mod_fwd")
    mod_all = _all_gather_small(mod_part.reshape(depth * N_DEV * bsz, n_mod), name="gather_mod",
                                comm=pending.take(0.0))
    mod_all = mod_all.reshape(N_DEV, depth, N_DEV * bsz, n_mod)[0::2]
    mod = lax.dynamic_slice_in_dim(mod_all, me * bsz, bsz, axis=2)
    mod = jnp.transpose(mod, (1, 2, 0, 3)).reshape(depth, bsz, N_MOD, 1, d)
    mods = [[mod[i][:, k] for k in range(N_MOD)] for i in range(depth)]

    saved = []
    xc = x
    h1 = _normmod(xc, norm_mix[0][None], mods[0][1], mods[0][0], name="normmod")
    for i in range(depth):
        j = i // 2
        sh1, sc1, g1, sh2, sc2, g2 = mods[i]
        mlp_norm = (norm_mlp[i][None], sc2, sh2)
        if i % 2 == 0:
            wq, wo = weight(("qkv", j)), weight(("o", j))
            qkv = _mm(h1.reshape(t, d), wq, bias=b_qkv[j], tn=qkv_dim, name="mm_qkv",
                      comm=pending.take(SMALL, True)).reshape(bsz, s, qkv_dim)
            mix = _attn_fwd(qkv, sinks[j], name="attn_fwd", comm=pending.take(LARGE, True))
            y1, x1, h2 = _mm(mix.reshape(t, d), wo, bias=b_o[j], epi="resid", resid=xc.reshape(t, d), gate=g1, seq=s,
                             norm=mlp_norm, tn=d, name="mm_out", comm=pending.take(SMALL, True))
            extra = (qkv, mix)
        else:
            wp1, wp2 = weight(("pw1", j)), weight(("pw2", j))
            u = _mm(h1.reshape(t, d), wp1, bias=b_pw1_f[j], out_dtype=F32, tn=d, name="mm_pw1",
                    comm=pending.take(MEDIUM, True)).reshape(bsz, s, 2 * d)
            mix, conv_v = _conv_fwd(u, w_dw_f[j], b_dw_f[j][None], ln_g_f[j][None], ln_b_f[j][None], name="conv_fwd",
                                    comm=pending.take(LARGE, True))
            y1, x1, h2 = _mm(mix.reshape(t, d), wp2, bias=b_pw2_f[j], epi="resid", resid=xc.reshape(t, d), gate=g1, seq=s,
                             norm=mlp_norm, tn=d, name="mm_out", comm=pending.take(SMALL, True))
            extra = (u, mix, conv_v)
        act, slope = _mm(h2, weight(("up", i)), epi="relu2", tm=512, tn=2 * d, name="mm_up", comm=pending.take(LARGE, True))
        mix_norm = (norm_mix[i + 1][None], mods[i + 1][1], mods[i + 1][0]) if i + 1 < depth else None
        y2, x2, *h_next = _mm(act, weight(("down", i)), epi="resid", resid=x1, gate=g2, seq=s, norm=mix_norm,
                              tm=512, tn=d, name="mm_down", comm=pending.take(LARGE, True))
        saved.append((xc, h1, extra, y1, x1.reshape(bsz, s, d), h2, act, y2, slope))
        xc = x2.reshape(bsz, s, d)
        h1 = h_next[0] if h_next else None
    pending.flush(LARGE)

    last_y2, last_g2 = saved[-1][7].reshape(bsz, s, d), mods[-1][5]
    dx, loss_cols, d_final, dyb, dg2, _ = _loss_head(xc, loss_target, final_norm[None], (last_y2, last_g2), name="loss_head")
    loss = lax.psum(0.5 / d * jnp.sum(loss_cols), ("x", "y", "c"))

    totals = {}

    def reduce_scatter(name, layer, n_layers, grad):
        view = grad.reshape(N_CHIPS, 2, grad.shape[1] // 2, grad.shape[2])
        half_shape = (N_CHIPS, 1) + view.shape[2:]

        def scatter(outs):
            chip = _add_half(view, outs[0], my_c, name="rs_chipsum")
            copies = [(lambda ins, outs, pos, kind=kind: ins[0].at[pl.ds(_peer_shard(kind, pos[0], pos[1]), 1)],
                       lambda ins, outs, pos, k=k: outs[0].at[pl.ds(k, 1)],
                       lambda ins, outs, pos, k=k: outs[0].at[pl.ds(k, 1)], kind) for k, kind in enumerate(CHIP_KINDS)]

            def total(outs):
                totals[name] = _add_pieces(chip, outs[0], my_qc, totals.get(name), layer, n_layers, name="rs_total")

            pending.add(_Plan("ici", 3 * _nbytes(chip.shape[1:], BF16) / MB, [chip],
                              [jax.ShapeDtypeStruct((3,) + chip.shape[1:], BF16)], copies, then=total))

        pending.add(_Plan("d2d", _nbytes(half_shape, F32) / MB, [view], [jax.ShapeDtypeStruct(half_shape, F32)],
                          [(lambda ins, outs, pos: ins[0].at[:, pl.ds(1 - pos[2], 1)],
                            lambda ins, outs, pos: outs[0], lambda ins, outs, pos: outs[0], "c")], then=scatter))

    dmods, small = [None] * depth, {}
    for i in reversed(range(depth)):
        j = i // 2
        xin, h1, extra, y1, x1, h2, act, _, slope = saved[i]
        sh1, sc1, g1, sh2, sc2, g2 = mods[i]
        dyb = dyb.reshape(t, d)
        reduce_scatter("down", i, depth,
                       _mm_tn(act, dyb, name="dw_down", comm=pending.take(LARGE)).reshape(N_CHIPS, -1, d))
        dup = _mm(dyb, weights["down", i], nt=True, epi="dact", act=slope, tm=512, tn=2 * d, name="mm_dact",
                  comm=pending.take(LARGE))
        reduce_scatter("up", i, depth,
                       _mm_tn(h2.reshape(t, d), dup, col_shards=N_CHIPS, name="dw_up", comm=pending.take(LARGE)))
        dh2 = _mm(dup, weights["up", i], nt=True, tm=512, tn=d, name="mm_dh2",
                  comm=pending.take(LARGE)).reshape(bsz, s, d)
        dx1, p2, dsh2, dyb, dg1, sdx = _normmod_bwd(x1, dh2, dx, norm_mlp[i][None], sc2, (y1.reshape(bsz, s, d), g1),
                                                    name="normmod_bwd")
        dyb = dyb.reshape(t, d)
        d_bias_out = jnp.sum(g1 * sdx, axis=(0, 1))
        if i % 2 == 0:
            qkv, mix = extra
            small["b_o", j] = d_bias_out
            reduce_scatter("o", j, n_attn,
                           _mm_tn(mix.reshape(t, d), dyb, name="dw_sq", comm=pending.take(SMALL)).reshape(N_CHIPS, -1, d))
            dmix = _mm(dyb, weights["o", j], nt=True, tn=d, name="mm_dmix", comm=pending.take(SMALL)).reshape(bsz, s, d)
            dqkv, d_bqkv, d_sink = _attn_bwd(qkv, dmix, sinks[j], name="attn_bwd", comm=pending.take(2 * LARGE))
            small["b_qkv", j], small["sinks", j] = d_bqkv[0], d_sink[:, 0]
            dqkv = dqkv.reshape(t, qkv_dim)
            dwq = _mm_tn(h1.reshape(t, d), dqkv, tn=qkv_dim, name="dw_qkv", comm=pending.take(SMALL))
            reduce_scatter("qkv", j, n_attn, jnp.transpose(dwq.reshape(d, N_CHIPS, -1), (1, 0, 2)))
            dh1 = _mm(dqkv, weights["qkv", j], nt=True, tn=d, name="mm_dh1a", comm=pending.take(SMALL))
        else:
            u, mix, conv_v = extra
            small["b_pw2", j] = d_bias_out
            reduce_scatter("pw2", j, n_conv,
                           _mm_tn(mix.reshape(t, d), dyb, name="dw_sq", comm=pending.take(SMALL)).reshape(N_CHIPS, -1, d))
            dz = _mm(dyb, weights["pw2", j], nt=True, out_dtype=F32, tn=d, name="mm_dz", comm=pending.take(SMALL)).reshape(bsz, s, d)
            dv, d_lng, d_lnb, d_bdw = _conv_bwd_ln(dz, conv_v, ln_g_f[j][None], ln_b_f[j][None], name="conv_bwd_ln")
            du, d_bpw1, d_wdw = _conv_bwd_taps(dv, u, w_dw_f[j], name="conv_bwd_taps", comm=pending.take(2 * LARGE))
            small["ln_g", j], small["ln_b", j], small["b_dw", j] = d_lng[0], d_lnb[0], d_bdw[0]
            small["b_pw1", j], small["w_dw", j] = d_bpw1[0], jnp.sum(d_wdw[:CONV_WIDTH], axis=1)
            du = du.reshape(t, 2 * d)
            reduce_scatter("pw1", j, n_conv,
                           _mm_tn(h1.reshape(t, d), du, col_shards=N_CHIPS, name="dw_pw1", comm=pending.take(MEDIUM)))
            dh1 = _mm(du, weights["pw1", j], nt=True, tn=d, name="mm_dh1c", comm=pending.take(MEDIUM))
        below = (saved[i - 1][7].reshape(bsz, s, d), mods[i - 1][5]) if i > 0 else None
        dx, p1, dsh1, *gate_grads = _normmod_bwd(xin, dh1.reshape(bsz, s, d), dx1, norm_mix[i][None], sc1, below,
                                                 name="normmod_bwd")
        small["norm_mix", i] = jnp.sum((1.0 + sc1) * p1, axis=(0, 1))
        small["norm_mlp", i] = jnp.sum((1.0 + sc2) * p2, axis=(0, 1))
        dmods[i] = jnp.concatenate([dsh1, norm_mix[i] * p1, dg1, dsh2, norm_mlp[i] * p2, dg2], axis=1)
        if i > 0:
            dyb, dg2, _ = gate_grads
    grad_x = dx

    small_names = ([("norm_mix", i) for i in range(depth)] + [("norm_mlp", i) for i in range(depth)]
                   + [(nm, j) for nm in ("b_qkv", "b_o", "sinks") for j in range(n_attn)]
                   + [(nm, j) for nm in ("b_pw1", "w_dw", "b_dw", "ln_g", "ln_b", "b_pw2") for j in range(n_conv)])
    small_list = [small[k] for k in small_names] + [d_final[0]]
    small_pack = _pack(small_list, d)
    dmod_rows = jnp.stack(dmods).reshape(depth * bsz * N_MOD, d)
    n_small = small_pack.shape[0]
    gathered = _all_gather_small(jnp.concatenate([small_pack, _pack([dmod_rows], d)], axis=0), name="gather_small",
                                 comm=pending.take(LARGE, True))
    gathered = gathered.reshape(N_DEV, -1, d)
    dmod_all = gathered[:, n_small:n_small + depth * bsz * N_MOD].reshape(N_DEV, depth, bsz, N_MOD, d)
    small_sum, g_b_mod = _sum_devices(gathered[:, :n_small], dmod_all, name="sum_devices")
    small_tot = dict(zip(small_names + ["final_norm"], _unpack(small_sum, [a.shape for a in small_list])))
    stacked = lambda nm, count: jnp.stack([small_tot[nm, k] for k in range(count)])
    g_norm_mix, g_norm_mlp = stacked("norm_mix", depth), stacked("norm_mlp", depth)
    g_b_qkv, g_b_o, g_sinks = stacked("b_qkv", n_attn), stacked("b_o", n_attn), stacked("sinks", n_attn)
    g_final = small_tot["final_norm"]
    g_b_mod = g_b_mod.reshape(depth, N_MOD * d)
    shard_cols = lambda g: lax.dynamic_slice_in_dim(g, mq * (g.shape[-1] // N_CHIPS), g.shape[-1] // N_CHIPS, axis=g.ndim - 1)
    g_b_pw1, g_w_dw, g_b_dw, g_ln_g, g_ln_b, g_b_pw2 = [shard_cols(stacked(nm, n_conv))
                                                        for nm in ("b_pw1", "w_dw", "b_dw", "ln_g", "ln_b", "b_pw2")]

    dmod_cols = jnp.transpose(dmod_all, (1, 0, 2, 3, 4)).reshape(depth, N_DEV * bsz, N_MOD * d)
    dmod_cols = lax.dynamic_slice_in_dim(dmod_cols, mq * n_mod, n_mod, axis=2)
    g_w_mod = _mod_bwd(c_all, dmod_cols, name="mod_bwd")

    def adam(w, g, m, v, name):
        two_d = lambda a: a.reshape(-1, a.shape[-1])
        return [r.reshape(w.shape) for r in _adamw(two_d(w), two_d(g), two_d(m), two_d(v), name=name)]

    results = {"w_mod": (g_w_mod,) + tuple(adam(w_mod, g_w_mod, m_w_mod, v_w_mod, "adamw"))}

    pending.flush(LARGE)
    names = ["qkv", "o", "pw1", "pw2", "up", "down"]
    bufs = [totals[nm] for nm in names]
    copies = []
    for a, buf in enumerate(bufs):
        for layer in range(buf.shape[0]):
            half = lambda ref, h, layer=layer: ref.at[pl.ds(layer, 1), pl.ds(h, 1)]
            copies.append((lambda ins, outs, pos, a=a, half=half: half(ins[a], pos[2]),
                           lambda ins, outs, pos, a=a, half=half: half(outs[a], pos[2]),
                           lambda ins, outs, pos, a=a, half=half: half(outs[a], 1 - pos[2]), "c"))
    shared = {}
    _exchange(_Plan("d2d", 0.0, bufs, [jax.ShapeDtypeStruct(b.shape, F32) for b in bufs], copies,
                    {a: a for a in range(len(bufs))}, lambda outs: shared.update(zip(names, outs))), name="rs_share")
    g_w_qkv, g_w_o, g_w_pw1, g_w_pw2, g_w_up, g_w_down = [
        shared[nm].reshape(shared[nm].shape[0], -1, shared[nm].shape[3]) for nm in names]

    for nm, w, g, m, v in (("w_qkv", w_qkv, g_w_qkv, m_w_qkv, v_w_qkv),
                           ("w_o", w_o, g_w_o, m_w_o, v_w_o), ("w_pw1", w_pw1, g_w_pw1, m_w_pw1, v_w_pw1),
                           ("w_pw2", w_pw2, g_w_pw2, m_w_pw2, v_w_pw2), ("w_up", w_up, g_w_up, m_w_up, v_w_up),
                           ("w_down", w_down, g_w_down, m_w_down, v_w_down)):
        results[nm] = (g,) + tuple(adam(w, g, m, v, "adamw"))
    small_w = dict(b_mod=(b_mod, g_b_mod, m_b_mod, v_b_mod), norm_mix=(norm_mix, g_norm_mix, m_norm_mix, v_norm_mix),
                   norm_mlp=(norm_mlp, g_norm_mlp, m_norm_mlp, v_norm_mlp), b_qkv=(b_qkv, g_b_qkv, m_b_qkv, v_b_qkv),
                   b_o=(b_o, g_b_o, m_b_o, v_b_o), sinks=(sinks, g_sinks, m_sinks, v_sinks),
                   b_pw1=(b_pw1, g_b_pw1, m_b_pw1, v_b_pw1), w_dw=(w_dw, g_w_dw, m_w_dw, v_w_dw),
                   b_dw=(b_dw, g_b_dw, m_b_dw, v_b_dw), conv_ln_g=(conv_ln_g, g_ln_g, m_conv_ln_g, v_conv_ln_g),
                   conv_ln_b=(conv_ln_b, g_ln_b, m_conv_ln_b, v_conv_ln_b), b_pw2=(b_pw2, g_b_pw2, m_b_pw2, v_b_pw2),
                   final_norm=(final_norm, g_final, m_final_norm, v_final_norm))
    names = list(small_w)
    deltas, new_ms, new_vs = _adamw_packed(*[[small_w[nm][k] for nm in names] for k in range(4)], d, name="adamw_small")
    for k, nm in enumerate(names):
        results[nm] = (small_w[nm][1], deltas[k], new_ms[k], new_vs[k])

    weight_order = ["w_mod", "b_mod", "norm_mix", "norm_mlp", "w_qkv", "b_qkv", "w_o", "b_o", "sinks", "w_pw1", "b_pw1",
                    "w_dw", "b_dw", "conv_ln_g", "conv_ln_b", "w_pw2", "b_pw2", "w_up", "w_down", "final_norm"]
    return (loss, grad_x, *[results[nm][0] for nm in weight_order], *[results[nm][1] for nm in weight_order],
            *[results[nm][2] for nm in weight_order], *[results[nm][3] for nm in weight_order])
```

```python
import functools

import jax
import jax.numpy as jnp
from jax import lax
from jax.experimental import pallas as pl
from jax.experimental.pallas import tpu as pltpu

F32, BF16 = jnp.float32, jnp.bfloat16
MESH = pl.DeviceIdType.MESH
N_CHIPS = 4
N_DEV = 8
LANES = 128
SUBLANES = 8
VMEM_LIMIT = 48 * 1024 * 1024

NORM_EPS = 1e-6
HEAD_DIM = 64
N_KV_HEADS = 2
WINDOW = 128
CONV_WIDTH = 31
CONV_HALO = 32
CONV_ROWS = 32
N_MOD = 6

ADAM_LR, ADAM_B1, ADAM_B2, ADAM_EPS, ADAM_WD, ADAM_STEP = 0.001, 0.9, 0.999, 1e-08, 0.01, 10

HBM_SPEC = pl.BlockSpec(memory_space=pltpu.HBM)
VMEM_SPEC = pl.BlockSpec(memory_space=pltpu.VMEM)
SMEM_SPEC = pl.BlockSpec(memory_space=pltpu.SMEM)


def _params(*sem):
    return pltpu.CompilerParams(dimension_semantics=sem or None, vmem_limit_bytes=VMEM_LIMIT)


def _row_tile(rows, width_bytes, target=1 << 20):
    t = rows
    while t % 2 == 0 and t > SUBLANES and t * width_bytes > target:
        t //= 2
    return t


CHIP_KINDS = ("x", "y", "xy")


def _position():
    x, y, c = lax.axis_index("x"), lax.axis_index("y"), lax.axis_index("c")
    return x, y, c, 2 * x + y


def _peer(kind, x, y, c):
    return {"c": (x, y, 1 - c), "x": (1 - x, y, c), "y": (x, 1 - y, c), "xy": (1 - x, 1 - y, c)}[kind]


def _peer_shard(kind, x, y):
    px, py, _ = _peer(kind, x, y, 0)
    return 2 * px + py


class _Plan:
    def __init__(self, link, cost, operands, out_shapes, copies, aliases=None, then=None):
        self.link, self.cost = link, cost
        self.operands, self.out_shapes, self.copies = list(operands), list(out_shapes), list(copies)
        self.aliases, self.then = dict(aliases or {}), then


def _merge(plans):
    operands, out_shapes, copies, aliases, thens = [], [], [], {}, []
    for p in plans:
        i0, o0 = len(operands), len(out_shapes)
        i1, o1 = i0 + len(p.operands), o0 + len(p.out_shapes)

        def shifted(f, i0=i0, i1=i1, o0=o0, o1=o1):
            return lambda ins, outs, pos: f(ins[i0:i1], outs[o0:o1], pos)

        copies += [(shifted(src), shifted(dst), shifted(land), kind) for src, dst, land, kind in p.copies]
        aliases.update({i0 + k: o0 + v for k, v in p.aliases.items()})
        operands += p.operands
        out_shapes += p.out_shapes
        thens.append((p.then, o0, o1))

    def then(outs):
        for f, o0, o1 in thens:
            if f is not None:
                f(outs[o0:o1])

    return _Plan("mixed", sum(p.cost for p in plans), operands, out_shapes, copies, aliases, then)


def _call(body, *, name, out_shape, operands, grid=(), in_specs=(), out_specs=(), scratch_shapes=(), sem=(), comm=None):
    single = not isinstance(out_shape, (tuple, list))
    out_shape = [out_shape] if single else list(out_shape)
    out_specs = [out_specs] if single else list(out_specs)
    if comm is None:
        res = pl.pallas_call(body, out_shape=out_shape, grid=grid, in_specs=list(in_specs), out_specs=out_specs,
                             scratch_shapes=list(scratch_shapes), name=name, compiler_params=_params(*sem))(*operands)
        return res[0] if single else res
    n_in, n_out, n_scr = len(operands), len(out_shape), len(scratch_shapes)
    c_in, c_out, n_cp = len(comm.operands), len(comm.out_shapes), len(comm.copies)

    def wrapped(*refs):
        ins, refs = refs[:n_in], refs[n_in:]
        cins, refs = refs[:c_in], refs[c_in:]
        outs, refs = refs[:n_out], refs[n_out:]
        couts, refs = refs[:c_out], refs[c_out:]
        scr, (send_sems, recv_sems) = refs[:n_scr], refs[n_scr:]

        def descriptors():
            pos = _position()
            sends, lands = [], []
            for k, (src, dst, landing, kind) in enumerate(comm.copies):
                common = dict(send_sem=send_sems.at[k], recv_sem=recv_sems.at[k],
                              device_id=_peer(kind, *pos[:3]), device_id_type=MESH)
                sends.append(pltpu.make_async_remote_copy(src_ref=src(cins, couts, pos), dst_ref=dst(cins, couts, pos), **common))
                lands.append(pltpu.make_async_remote_copy(src_ref=src(cins, couts, pos), dst_ref=landing(cins, couts, pos), **common))
            return sends, lands

        def start():
            for cp in descriptors()[0]:
                cp.start()

        def finish():
            sends, lands = descriptors()
            for cp in lands:
                cp.wait_recv()
            for cp in sends:
                cp.wait_send()

        if not grid:
            start()
            body(*ins, *outs, *scr)
            finish()
        else:
            ids = [pl.program_id(ax) for ax in range(len(grid))]
            first, last = ids[0] == 0, ids[0] == grid[0] - 1
            for ax in range(1, len(grid)):
                first, last = first & (ids[ax] == 0), last & (ids[ax] == grid[ax] - 1)
            pl.when(first)(start)
            body(*ins, *outs, *scr)
            pl.when(last)(finish)

    res = pl.pallas_call(wrapped, out_shape=out_shape + comm.out_shapes, grid=grid,
                         in_specs=list(in_specs) + [HBM_SPEC] * c_in, out_specs=out_specs + [HBM_SPEC] * c_out,
                         scratch_shapes=list(scratch_shapes) + [pltpu.SemaphoreType.DMA((n_cp,)), pltpu.SemaphoreType.DMA((n_cp,))],
                         input_output_aliases={n_in + k: n_out + v for k, v in comm.aliases.items()},
                         name=name, compiler_params=_params(*["arbitrary"] * len(grid)))(*operands, *comm.operands)
    if comm.then is not None:
        comm.then(res[n_out:])
    return res[0] if single else res[:n_out]


def _exchange(plan, *, name):
    _call(lambda: None, name=name, out_shape=[], operands=[], comm=plan)


def _mm(a, b, *, name, nt=False, tm=1024, tn=512, epi="plain", out_dtype=BF16,
        bias=None, act=None, resid=None, gate=None, seq=None, norm=None, comm=None):
    m, k = a.shape
    n = b.shape[0] if nt else b.shape[1]
    tm, tn = min(tm, m, seq or m), min(tn, n)
    assert m % tm == 0 and n % tn == 0
    dims = (((1,), (1,)), ((), ())) if nt else (((1,), (0,)), ((), ()))
    tile = pl.BlockSpec((tm, tn), lambda j, i: (i, j))
    operands = [a, b]
    in_specs = [pl.BlockSpec((tm, k), lambda j, i: (i, 0)),
                pl.BlockSpec((tn, k), lambda j, i: (j, 0)) if nt else pl.BlockSpec((k, tn), lambda j, i: (0, j))]
    if bias is not None:
        operands.append(bias.reshape(1, n))
        in_specs.append(pl.BlockSpec((1, tn), lambda j, i: (0, j)))
    if epi == "dact":
        operands.append(act)
        in_specs.append(tile)
    if epi == "resid":
        assert seq % tm == 0
        per_ex = pl.BlockSpec((None, 1, tn), lambda j, i: (i * tm // seq, 0, j))
        operands += [resid, gate]
        in_specs += [tile, per_ex]
        out_shape = (jax.ShapeDtypeStruct((m, n), BF16), jax.ShapeDtypeStruct((m, n), F32))
        out_specs = (tile, tile)
        if norm is not None:
            assert tn == n
            operands += list(norm)
            in_specs += [pl.BlockSpec((1, tn), lambda j, i: (0, j)), per_ex, per_ex]
            out_shape += (jax.ShapeDtypeStruct((m, n), BF16),)
            out_specs += (tile,)
    elif epi == "relu2":
        out_shape = (jax.ShapeDtypeStruct((m, n), BF16), jax.ShapeDtypeStruct((m, n), BF16))
        out_specs = (tile, tile)
    else:
        out_shape = jax.ShapeDtypeStruct((m, n), out_dtype)
        out_specs = tile

    def body(*refs):
        it = iter(refs)
        a_ref, b_ref = next(it), next(it)
        acc = lax.dot_general(a_ref[...], b_ref[...], dims, preferred_element_type=F32)
        if bias is not None:
            acc = acc + next(it)[...]
        if epi == "plain":
            next(it)[...] = acc.astype(out_dtype)
        elif epi == "relu2":
            r = jnp.maximum(acc, 0.0)
            next(it)[...] = (r * r).astype(BF16)
            next(it)[...] = (2.0 * r).astype(BF16)
        elif epi == "dact":
            slope_ref = next(it)
            next(it)[...] = (acc * slope_ref[...].astype(F32)).astype(out_dtype)
        else:
            resid_ref, gate_ref = next(it), next(it)
            norm_refs = [next(it) for _ in (norm or ())]
            y_ref, x_ref = next(it), next(it)
            y_ref[...] = acc.astype(BF16)
            x_new = resid_ref[...] + gate_ref[...] * acc
            x_ref[...] = x_new
            if norm is not None:
                g_ref, sc_ref, sh_ref = norm_refs
                r = lax.rsqrt(jnp.mean(x_new * x_new, axis=-1, keepdims=True) + NORM_EPS)
                next(it)[...] = (x_new * r * g_ref[...] * (1.0 + sc_ref[...]) + sh_ref[...]).astype(BF16)

    return _call(body, name=name, out_shape=out_shape, operands=operands, grid=(n // tn, m // tm), in_specs=in_specs,
                 out_specs=out_specs, sem=("parallel", "parallel"), comm=comm)


def _mm_tn(a, b, *, name, tm=1024, tn=1024, tk=2048, col_shards=None, comm=None):
    t, m = a.shape
    n = b.shape[1]
    tm, tk = min(tm, m), min(tk, t)
    if col_shards is None:
        tn = min(tn, n)
        out_shape = jax.ShapeDtypeStruct((m, n), F32)
        out_spec = pl.BlockSpec((tm, tn), lambda i, j, k: (i, j))
    else:
        per = n // col_shards
        tn = min(tn, per)
        assert per % tn == 0
        out_shape = jax.ShapeDtypeStruct((col_shards, m, per), F32)
        out_spec = pl.BlockSpec((None, tm, tn), lambda i, j, k: (j // (per // tn), i, j % (per // tn)))
    assert m % tm == 0 and n % tn == 0 and t % tk == 0

    def body(a_ref, b_ref, o_ref):
        @pl.when(pl.program_id(2) == 0)
        def _():
            o_ref[...] = jnp.zeros_like(o_ref)

        o_ref[...] += lax.dot_general(a_ref[...], b_ref[...], (((0,), (0,)), ((), ())),
                                      preferred_element_type=F32)

    return _call(body, name=name, out_shape=out_shape, operands=[a, b], grid=(m // tm, n // tn, t // tk),
                 in_specs=[pl.BlockSpec((tk, tm), lambda i, j, k: (k, i)), pl.BlockSpec((tk, tn), lambda i, j, k: (k, j))],
                 out_specs=out_spec, sem=("parallel", "parallel", "arbitrary"), comm=comm)


def _normmod(x, gamma, sc, sh, *, name):
    bsz, s, d = x.shape
    ts = _row_tile(s, d * 4, 2 << 20)

    def body(x_ref, g_ref, sc_ref, sh_ref, o_ref):
        xf = x_ref[...]
        r = lax.rsqrt(jnp.mean(xf * xf, axis=-1, keepdims=True) + NORM_EPS)
        o_ref[...] = (xf * r * g_ref[...] * (1.0 + sc_ref[...]) + sh_ref[...]).astype(BF16)

    row = pl.BlockSpec((None, ts, d), lambda b, i: (b, i, 0))
    per_ex = pl.BlockSpec((None, 1, d), lambda b, i: (b, 0, 0))
    return pl.pallas_call(body, out_shape=jax.ShapeDtypeStruct(x.shape, BF16), grid=(bsz, s // ts),
                          in_specs=[row, pl.BlockSpec((1, d), lambda b, i: (0, 0)), per_ex, per_ex],
                          out_specs=row, name=name, compiler_params=_params("parallel", "parallel"))(x, gamma, sc, sh)


def _gate_grads(dxv, y_ref, gate_ref, dy_ref, dg_ref, sdx_ref):
    dy_ref[...] = (gate_ref[...] * dxv).astype(BF16)

    @pl.when(pl.program_id(1) == 0)
    def _():
        dg_ref[...] = jnp.zeros_like(dg_ref)
        sdx_ref[...] = jnp.zeros_like(sdx_ref)

    dg_ref[...] += jnp.sum(dxv * y_ref[...].astype(F32), axis=0, keepdims=True)
    sdx_ref[...] += jnp.sum(dxv, axis=0, keepdims=True)


def _normmod_bwd(x, dh, dres, gamma, sc, producer=None, *, name):
    bsz, s, d = x.shape
    ts = _row_tile(s, d * 4, 1 << 20)

    def body(x_ref, dh_ref, dres_ref, g_ref, sc_ref, *rest):
        dx_ref, p_ref, dsh_ref = rest[-6:-3] if producer else rest
        xf = x_ref[...]
        r = lax.rsqrt(jnp.mean(xf * xf, axis=-1, keepdims=True) + NORM_EPS)
        xhat = xf * r
        dh_v = dh_ref[...].astype(F32)
        dxhat = dh_v * (g_ref[...] * (1.0 + sc_ref[...]))
        dxv = dres_ref[...] + r * (dxhat - xhat * jnp.mean(dxhat * xhat, axis=-1, keepdims=True))
        dx_ref[...] = dxv

        @pl.when(pl.program_id(1) == 0)
        def _():
            p_ref[...] = jnp.zeros_like(p_ref)
            dsh_ref[...] = jnp.zeros_like(dsh_ref)

        p_ref[...] += jnp.sum(dh_v * xhat, axis=0, keepdims=True)
        dsh_ref[...] += jnp.sum(dh_v, axis=0, keepdims=True)
        if producer:
            _gate_grads(dxv, rest[0], rest[1], *rest[-3:])

    row = pl.BlockSpec((None, ts, d), lambda b, i: (b, i, 0))
    per_ex = pl.BlockSpec((None, 1, d), lambda b, i: (b, 0, 0))
    vec = jax.ShapeDtypeStruct((bsz, 1, d), F32)
    gate_in, gate_out, gate_specs = (), (), ()
    if producer:
        gate_in, gate_out, gate_specs = (row, per_ex), (jax.ShapeDtypeStruct(x.shape, BF16), vec, vec), (row, per_ex, per_ex)
    return pl.pallas_call(body, out_shape=(jax.ShapeDtypeStruct(x.shape, F32), vec, vec) + gate_out, grid=(bsz, s // ts),
                          in_specs=[row, row, row, pl.BlockSpec((1, d), lambda b, i: (0, 0)), per_ex, *gate_in],
                          out_specs=(row, per_ex, per_ex) + gate_specs, name=name,
                          compiler_params=_params("parallel", "arbitrary"))(x, dh, dres, gamma, sc, *(producer or ()))


def _loss_head(x, target, gamma, producer, *, name):
    bsz, s, d = x.shape
    ts = _row_tile(s, d * 4, 1 << 20)

    def body(x_ref, t_ref, g_ref, y_ref, gate_ref, dx_ref, loss_ref, dg_ref, *gate_refs):
        xf = x_ref[...]
        r = lax.rsqrt(jnp.mean(xf * xf, axis=-1, keepdims=True) + NORM_EPS)
        xhat = xf * r
        err = xhat * g_ref[...] - t_ref[...]
        dy = err * (1.0 / d)
        dxhat = dy * g_ref[...]
        dxv = r * (dxhat - xhat * jnp.mean(dxhat * xhat, axis=-1, keepdims=True))
        dx_ref[...] = dxv

        @pl.when((pl.program_id(0) == 0) & (pl.program_id(1) == 0))
        def _():
            loss_ref[...] = jnp.zeros_like(loss_ref)
            dg_ref[...] = jnp.zeros_like(dg_ref)

        loss_ref[...] += jnp.sum(err * err, axis=0, keepdims=True)
        dg_ref[...] += jnp.sum(dy * xhat, axis=0, keepdims=True)
        _gate_grads(dxv, y_ref, gate_ref, *gate_refs)

    row = pl.BlockSpec((None, ts, d), lambda b, i: (b, i, 0))
    one = pl.BlockSpec((1, d), lambda b, i: (0, 0))
    per_ex = pl.BlockSpec((None, 1, d), lambda b, i: (b, 0, 0))
    vec, ex_vec = jax.ShapeDtypeStruct((1, d), F32), jax.ShapeDtypeStruct((bsz, 1, d), F32)
    return pl.pallas_call(body, out_shape=(jax.ShapeDtypeStruct(x.shape, F32), vec, vec,
                                           jax.ShapeDtypeStruct(x.shape, BF16), ex_vec, ex_vec), grid=(bsz, s // ts),
                          in_specs=[row, row, one, row, per_ex], out_specs=(row, one, one, row, per_ex, per_ex), name=name,
                          compiler_params=_params("arbitrary", "arbitrary"))(x, target, gamma, *producer)


def _alibi_slope(h, n_heads):
    return 2.0 ** (-8.0 * (h + 1) / n_heads)


def _attn_masks(first_block):
    qi = lax.broadcasted_iota(jnp.int32, (WINDOW, 2 * WINDOW), 0)
    ki = lax.broadcasted_iota(jnp.int32, (WINDOW, 2 * WINDOW), 1)
    dist = qi + WINDOW - ki
    first_key = jnp.where(first_block, WINDOW, 0)
    valid = (dist >= 0) & (dist < WINDOW) & (ki >= first_key)
    return dist.astype(F32), valid


def _dup_halves(span, kv, left):
    f = span.astype(F32)
    rolled = pltpu.roll(f, HEAD_DIM, axis=1)
    out = jnp.where(left, f, rolled) if kv == 0 else jnp.where(left, rolled, f)
    return out.astype(BF16)


def _attn_probs(s, h, n_heads, distf, valid, sink):
    s = s * (HEAD_DIM ** -0.5) - _alibi_slope(h, n_heads) * distf
    s = jnp.where(valid, s, -1e30)
    m = jnp.maximum(jnp.max(s, axis=-1, keepdims=True), sink)
    e = jnp.exp(s - m)
    e_sink = jnp.exp(sink - m)
    inv = 1.0 / (jnp.sum(e, axis=-1, keepdims=True) + e_sink)
    return e * inv, e_sink * inv


def _attn_specs(d, n_blocks, clamp):
    kcol = d // LANES
    cur = (lambda i: jnp.minimum(i, n_blocks - 1)) if clamp else (lambda i: i)
    prev = lambda i: jnp.maximum(cur(i) - 1, 0)
    kv = lambda col, blk: pl.BlockSpec((None, WINDOW, LANES), lambda b, i: (b, blk(i), col))
    return [pl.BlockSpec((None, WINDOW, d), lambda b, i: (b, cur(i), 0)),
            kv(kcol, prev), kv(kcol, cur), kv(kcol + 1, prev), kv(kcol + 1, cur)]


def _attn_fwd(qkv, sinks, *, name, comm=None):
    bsz, s, qkv_dim = qkv.shape
    d = qkv_dim - 2 * N_KV_HEADS * HEAD_DIM
    n_heads = d // HEAD_DIM
    group = n_heads // N_KV_HEADS
    pairs = group // 2
    n_blocks = s // WINDOW

    def body(q_ref, kp_ref, kc_ref, vp_ref, vc_ref, sink_ref, o_ref):
        left = lax.broadcasted_iota(jnp.int32, (1, LANES), 1) < HEAD_DIM
        distf, valid = _attn_masks(pl.program_id(1) == 0)
        kspan = jnp.concatenate([kp_ref[...], kc_ref[...]], axis=0)
        vspan = jnp.concatenate([vp_ref[...], vc_ref[...]], axis=0)
        for kv in range(N_KV_HEADS):
            kdup, vdup = _dup_halves(kspan, kv, left), _dup_halves(vspan, kv, left)
            res = []
            for par in range(2):
                keep = left if par == 0 else jnp.logical_not(left)
                cols = [pl.ds((kv * pairs + p) * LANES, LANES) for p in range(pairs)]
                lhs = jnp.concatenate([jnp.where(keep, q_ref[:, cl], jnp.zeros((), BF16)) for cl in cols], axis=0)
                sc = lax.dot_general(lhs, kdup, (((1,), (1,)), ((), ())), preferred_element_type=F32)
                probs = []
                for p in range(pairs):
                    h = kv * group + 2 * p + par
                    pr, _ = _attn_probs(sc[p * WINDOW:(p + 1) * WINDOW], h, n_heads, distf, valid, sink_ref[h])
                    probs.append(pr.astype(BF16))
                res.append(jnp.dot(jnp.concatenate(probs, axis=0), vdup, preferred_element_type=F32))
            for p in range(pairs):
                rows = slice(p * WINDOW, (p + 1) * WINDOW)
                o_ref[:, pl.ds((kv * pairs + p) * LANES, LANES)] = jnp.where(left, res[0][rows], res[1][rows]).astype(BF16)

    return _call(body, name=name, out_shape=jax.ShapeDtypeStruct((bsz, s, d), BF16), operands=[qkv] * 5 + [sinks],
                 grid=(bsz, n_blocks), in_specs=_attn_specs(d, n_blocks, False) + [SMEM_SPEC],
                 out_specs=pl.BlockSpec((None, WINDOW, d), lambda b, i: (b, i, 0)), sem=("parallel", "parallel"), comm=comm)


def _attn_bwd(qkv, do, sinks, *, name, comm=None):
    bsz, s, qkv_dim = qkv.shape
    d = qkv_dim - 2 * N_KV_HEADS * HEAD_DIM
    n_heads = d // HEAD_DIM
    group = n_heads // N_KV_HEADS
    pairs = group // 2
    n_blocks = s // WINDOW
    tn_dims = (((0,), (0,)), ((), ()))

    def body(q_ref, kp_ref, kc_ref, vp_ref, vc_ref, do_ref, sink_ref, dqkv_ref, colsum_ref, dsink_ref,
             dq_prev, dk_carry, dv_carry):
        b, i = pl.program_id(0), pl.program_id(1)
        left = lax.broadcasted_iota(jnp.int32, (1, LANES), 1) < HEAD_DIM

        @pl.when((b == 0) & (i == 0))
        def _():
            colsum_ref[...] = jnp.zeros_like(colsum_ref)
            dsink_ref[...] = jnp.zeros_like(dsink_ref)

        @pl.when(i == 0)
        def _():
            dqkv_ref[...] = jnp.zeros_like(dqkv_ref)
            dk_carry[...] = jnp.zeros_like(dk_carry)
            dv_carry[...] = jnp.zeros_like(dv_carry)

        @pl.when(i > 0)
        def _():
            dq_v = dq_prev[...]
            dqkv_ref[:, pl.ds(0, d)] = dq_v.astype(BF16)
            colsum_ref[:, pl.ds(0, d)] += jnp.sum(dq_v, axis=0, keepdims=True)

        @pl.when(i < n_blocks)
        def _():
            distf, valid = _attn_masks(i == 0)
            kspan = jnp.concatenate([kp_ref[...], kc_ref[...]], axis=0)
            vspan = jnp.concatenate([vp_ref[...], vc_ref[...]], axis=0)
            dk_blk, dv_blk = [], []
            for kv in range(N_KV_HEADS):
                kdup, vdup = _dup_halves(kspan, kv, left), _dup_halves(vspan, kv, left)
                dq_res, dk_sum, dv_sum = [], None, None
                for par in range(2):
                    keep = left if par == 0 else jnp.logical_not(left)
                    cols = [pl.ds((kv * pairs + p) * LANES, LANES) for p in range(pairs)]
                    zero = jnp.zeros((), BF16)
                    lhs = jnp.concatenate([jnp.where(keep, q_ref[:, cl], zero) for cl in cols], axis=0)
                    dol = jnp.concatenate([jnp.where(keep, do_ref[:, cl], zero) for cl in cols], axis=0)
                    sc = lax.dot_general(lhs, kdup, (((1,), (1,)), ((), ())), preferred_element_type=F32)
                    dp = lax.dot_general(dol, vdup, (((1,), (1,)), ((), ())), preferred_element_type=F32)
                    probs, dscores = [], []
                    for p in range(pairs):
                        h = kv * group + 2 * p + par
                        rows = slice(p * WINDOW, (p + 1) * WINDOW)
                        pr, p_sink = _attn_probs(sc[rows], h, n_heads, distf, valid, sink_ref[h])
                        delta = jnp.sum(pr * dp[rows], axis=-1, keepdims=True)
                        dscores.append((pr * (dp[rows] - delta) * (HEAD_DIM ** -0.5)).astype(BF16))
                        probs.append(pr.astype(BF16))
                        dsink_ref[pl.ds(h, 1), :] += jnp.zeros((1, LANES), F32) - jnp.sum(p_sink * delta)
                    ds_all = jnp.concatenate(dscores, axis=0)
                    p_all = jnp.concatenate(probs, axis=0)
                    dq_res.append(jnp.dot(ds_all, kdup, preferred_element_type=F32))
                    dk_par = lax.dot_general(ds_all, lhs, tn_dims, preferred_element_type=F32)
                    dv_par = lax.dot_general(p_all, dol, tn_dims, preferred_element_type=F32)
                    dk_sum = dk_par if dk_sum is None else dk_sum + dk_par
                    dv_sum = dv_par if dv_sum is None else dv_sum + dv_par
                for p in range(pairs):
                    rows = slice(p * WINDOW, (p + 1) * WINDOW)
                    dq_prev[:, pl.ds((kv * pairs + p) * LANES, LANES)] = jnp.where(left, dq_res[0][rows], dq_res[1][rows])
                dk_blk.append(dk_sum + pltpu.roll(dk_sum, HEAD_DIM, axis=1))
                dv_blk.append(dv_sum + pltpu.roll(dv_sum, HEAD_DIM, axis=1))
            dk_span = jnp.where(left, dk_blk[0], dk_blk[1])
            dv_span = jnp.where(left, dv_blk[0], dv_blk[1])
            dk_done = dk_carry[...] + dk_span[:WINDOW]
            dv_done = dv_carry[...] + dv_span[:WINDOW]
            dk_carry[...] = dk_span[WINDOW:]
            dv_carry[...] = dv_span[WINDOW:]

            @pl.when(i > 0)
            def _():
                dqkv_ref[:, pl.ds(d, LANES)] = dk_done.astype(BF16)
                dqkv_ref[:, pl.ds(d + LANES, LANES)] = dv_done.astype(BF16)
                colsum_ref[:, pl.ds(d, LANES)] += jnp.sum(dk_done, axis=0, keepdims=True)
                colsum_ref[:, pl.ds(d + LANES, LANES)] += jnp.sum(dv_done, axis=0, keepdims=True)

        @pl.when(i == n_blocks)
        def _():
            dk_done, dv_done = dk_carry[...], dv_carry[...]
            dqkv_ref[:, pl.ds(d, LANES)] = dk_done.astype(BF16)
            dqkv_ref[:, pl.ds(d + LANES, LANES)] = dv_done.astype(BF16)
            colsum_ref[:, pl.ds(d, LANES)] += jnp.sum(dk_done, axis=0, keepdims=True)
            colsum_ref[:, pl.ds(d + LANES, LANES)] += jnp.sum(dv_done, axis=0, keepdims=True)

    do_spec = pl.BlockSpec((None, WINDOW, d), lambda b, i: (b, jnp.minimum(i, n_blocks - 1), 0))
    out_shape = (jax.ShapeDtypeStruct((bsz, s, qkv_dim), BF16), jax.ShapeDtypeStruct((1, qkv_dim), F32),
                 jax.ShapeDtypeStruct((n_heads, LANES), F32))
    out_specs = (pl.BlockSpec((None, WINDOW, qkv_dim), lambda b, i: (b, jnp.maximum(i - 1, 0), 0)),
                 pl.BlockSpec((1, qkv_dim), lambda b, i: (0, 0)),
                 pl.BlockSpec((n_heads, LANES), lambda b, i: (0, 0)))
    return _call(body, name=name, out_shape=out_shape, operands=[qkv] * 5 + [do, sinks], grid=(bsz, n_blocks + 1),
                 in_specs=_attn_specs(d, n_blocks, True) + [do_spec, SMEM_SPEC], out_specs=out_specs,
                 scratch_shapes=[pltpu.VMEM((WINDOW, d), F32), pltpu.VMEM((WINDOW, LANES), F32), pltpu.VMEM((WINDOW, LANES), F32)],
                 sem=("arbitrary", "arbitrary"), comm=comm)


def _conv_tile(s):
    return min(256, s)


def _halo_specs(ts, width, s):
    per = ts // CONV_HALO
    prev = pl.BlockSpec((None, CONV_HALO, width), lambda b, i: (b, jnp.maximum(i * per - 1, 0), 0))
    nxt = pl.BlockSpec((None, CONV_HALO, width), lambda b, i: (b, jnp.minimum((i + 1) * per, s // CONV_HALO - 1), 0))
    cur = pl.BlockSpec((None, ts, width), lambda b, i: (b, i, 0))
    return prev, cur, nxt


def _glu(u, d):
    return u[:, :d] * jax.nn.sigmoid(u[:, d:])


def _store_shifted(shifted, value):
    rows = value.shape[0]
    shifted[0] = value
    for b in range(1, SUBLANES):
        shifted[b] = pltpu.roll(value, rows - b, axis=0)


def _at(shifted, base, offset, n_rows, lanes):
    return shifted[offset % SUBLANES, pl.ds(base + (offset - offset % SUBLANES), n_rows), lanes]


def _taps(w_ref, shifted, out_ref, ts, d, offset):
    for r0 in range(0, ts, CONV_ROWS):
        for l0 in range(0, d, LANES):
            lanes = pl.ds(l0, LANES)
            acc = jnp.zeros((CONV_ROWS, LANES), F32)
            for j in range(CONV_WIDTH):
                acc = acc + w_ref[pl.ds(j, 1), lanes] * _at(shifted, r0, offset(j), CONV_ROWS, lanes)
            out_ref[pl.ds(r0, CONV_ROWS), lanes] = acc


def _conv_fwd(u, w_dw, b_dw, ln_g, ln_b, *, name, comm=None):
    bsz, s, d2 = u.shape
    d = d2 // 2
    ts = _conv_tile(s)

    def body(up_ref, uc_ref, w_ref, bdw_ref, g_ref, b_ref, z_ref, v_ref, gbuf):
        halo = jnp.where(pl.program_id(1) > 0, _glu(up_ref[...], d), 0.0)
        _store_shifted(gbuf, jnp.concatenate([halo, _glu(uc_ref[...], d)], axis=0))
        _taps(w_ref, gbuf, v_ref, ts, d, lambda j: CONV_HALO - (CONV_WIDTH - 1) + j)
        v = v_ref[...] + bdw_ref[...]
        v_ref[...] = v
        mu = jnp.mean(v, axis=-1, keepdims=True)
        cen = v - mu
        rstd = lax.rsqrt(jnp.mean(cen * cen, axis=-1, keepdims=True) + NORM_EPS)
        ln = cen * rstd * g_ref[...] + b_ref[...]
        z_ref[...] = (ln * jax.nn.sigmoid(ln)).astype(BF16)

    prev, cur, _ = _halo_specs(ts, d2, s)
    one = pl.BlockSpec((1, d), lambda b, i: (0, 0))
    row = pl.BlockSpec((None, ts, d), lambda b, i: (b, i, 0))
    return _call(body, name=name, out_shape=(jax.ShapeDtypeStruct((bsz, s, d), BF16), jax.ShapeDtypeStruct((bsz, s, d), F32)),
                 operands=[u, u, w_dw, b_dw, ln_g, ln_b], grid=(bsz, s // ts),
                 in_specs=[prev, cur, pl.BlockSpec((CONV_HALO, d), lambda b, i: (0, 0)), one, one, one],
                 out_specs=(row, row), scratch_shapes=[pltpu.VMEM((SUBLANES, ts + CONV_HALO, d), F32)],
                 sem=("parallel", "parallel"), comm=comm)


def _conv_bwd_ln(dz, v, ln_g, ln_b, *, name):
    bsz, s, d = v.shape
    ts = _row_tile(s, d * 4, 1 << 20)

    def body(dz_ref, v_ref, g_ref, b_ref, dv_ref, dg_ref, db_ref, dbdw_ref):
        v_v = v_ref[...]
        mu = jnp.mean(v_v, axis=-1, keepdims=True)
        cen = v_v - mu
        rstd = lax.rsqrt(jnp.mean(cen * cen, axis=-1, keepdims=True) + NORM_EPS)
        vhat = cen * rstd
        ln = vhat * g_ref[...] + b_ref[...]
        sig = jax.nn.sigmoid(ln)
        dln = dz_ref[...] * (sig * (1.0 + ln * (1.0 - sig)))
        dvhat = dln * g_ref[...]
        dv = rstd * (dvhat - jnp.mean(dvhat, axis=-1, keepdims=True)
                     - vhat * jnp.mean(dvhat * vhat, axis=-1, keepdims=True))
        dv_ref[...] = dv

        @pl.when((pl.program_id(0) == 0) & (pl.program_id(1) == 0))
        def _():
            dg_ref[...] = jnp.zeros_like(dg_ref)
            db_ref[...] = jnp.zeros_like(db_ref)
            dbdw_ref[...] = jnp.zeros_like(dbdw_ref)

        dg_ref[...] += jnp.sum(dln * vhat, axis=0, keepdims=True)
        db_ref[...] += jnp.sum(dln, axis=0, keepdims=True)
        dbdw_ref[...] += jnp.sum(dv, axis=0, keepdims=True)

    row = pl.BlockSpec((None, ts, d), lambda b, i: (b, i, 0))
    one = pl.BlockSpec((1, d), lambda b, i: (0, 0))
    vec = jax.ShapeDtypeStruct((1, d), F32)
    return pl.pallas_call(body, out_shape=(jax.ShapeDtypeStruct(v.shape, F32), vec, vec, vec), grid=(bsz, s // ts),
                          in_specs=[row, row, one, one], out_specs=(row, one, one, one), name=name,
                          compiler_params=_params("arbitrary", "arbitrary"))(dz, v, ln_g, ln_b)


def _conv_bwd_taps(dv, u, w_dw, *, name, comm=None):
    bsz, s, d = dv.shape
    ts = _conv_tile(s)
    n_tiles = s // ts

    def body(dvc_ref, dvn_ref, uc_ref, w_ref, du_ref, dbu_ref, dw_ref, dvbuf, dglu, glu):
        i = pl.program_id(1)

        @pl.when((pl.program_id(0) == 0) & (i == 0))
        def _():
            dbu_ref[...] = jnp.zeros_like(dbu_ref)
            dw_ref[...] = jnp.zeros_like(dw_ref)

        u_v = uc_ref[...]
        a, sig = u_v[:, :d], jax.nn.sigmoid(u_v[:, d:])
        glu[...] = a * sig
        ahead = jnp.where(i < n_tiles - 1, dvn_ref[...], 0.0)
        _store_shifted(dvbuf, jnp.concatenate([dvc_ref[...], ahead], axis=0))
        _taps(w_ref, dvbuf, dglu, ts, d, lambda j: CONV_WIDTH - 1 - j)
        for l0 in range(0, d, LANES):
            lanes = pl.ds(l0, LANES)
            for j in range(CONV_WIDTH):
                acc = jnp.zeros((SUBLANES, LANES), F32)
                for r0 in range(0, ts, CONV_ROWS):
                    prod = glu[pl.ds(r0, CONV_ROWS), lanes] * _at(dvbuf, r0, CONV_WIDTH - 1 - j, CONV_ROWS, lanes)
                    for k in range(0, CONV_ROWS, SUBLANES):
                        acc = acc + prod[k:k + SUBLANES]
                dw_ref[j, :, lanes] += acc
        dg_v = dglu[...]
        da = dg_v * sig
        dgate = dg_v * a * sig * (1.0 - sig)
        du_ref[:, pl.ds(0, d)] = da.astype(BF16)
        du_ref[:, pl.ds(d, d)] = dgate.astype(BF16)
        dbu_ref[:, pl.ds(0, d)] += jnp.sum(da, axis=0, keepdims=True)
        dbu_ref[:, pl.ds(d, d)] += jnp.sum(dgate, axis=0, keepdims=True)

    _, dv_cur, dv_next = _halo_specs(ts, d, s)
    _, u_cur, _ = _halo_specs(ts, 2 * d, s)
    out_shape = (jax.ShapeDtypeStruct((bsz, s, 2 * d), BF16), jax.ShapeDtypeStruct((1, 2 * d), F32),
                 jax.ShapeDtypeStruct((CONV_HALO, SUBLANES, d), F32))
    out_specs = (pl.BlockSpec((None, ts, 2 * d), lambda b, i: (b, i, 0)), pl.BlockSpec((1, 2 * d), lambda b, i: (0, 0)),
                 pl.BlockSpec((CONV_HALO, SUBLANES, d), lambda b, i: (0, 0, 0)))
    return _call(body, name=name, out_shape=out_shape, operands=[dv, dv, u, w_dw], grid=(bsz, n_tiles),
                 in_specs=[dv_cur, dv_next, u_cur, pl.BlockSpec((CONV_HALO, d), lambda b, i: (0, 0))],
                 out_specs=out_specs,
                 scratch_shapes=[pltpu.VMEM((SUBLANES, ts + CONV_HALO, d), F32), pltpu.VMEM((ts, d), F32),
                                 pltpu.VMEM((ts, d), F32)],
                 sem=("arbitrary", "arbitrary"), comm=comm)


def _mod_fwd(c_all, w_mod, b_mod, *, name):
    n_layers, d, n = w_mod.shape
    rows = c_all.shape[0]

    def body(c_ref, w_ref, b_ref, o_ref):
        cv = c_ref[...]
        cs = (cv * jax.nn.sigmoid(cv)).astype(BF16)
        o_ref[...] = jnp.dot(cs, w_ref[...].astype(BF16), preferred_element_type=F32) + b_ref[...]

    return pl.pallas_call(body, out_shape=jax.ShapeDtypeStruct((n_layers, rows, n), F32), grid=(n_layers,),
                          in_specs=[pl.BlockSpec((rows, d), lambda l: (0, 0)), pl.BlockSpec((None, d, n), lambda l: (l, 0, 0)),
                                    pl.BlockSpec((None, 1, n), lambda l: (l, 0, 0))],
                          out_specs=pl.BlockSpec((None, rows, n), lambda l: (l, 0, 0)), name=name,
                          compiler_params=_params("parallel"))(c_all, w_mod, b_mod)


def _mod_bwd(c_all, dmod, *, name):
    n_layers, rows, n = dmod.shape
    d = c_all.shape[1]

    def body(c_ref, g_ref, o_ref):
        cv = c_ref[...]
        cs = (cv * jax.nn.sigmoid(cv)).astype(BF16)
        o_ref[...] = lax.dot_general(cs, g_ref[...].astype(BF16), (((0,), (0,)), ((), ())), preferred_element_type=F32)

    return pl.pallas_call(body, out_shape=jax.ShapeDtypeStruct((n_layers, d, n), F32), grid=(n_layers,),
                          in_specs=[pl.BlockSpec((rows, d), lambda l: (0, 0)), pl.BlockSpec((None, rows, n), lambda l: (l, 0, 0))],
                          out_specs=pl.BlockSpec((None, d, n), lambda l: (l, 0, 0)), name=name,
                          compiler_params=_params("parallel"))(c_all, dmod)


def _add_half(g, recv, my_c, *, name):
    p, _, r, cdim = g.shape
    tr = _row_tile(r, cdim * 4)

    def body(c_ref, g_ref, r_ref, o_ref):
        o_ref[...] = (g_ref[...] + r_ref[...]).astype(BF16)

    grid_spec = pltpu.PrefetchScalarGridSpec(
        num_scalar_prefetch=1, grid=(p, r // tr),
        in_specs=[pl.BlockSpec((None, None, tr, cdim), lambda q, i, c_ref: (q, c_ref[0], i, 0)),
                  pl.BlockSpec((None, None, tr, cdim), lambda q, i, c_ref: (q, 0, i, 0))],
        out_specs=pl.BlockSpec((None, tr, cdim), lambda q, i, c_ref: (q, i, 0)))
    return pl.pallas_call(body, out_shape=jax.ShapeDtypeStruct((p, r, cdim), BF16), grid_spec=grid_spec, name=name,
                          compiler_params=_params("parallel", "parallel"))(my_c, g, recv)


def _add_pieces(chip, recv, my_qc, stacked, layer, n_layers, *, name):
    _, r, cdim = chip.shape
    tr = _row_tile(r, cdim * 4)

    def body(qc_ref, own_ref, r0_ref, r1_ref, r2_ref, *rest):
        f32 = lambda ref: ref[...].astype(F32)
        rest[-1][...] = ((f32(own_ref) + f32(r0_ref)) + f32(r1_ref)) + f32(r2_ref)

    piece = lambda k: pl.BlockSpec((None, tr, cdim), lambda i, qc_ref: (k, i, 0))
    in_specs = [pl.BlockSpec((None, tr, cdim), lambda i, qc_ref: (qc_ref[0], i, 0)), piece(0), piece(1), piece(2)]
    operands = [my_qc, chip, recv, recv, recv]
    aliases = {}
    if stacked is not None:
        in_specs.append(pl.BlockSpec(memory_space=pl.ANY))
        operands.append(stacked)
        aliases = {len(operands) - 1: 0}
    grid_spec = pltpu.PrefetchScalarGridSpec(
        num_scalar_prefetch=1, grid=(r // tr,), in_specs=in_specs,
        out_specs=pl.BlockSpec((None, None, tr, cdim), lambda i, qc_ref: (layer, qc_ref[1], i, 0)))
    return pl.pallas_call(body, out_shape=jax.ShapeDtypeStruct((n_layers, 2, r, cdim), F32), grid_spec=grid_spec,
                          input_output_aliases=aliases, name=name, compiler_params=_params("parallel"))(*operands)


def _sum_devices(parts, dmod, *, name):
    def body(p_ref, m_ref, o_ref, b_ref):
        acc = p_ref[0]
        for k in range(1, N_DEV):
            acc = acc + p_ref[k]
        o_ref[...] = acc
        tot = None
        for k in range(N_DEV):
            for e in range(dmod.shape[2]):
                tot = m_ref[k, :, e] if tot is None else tot + m_ref[k, :, e]
        b_ref[...] = tot

    out_shape = (jax.ShapeDtypeStruct(parts.shape[1:], F32),
                 jax.ShapeDtypeStruct((dmod.shape[1],) + dmod.shape[3:], F32))
    return pl.pallas_call(body, out_shape=out_shape, in_specs=[VMEM_SPEC, VMEM_SPEC], out_specs=(VMEM_SPEC, VMEM_SPEC),
                          name=name, compiler_params=_params())(parts, dmod)


def _adamw(w, g, m, v, *, name, comm=None):
    r, cdim = w.shape
    tr = _row_tile(r, cdim * 4, 1 << 20)

    def body(w_ref, g_ref, m_ref, v_ref, d_ref, nm_ref, nv_ref):
        gv = g_ref[...]
        nm = ADAM_B1 * m_ref[...] + (1.0 - ADAM_B1) * gv
        nv = ADAM_B2 * v_ref[...] + (1.0 - ADAM_B2) * (gv * gv)
        m_hat = nm / (1.0 - ADAM_B1 ** ADAM_STEP)
        v_hat = nv / (1.0 - ADAM_B2 ** ADAM_STEP)
        d_ref[...] = -ADAM_LR * (m_hat / (jnp.sqrt(v_hat) + ADAM_EPS) + ADAM_WD * w_ref[...])
        nm_ref[...] = nm
        nv_ref[...] = nv

    row = pl.BlockSpec((tr, cdim), lambda i: (i, 0))
    shape = jax.ShapeDtypeStruct((r, cdim), F32)
    return _call(body, name=name, out_shape=(shape, shape, shape), operands=[w, g, m, v], grid=(r // tr,),
                 in_specs=[row] * 4, out_specs=(row, row, row), sem=("parallel",), comm=comm)


def _all_gather_small(block, *, name, comm=None):
    m_per, n = block.shape

    def body(x_ref, out_ref, send_sems, recv_sems, local_sem):
        x, y, c, _ = _position()
        me, sibling = (x, y, c), (x, y, 1 - c)
        chips = [_peer(k, x, y, c)[:2] for k in CHIP_KINDS]

        def rows(px, py, pc):
            return out_ref.at[pl.ds((4 * px + 2 * py + pc) * m_per, m_per), :]

        def copy(k, blk, to, src=None):
            return pltpu.make_async_remote_copy(src_ref=rows(*blk) if src is None else src, dst_ref=rows(*blk),
                                                send_sem=send_sems.at[k], recv_sem=recv_sems.at[k],
                                                device_id=to, device_id_type=MESH)

        mine = pltpu.make_async_copy(x_ref, rows(*me), local_sem)
        mine.start()
        first = [copy(0, me, sibling, src=x_ref)]
        first += [copy(1 + j, me, (*chip, c), src=x_ref) for j, chip in enumerate(chips)]
        for cp in first:
            cp.start()
        passed = [copy(4 + j, (*chip, c), sibling) for j, chip in enumerate(chips)]
        for j, chip in enumerate(chips):
            copy(1 + j, (*chip, c), me).wait_recv()
            passed[j].start()
        copy(0, sibling, me).wait_recv()
        for j, chip in enumerate(chips):
            copy(4 + j, (*chip, 1 - c), me).wait_recv()
        for cp in first + passed:
            cp.wait_send()
        mine.wait()

    return _call(body, name=name, out_shape=jax.ShapeDtypeStruct((N_DEV * m_per, n), block.dtype), operands=[block],
                 in_specs=[VMEM_SPEC], out_specs=VMEM_SPEC,
                 scratch_shapes=[pltpu.SemaphoreType.DMA((7,)), pltpu.SemaphoreType.DMA((7,)), pltpu.SemaphoreType.DMA],
                 comm=comm)


def _pack(arrays, width):
    flat = jnp.concatenate([a.reshape(-1).astype(F32) for a in arrays])
    rows = -(-flat.shape[0] // width)
    rows = -(-rows // SUBLANES) * SUBLANES
    return jnp.pad(flat, (0, rows * width - flat.shape[0])).reshape(rows, width)


def _unpack(packed, shapes):
    flat, out, off = packed.reshape(-1), [], 0
    for shp in shapes:
        size = 1
        for dim in shp:
            size *= dim
        out.append(flat[off:off + size].reshape(shp))
        off += size
    return out


def _adamw_packed(ws, gs, ms, vs, width, *, name):
    shapes = [w.shape for w in ws]
    res = _adamw(_pack(ws, width), _pack(gs, width), _pack(ms, width), _pack(vs, width), name=name)
    return [_unpack(r, shapes) for r in res]


MB = float(1 << 20)


def _nbytes(shape, dtype):
    size = jnp.dtype(dtype).itemsize
    for dim in shape:
        size *= dim
    return size


def _gather_plans(shard, on_ready):
    r, cdim = shard.shape
    half = r // 2
    buf = jax.ShapeDtypeStruct((N_CHIPS, r, cdim), shard.dtype)
    rows = lambda ref, q, h: ref.at[pl.ds(q, 1), pl.ds(h * half, half)]
    there = lambda kind, pos: _peer_shard(kind, pos[0], pos[1])
    cost = 3 * _nbytes((half, cdim), shard.dtype) / MB

    def forward(outs):
        copies = [(lambda ins, outs, pos, kind=kind: rows(ins[0], there(kind, pos), pos[2]),
                   lambda ins, outs, pos, kind=kind: rows(outs[0], there(kind, pos), pos[2]),
                   lambda ins, outs, pos, kind=kind: rows(outs[0], there(kind, pos), 1 - pos[2]), "c") for kind in CHIP_KINDS]
        return _Plan("d2d", cost, [outs[0]], [jax.ShapeDtypeStruct(buf.shape, buf.dtype)], copies, {0: 0},
                     lambda done: on_ready(done[0]))

    copies = [(lambda ins, outs, pos: ins[0].at[:, pl.ds(pos[2] * half, half)],
               lambda ins, outs, pos: rows(outs[0], pos[3], pos[2]),
               lambda ins, outs, pos, kind=kind: rows(outs[0], there(kind, pos), pos[2]), kind) for kind in CHIP_KINDS]
    own = lambda ins, outs, pos: outs[0].at[pl.ds(pos[3], 1)]
    copies.append((lambda ins, outs, pos: ins[0], own, own, "c"))
    return _Plan("ici", cost, [shard[None]], [buf], copies), forward


class _Exchanges:
    def __init__(self):
        self.queue = []

    def add(self, plan, front=False):
        if front:
            self.queue.insert(0, plan)
        else:
            self.queue.append(plan)

    def take(self, budget_mb, at_least_one=False):
        chosen, spent = [p for p in self.queue if p.link == "d2d"], 0.0
        for p in self.queue:
            if p.link == "ici" and (spent + p.cost <= budget_mb or (at_least_one and spent == 0.0)):
                chosen.append(p)
                spent += p.cost
        if not chosen:
            return None
        self.queue = [p for p in self.queue if all(p is not ch for ch in chosen)]
        return _merge(chosen)

    def flush(self, budget_mb, until=lambda: False):
        while self.queue and not until():
            _exchange(self.take(budget_mb, at_least_one=True), name="exchange")


def kernel(x, c, w_mod, b_mod, norm_mix, norm_mlp, w_qkv, b_qkv, w_o, b_o, sinks, w_pw1, b_pw1, w_dw, b_dw, conv_ln_g, conv_ln_b, w_pw2, b_pw2, w_up, w_down, final_norm, loss_target, m_w_mod, m_b_mod, m_norm_mix, m_norm_mlp, m_w_qkv, m_b_qkv, m_w_o, m_b_o, m_sinks, m_w_pw1, m_b_pw1, m_w_dw, m_b_dw, m_conv_ln_g, m_conv_ln_b, m_w_pw2, m_b_pw2, m_w_up, m_w_down, m_final_norm, v_w_mod, v_b_mod, v_norm_mix, v_norm_mlp, v_w_qkv, v_b_qkv, v_w_o, v_b_o, v_sinks, v_w_pw1, v_b_pw1, v_w_dw, v_b_dw, v_conv_ln_g, v_conv_ln_b, v_w_pw2, v_b_pw2, v_w_up, v_w_down, v_final_norm):
    bsz, s, d = x.shape
    t = bsz * s
    depth = w_mod.shape[0]
    n_attn, n_conv = w_qkv.shape[0], w_pw1.shape[0]
    qkv_dim = d + 2 * N_KV_HEADS * HEAD_DIM
    mx, my, mc, mq = _position()
    me = 4 * mx + 2 * my + mc
    my_c = jnp.reshape(mc, (1,)).astype(jnp.int32)
    my_qc = jnp.stack([mq, mc]).astype(jnp.int32)
    pending = _Exchanges()
    SMALL, MEDIUM, LARGE = 2.5, 3.5, 6.5

    weights = {}
    order = []
    for i in range(depth):
        j = i // 2
        order += ([(("qkv", j), w_qkv[j], True), (("o", j), w_o[j], False)] if i % 2 == 0 else
                  [(("pw1", j), w_pw1[j], True), (("pw2", j), w_pw2[j], False)])
        order += [(("up", i), w_up[i], True), (("down", i), w_down[i], False)]
    for key, shard, by_cols in order:
        def ready(buf, key=key, by_cols=by_cols):
            weights[key] = jnp.transpose(buf, (1, 0, 2)).reshape(buf.shape[1], -1) if by_cols else buf.reshape(-1, buf.shape[2])
        ici, forward = _gather_plans(shard.astype(BF16), ready)
        ici.then = lambda outs, forward=forward: pending.add(forward(outs), front=True)
        pending.add(ici)

    def weight(key):
        pending.flush(SMALL, until=lambda: key in weights)
        return weights[key]

    small_sharded = [b_pw1, w_dw, b_dw, conv_ln_g, conv_ln_b, b_pw2]
    c_pad = jnp.pad(c, ((0, SUBLANES - bsz), (0, 0)))
    gathered = _all_gather_small(jnp.concatenate([c_pad, _pack(small_sharded, d)], axis=0), name="gather_c",
                                 comm=pending.take(SMALL, True))
    gathered = gathered.reshape(N_DEV, -1, d)
    c_all = gathered[:, :bsz].reshape(N_DEV * bsz, d)
    per_chip = [_unpack(gathered[2 * q, SUBLANES:], [a.shape for a in small_sharded]) for q in range(N_CHIPS)]
    b_pw1_f, w_dw_f, b_dw_f, ln_g_f, ln_b_f, b_pw2_f = [jnp.concatenate([per_chip[q][k] for q in range(N_CHIPS)], axis=-1)
                                                           for k in range(len(small_sharded))]
    w_dw_f = jnp.pad(w_dw_f, ((0, 0), (0, CONV_HALO - CONV_WIDTH), (0, 0)))

    n_mod = w_mod.shape[2]
    b_mod_cols = lax.dynamic_slice_in_dim(b_mod, mq * n_mod, n_mod, axis=1).reshape(depth, 1, n_mod)
    c_all8 = gathered[:, :SUBLANES].reshape(N_DEV * SUBLANES, d)
    mod_part = _mod_fwd(c_all8, w_mod, b_mod_cols, name="mod_fwd")
    peer_rows = lambda kind, pos: pl.ds(SUBLANES * (2 * _peer_shard(kind, pos[0], pos[1]) + pos[2]), SUBLANES)
    got = {}
    pending.add(_Plan("ici", 0.0, [mod_part], [jax.ShapeDtypeStruct((len(CHIP_KINDS), depth, SUBLANES, n_mod), F32)],
                      [(lambda ins, outs, pos, kind=kind: ins[0].at[:, peer_rows(kind, pos)],
                        lambda ins, outs, pos, k=k: outs[0].at[k], lambda ins, outs, pos, k=k: outs[0].at[k], kind)
                       for k, kind in enumerate(CHIP_KINDS)], then=lambda outs: got.update(rows=outs[0])), front=True)
    _exchange(pending.take(0.0, at_least_one=True), name="mod_exchange")
    own = lax.dynamic_slice_in_dim(mod_part, me * SUBLANES, SUBLANES, axis=1)
    parts = [own, got["rows"][0], got["rows"][1], got["rows"][2]]
    part_of = (0, 2, 1, 3)

    def shard(sidx):
        dist, out = jnp.bitwise_xor(mq, sidx), parts[0]
        for distance in (1, 2, 3):
            out = jnp.where(dist == distance, parts[part_of[distance]], out)
        return out

    mod = jnp.stack([shard(sidx) for sidx in range(N_CHIPS)])[:, :, :bsz]
    mod = jnp.transpose(mod, (1, 2, 0, 3)).reshape(depth, bsz, N_MOD, 1, d)
    mods = [[mod[i][:, k] for k in range(N_MOD)] for i in range(depth)]

    saved = []
    xc = x
    h1 = _normmod(xc, norm_mix[0][None], mods[0][1], mods[0][0], name="normmod")
    for i in range(depth):
        j = i // 2
        sh1, sc1, g1, sh2, sc2, g2 = mods[i]
        mlp_norm = (norm_mlp[i][None], sc2, sh2)
        if i % 2 == 0:
            wq, wo = weight(("qkv", j)), weight(("o", j))
            qkv = _mm(h1.reshape(t, d), wq, bias=b_qkv[j], tn=qkv_dim, name="mm_qkv",
                      comm=pending.take(SMALL, True)).reshape(bsz, s, qkv_dim)
            mix = _attn_fwd(qkv, sinks[j], name="attn_fwd", comm=pending.take(LARGE, True))
            y1, x1, h2 = _mm(mix.reshape(t, d), wo, bias=b_o[j], epi="resid", resid=xc.reshape(t, d), gate=g1, seq=s,
                             norm=mlp_norm, tn=d, name="mm_out", comm=pending.take(SMALL, True))
            extra = (qkv, mix)
        else:
            wp1, wp2 = weight(("pw1", j)), weight(("pw2", j))
            u = _mm(h1.reshape(t, d), wp1, bias=b_pw1_f[j], out_dtype=F32, tn=d, name="mm_pw1",
                    comm=pending.take(MEDIUM, True)).reshape(bsz, s, 2 * d)
            mix, conv_v = _conv_fwd(u, w_dw_f[j], b_dw_f[j][None], ln_g_f[j][None], ln_b_f[j][None], name="conv_fwd",
                                    comm=pending.take(LARGE, True))
            y1, x1, h2 = _mm(mix.reshape(t, d), wp2, bias=b_pw2_f[j], epi="resid", resid=xc.reshape(t, d), gate=g1, seq=s,
                             norm=mlp_norm, tn=d, name="mm_out", comm=pending.take(SMALL, True))
            extra = (u, mix, conv_v)
        act, slope = _mm(h2, weight(("up", i)), epi="relu2", tm=1024, tn=2 * d, name="mm_up", comm=pending.take(LARGE, True))
        mix_norm = (norm_mix[i + 1][None], mods[i + 1][1], mods[i + 1][0]) if i + 1 < depth else None
        y2, x2, *h_next = _mm(act, weight(("down", i)), epi="resid", resid=x1, gate=g2, seq=s, norm=mix_norm,
                              tm=512, tn=d, name="mm_down", comm=pending.take(LARGE, True))
        saved.append((xc, h1, extra, y1, x1.reshape(bsz, s, d), h2, act, y2, slope))
        xc = x2.reshape(bsz, s, d)
        h1 = h_next[0] if h_next else None
    pending.flush(LARGE)

    last_y2, last_g2 = saved[-1][7].reshape(bsz, s, d), mods[-1][5]
    dx, loss_cols, d_final, dyb, dg2, _ = _loss_head(xc, loss_target, final_norm[None], (last_y2, last_g2), name="loss_head")
    loss = lax.psum(0.5 / d * jnp.sum(loss_cols), ("x", "y", "c"))

    totals = {}

    def reduce_scatter(name, layer, n_layers, grad):
        view = grad.reshape(N_CHIPS, 2, grad.shape[1] // 2, grad.shape[2])
        half_shape = (N_CHIPS, 1) + view.shape[2:]

        def scatter(outs):
            chip = _add_half(view, outs[0], my_c, name="rs_chipsum")
            copies = [(lambda ins, outs, pos, kind=kind: ins[0].at[pl.ds(_peer_shard(kind, pos[0], pos[1]), 1)],
                       lambda ins, outs, pos, k=k: outs[0].at[pl.ds(k, 1)],
                       lambda ins, outs, pos, k=k: outs[0].at[pl.ds(k, 1)], kind) for k, kind in enumerate(CHIP_KINDS)]

            def total(outs):
                totals[name] = _add_pieces(chip, outs[0], my_qc, totals.get(name), layer, n_layers, name="rs_total")

            pending.add(_Plan("ici", 3 * _nbytes(chip.shape[1:], BF16) / MB, [chip],
                              [jax.ShapeDtypeStruct((3,) + chip.shape[1:], BF16)], copies, then=total))

        pending.add(_Plan("d2d", _nbytes(half_shape, F32) / MB, [view], [jax.ShapeDtypeStruct(half_shape, F32)],
                          [(lambda ins, outs, pos: ins[0].at[:, pl.ds(1 - pos[2], 1)],
                            lambda ins, outs, pos: outs[0], lambda ins, outs, pos: outs[0], "c")], then=scatter))

    dmods, small = [None] * depth, {}
    for i in reversed(range(depth)):
        j = i // 2
        xin, h1, extra, y1, x1, h2, act, _, slope = saved[i]
        sh1, sc1, g1, sh2, sc2, g2 = mods[i]
        dyb = dyb.reshape(t, d)
        reduce_scatter("down", i, depth,
                       _mm_tn(act, dyb, name="dw_down", comm=pending.take(LARGE)).reshape(N_CHIPS, -1, d))
        dup = _mm(dyb, weights["down", i], nt=True, epi="dact", act=slope, tm=1024, tn=2 * d, name="mm_dact",
                  comm=pending.take(LARGE))
        reduce_scatter("up", i, depth,
                       _mm_tn(h2.reshape(t, d), dup, col_shards=N_CHIPS, name="dw_up", comm=pending.take(LARGE)))
        dh2 = _mm(dup, weights["up", i], nt=True, tm=512, tn=d, name="mm_dh2",
                  comm=pending.take(LARGE)).reshape(bsz, s, d)
        dx1, p2, dsh2, dyb, dg1, sdx = _normmod_bwd(x1, dh2, dx, norm_mlp[i][None], sc2, (y1.reshape(bsz, s, d), g1),
                                                    name="normmod_bwd")
        dyb = dyb.reshape(t, d)
        d_bias_out = jnp.sum(g1 * sdx, axis=(0, 1))
        if i % 2 == 0:
            qkv, mix = extra
            small["b_o", j] = d_bias_out
            reduce_scatter("o", j, n_attn,
                           _mm_tn(mix.reshape(t, d), dyb, name="dw_sq", comm=pending.take(SMALL)).reshape(N_CHIPS, -1, d))
            dmix = _mm(dyb, weights["o", j], nt=True, tn=d, name="mm_dmix", comm=pending.take(SMALL)).reshape(bsz, s, d)
            dqkv, d_bqkv, d_sink = _attn_bwd(qkv, dmix, sinks[j], name="attn_bwd", comm=pending.take(2 * LARGE))
            small["b_qkv", j], small["sinks", j] = d_bqkv[0], d_sink[:, 0]
            dqkv = dqkv.reshape(t, qkv_dim)
            dwq = _mm_tn(h1.reshape(t, d), dqkv, tn=qkv_dim, name="dw_qkv", comm=pending.take(SMALL))
            reduce_scatter("qkv", j, n_attn, jnp.transpose(dwq.reshape(d, N_CHIPS, -1), (1, 0, 2)))
            dh1 = _mm(dqkv, weights["qkv", j], nt=True, tn=d, name="mm_dh1a", comm=pending.take(SMALL))
        else:
            u, mix, conv_v = extra
            small["b_pw2", j] = d_bias_out
            reduce_scatter("pw2", j, n_conv,
                           _mm_tn(mix.reshape(t, d), dyb, name="dw_sq", comm=pending.take(SMALL)).reshape(N_CHIPS, -1, d))
            dz = _mm(dyb, weights["pw2", j], nt=True, out_dtype=F32, tn=d, name="mm_dz", comm=pending.take(SMALL)).reshape(bsz, s, d)
            dv, d_lng, d_lnb, d_bdw = _conv_bwd_ln(dz, conv_v, ln_g_f[j][None], ln_b_f[j][None], name="conv_bwd_ln")
            du, d_bpw1, d_wdw = _conv_bwd_taps(dv, u, w_dw_f[j], name="conv_bwd_taps", comm=pending.take(2 * LARGE))
            small["ln_g", j], small["ln_b", j], small["b_dw", j] = d_lng[0], d_lnb[0], d_bdw[0]
            small["b_pw1", j], small["w_dw", j] = d_bpw1[0], jnp.sum(d_wdw[:CONV_WIDTH], axis=1)
            du = du.reshape(t, 2 * d)
            reduce_scatter("pw1", j, n_conv,
                           _mm_tn(h1.reshape(t, d), du, col_shards=N_CHIPS, name="dw_pw1", comm=pending.take(MEDIUM)))
            dh1 = _mm(du, weights["pw1", j], nt=True, tn=d, name="mm_dh1c", comm=pending.take(MEDIUM))
        below = (saved[i - 1][7].reshape(bsz, s, d), mods[i - 1][5]) if i > 0 else None
        dx, p1, dsh1, *gate_grads = _normmod_bwd(xin, dh1.reshape(bsz, s, d), dx1, norm_mix[i][None], sc1, below,
                                                 name="normmod_bwd")
        small["norm_mix", i] = jnp.sum((1.0 + sc1) * p1, axis=(0, 1))
        small["norm_mlp", i] = jnp.sum((1.0 + sc2) * p2, axis=(0, 1))
        dmods[i] = jnp.concatenate([dsh1, norm_mix[i] * p1, dg1, dsh2, norm_mlp[i] * p2, dg2], axis=1)
        if i > 0:
            dyb, dg2, _ = gate_grads
    grad_x = dx

    small_names = ([("norm_mix", i) for i in range(depth)] + [("norm_mlp", i) for i in range(depth)]
                   + [(nm, j) for nm in ("b_qkv", "b_o", "sinks") for j in range(n_attn)]
                   + [(nm, j) for nm in ("b_pw1", "w_dw", "b_dw", "ln_g", "ln_b", "b_pw2") for j in range(n_conv)])
    small_list = [small[k] for k in small_names] + [d_final[0]]
    small_pack = _pack(small_list, d)
    dmod_rows = jnp.stack(dmods).reshape(depth * bsz * N_MOD, d)
    n_small = small_pack.shape[0]
    gathered = _all_gather_small(jnp.concatenate([small_pack, _pack([dmod_rows], d)], axis=0), name="gather_small",
                                 comm=pending.take(LARGE, True))
    gathered = gathered.reshape(N_DEV, -1, d)
    dmod_all = gathered[:, n_small:n_small + depth * bsz * N_MOD].reshape(N_DEV, depth, bsz, N_MOD, d)
    small_sum, g_b_mod = _sum_devices(gathered[:, :n_small], dmod_all, name="sum_devices")
    small_tot = dict(zip(small_names + ["final_norm"], _unpack(small_sum, [a.shape for a in small_list])))
    stacked = lambda nm, count: jnp.stack([small_tot[nm, k] for k in range(count)])
    g_norm_mix, g_norm_mlp = stacked("norm_mix", depth), stacked("norm_mlp", depth)
    g_b_qkv, g_b_o, g_sinks = stacked("b_qkv", n_attn), stacked("b_o", n_attn), stacked("sinks", n_attn)
    g_final = small_tot["final_norm"]
    g_b_mod = g_b_mod.reshape(depth, N_MOD * d)
    shard_cols = lambda g: lax.dynamic_slice_in_dim(g, mq * (g.shape[-1] // N_CHIPS), g.shape[-1] // N_CHIPS, axis=g.ndim - 1)
    g_b_pw1, g_w_dw, g_b_dw, g_ln_g, g_ln_b, g_b_pw2 = [shard_cols(stacked(nm, n_conv))
                                                        for nm in ("b_pw1", "w_dw", "b_dw", "ln_g", "ln_b", "b_pw2")]

    dmod_cols = jnp.transpose(dmod_all, (1, 0, 2, 3, 4)).reshape(depth, N_DEV * bsz, N_MOD * d)
    dmod_cols = lax.dynamic_slice_in_dim(dmod_cols, mq * n_mod, n_mod, axis=2)
    g_w_mod = _mod_bwd(c_all, dmod_cols, name="mod_bwd")

    def adam(w, g, m, v, name):
        two_d = lambda a: a.reshape(-1, a.shape[-1])
        return [r.reshape(w.shape) for r in _adamw(two_d(w), two_d(g), two_d(m), two_d(v), name=name)]

    results = {"w_mod": (g_w_mod,) + tuple(adam(w_mod, g_w_mod, m_w_mod, v_w_mod, "adamw"))}

    pending.flush(LARGE)
    names = ["qkv", "o", "pw1", "pw2", "up", "down"]
    bufs = [totals[nm] for nm in names]
    copies = []
    for a, buf in enumerate(bufs):
        for layer in range(buf.shape[0]):
            half = lambda ref, h, layer=layer: ref.at[pl.ds(layer, 1), pl.ds(h, 1)]
            copies.append((lambda ins, outs, pos, a=a, half=half: half(ins[a], pos[2]),
                           lambda ins, outs, pos, a=a, half=half: half(outs[a], pos[2]),
                           lambda ins, outs, pos, a=a, half=half: half(outs[a], 1 - pos[2]), "c"))
    shared = {}
    _exchange(_Plan("d2d", 0.0, bufs, [jax.ShapeDtypeStruct(b.shape, F32) for b in bufs], copies,
                    {a: a for a in range(len(bufs))}, lambda outs: shared.update(zip(names, outs))), name="rs_share")
    g_w_qkv, g_w_o, g_w_pw1, g_w_pw2, g_w_up, g_w_down = [
        shared[nm].reshape(shared[nm].shape[0], -1, shared[nm].shape[3]) for nm in names]

    for nm, w, g, m, v in (("w_qkv", w_qkv, g_w_qkv, m_w_qkv, v_w_qkv),
                           ("w_o", w_o, g_w_o, m_w_o, v_w_o), ("w_pw1", w_pw1, g_w_pw1, m_w_pw1, v_w_pw1),
                           ("w_pw2", w_pw2, g_w_pw2, m_w_pw2, v_w_pw2), ("w_up", w_up, g_w_up, m_w_up, v_w_up),
                           ("w_down", w_down, g_w_down, m_w_down, v_w_down)):
        results[nm] = (g,) + tuple(adam(w, g, m, v, "adamw"))
    small_w = dict(b_mod=(b_mod, g_b_mod, m_b_mod, v_b_mod), norm_mix=(norm_mix, g_norm_mix, m_norm_mix, v_norm_mix),
                   norm_mlp=(norm_mlp, g_norm_mlp, m_norm_mlp, v_norm_mlp), b_qkv=(b_qkv, g_b_qkv, m_b_qkv, v_b_qkv),
                   b_o=(b_o, g_b_o, m_b_o, v_b_o), sinks=(sinks, g_sinks, m_sinks, v_sinks),
                   b_pw1=(b_pw1, g_b_pw1, m_b_pw1, v_b_pw1), w_dw=(w_dw, g_w_dw, m_w_dw, v_w_dw),
                   b_dw=(b_dw, g_b_dw, m_b_dw, v_b_dw), conv_ln_g=(conv_ln_g, g_ln_g, m_conv_ln_g, v_conv_ln_g),
                   conv_ln_b=(conv_ln_b, g_ln_b, m_conv_ln_b, v_conv_ln_b), b_pw2=(b_pw2, g_b_pw2, m_b_pw2, v_b_pw2),
                   final_norm=(final_norm, g_final, m_final_norm, v_final_norm))
    names = list(small_w)
    deltas, new_ms, new_vs = _adamw_packed(*[[small_w[nm][k] for nm in names] for k in range(4)], d, name="adamw_small")
    for k, nm in enumerate(names):
        results[nm] = (small_w[nm][1], deltas[k], new_ms[k], new_vs[k])

    weight_order = ["w_mod", "b_mod", "norm_mix", "norm_mlp", "w_qkv", "b_qkv", "w_o", "b_o", "sinks", "w_pw1", "b_pw1",
                    "w_dw", "b_dw", "conv_ln_g", "conv_ln_b", "w_pw2", "b_pw2", "w_up", "w_down", "final_norm"]
    return (loss, grad_x, *[results[nm][0] for nm in weight_order], *[results[nm][1] for nm in weight_order],
            *[results[nm][2] for nm in weight_order], *[results[nm][3] for nm in weight_order])
```

```python
import functools

import jax
import jax.numpy as jnp
from jax import lax
from jax.experimental import pallas as pl
from jax.experimental.pallas import tpu as pltpu

F32, BF16 = jnp.float32, jnp.bfloat16
MESH = pl.DeviceIdType.MESH
N_CHIPS = 4
N_DEV = 8
LANES = 128
SUBLANES = 8
VMEM_LIMIT = 48 * 1024 * 1024

NORM_EPS = 1e-6
HEAD_DIM = 64
N_KV_HEADS = 2
WINDOW = 128
CONV_WIDTH = 31
CONV_HALO = 32
CONV_ROWS = 32
N_MOD = 6

ADAM_LR, ADAM_B1, ADAM_B2, ADAM_EPS, ADAM_WD, ADAM_STEP = 0.001, 0.9, 0.999, 1e-08, 0.01, 10

HBM_SPEC = pl.BlockSpec(memory_space=pltpu.HBM)
VMEM_SPEC = pl.BlockSpec(memory_space=pltpu.VMEM)
SMEM_SPEC = pl.BlockSpec(memory_space=pltpu.SMEM)


def _params(*sem):
    return pltpu.CompilerParams(dimension_semantics=sem or None, vmem_limit_bytes=VMEM_LIMIT)


def _row_tile(rows, width_bytes, target=1 << 20):
    t = rows
    while t % 2 == 0 and t > SUBLANES and t * width_bytes > target:
        t //= 2
    return t


CHIP_KINDS = ("x", "y", "xy")


def _position():
    x, y, c = lax.axis_index("x"), lax.axis_index("y"), lax.axis_index("c")
    return x, y, c, 2 * x + y


def _peer(kind, x, y, c):
    return {"c": (x, y, 1 - c), "x": (1 - x, y, c), "y": (x, 1 - y, c), "xy": (1 - x, 1 - y, c)}[kind]


def _peer_shard(kind, x, y):
    px, py, _ = _peer(kind, x, y, 0)
    return 2 * px + py


class _Plan:
    def __init__(self, link, cost, operands, out_shapes, copies, aliases=None, then=None):
        self.link, self.cost = link, cost
        self.operands, self.out_shapes, self.copies = list(operands), list(out_shapes), list(copies)
        self.aliases, self.then = dict(aliases or {}), then


def _merge(plans):
    operands, out_shapes, copies, aliases, thens = [], [], [], {}, []
    for p in plans:
        i0, o0 = len(operands), len(out_shapes)
        i1, o1 = i0 + len(p.operands), o0 + len(p.out_shapes)

        def shifted(f, i0=i0, i1=i1, o0=o0, o1=o1):
            return lambda ins, outs, pos: f(ins[i0:i1], outs[o0:o1], pos)

        copies += [(shifted(src), shifted(dst), shifted(land), kind) for src, dst, land, kind in p.copies]
        aliases.update({i0 + k: o0 + v for k, v in p.aliases.items()})
        operands += p.operands
        out_shapes += p.out_shapes
        thens.append((p.then, o0, o1))

    def then(outs):
        for f, o0, o1 in thens:
            if f is not None:
                f(outs[o0:o1])

    return _Plan("mixed", sum(p.cost for p in plans), operands, out_shapes, copies, aliases, then)


def _call(body, *, name, out_shape, operands, grid=(), in_specs=(), out_specs=(), scratch_shapes=(), sem=(), comm=None):
    single = not isinstance(out_shape, (tuple, list))
    out_shape = [out_shape] if single else list(out_shape)
    out_specs = [out_specs] if single else list(out_specs)
    if comm is None:
        res = pl.pallas_call(body, out_shape=out_shape, grid=grid, in_specs=list(in_specs), out_specs=out_specs,
                             scratch_shapes=list(scratch_shapes), name=name, compiler_params=_params(*sem))(*operands)
        return res[0] if single else res
    n_in, n_out, n_scr = len(operands), len(out_shape), len(scratch_shapes)
    c_in, c_out, n_cp = len(comm.operands), len(comm.out_shapes), len(comm.copies)

    def wrapped(*refs):
        ins, refs = refs[:n_in], refs[n_in:]
        cins, refs = refs[:c_in], refs[c_in:]
        outs, refs = refs[:n_out], refs[n_out:]
        couts, refs = refs[:c_out], refs[c_out:]
        scr, (send_sems, recv_sems) = refs[:n_scr], refs[n_scr:]

        def descriptors():
            pos = _position()
            sends, lands = [], []
            for k, (src, dst, landing, kind) in enumerate(comm.copies):
                common = dict(send_sem=send_sems.at[k], recv_sem=recv_sems.at[k],
                              device_id=_peer(kind, *pos[:3]), device_id_type=MESH)
                sends.append(pltpu.make_async_remote_copy(src_ref=src(cins, couts, pos), dst_ref=dst(cins, couts, pos), **common))
                lands.append(pltpu.make_async_remote_copy(src_ref=src(cins, couts, pos), dst_ref=landing(cins, couts, pos), **common))
            return sends, lands

        def start():
            for cp in descriptors()[0]:
                cp.start()

        def finish():
            sends, lands = descriptors()
            for cp in lands:
                cp.wait_recv()
            for cp in sends:
                cp.wait_send()

        if not grid:
            start()
            body(*ins, *outs, *scr)
            finish()
        else:
            ids = [pl.program_id(ax) for ax in range(len(grid))]
            first, last = ids[0] == 0, ids[0] == grid[0] - 1
            for ax in range(1, len(grid)):
                first, last = first & (ids[ax] == 0), last & (ids[ax] == grid[ax] - 1)
            pl.when(first)(start)
            body(*ins, *outs, *scr)
            pl.when(last)(finish)

    res = pl.pallas_call(wrapped, out_shape=out_shape + comm.out_shapes, grid=grid,
                         in_specs=list(in_specs) + [HBM_SPEC] * c_in, out_specs=out_specs + [HBM_SPEC] * c_out,
                         scratch_shapes=list(scratch_shapes) + [pltpu.SemaphoreType.DMA((n_cp,)), pltpu.SemaphoreType.DMA((n_cp,))],
                         input_output_aliases={n_in + k: n_out + v for k, v in comm.aliases.items()},
                         name=name, compiler_params=_params(*["arbitrary"] * len(grid)))(*operands, *comm.operands)
    if comm.then is not None:
        comm.then(res[n_out:])
    return res[0] if single else res[:n_out]


def _exchange(plan, *, name):
    _call(lambda: None, name=name, out_shape=[], operands=[], comm=plan)


def _mm(a, b, *, name, nt=False, tm=1024, tn=512, epi="plain", out_dtype=BF16,
        bias=None, act=None, resid=None, gate=None, seq=None, norm=None, comm=None):
    m, k = a.shape
    n = b.shape[0] if nt else b.shape[1]
    tm, tn = min(tm, m, seq or m), min(tn, n)
    assert m % tm == 0 and n % tn == 0
    dims = (((1,), (1,)), ((), ())) if nt else (((1,), (0,)), ((), ()))
    tile = pl.BlockSpec((tm, tn), lambda j, i: (i, j))
    operands = [a, b]
    in_specs = [pl.BlockSpec((tm, k), lambda j, i: (i, 0)),
                pl.BlockSpec((tn, k), lambda j, i: (j, 0)) if nt else pl.BlockSpec((k, tn), lambda j, i: (0, j))]
    if bias is not None:
        operands.append(bias.reshape(1, n))
        in_specs.append(pl.BlockSpec((1, tn), lambda j, i: (0, j)))
    if epi == "dact":
        operands.append(act)
        in_specs.append(tile)
    if epi == "resid":
        assert seq % tm == 0
        per_ex = pl.BlockSpec((None, 1, tn), lambda j, i: (i * tm // seq, 0, j))
        operands += [resid, gate]
        in_specs += [tile, per_ex]
        out_shape = (jax.ShapeDtypeStruct((m, n), BF16), jax.ShapeDtypeStruct((m, n), F32))
        out_specs = (tile, tile)
        if norm is not None:
            assert tn == n
            operands += list(norm)
            in_specs += [pl.BlockSpec((1, tn), lambda j, i: (0, j)), per_ex, per_ex]
            out_shape += (jax.ShapeDtypeStruct((m, n), BF16),)
            out_specs += (tile,)
    elif epi == "relu2":
        out_shape = (jax.ShapeDtypeStruct((m, n), BF16), jax.ShapeDtypeStruct((m, n), BF16))
        out_specs = (tile, tile)
    else:
        out_shape = jax.ShapeDtypeStruct((m, n), out_dtype)
        out_specs = tile

    def body(*refs):
        it = iter(refs)
        a_ref, b_ref = next(it), next(it)
        acc = lax.dot_general(a_ref[...], b_ref[...], dims, preferred_element_type=F32)
        if bias is not None:
            acc = acc + next(it)[...]
        if epi == "plain":
            next(it)[...] = acc.astype(out_dtype)
        elif epi == "relu2":
            r = jnp.maximum(acc, 0.0)
            next(it)[...] = (r * r).astype(BF16)
            next(it)[...] = (2.0 * r).astype(BF16)
        elif epi == "dact":
            slope_ref = next(it)
            next(it)[...] = (acc * slope_ref[...].astype(F32)).astype(out_dtype)
        else:
            resid_ref, gate_ref = next(it), next(it)
            norm_refs = [next(it) for _ in (norm or ())]
            y_ref, x_ref = next(it), next(it)
            y_ref[...] = acc.astype(BF16)
            x_new = resid_ref[...] + gate_ref[...] * acc
            x_ref[...] = x_new
            if norm is not None:
                g_ref, sc_ref, sh_ref = norm_refs
                r = lax.rsqrt(jnp.mean(x_new * x_new, axis=-1, keepdims=True) + NORM_EPS)
                next(it)[...] = (x_new * r * g_ref[...] * (1.0 + sc_ref[...]) + sh_ref[...]).astype(BF16)

    return _call(body, name=name, out_shape=out_shape, operands=operands, grid=(n // tn, m // tm), in_specs=in_specs,
                 out_specs=out_specs, sem=("parallel", "parallel"), comm=comm)


def _mm_tn(a, b, *, name, tm=1024, tn=1024, tk=2048, col_shards=None, comm=None):
    t, m = a.shape
    n = b.shape[1]
    tm, tk = min(tm, m), min(tk, t)
    if col_shards is None:
        tn = min(tn, n)
        out_shape = jax.ShapeDtypeStruct((m, n), F32)
        out_spec = pl.BlockSpec((tm, tn), lambda i, j, k: (i, j))
    else:
        per = n // col_shards
        tn = min(tn, per)
        assert per % tn == 0
        out_shape = jax.ShapeDtypeStruct((col_shards, m, per), F32)
        out_spec = pl.BlockSpec((None, tm, tn), lambda i, j, k: (j // (per // tn), i, j % (per // tn)))
    assert m % tm == 0 and n % tn == 0 and t % tk == 0

    def body(a_ref, b_ref, o_ref):
        @pl.when(pl.program_id(2) == 0)
        def _():
            o_ref[...] = jnp.zeros_like(o_ref)

        o_ref[...] += lax.dot_general(a_ref[...], b_ref[...], (((0,), (0,)), ((), ())),
                                      preferred_element_type=F32)

    return _call(body, name=name, out_shape=out_shape, operands=[a, b], grid=(m // tm, n // tn, t // tk),
                 in_specs=[pl.BlockSpec((tk, tm), lambda i, j, k: (k, i)), pl.BlockSpec((tk, tn), lambda i, j, k: (k, j))],
                 out_specs=out_spec, sem=("parallel", "parallel", "arbitrary"), comm=comm)


def _normmod(x, gamma, sc, sh, *, name):
    bsz, s, d = x.shape
    ts = _row_tile(s, d * 4, 2 << 20)

    def body(x_ref, g_ref, sc_ref, sh_ref, o_ref):
        xf = x_ref[...]
        r = lax.rsqrt(jnp.mean(xf * xf, axis=-1, keepdims=True) + NORM_EPS)
        o_ref[...] = (xf * r * g_ref[...] * (1.0 + sc_ref[...]) + sh_ref[...]).astype(BF16)

    row = pl.BlockSpec((None, ts, d), lambda b, i: (b, i, 0))
    per_ex = pl.BlockSpec((None, 1, d), lambda b, i: (b, 0, 0))
    return pl.pallas_call(body, out_shape=jax.ShapeDtypeStruct(x.shape, BF16), grid=(bsz, s // ts),
                          in_specs=[row, pl.BlockSpec((1, d), lambda b, i: (0, 0)), per_ex, per_ex],
                          out_specs=row, name=name, compiler_params=_params("parallel", "parallel"))(x, gamma, sc, sh)


def _gate_grads(dxv, y_ref, gate_ref, dy_ref, dg_ref, sdx_ref):
    dy_ref[...] = (gate_ref[...] * dxv).astype(BF16)

    @pl.when(pl.program_id(1) == 0)
    def _():
        dg_ref[...] = jnp.zeros_like(dg_ref)
        sdx_ref[...] = jnp.zeros_like(sdx_ref)

    dg_ref[...] += jnp.sum(dxv * y_ref[...].astype(F32), axis=0, keepdims=True)
    sdx_ref[...] += jnp.sum(dxv, axis=0, keepdims=True)


def _normmod_bwd(x, dh, dres, gamma, sc, producer=None, *, name):
    bsz, s, d = x.shape
    ts = _row_tile(s, d * 4, 2 << 20)

    def body(x_ref, dh_ref, dres_ref, g_ref, sc_ref, *rest):
        dx_ref, p_ref, dsh_ref = rest[-6:-3] if producer else rest
        xf = x_ref[...]
        r = lax.rsqrt(jnp.mean(xf * xf, axis=-1, keepdims=True) + NORM_EPS)
        xhat = xf * r
        dh_v = dh_ref[...].astype(F32)
        dxhat = dh_v * (g_ref[...] * (1.0 + sc_ref[...]))
        dxv = dres_ref[...] + r * (dxhat - xhat * jnp.mean(dxhat * xhat, axis=-1, keepdims=True))
        dx_ref[...] = dxv

        @pl.when(pl.program_id(1) == 0)
        def _():
            p_ref[...] = jnp.zeros_like(p_ref)
            dsh_ref[...] = jnp.zeros_like(dsh_ref)

        p_ref[...] += jnp.sum(dh_v * xhat, axis=0, keepdims=True)
        dsh_ref[...] += jnp.sum(dh_v, axis=0, keepdims=True)
        if producer:
            _gate_grads(dxv, rest[0], rest[1], *rest[-3:])

    row = pl.BlockSpec((None, ts, d), lambda b, i: (b, i, 0))
    per_ex = pl.BlockSpec((None, 1, d), lambda b, i: (b, 0, 0))
    vec = jax.ShapeDtypeStruct((bsz, 1, d), F32)
    gate_in, gate_out, gate_specs = (), (), ()
    if producer:
        gate_in, gate_out, gate_specs = (row, per_ex), (jax.ShapeDtypeStruct(x.shape, BF16), vec, vec), (row, per_ex, per_ex)
    return pl.pallas_call(body, out_shape=(jax.ShapeDtypeStruct(x.shape, F32), vec, vec) + gate_out, grid=(bsz, s // ts),
                          in_specs=[row, row, row, pl.BlockSpec((1, d), lambda b, i: (0, 0)), per_ex, *gate_in],
                          out_specs=(row, per_ex, per_ex) + gate_specs, name=name,
                          compiler_params=_params("parallel", "arbitrary"))(x, dh, dres, gamma, sc, *(producer or ()))


def _loss_head(x, target, gamma, producer, *, name):
    bsz, s, d = x.shape
    ts = _row_tile(s, d * 4, 2 << 20)

    def body(x_ref, t_ref, g_ref, y_ref, gate_ref, dx_ref, loss_ref, dg_ref, *gate_refs):
        xf = x_ref[...]
        r = lax.rsqrt(jnp.mean(xf * xf, axis=-1, keepdims=True) + NORM_EPS)
        xhat = xf * r
        err = xhat * g_ref[...] - t_ref[...]
        dy = err * (1.0 / d)
        dxhat = dy * g_ref[...]
        dxv = r * (dxhat - xhat * jnp.mean(dxhat * xhat, axis=-1, keepdims=True))
        dx_ref[...] = dxv

        @pl.when((pl.program_id(0) == 0) & (pl.program_id(1) == 0))
        def _():
            loss_ref[...] = jnp.zeros_like(loss_ref)
            dg_ref[...] = jnp.zeros_like(dg_ref)

        loss_ref[...] += jnp.sum(err * err, axis=0, keepdims=True)
        dg_ref[...] += jnp.sum(dy * xhat, axis=0, keepdims=True)
        _gate_grads(dxv, y_ref, gate_ref, *gate_refs)

    row = pl.BlockSpec((None, ts, d), lambda b, i: (b, i, 0))
    one = pl.BlockSpec((1, d), lambda b, i: (0, 0))
    per_ex = pl.BlockSpec((None, 1, d), lambda b, i: (b, 0, 0))
    vec, ex_vec = jax.ShapeDtypeStruct((1, d), F32), jax.ShapeDtypeStruct((bsz, 1, d), F32)
    return pl.pallas_call(body, out_shape=(jax.ShapeDtypeStruct(x.shape, F32), vec, vec,
                                           jax.ShapeDtypeStruct(x.shape, BF16), ex_vec, ex_vec), grid=(bsz, s // ts),
                          in_specs=[row, row, one, row, per_ex], out_specs=(row, one, one, row, per_ex, per_ex), name=name,
                          compiler_params=_params("arbitrary", "arbitrary"))(x, target, gamma, *producer)


def _alibi_slope(h, n_heads):
    return 2.0 ** (-8.0 * (h + 1) / n_heads)


def _attn_masks(first_block):
    qi = lax.broadcasted_iota(jnp.int32, (WINDOW, 2 * WINDOW), 0)
    ki = lax.broadcasted_iota(jnp.int32, (WINDOW, 2 * WINDOW), 1)
    dist = qi + WINDOW - ki
    first_key = jnp.where(first_block, WINDOW, 0)
    valid = (dist >= 0) & (dist < WINDOW) & (ki >= first_key)
    return dist.astype(F32), valid


def _dup_halves(span, kv, left):
    f = span.astype(F32)
    rolled = pltpu.roll(f, HEAD_DIM, axis=1)
    out = jnp.where(left, f, rolled) if kv == 0 else jnp.where(left, rolled, f)
    return out.astype(BF16)


def _attn_probs(s, h, n_heads, distf, valid, sink):
    s = s * (HEAD_DIM ** -0.5) - _alibi_slope(h, n_heads) * distf
    s = jnp.where(valid, s, -1e30)
    m = jnp.maximum(jnp.max(s, axis=-1, keepdims=True), sink)
    e = jnp.exp(s - m)
    e_sink = jnp.exp(sink - m)
    inv = 1.0 / (jnp.sum(e, axis=-1, keepdims=True) + e_sink)
    return e * inv, e_sink * inv


def _attn_specs(d, n_blocks, clamp):
    kcol = d // LANES
    cur = (lambda i: jnp.minimum(i, n_blocks - 1)) if clamp else (lambda i: i)
    prev = lambda i: jnp.maximum(cur(i) - 1, 0)
    kv = lambda col, blk: pl.BlockSpec((None, WINDOW, LANES), lambda b, i: (b, blk(i), col))
    return [pl.BlockSpec((None, WINDOW, d), lambda b, i: (b, cur(i), 0)),
            kv(kcol, prev), kv(kcol, cur), kv(kcol + 1, prev), kv(kcol + 1, cur)]


def _attn_fwd(qkv, sinks, *, name, comm=None):
    bsz, s, qkv_dim = qkv.shape
    d = qkv_dim - 2 * N_KV_HEADS * HEAD_DIM
    n_heads = d // HEAD_DIM
    group = n_heads // N_KV_HEADS
    pairs = group // 2
    n_blocks = s // WINDOW

    def body(q_ref, kp_ref, kc_ref, vp_ref, vc_ref, sink_ref, o_ref):
        left = lax.broadcasted_iota(jnp.int32, (1, LANES), 1) < HEAD_DIM
        distf, valid = _attn_masks(pl.program_id(1) == 0)
        kspan = jnp.concatenate([kp_ref[...], kc_ref[...]], axis=0)
        vspan = jnp.concatenate([vp_ref[...], vc_ref[...]], axis=0)
        for kv in range(N_KV_HEADS):
            kdup, vdup = _dup_halves(kspan, kv, left), _dup_halves(vspan, kv, left)
            res = []
            for par in range(2):
                keep = left if par == 0 else jnp.logical_not(left)
                cols = [pl.ds((kv * pairs + p) * LANES, LANES) for p in range(pairs)]
                lhs = jnp.concatenate([jnp.where(keep, q_ref[:, cl], jnp.zeros((), BF16)) for cl in cols], axis=0)
                sc = lax.dot_general(lhs, kdup, (((1,), (1,)), ((), ())), preferred_element_type=F32)
                probs = []
                for p in range(pairs):
                    h = kv * group + 2 * p + par
                    pr, _ = _attn_probs(sc[p * WINDOW:(p + 1) * WINDOW], h, n_heads, distf, valid, sink_ref[h])
                    probs.append(pr.astype(BF16))
                res.append(jnp.dot(jnp.concatenate(probs, axis=0), vdup, preferred_element_type=F32))
            for p in range(pairs):
                rows = slice(p * WINDOW, (p + 1) * WINDOW)
                o_ref[:, pl.ds((kv * pairs + p) * LANES, LANES)] = jnp.where(left, res[0][rows], res[1][rows]).astype(BF16)

    return _call(body, name=name, out_shape=jax.ShapeDtypeStruct((bsz, s, d), BF16), operands=[qkv] * 5 + [sinks],
                 grid=(bsz, n_blocks), in_specs=_attn_specs(d, n_blocks, False) + [SMEM_SPEC],
                 out_specs=pl.BlockSpec((None, WINDOW, d), lambda b, i: (b, i, 0)), sem=("parallel", "parallel"), comm=comm)


def _attn_bwd(qkv, do, sinks, *, name, comm=None):
    bsz, s, qkv_dim = qkv.shape
    d = qkv_dim - 2 * N_KV_HEADS * HEAD_DIM
    n_heads = d // HEAD_DIM
    group = n_heads // N_KV_HEADS
    pairs = group // 2
    n_blocks = s // WINDOW
    tn_dims = (((0,), (0,)), ((), ()))

    def body(q_ref, kp_ref, kc_ref, vp_ref, vc_ref, do_ref, sink_ref, dqkv_ref, colsum_ref, dsink_ref,
             dq_prev, dk_carry, dv_carry):
        b, i = pl.program_id(0), pl.program_id(1)
        left = lax.broadcasted_iota(jnp.int32, (1, LANES), 1) < HEAD_DIM

        @pl.when((b == 0) & (i == 0))
        def _():
            colsum_ref[...] = jnp.zeros_like(colsum_ref)
            dsink_ref[...] = jnp.zeros_like(dsink_ref)

        @pl.when(i == 0)
        def _():
            dqkv_ref[...] = jnp.zeros_like(dqkv_ref)
            dk_carry[...] = jnp.zeros_like(dk_carry)
            dv_carry[...] = jnp.zeros_like(dv_carry)

        @pl.when(i > 0)
        def _():
            dq_v = dq_prev[...]
            dqkv_ref[:, pl.ds(0, d)] = dq_v.astype(BF16)
            colsum_ref[:, pl.ds(0, d)] += jnp.sum(dq_v, axis=0, keepdims=True)

        @pl.when(i < n_blocks)
        def _():
            distf, valid = _attn_masks(i == 0)
            kspan = jnp.concatenate([kp_ref[...], kc_ref[...]], axis=0)
            vspan = jnp.concatenate([vp_ref[...], vc_ref[...]], axis=0)
            dk_blk, dv_blk = [], []
            for kv in range(N_KV_HEADS):
                kdup, vdup = _dup_halves(kspan, kv, left), _dup_halves(vspan, kv, left)
                dq_res, dk_sum, dv_sum = [], None, None
                for par in range(2):
                    keep = left if par == 0 else jnp.logical_not(left)
                    cols = [pl.ds((kv * pairs + p) * LANES, LANES) for p in range(pairs)]
                    zero = jnp.zeros((), BF16)
                    lhs = jnp.concatenate([jnp.where(keep, q_ref[:, cl], zero) for cl in cols], axis=0)
                    dol = jnp.concatenate([jnp.where(keep, do_ref[:, cl], zero) for cl in cols], axis=0)
                    sc = lax.dot_general(lhs, kdup, (((1,), (1,)), ((), ())), preferred_element_type=F32)
                    dp = lax.dot_general(dol, vdup, (((1,), (1,)), ((), ())), preferred_element_type=F32)
                    probs, dscores = [], []
                    for p in range(pairs):
                        h = kv * group + 2 * p + par
                        rows = slice(p * WINDOW, (p + 1) * WINDOW)
                        pr, p_sink = _attn_probs(sc[rows], h, n_heads, distf, valid, sink_ref[h])
                        delta = jnp.sum(pr * dp[rows], axis=-1, keepdims=True)
                        dscores.append((pr * (dp[rows] - delta) * (HEAD_DIM ** -0.5)).astype(BF16))
                        probs.append(pr.astype(BF16))
                        dsink_ref[pl.ds(h, 1), :] += jnp.zeros((1, LANES), F32) - jnp.sum(p_sink * delta)
                    ds_all = jnp.concatenate(dscores, axis=0)
                    p_all = jnp.concatenate(probs, axis=0)
                    dq_res.append(jnp.dot(ds_all, kdup, preferred_element_type=F32))
                    dk_par = lax.dot_general(ds_all, lhs, tn_dims, preferred_element_type=F32)
                    dv_par = lax.dot_general(p_all, dol, tn_dims, preferred_element_type=F32)
                    dk_sum = dk_par if dk_sum is None else dk_sum + dk_par
                    dv_sum = dv_par if dv_sum is None else dv_sum + dv_par
                for p in range(pairs):
                    rows = slice(p * WINDOW, (p + 1) * WINDOW)
                    dq_prev[:, pl.ds((kv * pairs + p) * LANES, LANES)] = jnp.where(left, dq_res[0][rows], dq_res[1][rows])
                dk_blk.append(dk_sum + pltpu.roll(dk_sum, HEAD_DIM, axis=1))
                dv_blk.append(dv_sum + pltpu.roll(dv_sum, HEAD_DIM, axis=1))
            dk_span = jnp.where(left, dk_blk[0], dk_blk[1])
            dv_span = jnp.where(left, dv_blk[0], dv_blk[1])
            dk_done = dk_carry[...] + dk_span[:WINDOW]
            dv_done = dv_carry[...] + dv_span[:WINDOW]
            dk_carry[...] = dk_span[WINDOW:]
            dv_carry[...] = dv_span[WINDOW:]

            @pl.when(i > 0)
            def _():
                dqkv_ref[:, pl.ds(d, LANES)] = dk_done.astype(BF16)
                dqkv_ref[:, pl.ds(d + LANES, LANES)] = dv_done.astype(BF16)
                colsum_ref[:, pl.ds(d, LANES)] += jnp.sum(dk_done, axis=0, keepdims=True)
                colsum_ref[:, pl.ds(d + LANES, LANES)] += jnp.sum(dv_done, axis=0, keepdims=True)

        @pl.when(i == n_blocks)
        def _():
            dk_done, dv_done = dk_carry[...], dv_carry[...]
            dqkv_ref[:, pl.ds(d, LANES)] = dk_done.astype(BF16)
            dqkv_ref[:, pl.ds(d + LANES, LANES)] = dv_done.astype(BF16)
            colsum_ref[:, pl.ds(d, LANES)] += jnp.sum(dk_done, axis=0, keepdims=True)
            colsum_ref[:, pl.ds(d + LANES, LANES)] += jnp.sum(dv_done, axis=0, keepdims=True)

    do_spec = pl.BlockSpec((None, WINDOW, d), lambda b, i: (b, jnp.minimum(i, n_blocks - 1), 0))
    out_shape = (jax.ShapeDtypeStruct((bsz, s, qkv_dim), BF16), jax.ShapeDtypeStruct((1, qkv_dim), F32),
                 jax.ShapeDtypeStruct((n_heads, LANES), F32))
    out_specs = (pl.BlockSpec((None, WINDOW, qkv_dim), lambda b, i: (b, jnp.maximum(i - 1, 0), 0)),
                 pl.BlockSpec((1, qkv_dim), lambda b, i: (0, 0)),
                 pl.BlockSpec((n_heads, LANES), lambda b, i: (0, 0)))
    return _call(body, name=name, out_shape=out_shape, operands=[qkv] * 5 + [do, sinks], grid=(bsz, n_blocks + 1),
                 in_specs=_attn_specs(d, n_blocks, True) + [do_spec, SMEM_SPEC], out_specs=out_specs,
                 scratch_shapes=[pltpu.VMEM((WINDOW, d), F32), pltpu.VMEM((WINDOW, LANES), F32), pltpu.VMEM((WINDOW, LANES), F32)],
                 sem=("arbitrary", "arbitrary"), comm=comm)


def _conv_tile(s):
    return min(256, s)


def _halo_specs(ts, width, s):
    per = ts // CONV_HALO
    prev = pl.BlockSpec((None, CONV_HALO, width), lambda b, i: (b, jnp.maximum(i * per - 1, 0), 0))
    nxt = pl.BlockSpec((None, CONV_HALO, width), lambda b, i: (b, jnp.minimum((i + 1) * per, s // CONV_HALO - 1), 0))
    cur = pl.BlockSpec((None, ts, width), lambda b, i: (b, i, 0))
    return prev, cur, nxt


def _glu(u, d):
    return u[:, :d] * jax.nn.sigmoid(u[:, d:])


def _store_shifted(shifted, value):
    rows = value.shape[0]
    shifted[0] = value
    for b in range(1, SUBLANES):
        shifted[b] = pltpu.roll(value, rows - b, axis=0)


def _at(shifted, base, offset, n_rows, lanes):
    return shifted[offset % SUBLANES, pl.ds(base + (offset - offset % SUBLANES), n_rows), lanes]


def _taps(w_ref, shifted, out_ref, ts, d, offset):
    for r0 in range(0, ts, CONV_ROWS):
        for l0 in range(0, d, LANES):
            lanes = pl.ds(l0, LANES)
            acc = jnp.zeros((CONV_ROWS, LANES), F32)
            for j in range(CONV_WIDTH):
                acc = acc + w_ref[pl.ds(j, 1), lanes] * _at(shifted, r0, offset(j), CONV_ROWS, lanes)
            out_ref[pl.ds(r0, CONV_ROWS), lanes] = acc


def _conv_fwd(u, w_dw, b_dw, ln_g, ln_b, *, name, comm=None):
    bsz, s, d2 = u.shape
    d = d2 // 2
    ts = _conv_tile(s)

    def body(up_ref, uc_ref, w_ref, bdw_ref, g_ref, b_ref, z_ref, v_ref, gbuf):
        halo = jnp.where(pl.program_id(1) > 0, _glu(up_ref[...], d), 0.0)
        _store_shifted(gbuf, jnp.concatenate([halo, _glu(uc_ref[...], d)], axis=0))
        _taps(w_ref, gbuf, v_ref, ts, d, lambda j: CONV_HALO - (CONV_WIDTH - 1) + j)
        v = v_ref[...] + bdw_ref[...]
        v_ref[...] = v
        mu = jnp.mean(v, axis=-1, keepdims=True)
        cen = v - mu
        rstd = lax.rsqrt(jnp.mean(cen * cen, axis=-1, keepdims=True) + NORM_EPS)
        ln = cen * rstd * g_ref[...] + b_ref[...]
        z_ref[...] = (ln * jax.nn.sigmoid(ln)).astype(BF16)

    prev, cur, _ = _halo_specs(ts, d2, s)
    one = pl.BlockSpec((1, d), lambda b, i: (0, 0))
    row = pl.BlockSpec((None, ts, d), lambda b, i: (b, i, 0))
    return _call(body, name=name, out_shape=(jax.ShapeDtypeStruct((bsz, s, d), BF16), jax.ShapeDtypeStruct((bsz, s, d), F32)),
                 operands=[u, u, w_dw, b_dw, ln_g, ln_b], grid=(bsz, s // ts),
                 in_specs=[prev, cur, pl.BlockSpec((CONV_HALO, d), lambda b, i: (0, 0)), one, one, one],
                 out_specs=(row, row), scratch_shapes=[pltpu.VMEM((SUBLANES, ts + CONV_HALO, d), F32)],
                 sem=("parallel", "parallel"), comm=comm)


def _conv_bwd_ln(dz, v, ln_g, ln_b, *, name):
    bsz, s, d = v.shape
    ts = _row_tile(s, d * 4, 2 << 20)

    def body(dz_ref, v_ref, g_ref, b_ref, dv_ref, dg_ref, db_ref, dbdw_ref):
        v_v = v_ref[...]
        mu = jnp.mean(v_v, axis=-1, keepdims=True)
        cen = v_v - mu
        rstd = lax.rsqrt(jnp.mean(cen * cen, axis=-1, keepdims=True) + NORM_EPS)
        vhat = cen * rstd
        ln = vhat * g_ref[...] + b_ref[...]
        sig = jax.nn.sigmoid(ln)
        dln = dz_ref[...] * (sig * (1.0 + ln * (1.0 - sig)))
        dvhat = dln * g_ref[...]
        dv = rstd * (dvhat - jnp.mean(dvhat, axis=-1, keepdims=True)
                     - vhat * jnp.mean(dvhat * vhat, axis=-1, keepdims=True))
        dv_ref[...] = dv

        @pl.when((pl.program_id(0) == 0) & (pl.program_id(1) == 0))
        def _():
            dg_ref[...] = jnp.zeros_like(dg_ref)
            db_ref[...] = jnp.zeros_like(db_ref)
            dbdw_ref[...] = jnp.zeros_like(dbdw_ref)

        dg_ref[...] += jnp.sum(dln * vhat, axis=0, keepdims=True)
        db_ref[...] += jnp.sum(dln, axis=0, keepdims=True)
        dbdw_ref[...] += jnp.sum(dv, axis=0, keepdims=True)

    row = pl.BlockSpec((None, ts, d), lambda b, i: (b, i, 0))
    one = pl.BlockSpec((1, d), lambda b, i: (0, 0))
    vec = jax.ShapeDtypeStruct((1, d), F32)
    return pl.pallas_call(body, out_shape=(jax.ShapeDtypeStruct(v.shape, F32), vec, vec, vec), grid=(bsz, s // ts),
                          in_specs=[row, row, one, one], out_specs=(row, one, one, one), name=name,
                          compiler_params=_params("arbitrary", "arbitrary"))(dz, v, ln_g, ln_b)


def _conv_bwd_taps(dv, u, w_dw, *, name, comm=None):
    bsz, s, d = dv.shape
    ts = _conv_tile(s)
    n_tiles = s // ts

    def body(dvc_ref, dvn_ref, uc_ref, w_ref, du_ref, dbu_ref, dw_ref, dvbuf, dglu, glu):
        i = pl.program_id(1)

        @pl.when((pl.program_id(0) == 0) & (i == 0))
        def _():
            dbu_ref[...] = jnp.zeros_like(dbu_ref)
            dw_ref[...] = jnp.zeros_like(dw_ref)

        u_v = uc_ref[...]
        a, sig = u_v[:, :d], jax.nn.sigmoid(u_v[:, d:])
        glu[...] = a * sig
        ahead = jnp.where(i < n_tiles - 1, dvn_ref[...], 0.0)
        _store_shifted(dvbuf, jnp.concatenate([dvc_ref[...], ahead], axis=0))
        _taps(w_ref, dvbuf, dglu, ts, d, lambda j: CONV_WIDTH - 1 - j)
        for l0 in range(0, d, LANES):
            lanes = pl.ds(l0, LANES)
            for j in range(CONV_WIDTH):
                acc = jnp.zeros((SUBLANES, LANES), F32)
                for r0 in range(0, ts, CONV_ROWS):
                    prod = glu[pl.ds(r0, CONV_ROWS), lanes] * _at(dvbuf, r0, CONV_WIDTH - 1 - j, CONV_ROWS, lanes)
                    for k in range(0, CONV_ROWS, SUBLANES):
                        acc = acc + prod[k:k + SUBLANES]
                dw_ref[j, :, lanes] += acc
        dg_v = dglu[...]
        da = dg_v * sig
        dgate = dg_v * a * sig * (1.0 - sig)
        du_ref[:, pl.ds(0, d)] = da.astype(BF16)
        du_ref[:, pl.ds(d, d)] = dgate.astype(BF16)
        dbu_ref[:, pl.ds(0, d)] += jnp.sum(da, axis=0, keepdims=True)
        dbu_ref[:, pl.ds(d, d)] += jnp.sum(dgate, axis=0, keepdims=True)

    _, dv_cur, dv_next = _halo_specs(ts, d, s)
    _, u_cur, _ = _halo_specs(ts, 2 * d, s)
    out_shape = (jax.ShapeDtypeStruct((bsz, s, 2 * d), BF16), jax.ShapeDtypeStruct((1, 2 * d), F32),
                 jax.ShapeDtypeStruct((CONV_HALO, SUBLANES, d), F32))
    out_specs = (pl.BlockSpec((None, ts, 2 * d), lambda b, i: (b, i, 0)), pl.BlockSpec((1, 2 * d), lambda b, i: (0, 0)),
                 pl.BlockSpec((CONV_HALO, SUBLANES, d), lambda b, i: (0, 0, 0)))
    return _call(body, name=name, out_shape=out_shape, operands=[dv, dv, u, w_dw], grid=(bsz, n_tiles),
                 in_specs=[dv_cur, dv_next, u_cur, pl.BlockSpec((CONV_HALO, d), lambda b, i: (0, 0))],
                 out_specs=out_specs,
                 scratch_shapes=[pltpu.VMEM((SUBLANES, ts + CONV_HALO, d), F32), pltpu.VMEM((ts, d), F32),
                                 pltpu.VMEM((ts, d), F32)],
                 sem=("arbitrary", "arbitrary"), comm=comm)


def _mod_fwd(c_all, w_mod, b_mod, *, name):
    n_layers, d, n = w_mod.shape
    rows = c_all.shape[0]

    def body(c_ref, w_ref, b_ref, o_ref):
        cv = c_ref[...]
        cs = (cv * jax.nn.sigmoid(cv)).astype(BF16)
        o_ref[...] = jnp.dot(cs, w_ref[...].astype(BF16), preferred_element_type=F32) + b_ref[...]

    return pl.pallas_call(body, out_shape=jax.ShapeDtypeStruct((n_layers, rows, n), F32), grid=(n_layers,),
                          in_specs=[pl.BlockSpec((rows, d), lambda l: (0, 0)), pl.BlockSpec((None, d, n), lambda l: (l, 0, 0)),
                                    pl.BlockSpec((None, 1, n), lambda l: (l, 0, 0))],
                          out_specs=pl.BlockSpec((None, rows, n), lambda l: (l, 0, 0)), name=name,
                          compiler_params=_params("parallel"))(c_all, w_mod, b_mod)


def _mod_bwd(c_all, dmod, *, name):
    n_layers, rows, n = dmod.shape
    d = c_all.shape[1]

    def body(c_ref, g_ref, o_ref):
        cv = c_ref[...]
        cs = (cv * jax.nn.sigmoid(cv)).astype(BF16)
        o_ref[...] = lax.dot_general(cs, g_ref[...].astype(BF16), (((0,), (0,)), ((), ())), preferred_element_type=F32)

    return pl.pallas_call(body, out_shape=jax.ShapeDtypeStruct((n_layers, d, n), F32), grid=(n_layers,),
                          in_specs=[pl.BlockSpec((rows, d), lambda l: (0, 0)), pl.BlockSpec((None, rows, n), lambda l: (l, 0, 0))],
                          out_specs=pl.BlockSpec((None, d, n), lambda l: (l, 0, 0)), name=name,
                          compiler_params=_params("parallel"))(c_all, dmod)


def _add_half(g, recv, my_c, *, name):
    p, _, r, cdim = g.shape
    tr = _row_tile(r, cdim * 4)

    def body(c_ref, g_ref, r_ref, o_ref):
        o_ref[...] = (g_ref[...] + r_ref[...]).astype(BF16)

    grid_spec = pltpu.PrefetchScalarGridSpec(
        num_scalar_prefetch=1, grid=(p, r // tr),
        in_specs=[pl.BlockSpec((None, None, tr, cdim), lambda q, i, c_ref: (q, c_ref[0], i, 0)),
                  pl.BlockSpec((None, None, tr, cdim), lambda q, i, c_ref: (q, 0, i, 0))],
        out_specs=pl.BlockSpec((None, tr, cdim), lambda q, i, c_ref: (q, i, 0)))
    return pl.pallas_call(body, out_shape=jax.ShapeDtypeStruct((p, r, cdim), BF16), grid_spec=grid_spec, name=name,
                          compiler_params=_params("parallel", "parallel"))(my_c, g, recv)


def _add_pieces(chip, recv, my_qc, stacked, layer, n_layers, *, name):
    _, r, cdim = chip.shape
    tr = _row_tile(r, cdim * 4)

    def body(qc_ref, own_ref, r0_ref, r1_ref, r2_ref, *rest):
        f32 = lambda ref: ref[...].astype(F32)
        rest[-1][...] = ((f32(own_ref) + f32(r0_ref)) + f32(r1_ref)) + f32(r2_ref)

    piece = lambda k: pl.BlockSpec((None, tr, cdim), lambda i, qc_ref: (k, i, 0))
    in_specs = [pl.BlockSpec((None, tr, cdim), lambda i, qc_ref: (qc_ref[0], i, 0)), piece(0), piece(1), piece(2)]
    operands = [my_qc, chip, recv, recv, recv]
    aliases = {}
    if stacked is not None:
        in_specs.append(pl.BlockSpec(memory_space=pl.ANY))
        operands.append(stacked)
        aliases = {len(operands) - 1: 0}
    grid_spec = pltpu.PrefetchScalarGridSpec(
        num_scalar_prefetch=1, grid=(r // tr,), in_specs=in_specs,
        out_specs=pl.BlockSpec((None, None, tr, cdim), lambda i, qc_ref: (layer, qc_ref[1], i, 0)))
    return pl.pallas_call(body, out_shape=jax.ShapeDtypeStruct((n_layers, 2, r, cdim), F32), grid_spec=grid_spec,
                          input_output_aliases=aliases, name=name, compiler_params=_params("parallel"))(*operands)


def _sum_devices(parts, dmod, *, name):
    def body(p_ref, m_ref, o_ref, b_ref):
        acc = p_ref[0]
        for k in range(1, N_DEV):
            acc = acc + p_ref[k]
        o_ref[...] = acc
        tot = None
        for k in range(N_DEV):
            for e in range(dmod.shape[2]):
                tot = m_ref[k, :, e] if tot is None else tot + m_ref[k, :, e]
        b_ref[...] = tot

    out_shape = (jax.ShapeDtypeStruct(parts.shape[1:], F32),
                 jax.ShapeDtypeStruct((dmod.shape[1],) + dmod.shape[3:], F32))
    return pl.pallas_call(body, out_shape=out_shape, in_specs=[VMEM_SPEC, VMEM_SPEC], out_specs=(VMEM_SPEC, VMEM_SPEC),
                          name=name, compiler_params=_params())(parts, dmod)


def _adamw(w, g, m, v, *, name, comm=None):
    r, cdim = w.shape
    tr = _row_tile(r, cdim * 4, 2 << 20)

    def body(w_ref, g_ref, m_ref, v_ref, d_ref, nm_ref, nv_ref):
        gv = g_ref[...]
        nm = ADAM_B1 * m_ref[...] + (1.0 - ADAM_B1) * gv
        nv = ADAM_B2 * v_ref[...] + (1.0 - ADAM_B2) * (gv * gv)
        m_hat = nm / (1.0 - ADAM_B1 ** ADAM_STEP)
        v_hat = nv / (1.0 - ADAM_B2 ** ADAM_STEP)
        d_ref[...] = -ADAM_LR * (m_hat / (jnp.sqrt(v_hat) + ADAM_EPS) + ADAM_WD * w_ref[...])
        nm_ref[...] = nm
        nv_ref[...] = nv

    row = pl.BlockSpec((tr, cdim), lambda i: (i, 0))
    shape = jax.ShapeDtypeStruct((r, cdim), F32)
    return _call(body, name=name, out_shape=(shape, shape, shape), operands=[w, g, m, v], grid=(r // tr,),
                 in_specs=[row] * 4, out_specs=(row, row, row), sem=("parallel",), comm=comm)


def _all_gather_small(block, *, name, comm=None):
    m_per, n = block.shape

    def body(x_ref, out_ref, send_sems, recv_sems, local_sem):
        x, y, c, _ = _position()
        me, sibling = (x, y, c), (x, y, 1 - c)
        chips = [_peer(k, x, y, c)[:2] for k in CHIP_KINDS]

        def rows(px, py, pc):
            return out_ref.at[pl.ds((4 * px + 2 * py + pc) * m_per, m_per), :]

        def copy(k, blk, to, src=None):
            return pltpu.make_async_remote_copy(src_ref=rows(*blk) if src is None else src, dst_ref=rows(*blk),
                                                send_sem=send_sems.at[k], recv_sem=recv_sems.at[k],
                                                device_id=to, device_id_type=MESH)

        mine = pltpu.make_async_copy(x_ref, rows(*me), local_sem)
        mine.start()
        first = [copy(0, me, sibling, src=x_ref)]
        first += [copy(1 + j, me, (*chip, c), src=x_ref) for j, chip in enumerate(chips)]
        for cp in first:
            cp.start()
        passed = [copy(4 + j, (*chip, c), sibling) for j, chip in enumerate(chips)]
        for j, chip in enumerate(chips):
            copy(1 + j, (*chip, c), me).wait_recv()
            passed[j].start()
        copy(0, sibling, me).wait_recv()
        for j, chip in enumerate(chips):
            copy(4 + j, (*chip, 1 - c), me).wait_recv()
        for cp in first + passed:
            cp.wait_send()
        mine.wait()

    return _call(body, name=name, out_shape=jax.ShapeDtypeStruct((N_DEV * m_per, n), block.dtype), operands=[block],
                 in_specs=[VMEM_SPEC], out_specs=VMEM_SPEC,
                 scratch_shapes=[pltpu.SemaphoreType.DMA((7,)), pltpu.SemaphoreType.DMA((7,)), pltpu.SemaphoreType.DMA],
                 comm=comm)


def _pack(arrays, width):
    flat = jnp.concatenate([a.reshape(-1).astype(F32) for a in arrays])
    rows = -(-flat.shape[0] // width)
    rows = -(-rows // SUBLANES) * SUBLANES
    return jnp.pad(flat, (0, rows * width - flat.shape[0])).reshape(rows, width)


def _unpack(packed, shapes):
    flat, out, off = packed.reshape(-1), [], 0
    for shp in shapes:
        size = 1
        for dim in shp:
            size *= dim
        out.append(flat[off:off + size].reshape(shp))
        off += size
    return out


def _adamw_packed(ws, gs, ms, vs, width, *, name):
    shapes = [w.shape for w in ws]
    res = _adamw(_pack(ws, width), _pack(gs, width), _pack(ms, width), _pack(vs, width), name=name)
    return [_unpack(r, shapes) for r in res]


MB = float(1 << 20)


def _nbytes(shape, dtype):
    size = jnp.dtype(dtype).itemsize
    for dim in shape:
        size *= dim
    return size


def _gather_plans(shard, on_ready):
    r, cdim = shard.shape
    half = r // 2
    buf = jax.ShapeDtypeStruct((N_CHIPS, r, cdim), shard.dtype)
    rows = lambda ref, q, h: ref.at[pl.ds(q, 1), pl.ds(h * half, half)]
    there = lambda kind, pos: _peer_shard(kind, pos[0], pos[1])
    cost = 3 * _nbytes((half, cdim), shard.dtype) / MB

    def forward(outs):
        copies = [(lambda ins, outs, pos, kind=kind: rows(ins[0], there(kind, pos), pos[2]),
                   lambda ins, outs, pos, kind=kind: rows(outs[0], there(kind, pos), pos[2]),
                   lambda ins, outs, pos, kind=kind: rows(outs[0], there(kind, pos), 1 - pos[2]), "c") for kind in CHIP_KINDS]
        return _Plan("d2d", cost, [outs[0]], [jax.ShapeDtypeStruct(buf.shape, buf.dtype)], copies, {0: 0},
                     lambda done: on_ready(done[0]))

    copies = [(lambda ins, outs, pos: ins[0].at[:, pl.ds(pos[2] * half, half)],
               lambda ins, outs, pos: rows(outs[0], pos[3], pos[2]),
               lambda ins, outs, pos, kind=kind: rows(outs[0], there(kind, pos), pos[2]), kind) for kind in CHIP_KINDS]
    own = lambda ins, outs, pos: outs[0].at[pl.ds(pos[3], 1)]
    copies.append((lambda ins, outs, pos: ins[0], own, own, "c"))
    return _Plan("ici", cost, [shard[None]], [buf], copies), forward


class _Exchanges:
    def __init__(self):
        self.queue = []

    def add(self, plan, front=False):
        if front:
            self.queue.insert(0, plan)
        else:
            self.queue.append(plan)

    def take(self, budget_mb, at_least_one=False):
        chosen, spent = [p for p in self.queue if p.link == "d2d"], 0.0
        for p in self.queue:
            if p.link == "ici" and (spent + p.cost <= budget_mb or (at_least_one and spent == 0.0)):
                chosen.append(p)
                spent += p.cost
        if not chosen:
            return None
        self.queue = [p for p in self.queue if all(p is not ch for ch in chosen)]
        return _merge(chosen)

    def flush(self, budget_mb, until=lambda: False):
        while self.queue and not until():
            _exchange(self.take(budget_mb, at_least_one=True), name="exchange")


def kernel(x, c, w_mod, b_mod, norm_mix, norm_mlp, w_qkv, b_qkv, w_o, b_o, sinks, w_pw1, b_pw1, w_dw, b_dw, conv_ln_g, conv_ln_b, w_pw2, b_pw2, w_up, w_down, final_norm, loss_target, m_w_mod, m_b_mod, m_norm_mix, m_norm_mlp, m_w_qkv, m_b_qkv, m_w_o, m_b_o, m_sinks, m_w_pw1, m_b_pw1, m_w_dw, m_b_dw, m_conv_ln_g, m_conv_ln_b, m_w_pw2, m_b_pw2, m_w_up, m_w_down, m_final_norm, v_w_mod, v_b_mod, v_norm_mix, v_norm_mlp, v_w_qkv, v_b_qkv, v_w_o, v_b_o, v_sinks, v_w_pw1, v_b_pw1, v_w_dw, v_b_dw, v_conv_ln_g, v_conv_ln_b, v_w_pw2, v_b_pw2, v_w_up, v_w_down, v_final_norm):
    bsz, s, d = x.shape
    t = bsz * s
    depth = w_mod.shape[0]
    n_attn, n_conv = w_qkv.shape[0], w_pw1.shape[0]
    qkv_dim = d + 2 * N_KV_HEADS * HEAD_DIM
    mx, my, mc, mq = _position()
    me = 4 * mx + 2 * my + mc
    my_c = jnp.reshape(mc, (1,)).astype(jnp.int32)
    my_qc = jnp.stack([mq, mc]).astype(jnp.int32)
    pending = _Exchanges()
    SMALL, MEDIUM, LARGE = 2.5, 3.5, 6.5

    weights = {}
    order = []
    for i in range(depth):
        j = i // 2
        order += ([(("qkv", j), w_qkv[j], True), (("o", j), w_o[j], False)] if i % 2 == 0 else
                  [(("pw1", j), w_pw1[j], True), (("pw2", j), w_pw2[j], False)])
        order += [(("up", i), w_up[i], True), (("down", i), w_down[i], False)]
    for key, shard, by_cols in order:
        def ready(buf, key=key, by_cols=by_cols):
            weights[key] = jnp.transpose(buf, (1, 0, 2)).reshape(buf.shape[1], -1) if by_cols else buf.reshape(-1, buf.shape[2])
        ici, forward = _gather_plans(shard.astype(BF16), ready)
        ici.then = lambda outs, forward=forward: pending.add(forward(outs), front=True)
        pending.add(ici)

    def weight(key):
        pending.flush(SMALL, until=lambda: key in weights)
        return weights[key]

    small_sharded = [b_pw1, w_dw, b_dw, conv_ln_g, conv_ln_b, b_pw2]
    c_pad = jnp.pad(c, ((0, SUBLANES - bsz), (0, 0)))
    gathered = _all_gather_small(jnp.concatenate([c_pad, _pack(small_sharded, d)], axis=0), name="gather_c",
                                 comm=pending.take(SMALL, True))
    gathered = gathered.reshape(N_DEV, -1, d)
    c_all = gathered[:, :bsz].reshape(N_DEV * bsz, d)
    per_chip = [_unpack(gathered[2 * q, SUBLANES:], [a.shape for a in small_sharded]) for q in range(N_CHIPS)]
    b_pw1_f, w_dw_f, b_dw_f, ln_g_f, ln_b_f, b_pw2_f = [jnp.concatenate([per_chip[q][k] for q in range(N_CHIPS)], axis=-1)
                                                           for k in range(len(small_sharded))]
    w_dw_f = jnp.pad(w_dw_f, ((0, 0), (0, CONV_HALO - CONV_WIDTH), (0, 0)))

    n_mod = w_mod.shape[2]
    b_mod_cols = lax.dynamic_slice_in_dim(b_mod, mq * n_mod, n_mod, axis=1).reshape(depth, 1, n_mod)
    c_all8 = gathered[:, :SUBLANES].reshape(N_DEV * SUBLANES, d)
    mod_part = _mod_fwd(c_all8, w_mod, b_mod_cols, name="mod_fwd")
    peer_rows = lambda kind, pos: pl.ds(SUBLANES * (2 * _peer_shard(kind, pos[0], pos[1]) + pos[2]), SUBLANES)
    got = {}
    pending.add(_Plan("ici", 0.01, [mod_part],[jax.ShapeDtypeStruct((len(CHIP_KINDS), depth, SUBLANES, n_mod), F32)],
                      [(lambda ins, outs, pos, kind=kind: ins[0].at[:, peer_rows(kind, pos)],
                        lambda ins, outs, pos, k=k: outs[0].at[k], lambda ins, outs, pos, k=k: outs[0].at[k], kind)
                       for k, kind in enumerate(CHIP_KINDS)], then=lambda outs: got.update(rows=outs[0])), front=True)
    _exchange(pending.take(0.0, at_least_one=True), name="mod_exchange")
    own = lax.dynamic_slice_in_dim(mod_part, me * SUBLANES, SUBLANES, axis=1)
    parts = [own, got["rows"][0], got["rows"][1], got["rows"][2]]
    part_of = (0, 2, 1, 3)

    def shard(sidx):
        dist, out = jnp.bitwise_xor(mq, sidx), parts[0]
        for distance in (1, 2, 3):
            out = jnp.where(dist == distance, parts[part_of[distance]], out)
        return out

    mod = jnp.stack([shard(sidx) for sidx in range(N_CHIPS)])[:, :, :bsz]
    mod = jnp.transpose(mod, (1, 2, 0, 3)).reshape(depth, bsz, N_MOD, 1, d)
    mods = [[mod[i][:, k] for k in range(N_MOD)] for i in range(depth)]

    saved = []
    xc = x
    h1 = _normmod(xc, norm_mix[0][None], mods[0][1], mods[0][0], name="normmod")
    for i in range(depth):
        j = i // 2
        sh1, sc1, g1, sh2, sc2, g2 = mods[i]
        mlp_norm = (norm_mlp[i][None], sc2, sh2)
        if i % 2 == 0:
            wq, wo = weight(("qkv", j)), weight(("o", j))
            qkv = _mm(h1.reshape(t, d), wq, bias=b_qkv[j], tn=qkv_dim, name="mm_qkv",
                      comm=pending.take(SMALL, True)).reshape(bsz, s, qkv_dim)
            mix = _attn_fwd(qkv, sinks[j], name="attn_fwd", comm=pending.take(LARGE, True))
            y1, x1, h2 = _mm(mix.reshape(t, d), wo, bias=b_o[j], epi="resid", resid=xc.reshape(t, d), gate=g1, seq=s,
                             norm=mlp_norm, tn=d, name="mm_out", comm=pending.take(SMALL, True))
            extra = (qkv, mix)
        else:
            wp1, wp2 = weight(("pw1", j)), weight(("pw2", j))
            u = _mm(h1.reshape(t, d), wp1, bias=b_pw1_f[j], out_dtype=F32, tn=d, name="mm_pw1",
                    comm=pending.take(MEDIUM, True)).reshape(bsz, s, 2 * d)
            mix, conv_v = _conv_fwd(u, w_dw_f[j], b_dw_f[j][None], ln_g_f[j][None], ln_b_f[j][None], name="conv_fwd",
                                    comm=pending.take(LARGE, True))
            y1, x1, h2 = _mm(mix.reshape(t, d), wp2, bias=b_pw2_f[j], epi="resid", resid=xc.reshape(t, d), gate=g1, seq=s,
                             norm=mlp_norm, tn=d, name="mm_out", comm=pending.take(SMALL, True))
            extra = (u, mix, conv_v)
        act, slope = _mm(h2, weight(("up", i)), epi="relu2", tm=1024, tn=2 * d, name="mm_up", comm=pending.take(LARGE, True))
        mix_norm = (norm_mix[i + 1][None], mods[i + 1][1], mods[i + 1][0]) if i + 1 < depth else None
        y2, x2, *h_next = _mm(act, weight(("down", i)), epi="resid", resid=x1, gate=g2, seq=s, norm=mix_norm,
                              tm=512, tn=d, name="mm_down", comm=pending.take(LARGE, True))
        saved.append((xc, h1, extra, y1, x1.reshape(bsz, s, d), h2, act, y2, slope))
        xc = x2.reshape(bsz, s, d)
        h1 = h_next[0] if h_next else None
    pending.flush(LARGE)

    last_y2, last_g2 = saved[-1][7].reshape(bsz, s, d), mods[-1][5]
    dx, loss_cols, d_final, dyb, dg2, _ = _loss_head(xc, loss_target, final_norm[None], (last_y2, last_g2), name="loss_head")
    loss = lax.psum(0.5 / d * jnp.sum(loss_cols), ("x", "y", "c"))

    totals = {}

    def reduce_scatter(name, layer, n_layers, grad):
        view = grad.reshape(N_CHIPS, 2, grad.shape[1] // 2, grad.shape[2])
        half_shape = (N_CHIPS, 1) + view.shape[2:]

        def scatter(outs):
            chip = _add_half(view, outs[0], my_c, name="rs_chipsum")
            copies = [(lambda ins, outs, pos, kind=kind: ins[0].at[pl.ds(_peer_shard(kind, pos[0], pos[1]), 1)],
                       lambda ins, outs, pos, k=k: outs[0].at[pl.ds(k, 1)],
                       lambda ins, outs, pos, k=k: outs[0].at[pl.ds(k, 1)], kind) for k, kind in enumerate(CHIP_KINDS)]

            def total(outs):
                totals[name] = _add_pieces(chip, outs[0], my_qc, totals.get(name), layer, n_layers, name="rs_total")

            pending.add(_Plan("ici", 3 * _nbytes(chip.shape[1:], BF16) / MB, [chip],
                              [jax.ShapeDtypeStruct((3,) + chip.shape[1:], BF16)], copies, then=total))

        pending.add(_Plan("d2d", _nbytes(half_shape, F32) / MB, [view], [jax.ShapeDtypeStruct(half_shape, F32)],
                          [(lambda ins, outs, pos: ins[0].at[:, pl.ds(1 - pos[2], 1)],
                            lambda ins, outs, pos: outs[0], lambda ins, outs, pos: outs[0], "c")], then=scatter))

    dmods, small = [None] * depth, {}
    for i in reversed(range(depth)):
        j = i // 2
        xin, h1, extra, y1, x1, h2, act, _, slope = saved[i]
        sh1, sc1, g1, sh2, sc2, g2 = mods[i]
        dyb = dyb.reshape(t, d)
        reduce_scatter("down", i, depth,
                       _mm_tn(act, dyb, name="dw_down", comm=pending.take(LARGE)).reshape(N_CHIPS, -1, d))
        dup = _mm(dyb, weights["down", i], nt=True, epi="dact", act=slope, tm=1024, tn=2 * d, name="mm_dact",
                  comm=pending.take(LARGE))
        reduce_scatter("up", i, depth,
                       _mm_tn(h2.reshape(t, d), dup, col_shards=N_CHIPS, name="dw_up", comm=pending.take(LARGE)))
        dh2 = _mm(dup, weights["up", i], nt=True, tm=512, tn=d, name="mm_dh2",
                  comm=pending.take(LARGE)).reshape(bsz, s, d)
        dx1, p2, dsh2, dyb, dg1, sdx = _normmod_bwd(x1, dh2, dx, norm_mlp[i][None], sc2, (y1.reshape(bsz, s, d), g1),
                                                    name="normmod_bwd")
        dyb = dyb.reshape(t, d)
        d_bias_out = jnp.sum(g1 * sdx, axis=(0, 1))
        if i % 2 == 0:
            qkv, mix = extra
            small["b_o", j] = d_bias_out
            reduce_scatter("o", j, n_attn,
                           _mm_tn(mix.reshape(t, d), dyb, name="dw_sq", comm=pending.take(SMALL)).reshape(N_CHIPS, -1, d))
            dmix = _mm(dyb, weights["o", j], nt=True, tn=d, name="mm_dmix", comm=pending.take(SMALL)).reshape(bsz, s, d)
            dqkv, d_bqkv, d_sink = _attn_bwd(qkv, dmix, sinks[j], name="attn_bwd", comm=pending.take(2 * LARGE))
            small["b_qkv", j], small["sinks", j] = d_bqkv[0], d_sink[:, 0]
            dqkv = dqkv.reshape(t, qkv_dim)
            dwq = _mm_tn(h1.reshape(t, d), dqkv, tn=qkv_dim, name="dw_qkv", comm=pending.take(SMALL))
            reduce_scatter("qkv", j, n_attn, jnp.transpose(dwq.reshape(d, N_CHIPS, -1), (1, 0, 2)))
            dh1 = _mm(dqkv, weights["qkv", j], nt=True, tn=d, name="mm_dh1a", comm=pending.take(SMALL))
        else:
            u, mix, conv_v = extra
            small["b_pw2", j] = d_bias_out
            reduce_scatter("pw2", j, n_conv,
                           _mm_tn(mix.reshape(t, d), dyb, name="dw_sq", comm=pending.take(SMALL)).reshape(N_CHIPS, -1, d))
            dz = _mm(dyb, weights["pw2", j], nt=True, out_dtype=F32, tn=d, name="mm_dz", comm=pending.take(SMALL)).reshape(bsz, s, d)
            dv, d_lng, d_lnb, d_bdw = _conv_bwd_ln(dz, conv_v, ln_g_f[j][None], ln_b_f[j][None], name="conv_bwd_ln")
            du, d_bpw1, d_wdw = _conv_bwd_taps(dv, u, w_dw_f[j], name="conv_bwd_taps", comm=pending.take(2 * LARGE))
            small["ln_g", j], small["ln_b", j], small["b_dw", j] = d_lng[0], d_lnb[0], d_bdw[0]
            small["b_pw1", j], small["w_dw", j] = d_bpw1[0], jnp.sum(d_wdw[:CONV_WIDTH], axis=1)
            du = du.reshape(t, 2 * d)
            reduce_scatter("pw1", j, n_conv,
                           _mm_tn(h1.reshape(t, d), du, col_shards=N_CHIPS, name="dw_pw1", comm=pending.take(MEDIUM)))
            dh1 = _mm(du, weights["pw1", j], nt=True, tn=d, name="mm_dh1c", comm=pending.take(MEDIUM))
        below = (saved[i - 1][7].reshape(bsz, s, d), mods[i - 1][5]) if i > 0 else None
        dx, p1, dsh1, *gate_grads = _normmod_bwd(xin, dh1.reshape(bsz, s, d), dx1, norm_mix[i][None], sc1, below,
                                                 name="normmod_bwd")
        small["norm_mix", i] = jnp.sum((1.0 + sc1) * p1, axis=(0, 1))
        small["norm_mlp", i] = jnp.sum((1.0 + sc2) * p2, axis=(0, 1))
        dmods[i] = jnp.concatenate([dsh1, norm_mix[i] * p1, dg1, dsh2, norm_mlp[i] * p2, dg2], axis=1)
        if i > 0:
            dyb, dg2, _ = gate_grads
    grad_x = dx

    small_names = ([("norm_mix", i) for i in range(depth)] + [("norm_mlp", i) for i in range(depth)]
                   + [(nm, j) for nm in ("b_qkv", "b_o", "sinks") for j in range(n_attn)]
                   + [(nm, j) for nm in ("b_pw1", "w_dw", "b_dw", "ln_g", "ln_b", "b_pw2") for j in range(n_conv)])
    small_list = [small[k] for k in small_names] + [d_final[0]]
    small_pack = _pack(small_list, d)
    dmod_rows = jnp.stack(dmods).reshape(depth * bsz * N_MOD, d)
    n_small = small_pack.shape[0]
    gathered = _all_gather_small(jnp.concatenate([small_pack, _pack([dmod_rows], d)], axis=0), name="gather_small",
                                 comm=pending.take(LARGE, True))
    gathered = gathered.reshape(N_DEV, -1, d)
    dmod_all = gathered[:, n_small:n_small + depth * bsz * N_MOD].reshape(N_DEV, depth, bsz, N_MOD, d)
    small_sum, g_b_mod = _sum_devices(gathered[:, :n_small], dmod_all, name="sum_devices")
    small_tot = dict(zip(small_names + ["final_norm"], _unpack(small_sum, [a.shape for a in small_list])))
    stacked = lambda nm, count: jnp.stack([small_tot[nm, k] for k in range(count)])
    g_norm_mix, g_norm_mlp = stacked("norm_mix", depth), stacked("norm_mlp", depth)
    g_b_qkv, g_b_o, g_sinks = stacked("b_qkv", n_attn), stacked("b_o", n_attn), stacked("sinks", n_attn)
    g_final = small_tot["final_norm"]
    g_b_mod = g_b_mod.reshape(depth, N_MOD * d)
    shard_cols = lambda g: lax.dynamic_slice_in_dim(g, mq * (g.shape[-1] // N_CHIPS), g.shape[-1] // N_CHIPS, axis=g.ndim - 1)
    g_b_pw1, g_w_dw, g_b_dw, g_ln_g, g_ln_b, g_b_pw2 = [shard_cols(stacked(nm, n_conv))
                                                        for nm in ("b_pw1", "w_dw", "b_dw", "ln_g", "ln_b", "b_pw2")]

    dmod_cols = jnp.transpose(dmod_all, (1, 0, 2, 3, 4)).reshape(depth, N_DEV * bsz, N_MOD * d)
    dmod_cols = lax.dynamic_slice_in_dim(dmod_cols, mq * n_mod, n_mod, axis=2)
    g_w_mod = _mod_bwd(c_all, dmod_cols, name="mod_bwd")

    def adam(w, g, m, v, name):
        two_d = lambda a: a.reshape(-1, a.shape[-1])
        return [r.reshape(w.shape) for r in _adamw(two_d(w), two_d(g), two_d(m), two_d(v), name=name)]

    results = {"w_mod": (g_w_mod,) + tuple(adam(w_mod, g_w_mod, m_w_mod, v_w_mod, "adamw"))}

    pending.flush(LARGE)
    names = ["qkv", "o", "pw1", "pw2", "up", "down"]
    bufs = [totals[nm] for nm in names]
    copies = []
    for a, buf in enumerate(bufs):
        for layer in range(buf.shape[0]):
            half = lambda ref, h, layer=layer: ref.at[pl.ds(layer, 1), pl.ds(h, 1)]
            copies.append((lambda ins, outs, pos, a=a, half=half: half(ins[a], pos[2]),
                           lambda ins, outs, pos, a=a, half=half: half(outs[a], pos[2]),
                           lambda ins, outs, pos, a=a, half=half: half(outs[a], 1 - pos[2]), "c"))
    shared = {}
    _exchange(_Plan("d2d", 0.0, bufs, [jax.ShapeDtypeStruct(b.shape, F32) for b in bufs], copies,
                    {a: a for a in range(len(bufs))}, lambda outs: shared.update(zip(names, outs))), name="rs_share")
    g_w_qkv, g_w_o, g_w_pw1, g_w_pw2, g_w_up, g_w_down = [
        shared[nm].reshape(shared[nm].shape[0], -1, shared[nm].shape[3]) for nm in names]

    for nm, w, g, m, v in (("w_qkv", w_qkv, g_w_qkv, m_w_qkv, v_w_qkv),
                           ("w_o", w_o, g_w_o, m_w_o, v_w_o), ("w_pw1", w_pw1, g_w_pw1, m_w_pw1, v_w_pw1),
                           ("w_pw2", w_pw2, g_w_pw2, m_w_pw2, v_w_pw2), ("w_up", w_up, g_w_up, m_w_up, v_w_up),
                           ("w_down", w_down, g_w_down, m_w_down, v_w_down)):
        results[nm] = (g,) + tuple(adam(w, g, m, v, "adamw"))
    small_w = dict(b_mod=(b_mod, g_b_mod, m_b_mod, v_b_mod), norm_mix=(norm_mix, g_norm_mix, m_norm_mix, v_norm_mix),
                   norm_mlp=(norm_mlp, g_norm_mlp, m_norm_mlp, v_norm_mlp), b_qkv=(b_qkv, g_b_qkv, m_b_qkv, v_b_qkv),
                   b_o=(b_o, g_b_o, m_b_o, v_b_o), sinks=(sinks, g_sinks, m_sinks, v_sinks),
                   b_pw1=(b_pw1, g_b_pw1, m_b_pw1, v_b_pw1), w_dw=(w_dw, g_w_dw, m_w_dw, v_w_dw),
                   b_dw=(b_dw, g_b_dw, m_b_dw, v_b_dw), conv_ln_g=(conv_ln_g, g_ln_g, m_conv_ln_g, v_conv_ln_g),
                   conv_ln_b=(conv_ln_b, g_ln_b, m_conv_ln_b, v_conv_ln_b), b_pw2=(b_pw2, g_b_pw2, m_b_pw2, v_b_pw2),
                   final_norm=(final_norm, g_final, m_final_norm, v_final_norm))
    names = list(small_w)
    deltas, new_ms, new_vs = _adamw_packed(*[[small_w[nm][k] for nm in names] for k in range(4)], d, name="adamw_small")
    for k, nm in enumerate(names):
        results[nm] = (small_w[nm][1], deltas[k], new_ms[k], new_vs[k])

    weight_order = ["w_mod", "b_mod", "norm_mix", "norm_mlp", "w_qkv", "b_qkv", "w_o", "b_o", "sinks", "w_pw1", "b_pw1",
                    "w_dw", "b_dw", "conv_ln_g", "conv_ln_b", "w_pw2", "b_pw2", "w_up", "w_down", "final_norm"]
    return (loss, grad_x, *[results[nm][0] for nm in weight_order], *[results[nm][1] for nm in weight_order],
            *[results[nm][2] for nm in weight_order], *[results[nm][3] for nm in weight_order])
```

```python
import functools

import jax
import jax.numpy as jnp
from jax import lax
from jax.experimental import pallas as pl
from jax.experimental.pallas import tpu as pltpu

F32, BF16 = jnp.float32, jnp.bfloat16
MESH = pl.DeviceIdType.MESH
N_CHIPS = 4
N_DEV = 8
LANES = 128
SUBLANES = 8
VMEM_LIMIT = 48 * 1024 * 1024

NORM_EPS = 1e-6
HEAD_DIM = 64
N_KV_HEADS = 2
WINDOW = 128
CONV_WIDTH = 31
CONV_HALO = 32
CONV_ROWS = 32
N_MOD = 6

ADAM_LR, ADAM_B1, ADAM_B2, ADAM_EPS, ADAM_WD, ADAM_STEP = 0.001, 0.9, 0.999, 1e-08, 0.01, 10

HBM_SPEC = pl.BlockSpec(memory_space=pltpu.HBM)
VMEM_SPEC = pl.BlockSpec(memory_space=pltpu.VMEM)
SMEM_SPEC = pl.BlockSpec(memory_space=pltpu.SMEM)


def _params(*sem):
    return pltpu.CompilerParams(dimension_semantics=sem or None, vmem_limit_bytes=VMEM_LIMIT)


def _row_tile(rows, width_bytes, target=2 << 20):
    t = rows
    while t % 2 == 0 and t > SUBLANES and t * width_bytes > target:
        t //= 2
    return t


CHIP_KINDS = ("x", "y", "xy")


def _position():
    x, y, c = lax.axis_index("x"), lax.axis_index("y"), lax.axis_index("c")
    return x, y, c, 2 * x + y


def _peer(kind, x, y, c):
    return {"c": (x, y, 1 - c), "x": (1 - x, y, c), "y": (x, 1 - y, c), "xy": (1 - x, 1 - y, c)}[kind]


def _peer_shard(kind, x, y):
    px, py, _ = _peer(kind, x, y, 0)
    return 2 * px + py


class _Plan:
    def __init__(self, link, cost, operands, out_shapes, copies, aliases=None, then=None):
        self.link, self.cost = link, cost
        self.operands, self.out_shapes, self.copies = list(operands), list(out_shapes), list(copies)
        self.aliases, self.then = dict(aliases or {}), then


def _merge(plans):
    operands, out_shapes, copies, aliases, thens = [], [], [], {}, []
    for p in plans:
        i0, o0 = len(operands), len(out_shapes)
        i1, o1 = i0 + len(p.operands), o0 + len(p.out_shapes)

        def shifted(f, i0=i0, i1=i1, o0=o0, o1=o1):
            return lambda ins, outs, pos: f(ins[i0:i1], outs[o0:o1], pos)

        copies += [(shifted(src), shifted(dst), shifted(land), kind) for src, dst, land, kind in p.copies]
        aliases.update({i0 + k: o0 + v for k, v in p.aliases.items()})
        operands += p.operands
        out_shapes += p.out_shapes
        thens.append((p.then, o0, o1))

    def then(outs):
        for f, o0, o1 in thens:
            if f is not None:
                f(outs[o0:o1])

    return _Plan("mixed", sum(p.cost for p in plans), operands, out_shapes, copies, aliases, then)


def _call(body, *, name, out_shape, operands, grid=(), in_specs=(), out_specs=(), scratch_shapes=(), sem=(), comm=None):
    single = not isinstance(out_shape, (tuple, list))
    out_shape = [out_shape] if single else list(out_shape)
    out_specs = [out_specs] if single else list(out_specs)
    if comm is None:
        res = pl.pallas_call(body, out_shape=out_shape, grid=grid, in_specs=list(in_specs), out_specs=out_specs,
                             scratch_shapes=list(scratch_shapes), name=name, compiler_params=_params(*sem))(*operands)
        return res[0] if single else res
    n_in, n_out, n_scr = len(operands), len(out_shape), len(scratch_shapes)
    c_in, c_out, n_cp = len(comm.operands), len(comm.out_shapes), len(comm.copies)

    def wrapped(*refs):
        ins, refs = refs[:n_in], refs[n_in:]
        cins, refs = refs[:c_in], refs[c_in:]
        outs, refs = refs[:n_out], refs[n_out:]
        couts, refs = refs[:c_out], refs[c_out:]
        scr, (send_sems, recv_sems) = refs[:n_scr], refs[n_scr:]

        def descriptors():
            pos = _position()
            sends, lands = [], []
            for k, (src, dst, landing, kind) in enumerate(comm.copies):
                common = dict(send_sem=send_sems.at[k], recv_sem=recv_sems.at[k],
                              device_id=_peer(kind, *pos[:3]), device_id_type=MESH)
                sends.append(pltpu.make_async_remote_copy(src_ref=src(cins, couts, pos), dst_ref=dst(cins, couts, pos), **common))
                lands.append(pltpu.make_async_remote_copy(src_ref=src(cins, couts, pos), dst_ref=landing(cins, couts, pos), **common))
            return sends, lands

        def start():
            for cp in descriptors()[0]:
                cp.start()

        def finish():
            sends, lands = descriptors()
            for cp in lands:
                cp.wait_recv()
            for cp in sends:
                cp.wait_send()

        if not grid:
            start()
            body(*ins, *outs, *scr)
            finish()
        else:
            ids = [pl.program_id(ax) for ax in range(len(grid))]
            first, last = ids[0] == 0, ids[0] == grid[0] - 1
            for ax in range(1, len(grid)):
                first, last = first & (ids[ax] == 0), last & (ids[ax] == grid[ax] - 1)
            pl.when(first)(start)
            body(*ins, *outs, *scr)
            pl.when(last)(finish)

    res = pl.pallas_call(wrapped, out_shape=out_shape + comm.out_shapes, grid=grid,
                         in_specs=list(in_specs) + [HBM_SPEC] * c_in, out_specs=out_specs + [HBM_SPEC] * c_out,
                         scratch_shapes=list(scratch_shapes) + [pltpu.SemaphoreType.DMA((n_cp,)), pltpu.SemaphoreType.DMA((n_cp,))],
                         input_output_aliases={n_in + k: n_out + v for k, v in comm.aliases.items()},
                         name=name, compiler_params=_params(*["arbitrary"] * len(grid)))(*operands, *comm.operands)
    if comm.then is not None:
        comm.then(res[n_out:])
    return res[0] if single else res[:n_out]


def _exchange(plan, *, name):
    _call(lambda: None, name=name, out_shape=[], operands=[], comm=plan)


def _mm(a, b, *, name, nt=False, tm=1024, tn=512, epi="plain", out_dtype=BF16,
        bias=None, act=None, resid=None, gate=None, seq=None, norm=None, comm=None):
    m, k = a.shape
    n = b.shape[0] if nt else b.shape[1]
    tm, tn = min(tm, m, seq or m), min(tn, n)
    assert m % tm == 0 and n % tn == 0
    dims = (((1,), (1,)), ((), ())) if nt else (((1,), (0,)), ((), ()))
    tile = pl.BlockSpec((tm, tn), lambda j, i: (i, j))
    operands = [a, b]
    in_specs = [pl.BlockSpec((tm, k), lambda j, i: (i, 0)),
                pl.BlockSpec((tn, k), lambda j, i: (j, 0)) if nt else pl.BlockSpec((k, tn), lambda j, i: (0, j))]
    if bias is not None:
        operands.append(bias.reshape(1, n))
        in_specs.append(pl.BlockSpec((1, tn), lambda j, i: (0, j)))
    if epi == "dact":
        operands.append(act)
        in_specs.append(tile)
    if epi == "resid":
        assert seq % tm == 0
        per_ex = pl.BlockSpec((None, 1, tn), lambda j, i: (i * tm // seq, 0, j))
        operands += [resid, gate]
        in_specs += [tile, per_ex]
        out_shape = (jax.ShapeDtypeStruct((m, n), BF16), jax.ShapeDtypeStruct((m, n), F32))
        out_specs = (tile, tile)
        if norm is not None:
            assert tn == n
            operands += list(norm)
            in_specs += [pl.BlockSpec((1, tn), lambda j, i: (0, j)), per_ex, per_ex]
            out_shape += (jax.ShapeDtypeStruct((m, n), BF16),)
            out_specs += (tile,)
    elif epi == "relu2":
        out_shape = (jax.ShapeDtypeStruct((m, n), BF16), jax.ShapeDtypeStruct((m, n), BF16))
        out_specs = (tile, tile)
    else:
        out_shape = jax.ShapeDtypeStruct((m, n), out_dtype)
        out_specs = tile

    def body(*refs):
        it = iter(refs)
        a_ref, b_ref = next(it), next(it)
        acc = lax.dot_general(a_ref[...], b_ref[...], dims, preferred_element_type=F32)
        if bias is not None:
            acc = acc + next(it)[...]
        if epi == "plain":
            next(it)[...] = acc.astype(out_dtype)
        elif epi == "relu2":
            r = jnp.maximum(acc, 0.0)
            next(it)[...] = (r * r).astype(BF16)
            next(it)[...] = (2.0 * r).astype(BF16)
        elif epi == "dact":
            slope_ref = next(it)
            next(it)[...] = (acc * slope_ref[...].astype(F32)).astype(out_dtype)
        else:
            resid_ref, gate_ref = next(it), next(it)
            norm_refs = [next(it) for _ in (norm or ())]
            y_ref, x_ref = next(it), next(it)
            y_ref[...] = acc.astype(BF16)
            x_new = resid_ref[...] + gate_ref[...] * acc
            x_ref[...] = x_new
            if norm is not None:
                g_ref, sc_ref, sh_ref = norm_refs
                r = lax.rsqrt(jnp.mean(x_new * x_new, axis=-1, keepdims=True) + NORM_EPS)
                next(it)[...] = (x_new * r * g_ref[...] * (1.0 + sc_ref[...]) + sh_ref[...]).astype(BF16)

    return _call(body, name=name, out_shape=out_shape, operands=operands, grid=(n // tn, m // tm), in_specs=in_specs,
                 out_specs=out_specs, sem=("parallel", "parallel"), comm=comm)


def _mm_tn(a, b, *, name, tm=1024, tn=1024, tk=2048, col_shards=None, comm=None):
    t, m = a.shape
    n = b.shape[1]
    tm, tk = min(tm, m), min(tk, t)
    if col_shards is None:
        tn = min(tn, n)
        out_shape = jax.ShapeDtypeStruct((m, n), F32)
        out_spec = pl.BlockSpec((tm, tn), lambda i, j, k: (i, j))
    else:
        per = n // col_shards
        tn = min(tn, per)
        assert per % tn == 0
        out_shape = jax.ShapeDtypeStruct((col_shards, m, per), F32)
        out_spec = pl.BlockSpec((None, tm, tn), lambda i, j, k: (j // (per // tn), i, j % (per // tn)))
    assert m % tm == 0 and n % tn == 0 and t % tk == 0

    def body(a_ref, b_ref, o_ref):
        @pl.when(pl.program_id(2) == 0)
        def _():
            o_ref[...] = jnp.zeros_like(o_ref)

        o_ref[...] += lax.dot_general(a_ref[...], b_ref[...], (((0,), (0,)), ((), ())),
                                      preferred_element_type=F32)

    return _call(body, name=name, out_shape=out_shape, operands=[a, b], grid=(m // tm, n // tn, t // tk),
                 in_specs=[pl.BlockSpec((tk, tm), lambda i, j, k: (k, i)), pl.BlockSpec((tk, tn), lambda i, j, k: (k, j))],
                 out_specs=out_spec, sem=("parallel", "parallel", "arbitrary"), comm=comm)


def _normmod(x, gamma, sc, sh, *, name):
    bsz, s, d = x.shape
    ts = _row_tile(s, d * 4, 2 << 20)

    def body(x_ref, g_ref, sc_ref, sh_ref, o_ref):
        xf = x_ref[...]
        r = lax.rsqrt(jnp.mean(xf * xf, axis=-1, keepdims=True) + NORM_EPS)
        o_ref[...] = (xf * r * g_ref[...] * (1.0 + sc_ref[...]) + sh_ref[...]).astype(BF16)

    row = pl.BlockSpec((None, ts, d), lambda b, i: (b, i, 0))
    per_ex = pl.BlockSpec((None, 1, d), lambda b, i: (b, 0, 0))
    return pl.pallas_call(body, out_shape=jax.ShapeDtypeStruct(x.shape, BF16), grid=(bsz, s // ts),
                          in_specs=[row, pl.BlockSpec((1, d), lambda b, i: (0, 0)), per_ex, per_ex],
                          out_specs=row, name=name, compiler_params=_params("parallel", "parallel"))(x, gamma, sc, sh)


def _gate_grads(dxv, y_ref, gate_ref, dy_ref, dg_ref, sdx_ref):
    dy_ref[...] = (gate_ref[...] * dxv).astype(BF16)

    @pl.when(pl.program_id(1) == 0)
    def _():
        dg_ref[...] = jnp.zeros_like(dg_ref)
        sdx_ref[...] = jnp.zeros_like(sdx_ref)

    dg_ref[...] += jnp.sum(dxv * y_ref[...].astype(F32), axis=0, keepdims=True)
    sdx_ref[...] += jnp.sum(dxv, axis=0, keepdims=True)


def _normmod_bwd(x, dh, dres, gamma, sc, producer=None, *, name):
    bsz, s, d = x.shape
    ts = _row_tile(s, d * 4, 2 << 20)

    def body(x_ref, dh_ref, dres_ref, g_ref, sc_ref, *rest):
        dx_ref, p_ref, dsh_ref = rest[-6:-3] if producer else rest
        xf = x_ref[...]
        r = lax.rsqrt(jnp.mean(xf * xf, axis=-1, keepdims=True) + NORM_EPS)
        xhat = xf * r
        dh_v = dh_ref[...].astype(F32)
        dxhat = dh_v * (g_ref[...] * (1.0 + sc_ref[...]))
        dxv = dres_ref[...] + r * (dxhat - xhat * jnp.mean(dxhat * xhat, axis=-1, keepdims=True))
        dx_ref[...] = dxv

        @pl.when(pl.program_id(1) == 0)
        def _():
            p_ref[...] = jnp.zeros_like(p_ref)
            dsh_ref[...] = jnp.zeros_like(dsh_ref)

        p_ref[...] += jnp.sum(dh_v * xhat, axis=0, keepdims=True)
        dsh_ref[...] += jnp.sum(dh_v, axis=0, keepdims=True)
        if producer:
            _gate_grads(dxv, rest[0], rest[1], *rest[-3:])

    row = pl.BlockSpec((None, ts, d), lambda b, i: (b, i, 0))
    per_ex = pl.BlockSpec((None, 1, d), lambda b, i: (b, 0, 0))
    vec = jax.ShapeDtypeStruct((bsz, 1, d), F32)
    gate_in, gate_out, gate_specs = (), (), ()
    if producer:
        gate_in, gate_out, gate_specs = (row, per_ex), (jax.ShapeDtypeStruct(x.shape, BF16), vec, vec), (row, per_ex, per_ex)
    return pl.pallas_call(body, out_shape=(jax.ShapeDtypeStruct(x.shape, F32), vec, vec) + gate_out, grid=(bsz, s // ts),
                          in_specs=[row, row, row, pl.BlockSpec((1, d), lambda b, i: (0, 0)), per_ex, *gate_in],
                          out_specs=(row, per_ex, per_ex) + gate_specs, name=name,
                          compiler_params=_params("parallel", "arbitrary"))(x, dh, dres, gamma, sc, *(producer or ()))


def _loss_head(x, target, gamma, producer, *, name):
    bsz, s, d = x.shape
    ts = _row_tile(s, d * 4, 2 << 20)

    def body(x_ref, t_ref, g_ref, y_ref, gate_ref, dx_ref, loss_ref, dg_ref, *gate_refs):
        xf = x_ref[...]
        r = lax.rsqrt(jnp.mean(xf * xf, axis=-1, keepdims=True) + NORM_EPS)
        xhat = xf * r
        err = xhat * g_ref[...] - t_ref[...]
        dy = err * (1.0 / d)
        dxhat = dy * g_ref[...]
        dxv = r * (dxhat - xhat * jnp.mean(dxhat * xhat, axis=-1, keepdims=True))
        dx_ref[...] = dxv

        @pl.when((pl.program_id(0) == 0) & (pl.program_id(1) == 0))
        def _():
            loss_ref[...] = jnp.zeros_like(loss_ref)
            dg_ref[...] = jnp.zeros_like(dg_ref)

        loss_ref[...] += jnp.sum(err * err, axis=0, keepdims=True)
        dg_ref[...] += jnp.sum(dy * xhat, axis=0, keepdims=True)
        _gate_grads(dxv, y_ref, gate_ref, *gate_refs)

    row = pl.BlockSpec((None, ts, d), lambda b, i: (b, i, 0))
    one = pl.BlockSpec((1, d), lambda b, i: (0, 0))
    per_ex = pl.BlockSpec((None, 1, d), lambda b, i: (b, 0, 0))
    vec, ex_vec = jax.ShapeDtypeStruct((1, d), F32), jax.ShapeDtypeStruct((bsz, 1, d), F32)
    return pl.pallas_call(body, out_shape=(jax.ShapeDtypeStruct(x.shape, F32), vec, vec,
                                           jax.ShapeDtypeStruct(x.shape, BF16), ex_vec, ex_vec), grid=(bsz, s // ts),
                          in_specs=[row, row, one, row, per_ex], out_specs=(row, one, one, row, per_ex, per_ex), name=name,
                          compiler_params=_params("arbitrary", "arbitrary"))(x, target, gamma, *producer)


def _alibi_slope(h, n_heads):
    return 2.0 ** (-8.0 * (h + 1) / n_heads)


def _attn_masks(first_block):
    qi = lax.broadcasted_iota(jnp.int32, (WINDOW, 2 * WINDOW), 0)
    ki = lax.broadcasted_iota(jnp.int32, (WINDOW, 2 * WINDOW), 1)
    dist = qi + WINDOW - ki
    first_key = jnp.where(first_block, WINDOW, 0)
    valid = (dist >= 0) & (dist < WINDOW) & (ki >= first_key)
    return dist.astype(F32), valid


def _dup_halves(span, kv, left):
    f = span.astype(F32)
    rolled = pltpu.roll(f, HEAD_DIM, axis=1)
    out = jnp.where(left, f, rolled) if kv == 0 else jnp.where(left, rolled, f)
    return out.astype(BF16)


def _attn_probs(s, h, n_heads, distf, valid, sink):
    s = s * (HEAD_DIM ** -0.5) - _alibi_slope(h, n_heads) * distf
    s = jnp.where(valid, s, -1e30)
    m = jnp.maximum(jnp.max(s, axis=-1, keepdims=True), sink)
    e = jnp.exp(s - m)
    e_sink = jnp.exp(sink - m)
    inv = 1.0 / (jnp.sum(e, axis=-1, keepdims=True) + e_sink)
    return e * inv, e_sink * inv


def _attn_specs(d, n_blocks, clamp):
    kcol = d // LANES
    cur = (lambda i: jnp.minimum(i, n_blocks - 1)) if clamp else (lambda i: i)
    prev = lambda i: jnp.maximum(cur(i) - 1, 0)
    kv = lambda col, blk: pl.BlockSpec((None, WINDOW, LANES), lambda b, i: (b, blk(i), col))
    return [pl.BlockSpec((None, WINDOW, d), lambda b, i: (b, cur(i), 0)),
            kv(kcol, prev), kv(kcol, cur), kv(kcol + 1, prev), kv(kcol + 1, cur)]


def _attn_fwd(qkv, sinks, *, name, comm=None):
    bsz, s, qkv_dim = qkv.shape
    d = qkv_dim - 2 * N_KV_HEADS * HEAD_DIM
    n_heads = d // HEAD_DIM
    group = n_heads // N_KV_HEADS
    pairs = group // 2
    n_blocks = s // WINDOW

    def body(q_ref, kp_ref, kc_ref, vp_ref, vc_ref, sink_ref, o_ref):
        left = lax.broadcasted_iota(jnp.int32, (1, LANES), 1) < HEAD_DIM
        distf, valid = _attn_masks(pl.program_id(1) == 0)
        kspan = jnp.concatenate([kp_ref[...], kc_ref[...]], axis=0)
        vspan = jnp.concatenate([vp_ref[...], vc_ref[...]], axis=0)
        for kv in range(N_KV_HEADS):
            kdup, vdup = _dup_halves(kspan, kv, left), _dup_halves(vspan, kv, left)
            res = []
            for par in range(2):
                keep = left if par == 0 else jnp.logical_not(left)
                cols = [pl.ds((kv * pairs + p) * LANES, LANES) for p in range(pairs)]
                lhs = jnp.concatenate([jnp.where(keep, q_ref[:, cl], jnp.zeros((), BF16)) for cl in cols], axis=0)
                sc = lax.dot_general(lhs, kdup, (((1,), (1,)), ((), ())), preferred_element_type=F32)
                probs = []
                for p in range(pairs):
                    h = kv * group + 2 * p + par
                    pr, _ = _attn_probs(sc[p * WINDOW:(p + 1) * WINDOW], h, n_heads, distf, valid, sink_ref[h])
                    probs.append(pr.astype(BF16))
                res.append(jnp.dot(jnp.concatenate(probs, axis=0), vdup, preferred_element_type=F32))
            for p in range(pairs):
                rows = slice(p * WINDOW, (p + 1) * WINDOW)
                o_ref[:, pl.ds((kv * pairs + p) * LANES, LANES)] = jnp.where(left, res[0][rows], res[1][rows]).astype(BF16)

    return _call(body, name=name, out_shape=jax.ShapeDtypeStruct((bsz, s, d), BF16), operands=[qkv] * 5 + [sinks],
                 grid=(bsz, n_blocks), in_specs=_attn_specs(d, n_blocks, False) + [SMEM_SPEC],
                 out_specs=pl.BlockSpec((None, WINDOW, d), lambda b, i: (b, i, 0)), sem=("parallel", "parallel"), comm=comm)


def _attn_bwd(qkv, do, sinks, *, name, comm=None):
    bsz, s, qkv_dim = qkv.shape
    d = qkv_dim - 2 * N_KV_HEADS * HEAD_DIM
    n_heads = d // HEAD_DIM
    group = n_heads // N_KV_HEADS
    pairs = group // 2
    n_blocks = s // WINDOW
    tn_dims = (((0,), (0,)), ((), ()))

    def body(q_ref, kp_ref, kc_ref, vp_ref, vc_ref, do_ref, sink_ref, dqkv_ref, colsum_ref, dsink_ref,
             dq_prev, dk_carry, dv_carry):
        b, i = pl.program_id(0), pl.program_id(1)
        left = lax.broadcasted_iota(jnp.int32, (1, LANES), 1) < HEAD_DIM

        @pl.when((b == 0) & (i == 0))
        def _():
            colsum_ref[...] = jnp.zeros_like(colsum_ref)
            dsink_ref[...] = jnp.zeros_like(dsink_ref)

        @pl.when(i == 0)
        def _():
            dqkv_ref[...] = jnp.zeros_like(dqkv_ref)
            dk_carry[...] = jnp.zeros_like(dk_carry)
            dv_carry[...] = jnp.zeros_like(dv_carry)

        @pl.when(i > 0)
        def _():
            dq_v = dq_prev[...]
            dqkv_ref[:, pl.ds(0, d)] = dq_v.astype(BF16)
            colsum_ref[:, pl.ds(0, d)] += jnp.sum(dq_v, axis=0, keepdims=True)

        @pl.when(i < n_blocks)
        def _():
            distf, valid = _attn_masks(i == 0)
            kspan = jnp.concatenate([kp_ref[...], kc_ref[...]], axis=0)
            vspan = jnp.concatenate([vp_ref[...], vc_ref[...]], axis=0)
            dk_blk, dv_blk = [], []
            for kv in range(N_KV_HEADS):
                kdup, vdup = _dup_halves(kspan, kv, left), _dup_halves(vspan, kv, left)
                dq_res, dk_sum, dv_sum = [], None, None
                for par in range(2):
                    keep = left if par == 0 else jnp.logical_not(left)
                    cols = [pl.ds((kv * pairs + p) * LANES, LANES) for p in range(pairs)]
                    zero = jnp.zeros((), BF16)
                    lhs = jnp.concatenate([jnp.where(keep, q_ref[:, cl], zero) for cl in cols], axis=0)
                    dol = jnp.concatenate([jnp.where(keep, do_ref[:, cl], zero) for cl in cols], axis=0)
                    sc = lax.dot_general(lhs, kdup, (((1,), (1,)), ((), ())), preferred_element_type=F32)
                    dp = lax.dot_general(dol, vdup, (((1,), (1,)), ((), ())), preferred_element_type=F32)
                    probs, dscores = [], []
                    for p in range(pairs):
                        h = kv * group + 2 * p + par
                        rows = slice(p * WINDOW, (p + 1) * WINDOW)
                        pr, p_sink = _attn_probs(sc[rows], h, n_heads, distf, valid, sink_ref[h])
                        delta = jnp.sum(pr * dp[rows], axis=-1, keepdims=True)
                        dscores.append((pr * (dp[rows] - delta) * (HEAD_DIM ** -0.5)).astype(BF16))
                        probs.append(pr.astype(BF16))
                        dsink_ref[pl.ds(h, 1), :] += jnp.zeros((1, LANES), F32) - jnp.sum(p_sink * delta)
                    ds_all = jnp.concatenate(dscores, axis=0)
                    p_all = jnp.concatenate(probs, axis=0)
                    dq_res.append(jnp.dot(ds_all, kdup, preferred_element_type=F32))
                    dk_par = lax.dot_general(ds_all, lhs, tn_dims, preferred_element_type=F32)
                    dv_par = lax.dot_general(p_all, dol, tn_dims, preferred_element_type=F32)
                    dk_sum = dk_par if dk_sum is None else dk_sum + dk_par
                    dv_sum = dv_par if dv_sum is None else dv_sum + dv_par
                for p in range(pairs):
                    rows = slice(p * WINDOW, (p + 1) * WINDOW)
                    dq_prev[:, pl.ds((kv * pairs + p) * LANES, LANES)] = jnp.where(left, dq_res[0][rows], dq_res[1][rows])
                dk_blk.append(dk_sum + pltpu.roll(dk_sum, HEAD_DIM, axis=1))
                dv_blk.append(dv_sum + pltpu.roll(dv_sum, HEAD_DIM, axis=1))
            dk_span = jnp.where(left, dk_blk[0], dk_blk[1])
            dv_span = jnp.where(left, dv_blk[0], dv_blk[1])
            dk_done = dk_carry[...] + dk_span[:WINDOW]
            dv_done = dv_carry[...] + dv_span[:WINDOW]
            dk_carry[...] = dk_span[WINDOW:]
            dv_carry[...] = dv_span[WINDOW:]

            @pl.when(i > 0)
            def _():
                dqkv_ref[:, pl.ds(d, LANES)] = dk_done.astype(BF16)
                dqkv_ref[:, pl.ds(d + LANES, LANES)] = dv_done.astype(BF16)
                colsum_ref[:, pl.ds(d, LANES)] += jnp.sum(dk_done, axis=0, keepdims=True)
                colsum_ref[:, pl.ds(d + LANES, LANES)] += jnp.sum(dv_done, axis=0, keepdims=True)

        @pl.when(i == n_blocks)
        def _():
            dk_done, dv_done = dk_carry[...], dv_carry[...]
            dqkv_ref[:, pl.ds(d, LANES)] = dk_done.astype(BF16)
            dqkv_ref[:, pl.ds(d + LANES, LANES)] = dv_done.astype(BF16)
            colsum_ref[:, pl.ds(d, LANES)] += jnp.sum(dk_done, axis=0, keepdims=True)
            colsum_ref[:, pl.ds(d + LANES, LANES)] += jnp.sum(dv_done, axis=0, keepdims=True)

    do_spec = pl.BlockSpec((None, WINDOW, d), lambda b, i: (b, jnp.minimum(i, n_blocks - 1), 0))
    out_shape = (jax.ShapeDtypeStruct((bsz, s, qkv_dim), BF16), jax.ShapeDtypeStruct((1, qkv_dim), F32),
                 jax.ShapeDtypeStruct((n_heads, LANES), F32))
    out_specs = (pl.BlockSpec((None, WINDOW, qkv_dim), lambda b, i: (b, jnp.maximum(i - 1, 0), 0)),
                 pl.BlockSpec((1, qkv_dim), lambda b, i: (0, 0)),
                 pl.BlockSpec((n_heads, LANES), lambda b, i: (0, 0)))
    return _call(body, name=name, out_shape=out_shape, operands=[qkv] * 5 + [do, sinks], grid=(bsz, n_blocks + 1),
                 in_specs=_attn_specs(d, n_blocks, True) + [do_spec, SMEM_SPEC], out_specs=out_specs,
                 scratch_shapes=[pltpu.VMEM((WINDOW, d), F32), pltpu.VMEM((WINDOW, LANES), F32), pltpu.VMEM((WINDOW, LANES), F32)],
                 sem=("arbitrary", "arbitrary"), comm=comm)


def _conv_tile(s):
    return min(256, s)


def _halo_specs(ts, width, s):
    per = ts // CONV_HALO
    prev = pl.BlockSpec((None, CONV_HALO, width), lambda b, i: (b, jnp.maximum(i * per - 1, 0), 0))
    nxt = pl.BlockSpec((None, CONV_HALO, width), lambda b, i: (b, jnp.minimum((i + 1) * per, s // CONV_HALO - 1), 0))
    cur = pl.BlockSpec((None, ts, width), lambda b, i: (b, i, 0))
    return prev, cur, nxt


def _glu(u, d):
    return u[:, :d] * jax.nn.sigmoid(u[:, d:])


def _store_shifted(shifted, value):
    rows = value.shape[0]
    shifted[0] = value
    for b in range(1, SUBLANES):
        shifted[b] = pltpu.roll(value, rows - b, axis=0)


def _at(shifted, base, offset, n_rows, lanes):
    return shifted[offset % SUBLANES, pl.ds(base + (offset - offset % SUBLANES), n_rows), lanes]


def _taps(w_ref, shifted, out_ref, ts, d, offset):
    for r0 in range(0, ts, CONV_ROWS):
        for l0 in range(0, d, LANES):
            lanes = pl.ds(l0, LANES)
            acc = jnp.zeros((CONV_ROWS, LANES), F32)
            for j in range(CONV_WIDTH):
                acc = acc + w_ref[pl.ds(j, 1), lanes] * _at(shifted, r0, offset(j), CONV_ROWS, lanes)
            out_ref[pl.ds(r0, CONV_ROWS), lanes] = acc


def _conv_fwd(u, w_dw, b_dw, ln_g, ln_b, *, name, comm=None):
    bsz, s, d2 = u.shape
    d = d2 // 2
    ts = _conv_tile(s)

    def body(up_ref, uc_ref, w_ref, bdw_ref, g_ref, b_ref, z_ref, v_ref, gbuf):
        halo = jnp.where(pl.program_id(1) > 0, _glu(up_ref[...], d), 0.0)
        _store_shifted(gbuf, jnp.concatenate([halo, _glu(uc_ref[...], d)], axis=0))
        _taps(w_ref, gbuf, v_ref, ts, d, lambda j: CONV_HALO - (CONV_WIDTH - 1) + j)
        v = v_ref[...] + bdw_ref[...]
        v_ref[...] = v
        mu = jnp.mean(v, axis=-1, keepdims=True)
        cen = v - mu
        rstd = lax.rsqrt(jnp.mean(cen * cen, axis=-1, keepdims=True) + NORM_EPS)
        ln = cen * rstd * g_ref[...] + b_ref[...]
        z_ref[...] = (ln * jax.nn.sigmoid(ln)).astype(BF16)

    prev, cur, _ = _halo_specs(ts, d2, s)
    one = pl.BlockSpec((1, d), lambda b, i: (0, 0))
    row = pl.BlockSpec((None, ts, d), lambda b, i: (b, i, 0))
    return _call(body, name=name, out_shape=(jax.ShapeDtypeStruct((bsz, s, d), BF16), jax.ShapeDtypeStruct((bsz, s, d), F32)),
                 operands=[u, u, w_dw, b_dw, ln_g, ln_b], grid=(bsz, s // ts),
                 in_specs=[prev, cur, pl.BlockSpec((CONV_HALO, d), lambda b, i: (0, 0)), one, one, one],
                 out_specs=(row, row), scratch_shapes=[pltpu.VMEM((SUBLANES, ts + CONV_HALO, d), F32)],
                 sem=("parallel", "parallel"), comm=comm)


def _conv_bwd_ln(dz, v, ln_g, ln_b, *, name):
    bsz, s, d = v.shape
    ts = _row_tile(s, d * 4, 2 << 20)

    def body(dz_ref, v_ref, g_ref, b_ref, dv_ref, dg_ref, db_ref, dbdw_ref):
        v_v = v_ref[...]
        mu = jnp.mean(v_v, axis=-1, keepdims=True)
        cen = v_v - mu
        rstd = lax.rsqrt(jnp.mean(cen * cen, axis=-1, keepdims=True) + NORM_EPS)
        vhat = cen * rstd
        ln = vhat * g_ref[...] + b_ref[...]
        sig = jax.nn.sigmoid(ln)
        dln = dz_ref[...] * (sig * (1.0 + ln * (1.0 - sig)))
        dvhat = dln * g_ref[...]
        dv = rstd * (dvhat - jnp.mean(dvhat, axis=-1, keepdims=True)
                     - vhat * jnp.mean(dvhat * vhat, axis=-1, keepdims=True))
        dv_ref[...] = dv

        @pl.when((pl.program_id(0) == 0) & (pl.program_id(1) == 0))
        def _():
            dg_ref[...] = jnp.zeros_like(dg_ref)
            db_ref[...] = jnp.zeros_like(db_ref)
            dbdw_ref[...] = jnp.zeros_like(dbdw_ref)

        dg_ref[...] += jnp.sum(dln * vhat, axis=0, keepdims=True)
        db_ref[...] += jnp.sum(dln, axis=0, keepdims=True)
        dbdw_ref[...] += jnp.sum(dv, axis=0, keepdims=True)

    row = pl.BlockSpec((None, ts, d), lambda b, i: (b, i, 0))
    one = pl.BlockSpec((1, d), lambda b, i: (0, 0))
    vec = jax.ShapeDtypeStruct((1, d), F32)
    return pl.pallas_call(body, out_shape=(jax.ShapeDtypeStruct(v.shape, F32), vec, vec, vec), grid=(bsz, s // ts),
                          in_specs=[row, row, one, one], out_specs=(row, one, one, one), name=name,
                          compiler_params=_params("arbitrary", "arbitrary"))(dz, v, ln_g, ln_b)


def _conv_bwd_taps(dv, u, w_dw, *, name, comm=None):
    bsz, s, d = dv.shape
    ts = _conv_tile(s)
    n_tiles = s // ts

    def body(dvc_ref, dvn_ref, uc_ref, w_ref, du_ref, dbu_ref, dw_ref, dvbuf, dglu, glu):
        i = pl.program_id(1)

        @pl.when((pl.program_id(0) == 0) & (i == 0))
        def _():
            dbu_ref[...] = jnp.zeros_like(dbu_ref)
            dw_ref[...] = jnp.zeros_like(dw_ref)

        u_v = uc_ref[...]
        a, sig = u_v[:, :d], jax.nn.sigmoid(u_v[:, d:])
        glu[...] = a * sig
        ahead = jnp.where(i < n_tiles - 1, dvn_ref[...], 0.0)
        _store_shifted(dvbuf, jnp.concatenate([dvc_ref[...], ahead], axis=0))
        _taps(w_ref, dvbuf, dglu, ts, d, lambda j: CONV_WIDTH - 1 - j)
        for l0 in range(0, d, LANES):
            lanes = pl.ds(l0, LANES)
            for j in range(CONV_WIDTH):
                acc = jnp.zeros((SUBLANES, LANES), F32)
                for r0 in range(0, ts, CONV_ROWS):
                    prod = glu[pl.ds(r0, CONV_ROWS), lanes] * _at(dvbuf, r0, CONV_WIDTH - 1 - j, CONV_ROWS, lanes)
                    for k in range(0, CONV_ROWS, SUBLANES):
                        acc = acc + prod[k:k + SUBLANES]
                dw_ref[j, :, lanes] += acc
        dg_v = dglu[...]
        da = dg_v * sig
        dgate = dg_v * a * sig * (1.0 - sig)
        du_ref[:, pl.ds(0, d)] = da.astype(BF16)
        du_ref[:, pl.ds(d, d)] = dgate.astype(BF16)
        dbu_ref[:, pl.ds(0, d)] += jnp.sum(da, axis=0, keepdims=True)
        dbu_ref[:, pl.ds(d, d)] += jnp.sum(dgate, axis=0, keepdims=True)

    _, dv_cur, dv_next = _halo_specs(ts, d, s)
    _, u_cur, _ = _halo_specs(ts, 2 * d, s)
    out_shape = (jax.ShapeDtypeStruct((bsz, s, 2 * d), BF16), jax.ShapeDtypeStruct((1, 2 * d), F32),
                 jax.ShapeDtypeStruct((CONV_HALO, SUBLANES, d), F32))
    out_specs = (pl.BlockSpec((None, ts, 2 * d), lambda b, i: (b, i, 0)), pl.BlockSpec((1, 2 * d), lambda b, i: (0, 0)),
                 pl.BlockSpec((CONV_HALO, SUBLANES, d), lambda b, i: (0, 0, 0)))
    return _call(body, name=name, out_shape=out_shape, operands=[dv, dv, u, w_dw], grid=(bsz, n_tiles),
                 in_specs=[dv_cur, dv_next, u_cur, pl.BlockSpec((CONV_HALO, d), lambda b, i: (0, 0))],
                 out_specs=out_specs,
                 scratch_shapes=[pltpu.VMEM((SUBLANES, ts + CONV_HALO, d), F32), pltpu.VMEM((ts, d), F32),
                                 pltpu.VMEM((ts, d), F32)],
                 sem=("arbitrary", "arbitrary"), comm=comm)


def _mod_fwd(c_all, w_mod, b_mod, *, name):
    n_layers, d, n = w_mod.shape
    rows = c_all.shape[0]

    def body(c_ref, w_ref, b_ref, o_ref):
        cv = c_ref[...]
        cs = (cv * jax.nn.sigmoid(cv)).astype(BF16)
        o_ref[...] = jnp.dot(cs, w_ref[...].astype(BF16), preferred_element_type=F32) + b_ref[...]

    return pl.pallas_call(body, out_shape=jax.ShapeDtypeStruct((n_layers, rows, n), F32), grid=(n_layers,),
                          in_specs=[pl.BlockSpec((rows, d), lambda l: (0, 0)), pl.BlockSpec((None, d, n), lambda l: (l, 0, 0)),
                                    pl.BlockSpec((None, 1, n), lambda l: (l, 0, 0))],
                          out_specs=pl.BlockSpec((None, rows, n), lambda l: (l, 0, 0)), name=name,
                          compiler_params=_params("parallel"))(c_all, w_mod, b_mod)


def _mod_bwd(c_all, dmod, *, name):
    n_layers, rows, n = dmod.shape
    d = c_all.shape[1]

    def body(c_ref, g_ref, o_ref):
        cv = c_ref[...]
        cs = (cv * jax.nn.sigmoid(cv)).astype(BF16)
        o_ref[...] = lax.dot_general(cs, g_ref[...].astype(BF16), (((0,), (0,)), ((), ())), preferred_element_type=F32)

    return pl.pallas_call(body, out_shape=jax.ShapeDtypeStruct((n_layers, d, n), F32), grid=(n_layers,),
                          in_specs=[pl.BlockSpec((rows, d), lambda l: (0, 0)), pl.BlockSpec((None, rows, n), lambda l: (l, 0, 0))],
                          out_specs=pl.BlockSpec((None, d, n), lambda l: (l, 0, 0)), name=name,
                          compiler_params=_params("parallel"))(c_all, dmod)


def _add_half(g, recv, my_c, *, name):
    p, _, r, cdim = g.shape
    tr = _row_tile(r, cdim * 4)

    def body(c_ref, g_ref, r_ref, o_ref):
        o_ref[...] = (g_ref[...] + r_ref[...]).astype(BF16)

    grid_spec = pltpu.PrefetchScalarGridSpec(
        num_scalar_prefetch=1, grid=(p, r // tr),
        in_specs=[pl.BlockSpec((None, None, tr, cdim), lambda q, i, c_ref: (q, c_ref[0], i, 0)),
                  pl.BlockSpec((None, None, tr, cdim), lambda q, i, c_ref: (q, 0, i, 0))],
        out_specs=pl.BlockSpec((None, tr, cdim), lambda q, i, c_ref: (q, i, 0)))
    return pl.pallas_call(body, out_shape=jax.ShapeDtypeStruct((p, r, cdim), BF16), grid_spec=grid_spec, name=name,
                          compiler_params=_params("parallel", "parallel"))(my_c, g, recv)


def _add_pieces(chip, recv, my_qc, stacked, layer, n_layers, *, name):
    _, r, cdim = chip.shape
    tr = _row_tile(r, cdim * 4)

    def body(qc_ref, own_ref, r0_ref, r1_ref, r2_ref, *rest):
        f32 = lambda ref: ref[...].astype(F32)
        rest[-1][...] = ((f32(own_ref) + f32(r0_ref)) + f32(r1_ref)) + f32(r2_ref)

    piece = lambda k: pl.BlockSpec((None, tr, cdim), lambda i, qc_ref: (k, i, 0))
    in_specs = [pl.BlockSpec((None, tr, cdim), lambda i, qc_ref: (qc_ref[0], i, 0)), piece(0), piece(1), piece(2)]
    operands = [my_qc, chip, recv, recv, recv]
    aliases = {}
    if stacked is not None:
        in_specs.append(pl.BlockSpec(memory_space=pl.ANY))
        operands.append(stacked)
        aliases = {len(operands) - 1: 0}
    grid_spec = pltpu.PrefetchScalarGridSpec(
        num_scalar_prefetch=1, grid=(r // tr,), in_specs=in_specs,
        out_specs=pl.BlockSpec((None, None, tr, cdim), lambda i, qc_ref: (layer, qc_ref[1], i, 0)))
    return pl.pallas_call(body, out_shape=jax.ShapeDtypeStruct((n_layers, 2, r, cdim), F32), grid_spec=grid_spec,
                          input_output_aliases=aliases, name=name, compiler_params=_params("parallel"))(*operands)


def _sum_devices(parts, dmod, *, name):
    def body(p_ref, m_ref, o_ref, b_ref):
        acc = p_ref[0]
        for k in range(1, N_DEV):
            acc = acc + p_ref[k]
        o_ref[...] = acc
        tot = None
        for k in range(N_DEV):
            for e in range(dmod.shape[2]):
                tot = m_ref[k, :, e] if tot is None else tot + m_ref[k, :, e]
        b_ref[...] = tot

    out_shape = (jax.ShapeDtypeStruct(parts.shape[1:], F32),
                 jax.ShapeDtypeStruct((dmod.shape[1],) + dmod.shape[3:], F32))
    return pl.pallas_call(body, out_shape=out_shape, in_specs=[VMEM_SPEC, VMEM_SPEC], out_specs=(VMEM_SPEC, VMEM_SPEC),
                          name=name, compiler_params=_params())(parts, dmod)


def _adamw(w, g, m, v, *, name, comm=None):
    r, cdim = w.shape
    tr = _row_tile(r, cdim * 4, 2 << 20)

    def body(w_ref, g_ref, m_ref, v_ref, d_ref, nm_ref, nv_ref):
        gv = g_ref[...]
        nm = ADAM_B1 * m_ref[...] + (1.0 - ADAM_B1) * gv
        nv = ADAM_B2 * v_ref[...] + (1.0 - ADAM_B2) * (gv * gv)
        m_hat = nm / (1.0 - ADAM_B1 ** ADAM_STEP)
        v_hat = nv / (1.0 - ADAM_B2 ** ADAM_STEP)
        d_ref[...] = -ADAM_LR * (m_hat / (jnp.sqrt(v_hat) + ADAM_EPS) + ADAM_WD * w_ref[...])
        nm_ref[...] = nm
        nv_ref[...] = nv

    row = pl.BlockSpec((tr, cdim), lambda i: (i, 0))
    shape = jax.ShapeDtypeStruct((r, cdim), F32)
    return _call(body, name=name, out_shape=(shape, shape, shape), operands=[w, g, m, v], grid=(r // tr,),
                 in_specs=[row] * 4, out_specs=(row, row, row), sem=("parallel",), comm=comm)


def _all_gather_small(block, *, name, comm=None):
    m_per, n = block.shape

    def body(x_ref, out_ref, send_sems, recv_sems, local_sem):
        x, y, c, _ = _position()
        me, sibling = (x, y, c), (x, y, 1 - c)
        chips = [_peer(k, x, y, c)[:2] for k in CHIP_KINDS]

        def rows(px, py, pc):
            return out_ref.at[pl.ds((4 * px + 2 * py + pc) * m_per, m_per), :]

        def copy(k, blk, to, src=None):
            return pltpu.make_async_remote_copy(src_ref=rows(*blk) if src is None else src, dst_ref=rows(*blk),
                                                send_sem=send_sems.at[k], recv_sem=recv_sems.at[k],
                                                device_id=to, device_id_type=MESH)

        mine = pltpu.make_async_copy(x_ref, rows(*me), local_sem)
        mine.start()
        first = [copy(0, me, sibling, src=x_ref)]
        first += [copy(1 + j, me, (*chip, c), src=x_ref) for j, chip in enumerate(chips)]
        for cp in first:
            cp.start()
        passed = [copy(4 + j, (*chip, c), sibling) for j, chip in enumerate(chips)]
        for j, chip in enumerate(chips):
            copy(1 + j, (*chip, c), me).wait_recv()
            passed[j].start()
        copy(0, sibling, me).wait_recv()
        for j, chip in enumerate(chips):
            copy(4 + j, (*chip, 1 - c), me).wait_recv()
        for cp in first + passed:
            cp.wait_send()
        mine.wait()

    return _call(body, name=name, out_shape=jax.ShapeDtypeStruct((N_DEV * m_per, n), block.dtype), operands=[block],
                 in_specs=[VMEM_SPEC], out_specs=VMEM_SPEC,
                 scratch_shapes=[pltpu.SemaphoreType.DMA((7,)), pltpu.SemaphoreType.DMA((7,)), pltpu.SemaphoreType.DMA],
                 comm=comm)


def _pack(arrays, width):
    flat = jnp.concatenate([a.reshape(-1).astype(F32) for a in arrays])
    rows = -(-flat.shape[0] // width)
    rows = -(-rows // SUBLANES) * SUBLANES
    return jnp.pad(flat, (0, rows * width - flat.shape[0])).reshape(rows, width)


def _unpack(packed, shapes):
    flat, out, off = packed.reshape(-1), [], 0
    for shp in shapes:
        size = 1
        for dim in shp:
            size *= dim
        out.append(flat[off:off + size].reshape(shp))
        off += size
    return out


def _adamw_packed(ws, gs, ms, vs, width, *, name):
    shapes = [w.shape for w in ws]
    res = _adamw(_pack(ws, width), _pack(gs, width), _pack(ms, width), _pack(vs, width), name=name)
    return [_unpack(r, shapes) for r in res]


MB = float(1 << 20)


def _nbytes(shape, dtype):
    size = jnp.dtype(dtype).itemsize
    for dim in shape:
        size *= dim
    return size


def _gather_plans(shard, on_ready):
    r, cdim = shard.shape
    half = r // 2
    buf = jax.ShapeDtypeStruct((N_CHIPS, r, cdim), shard.dtype)
    rows = lambda ref, q, h: ref.at[pl.ds(q, 1), pl.ds(h * half, half)]
    there = lambda kind, pos: _peer_shard(kind, pos[0], pos[1])
    cost = 3 * _nbytes((half, cdim), shard.dtype) / MB

    def forward(outs):
        copies = [(lambda ins, outs, pos, kind=kind: rows(ins[0], there(kind, pos), pos[2]),
                   lambda ins, outs, pos, kind=kind: rows(outs[0], there(kind, pos), pos[2]),
                   lambda ins, outs, pos, kind=kind: rows(outs[0], there(kind, pos), 1 - pos[2]), "c") for kind in CHIP_KINDS]
        return _Plan("d2d", cost, [outs[0]], [jax.ShapeDtypeStruct(buf.shape, buf.dtype)], copies, {0: 0},
                     lambda done: on_ready(done[0]))

    copies = [(lambda ins, outs, pos: ins[0].at[:, pl.ds(pos[2] * half, half)],
               lambda ins, outs, pos: rows(outs[0], pos[3], pos[2]),
               lambda ins, outs, pos, kind=kind: rows(outs[0], there(kind, pos), pos[2]), kind) for kind in CHIP_KINDS]
    own = lambda ins, outs, pos: outs[0].at[pl.ds(pos[3], 1)]
    copies.append((lambda ins, outs, pos: ins[0], own, own, "c"))
    return _Plan("ici", cost, [shard[None]], [buf], copies), forward


class _Exchanges:
    def __init__(self):
        self.queue = []

    def add(self, plan, front=False):
        if front:
            self.queue.insert(0, plan)
        else:
            self.queue.append(plan)

    def take(self, budget_mb, at_least_one=False):
        chosen, spent = [p for p in self.queue if p.link == "d2d"], 0.0
        for p in self.queue:
            if p.link == "ici" and (spent + p.cost <= budget_mb or (at_least_one and spent == 0.0)):
                chosen.append(p)
                spent += p.cost
        if not chosen:
            return None
        self.queue = [p for p in self.queue if all(p is not ch for ch in chosen)]
        return _merge(chosen)

    def flush(self, budget_mb, until=lambda: False):
        while self.queue and not until():
            _exchange(self.take(budget_mb, at_least_one=True), name="exchange")


def kernel(x, c, w_mod, b_mod, norm_mix, norm_mlp, w_qkv, b_qkv, w_o, b_o, sinks, w_pw1, b_pw1, w_dw, b_dw, conv_ln_g, conv_ln_b, w_pw2, b_pw2, w_up, w_down, final_norm, loss_target, m_w_mod, m_b_mod, m_norm_mix, m_norm_mlp, m_w_qkv, m_b_qkv, m_w_o, m_b_o, m_sinks, m_w_pw1, m_b_pw1, m_w_dw, m_b_dw, m_conv_ln_g, m_conv_ln_b, m_w_pw2, m_b_pw2, m_w_up, m_w_down, m_final_norm, v_w_mod, v_b_mod, v_norm_mix, v_norm_mlp, v_w_qkv, v_b_qkv, v_w_o, v_b_o, v_sinks, v_w_pw1, v_b_pw1, v_w_dw, v_b_dw, v_conv_ln_g, v_conv_ln_b, v_w_pw2, v_b_pw2, v_w_up, v_w_down, v_final_norm):
    bsz, s, d = x.shape
    t = bsz * s
    depth = w_mod.shape[0]
    n_attn, n_conv = w_qkv.shape[0], w_pw1.shape[0]
    qkv_dim = d + 2 * N_KV_HEADS * HEAD_DIM
    mx, my, mc, mq = _position()
    me = 4 * mx + 2 * my + mc
    my_c = jnp.reshape(mc, (1,)).astype(jnp.int32)
    my_qc = jnp.stack([mq, mc]).astype(jnp.int32)
    pending = _Exchanges()
    SMALL, MEDIUM, LARGE = 2.5, 3.5, 6.5

    weights = {}
    order = []
    for i in range(depth):
        j = i // 2
        order += ([(("qkv", j), w_qkv[j], True), (("o", j), w_o[j], False)] if i % 2 == 0 else
                  [(("pw1", j), w_pw1[j], True), (("pw2", j), w_pw2[j], False)])
        order += [(("up", i), w_up[i], True), (("down", i), w_down[i], False)]
    for key, shard, by_cols in order:
        def ready(buf, key=key, by_cols=by_cols):
            weights[key] = jnp.transpose(buf, (1, 0, 2)).reshape(buf.shape[1], -1) if by_cols else buf.reshape(-1, buf.shape[2])
        ici, forward = _gather_plans(shard.astype(BF16), ready)
        ici.then = lambda outs, forward=forward: pending.add(forward(outs), front=True)
        pending.add(ici)

    def weight(key):
        pending.flush(SMALL, until=lambda: key in weights)
        return weights[key]

    small_sharded = [b_pw1, w_dw, b_dw, conv_ln_g, conv_ln_b, b_pw2]
    c_pad = jnp.pad(c, ((0, SUBLANES - bsz), (0, 0)))
    gathered = _all_gather_small(jnp.concatenate([c_pad, _pack(small_sharded, d)], axis=0), name="gather_c",
                                 comm=pending.take(SMALL, True))
    gathered = gathered.reshape(N_DEV, -1, d)
    c_all = gathered[:, :bsz].reshape(N_DEV * bsz, d)
    per_chip = [_unpack(gathered[2 * q, SUBLANES:], [a.shape for a in small_sharded]) for q in range(N_CHIPS)]
    b_pw1_f, w_dw_f, b_dw_f, ln_g_f, ln_b_f, b_pw2_f = [jnp.concatenate([per_chip[q][k] for q in range(N_CHIPS)], axis=-1)
                                                           for k in range(len(small_sharded))]
    w_dw_f = jnp.pad(w_dw_f, ((0, 0), (0, CONV_HALO - CONV_WIDTH), (0, 0)))

    n_mod = w_mod.shape[2]
    b_mod_cols = lax.dynamic_slice_in_dim(b_mod, mq * n_mod, n_mod, axis=1).reshape(depth, 1, n_mod)
    c_all8 = gathered[:, :SUBLANES].reshape(N_DEV * SUBLANES, d)
    mod_part = _mod_fwd(c_all8, w_mod, b_mod_cols, name="mod_fwd")
    peer_rows = lambda kind, pos: pl.ds(SUBLANES * (2 * _peer_shard(kind, pos[0], pos[1]) + pos[2]), SUBLANES)
    got = {}
    pending.add(_Plan("ici", 0.01, [mod_part],[jax.ShapeDtypeStruct((len(CHIP_KINDS), depth, SUBLANES, n_mod), F32)],
                      [(lambda ins, outs, pos, kind=kind: ins[0].at[:, peer_rows(kind, pos)],
                        lambda ins, outs, pos, k=k: outs[0].at[k], lambda ins, outs, pos, k=k: outs[0].at[k], kind)
                       for k, kind in enumerate(CHIP_KINDS)], then=lambda outs: got.update(rows=outs[0])), front=True)
    _exchange(pending.take(0.0, at_least_one=True), name="mod_exchange")
    own = lax.dynamic_slice_in_dim(mod_part, me * SUBLANES, SUBLANES, axis=1)
    parts = [own, got["rows"][0], got["rows"][1], got["rows"][2]]
    part_of = (0, 2, 1, 3)

    def shard(sidx):
        dist, out = jnp.bitwise_xor(mq, sidx), parts[0]
        for distance in (1, 2, 3):
            out = jnp.where(dist == distance, parts[part_of[distance]], out)
        return out

    mod = jnp.stack([shard(sidx) for sidx in range(N_CHIPS)])[:, :, :bsz]
    mod = jnp.transpose(mod, (1, 2, 0, 3)).reshape(depth, bsz, N_MOD, 1, d)
    mods = [[mod[i][:, k] for k in range(N_MOD)] for i in range(depth)]

    saved = []
    xc = x
    h1 = _normmod(xc, norm_mix[0][None], mods[0][1], mods[0][0], name="normmod")
    for i in range(depth):
        j = i // 2
        sh1, sc1, g1, sh2, sc2, g2 = mods[i]
        mlp_norm = (norm_mlp[i][None], sc2, sh2)
        if i % 2 == 0:
            wq, wo = weight(("qkv", j)), weight(("o", j))
            qkv = _mm(h1.reshape(t, d), wq, bias=b_qkv[j], tn=qkv_dim, name="mm_qkv",
                      comm=pending.take(SMALL)).reshape(bsz, s, qkv_dim)
            mix = _attn_fwd(qkv, sinks[j], name="attn_fwd", comm=pending.take(LARGE, True))
            y1, x1, h2 = _mm(mix.reshape(t, d), wo, bias=b_o[j], epi="resid", resid=xc.reshape(t, d), gate=g1, seq=s,
                             norm=mlp_norm, tn=d, name="mm_out", comm=pending.take(SMALL))
            extra = (qkv, mix)
        else:
            wp1, wp2 = weight(("pw1", j)), weight(("pw2", j))
            u = _mm(h1.reshape(t, d), wp1, bias=b_pw1_f[j], out_dtype=F32, tn=d, name="mm_pw1",
                    comm=pending.take(MEDIUM)).reshape(bsz, s, 2 * d)
            mix, conv_v = _conv_fwd(u, w_dw_f[j], b_dw_f[j][None], ln_g_f[j][None], ln_b_f[j][None], name="conv_fwd",
                                    comm=pending.take(LARGE, True))
            y1, x1, h2 = _mm(mix.reshape(t, d), wp2, bias=b_pw2_f[j], epi="resid", resid=xc.reshape(t, d), gate=g1, seq=s,
                             norm=mlp_norm, tn=d, name="mm_out", comm=pending.take(SMALL))
            extra = (u, mix, conv_v)
        act, slope = _mm(h2, weight(("up", i)), epi="relu2", tm=1024, tn=2 * d, name="mm_up", comm=pending.take(LARGE, True))
        mix_norm = (norm_mix[i + 1][None], mods[i + 1][1], mods[i + 1][0]) if i + 1 < depth else None
        y2, x2, *h_next = _mm(act, weight(("down", i)), epi="resid", resid=x1, gate=g2, seq=s, norm=mix_norm,
                              tm=512, tn=d, name="mm_down", comm=pending.take(LARGE, True))
        saved.append((xc, h1, extra, y1, x1.reshape(bsz, s, d), h2, act, y2, slope))
        xc = x2.reshape(bsz, s, d)
        h1 = h_next[0] if h_next else None
    pending.flush(LARGE)

    last_y2, last_g2 = saved[-1][7].reshape(bsz, s, d), mods[-1][5]
    dx, loss_cols, d_final, dyb, dg2, _ = _loss_head(xc, loss_target, final_norm[None], (last_y2, last_g2), name="loss_head")
    loss = lax.psum(0.5 / d * jnp.sum(loss_cols), ("x", "y", "c"))

    totals = {}

    def reduce_scatter(name, layer, n_layers, grad):
        view = grad.reshape(N_CHIPS, 2, grad.shape[1] // 2, grad.shape[2])
        half_shape = (N_CHIPS, 1) + view.shape[2:]

        def scatter(outs):
            chip = _add_half(view, outs[0], my_c, name="rs_chipsum")
            copies = [(lambda ins, outs, pos, kind=kind: ins[0].at[pl.ds(_peer_shard(kind, pos[0], pos[1]), 1)],
                       lambda ins, outs, pos, k=k: outs[0].at[pl.ds(k, 1)],
                       lambda ins, outs, pos, k=k: outs[0].at[pl.ds(k, 1)], kind) for k, kind in enumerate(CHIP_KINDS)]

            def total(outs):
                totals[name] = _add_pieces(chip, outs[0], my_qc, totals.get(name), layer, n_layers, name="rs_total")

            pending.add(_Plan("ici", 3 * _nbytes(chip.shape[1:], BF16) / MB, [chip],
                              [jax.ShapeDtypeStruct((3,) + chip.shape[1:], BF16)], copies, then=total))

        pending.add(_Plan("d2d", _nbytes(half_shape, F32) / MB, [view], [jax.ShapeDtypeStruct(half_shape, F32)],
                          [(lambda ins, outs, pos: ins[0].at[:, pl.ds(1 - pos[2], 1)],
                            lambda ins, outs, pos: outs[0], lambda ins, outs, pos: outs[0], "c")], then=scatter))

    dmods, small = [None] * depth, {}
    for i in reversed(range(depth)):
        j = i // 2
        xin, h1, extra, y1, x1, h2, act, _, slope = saved[i]
        sh1, sc1, g1, sh2, sc2, g2 = mods[i]
        dyb = dyb.reshape(t, d)
        reduce_scatter("down", i, depth,
                       _mm_tn(act, dyb, name="dw_down", comm=pending.take(LARGE)).reshape(N_CHIPS, -1, d))
        dup = _mm(dyb, weights["down", i], nt=True, epi="dact", act=slope, tm=1024, tn=2 * d, name="mm_dact",
                  comm=pending.take(LARGE))
        reduce_scatter("up", i, depth,
                       _mm_tn(h2.reshape(t, d), dup, col_shards=N_CHIPS, name="dw_up", comm=pending.take(LARGE)))
        dh2 = _mm(dup, weights["up", i], nt=True, tm=512, tn=d, name="mm_dh2",
                  comm=pending.take(LARGE)).reshape(bsz, s, d)
        dx1, p2, dsh2, dyb, dg1, sdx = _normmod_bwd(x1, dh2, dx, norm_mlp[i][None], sc2, (y1.reshape(bsz, s, d), g1),
                                                    name="normmod_bwd")
        dyb = dyb.reshape(t, d)
        d_bias_out = jnp.sum(g1 * sdx, axis=(0, 1))
        if i % 2 == 0:
            qkv, mix = extra
            small["b_o", j] = d_bias_out
            reduce_scatter("o", j, n_attn,
                           _mm_tn(mix.reshape(t, d), dyb, name="dw_sq", comm=pending.take(SMALL)).reshape(N_CHIPS, -1, d))
            dmix = _mm(dyb, weights["o", j], nt=True, tn=d, name="mm_dmix", comm=pending.take(SMALL)).reshape(bsz, s, d)
            dqkv, d_bqkv, d_sink = _attn_bwd(qkv, dmix, sinks[j], name="attn_bwd", comm=pending.take(2 * LARGE))
            small["b_qkv", j], small["sinks", j] = d_bqkv[0], d_sink[:, 0]
            dqkv = dqkv.reshape(t, qkv_dim)
            dwq = _mm_tn(h1.reshape(t, d), dqkv, tn=qkv_dim, name="dw_qkv", comm=pending.take(SMALL))
            reduce_scatter("qkv", j, n_attn, jnp.transpose(dwq.reshape(d, N_CHIPS, -1), (1, 0, 2)))
            dh1 = _mm(dqkv, weights["qkv", j], nt=True, tn=d, name="mm_dh1a", comm=pending.take(SMALL))
        else:
            u, mix, conv_v = extra
            small["b_pw2", j] = d_bias_out
            reduce_scatter("pw2", j, n_conv,
                           _mm_tn(mix.reshape(t, d), dyb, name="dw_sq", comm=pending.take(SMALL)).reshape(N_CHIPS, -1, d))
            dz = _mm(dyb, weights["pw2", j], nt=True, out_dtype=F32, tn=d, name="mm_dz", comm=pending.take(SMALL)).reshape(bsz, s, d)
            dv, d_lng, d_lnb, d_bdw = _conv_bwd_ln(dz, conv_v, ln_g_f[j][None], ln_b_f[j][None], name="conv_bwd_ln")
            du, d_bpw1, d_wdw = _conv_bwd_taps(dv, u, w_dw_f[j], name="conv_bwd_taps", comm=pending.take(2 * LARGE))
            small["ln_g", j], small["ln_b", j], small["b_dw", j] = d_lng[0], d_lnb[0], d_bdw[0]
            small["b_pw1", j], small["w_dw", j] = d_bpw1[0], jnp.sum(d_wdw[:CONV_WIDTH], axis=1)
            du = du.reshape(t, 2 * d)
            reduce_scatter("pw1", j, n_conv,
                           _mm_tn(h1.reshape(t, d), du, col_shards=N_CHIPS, name="dw_pw1", comm=pending.take(MEDIUM)))
            dh1 = _mm(du, weights["pw1", j], nt=True, tn=d, name="mm_dh1c", comm=pending.take(MEDIUM))
        below = (saved[i - 1][7].reshape(bsz, s, d), mods[i - 1][5]) if i > 0 else None
        dx, p1, dsh1, *gate_grads = _normmod_bwd(xin, dh1.reshape(bsz, s, d), dx1, norm_mix[i][None], sc1, below,
                                                 name="normmod_bwd")
        small["norm_mix", i] = jnp.sum((1.0 + sc1) * p1, axis=(0, 1))
        small["norm_mlp", i] = jnp.sum((1.0 + sc2) * p2, axis=(0, 1))
        dmods[i] = jnp.concatenate([dsh1, norm_mix[i] * p1, dg1, dsh2, norm_mlp[i] * p2, dg2], axis=1)
        if i > 0:
            dyb, dg2, _ = gate_grads
    grad_x = dx

    small_names = ([("norm_mix", i) for i in range(depth)] + [("norm_mlp", i) for i in range(depth)]
                   + [(nm, j) for nm in ("b_qkv", "b_o", "sinks") for j in range(n_attn)]
                   + [(nm, j) for nm in ("b_pw1", "w_dw", "b_dw", "ln_g", "ln_b", "b_pw2") for j in range(n_conv)])
    small_list = [small[k] for k in small_names] + [d_final[0]]
    small_pack = _pack(small_list, d)
    dmod_rows = jnp.stack(dmods).reshape(depth * bsz * N_MOD, d)
    n_small = small_pack.shape[0]
    gathered = _all_gather_small(jnp.concatenate([small_pack, _pack([dmod_rows], d)], axis=0), name="gather_small",
                                 comm=pending.take(LARGE, True))
    gathered = gathered.reshape(N_DEV, -1, d)
    dmod_all = gathered[:, n_small:n_small + depth * bsz * N_MOD].reshape(N_DEV, depth, bsz, N_MOD, d)
    small_sum, g_b_mod = _sum_devices(gathered[:, :n_small], dmod_all, name="sum_devices")
    small_tot = dict(zip(small_names + ["final_norm"], _unpack(small_sum, [a.shape for a in small_list])))
    stacked = lambda nm, count: jnp.stack([small_tot[nm, k] for k in range(count)])
    g_norm_mix, g_norm_mlp = stacked("norm_mix", depth), stacked("norm_mlp", depth)
    g_b_qkv, g_b_o, g_sinks = stacked("b_qkv", n_attn), stacked("b_o", n_attn), stacked("sinks", n_attn)
    g_final = small_tot["final_norm"]
    g_b_mod = g_b_mod.reshape(depth, N_MOD * d)
    shard_cols = lambda g: lax.dynamic_slice_in_dim(g, mq * (g.shape[-1] // N_CHIPS), g.shape[-1] // N_CHIPS, axis=g.ndim - 1)
    g_b_pw1, g_w_dw, g_b_dw, g_ln_g, g_ln_b, g_b_pw2 = [shard_cols(stacked(nm, n_conv))
                                                        for nm in ("b_pw1", "w_dw", "b_dw", "ln_g", "ln_b", "b_pw2")]

    dmod_cols = jnp.transpose(dmod_all, (1, 0, 2, 3, 4)).reshape(depth, N_DEV * bsz, N_MOD * d)
    dmod_cols = lax.dynamic_slice_in_dim(dmod_cols, mq * n_mod, n_mod, axis=2)
    g_w_mod = _mod_bwd(c_all, dmod_cols, name="mod_bwd")

    def adam(w, g, m, v, name):
        two_d = lambda a: a.reshape(-1, a.shape[-1])
        return [r.reshape(w.shape) for r in _adamw(two_d(w), two_d(g), two_d(m), two_d(v), name=name)]

    results = {"w_mod": (g_w_mod,) + tuple(adam(w_mod, g_w_mod, m_w_mod, v_w_mod, "adamw"))}

    pending.flush(LARGE)
    names = ["qkv", "o", "pw1", "pw2", "up", "down"]
    bufs = [totals[nm] for nm in names]
    copies = []
    for a, buf in enumerate(bufs):
        for layer in range(buf.shape[0]):
            half = lambda ref, h, layer=layer: ref.at[pl.ds(layer, 1), pl.ds(h, 1)]
            copies.append((lambda ins, outs, pos, a=a, half=half: half(ins[a], pos[2]),
                           lambda ins, outs, pos, a=a, half=half: half(outs[a], pos[2]),
                           lambda ins, outs, pos, a=a, half=half: half(outs[a], 1 - pos[2]), "c"))
    shared = {}
    _exchange(_Plan("d2d", 0.0, bufs, [jax.ShapeDtypeStruct(b.shape, F32) for b in bufs], copies,
                    {a: a for a in range(len(bufs))}, lambda outs: shared.update(zip(names, outs))), name="rs_share")
    g_w_qkv, g_w_o, g_w_pw1, g_w_pw2, g_w_up, g_w_down = [
        shared[nm].reshape(shared[nm].shape[0], -1, shared[nm].shape[3]) for nm in names]

    for nm, w, g, m, v in (("w_qkv", w_qkv, g_w_qkv, m_w_qkv, v_w_qkv),
                           ("w_o", w_o, g_w_o, m_w_o, v_w_o), ("w_pw1", w_pw1, g_w_pw1, m_w_pw1, v_w_pw1),
                           ("w_pw2", w_pw2, g_w_pw2, m_w_pw2, v_w_pw2), ("w_up", w_up, g_w_up, m_w_up, v_w_up),
                           ("w_down", w_down, g_w_down, m_w_down, v_w_down)):
        results[nm] = (g,) + tuple(adam(w, g, m, v, "adamw"))
    small_w = dict(b_mod=(b_mod, g_b_mod, m_b_mod, v_b_mod), norm_mix=(norm_mix, g_norm_mix, m_norm_mix, v_norm_mix),
                   norm_mlp=(norm_mlp, g_norm_mlp, m_norm_mlp, v_norm_mlp), b_qkv=(b_qkv, g_b_qkv, m_b_qkv, v_b_qkv),
                   b_o=(b_o, g_b_o, m_b_o, v_b_o), sinks=(sinks, g_sinks, m_sinks, v_sinks),
                   b_pw1=(b_pw1, g_b_pw1, m_b_pw1, v_b_pw1), w_dw=(w_dw, g_w_dw, m_w_dw, v_w_dw),
                   b_dw=(b_dw, g_b_dw, m_b_dw, v_b_dw), conv_ln_g=(conv_ln_g, g_ln_g, m_conv_ln_g, v_conv_ln_g),
                   conv_ln_b=(conv_ln_b, g_ln_b, m_conv_ln_b, v_conv_ln_b), b_pw2=(b_pw2, g_b_pw2, m_b_pw2, v_b_pw2),
                   final_norm=(final_norm, g_final, m_final_norm, v_final_norm))
    names = list(small_w)
    deltas, new_ms, new_vs = _adamw_packed(*[[small_w[nm][k] for nm in names] for k in range(4)], d, name="adamw_small")
    for k, nm in enumerate(names):
        results[nm] = (small_w[nm][1], deltas[k], new_ms[k], new_vs[k])

    weight_order = ["w_mod", "b_mod", "norm_mix", "norm_mlp", "w_qkv", "b_qkv", "w_o", "b_o", "sinks", "w_pw1", "b_pw1",
                    "w_dw", "b_dw", "conv_ln_g", "conv_ln_b", "w_pw2", "b_pw2", "w_up", "w_down", "final_norm"]
    return (loss, grad_x, *[results[nm][0] for nm in weight_order], *[results[nm][1] for nm in weight_order],
            *[results[nm][2] for nm in weight_order], *[results[nm][3] for nm in weight_order])
```

```python
import functools

import jax
import jax.numpy as jnp
from jax import lax
from jax.experimental import pallas as pl
from jax.experimental.pallas import tpu as pltpu

F32, BF16 = jnp.float32, jnp.bfloat16
MESH = pl.DeviceIdType.MESH
N_CHIPS = 4
N_DEV = 8
LANES = 128
SUBLANES = 8
VMEM_LIMIT = 48 * 1024 * 1024

NORM_EPS = 1e-6
HEAD_DIM = 64
N_KV_HEADS = 2
WINDOW = 128
CONV_WIDTH = 31
CONV_HALO = 32
CONV_ROWS = 32
N_MOD = 6

ADAM_LR, ADAM_B1, ADAM_B2, ADAM_EPS, ADAM_WD, ADAM_STEP = 0.001, 0.9, 0.999, 1e-08, 0.01, 10

HBM_SPEC = pl.BlockSpec(memory_space=pltpu.HBM)
VMEM_SPEC = pl.BlockSpec(memory_space=pltpu.VMEM)
SMEM_SPEC = pl.BlockSpec(memory_space=pltpu.SMEM)


def _params(*sem):
    return pltpu.CompilerParams(dimension_semantics=sem or None, vmem_limit_bytes=VMEM_LIMIT)


def _row_tile(rows, width_bytes, target=2 << 20):
    t = rows
    while t % 2 == 0 and t > SUBLANES and t * width_bytes > target:
        t //= 2
    return t


CHIP_KINDS = ("x", "y", "xy")


def _position():
    x, y, c = lax.axis_index("x"), lax.axis_index("y"), lax.axis_index("c")
    return x, y, c, 2 * x + y


def _peer(kind, x, y, c):
    return {"c": (x, y, 1 - c), "x": (1 - x, y, c), "y": (x, 1 - y, c), "xy": (1 - x, 1 - y, c)}[kind]


def _peer_shard(kind, x, y):
    px, py, _ = _peer(kind, x, y, 0)
    return 2 * px + py


class _Plan:
    def __init__(self, link, cost, operands, out_shapes, copies, aliases=None, then=None):
        self.link, self.cost = link, cost
        self.operands, self.out_shapes, self.copies = list(operands), list(out_shapes), list(copies)
        self.aliases, self.then = dict(aliases or {}), then


def _merge(plans):
    operands, out_shapes, copies, aliases, thens = [], [], [], {}, []
    for p in plans:
        i0, o0 = len(operands), len(out_shapes)
        i1, o1 = i0 + len(p.operands), o0 + len(p.out_shapes)

        def shifted(f, i0=i0, i1=i1, o0=o0, o1=o1):
            return lambda ins, outs, pos: f(ins[i0:i1], outs[o0:o1], pos)

        copies += [(shifted(src), shifted(dst), shifted(land), kind) for src, dst, land, kind in p.copies]
        aliases.update({i0 + k: o0 + v for k, v in p.aliases.items()})
        operands += p.operands
        out_shapes += p.out_shapes
        thens.append((p.then, o0, o1))

    def then(outs):
        for f, o0, o1 in thens:
            if f is not None:
                f(outs[o0:o1])

    return _Plan("mixed", sum(p.cost for p in plans), operands, out_shapes, copies, aliases, then)


def _call(body, *, name, out_shape, operands, grid=(), in_specs=(), out_specs=(), scratch_shapes=(), sem=(), comm=None):
    single = not isinstance(out_shape, (tuple, list))
    out_shape = [out_shape] if single else list(out_shape)
    out_specs = [out_specs] if single else list(out_specs)
    if comm is None:
        res = pl.pallas_call(body, out_shape=out_shape, grid=grid, in_specs=list(in_specs), out_specs=out_specs,
                             scratch_shapes=list(scratch_shapes), name=name, compiler_params=_params(*sem))(*operands)
        return res[0] if single else res
    n_in, n_out, n_scr = len(operands), len(out_shape), len(scratch_shapes)
    c_in, c_out, n_cp = len(comm.operands), len(comm.out_shapes), len(comm.copies)

    def wrapped(*refs):
        ins, refs = refs[:n_in], refs[n_in:]
        cins, refs = refs[:c_in], refs[c_in:]
        outs, refs = refs[:n_out], refs[n_out:]
        couts, refs = refs[:c_out], refs[c_out:]
        scr, (send_sems, recv_sems) = refs[:n_scr], refs[n_scr:]

        def descriptors():
            pos = _position()
            sends, lands = [], []
            for k, (src, dst, landing, kind) in enumerate(comm.copies):
                common = dict(send_sem=send_sems.at[k], recv_sem=recv_sems.at[k],
                              device_id=_peer(kind, *pos[:3]), device_id_type=MESH)
                sends.append(pltpu.make_async_remote_copy(src_ref=src(cins, couts, pos), dst_ref=dst(cins, couts, pos), **common))
                lands.append(pltpu.make_async_remote_copy(src_ref=src(cins, couts, pos), dst_ref=landing(cins, couts, pos), **common))
            return sends, lands

        def start():
            for cp in descriptors()[0]:
                cp.start()

        def finish():
            sends, lands = descriptors()
            for cp in lands:
                cp.wait_recv()
            for cp in sends:
                cp.wait_send()

        if not grid:
            start()
            body(*ins, *outs, *scr)
            finish()
        else:
            ids = [pl.program_id(ax) for ax in range(len(grid))]
            first, last = ids[0] == 0, ids[0] == grid[0] - 1
            for ax in range(1, len(grid)):
                first, last = first & (ids[ax] == 0), last & (ids[ax] == grid[ax] - 1)
            pl.when(first)(start)
            body(*ins, *outs, *scr)
            pl.when(last)(finish)

    res = pl.pallas_call(wrapped, out_shape=out_shape + comm.out_shapes, grid=grid,
                         in_specs=list(in_specs) + [HBM_SPEC] * c_in, out_specs=out_specs + [HBM_SPEC] * c_out,
                         scratch_shapes=list(scratch_shapes) + [pltpu.SemaphoreType.DMA((n_cp,)), pltpu.SemaphoreType.DMA((n_cp,))],
                         input_output_aliases={n_in + k: n_out + v for k, v in comm.aliases.items()},
                         name=name, compiler_params=_params(*["arbitrary"] * len(grid)))(*operands, *comm.operands)
    if comm.then is not None:
        comm.then(res[n_out:])
    return res[0] if single else res[:n_out]


def _exchange(plan, *, name):
    _call(lambda: None, name=name, out_shape=[], operands=[], comm=plan)


def _mm(a, b, *, name, nt=False, tm=1024, tn=512, epi="plain", out_dtype=BF16,
        bias=None, act=None, resid=None, gate=None, seq=None, norm=None, comm=None):
    m, k = a.shape
    n = b.shape[0] if nt else b.shape[1]
    tm, tn = min(tm, m, seq or m), min(tn, n)
    assert m % tm == 0 and n % tn == 0
    dims = (((1,), (1,)), ((), ())) if nt else (((1,), (0,)), ((), ()))
    tile = pl.BlockSpec((tm, tn), lambda j, i: (i, j))
    operands = [a, b]
    in_specs = [pl.BlockSpec((tm, k), lambda j, i: (i, 0)),
                pl.BlockSpec((tn, k), lambda j, i: (j, 0)) if nt else pl.BlockSpec((k, tn), lambda j, i: (0, j))]
    if bias is not None:
        operands.append(bias.reshape(1, n))
        in_specs.append(pl.BlockSpec((1, tn), lambda j, i: (0, j)))
    if epi == "dact":
        operands.append(act)
        in_specs.append(tile)
    if epi == "resid":
        assert seq % tm == 0
        per_ex = pl.BlockSpec((None, 1, tn), lambda j, i: (i * tm // seq, 0, j))
        operands += [resid, gate]
        in_specs += [tile, per_ex]
        out_shape = (jax.ShapeDtypeStruct((m, n), BF16), jax.ShapeDtypeStruct((m, n), F32))
        out_specs = (tile, tile)
        if norm is not None:
            assert tn == n
            operands += list(norm)
            in_specs += [pl.BlockSpec((1, tn), lambda j, i: (0, j)), per_ex, per_ex]
            out_shape += (jax.ShapeDtypeStruct((m, n), BF16),)
            out_specs += (tile,)
    elif epi == "relu2":
        out_shape = (jax.ShapeDtypeStruct((m, n), BF16), jax.ShapeDtypeStruct((m, n), BF16))
        out_specs = (tile, tile)
    else:
        out_shape = jax.ShapeDtypeStruct((m, n), out_dtype)
        out_specs = tile

    def body(*refs):
        it = iter(refs)
        a_ref, b_ref = next(it), next(it)
        acc = lax.dot_general(a_ref[...], b_ref[...], dims, preferred_element_type=F32)
        if bias is not None:
            acc = acc + next(it)[...]
        if epi == "plain":
            next(it)[...] = acc.astype(out_dtype)
        elif epi == "relu2":
            r = jnp.maximum(acc, 0.0)
            next(it)[...] = (r * r).astype(BF16)
            next(it)[...] = (2.0 * r).astype(BF16)
        elif epi == "dact":
            slope_ref = next(it)
            next(it)[...] = (acc * slope_ref[...].astype(F32)).astype(out_dtype)
        else:
            resid_ref, gate_ref = next(it), next(it)
            norm_refs = [next(it) for _ in (norm or ())]
            y_ref, x_ref = next(it), next(it)
            y_ref[...] = acc.astype(BF16)
            x_new = resid_ref[...] + gate_ref[...] * acc
            x_ref[...] = x_new
            if norm is not None:
                g_ref, sc_ref, sh_ref = norm_refs
                r = lax.rsqrt(jnp.mean(x_new * x_new, axis=-1, keepdims=True) + NORM_EPS)
                next(it)[...] = (x_new * r * g_ref[...] * (1.0 + sc_ref[...]) + sh_ref[...]).astype(BF16)

    return _call(body, name=name, out_shape=out_shape, operands=operands, grid=(n // tn, m // tm), in_specs=in_specs,
                 out_specs=out_specs, sem=("parallel", "parallel"), comm=comm)


def _mm_tn(a, b, *, name, tm=1024, tn=1024, tk=2048, col_shards=None, comm=None):
    t, m = a.shape
    n = b.shape[1]
    tm, tk = min(tm, m), min(tk, t)
    if col_shards is None:
        tn = min(tn, n)
        out_shape = jax.ShapeDtypeStruct((m, n), F32)
        out_spec = pl.BlockSpec((tm, tn), lambda i, j, k: (i, j))
    else:
        per = n // col_shards
        tn = min(tn, per)
        assert per % tn == 0
        out_shape = jax.ShapeDtypeStruct((col_shards, m, per), F32)
        out_spec = pl.BlockSpec((None, tm, tn), lambda i, j, k: (j // (per // tn), i, j % (per // tn)))
    assert m % tm == 0 and n % tn == 0 and t % tk == 0

    def body(a_ref, b_ref, o_ref):
        @pl.when(pl.program_id(2) == 0)
        def _():
            o_ref[...] = jnp.zeros_like(o_ref)

        o_ref[...] += lax.dot_general(a_ref[...], b_ref[...], (((0,), (0,)), ((), ())),
                                      preferred_element_type=F32)

    return _call(body, name=name, out_shape=out_shape, operands=[a, b], grid=(m // tm, n // tn, t // tk),
                 in_specs=[pl.BlockSpec((tk, tm), lambda i, j, k: (k, i)), pl.BlockSpec((tk, tn), lambda i, j, k: (k, j))],
                 out_specs=out_spec, sem=("parallel", "parallel", "arbitrary"), comm=comm)


def _normmod(x, gamma, sc, sh, *, name):
    bsz, s, d = x.shape
    ts = _row_tile(s, d * 4, 2 << 20)

    def body(x_ref, g_ref, sc_ref, sh_ref, o_ref):
        xf = x_ref[...]
        r = lax.rsqrt(jnp.mean(xf * xf, axis=-1, keepdims=True) + NORM_EPS)
        o_ref[...] = (xf * r * g_ref[...] * (1.0 + sc_ref[...]) + sh_ref[...]).astype(BF16)

    row = pl.BlockSpec((None, ts, d), lambda b, i: (b, i, 0))
    per_ex = pl.BlockSpec((None, 1, d), lambda b, i: (b, 0, 0))
    return pl.pallas_call(body, out_shape=jax.ShapeDtypeStruct(x.shape, BF16), grid=(bsz, s // ts),
                          in_specs=[row, pl.BlockSpec((1, d), lambda b, i: (0, 0)), per_ex, per_ex],
                          out_specs=row, name=name, compiler_params=_params("parallel", "parallel"))(x, gamma, sc, sh)


def _gate_grads(dxv, y_ref, gate_ref, dy_ref, dg_ref, sdx_ref):
    dy_ref[...] = (gate_ref[...] * dxv).astype(BF16)

    @pl.when(pl.program_id(1) == 0)
    def _():
        dg_ref[...] = jnp.zeros_like(dg_ref)
        sdx_ref[...] = jnp.zeros_like(sdx_ref)

    dg_ref[...] += jnp.sum(dxv * y_ref[...].astype(F32), axis=0, keepdims=True)
    sdx_ref[...] += jnp.sum(dxv, axis=0, keepdims=True)


def _normmod_bwd(x, dh, dres, gamma, sc, producer=None, *, name, comm=None):
    bsz, s, d = x.shape
    ts = _row_tile(s, d * 4, 2 << 20)

    def body(x_ref, dh_ref, dres_ref, g_ref, sc_ref, *rest):
        dx_ref, p_ref, dsh_ref = rest[-6:-3] if producer else rest
        xf = x_ref[...]
        r = lax.rsqrt(jnp.mean(xf * xf, axis=-1, keepdims=True) + NORM_EPS)
        xhat = xf * r
        dh_v = dh_ref[...].astype(F32)
        dxhat = dh_v * (g_ref[...] * (1.0 + sc_ref[...]))
        dxv = dres_ref[...] + r * (dxhat - xhat * jnp.mean(dxhat * xhat, axis=-1, keepdims=True))
        dx_ref[...] = dxv

        @pl.when(pl.program_id(1) == 0)
        def _():
            p_ref[...] = jnp.zeros_like(p_ref)
            dsh_ref[...] = jnp.zeros_like(dsh_ref)

        p_ref[...] += jnp.sum(dh_v * xhat, axis=0, keepdims=True)
        dsh_ref[...] += jnp.sum(dh_v, axis=0, keepdims=True)
        if producer:
            _gate_grads(dxv, rest[0], rest[1], *rest[-3:])

    row = pl.BlockSpec((None, ts, d), lambda b, i: (b, i, 0))
    per_ex = pl.BlockSpec((None, 1, d), lambda b, i: (b, 0, 0))
    vec = jax.ShapeDtypeStruct((bsz, 1, d), F32)
    gate_in, gate_out, gate_specs = (), (), ()
    if producer:
        gate_in, gate_out, gate_specs = (row, per_ex), (jax.ShapeDtypeStruct(x.shape, BF16), vec, vec), (row, per_ex, per_ex)
    return _call(body, name=name, out_shape=(jax.ShapeDtypeStruct(x.shape, F32), vec, vec) + gate_out,
                 operands=[x, dh, dres, gamma, sc, *(producer or ())], grid=(bsz, s // ts),
                 in_specs=[row, row, row, pl.BlockSpec((1, d), lambda b, i: (0, 0)), per_ex, *gate_in],
                 out_specs=(row, per_ex, per_ex) + gate_specs, sem=("parallel", "arbitrary"), comm=comm)


def _loss_head(x, target, gamma, producer, *, name):
    bsz, s, d = x.shape
    ts = _row_tile(s, d * 4, 2 << 20)

    def body(x_ref, t_ref, g_ref, y_ref, gate_ref, dx_ref, loss_ref, dg_ref, *gate_refs):
        xf = x_ref[...]
        r = lax.rsqrt(jnp.mean(xf * xf, axis=-1, keepdims=True) + NORM_EPS)
        xhat = xf * r
        err = xhat * g_ref[...] - t_ref[...]
        dy = err * (1.0 / d)
        dxhat = dy * g_ref[...]
        dxv = r * (dxhat - xhat * jnp.mean(dxhat * xhat, axis=-1, keepdims=True))
        dx_ref[...] = dxv

        @pl.when((pl.program_id(0) == 0) & (pl.program_id(1) == 0))
        def _():
            loss_ref[...] = jnp.zeros_like(loss_ref)
            dg_ref[...] = jnp.zeros_like(dg_ref)

        loss_ref[...] += jnp.sum(err * err, axis=0, keepdims=True)
        dg_ref[...] += jnp.sum(dy * xhat, axis=0, keepdims=True)
        _gate_grads(dxv, y_ref, gate_ref, *gate_refs)

    row = pl.BlockSpec((None, ts, d), lambda b, i: (b, i, 0))
    one = pl.BlockSpec((1, d), lambda b, i: (0, 0))
    per_ex = pl.BlockSpec((None, 1, d), lambda b, i: (b, 0, 0))
    vec, ex_vec = jax.ShapeDtypeStruct((1, d), F32), jax.ShapeDtypeStruct((bsz, 1, d), F32)
    return pl.pallas_call(body, out_shape=(jax.ShapeDtypeStruct(x.shape, F32), vec, vec,
                                           jax.ShapeDtypeStruct(x.shape, BF16), ex_vec, ex_vec), grid=(bsz, s // ts),
                          in_specs=[row, row, one, row, per_ex], out_specs=(row, one, one, row, per_ex, per_ex), name=name,
                          compiler_params=_params("arbitrary", "arbitrary"))(x, target, gamma, *producer)


def _alibi_slope(h, n_heads):
    return 2.0 ** (-8.0 * (h + 1) / n_heads)


def _attn_masks(first_block):
    qi = lax.broadcasted_iota(jnp.int32, (WINDOW, 2 * WINDOW), 0)
    ki = lax.broadcasted_iota(jnp.int32, (WINDOW, 2 * WINDOW), 1)
    dist = qi + WINDOW - ki
    first_key = jnp.where(first_block, WINDOW, 0)
    valid = (dist >= 0) & (dist < WINDOW) & (ki >= first_key)
    return dist.astype(F32), valid


def _dup_halves(span, kv, left):
    f = span.astype(F32)
    rolled = pltpu.roll(f, HEAD_DIM, axis=1)
    out = jnp.where(left, f, rolled) if kv == 0 else jnp.where(left, rolled, f)
    return out.astype(BF16)


def _attn_probs(s, h, n_heads, distf, valid, sink):
    s = s * (HEAD_DIM ** -0.5) - _alibi_slope(h, n_heads) * distf
    s = jnp.where(valid, s, -1e30)
    m = jnp.maximum(jnp.max(s, axis=-1, keepdims=True), sink)
    e = jnp.exp(s - m)
    e_sink = jnp.exp(sink - m)
    inv = 1.0 / (jnp.sum(e, axis=-1, keepdims=True) + e_sink)
    return e * inv, e_sink * inv


def _attn_specs(d, n_blocks, clamp):
    kcol = d // LANES
    cur = (lambda i: jnp.minimum(i, n_blocks - 1)) if clamp else (lambda i: i)
    prev = lambda i: jnp.maximum(cur(i) - 1, 0)
    kv = lambda col, blk: pl.BlockSpec((None, WINDOW, LANES), lambda b, i: (b, blk(i), col))
    return [pl.BlockSpec((None, WINDOW, d), lambda b, i: (b, cur(i), 0)),
            kv(kcol, prev), kv(kcol, cur), kv(kcol + 1, prev), kv(kcol + 1, cur)]


def _attn_fwd(qkv, sinks, *, name, comm=None):
    bsz, s, qkv_dim = qkv.shape
    d = qkv_dim - 2 * N_KV_HEADS * HEAD_DIM
    n_heads = d // HEAD_DIM
    group = n_heads // N_KV_HEADS
    pairs = group // 2
    n_blocks = s // WINDOW

    def body(q_ref, kp_ref, kc_ref, vp_ref, vc_ref, sink_ref, o_ref):
        left = lax.broadcasted_iota(jnp.int32, (1, LANES), 1) < HEAD_DIM
        distf, valid = _attn_masks(pl.program_id(1) == 0)
        kspan = jnp.concatenate([kp_ref[...], kc_ref[...]], axis=0)
        vspan = jnp.concatenate([vp_ref[...], vc_ref[...]], axis=0)
        for kv in range(N_KV_HEADS):
            kdup, vdup = _dup_halves(kspan, kv, left), _dup_halves(vspan, kv, left)
            res = []
            for par in range(2):
                keep = left if par == 0 else jnp.logical_not(left)
                cols = [pl.ds((kv * pairs + p) * LANES, LANES) for p in range(pairs)]
                lhs = jnp.concatenate([jnp.where(keep, q_ref[:, cl], jnp.zeros((), BF16)) for cl in cols], axis=0)
                sc = lax.dot_general(lhs, kdup, (((1,), (1,)), ((), ())), preferred_element_type=F32)
                probs = []
                for p in range(pairs):
                    h = kv * group + 2 * p + par
                    pr, _ = _attn_probs(sc[p * WINDOW:(p + 1) * WINDOW], h, n_heads, distf, valid, sink_ref[h])
                    probs.append(pr.astype(BF16))
                res.append(jnp.dot(jnp.concatenate(probs, axis=0), vdup, preferred_element_type=F32))
            for p in range(pairs):
                rows = slice(p * WINDOW, (p + 1) * WINDOW)
                o_ref[:, pl.ds((kv * pairs + p) * LANES, LANES)] = jnp.where(left, res[0][rows], res[1][rows]).astype(BF16)

    return _call(body, name=name, out_shape=jax.ShapeDtypeStruct((bsz, s, d), BF16), operands=[qkv] * 5 + [sinks],
                 grid=(bsz, n_blocks), in_specs=_attn_specs(d, n_blocks, False) + [SMEM_SPEC],
                 out_specs=pl.BlockSpec((None, WINDOW, d), lambda b, i: (b, i, 0)), sem=("parallel", "parallel"), comm=comm)


def _attn_bwd(qkv, do, sinks, *, name, comm=None):
    bsz, s, qkv_dim = qkv.shape
    d = qkv_dim - 2 * N_KV_HEADS * HEAD_DIM
    n_heads = d // HEAD_DIM
    group = n_heads // N_KV_HEADS
    pairs = group // 2
    n_blocks = s // WINDOW
    tn_dims = (((0,), (0,)), ((), ()))

    def body(q_ref, kp_ref, kc_ref, vp_ref, vc_ref, do_ref, sink_ref, dqkv_ref, colsum_ref, dsink_ref,
             dq_prev, dk_carry, dv_carry):
        b, i = pl.program_id(0), pl.program_id(1)
        left = lax.broadcasted_iota(jnp.int32, (1, LANES), 1) < HEAD_DIM

        @pl.when((b == 0) & (i == 0))
        def _():
            colsum_ref[...] = jnp.zeros_like(colsum_ref)
            dsink_ref[...] = jnp.zeros_like(dsink_ref)

        @pl.when(i == 0)
        def _():
            dqkv_ref[...] = jnp.zeros_like(dqkv_ref)
            dk_carry[...] = jnp.zeros_like(dk_carry)
            dv_carry[...] = jnp.zeros_like(dv_carry)

        @pl.when(i > 0)
        def _():
            dq_v = dq_prev[...]
            dqkv_ref[:, pl.ds(0, d)] = dq_v.astype(BF16)
            colsum_ref[:, pl.ds(0, d)] += jnp.sum(dq_v, axis=0, keepdims=True)

        @pl.when(i < n_blocks)
        def _():
            distf, valid = _attn_masks(i == 0)
            kspan = jnp.concatenate([kp_ref[...], kc_ref[...]], axis=0)
            vspan = jnp.concatenate([vp_ref[...], vc_ref[...]], axis=0)
            dk_blk, dv_blk = [], []
            for kv in range(N_KV_HEADS):
                kdup, vdup = _dup_halves(kspan, kv, left), _dup_halves(vspan, kv, left)
                dq_res, dk_sum, dv_sum = [], None, None
                for par in range(2):
                    keep = left if par == 0 else jnp.logical_not(left)
                    cols = [pl.ds((kv * pairs + p) * LANES, LANES) for p in range(pairs)]
                    zero = jnp.zeros((), BF16)
                    lhs = jnp.concatenate([jnp.where(keep, q_ref[:, cl], zero) for cl in cols], axis=0)
                    dol = jnp.concatenate([jnp.where(keep, do_ref[:, cl], zero) for cl in cols], axis=0)
                    sc = lax.dot_general(lhs, kdup, (((1,), (1,)), ((), ())), preferred_element_type=F32)
                    dp = lax.dot_general(dol, vdup, (((1,), (1,)), ((), ())), preferred_element_type=F32)
                    probs, dscores = [], []
                    for p in range(pairs):
                        h = kv * group + 2 * p + par
                        rows = slice(p * WINDOW, (p + 1) * WINDOW)
                        pr, p_sink = _attn_probs(sc[rows], h, n_heads, distf, valid, sink_ref[h])
                        delta = jnp.sum(pr * dp[rows], axis=-1, keepdims=True)
                        dscores.append((pr * (dp[rows] - delta) * (HEAD_DIM ** -0.5)).astype(BF16))
                        probs.append(pr.astype(BF16))
                        dsink_ref[pl.ds(h, 1), :] += jnp.zeros((1, LANES), F32) - jnp.sum(p_sink * delta)
                    ds_all = jnp.concatenate(dscores, axis=0)
                    p_all = jnp.concatenate(probs, axis=0)
                    dq_res.append(jnp.dot(ds_all, kdup, preferred_element_type=F32))
                    dk_par = lax.dot_general(ds_all, lhs, tn_dims, preferred_element_type=F32)
                    dv_par = lax.dot_general(p_all, dol, tn_dims, preferred_element_type=F32)
                    dk_sum = dk_par if dk_sum is None else dk_sum + dk_par
                    dv_sum = dv_par if dv_sum is None else dv_sum + dv_par
                for p in range(pairs):
                    rows = slice(p * WINDOW, (p + 1) * WINDOW)
                    dq_prev[:, pl.ds((kv * pairs + p) * LANES, LANES)] = jnp.where(left, dq_res[0][rows], dq_res[1][rows])
                dk_blk.append(dk_sum + pltpu.roll(dk_sum, HEAD_DIM, axis=1))
                dv_blk.append(dv_sum + pltpu.roll(dv_sum, HEAD_DIM, axis=1))
            dk_span = jnp.where(left, dk_blk[0], dk_blk[1])
            dv_span = jnp.where(left, dv_blk[0], dv_blk[1])
            dk_done = dk_carry[...] + dk_span[:WINDOW]
            dv_done = dv_carry[...] + dv_span[:WINDOW]
            dk_carry[...] = dk_span[WINDOW:]
            dv_carry[...] = dv_span[WINDOW:]

            @pl.when(i > 0)
            def _():
                dqkv_ref[:, pl.ds(d, LANES)] = dk_done.astype(BF16)
                dqkv_ref[:, pl.ds(d + LANES, LANES)] = dv_done.astype(BF16)
                colsum_ref[:, pl.ds(d, LANES)] += jnp.sum(dk_done, axis=0, keepdims=True)
                colsum_ref[:, pl.ds(d + LANES, LANES)] += jnp.sum(dv_done, axis=0, keepdims=True)

        @pl.when(i == n_blocks)
        def _():
            dk_done, dv_done = dk_carry[...], dv_carry[...]
            dqkv_ref[:, pl.ds(d, LANES)] = dk_done.astype(BF16)
            dqkv_ref[:, pl.ds(d + LANES, LANES)] = dv_done.astype(BF16)
            colsum_ref[:, pl.ds(d, LANES)] += jnp.sum(dk_done, axis=0, keepdims=True)
            colsum_ref[:, pl.ds(d + LANES, LANES)] += jnp.sum(dv_done, axis=0, keepdims=True)

    do_spec = pl.BlockSpec((None, WINDOW, d), lambda b, i: (b, jnp.minimum(i, n_blocks - 1), 0))
    out_shape = (jax.ShapeDtypeStruct((bsz, s, qkv_dim), BF16), jax.ShapeDtypeStruct((1, qkv_dim), F32),
                 jax.ShapeDtypeStruct((n_heads, LANES), F32))
    out_specs = (pl.BlockSpec((None, WINDOW, qkv_dim), lambda b, i: (b, jnp.maximum(i - 1, 0), 0)),
                 pl.BlockSpec((1, qkv_dim), lambda b, i: (0, 0)),
                 pl.BlockSpec((n_heads, LANES), lambda b, i: (0, 0)))
    return _call(body, name=name, out_shape=out_shape, operands=[qkv] * 5 + [do, sinks], grid=(bsz, n_blocks + 1),
                 in_specs=_attn_specs(d, n_blocks, True) + [do_spec, SMEM_SPEC], out_specs=out_specs,
                 scratch_shapes=[pltpu.VMEM((WINDOW, d), F32), pltpu.VMEM((WINDOW, LANES), F32), pltpu.VMEM((WINDOW, LANES), F32)],
                 sem=("arbitrary", "arbitrary"), comm=comm)


def _conv_tile(s):
    return min(256, s)


def _halo_specs(ts, width, s):
    per = ts // CONV_HALO
    prev = pl.BlockSpec((None, CONV_HALO, width), lambda b, i: (b, jnp.maximum(i * per - 1, 0), 0))
    nxt = pl.BlockSpec((None, CONV_HALO, width), lambda b, i: (b, jnp.minimum((i + 1) * per, s // CONV_HALO - 1), 0))
    cur = pl.BlockSpec((None, ts, width), lambda b, i: (b, i, 0))
    return prev, cur, nxt


def _glu(u, d):
    return u[:, :d] * jax.nn.sigmoid(u[:, d:])


def _store_shifted(shifted, value):
    rows = value.shape[0]
    shifted[0] = value
    for b in range(1, SUBLANES):
        shifted[b] = pltpu.roll(value, rows - b, axis=0)


def _at(shifted, base, offset, n_rows, lanes):
    return shifted[offset % SUBLANES, pl.ds(base + (offset - offset % SUBLANES), n_rows), lanes]


def _taps(w_ref, shifted, out_ref, ts, d, offset):
    for r0 in range(0, ts, CONV_ROWS):
        for l0 in range(0, d, LANES):
            lanes = pl.ds(l0, LANES)
            acc = jnp.zeros((CONV_ROWS, LANES), F32)
            for j in range(CONV_WIDTH):
                acc = acc + w_ref[pl.ds(j, 1), lanes] * _at(shifted, r0, offset(j), CONV_ROWS, lanes)
            out_ref[pl.ds(r0, CONV_ROWS), lanes] = acc


def _conv_fwd(u, w_dw, b_dw, ln_g, ln_b, *, name, comm=None):
    bsz, s, d2 = u.shape
    d = d2 // 2
    ts = _conv_tile(s)

    def body(up_ref, uc_ref, w_ref, bdw_ref, g_ref, b_ref, z_ref, v_ref, gbuf):
        halo = jnp.where(pl.program_id(1) > 0, _glu(up_ref[...], d), 0.0)
        _store_shifted(gbuf, jnp.concatenate([halo, _glu(uc_ref[...], d)], axis=0))
        _taps(w_ref, gbuf, v_ref, ts, d, lambda j: CONV_HALO - (CONV_WIDTH - 1) + j)
        v = v_ref[...] + bdw_ref[...]
        v_ref[...] = v
        mu = jnp.mean(v, axis=-1, keepdims=True)
        cen = v - mu
        rstd = lax.rsqrt(jnp.mean(cen * cen, axis=-1, keepdims=True) + NORM_EPS)
        ln = cen * rstd * g_ref[...] + b_ref[...]
        z_ref[...] = (ln * jax.nn.sigmoid(ln)).astype(BF16)

    prev, cur, _ = _halo_specs(ts, d2, s)
    one = pl.BlockSpec((1, d), lambda b, i: (0, 0))
    row = pl.BlockSpec((None, ts, d), lambda b, i: (b, i, 0))
    return _call(body, name=name, out_shape=(jax.ShapeDtypeStruct((bsz, s, d), BF16), jax.ShapeDtypeStruct((bsz, s, d), F32)),
                 operands=[u, u, w_dw, b_dw, ln_g, ln_b], grid=(bsz, s // ts),
                 in_specs=[prev, cur, pl.BlockSpec((CONV_HALO, d), lambda b, i: (0, 0)), one, one, one],
                 out_specs=(row, row), scratch_shapes=[pltpu.VMEM((SUBLANES, ts + CONV_HALO, d), F32)],
                 sem=("parallel", "parallel"), comm=comm)


def _conv_bwd_ln(dz, v, ln_g, ln_b, *, name):
    bsz, s, d = v.shape
    ts = _row_tile(s, d * 4, 2 << 20)

    def body(dz_ref, v_ref, g_ref, b_ref, dv_ref, dg_ref, db_ref, dbdw_ref):
        v_v = v_ref[...]
        mu = jnp.mean(v_v, axis=-1, keepdims=True)
        cen = v_v - mu
        rstd = lax.rsqrt(jnp.mean(cen * cen, axis=-1, keepdims=True) + NORM_EPS)
        vhat = cen * rstd
        ln = vhat * g_ref[...] + b_ref[...]
        sig = jax.nn.sigmoid(ln)
        dln = dz_ref[...] * (sig * (1.0 + ln * (1.0 - sig)))
        dvhat = dln * g_ref[...]
        dv = rstd * (dvhat - jnp.mean(dvhat, axis=-1, keepdims=True)
                     - vhat * jnp.mean(dvhat * vhat, axis=-1, keepdims=True))
        dv_ref[...] = dv

        @pl.when((pl.program_id(0) == 0) & (pl.program_id(1) == 0))
        def _():
            dg_ref[...] = jnp.zeros_like(dg_ref)
            db_ref[...] = jnp.zeros_like(db_ref)
            dbdw_ref[...] = jnp.zeros_like(dbdw_ref)

        dg_ref[...] += jnp.sum(dln * vhat, axis=0, keepdims=True)
        db_ref[...] += jnp.sum(dln, axis=0, keepdims=True)
        dbdw_ref[...] += jnp.sum(dv, axis=0, keepdims=True)

    row = pl.BlockSpec((None, ts, d), lambda b, i: (b, i, 0))
    one = pl.BlockSpec((1, d), lambda b, i: (0, 0))
    vec = jax.ShapeDtypeStruct((1, d), F32)
    return pl.pallas_call(body, out_shape=(jax.ShapeDtypeStruct(v.shape, F32), vec, vec, vec), grid=(bsz, s // ts),
                          in_specs=[row, row, one, one], out_specs=(row, one, one, one), name=name,
                          compiler_params=_params("arbitrary", "arbitrary"))(dz, v, ln_g, ln_b)


def _conv_bwd_taps(dv, u, w_dw, *, name, comm=None):
    bsz, s, d = dv.shape
    ts = _conv_tile(s)
    n_tiles = s // ts

    def body(dvc_ref, dvn_ref, uc_ref, w_ref, du_ref, dbu_ref, dw_ref, dvbuf, dglu, glu):
        i = pl.program_id(1)

        @pl.when((pl.program_id(0) == 0) & (i == 0))
        def _():
            dbu_ref[...] = jnp.zeros_like(dbu_ref)
            dw_ref[...] = jnp.zeros_like(dw_ref)

        u_v = uc_ref[...]
        a, sig = u_v[:, :d], jax.nn.sigmoid(u_v[:, d:])
        glu[...] = a * sig
        ahead = jnp.where(i < n_tiles - 1, dvn_ref[...], 0.0)
        _store_shifted(dvbuf, jnp.concatenate([dvc_ref[...], ahead], axis=0))
        _taps(w_ref, dvbuf, dglu, ts, d, lambda j: CONV_WIDTH - 1 - j)
        for l0 in range(0, d, LANES):
            lanes = pl.ds(l0, LANES)
            for j in range(CONV_WIDTH):
                acc = jnp.zeros((SUBLANES, LANES), F32)
                for r0 in range(0, ts, CONV_ROWS):
                    prod = glu[pl.ds(r0, CONV_ROWS), lanes] * _at(dvbuf, r0, CONV_WIDTH - 1 - j, CONV_ROWS, lanes)
                    for k in range(0, CONV_ROWS, SUBLANES):
                        acc = acc + prod[k:k + SUBLANES]
                dw_ref[j, :, lanes] += acc
        dg_v = dglu[...]
        da = dg_v * sig
        dgate = dg_v * a * sig * (1.0 - sig)
        du_ref[:, pl.ds(0, d)] = da.astype(BF16)
        du_ref[:, pl.ds(d, d)] = dgate.astype(BF16)
        dbu_ref[:, pl.ds(0, d)] += jnp.sum(da, axis=0, keepdims=True)
        dbu_ref[:, pl.ds(d, d)] += jnp.sum(dgate, axis=0, keepdims=True)

    _, dv_cur, dv_next = _halo_specs(ts, d, s)
    _, u_cur, _ = _halo_specs(ts, 2 * d, s)
    out_shape = (jax.ShapeDtypeStruct((bsz, s, 2 * d), BF16), jax.ShapeDtypeStruct((1, 2 * d), F32),
                 jax.ShapeDtypeStruct((CONV_HALO, SUBLANES, d), F32))
    out_specs = (pl.BlockSpec((None, ts, 2 * d), lambda b, i: (b, i, 0)), pl.BlockSpec((1, 2 * d), lambda b, i: (0, 0)),
                 pl.BlockSpec((CONV_HALO, SUBLANES, d), lambda b, i: (0, 0, 0)))
    return _call(body, name=name, out_shape=out_shape, operands=[dv, dv, u, w_dw], grid=(bsz, n_tiles),
                 in_specs=[dv_cur, dv_next, u_cur, pl.BlockSpec((CONV_HALO, d), lambda b, i: (0, 0))],
                 out_specs=out_specs,
                 scratch_shapes=[pltpu.VMEM((SUBLANES, ts + CONV_HALO, d), F32), pltpu.VMEM((ts, d), F32),
                                 pltpu.VMEM((ts, d), F32)],
                 sem=("arbitrary", "arbitrary"), comm=comm)


def _mod_fwd(c_all, w_mod, b_mod, *, name):
    n_layers, d, n = w_mod.shape
    rows = c_all.shape[0]

    def body(c_ref, w_ref, b_ref, o_ref):
        cv = c_ref[...]
        cs = (cv * jax.nn.sigmoid(cv)).astype(BF16)
        o_ref[...] = jnp.dot(cs, w_ref[...].astype(BF16), preferred_element_type=F32) + b_ref[...]

    return pl.pallas_call(body, out_shape=jax.ShapeDtypeStruct((n_layers, rows, n), F32), grid=(n_layers,),
                          in_specs=[pl.BlockSpec((rows, d), lambda l: (0, 0)), pl.BlockSpec((None, d, n), lambda l: (l, 0, 0)),
                                    pl.BlockSpec((None, 1, n), lambda l: (l, 0, 0))],
                          out_specs=pl.BlockSpec((None, rows, n), lambda l: (l, 0, 0)), name=name,
                          compiler_params=_params("parallel"))(c_all, w_mod, b_mod)


def _mod_bwd(c_all, dmod, *, name):
    n_layers, rows, n = dmod.shape
    d = c_all.shape[1]

    def body(c_ref, g_ref, o_ref):
        cv = c_ref[...]
        cs = (cv * jax.nn.sigmoid(cv)).astype(BF16)
        o_ref[...] = lax.dot_general(cs, g_ref[...].astype(BF16), (((0,), (0,)), ((), ())), preferred_element_type=F32)

    return pl.pallas_call(body, out_shape=jax.ShapeDtypeStruct((n_layers, d, n), F32), grid=(n_layers,),
                          in_specs=[pl.BlockSpec((rows, d), lambda l: (0, 0)), pl.BlockSpec((None, rows, n), lambda l: (l, 0, 0))],
                          out_specs=pl.BlockSpec((None, d, n), lambda l: (l, 0, 0)), name=name,
                          compiler_params=_params("parallel"))(c_all, dmod)


def _add_half(g, recv, my_c, *, name):
    p, _, r, cdim = g.shape
    tr = _row_tile(r, cdim * 4)

    def body(c_ref, g_ref, r_ref, o_ref):
        o_ref[...] = (g_ref[...] + r_ref[...]).astype(BF16)

    grid_spec = pltpu.PrefetchScalarGridSpec(
        num_scalar_prefetch=1, grid=(p, r // tr),
        in_specs=[pl.BlockSpec((None, None, tr, cdim), lambda q, i, c_ref: (q, c_ref[0], i, 0)),
                  pl.BlockSpec((None, None, tr, cdim), lambda q, i, c_ref: (q, 0, i, 0))],
        out_specs=pl.BlockSpec((None, tr, cdim), lambda q, i, c_ref: (q, i, 0)))
    return pl.pallas_call(body, out_shape=jax.ShapeDtypeStruct((p, r, cdim), BF16), grid_spec=grid_spec, name=name,
                          compiler_params=_params("parallel", "parallel"))(my_c, g, recv)


def _add_pieces(chip, recv, my_qc, stacked, layer, n_layers, *, name):
    _, r, cdim = chip.shape
    tr = _row_tile(r, cdim * 4)

    def body(qc_ref, own_ref, r0_ref, r1_ref, r2_ref, *rest):
        f32 = lambda ref: ref[...].astype(F32)
        rest[-1][...] = ((f32(own_ref) + f32(r0_ref)) + f32(r1_ref)) + f32(r2_ref)

    piece = lambda k: pl.BlockSpec((None, tr, cdim), lambda i, qc_ref: (k, i, 0))
    in_specs = [pl.BlockSpec((None, tr, cdim), lambda i, qc_ref: (qc_ref[0], i, 0)), piece(0), piece(1), piece(2)]
    operands = [my_qc, chip, recv, recv, recv]
    aliases = {}
    if stacked is not None:
        in_specs.append(pl.BlockSpec(memory_space=pl.ANY))
        operands.append(stacked)
        aliases = {len(operands) - 1: 0}
    grid_spec = pltpu.PrefetchScalarGridSpec(
        num_scalar_prefetch=1, grid=(r // tr,), in_specs=in_specs,
        out_specs=pl.BlockSpec((None, None, tr, cdim), lambda i, qc_ref: (layer, qc_ref[1], i, 0)))
    return pl.pallas_call(body, out_shape=jax.ShapeDtypeStruct((n_layers, 2, r, cdim), F32), grid_spec=grid_spec,
                          input_output_aliases=aliases, name=name, compiler_params=_params("parallel"))(*operands)


def _sum_devices(parts, dmod, *, name):
    def body(p_ref, m_ref, o_ref, b_ref):
        acc = p_ref[0]
        for k in range(1, N_DEV):
            acc = acc + p_ref[k]
        o_ref[...] = acc
        tot = None
        for k in range(N_DEV):
            for e in range(dmod.shape[2]):
                tot = m_ref[k, :, e] if tot is None else tot + m_ref[k, :, e]
        b_ref[...] = tot

    out_shape = (jax.ShapeDtypeStruct(parts.shape[1:], F32),
                 jax.ShapeDtypeStruct((dmod.shape[1],) + dmod.shape[3:], F32))
    return pl.pallas_call(body, out_shape=out_shape, in_specs=[VMEM_SPEC, VMEM_SPEC], out_specs=(VMEM_SPEC, VMEM_SPEC),
                          name=name, compiler_params=_params())(parts, dmod)


def _adamw(w, g, m, v, *, name, comm=None):
    r, cdim = w.shape
    tr = _row_tile(r, cdim * 4, 2 << 20)

    def body(w_ref, g_ref, m_ref, v_ref, d_ref, nm_ref, nv_ref):
        gv = g_ref[...]
        nm = ADAM_B1 * m_ref[...] + (1.0 - ADAM_B1) * gv
        nv = ADAM_B2 * v_ref[...] + (1.0 - ADAM_B2) * (gv * gv)
        m_hat = nm / (1.0 - ADAM_B1 ** ADAM_STEP)
        v_hat = nv / (1.0 - ADAM_B2 ** ADAM_STEP)
        d_ref[...] = -ADAM_LR * (m_hat / (jnp.sqrt(v_hat) + ADAM_EPS) + ADAM_WD * w_ref[...])
        nm_ref[...] = nm
        nv_ref[...] = nv

    row = pl.BlockSpec((tr, cdim), lambda i: (i, 0))
    shape = jax.ShapeDtypeStruct((r, cdim), F32)
    return _call(body, name=name, out_shape=(shape, shape, shape), operands=[w, g, m, v], grid=(r // tr,),
                 in_specs=[row] * 4, out_specs=(row, row, row), sem=("parallel",), comm=comm)


def _all_gather_small(block, *, name, comm=None):
    m_per, n = block.shape

    def body(x_ref, out_ref, send_sems, recv_sems, local_sem):
        x, y, c, _ = _position()
        me, sibling = (x, y, c), (x, y, 1 - c)
        chips = [_peer(k, x, y, c)[:2] for k in CHIP_KINDS]

        def rows(px, py, pc):
            return out_ref.at[pl.ds((4 * px + 2 * py + pc) * m_per, m_per), :]

        def copy(k, blk, to, src=None):
            return pltpu.make_async_remote_copy(src_ref=rows(*blk) if src is None else src, dst_ref=rows(*blk),
                                                send_sem=send_sems.at[k], recv_sem=recv_sems.at[k],
                                                device_id=to, device_id_type=MESH)

        mine = pltpu.make_async_copy(x_ref, rows(*me), local_sem)
        mine.start()
        first = [copy(0, me, sibling, src=x_ref)]
        first += [copy(1 + j, me, (*chip, c), src=x_ref) for j, chip in enumerate(chips)]
        for cp in first:
            cp.start()
        passed = [copy(4 + j, (*chip, c), sibling) for j, chip in enumerate(chips)]
        for j, chip in enumerate(chips):
            copy(1 + j, (*chip, c), me).wait_recv()
            passed[j].start()
        copy(0, sibling, me).wait_recv()
        for j, chip in enumerate(chips):
            copy(4 + j, (*chip, 1 - c), me).wait_recv()
        for cp in first + passed:
            cp.wait_send()
        mine.wait()

    return _call(body, name=name, out_shape=jax.ShapeDtypeStruct((N_DEV * m_per, n), block.dtype), operands=[block],
                 in_specs=[VMEM_SPEC], out_specs=VMEM_SPEC,
                 scratch_shapes=[pltpu.SemaphoreType.DMA((7,)), pltpu.SemaphoreType.DMA((7,)), pltpu.SemaphoreType.DMA],
                 comm=comm)


def _pack(arrays, width):
    flat = jnp.concatenate([a.reshape(-1).astype(F32) for a in arrays])
    rows = -(-flat.shape[0] // width)
    rows = -(-rows // SUBLANES) * SUBLANES
    return jnp.pad(flat, (0, rows * width - flat.shape[0])).reshape(rows, width)


def _unpack(packed, shapes):
    flat, out, off = packed.reshape(-1), [], 0
    for shp in shapes:
        size = 1
        for dim in shp:
            size *= dim
        out.append(flat[off:off + size].reshape(shp))
        off += size
    return out


def _adamw_packed(ws, gs, ms, vs, width, *, name):
    shapes = [w.shape for w in ws]
    res = _adamw(_pack(ws, width), _pack(gs, width), _pack(ms, width), _pack(vs, width), name=name)
    return [_unpack(r, shapes) for r in res]


MB = float(1 << 20)


def _nbytes(shape, dtype):
    size = jnp.dtype(dtype).itemsize
    for dim in shape:
        size *= dim
    return size


def _gather_plans(shard, on_ready):
    r, cdim = shard.shape
    half = r // 2
    buf = jax.ShapeDtypeStruct((N_CHIPS, r, cdim), shard.dtype)
    rows = lambda ref, q, h: ref.at[pl.ds(q, 1), pl.ds(h * half, half)]
    there = lambda kind, pos: _peer_shard(kind, pos[0], pos[1])
    cost = 3 * _nbytes((half, cdim), shard.dtype) / MB

    def forward(outs):
        copies = [(lambda ins, outs, pos, kind=kind: rows(ins[0], there(kind, pos), pos[2]),
                   lambda ins, outs, pos, kind=kind: rows(outs[0], there(kind, pos), pos[2]),
                   lambda ins, outs, pos, kind=kind: rows(outs[0], there(kind, pos), 1 - pos[2]), "c") for kind in CHIP_KINDS]
        return _Plan("d2d", cost, [outs[0]], [jax.ShapeDtypeStruct(buf.shape, buf.dtype)], copies, {0: 0},
                     lambda done: on_ready(done[0]))

    copies = [(lambda ins, outs, pos: ins[0].at[:, pl.ds(pos[2] * half, half)],
               lambda ins, outs, pos: rows(outs[0], pos[3], pos[2]),
               lambda ins, outs, pos, kind=kind: rows(outs[0], there(kind, pos), pos[2]), kind) for kind in CHIP_KINDS]
    own = lambda ins, outs, pos: outs[0].at[pl.ds(pos[3], 1)]
    copies.append((lambda ins, outs, pos: ins[0], own, own, "c"))
    return _Plan("ici", cost, [shard[None]], [buf], copies), forward


class _Exchanges:
    def __init__(self):
        self.queue = []

    def add(self, plan, front=False):
        if front:
            self.queue.insert(0, plan)
        else:
            self.queue.append(plan)

    def take(self, budget_mb, at_least_one=False):
        chosen, spent = [p for p in self.queue if p.link == "d2d"], 0.0
        for p in self.queue:
            if p.link == "ici" and (spent + p.cost <= budget_mb or (at_least_one and spent == 0.0)):
                chosen.append(p)
                spent += p.cost
        if not chosen:
            return None
        self.queue = [p for p in self.queue if all(p is not ch for ch in chosen)]
        return _merge(chosen)

    def flush(self, budget_mb, until=lambda: False):
        while self.queue and not until():
            _exchange(self.take(budget_mb, at_least_one=True), name="exchange")


def kernel(x, c, w_mod, b_mod, norm_mix, norm_mlp, w_qkv, b_qkv, w_o, b_o, sinks, w_pw1, b_pw1, w_dw, b_dw, conv_ln_g, conv_ln_b, w_pw2, b_pw2, w_up, w_down, final_norm, loss_target, m_w_mod, m_b_mod, m_norm_mix, m_norm_mlp, m_w_qkv, m_b_qkv, m_w_o, m_b_o, m_sinks, m_w_pw1, m_b_pw1, m_w_dw, m_b_dw, m_conv_ln_g, m_conv_ln_b, m_w_pw2, m_b_pw2, m_w_up, m_w_down, m_final_norm, v_w_mod, v_b_mod, v_norm_mix, v_norm_mlp, v_w_qkv, v_b_qkv, v_w_o, v_b_o, v_sinks, v_w_pw1, v_b_pw1, v_w_dw, v_b_dw, v_conv_ln_g, v_conv_ln_b, v_w_pw2, v_b_pw2, v_w_up, v_w_down, v_final_norm):
    bsz, s, d = x.shape
    t = bsz * s
    depth = w_mod.shape[0]
    n_attn, n_conv = w_qkv.shape[0], w_pw1.shape[0]
    qkv_dim = d + 2 * N_KV_HEADS * HEAD_DIM
    mx, my, mc, mq = _position()
    me = 4 * mx + 2 * my + mc
    my_c = jnp.reshape(mc, (1,)).astype(jnp.int32)
    my_qc = jnp.stack([mq, mc]).astype(jnp.int32)
    pending = _Exchanges()
    SMALL, MEDIUM, LARGE = 2.5, 3.5, 6.5

    weights = {}
    order = []
    for i in range(depth):
        j = i // 2
        order += ([(("qkv", j), w_qkv[j], True), (("o", j), w_o[j], False)] if i % 2 == 0 else
                  [(("pw1", j), w_pw1[j], True), (("pw2", j), w_pw2[j], False)])
        order += [(("up", i), w_up[i], True), (("down", i), w_down[i], False)]
    for key, shard, by_cols in order:
        def ready(buf, key=key, by_cols=by_cols):
            weights[key] = jnp.transpose(buf, (1, 0, 2)).reshape(buf.shape[1], -1) if by_cols else buf.reshape(-1, buf.shape[2])
        ici, forward = _gather_plans(shard.astype(BF16), ready)
        ici.then = lambda outs, forward=forward: pending.add(forward(outs), front=True)
        pending.add(ici)

    def weight(key):
        pending.flush(SMALL, until=lambda: key in weights)
        return weights[key]

    small_sharded = [b_pw1, w_dw, b_dw, conv_ln_g, conv_ln_b, b_pw2]
    c_pad = jnp.pad(c, ((0, SUBLANES - bsz), (0, 0)))
    gathered = _all_gather_small(jnp.concatenate([c_pad, _pack(small_sharded, d)], axis=0), name="gather_c",
                                 comm=pending.take(SMALL, True))
    gathered = gathered.reshape(N_DEV, -1, d)
    c_all = gathered[:, :bsz].reshape(N_DEV * bsz, d)
    per_chip = [_unpack(gathered[2 * q, SUBLANES:], [a.shape for a in small_sharded]) for q in range(N_CHIPS)]
    b_pw1_f, w_dw_f, b_dw_f, ln_g_f, ln_b_f, b_pw2_f = [jnp.concatenate([per_chip[q][k] for q in range(N_CHIPS)], axis=-1)
                                                           for k in range(len(small_sharded))]
    w_dw_f = jnp.pad(w_dw_f, ((0, 0), (0, CONV_HALO - CONV_WIDTH), (0, 0)))

    n_mod = w_mod.shape[2]
    b_mod_cols = lax.dynamic_slice_in_dim(b_mod, mq * n_mod, n_mod, axis=1).reshape(depth, 1, n_mod)
    c_all8 = gathered[:, :SUBLANES].reshape(N_DEV * SUBLANES, d)
    mod_part = _mod_fwd(c_all8, w_mod, b_mod_cols, name="mod_fwd")
    peer_rows = lambda kind, pos: pl.ds(SUBLANES * (2 * _peer_shard(kind, pos[0], pos[1]) + pos[2]), SUBLANES)
    got = {}
    pending.add(_Plan("ici", 0.01, [mod_part],[jax.ShapeDtypeStruct((len(CHIP_KINDS), depth, SUBLANES, n_mod), F32)],
                      [(lambda ins, outs, pos, kind=kind: ins[0].at[:, peer_rows(kind, pos)],
                        lambda ins, outs, pos, k=k: outs[0].at[k], lambda ins, outs, pos, k=k: outs[0].at[k], kind)
                       for k, kind in enumerate(CHIP_KINDS)], then=lambda outs: got.update(rows=outs[0])), front=True)
    _exchange(pending.take(0.0, at_least_one=True), name="mod_exchange")
    own = lax.dynamic_slice_in_dim(mod_part, me * SUBLANES, SUBLANES, axis=1)
    parts = [own, got["rows"][0], got["rows"][1], got["rows"][2]]
    part_of = (0, 2, 1, 3)

    def shard(sidx):
        dist, out = jnp.bitwise_xor(mq, sidx), parts[0]
        for distance in (1, 2, 3):
            out = jnp.where(dist == distance, parts[part_of[distance]], out)
        return out

    mod = jnp.stack([shard(sidx) for sidx in range(N_CHIPS)])[:, :, :bsz]
    mod = jnp.transpose(mod, (1, 2, 0, 3)).reshape(depth, bsz, N_MOD, 1, d)
    mods = [[mod[i][:, k] for k in range(N_MOD)] for i in range(depth)]

    saved = []
    xc = x
    h1 = _normmod(xc, norm_mix[0][None], mods[0][1], mods[0][0], name="normmod")
    for i in range(depth):
        j = i // 2
        sh1, sc1, g1, sh2, sc2, g2 = mods[i]
        mlp_norm = (norm_mlp[i][None], sc2, sh2)
        if i % 2 == 0:
            wq, wo = weight(("qkv", j)), weight(("o", j))
            qkv = _mm(h1.reshape(t, d), wq, bias=b_qkv[j], tn=qkv_dim, name="mm_qkv",
                      comm=pending.take(SMALL)).reshape(bsz, s, qkv_dim)
            mix = _attn_fwd(qkv, sinks[j], name="attn_fwd", comm=pending.take(LARGE, True))
            y1, x1, h2 = _mm(mix.reshape(t, d), wo, bias=b_o[j], epi="resid", resid=xc.reshape(t, d), gate=g1, seq=s,
                             norm=mlp_norm, tn=d, name="mm_out", comm=pending.take(SMALL))
            extra = (qkv, mix)
        else:
            wp1, wp2 = weight(("pw1", j)), weight(("pw2", j))
            u = _mm(h1.reshape(t, d), wp1, bias=b_pw1_f[j], out_dtype=F32, tn=d, name="mm_pw1",
                    comm=pending.take(MEDIUM)).reshape(bsz, s, 2 * d)
            mix, conv_v = _conv_fwd(u, w_dw_f[j], b_dw_f[j][None], ln_g_f[j][None], ln_b_f[j][None], name="conv_fwd",
                                    comm=pending.take(LARGE, True))
            y1, x1, h2 = _mm(mix.reshape(t, d), wp2, bias=b_pw2_f[j], epi="resid", resid=xc.reshape(t, d), gate=g1, seq=s,
                             norm=mlp_norm, tn=d, name="mm_out", comm=pending.take(SMALL))
            extra = (u, mix, conv_v)
        act, slope = _mm(h2, weight(("up", i)), epi="relu2", tm=1024, tn=2 * d, name="mm_up", comm=pending.take(LARGE, True))
        mix_norm = (norm_mix[i + 1][None], mods[i + 1][1], mods[i + 1][0]) if i + 1 < depth else None
        y2, x2, *h_next = _mm(act, weight(("down", i)), epi="resid", resid=x1, gate=g2, seq=s, norm=mix_norm,
                              tm=512, tn=d, name="mm_down", comm=pending.take(LARGE, True))
        saved.append((xc, h1, extra, y1, x1.reshape(bsz, s, d), h2, act, y2, slope))
        xc = x2.reshape(bsz, s, d)
        h1 = h_next[0] if h_next else None
    pending.flush(LARGE)

    last_y2, last_g2 = saved[-1][7].reshape(bsz, s, d), mods[-1][5]
    dx, loss_cols, d_final, dyb, dg2, _ = _loss_head(xc, loss_target, final_norm[None], (last_y2, last_g2), name="loss_head")
    loss = lax.psum(0.5 / d * jnp.sum(loss_cols), ("x", "y", "c"))

    totals = {}

    def reduce_scatter(name, layer, n_layers, grad):
        view = grad.reshape(N_CHIPS, 2, grad.shape[1] // 2, grad.shape[2])
        half_shape = (N_CHIPS, 1) + view.shape[2:]

        def scatter(outs):
            chip = _add_half(view, outs[0], my_c, name="rs_chipsum")
            copies = [(lambda ins, outs, pos, kind=kind: ins[0].at[pl.ds(_peer_shard(kind, pos[0], pos[1]), 1)],
                       lambda ins, outs, pos, k=k: outs[0].at[pl.ds(k, 1)],
                       lambda ins, outs, pos, k=k: outs[0].at[pl.ds(k, 1)], kind) for k, kind in enumerate(CHIP_KINDS)]

            def total(outs):
                totals[name] = _add_pieces(chip, outs[0], my_qc, totals.get(name), layer, n_layers, name="rs_total")

            pending.add(_Plan("ici", 3 * _nbytes(chip.shape[1:], BF16) / MB, [chip],
                              [jax.ShapeDtypeStruct((3,) + chip.shape[1:], BF16)], copies, then=total))

        pending.add(_Plan("d2d", _nbytes(half_shape, F32) / MB, [view], [jax.ShapeDtypeStruct(half_shape, F32)],
                          [(lambda ins, outs, pos: ins[0].at[:, pl.ds(1 - pos[2], 1)],
                            lambda ins, outs, pos: outs[0], lambda ins, outs, pos: outs[0], "c")], then=scatter))

    dmods, small = [None] * depth, {}
    for i in reversed(range(depth)):
        j = i // 2
        xin, h1, extra, y1, x1, h2, act, _, slope = saved[i]
        sh1, sc1, g1, sh2, sc2, g2 = mods[i]
        dyb = dyb.reshape(t, d)
        reduce_scatter("down", i, depth,
                       _mm_tn(act, dyb, name="dw_down", comm=pending.take(LARGE)).reshape(N_CHIPS, -1, d))
        dup = _mm(dyb, weights["down", i], nt=True, epi="dact", act=slope, tm=1024, tn=2 * d, name="mm_dact",
                  comm=pending.take(LARGE))
        reduce_scatter("up", i, depth,
                       _mm_tn(h2.reshape(t, d), dup, col_shards=N_CHIPS, name="dw_up", comm=pending.take(LARGE)))
        dh2 = _mm(dup, weights["up", i], nt=True, tm=512, tn=d, name="mm_dh2",
                  comm=pending.take(LARGE)).reshape(bsz, s, d)
        dx1, p2, dsh2, dyb, dg1, sdx = _normmod_bwd(x1, dh2, dx, norm_mlp[i][None], sc2, (y1.reshape(bsz, s, d), g1),
                                                    name="normmod_bwd", comm=pending.take(SMALL))
        dyb = dyb.reshape(t, d)
        d_bias_out = jnp.sum(g1 * sdx, axis=(0, 1))
        if i % 2 == 0:
            qkv, mix = extra
            small["b_o", j] = d_bias_out
            reduce_scatter("o", j, n_attn,
                           _mm_tn(mix.reshape(t, d), dyb, name="dw_sq", comm=pending.take(SMALL)).reshape(N_CHIPS, -1, d))
            dmix = _mm(dyb, weights["o", j], nt=True, tn=d, name="mm_dmix", comm=pending.take(SMALL)).reshape(bsz, s, d)
            dqkv, d_bqkv, d_sink = _attn_bwd(qkv, dmix, sinks[j], name="attn_bwd", comm=pending.take(2 * LARGE))
            small["b_qkv", j], small["sinks", j] = d_bqkv[0], d_sink[:, 0]
            dqkv = dqkv.reshape(t, qkv_dim)
            dwq = _mm_tn(h1.reshape(t, d), dqkv, tn=qkv_dim, name="dw_qkv", comm=pending.take(SMALL))
            reduce_scatter("qkv", j, n_attn, jnp.transpose(dwq.reshape(d, N_CHIPS, -1), (1, 0, 2)))
            dh1 = _mm(dqkv, weights["qkv", j], nt=True, tn=d, name="mm_dh1a", comm=pending.take(SMALL))
        else:
            u, mix, conv_v = extra
            small["b_pw2", j] = d_bias_out
            reduce_scatter("pw2", j, n_conv,
                           _mm_tn(mix.reshape(t, d), dyb, name="dw_sq", comm=pending.take(SMALL)).reshape(N_CHIPS, -1, d))
            dz = _mm(dyb, weights["pw2", j], nt=True, out_dtype=F32, tn=d, name="mm_dz", comm=pending.take(SMALL)).reshape(bsz, s, d)
            dv, d_lng, d_lnb, d_bdw = _conv_bwd_ln(dz, conv_v, ln_g_f[j][None], ln_b_f[j][None], name="conv_bwd_ln")
            du, d_bpw1, d_wdw = _conv_bwd_taps(dv, u, w_dw_f[j], name="conv_bwd_taps", comm=pending.take(2 * LARGE))
            small["ln_g", j], small["ln_b", j], small["b_dw", j] = d_lng[0], d_lnb[0], d_bdw[0]
            small["b_pw1", j], small["w_dw", j] = d_bpw1[0], jnp.sum(d_wdw[:CONV_WIDTH], axis=1)
            du = du.reshape(t, 2 * d)
            reduce_scatter("pw1", j, n_conv,
                           _mm_tn(h1.reshape(t, d), du, col_shards=N_CHIPS, name="dw_pw1", comm=pending.take(MEDIUM)))
            dh1 = _mm(du, weights["pw1", j], nt=True, tn=d, name="mm_dh1c", comm=pending.take(MEDIUM))
        below = (saved[i - 1][7].reshape(bsz, s, d), mods[i - 1][5]) if i > 0 else None
        dx, p1, dsh1, *gate_grads = _normmod_bwd(xin, dh1.reshape(bsz, s, d), dx1, norm_mix[i][None], sc1, below,
                                                 name="normmod_bwd", comm=pending.take(SMALL, i == 0))
        small["norm_mix", i] = jnp.sum((1.0 + sc1) * p1, axis=(0, 1))
        small["norm_mlp", i] = jnp.sum((1.0 + sc2) * p2, axis=(0, 1))
        dmods[i] = jnp.concatenate([dsh1, norm_mix[i] * p1, dg1, dsh2, norm_mlp[i] * p2, dg2], axis=1)
        if i > 0:
            dyb, dg2, _ = gate_grads
    grad_x = dx

    small_names = ([("norm_mix", i) for i in range(depth)] + [("norm_mlp", i) for i in range(depth)]
                   + [(nm, j) for nm in ("b_qkv", "b_o", "sinks") for j in range(n_attn)]
                   + [(nm, j) for nm in ("b_pw1", "w_dw", "b_dw", "ln_g", "ln_b", "b_pw2") for j in range(n_conv)])
    small_list = [small[k] for k in small_names] + [d_final[0]]
    small_pack = _pack(small_list, d)
    dmod_rows = jnp.stack(dmods).reshape(depth * bsz * N_MOD, d)
    n_small = small_pack.shape[0]
    pending.flush(LARGE)
    big_names = ["qkv", "o", "pw1", "pw2", "up", "down"]
    bufs = [totals[nm] for nm in big_names]
    copies = []
    for a, buf in enumerate(bufs):
        for layer in range(buf.shape[0]):
            half = lambda ref, h, layer=layer: ref.at[pl.ds(layer, 1), pl.ds(h, 1)]
            copies.append((lambda ins, outs, pos, a=a, half=half: half(ins[a], pos[2]),
                           lambda ins, outs, pos, a=a, half=half: half(outs[a], pos[2]),
                           lambda ins, outs, pos, a=a, half=half: half(outs[a], 1 - pos[2]), "c"))
    shared = {}
    share = _Plan("d2d", 0.0, bufs, [jax.ShapeDtypeStruct(b.shape, F32) for b in bufs], copies,
                  {a: a for a in range(len(bufs))}, lambda outs: shared.update(zip(big_names, outs)))
    gathered = _all_gather_small(jnp.concatenate([small_pack, _pack([dmod_rows], d)], axis=0), name="gather_small",
                                 comm=share)
    gathered = gathered.reshape(N_DEV, -1, d)
    dmod_all = gathered[:, n_small:n_small + depth * bsz * N_MOD].reshape(N_DEV, depth, bsz, N_MOD, d)
    small_sum, g_b_mod = _sum_devices(gathered[:, :n_small], dmod_all, name="sum_devices")
    small_tot = dict(zip(small_names + ["final_norm"], _unpack(small_sum, [a.shape for a in small_list])))
    stacked = lambda nm, count: jnp.stack([small_tot[nm, k] for k in range(count)])
    g_norm_mix, g_norm_mlp = stacked("norm_mix", depth), stacked("norm_mlp", depth)
    g_b_qkv, g_b_o, g_sinks = stacked("b_qkv", n_attn), stacked("b_o", n_attn), stacked("sinks", n_attn)
    g_final = small_tot["final_norm"]
    g_b_mod = g_b_mod.reshape(depth, N_MOD * d)
    shard_cols = lambda g: lax.dynamic_slice_in_dim(g, mq * (g.shape[-1] // N_CHIPS), g.shape[-1] // N_CHIPS, axis=g.ndim - 1)
    g_b_pw1, g_w_dw, g_b_dw, g_ln_g, g_ln_b, g_b_pw2 = [shard_cols(stacked(nm, n_conv))
                                                        for nm in ("b_pw1", "w_dw", "b_dw", "ln_g", "ln_b", "b_pw2")]

    dmod_cols = jnp.transpose(dmod_all, (1, 0, 2, 3, 4)).reshape(depth, N_DEV * bsz, N_MOD * d)
    dmod_cols = lax.dynamic_slice_in_dim(dmod_cols, mq * n_mod, n_mod, axis=2)
    g_w_mod = _mod_bwd(c_all, dmod_cols, name="mod_bwd")

    def adam(w, g, m, v, name):
        two_d = lambda a: a.reshape(-1, a.shape[-1])
        return [r.reshape(w.shape) for r in _adamw(two_d(w), two_d(g), two_d(m), two_d(v), name=name)]

    results = {"w_mod": (g_w_mod,) + tuple(adam(w_mod, g_w_mod, m_w_mod, v_w_mod, "adamw"))}

    g_w_qkv, g_w_o, g_w_pw1, g_w_pw2, g_w_up, g_w_down = [
        shared[nm].reshape(shared[nm].shape[0], -1, shared[nm].shape[3]) for nm in big_names]
    for nm, w, g, m, v in (("w_qkv", w_qkv, g_w_qkv, m_w_qkv, v_w_qkv),
                           ("w_o", w_o, g_w_o, m_w_o, v_w_o), ("w_pw1", w_pw1, g_w_pw1, m_w_pw1, v_w_pw1),
                           ("w_pw2", w_pw2, g_w_pw2, m_w_pw2, v_w_pw2), ("w_up", w_up, g_w_up, m_w_up, v_w_up),
                           ("w_down", w_down, g_w_down, m_w_down, v_w_down)):
        results[nm] = (g,) + tuple(adam(w, g, m, v, "adamw"))
    small_w = dict(b_mod=(b_mod, g_b_mod, m_b_mod, v_b_mod), norm_mix=(norm_mix, g_norm_mix, m_norm_mix, v_norm_mix),
                   norm_mlp=(norm_mlp, g_norm_mlp, m_norm_mlp, v_norm_mlp), b_qkv=(b_qkv, g_b_qkv, m_b_qkv, v_b_qkv),
                   b_o=(b_o, g_b_o, m_b_o, v_b_o), sinks=(sinks, g_sinks, m_sinks, v_sinks),
                   b_pw1=(b_pw1, g_b_pw1, m_b_pw1, v_b_pw1), w_dw=(w_dw, g_w_dw, m_w_dw, v_w_dw),
                   b_dw=(b_dw, g_b_dw, m_b_dw, v_b_dw), conv_ln_g=(conv_ln_g, g_ln_g, m_conv_ln_g, v_conv_ln_g),
                   conv_ln_b=(conv_ln_b, g_ln_b, m_conv_ln_b, v_conv_ln_b), b_pw2=(b_pw2, g_b_pw2, m_b_pw2, v_b_pw2),
                   final_norm=(final_norm, g_final, m_final_norm, v_final_norm))
    names = list(small_w)
    deltas, new_ms, new_vs = _adamw_packed(*[[small_w[nm][k] for nm in names] for k in range(4)], d, name="adamw_small")
    for k, nm in enumerate(names):
        results[nm] = (small_w[nm][1], deltas[k], new_ms[k], new_vs[k])

    weight_order = ["w_mod", "b_mod", "norm_mix", "norm_mlp", "w_qkv", "b_qkv", "w_o", "b_o", "sinks", "w_pw1", "b_pw1",
                    "w_dw", "b_dw", "conv_ln_g", "conv_ln_b", "w_pw2", "b_pw2", "w_up", "w_down", "final_norm"]
    return (loss, grad_x, *[results[nm][0] for nm in weight_order], *[results[nm][1] for nm in weight_order],
            *[results[nm][2] for nm in weight_order], *[results[nm][3] for nm in weight_order])
```

```python
import functools

import jax
import jax.numpy as jnp
from jax import lax
from jax.experimental import pallas as pl
from jax.experimental.pallas import tpu as pltpu

F32, BF16 = jnp.float32, jnp.bfloat16
MESH = pl.DeviceIdType.MESH
N_CHIPS = 4
N_DEV = 8
LANES = 128
SUBLANES = 8
VMEM_LIMIT = 48 * 1024 * 1024

NORM_EPS = 1e-6
HEAD_DIM = 64
N_KV_HEADS = 2
WINDOW = 128
CONV_WIDTH = 31
CONV_HALO = 32
CONV_ROWS = 32
N_MOD = 6

ADAM_LR, ADAM_B1, ADAM_B2, ADAM_EPS, ADAM_WD, ADAM_STEP = 0.001, 0.9, 0.999, 1e-08, 0.01, 10

HBM_SPEC = pl.BlockSpec(memory_space=pltpu.HBM)
VMEM_SPEC = pl.BlockSpec(memory_space=pltpu.VMEM)
SMEM_SPEC = pl.BlockSpec(memory_space=pltpu.SMEM)


def _params(*sem):
    return pltpu.CompilerParams(dimension_semantics=sem or None, vmem_limit_bytes=VMEM_LIMIT)


def _row_tile(rows, width_bytes, target=2 << 20):
    t = rows
    while t % 2 == 0 and t > SUBLANES and t * width_bytes > target:
        t //= 2
    return t


CHIP_KINDS = ("x", "y", "xy")


def _position():
    x, y, c = lax.axis_index("x"), lax.axis_index("y"), lax.axis_index("c")
    return x, y, c, 2 * x + y


def _peer(kind, x, y, c):
    return {"c": (x, y, 1 - c), "x": (1 - x, y, c), "y": (x, 1 - y, c), "xy": (1 - x, 1 - y, c)}[kind]


def _peer_shard(kind, x, y):
    px, py, _ = _peer(kind, x, y, 0)
    return 2 * px + py


class _Plan:
    def __init__(self, link, cost, operands, out_shapes, copies, aliases=None, then=None):
        self.link, self.cost = link, cost
        self.operands, self.out_shapes, self.copies = list(operands), list(out_shapes), list(copies)
        self.aliases, self.then = dict(aliases or {}), then


def _merge(plans):
    operands, out_shapes, copies, aliases, thens = [], [], [], {}, []
    for p in plans:
        i0, o0 = len(operands), len(out_shapes)
        i1, o1 = i0 + len(p.operands), o0 + len(p.out_shapes)

        def shifted(f, i0=i0, i1=i1, o0=o0, o1=o1):
            return lambda ins, outs, pos: f(ins[i0:i1], outs[o0:o1], pos)

        copies += [(shifted(src), shifted(dst), shifted(land), kind) for src, dst, land, kind in p.copies]
        aliases.update({i0 + k: o0 + v for k, v in p.aliases.items()})
        operands += p.operands
        out_shapes += p.out_shapes
        thens.append((p.then, o0, o1))

    def then(outs):
        for f, o0, o1 in thens:
            if f is not None:
                f(outs[o0:o1])

    return _Plan("mixed", sum(p.cost for p in plans), operands, out_shapes, copies, aliases, then)


def _call(body, *, name, out_shape, operands, grid=(), in_specs=(), out_specs=(), scratch_shapes=(), sem=(), comm=None):
    single = not isinstance(out_shape, (tuple, list))
    out_shape = [out_shape] if single else list(out_shape)
    out_specs = [out_specs] if single else list(out_specs)
    if comm is None:
        res = pl.pallas_call(body, out_shape=out_shape, grid=grid, in_specs=list(in_specs), out_specs=out_specs,
                             scratch_shapes=list(scratch_shapes), name=name, compiler_params=_params(*sem))(*operands)
        return res[0] if single else res
    n_in, n_out, n_scr = len(operands), len(out_shape), len(scratch_shapes)
    c_in, c_out, n_cp = len(comm.operands), len(comm.out_shapes), len(comm.copies)

    def wrapped(*refs):
        ins, refs = refs[:n_in], refs[n_in:]
        cins, refs = refs[:c_in], refs[c_in:]
        outs, refs = refs[:n_out], refs[n_out:]
        couts, refs = refs[:c_out], refs[c_out:]
        scr, (send_sems, recv_sems) = refs[:n_scr], refs[n_scr:]

        def descriptors():
            pos = _position()
            sends, lands = [], []
            for k, (src, dst, landing, kind) in enumerate(comm.copies):
                common = dict(send_sem=send_sems.at[k], recv_sem=recv_sems.at[k],
                              device_id=_peer(kind, *pos[:3]), device_id_type=MESH)
                sends.append(pltpu.make_async_remote_copy(src_ref=src(cins, couts, pos), dst_ref=dst(cins, couts, pos), **common))
                lands.append(pltpu.make_async_remote_copy(src_ref=src(cins, couts, pos), dst_ref=landing(cins, couts, pos), **common))
            return sends, lands

        def start():
            for cp in descriptors()[0]:
                cp.start()

        def finish():
            sends, lands = descriptors()
            for cp in lands:
                cp.wait_recv()
            for cp in sends:
                cp.wait_send()

        if not grid:
            start()
            body(*ins, *outs, *scr)
            finish()
        else:
            ids = [pl.program_id(ax) for ax in range(len(grid))]
            first, last = ids[0] == 0, ids[0] == grid[0] - 1
            for ax in range(1, len(grid)):
                first, last = first & (ids[ax] == 0), last & (ids[ax] == grid[ax] - 1)
            pl.when(first)(start)
            body(*ins, *outs, *scr)
            pl.when(last)(finish)

    res = pl.pallas_call(wrapped, out_shape=out_shape + comm.out_shapes, grid=grid,
                         in_specs=list(in_specs) + [HBM_SPEC] * c_in, out_specs=out_specs + [HBM_SPEC] * c_out,
                         scratch_shapes=list(scratch_shapes) + [pltpu.SemaphoreType.DMA((n_cp,)), pltpu.SemaphoreType.DMA((n_cp,))],
                         input_output_aliases={n_in + k: n_out + v for k, v in comm.aliases.items()},
                         name=name, compiler_params=_params(*["arbitrary"] * len(grid)))(*operands, *comm.operands)
    if comm.then is not None:
        comm.then(res[n_out:])
    return res[0] if single else res[:n_out]


def _exchange(plan, *, name):
    _call(lambda: None, name=name, out_shape=[], operands=[], comm=plan)


def _mm(a, b, *, name, nt=False, tm=1024, tn=512, epi="plain", out_dtype=BF16,
        bias=None, act=None, resid=None, gate=None, seq=None, norm=None, comm=None):
    m, k = a.shape
    n = b.shape[0] if nt else b.shape[1]
    tm, tn = min(tm, m, seq or m), min(tn, n)
    assert m % tm == 0 and n % tn == 0
    dims = (((1,), (1,)), ((), ())) if nt else (((1,), (0,)), ((), ()))
    tile = pl.BlockSpec((tm, tn), lambda j, i: (i, j))
    operands = [a, b]
    in_specs = [pl.BlockSpec((tm, k), lambda j, i: (i, 0)),
                pl.BlockSpec((tn, k), lambda j, i: (j, 0)) if nt else pl.BlockSpec((k, tn), lambda j, i: (0, j))]
    if bias is not None:
        operands.append(bias.reshape(1, n))
        in_specs.append(pl.BlockSpec((1, tn), lambda j, i: (0, j)))
    if epi == "dact":
        operands.append(act)
        in_specs.append(tile)
    if epi == "resid":
        assert seq % tm == 0
        per_ex = pl.BlockSpec((None, 1, tn), lambda j, i: (i * tm // seq, 0, j))
        operands += [resid, gate]
        in_specs += [tile, per_ex]
        out_shape = (jax.ShapeDtypeStruct((m, n), BF16), jax.ShapeDtypeStruct((m, n), F32))
        out_specs = (tile, tile)
        if norm is not None:
            assert tn == n
            operands += list(norm)
            in_specs += [pl.BlockSpec((1, tn), lambda j, i: (0, j)), per_ex, per_ex]
            out_shape += (jax.ShapeDtypeStruct((m, n), BF16),)
            out_specs += (tile,)
    elif epi == "relu2":
        out_shape = (jax.ShapeDtypeStruct((m, n), BF16), jax.ShapeDtypeStruct((m, n), BF16))
        out_specs = (tile, tile)
    else:
        out_shape = jax.ShapeDtypeStruct((m, n), out_dtype)
        out_specs = tile

    def body(*refs):
        it = iter(refs)
        a_ref, b_ref = next(it), next(it)
        acc = lax.dot_general(a_ref[...], b_ref[...], dims, preferred_element_type=F32)
        if bias is not None:
            acc = acc + next(it)[...]
        if epi == "plain":
            next(it)[...] = acc.astype(out_dtype)
        elif epi == "relu2":
            r = jnp.maximum(acc, 0.0)
            next(it)[...] = (r * r).astype(BF16)
            next(it)[...] = (2.0 * r).astype(BF16)
        elif epi == "dact":
            slope_ref = next(it)
            next(it)[...] = (acc * slope_ref[...].astype(F32)).astype(out_dtype)
        else:
            resid_ref, gate_ref = next(it), next(it)
            norm_refs = [next(it) for _ in (norm or ())]
            y_ref, x_ref = next(it), next(it)
            y_ref[...] = acc.astype(BF16)
            x_new = resid_ref[...] + gate_ref[...] * acc
            x_ref[...] = x_new
            if norm is not None:
                g_ref, sc_ref, sh_ref = norm_refs
                r = lax.rsqrt(jnp.mean(x_new * x_new, axis=-1, keepdims=True) + NORM_EPS)
                next(it)[...] = (x_new * r * g_ref[...] * (1.0 + sc_ref[...]) + sh_ref[...]).astype(BF16)

    return _call(body, name=name, out_shape=out_shape, operands=operands, grid=(n // tn, m // tm), in_specs=in_specs,
                 out_specs=out_specs, sem=("parallel", "parallel"), comm=comm)


def _mm_tn(a, b, *, name, tm=1024, tn=1024, tk=2048, col_shards=None, comm=None):
    t, m = a.shape
    n = b.shape[1]
    tm, tk = min(tm, m), min(tk, t)
    if col_shards is None:
        tn = min(tn, n)
        out_shape = jax.ShapeDtypeStruct((m, n), F32)
        out_spec = pl.BlockSpec((tm, tn), lambda i, j, k: (i, j))
    else:
        per = n // col_shards
        tn = min(tn, per)
        assert per % tn == 0
        out_shape = jax.ShapeDtypeStruct((col_shards, m, per), F32)
        out_spec = pl.BlockSpec((None, tm, tn), lambda i, j, k: (j // (per // tn), i, j % (per // tn)))
    assert m % tm == 0 and n % tn == 0 and t % tk == 0

    def body(a_ref, b_ref, o_ref):
        @pl.when(pl.program_id(2) == 0)
        def _():
            o_ref[...] = jnp.zeros_like(o_ref)

        o_ref[...] += lax.dot_general(a_ref[...], b_ref[...], (((0,), (0,)), ((), ())),
                                      preferred_element_type=F32)

    return _call(body, name=name, out_shape=out_shape, operands=[a, b], grid=(m // tm, n // tn, t // tk),
                 in_specs=[pl.BlockSpec((tk, tm), lambda i, j, k: (k, i)), pl.BlockSpec((tk, tn), lambda i, j, k: (k, j))],
                 out_specs=out_spec, sem=("parallel", "parallel", "arbitrary"), comm=comm)


def _normmod(x, gamma, sc, sh, *, name):
    bsz, s, d = x.shape
    ts = _row_tile(s, d * 4, 2 << 20)

    def body(x_ref, g_ref, sc_ref, sh_ref, o_ref):
        xf = x_ref[...]
        r = lax.rsqrt(jnp.mean(xf * xf, axis=-1, keepdims=True) + NORM_EPS)
        o_ref[...] = (xf * r * g_ref[...] * (1.0 + sc_ref[...]) + sh_ref[...]).astype(BF16)

    row = pl.BlockSpec((None, ts, d), lambda b, i: (b, i, 0))
    per_ex = pl.BlockSpec((None, 1, d), lambda b, i: (b, 0, 0))
    return pl.pallas_call(body, out_shape=jax.ShapeDtypeStruct(x.shape, BF16), grid=(bsz, s // ts),
                          in_specs=[row, pl.BlockSpec((1, d), lambda b, i: (0, 0)), per_ex, per_ex],
                          out_specs=row, name=name, compiler_params=_params("parallel", "parallel"))(x, gamma, sc, sh)


def _gate_grads(dxv, y_ref, gate_ref, dy_ref, dg_ref, sdx_ref):
    dy_ref[...] = (gate_ref[...] * dxv).astype(BF16)

    @pl.when(pl.program_id(1) == 0)
    def _():
        dg_ref[...] = jnp.zeros_like(dg_ref)
        sdx_ref[...] = jnp.zeros_like(sdx_ref)

    dg_ref[...] += jnp.sum(dxv * y_ref[...].astype(F32), axis=0, keepdims=True)
    sdx_ref[...] += jnp.sum(dxv, axis=0, keepdims=True)


def _normmod_bwd(x, dh, dres, gamma, sc, producer=None, *, name, comm=None):
    bsz, s, d = x.shape
    ts = _row_tile(s, d * 4, 2 << 20)

    def body(x_ref, dh_ref, dres_ref, g_ref, sc_ref, *rest):
        dx_ref, p_ref, dsh_ref = rest[-6:-3] if producer else rest
        xf = x_ref[...]
        r = lax.rsqrt(jnp.mean(xf * xf, axis=-1, keepdims=True) + NORM_EPS)
        xhat = xf * r
        dh_v = dh_ref[...].astype(F32)
        dxhat = dh_v * (g_ref[...] * (1.0 + sc_ref[...]))
        dxv = dres_ref[...] + r * (dxhat - xhat * jnp.mean(dxhat * xhat, axis=-1, keepdims=True))
        dx_ref[...] = dxv

        @pl.when(pl.program_id(1) == 0)
        def _():
            p_ref[...] = jnp.zeros_like(p_ref)
            dsh_ref[...] = jnp.zeros_like(dsh_ref)

        p_ref[...] += jnp.sum(dh_v * xhat, axis=0, keepdims=True)
        dsh_ref[...] += jnp.sum(dh_v, axis=0, keepdims=True)
        if producer:
            _gate_grads(dxv, rest[0], rest[1], *rest[-3:])

    row = pl.BlockSpec((None, ts, d), lambda b, i: (b, i, 0))
    per_ex = pl.BlockSpec((None, 1, d), lambda b, i: (b, 0, 0))
    vec = jax.ShapeDtypeStruct((bsz, 1, d), F32)
    gate_in, gate_out, gate_specs = (), (), ()
    if producer:
        gate_in, gate_out, gate_specs = (row, per_ex), (jax.ShapeDtypeStruct(x.shape, BF16), vec, vec), (row, per_ex, per_ex)
    return _call(body, name=name, out_shape=(jax.ShapeDtypeStruct(x.shape, F32), vec, vec) + gate_out,
                 operands=[x, dh, dres, gamma, sc, *(producer or ())], grid=(bsz, s // ts),
                 in_specs=[row, row, row, pl.BlockSpec((1, d), lambda b, i: (0, 0)), per_ex, *gate_in],
                 out_specs=(row, per_ex, per_ex) + gate_specs, sem=("parallel", "arbitrary"), comm=comm)


def _loss_head(x, target, gamma, producer, *, name):
    bsz, s, d = x.shape
    ts = _row_tile(s, d * 4, 2 << 20)

    def body(x_ref, t_ref, g_ref, y_ref, gate_ref, dx_ref, loss_ref, dg_ref, *gate_refs):
        xf = x_ref[...]
        r = lax.rsqrt(jnp.mean(xf * xf, axis=-1, keepdims=True) + NORM_EPS)
        xhat = xf * r
        err = xhat * g_ref[...] - t_ref[...]
        dy = err * (1.0 / d)
        dxhat = dy * g_ref[...]
        dxv = r * (dxhat - xhat * jnp.mean(dxhat * xhat, axis=-1, keepdims=True))
        dx_ref[...] = dxv

        @pl.when((pl.program_id(0) == 0) & (pl.program_id(1) == 0))
        def _():
            loss_ref[...] = jnp.zeros_like(loss_ref)
            dg_ref[...] = jnp.zeros_like(dg_ref)

        loss_ref[...] += jnp.sum(err * err, axis=0, keepdims=True)
        dg_ref[...] += jnp.sum(dy * xhat, axis=0, keepdims=True)
        _gate_grads(dxv, y_ref, gate_ref, *gate_refs)

    row = pl.BlockSpec((None, ts, d), lambda b, i: (b, i, 0))
    one = pl.BlockSpec((1, d), lambda b, i: (0, 0))
    per_ex = pl.BlockSpec((None, 1, d), lambda b, i: (b, 0, 0))
    vec, ex_vec = jax.ShapeDtypeStruct((1, d), F32), jax.ShapeDtypeStruct((bsz, 1, d), F32)
    return pl.pallas_call(body, out_shape=(jax.ShapeDtypeStruct(x.shape, F32), vec, vec,
                                           jax.ShapeDtypeStruct(x.shape, BF16), ex_vec, ex_vec), grid=(bsz, s // ts),
                          in_specs=[row, row, one, row, per_ex], out_specs=(row, one, one, row, per_ex, per_ex), name=name,
                          compiler_params=_params("arbitrary", "arbitrary"))(x, target, gamma, *producer)


def _alibi_slope(h, n_heads):
    return 2.0 ** (-8.0 * (h + 1) / n_heads)


def _attn_masks(first_block):
    qi = lax.broadcasted_iota(jnp.int32, (WINDOW, 2 * WINDOW), 0)
    ki = lax.broadcasted_iota(jnp.int32, (WINDOW, 2 * WINDOW), 1)
    dist = qi + WINDOW - ki
    first_key = jnp.where(first_block, WINDOW, 0)
    valid = (dist >= 0) & (dist < WINDOW) & (ki >= first_key)
    return dist.astype(F32), valid


def _dup_halves(span, kv, left):
    f = span.astype(F32)
    rolled = pltpu.roll(f, HEAD_DIM, axis=1)
    out = jnp.where(left, f, rolled) if kv == 0 else jnp.where(left, rolled, f)
    return out.astype(BF16)


def _attn_probs(s, h, n_heads, distf, valid, sink):
    s = s * (HEAD_DIM ** -0.5) - _alibi_slope(h, n_heads) * distf
    s = jnp.where(valid, s, -1e30)
    m = jnp.maximum(jnp.max(s, axis=-1, keepdims=True), sink)
    e = jnp.exp(s - m)
    e_sink = jnp.exp(sink - m)
    inv = 1.0 / (jnp.sum(e, axis=-1, keepdims=True) + e_sink)
    return e * inv, e_sink * inv


def _attn_specs(d, n_blocks, clamp):
    kcol = d // LANES
    cur = (lambda i: jnp.minimum(i, n_blocks - 1)) if clamp else (lambda i: i)
    prev = lambda i: jnp.maximum(cur(i) - 1, 0)
    kv = lambda col, blk: pl.BlockSpec((None, WINDOW, LANES), lambda b, i: (b, blk(i), col))
    return [pl.BlockSpec((None, WINDOW, d), lambda b, i: (b, cur(i), 0)),
            kv(kcol, prev), kv(kcol, cur), kv(kcol + 1, prev), kv(kcol + 1, cur)]


def _attn_fwd(qkv, sinks, *, name, comm=None):
    bsz, s, qkv_dim = qkv.shape
    d = qkv_dim - 2 * N_KV_HEADS * HEAD_DIM
    n_heads = d // HEAD_DIM
    group = n_heads // N_KV_HEADS
    pairs = group // 2
    n_blocks = s // WINDOW

    def body(q_ref, kp_ref, kc_ref, vp_ref, vc_ref, sink_ref, o_ref):
        left = lax.broadcasted_iota(jnp.int32, (1, LANES), 1) < HEAD_DIM
        distf, valid = _attn_masks(pl.program_id(1) == 0)
        kspan = jnp.concatenate([kp_ref[...], kc_ref[...]], axis=0)
        vspan = jnp.concatenate([vp_ref[...], vc_ref[...]], axis=0)
        for kv in range(N_KV_HEADS):
            kdup, vdup = _dup_halves(kspan, kv, left), _dup_halves(vspan, kv, left)
            res = []
            for par in range(2):
                keep = left if par == 0 else jnp.logical_not(left)
                cols = [pl.ds((kv * pairs + p) * LANES, LANES) for p in range(pairs)]
                lhs = jnp.concatenate([jnp.where(keep, q_ref[:, cl], jnp.zeros((), BF16)) for cl in cols], axis=0)
                sc = lax.dot_general(lhs, kdup, (((1,), (1,)), ((), ())), preferred_element_type=F32)
                probs = []
                for p in range(pairs):
                    h = kv * group + 2 * p + par
                    pr, _ = _attn_probs(sc[p * WINDOW:(p + 1) * WINDOW], h, n_heads, distf, valid, sink_ref[h])
                    probs.append(pr.astype(BF16))
                res.append(jnp.dot(jnp.concatenate(probs, axis=0), vdup, preferred_element_type=F32))
            for p in range(pairs):
                rows = slice(p * WINDOW, (p + 1) * WINDOW)
                o_ref[:, pl.ds((kv * pairs + p) * LANES, LANES)] = jnp.where(left, res[0][rows], res[1][rows]).astype(BF16)

    return _call(body, name=name, out_shape=jax.ShapeDtypeStruct((bsz, s, d), BF16), operands=[qkv] * 5 + [sinks],
                 grid=(bsz, n_blocks), in_specs=_attn_specs(d, n_blocks, False) + [SMEM_SPEC],
                 out_specs=pl.BlockSpec((None, WINDOW, d), lambda b, i: (b, i, 0)), sem=("parallel", "parallel"), comm=comm)


def _attn_bwd(qkv, do, sinks, *, name, comm=None):
    bsz, s, qkv_dim = qkv.shape
    d = qkv_dim - 2 * N_KV_HEADS * HEAD_DIM
    n_heads = d // HEAD_DIM
    group = n_heads // N_KV_HEADS
    pairs = group // 2
    n_blocks = s // WINDOW
    tn_dims = (((0,), (0,)), ((), ()))

    def body(q_ref, kp_ref, kc_ref, vp_ref, vc_ref, do_ref, sink_ref, dqkv_ref, colsum_ref, dsink_ref,
             dq_prev, dk_carry, dv_carry):
        b, i = pl.program_id(0), pl.program_id(1)
        left = lax.broadcasted_iota(jnp.int32, (1, LANES), 1) < HEAD_DIM

        @pl.when((b == 0) & (i == 0))
        def _():
            colsum_ref[...] = jnp.zeros_like(colsum_ref)
            dsink_ref[...] = jnp.zeros_like(dsink_ref)

        @pl.when(i == 0)
        def _():
            dqkv_ref[...] = jnp.zeros_like(dqkv_ref)
            dk_carry[...] = jnp.zeros_like(dk_carry)
            dv_carry[...] = jnp.zeros_like(dv_carry)

        @pl.when(i > 0)
        def _():
            dq_v = dq_prev[...]
            dqkv_ref[:, pl.ds(0, d)] = dq_v.astype(BF16)
            colsum_ref[:, pl.ds(0, d)] += jnp.sum(dq_v, axis=0, keepdims=True)

        @pl.when(i < n_blocks)
        def _():
            distf, valid = _attn_masks(i == 0)
            kspan = jnp.concatenate([kp_ref[...], kc_ref[...]], axis=0)
            vspan = jnp.concatenate([vp_ref[...], vc_ref[...]], axis=0)
            dk_blk, dv_blk = [], []
            for kv in range(N_KV_HEADS):
                kdup, vdup = _dup_halves(kspan, kv, left), _dup_halves(vspan, kv, left)
                dq_res, dk_sum, dv_sum = [], None, None
                for par in range(2):
                    keep = left if par == 0 else jnp.logical_not(left)
                    cols = [pl.ds((kv * pairs + p) * LANES, LANES) for p in range(pairs)]
                    zero = jnp.zeros((), BF16)
                    lhs = jnp.concatenate([jnp.where(keep, q_ref[:, cl], zero) for cl in cols], axis=0)
                    dol = jnp.concatenate([jnp.where(keep, do_ref[:, cl], zero) for cl in cols], axis=0)
                    sc = lax.dot_general(lhs, kdup, (((1,), (1,)), ((), ())), preferred_element_type=F32)
                    dp = lax.dot_general(dol, vdup, (((1,), (1,)), ((), ())), preferred_element_type=F32)
                    probs, dscores = [], []
                    for p in range(pairs):
                        h = kv * group + 2 * p + par
                        rows = slice(p * WINDOW, (p + 1) * WINDOW)
                        pr, p_sink = _attn_probs(sc[rows], h, n_heads, distf, valid, sink_ref[h])
                        delta = jnp.sum(pr * dp[rows], axis=-1, keepdims=True)
                        dscores.append((pr * (dp[rows] - delta) * (HEAD_DIM ** -0.5)).astype(BF16))
                        probs.append(pr.astype(BF16))
                        dsink_ref[pl.ds(h, 1), :] += jnp.zeros((1, LANES), F32) - jnp.sum(p_sink * delta)
                    ds_all = jnp.concatenate(dscores, axis=0)
                    p_all = jnp.concatenate(probs, axis=0)
                    dq_res.append(jnp.dot(ds_all, kdup, preferred_element_type=F32))
                    dk_par = lax.dot_general(ds_all, lhs, tn_dims, preferred_element_type=F32)
                    dv_par = lax.dot_general(p_all, dol, tn_dims, preferred_element_type=F32)
                    dk_sum = dk_par if dk_sum is None else dk_sum + dk_par
                    dv_sum = dv_par if dv_sum is None else dv_sum + dv_par
                for p in range(pairs):
                    rows = slice(p * WINDOW, (p + 1) * WINDOW)
                    dq_prev[:, pl.ds((kv * pairs + p) * LANES, LANES)] = jnp.where(left, dq_res[0][rows], dq_res[1][rows])
                dk_blk.append(dk_sum + pltpu.roll(dk_sum, HEAD_DIM, axis=1))
                dv_blk.append(dv_sum + pltpu.roll(dv_sum, HEAD_DIM, axis=1))
            dk_span = jnp.where(left, dk_blk[0], dk_blk[1])
            dv_span = jnp.where(left, dv_blk[0], dv_blk[1])
            dk_done = dk_carry[...] + dk_span[:WINDOW]
            dv_done = dv_carry[...] + dv_span[:WINDOW]
            dk_carry[...] = dk_span[WINDOW:]
            dv_carry[...] = dv_span[WINDOW:]

            @pl.when(i > 0)
            def _():
                dqkv_ref[:, pl.ds(d, LANES)] = dk_done.astype(BF16)
                dqkv_ref[:, pl.ds(d + LANES, LANES)] = dv_done.astype(BF16)
                colsum_ref[:, pl.ds(d, LANES)] += jnp.sum(dk_done, axis=0, keepdims=True)
                colsum_ref[:, pl.ds(d + LANES, LANES)] += jnp.sum(dv_done, axis=0, keepdims=True)

        @pl.when(i == n_blocks)
        def _():
            dk_done, dv_done = dk_carry[...], dv_carry[...]
            dqkv_ref[:, pl.ds(d, LANES)] = dk_done.astype(BF16)
            dqkv_ref[:, pl.ds(d + LANES, LANES)] = dv_done.astype(BF16)
            colsum_ref[:, pl.ds(d, LANES)] += jnp.sum(dk_done, axis=0, keepdims=True)
            colsum_ref[:, pl.ds(d + LANES, LANES)] += jnp.sum(dv_done, axis=0, keepdims=True)

    do_spec = pl.BlockSpec((None, WINDOW, d), lambda b, i: (b, jnp.minimum(i, n_blocks - 1), 0))
    out_shape = (jax.ShapeDtypeStruct((bsz, s, qkv_dim), BF16), jax.ShapeDtypeStruct((1, qkv_dim), F32),
                 jax.ShapeDtypeStruct((n_heads, LANES), F32))
    out_specs = (pl.BlockSpec((None, WINDOW, qkv_dim), lambda b, i: (b, jnp.maximum(i - 1, 0), 0)),
                 pl.BlockSpec((1, qkv_dim), lambda b, i: (0, 0)),
                 pl.BlockSpec((n_heads, LANES), lambda b, i: (0, 0)))
    return _call(body, name=name, out_shape=out_shape, operands=[qkv] * 5 + [do, sinks], grid=(bsz, n_blocks + 1),
                 in_specs=_attn_specs(d, n_blocks, True) + [do_spec, SMEM_SPEC], out_specs=out_specs,
                 scratch_shapes=[pltpu.VMEM((WINDOW, d), F32), pltpu.VMEM((WINDOW, LANES), F32), pltpu.VMEM((WINDOW, LANES), F32)],
                 sem=("arbitrary", "arbitrary"), comm=comm)


def _conv_tile(s):
    return min(256, s)


def _halo_specs(ts, width, s):
    per = ts // CONV_HALO
    prev = pl.BlockSpec((None, CONV_HALO, width), lambda b, i: (b, jnp.maximum(i * per - 1, 0), 0))
    nxt = pl.BlockSpec((None, CONV_HALO, width), lambda b, i: (b, jnp.minimum((i + 1) * per, s // CONV_HALO - 1), 0))
    cur = pl.BlockSpec((None, ts, width), lambda b, i: (b, i, 0))
    return prev, cur, nxt


def _glu(u, d):
    return u[:, :d] * jax.nn.sigmoid(u[:, d:])


def _store_shifted(shifted, value):
    rows = value.shape[0]
    shifted[0] = value
    for b in range(1, SUBLANES):
        shifted[b] = pltpu.roll(value, rows - b, axis=0)


def _at(shifted, base, offset, n_rows, lanes):
    return shifted[offset % SUBLANES, pl.ds(base + (offset - offset % SUBLANES), n_rows), lanes]


def _taps(w_ref, shifted, out_ref, ts, d, offset):
    for r0 in range(0, ts, CONV_ROWS):
        for l0 in range(0, d, LANES):
            lanes = pl.ds(l0, LANES)
            acc = jnp.zeros((CONV_ROWS, LANES), F32)
            for j in range(CONV_WIDTH):
                acc = acc + w_ref[pl.ds(j, 1), lanes] * _at(shifted, r0, offset(j), CONV_ROWS, lanes)
            out_ref[pl.ds(r0, CONV_ROWS), lanes] = acc


def _conv_fwd(u, w_dw, b_dw, ln_g, ln_b, *, name, comm=None):
    bsz, s, d2 = u.shape
    d = d2 // 2
    ts = _conv_tile(s)

    def body(up_ref, uc_ref, w_ref, bdw_ref, g_ref, b_ref, z_ref, v_ref, gbuf):
        halo = jnp.where(pl.program_id(1) > 0, _glu(up_ref[...], d), 0.0)
        _store_shifted(gbuf, jnp.concatenate([halo, _glu(uc_ref[...], d)], axis=0))
        _taps(w_ref, gbuf, v_ref, ts, d, lambda j: CONV_HALO - (CONV_WIDTH - 1) + j)
        v = v_ref[...] + bdw_ref[...]
        v_ref[...] = v
        mu = jnp.mean(v, axis=-1, keepdims=True)
        cen = v - mu
        rstd = lax.rsqrt(jnp.mean(cen * cen, axis=-1, keepdims=True) + NORM_EPS)
        ln = cen * rstd * g_ref[...] + b_ref[...]
        z_ref[...] = (ln * jax.nn.sigmoid(ln)).astype(BF16)

    prev, cur, _ = _halo_specs(ts, d2, s)
    one = pl.BlockSpec((1, d), lambda b, i: (0, 0))
    row = pl.BlockSpec((None, ts, d), lambda b, i: (b, i, 0))
    return _call(body, name=name, out_shape=(jax.ShapeDtypeStruct((bsz, s, d), BF16), jax.ShapeDtypeStruct((bsz, s, d), F32)),
                 operands=[u, u, w_dw, b_dw, ln_g, ln_b], grid=(bsz, s // ts),
                 in_specs=[prev, cur, pl.BlockSpec((CONV_HALO, d), lambda b, i: (0, 0)), one, one, one],
                 out_specs=(row, row), scratch_shapes=[pltpu.VMEM((SUBLANES, ts + CONV_HALO, d), F32)],
                 sem=("parallel", "parallel"), comm=comm)


def _conv_bwd_ln(dz, v, ln_g, ln_b, *, name):
    bsz, s, d = v.shape
    ts = _row_tile(s, d * 4, 2 << 20)

    def body(dz_ref, v_ref, g_ref, b_ref, dv_ref, dg_ref, db_ref, dbdw_ref):
        v_v = v_ref[...]
        mu = jnp.mean(v_v, axis=-1, keepdims=True)
        cen = v_v - mu
        rstd = lax.rsqrt(jnp.mean(cen * cen, axis=-1, keepdims=True) + NORM_EPS)
        vhat = cen * rstd
        ln = vhat * g_ref[...] + b_ref[...]
        sig = jax.nn.sigmoid(ln)
        dln = dz_ref[...] * (sig * (1.0 + ln * (1.0 - sig)))
        dvhat = dln * g_ref[...]
        dv = rstd * (dvhat - jnp.mean(dvhat, axis=-1, keepdims=True)
                     - vhat * jnp.mean(dvhat * vhat, axis=-1, keepdims=True))
        dv_ref[...] = dv

        @pl.when((pl.program_id(0) == 0) & (pl.program_id(1) == 0))
        def _():
            dg_ref[...] = jnp.zeros_like(dg_ref)
            db_ref[...] = jnp.zeros_like(db_ref)
            dbdw_ref[...] = jnp.zeros_like(dbdw_ref)

        dg_ref[...] += jnp.sum(dln * vhat, axis=0, keepdims=True)
        db_ref[...] += jnp.sum(dln, axis=0, keepdims=True)
        dbdw_ref[...] += jnp.sum(dv, axis=0, keepdims=True)

    row = pl.BlockSpec((None, ts, d), lambda b, i: (b, i, 0))
    one = pl.BlockSpec((1, d), lambda b, i: (0, 0))
    vec = jax.ShapeDtypeStruct((1, d), F32)
    return pl.pallas_call(body, out_shape=(jax.ShapeDtypeStruct(v.shape, F32), vec, vec, vec), grid=(bsz, s // ts),
                          in_specs=[row, row, one, one], out_specs=(row, one, one, one), name=name,
                          compiler_params=_params("arbitrary", "arbitrary"))(dz, v, ln_g, ln_b)


def _conv_bwd_taps(dv, u, w_dw, *, name, comm=None):
    bsz, s, d = dv.shape
    ts = _conv_tile(s)
    n_tiles = s // ts

    def body(dvc_ref, dvn_ref, uc_ref, w_ref, du_ref, dbu_ref, dw_ref, dvbuf, dglu, glu):
        i = pl.program_id(1)

        @pl.when((pl.program_id(0) == 0) & (i == 0))
        def _():
            dbu_ref[...] = jnp.zeros_like(dbu_ref)
            dw_ref[...] = jnp.zeros_like(dw_ref)

        u_v = uc_ref[...]
        a, sig = u_v[:, :d], jax.nn.sigmoid(u_v[:, d:])
        glu[...] = a * sig
        ahead = jnp.where(i < n_tiles - 1, dvn_ref[...], 0.0)
        _store_shifted(dvbuf, jnp.concatenate([dvc_ref[...], ahead], axis=0))
        _taps(w_ref, dvbuf, dglu, ts, d, lambda j: CONV_WIDTH - 1 - j)
        for l0 in range(0, d, LANES):
            lanes = pl.ds(l0, LANES)
            for j in range(CONV_WIDTH):
                acc = jnp.zeros((SUBLANES, LANES), F32)
                for r0 in range(0, ts, CONV_ROWS):
                    prod = glu[pl.ds(r0, CONV_ROWS), lanes] * _at(dvbuf, r0, CONV_WIDTH - 1 - j, CONV_ROWS, lanes)
                    for k in range(0, CONV_ROWS, SUBLANES):
                        acc = acc + prod[k:k + SUBLANES]
                dw_ref[j, :, lanes] += acc
        dg_v = dglu[...]
        da = dg_v * sig
        dgate = dg_v * a * sig * (1.0 - sig)
        du_ref[:, pl.ds(0, d)] = da.astype(BF16)
        du_ref[:, pl.ds(d, d)] = dgate.astype(BF16)
        dbu_ref[:, pl.ds(0, d)] += jnp.sum(da, axis=0, keepdims=True)
        dbu_ref[:, pl.ds(d, d)] += jnp.sum(dgate, axis=0, keepdims=True)

    _, dv_cur, dv_next = _halo_specs(ts, d, s)
    _, u_cur, _ = _halo_specs(ts, 2 * d, s)
    out_shape = (jax.ShapeDtypeStruct((bsz, s, 2 * d), BF16), jax.ShapeDtypeStruct((1, 2 * d), F32),
                 jax.ShapeDtypeStruct((CONV_HALO, SUBLANES, d), F32))
    out_specs = (pl.BlockSpec((None, ts, 2 * d), lambda b, i: (b, i, 0)), pl.BlockSpec((1, 2 * d), lambda b, i: (0, 0)),
                 pl.BlockSpec((CONV_HALO, SUBLANES, d), lambda b, i: (0, 0, 0)))
    return _call(body, name=name, out_shape=out_shape, operands=[dv, dv, u, w_dw], grid=(bsz, n_tiles),
                 in_specs=[dv_cur, dv_next, u_cur, pl.BlockSpec((CONV_HALO, d), lambda b, i: (0, 0))],
                 out_specs=out_specs,
                 scratch_shapes=[pltpu.VMEM((SUBLANES, ts + CONV_HALO, d), F32), pltpu.VMEM((ts, d), F32),
                                 pltpu.VMEM((ts, d), F32)],
                 sem=("arbitrary", "arbitrary"), comm=comm)


def _mod_fwd(c_all, w_mod, b_mod, *, name):
    n_layers, d, n = w_mod.shape
    rows = c_all.shape[0]

    def body(c_ref, w_ref, b_ref, o_ref):
        cv = c_ref[...]
        cs = (cv * jax.nn.sigmoid(cv)).astype(BF16)
        o_ref[...] = jnp.dot(cs, w_ref[...].astype(BF16), preferred_element_type=F32) + b_ref[...]

    return pl.pallas_call(body, out_shape=jax.ShapeDtypeStruct((n_layers, rows, n), F32), grid=(n_layers,),
                          in_specs=[pl.BlockSpec((rows, d), lambda l: (0, 0)), pl.BlockSpec((None, d, n), lambda l: (l, 0, 0)),
                                    pl.BlockSpec((None, 1, n), lambda l: (l, 0, 0))],
                          out_specs=pl.BlockSpec((None, rows, n), lambda l: (l, 0, 0)), name=name,
                          compiler_params=_params("parallel"))(c_all, w_mod, b_mod)


def _mod_bwd(c_all, dmod, *, name):
    n_layers, rows, n = dmod.shape
    d = c_all.shape[1]

    def body(c_ref, g_ref, o_ref):
        cv = c_ref[...]
        cs = (cv * jax.nn.sigmoid(cv)).astype(BF16)
        o_ref[...] = lax.dot_general(cs, g_ref[...].astype(BF16), (((0,), (0,)), ((), ())), preferred_element_type=F32)

    return pl.pallas_call(body, out_shape=jax.ShapeDtypeStruct((n_layers, d, n), F32), grid=(n_layers,),
                          in_specs=[pl.BlockSpec((rows, d), lambda l: (0, 0)), pl.BlockSpec((None, rows, n), lambda l: (l, 0, 0))],
                          out_specs=pl.BlockSpec((None, d, n), lambda l: (l, 0, 0)), name=name,
                          compiler_params=_params("parallel"))(c_all, dmod)


def _add_half(g, recv, my_c, *, name):
    p, _, r, cdim = g.shape
    tr = _row_tile(r, cdim * 4)

    def body(c_ref, g_ref, r_ref, o_ref):
        o_ref[...] = (g_ref[...] + r_ref[...]).astype(BF16)

    grid_spec = pltpu.PrefetchScalarGridSpec(
        num_scalar_prefetch=1, grid=(p, r // tr),
        in_specs=[pl.BlockSpec((None, None, tr, cdim), lambda q, i, c_ref: (q, c_ref[0], i, 0)),
                  pl.BlockSpec((None, None, tr, cdim), lambda q, i, c_ref: (q, 0, i, 0))],
        out_specs=pl.BlockSpec((None, tr, cdim), lambda q, i, c_ref: (q, i, 0)))
    return pl.pallas_call(body, out_shape=jax.ShapeDtypeStruct((p, r, cdim), BF16), grid_spec=grid_spec, name=name,
                          compiler_params=_params("parallel", "parallel"))(my_c, g, recv)


def _add_pieces(chip, recv, my_qc, stacked, layer, n_layers, *, name):
    _, r, cdim = chip.shape
    tr = _row_tile(r, cdim * 4)

    def body(qc_ref, own_ref, r0_ref, r1_ref, r2_ref, *rest):
        f32 = lambda ref: ref[...].astype(F32)
        rest[-1][...] = ((f32(own_ref) + f32(r0_ref)) + f32(r1_ref)) + f32(r2_ref)

    piece = lambda k: pl.BlockSpec((None, tr, cdim), lambda i, qc_ref: (k, i, 0))
    in_specs = [pl.BlockSpec((None, tr, cdim), lambda i, qc_ref: (qc_ref[0], i, 0)), piece(0), piece(1), piece(2)]
    operands = [my_qc, chip, recv, recv, recv]
    aliases = {}
    if stacked is not None:
        in_specs.append(pl.BlockSpec(memory_space=pl.ANY))
        operands.append(stacked)
        aliases = {len(operands) - 1: 0}
    grid_spec = pltpu.PrefetchScalarGridSpec(
        num_scalar_prefetch=1, grid=(r // tr,), in_specs=in_specs,
        out_specs=pl.BlockSpec((None, None, tr, cdim), lambda i, qc_ref: (layer, qc_ref[1], i, 0)))
    return pl.pallas_call(body, out_shape=jax.ShapeDtypeStruct((n_layers, 2, r, cdim), F32), grid_spec=grid_spec,
                          input_output_aliases=aliases, name=name, compiler_params=_params("parallel"))(*operands)


def _sum_devices(parts, dmod, *, name):
    def body(p_ref, m_ref, o_ref, b_ref):
        acc = p_ref[0]
        for k in range(1, N_DEV):
            acc = acc + p_ref[k]
        o_ref[...] = acc
        tot = None
        for k in range(N_DEV):
            for e in range(dmod.shape[2]):
                tot = m_ref[k, :, e] if tot is None else tot + m_ref[k, :, e]
        b_ref[...] = tot

    out_shape = (jax.ShapeDtypeStruct(parts.shape[1:], F32),
                 jax.ShapeDtypeStruct((dmod.shape[1],) + dmod.shape[3:], F32))
    return pl.pallas_call(body, out_shape=out_shape, in_specs=[VMEM_SPEC, VMEM_SPEC], out_specs=(VMEM_SPEC, VMEM_SPEC),
                          name=name, compiler_params=_params())(parts, dmod)


def _adamw(w, g, m, v, *, name, comm=None):
    r, cdim = w.shape
    tr = _row_tile(r, cdim * 4, 2 << 20)

    def body(w_ref, g_ref, m_ref, v_ref, d_ref, nm_ref, nv_ref):
        gv = g_ref[...]
        nm = ADAM_B1 * m_ref[...] + (1.0 - ADAM_B1) * gv
        nv = ADAM_B2 * v_ref[...] + (1.0 - ADAM_B2) * (gv * gv)
        m_hat = nm / (1.0 - ADAM_B1 ** ADAM_STEP)
        v_hat = nv / (1.0 - ADAM_B2 ** ADAM_STEP)
        d_ref[...] = -ADAM_LR * (m_hat / (jnp.sqrt(v_hat) + ADAM_EPS) + ADAM_WD * w_ref[...])
        nm_ref[...] = nm
        nv_ref[...] = nv

    row = pl.BlockSpec((tr, cdim), lambda i: (i, 0))
    shape = jax.ShapeDtypeStruct((r, cdim), F32)
    return _call(body, name=name, out_shape=(shape, shape, shape), operands=[w, g, m, v], grid=(r // tr,),
                 in_specs=[row] * 4, out_specs=(row, row, row), sem=("parallel",), comm=comm)


def _all_gather_small(block, *, name, comm=None):
    m_per, n = block.shape

    def body(x_ref, out_ref, send_sems, recv_sems, local_sem):
        x, y, c, _ = _position()
        me, sibling = (x, y, c), (x, y, 1 - c)
        chips = [_peer(k, x, y, c)[:2] for k in CHIP_KINDS]

        def rows(px, py, pc):
            return out_ref.at[pl.ds((4 * px + 2 * py + pc) * m_per, m_per), :]

        def copy(k, blk, to, src=None):
            return pltpu.make_async_remote_copy(src_ref=rows(*blk) if src is None else src, dst_ref=rows(*blk),
                                                send_sem=send_sems.at[k], recv_sem=recv_sems.at[k],
                                                device_id=to, device_id_type=MESH)

        mine = pltpu.make_async_copy(x_ref, rows(*me), local_sem)
        mine.start()
        first = [copy(0, me, sibling, src=x_ref)]
        first += [copy(1 + j, me, (*chip, c), src=x_ref) for j, chip in enumerate(chips)]
        for cp in first:
            cp.start()
        passed = [copy(4 + j, (*chip, c), sibling) for j, chip in enumerate(chips)]
        for j, chip in enumerate(chips):
            copy(1 + j, (*chip, c), me).wait_recv()
            passed[j].start()
        copy(0, sibling, me).wait_recv()
        for j, chip in enumerate(chips):
            copy(4 + j, (*chip, 1 - c), me).wait_recv()
        for cp in first + passed:
            cp.wait_send()
        mine.wait()

    return _call(body, name=name, out_shape=jax.ShapeDtypeStruct((N_DEV * m_per, n), block.dtype), operands=[block],
                 in_specs=[VMEM_SPEC], out_specs=VMEM_SPEC,
                 scratch_shapes=[pltpu.SemaphoreType.DMA((7,)), pltpu.SemaphoreType.DMA((7,)), pltpu.SemaphoreType.DMA],
                 comm=comm)


def _pack(arrays, width):
    flat = jnp.concatenate([a.reshape(-1).astype(F32) for a in arrays])
    rows = -(-flat.shape[0] // width)
    rows = -(-rows // SUBLANES) * SUBLANES
    return jnp.pad(flat, (0, rows * width - flat.shape[0])).reshape(rows, width)


def _unpack(packed, shapes):
    flat, out, off = packed.reshape(-1), [], 0
    for shp in shapes:
        size = 1
        for dim in shp:
            size *= dim
        out.append(flat[off:off + size].reshape(shp))
        off += size
    return out


def _adamw_packed(ws, gs, ms, vs, width, *, name):
    shapes = [w.shape for w in ws]
    res = _adamw(_pack(ws, width), _pack(gs, width), _pack(ms, width), _pack(vs, width), name=name)
    return [_unpack(r, shapes) for r in res]


MB = float(1 << 20)


def _nbytes(shape, dtype):
    size = jnp.dtype(dtype).itemsize
    for dim in shape:
        size *= dim
    return size


def _gather_plans(shard, side_by_side, on_ready):
    r, cdim = shard.shape
    half = r // 2
    if side_by_side:
        assert cdim % LANES == 0
        buf = jax.ShapeDtypeStruct((r, N_CHIPS * cdim), shard.dtype)
        rows = lambda ref, q, h: ref.at[pl.ds(h * half, half), pl.ds(q * cdim, cdim)]
        whole = lambda ref, q: ref.at[:, pl.ds(q * cdim, cdim)]
        src, src_rows = shard, (lambda ref, h: ref.at[pl.ds(h * half, half)])
    else:
        buf = jax.ShapeDtypeStruct((N_CHIPS, r, cdim), shard.dtype)
        rows = lambda ref, q, h: ref.at[pl.ds(q, 1), pl.ds(h * half, half)]
        whole = lambda ref, q: ref.at[pl.ds(q, 1)]
        src, src_rows = shard[None], (lambda ref, h: ref.at[:, pl.ds(h * half, half)])
    there = lambda kind, pos: _peer_shard(kind, pos[0], pos[1])
    cost = 3 * _nbytes((half, cdim), shard.dtype) / MB

    def forward(outs):
        copies = [(lambda ins, outs, pos, kind=kind: rows(ins[0], there(kind, pos), pos[2]),
                   lambda ins, outs, pos, kind=kind: rows(outs[0], there(kind, pos), pos[2]),
                   lambda ins, outs, pos, kind=kind: rows(outs[0], there(kind, pos), 1 - pos[2]), "c") for kind in CHIP_KINDS]
        return _Plan("d2d", cost, [outs[0]], [jax.ShapeDtypeStruct(buf.shape, buf.dtype)], copies, {0: 0},
                     lambda done: on_ready(done[0]))

    copies = [(lambda ins, outs, pos: src_rows(ins[0], pos[2]),
               lambda ins, outs, pos: rows(outs[0], pos[3], pos[2]),
               lambda ins, outs, pos, kind=kind: rows(outs[0], there(kind, pos), pos[2]), kind) for kind in CHIP_KINDS]
    own = lambda ins, outs, pos: whole(outs[0], pos[3])
    copies.append((lambda ins, outs, pos: ins[0], own, own, "c"))
    return _Plan("ici", cost, [src], [buf], copies), forward


class _Exchanges:
    def __init__(self):
        self.queue = []

    def add(self, plan, front=False):
        if front:
            self.queue.insert(0, plan)
        else:
            self.queue.append(plan)

    def take(self, budget_mb, at_least_one=False):
        chosen, spent = [p for p in self.queue if p.link == "d2d"], 0.0
        for p in self.queue:
            if p.link == "ici" and (spent + p.cost <= budget_mb or (at_least_one and spent == 0.0)):
                chosen.append(p)
                spent += p.cost
        if not chosen:
            return None
        self.queue = [p for p in self.queue if all(p is not ch for ch in chosen)]
        return _merge(chosen)

    def flush(self, budget_mb, until=lambda: False):
        while self.queue and not until():
            _exchange(self.take(budget_mb, at_least_one=True), name="exchange")


def kernel(x, c, w_mod, b_mod, norm_mix, norm_mlp, w_qkv, b_qkv, w_o, b_o, sinks, w_pw1, b_pw1, w_dw, b_dw, conv_ln_g, conv_ln_b, w_pw2, b_pw2, w_up, w_down, final_norm, loss_target, m_w_mod, m_b_mod, m_norm_mix, m_norm_mlp, m_w_qkv, m_b_qkv, m_w_o, m_b_o, m_sinks, m_w_pw1, m_b_pw1, m_w_dw, m_b_dw, m_conv_ln_g, m_conv_ln_b, m_w_pw2, m_b_pw2, m_w_up, m_w_down, m_final_norm, v_w_mod, v_b_mod, v_norm_mix, v_norm_mlp, v_w_qkv, v_b_qkv, v_w_o, v_b_o, v_sinks, v_w_pw1, v_b_pw1, v_w_dw, v_b_dw, v_conv_ln_g, v_conv_ln_b, v_w_pw2, v_b_pw2, v_w_up, v_w_down, v_final_norm):
    bsz, s, d = x.shape
    t = bsz * s
    depth = w_mod.shape[0]
    n_attn, n_conv = w_qkv.shape[0], w_pw1.shape[0]
    qkv_dim = d + 2 * N_KV_HEADS * HEAD_DIM
    mx, my, mc, mq = _position()
    me = 4 * mx + 2 * my + mc
    my_c = jnp.reshape(mc, (1,)).astype(jnp.int32)
    my_qc = jnp.stack([mq, mc]).astype(jnp.int32)
    pending = _Exchanges()
    SMALL, MEDIUM, LARGE = 2.5, 3.5, 6.5

    weights = {}
    order = []
    for i in range(depth):
        j = i // 2
        order += ([(("qkv", j), w_qkv[j], True), (("o", j), w_o[j], False)] if i % 2 == 0 else
                  [(("pw1", j), w_pw1[j], True), (("pw2", j), w_pw2[j], False)])
        order += [(("up", i), w_up[i], True), (("down", i), w_down[i], False)]
    for key, shard, by_cols in order:
        side_by_side = by_cols and shard.shape[1] % LANES == 0

        def ready(buf, key=key, by_cols=by_cols):
            if buf.ndim == 3:
                buf = jnp.transpose(buf, (1, 0, 2)).reshape(buf.shape[1], -1) if by_cols else buf.reshape(-1, buf.shape[2])
            weights[key] = buf
        ici, forward = _gather_plans(shard.astype(BF16), side_by_side, ready)
        ici.then = lambda outs, forward=forward: pending.add(forward(outs), front=True)
        pending.add(ici)

    def weight(key):
        pending.flush(SMALL, until=lambda: key in weights)
        return weights[key]

    small_sharded = [b_pw1, w_dw, b_dw, conv_ln_g, conv_ln_b, b_pw2]
    c_pad = jnp.pad(c, ((0, SUBLANES - bsz), (0, 0)))
    gathered = _all_gather_small(jnp.concatenate([c_pad, _pack(small_sharded, d)], axis=0), name="gather_c",
                                 comm=pending.take(SMALL, True))
    gathered = gathered.reshape(N_DEV, -1, d)
    c_all = gathered[:, :bsz].reshape(N_DEV * bsz, d)
    per_chip = [_unpack(gathered[2 * q, SUBLANES:], [a.shape for a in small_sharded]) for q in range(N_CHIPS)]
    b_pw1_f, w_dw_f, b_dw_f, ln_g_f, ln_b_f, b_pw2_f = [jnp.concatenate([per_chip[q][k] for q in range(N_CHIPS)], axis=-1)
                                                           for k in range(len(small_sharded))]
    w_dw_f = jnp.pad(w_dw_f, ((0, 0), (0, CONV_HALO - CONV_WIDTH), (0, 0)))

    n_mod = w_mod.shape[2]
    b_mod_cols = lax.dynamic_slice_in_dim(b_mod, mq * n_mod, n_mod, axis=1).reshape(depth, 1, n_mod)
    c_all8 = gathered[:, :SUBLANES].reshape(N_DEV * SUBLANES, d)
    mod_part = _mod_fwd(c_all8, w_mod, b_mod_cols, name="mod_fwd")
    peer_rows = lambda kind, pos: pl.ds(SUBLANES * (2 * _peer_shard(kind, pos[0], pos[1]) + pos[2]), SUBLANES)
    got = {}
    pending.add(_Plan("ici", 0.01, [mod_part],[jax.ShapeDtypeStruct((len(CHIP_KINDS), depth, SUBLANES, n_mod), F32)],
                      [(lambda ins, outs, pos, kind=kind: ins[0].at[:, peer_rows(kind, pos)],
                        lambda ins, outs, pos, k=k: outs[0].at[k], lambda ins, outs, pos, k=k: outs[0].at[k], kind)
                       for k, kind in enumerate(CHIP_KINDS)], then=lambda outs: got.update(rows=outs[0])), front=True)
    _exchange(pending.take(0.0, at_least_one=True), name="mod_exchange")
    own = lax.dynamic_slice_in_dim(mod_part, me * SUBLANES, SUBLANES, axis=1)
    parts = [own, got["rows"][0], got["rows"][1], got["rows"][2]]
    part_of = (0, 2, 1, 3)

    def shard(sidx):
        dist, out = jnp.bitwise_xor(mq, sidx), parts[0]
        for distance in (1, 2, 3):
            out = jnp.where(dist == distance, parts[part_of[distance]], out)
        return out

    mod = jnp.stack([shard(sidx) for sidx in range(N_CHIPS)])[:, :, :bsz]
    mod = jnp.transpose(mod, (1, 2, 0, 3)).reshape(depth, bsz, N_MOD, 1, d)
    mods = [[mod[i][:, k] for k in range(N_MOD)] for i in range(depth)]

    saved = []
    xc = x
    h1 = _normmod(xc, norm_mix[0][None], mods[0][1], mods[0][0], name="normmod")
    for i in range(depth):
        j = i // 2
        sh1, sc1, g1, sh2, sc2, g2 = mods[i]
        mlp_norm = (norm_mlp[i][None], sc2, sh2)
        if i % 2 == 0:
            wq, wo = weight(("qkv", j)), weight(("o", j))
            qkv = _mm(h1.reshape(t, d), wq, bias=b_qkv[j], tn=qkv_dim, name="mm_qkv",
                      comm=pending.take(SMALL)).reshape(bsz, s, qkv_dim)
            mix = _attn_fwd(qkv, sinks[j], name="attn_fwd", comm=pending.take(LARGE, True))
            y1, x1, h2 = _mm(mix.reshape(t, d), wo, bias=b_o[j], epi="resid", resid=xc.reshape(t, d), gate=g1, seq=s,
                             norm=mlp_norm, tn=d, name="mm_out", comm=pending.take(SMALL))
            extra = (qkv, mix)
        else:
            wp1, wp2 = weight(("pw1", j)), weight(("pw2", j))
            u = _mm(h1.reshape(t, d), wp1, bias=b_pw1_f[j], out_dtype=F32, tn=d, name="mm_pw1",
                    comm=pending.take(MEDIUM)).reshape(bsz, s, 2 * d)
            mix, conv_v = _conv_fwd(u, w_dw_f[j], b_dw_f[j][None], ln_g_f[j][None], ln_b_f[j][None], name="conv_fwd",
                                    comm=pending.take(LARGE, True))
            y1, x1, h2 = _mm(mix.reshape(t, d), wp2, bias=b_pw2_f[j], epi="resid", resid=xc.reshape(t, d), gate=g1, seq=s,
                             norm=mlp_norm, tn=d, name="mm_out", comm=pending.take(SMALL))
            extra = (u, mix, conv_v)
        act, slope = _mm(h2, weight(("up", i)), epi="relu2", tm=1024, tn=2 * d, name="mm_up", comm=pending.take(LARGE, True))
        mix_norm = (norm_mix[i + 1][None], mods[i + 1][1], mods[i + 1][0]) if i + 1 < depth else None
        y2, x2, *h_next = _mm(act, weight(("down", i)), epi="resid", resid=x1, gate=g2, seq=s, norm=mix_norm,
                              tm=512, tn=d, name="mm_down", comm=pending.take(LARGE, True))
        saved.append((xc, h1, extra, y1, x1.reshape(bsz, s, d), h2, act, y2, slope))
        xc = x2.reshape(bsz, s, d)
        h1 = h_next[0] if h_next else None
    pending.flush(LARGE)

    last_y2, last_g2 = saved[-1][7].reshape(bsz, s, d), mods[-1][5]
    dx, loss_cols, d_final, dyb, dg2, _ = _loss_head(xc, loss_target, final_norm[None], (last_y2, last_g2), name="loss_head")
    loss = lax.psum(0.5 / d * jnp.sum(loss_cols), ("x", "y", "c"))

    totals = {}

    def reduce_scatter(name, layer, n_layers, grad):
        view = grad.reshape(N_CHIPS, 2, grad.shape[1] // 2, grad.shape[2])
        half_shape = (N_CHIPS, 1) + view.shape[2:]

        def scatter(outs):
            chip = _add_half(view, outs[0], my_c, name="rs_chipsum")
            copies = [(lambda ins, outs, pos, kind=kind: ins[0].at[pl.ds(_peer_shard(kind, pos[0], pos[1]), 1)],
                       lambda ins, outs, pos, k=k: outs[0].at[pl.ds(k, 1)],
                       lambda ins, outs, pos, k=k: outs[0].at[pl.ds(k, 1)], kind) for k, kind in enumerate(CHIP_KINDS)]

            def total(outs):
                totals[name] = _add_pieces(chip, outs[0], my_qc, totals.get(name), layer, n_layers, name="rs_total")

            pending.add(_Plan("ici", 3 * _nbytes(chip.shape[1:], BF16) / MB, [chip],
                              [jax.ShapeDtypeStruct((3,) + chip.shape[1:], BF16)], copies, then=total))

        pending.add(_Plan("d2d", _nbytes(half_shape, F32) / MB, [view], [jax.ShapeDtypeStruct(half_shape, F32)],
                          [(lambda ins, outs, pos: ins[0].at[:, pl.ds(1 - pos[2], 1)],
                            lambda ins, outs, pos: outs[0], lambda ins, outs, pos: outs[0], "c")], then=scatter))

    dmods, small = [None] * depth, {}
    for i in reversed(range(depth)):
        j = i // 2
        xin, h1, extra, y1, x1, h2, act, _, slope = saved[i]
        sh1, sc1, g1, sh2, sc2, g2 = mods[i]
        dyb = dyb.reshape(t, d)
        reduce_scatter("down", i, depth,
                       _mm_tn(act, dyb, name="dw_down", comm=pending.take(LARGE)).reshape(N_CHIPS, -1, d))
        dup = _mm(dyb, weights["down", i], nt=True, epi="dact", act=slope, tm=1024, tn=2 * d, name="mm_dact",
                  comm=pending.take(LARGE))
        reduce_scatter("up", i, depth,
                       _mm_tn(h2.reshape(t, d), dup, col_shards=N_CHIPS, name="dw_up", comm=pending.take(LARGE)))
        dh2 = _mm(dup, weights["up", i], nt=True, tm=512, tn=d, name="mm_dh2",
                  comm=pending.take(LARGE)).reshape(bsz, s, d)
        dx1, p2, dsh2, dyb, dg1, sdx = _normmod_bwd(x1, dh2, dx, norm_mlp[i][None], sc2, (y1.reshape(bsz, s, d), g1),
                                                    name="normmod_bwd", comm=pending.take(SMALL))
        dyb = dyb.reshape(t, d)
        d_bias_out = jnp.sum(g1 * sdx, axis=(0, 1))
        if i % 2 == 0:
            qkv, mix = extra
            small["b_o", j] = d_bias_out
            reduce_scatter("o", j, n_attn,
                           _mm_tn(mix.reshape(t, d), dyb, name="dw_sq", comm=pending.take(SMALL)).reshape(N_CHIPS, -1, d))
            dmix = _mm(dyb, weights["o", j], nt=True, tn=d, name="mm_dmix", comm=pending.take(SMALL)).reshape(bsz, s, d)
            dqkv, d_bqkv, d_sink = _attn_bwd(qkv, dmix, sinks[j], name="attn_bwd", comm=pending.take(2 * LARGE))
            small["b_qkv", j], small["sinks", j] = d_bqkv[0], d_sink[:, 0]
            dqkv = dqkv.reshape(t, qkv_dim)
            dwq = _mm_tn(h1.reshape(t, d), dqkv, tn=qkv_dim, name="dw_qkv", comm=pending.take(SMALL))
            reduce_scatter("qkv", j, n_attn, jnp.transpose(dwq.reshape(d, N_CHIPS, -1), (1, 0, 2)))
            dh1 = _mm(dqkv, weights["qkv", j], nt=True, tn=d, name="mm_dh1a", comm=pending.take(SMALL))
        else:
            u, mix, conv_v = extra
            small["b_pw2", j] = d_bias_out
            reduce_scatter("pw2", j, n_conv,
                           _mm_tn(mix.reshape(t, d), dyb, name="dw_sq", comm=pending.take(SMALL)).reshape(N_CHIPS, -1, d))
            dz = _mm(dyb, weights["pw2", j], nt=True, out_dtype=F32, tn=d, name="mm_dz", comm=pending.take(SMALL)).reshape(bsz, s, d)
            dv, d_lng, d_lnb, d_bdw = _conv_bwd_ln(dz, conv_v, ln_g_f[j][None], ln_b_f[j][None], name="conv_bwd_ln")
            du, d_bpw1, d_wdw = _conv_bwd_taps(dv, u, w_dw_f[j], name="conv_bwd_taps", comm=pending.take(2 * LARGE))
            small["ln_g", j], small["ln_b", j], small["b_dw", j] = d_lng[0], d_lnb[0], d_bdw[0]
            small["b_pw1", j], small["w_dw", j] = d_bpw1[0], jnp.sum(d_wdw[:CONV_WIDTH], axis=1)
            du = du.reshape(t, 2 * d)
            reduce_scatter("pw1", j, n_conv,
                           _mm_tn(h1.reshape(t, d), du, col_shards=N_CHIPS, name="dw_pw1", comm=pending.take(MEDIUM)))
            dh1 = _mm(du, weights["pw1", j], nt=True, tn=d, name="mm_dh1c", comm=pending.take(MEDIUM))
        below = (saved[i - 1][7].reshape(bsz, s, d), mods[i - 1][5]) if i > 0 else None
        dx, p1, dsh1, *gate_grads = _normmod_bwd(xin, dh1.reshape(bsz, s, d), dx1, norm_mix[i][None], sc1, below,
                                                 name="normmod_bwd", comm=pending.take(SMALL, i == 0))
        small["norm_mix", i] = jnp.sum((1.0 + sc1) * p1, axis=(0, 1))
        small["norm_mlp", i] = jnp.sum((1.0 + sc2) * p2, axis=(0, 1))
        dmods[i] = jnp.concatenate([dsh1, norm_mix[i] * p1, dg1, dsh2, norm_mlp[i] * p2, dg2], axis=1)
        if i > 0:
            dyb, dg2, _ = gate_grads
    grad_x = dx

    small_names = ([("norm_mix", i) for i in range(depth)] + [("norm_mlp", i) for i in range(depth)]
                   + [(nm, j) for nm in ("b_qkv", "b_o", "sinks") for j in range(n_attn)]
                   + [(nm, j) for nm in ("b_pw1", "w_dw", "b_dw", "ln_g", "ln_b", "b_pw2") for j in range(n_conv)])
    small_list = [small[k] for k in small_names] + [d_final[0]]
    small_pack = _pack(small_list, d)
    dmod_rows = jnp.stack(dmods).reshape(depth * bsz * N_MOD, d)
    n_small = small_pack.shape[0]
    pending.flush(LARGE)
    big_names = ["qkv", "o", "pw1", "pw2", "up", "down"]
    bufs = [totals[nm] for nm in big_names]
    copies = []
    for a, buf in enumerate(bufs):
        for layer in range(buf.shape[0]):
            half = lambda ref, h, layer=layer: ref.at[pl.ds(layer, 1), pl.ds(h, 1)]
            copies.append((lambda ins, outs, pos, a=a, half=half: half(ins[a], pos[2]),
                           lambda ins, outs, pos, a=a, half=half: half(outs[a], pos[2]),
                           lambda ins, outs, pos, a=a, half=half: half(outs[a], 1 - pos[2]), "c"))
    shared = {}
    share = _Plan("d2d", 0.0, bufs, [jax.ShapeDtypeStruct(b.shape, F32) for b in bufs], copies,
                  {a: a for a in range(len(bufs))}, lambda outs: shared.update(zip(big_names, outs)))
    gathered = _all_gather_small(jnp.concatenate([small_pack, _pack([dmod_rows], d)], axis=0), name="gather_small",
                                 comm=share)
    gathered = gathered.reshape(N_DEV, -1, d)
    dmod_all = gathered[:, n_small:n_small + depth * bsz * N_MOD].reshape(N_DEV, depth, bsz, N_MOD, d)
    small_sum, g_b_mod = _sum_devices(gathered[:, :n_small], dmod_all, name="sum_devices")
    small_tot = dict(zip(small_names + ["final_norm"], _unpack(small_sum, [a.shape for a in small_list])))
    stacked = lambda nm, count: jnp.stack([small_tot[nm, k] for k in range(count)])
    g_norm_mix, g_norm_mlp = stacked("norm_mix", depth), stacked("norm_mlp", depth)
    g_b_qkv, g_b_o, g_sinks = stacked("b_qkv", n_attn), stacked("b_o", n_attn), stacked("sinks", n_attn)
    g_final = small_tot["final_norm"]
    g_b_mod = g_b_mod.reshape(depth, N_MOD * d)
    shard_cols = lambda g: lax.dynamic_slice_in_dim(g, mq * (g.shape[-1] // N_CHIPS), g.shape[-1] // N_CHIPS, axis=g.ndim - 1)
    g_b_pw1, g_w_dw, g_b_dw, g_ln_g, g_ln_b, g_b_pw2 = [shard_cols(stacked(nm, n_conv))
                                                        for nm in ("b_pw1", "w_dw", "b_dw", "ln_g", "ln_b", "b_pw2")]

    dmod_cols = jnp.transpose(dmod_all, (1, 0, 2, 3, 4)).reshape(depth, N_DEV * bsz, N_MOD * d)
    dmod_cols = lax.dynamic_slice_in_dim(dmod_cols, mq * n_mod, n_mod, axis=2)
    g_w_mod = _mod_bwd(c_all, dmod_cols, name="mod_bwd")

    def adam(w, g, m, v, name):
        two_d = lambda a: a.reshape(-1, a.shape[-1])
        return [r.reshape(w.shape) for r in _adamw(two_d(w), two_d(g), two_d(m), two_d(v), name=name)]

    results = {"w_mod": (g_w_mod,) + tuple(adam(w_mod, g_w_mod, m_w_mod, v_w_mod, "adamw"))}

    g_w_qkv, g_w_o, g_w_pw1, g_w_pw2, g_w_up, g_w_down = [
        shared[nm].reshape(shared[nm].shape[0], -1, shared[nm].shape[3]) for nm in big_names]
    for nm, w, g, m, v in (("w_qkv", w_qkv, g_w_qkv, m_w_qkv, v_w_qkv),
                           ("w_o", w_o, g_w_o, m_w_o, v_w_o), ("w_pw1", w_pw1, g_w_pw1, m_w_pw1, v_w_pw1),
                           ("w_pw2", w_pw2, g_w_pw2, m_w_pw2, v_w_pw2), ("w_up", w_up, g_w_up, m_w_up, v_w_up),
                           ("w_down", w_down, g_w_down, m_w_down, v_w_down)):
        results[nm] = (g,) + tuple(adam(w, g, m, v, "adamw"))
    small_w = dict(b_mod=(b_mod, g_b_mod, m_b_mod, v_b_mod), norm_mix=(norm_mix, g_norm_mix, m_norm_mix, v_norm_mix),
                   norm_mlp=(norm_mlp, g_norm_mlp, m_norm_mlp, v_norm_mlp), b_qkv=(b_qkv, g_b_qkv, m_b_qkv, v_b_qkv),
                   b_o=(b_o, g_b_o, m_b_o, v_b_o), sinks=(sinks, g_sinks, m_sinks, v_sinks),
                   b_pw1=(b_pw1, g_b_pw1, m_b_pw1, v_b_pw1), w_dw=(w_dw, g_w_dw, m_w_dw, v_w_dw),
                   b_dw=(b_dw, g_b_dw, m_b_dw, v_b_dw), conv_ln_g=(conv_ln_g, g_ln_g, m_conv_ln_g, v_conv_ln_g),
                   conv_ln_b=(conv_ln_b, g_ln_b, m_conv_ln_b, v_conv_ln_b), b_pw2=(b_pw2, g_b_pw2, m_b_pw2, v_b_pw2),
                   final_norm=(final_norm, g_final, m_final_norm, v_final_norm))
    names = list(small_w)
    deltas, new_ms, new_vs = _adamw_packed(*[[small_w[nm][k] for nm in names] for k in range(4)], d, name="adamw_small")
    for k, nm in enumerate(names):
        results[nm] = (small_w[nm][1], deltas[k], new_ms[k], new_vs[k])

    weight_order = ["w_mod", "b_mod", "norm_mix", "norm_mlp", "w_qkv", "b_qkv", "w_o", "b_o", "sinks", "w_pw1", "b_pw1",
                    "w_dw", "b_dw", "conv_ln_g", "conv_ln_b", "w_pw2", "b_pw2", "w_up", "w_down", "final_norm"]
    return (loss, grad_x, *[results[nm][0] for nm in weight_order], *[results[nm][1] for nm in weight_order],
            *[results[nm][2] for nm in weight_order], *[results[nm][3] for nm in weight_order])
```

```python
import functools

import jax
import jax.numpy as jnp
from jax import lax
from jax.experimental import pallas as pl
from jax.experimental.pallas import tpu as pltpu

F32, BF16 = jnp.float32, jnp.bfloat16
MESH = pl.DeviceIdType.MESH
N_CHIPS = 4
N_DEV = 8
LANES = 128
SUBLANES = 8
VMEM_LIMIT = 48 * 1024 * 1024

NORM_EPS = 1e-6
HEAD_DIM = 64
N_KV_HEADS = 2
WINDOW = 128
CONV_WIDTH = 31
CONV_HALO = 32
CONV_ROWS = 32
N_MOD = 6

ADAM_LR, ADAM_B1, ADAM_B2, ADAM_EPS, ADAM_WD, ADAM_STEP = 0.001, 0.9, 0.999, 1e-08, 0.01, 10

HBM_SPEC = pl.BlockSpec(memory_space=pltpu.HBM)
VMEM_SPEC = pl.BlockSpec(memory_space=pltpu.VMEM)
SMEM_SPEC = pl.BlockSpec(memory_space=pltpu.SMEM)


def _params(*sem):
    return pltpu.CompilerParams(dimension_semantics=sem or None, vmem_limit_bytes=VMEM_LIMIT)


def _row_tile(rows, width_bytes, target=2 << 20):
    t = rows
    while t % 2 == 0 and t > SUBLANES and t * width_bytes > target:
        t //= 2
    return t


CHIP_KINDS = ("x", "y", "xy")


def _position():
    x, y, c = lax.axis_index("x"), lax.axis_index("y"), lax.axis_index("c")
    return x, y, c, 2 * x + y


def _peer(kind, x, y, c):
    return {"c": (x, y, 1 - c), "x": (1 - x, y, c), "y": (x, 1 - y, c), "xy": (1 - x, 1 - y, c)}[kind]


def _peer_shard(kind, x, y):
    px, py, _ = _peer(kind, x, y, 0)
    return 2 * px + py


class _Plan:
    def __init__(self, link, cost, operands, out_shapes, copies, aliases=None, then=None):
        self.link, self.cost = link, cost
        self.operands, self.out_shapes, self.copies = list(operands), list(out_shapes), list(copies)
        self.aliases, self.then = dict(aliases or {}), then


def _merge(plans):
    operands, out_shapes, copies, aliases, thens = [], [], [], {}, []
    for p in plans:
        i0, o0 = len(operands), len(out_shapes)
        i1, o1 = i0 + len(p.operands), o0 + len(p.out_shapes)

        def shifted(f, i0=i0, i1=i1, o0=o0, o1=o1):
            return lambda ins, outs, pos: f(ins[i0:i1], outs[o0:o1], pos)

        copies += [(shifted(src), shifted(dst), shifted(land), kind) for src, dst, land, kind in p.copies]
        aliases.update({i0 + k: o0 + v for k, v in p.aliases.items()})
        operands += p.operands
        out_shapes += p.out_shapes
        thens.append((p.then, o0, o1))

    def then(outs):
        for f, o0, o1 in thens:
            if f is not None:
                f(outs[o0:o1])

    return _Plan("mixed", sum(p.cost for p in plans), operands, out_shapes, copies, aliases, then)


def _call(body, *, name, out_shape, operands, grid=(), in_specs=(), out_specs=(), scratch_shapes=(), sem=(), comm=None):
    single = not isinstance(out_shape, (tuple, list))
    out_shape = [out_shape] if single else list(out_shape)
    out_specs = [out_specs] if single else list(out_specs)
    if comm is None:
        res = pl.pallas_call(body, out_shape=out_shape, grid=grid, in_specs=list(in_specs), out_specs=out_specs,
                             scratch_shapes=list(scratch_shapes), name=name, compiler_params=_params(*sem))(*operands)
        return res[0] if single else res
    n_in, n_out, n_scr = len(operands), len(out_shape), len(scratch_shapes)
    c_in, c_out, n_cp = len(comm.operands), len(comm.out_shapes), len(comm.copies)

    def wrapped(*refs):
        ins, refs = refs[:n_in], refs[n_in:]
        cins, refs = refs[:c_in], refs[c_in:]
        outs, refs = refs[:n_out], refs[n_out:]
        couts, refs = refs[:c_out], refs[c_out:]
        scr, (send_sems, recv_sems) = refs[:n_scr], refs[n_scr:]

        def descriptors():
            pos = _position()
            sends, lands = [], []
            for k, (src, dst, landing, kind) in enumerate(comm.copies):
                common = dict(send_sem=send_sems.at[k], recv_sem=recv_sems.at[k],
                              device_id=_peer(kind, *pos[:3]), device_id_type=MESH)
                sends.append(pltpu.make_async_remote_copy(src_ref=src(cins, couts, pos), dst_ref=dst(cins, couts, pos), **common))
                lands.append(pltpu.make_async_remote_copy(src_ref=src(cins, couts, pos), dst_ref=landing(cins, couts, pos), **common))
            return sends, lands

        def start():
            for cp in descriptors()[0]:
                cp.start()

        def finish():
            sends, lands = descriptors()
            for cp in lands:
                cp.wait_recv()
            for cp in sends:
                cp.wait_send()

        if not grid:
            start()
            body(*ins, *outs, *scr)
            finish()
        else:
            ids = [pl.program_id(ax) for ax in range(len(grid))]
            first, last = ids[0] == 0, ids[0] == grid[0] - 1
            for ax in range(1, len(grid)):
                first, last = first & (ids[ax] == 0), last & (ids[ax] == grid[ax] - 1)
            pl.when(first)(start)
            body(*ins, *outs, *scr)
            pl.when(last)(finish)

    res = pl.pallas_call(wrapped, out_shape=out_shape + comm.out_shapes, grid=grid,
                         in_specs=list(in_specs) + [HBM_SPEC] * c_in, out_specs=out_specs + [HBM_SPEC] * c_out,
                         scratch_shapes=list(scratch_shapes) + [pltpu.SemaphoreType.DMA((n_cp,)), pltpu.SemaphoreType.DMA((n_cp,))],
                         input_output_aliases={n_in + k: n_out + v for k, v in comm.aliases.items()},
                         name=name, compiler_params=_params(*["arbitrary"] * len(grid)))(*operands, *comm.operands)
    if comm.then is not None:
        comm.then(res[n_out:])
    return res[0] if single else res[:n_out]


def _exchange(plan, *, name):
    _call(lambda: None, name=name, out_shape=[], operands=[], comm=plan)


def _mm(a, b, *, name, nt=False, tm=1024, tn=512, epi="plain", out_dtype=BF16,
        bias=None, act=None, resid=None, gate=None, seq=None, norm=None, comm=None):
    m, k = a.shape
    n = b.shape[0] if nt else b.shape[1]
    tm, tn = min(tm, m, seq or m), min(tn, n)
    assert m % tm == 0 and n % tn == 0
    dims = (((1,), (1,)), ((), ())) if nt else (((1,), (0,)), ((), ()))
    tile = pl.BlockSpec((tm, tn), lambda j, i: (i, j))
    operands = [a, b]
    in_specs = [pl.BlockSpec((tm, k), lambda j, i: (i, 0)),
                pl.BlockSpec((tn, k), lambda j, i: (j, 0)) if nt else pl.BlockSpec((k, tn), lambda j, i: (0, j))]
    if bias is not None:
        operands.append(bias.reshape(1, n))
        in_specs.append(pl.BlockSpec((1, tn), lambda j, i: (0, j)))
    if epi == "dact":
        operands.append(act)
        in_specs.append(tile)
    if epi == "resid":
        assert seq % tm == 0
        per_ex = pl.BlockSpec((None, 1, tn), lambda j, i: (i * tm // seq, 0, j))
        operands += [resid, gate]
        in_specs += [tile, per_ex]
        out_shape = (jax.ShapeDtypeStruct((m, n), BF16), jax.ShapeDtypeStruct((m, n), F32))
        out_specs = (tile, tile)
        if norm is not None:
            assert tn == n
            operands += list(norm)
            in_specs += [pl.BlockSpec((1, tn), lambda j, i: (0, j)), per_ex, per_ex]
            out_shape += (jax.ShapeDtypeStruct((m, n), BF16),)
            out_specs += (tile,)
    elif epi == "relu2":
        out_shape = (jax.ShapeDtypeStruct((m, n), BF16), jax.ShapeDtypeStruct((m, n), BF16))
        out_specs = (tile, tile)
    else:
        out_shape = jax.ShapeDtypeStruct((m, n), out_dtype)
        out_specs = tile

    def body(*refs):
        it = iter(refs)
        a_ref, b_ref = next(it), next(it)
        acc = lax.dot_general(a_ref[...], b_ref[...], dims, preferred_element_type=F32)
        if bias is not None:
            acc = acc + next(it)[...]
        if epi == "plain":
            next(it)[...] = acc.astype(out_dtype)
        elif epi == "relu2":
            r = jnp.maximum(acc, 0.0)
            next(it)[...] = (r * r).astype(BF16)
            next(it)[...] = (2.0 * r).astype(BF16)
        elif epi == "dact":
            slope_ref = next(it)
            next(it)[...] = (acc * slope_ref[...].astype(F32)).astype(out_dtype)
        else:
            resid_ref, gate_ref = next(it), next(it)
            norm_refs = [next(it) for _ in (norm or ())]
            y_ref, x_ref = next(it), next(it)
            y_ref[...] = acc.astype(BF16)
            x_new = resid_ref[...] + gate_ref[...] * acc
            x_ref[...] = x_new
            if norm is not None:
                g_ref, sc_ref, sh_ref = norm_refs
                r = lax.rsqrt(jnp.mean(x_new * x_new, axis=-1, keepdims=True) + NORM_EPS)
                next(it)[...] = (x_new * r * g_ref[...] * (1.0 + sc_ref[...]) + sh_ref[...]).astype(BF16)

    return _call(body, name=name, out_shape=out_shape, operands=operands, grid=(n // tn, m // tm), in_specs=in_specs,
                 out_specs=out_specs, sem=("parallel", "parallel"), comm=comm)


def _mm_tn(a, b, *, name, tm=1024, tn=1024, tk=2048, col_shards=None, comm=None):
    t, m = a.shape
    n = b.shape[1]
    tm, tk = min(tm, m), min(tk, t)
    if col_shards is None:
        tn = min(tn, n)
        out_shape = jax.ShapeDtypeStruct((m, n), F32)
        out_spec = pl.BlockSpec((tm, tn), lambda i, j, k: (i, j))
    else:
        per = n // col_shards
        tn = min(tn, per)
        assert per % tn == 0
        out_shape = jax.ShapeDtypeStruct((col_shards, m, per), F32)
        out_spec = pl.BlockSpec((None, tm, tn), lambda i, j, k: (j // (per // tn), i, j % (per // tn)))
    assert m % tm == 0 and n % tn == 0 and t % tk == 0

    def body(a_ref, b_ref, o_ref):
        @pl.when(pl.program_id(2) == 0)
        def _():
            o_ref[...] = jnp.zeros_like(o_ref)

        o_ref[...] += lax.dot_general(a_ref[...], b_ref[...], (((0,), (0,)), ((), ())),
                                      preferred_element_type=F32)

    return _call(body, name=name, out_shape=out_shape, operands=[a, b], grid=(m // tm, n // tn, t // tk),
                 in_specs=[pl.BlockSpec((tk, tm), lambda i, j, k: (k, i)), pl.BlockSpec((tk, tn), lambda i, j, k: (k, j))],
                 out_specs=out_spec, sem=("parallel", "parallel", "arbitrary"), comm=comm)


def _normmod(x, gamma, sc, sh, *, name):
    bsz, s, d = x.shape
    ts = _row_tile(s, d * 4, 2 << 20)

    def body(x_ref, g_ref, sc_ref, sh_ref, o_ref):
        xf = x_ref[...]
        r = lax.rsqrt(jnp.mean(xf * xf, axis=-1, keepdims=True) + NORM_EPS)
        o_ref[...] = (xf * r * g_ref[...] * (1.0 + sc_ref[...]) + sh_ref[...]).astype(BF16)

    row = pl.BlockSpec((None, ts, d), lambda b, i: (b, i, 0))
    per_ex = pl.BlockSpec((None, 1, d), lambda b, i: (b, 0, 0))
    return pl.pallas_call(body, out_shape=jax.ShapeDtypeStruct(x.shape, BF16), grid=(bsz, s // ts),
                          in_specs=[row, pl.BlockSpec((1, d), lambda b, i: (0, 0)), per_ex, per_ex],
                          out_specs=row, name=name, compiler_params=_params("parallel", "parallel"))(x, gamma, sc, sh)


def _gate_grads(dxv, y_ref, gate_ref, dy_ref, dg_ref, sdx_ref):
    dy_ref[...] = (gate_ref[...] * dxv).astype(BF16)

    @pl.when(pl.program_id(1) == 0)
    def _():
        dg_ref[...] = jnp.zeros_like(dg_ref)
        sdx_ref[...] = jnp.zeros_like(sdx_ref)

    dg_ref[...] += jnp.sum(dxv * y_ref[...].astype(F32), axis=0, keepdims=True)
    sdx_ref[...] += jnp.sum(dxv, axis=0, keepdims=True)


def _normmod_bwd(x, dh, dres, gamma, sc, producer=None, *, name, comm=None):
    bsz, s, d = x.shape
    ts = _row_tile(s, d * 4, 2 << 20)

    def body(x_ref, dh_ref, dres_ref, g_ref, sc_ref, *rest):
        dx_ref, p_ref, dsh_ref = rest[-6:-3] if producer else rest
        xf = x_ref[...]
        r = lax.rsqrt(jnp.mean(xf * xf, axis=-1, keepdims=True) + NORM_EPS)
        xhat = xf * r
        dh_v = dh_ref[...].astype(F32)
        dxhat = dh_v * (g_ref[...] * (1.0 + sc_ref[...]))
        dxv = dres_ref[...] + r * (dxhat - xhat * jnp.mean(dxhat * xhat, axis=-1, keepdims=True))
        dx_ref[...] = dxv

        @pl.when(pl.program_id(1) == 0)
        def _():
            p_ref[...] = jnp.zeros_like(p_ref)
            dsh_ref[...] = jnp.zeros_like(dsh_ref)

        p_ref[...] += jnp.sum(dh_v * xhat, axis=0, keepdims=True)
        dsh_ref[...] += jnp.sum(dh_v, axis=0, keepdims=True)
        if producer:
            _gate_grads(dxv, rest[0], rest[1], *rest[-3:])

    row = pl.BlockSpec((None, ts, d), lambda b, i: (b, i, 0))
    per_ex = pl.BlockSpec((None, 1, d), lambda b, i: (b, 0, 0))
    vec = jax.ShapeDtypeStruct((bsz, 1, d), F32)
    gate_in, gate_out, gate_specs = (), (), ()
    if producer:
        gate_in, gate_out, gate_specs = (row, per_ex), (jax.ShapeDtypeStruct(x.shape, BF16), vec, vec), (row, per_ex, per_ex)
    return _call(body, name=name, out_shape=(jax.ShapeDtypeStruct(x.shape, F32), vec, vec) + gate_out,
                 operands=[x, dh, dres, gamma, sc, *(producer or ())], grid=(bsz, s // ts),
                 in_specs=[row, row, row, pl.BlockSpec((1, d), lambda b, i: (0, 0)), per_ex, *gate_in],
                 out_specs=(row, per_ex, per_ex) + gate_specs, sem=("parallel", "arbitrary"), comm=comm)


def _loss_head(x, target, gamma, producer, *, name):
    bsz, s, d = x.shape
    ts = _row_tile(s, d * 4, 2 << 20)

    def body(x_ref, t_ref, g_ref, y_ref, gate_ref, dx_ref, loss_ref, dg_ref, *gate_refs):
        xf = x_ref[...]
        r = lax.rsqrt(jnp.mean(xf * xf, axis=-1, keepdims=True) + NORM_EPS)
        xhat = xf * r
        err = xhat * g_ref[...] - t_ref[...]
        dy = err * (1.0 / d)
        dxhat = dy * g_ref[...]
        dxv = r * (dxhat - xhat * jnp.mean(dxhat * xhat, axis=-1, keepdims=True))
        dx_ref[...] = dxv

        @pl.when((pl.program_id(0) == 0) & (pl.program_id(1) == 0))
        def _():
            loss_ref[...] = jnp.zeros_like(loss_ref)
            dg_ref[...] = jnp.zeros_like(dg_ref)

        loss_ref[...] += jnp.sum(err * err, axis=0, keepdims=True)
        dg_ref[...] += jnp.sum(dy * xhat, axis=0, keepdims=True)
        _gate_grads(dxv, y_ref, gate_ref, *gate_refs)

    row = pl.BlockSpec((None, ts, d), lambda b, i: (b, i, 0))
    one = pl.BlockSpec((1, d), lambda b, i: (0, 0))
    per_ex = pl.BlockSpec((None, 1, d), lambda b, i: (b, 0, 0))
    vec, ex_vec = jax.ShapeDtypeStruct((1, d), F32), jax.ShapeDtypeStruct((bsz, 1, d), F32)
    return pl.pallas_call(body, out_shape=(jax.ShapeDtypeStruct(x.shape, F32), vec, vec,
                                           jax.ShapeDtypeStruct(x.shape, BF16), ex_vec, ex_vec), grid=(bsz, s // ts),
                          in_specs=[row, row, one, row, per_ex], out_specs=(row, one, one, row, per_ex, per_ex), name=name,
                          compiler_params=_params("arbitrary", "arbitrary"))(x, target, gamma, *producer)


def _alibi_slope(h, n_heads):
    return 2.0 ** (-8.0 * (h + 1) / n_heads)


def _attn_masks(first_block):
    qi = lax.broadcasted_iota(jnp.int32, (WINDOW, 2 * WINDOW), 0)
    ki = lax.broadcasted_iota(jnp.int32, (WINDOW, 2 * WINDOW), 1)
    dist = qi + WINDOW - ki
    first_key = jnp.where(first_block, WINDOW, 0)
    valid = (dist >= 0) & (dist < WINDOW) & (ki >= first_key)
    return dist.astype(F32), valid


def _dup_halves(span, kv, left):
    f = span.astype(F32)
    rolled = pltpu.roll(f, HEAD_DIM, axis=1)
    out = jnp.where(left, f, rolled) if kv == 0 else jnp.where(left, rolled, f)
    return out.astype(BF16)


def _attn_probs(s, h, n_heads, distf, valid, sink):
    s = s * (HEAD_DIM ** -0.5) - _alibi_slope(h, n_heads) * distf
    s = jnp.where(valid, s, -1e30)
    m = jnp.maximum(jnp.max(s, axis=-1, keepdims=True), sink)
    e = jnp.exp(s - m)
    e_sink = jnp.exp(sink - m)
    inv = 1.0 / (jnp.sum(e, axis=-1, keepdims=True) + e_sink)
    return e * inv, e_sink * inv


def _attn_specs(d, n_blocks, clamp):
    kcol = d // LANES
    cur = (lambda i: jnp.minimum(i, n_blocks - 1)) if clamp else (lambda i: i)
    prev = lambda i: jnp.maximum(cur(i) - 1, 0)
    kv = lambda col, blk: pl.BlockSpec((None, WINDOW, LANES), lambda b, i: (b, blk(i), col))
    return [pl.BlockSpec((None, WINDOW, d), lambda b, i: (b, cur(i), 0)),
            kv(kcol, prev), kv(kcol, cur), kv(kcol + 1, prev), kv(kcol + 1, cur)]


def _attn_fwd(qkv, sinks, *, name, comm=None):
    bsz, s, qkv_dim = qkv.shape
    d = qkv_dim - 2 * N_KV_HEADS * HEAD_DIM
    n_heads = d // HEAD_DIM
    group = n_heads // N_KV_HEADS
    pairs = group // 2
    n_blocks = s // WINDOW

    def body(q_ref, kp_ref, kc_ref, vp_ref, vc_ref, sink_ref, o_ref):
        left = lax.broadcasted_iota(jnp.int32, (1, LANES), 1) < HEAD_DIM
        distf, valid = _attn_masks(pl.program_id(1) == 0)
        kspan = jnp.concatenate([kp_ref[...], kc_ref[...]], axis=0)
        vspan = jnp.concatenate([vp_ref[...], vc_ref[...]], axis=0)
        for kv in range(N_KV_HEADS):
            kdup, vdup = _dup_halves(kspan, kv, left), _dup_halves(vspan, kv, left)
            res = []
            for par in range(2):
                keep = left if par == 0 else jnp.logical_not(left)
                cols = [pl.ds((kv * pairs + p) * LANES, LANES) for p in range(pairs)]
                lhs = jnp.concatenate([jnp.where(keep, q_ref[:, cl], jnp.zeros((), BF16)) for cl in cols], axis=0)
                sc = lax.dot_general(lhs, kdup, (((1,), (1,)), ((), ())), preferred_element_type=F32)
                probs = []
                for p in range(pairs):
                    h = kv * group + 2 * p + par
                    pr, _ = _attn_probs(sc[p * WINDOW:(p + 1) * WINDOW], h, n_heads, distf, valid, sink_ref[h])
                    probs.append(pr.astype(BF16))
                res.append(jnp.dot(jnp.concatenate(probs, axis=0), vdup, preferred_element_type=F32))
            for p in range(pairs):
                rows = slice(p * WINDOW, (p + 1) * WINDOW)
                o_ref[:, pl.ds((kv * pairs + p) * LANES, LANES)] = jnp.where(left, res[0][rows], res[1][rows]).astype(BF16)

    return _call(body, name=name, out_shape=jax.ShapeDtypeStruct((bsz, s, d), BF16), operands=[qkv] * 5 + [sinks],
                 grid=(bsz, n_blocks), in_specs=_attn_specs(d, n_blocks, False) + [SMEM_SPEC],
                 out_specs=pl.BlockSpec((None, WINDOW, d), lambda b, i: (b, i, 0)), sem=("parallel", "parallel"), comm=comm)


def _attn_bwd(qkv, do, sinks, *, name, comm=None):
    bsz, s, qkv_dim = qkv.shape
    d = qkv_dim - 2 * N_KV_HEADS * HEAD_DIM
    n_heads = d // HEAD_DIM
    group = n_heads // N_KV_HEADS
    pairs = group // 2
    n_blocks = s // WINDOW
    tn_dims = (((0,), (0,)), ((), ()))

    def body(q_ref, kp_ref, kc_ref, vp_ref, vc_ref, do_ref, sink_ref, dqkv_ref, colsum_ref, dsink_ref,
             dq_prev, dk_carry, dv_carry):
        b, i = pl.program_id(0), pl.program_id(1)
        left = lax.broadcasted_iota(jnp.int32, (1, LANES), 1) < HEAD_DIM

        @pl.when((b == 0) & (i == 0))
        def _():
            colsum_ref[...] = jnp.zeros_like(colsum_ref)
            dsink_ref[...] = jnp.zeros_like(dsink_ref)

        @pl.when(i == 0)
        def _():
            dqkv_ref[...] = jnp.zeros_like(dqkv_ref)
            dk_carry[...] = jnp.zeros_like(dk_carry)
            dv_carry[...] = jnp.zeros_like(dv_carry)

        @pl.when(i > 0)
        def _():
            dq_v = dq_prev[...]
            dqkv_ref[:, pl.ds(0, d)] = dq_v.astype(BF16)
            colsum_ref[:, pl.ds(0, d)] += jnp.sum(dq_v, axis=0, keepdims=True)

        @pl.when(i < n_blocks)
        def _():
            distf, valid = _attn_masks(i == 0)
            kspan = jnp.concatenate([kp_ref[...], kc_ref[...]], axis=0)
            vspan = jnp.concatenate([vp_ref[...], vc_ref[...]], axis=0)
            dk_blk, dv_blk = [], []
            for kv in range(N_KV_HEADS):
                kdup, vdup = _dup_halves(kspan, kv, left), _dup_halves(vspan, kv, left)
                dq_res, dk_sum, dv_sum = [], None, None
                for par in range(2):
                    keep = left if par == 0 else jnp.logical_not(left)
                    cols = [pl.ds((kv * pairs + p) * LANES, LANES) for p in range(pairs)]
                    zero = jnp.zeros((), BF16)
                    lhs = jnp.concatenate([jnp.where(keep, q_ref[:, cl], zero) for cl in cols], axis=0)
                    dol = jnp.concatenate([jnp.where(keep, do_ref[:, cl], zero) for cl in cols], axis=0)
                    sc = lax.dot_general(lhs, kdup, (((1,), (1,)), ((), ())), preferred_element_type=F32)
                    dp = lax.dot_general(dol, vdup, (((1,), (1,)), ((), ())), preferred_element_type=F32)
                    probs, dscores = [], []
                    for p in range(pairs):
                        h = kv * group + 2 * p + par
                        rows = slice(p * WINDOW, (p + 1) * WINDOW)
                        pr, p_sink = _attn_probs(sc[rows], h, n_heads, distf, valid, sink_ref[h])
                        delta = jnp.sum(pr * dp[rows], axis=-1, keepdims=True)
                        dscores.append((pr * (dp[rows] - delta) * (HEAD_DIM ** -0.5)).astype(BF16))
                        probs.append(pr.astype(BF16))
                        dsink_ref[pl.ds(h, 1), :] += jnp.zeros((1, LANES), F32) - jnp.sum(p_sink * delta)
                    ds_all = jnp.concatenate(dscores, axis=0)
                    p_all = jnp.concatenate(probs, axis=0)
                    dq_res.append(jnp.dot(ds_all, kdup, preferred_element_type=F32))
                    dk_par = lax.dot_general(ds_all, lhs, tn_dims, preferred_element_type=F32)
                    dv_par = lax.dot_general(p_all, dol, tn_dims, preferred_element_type=F32)
                    dk_sum = dk_par if dk_sum is None else dk_sum + dk_par
                    dv_sum = dv_par if dv_sum is None else dv_sum + dv_par
                for p in range(pairs):
                    rows = slice(p * WINDOW, (p + 1) * WINDOW)
                    dq_prev[:, pl.ds((kv * pairs + p) * LANES, LANES)] = jnp.where(left, dq_res[0][rows], dq_res[1][rows])
                dk_blk.append(dk_sum + pltpu.roll(dk_sum, HEAD_DIM, axis=1))
                dv_blk.append(dv_sum + pltpu.roll(dv_sum, HEAD_DIM, axis=1))
            dk_span = jnp.where(left, dk_blk[0], dk_blk[1])
            dv_span = jnp.where(left, dv_blk[0], dv_blk[1])
            dk_done = dk_carry[...] + dk_span[:WINDOW]
            dv_done = dv_carry[...] + dv_span[:WINDOW]
            dk_carry[...] = dk_span[WINDOW:]
            dv_carry[...] = dv_span[WINDOW:]

            @pl.when(i > 0)
            def _():
                dqkv_ref[:, pl.ds(d, LANES)] = dk_done.astype(BF16)
                dqkv_ref[:, pl.ds(d + LANES, LANES)] = dv_done.astype(BF16)
                colsum_ref[:, pl.ds(d, LANES)] += jnp.sum(dk_done, axis=0, keepdims=True)
                colsum_ref[:, pl.ds(d + LANES, LANES)] += jnp.sum(dv_done, axis=0, keepdims=True)

        @pl.when(i == n_blocks)
        def _():
            dk_done, dv_done = dk_carry[...], dv_carry[...]
            dqkv_ref[:, pl.ds(d, LANES)] = dk_done.astype(BF16)
            dqkv_ref[:, pl.ds(d + LANES, LANES)] = dv_done.astype(BF16)
            colsum_ref[:, pl.ds(d, LANES)] += jnp.sum(dk_done, axis=0, keepdims=True)
            colsum_ref[:, pl.ds(d + LANES, LANES)] += jnp.sum(dv_done, axis=0, keepdims=True)

    do_spec = pl.BlockSpec((None, WINDOW, d), lambda b, i: (b, jnp.minimum(i, n_blocks - 1), 0))
    out_shape = (jax.ShapeDtypeStruct((bsz, s, qkv_dim), BF16), jax.ShapeDtypeStruct((1, qkv_dim), F32),
                 jax.ShapeDtypeStruct((n_heads, LANES), F32))
    out_specs = (pl.BlockSpec((None, WINDOW, qkv_dim), lambda b, i: (b, jnp.maximum(i - 1, 0), 0)),
                 pl.BlockSpec((1, qkv_dim), lambda b, i: (0, 0)),
                 pl.BlockSpec((n_heads, LANES), lambda b, i: (0, 0)))
    return _call(body, name=name, out_shape=out_shape, operands=[qkv] * 5 + [do, sinks], grid=(bsz, n_blocks + 1),
                 in_specs=_attn_specs(d, n_blocks, True) + [do_spec, SMEM_SPEC], out_specs=out_specs,
                 scratch_shapes=[pltpu.VMEM((WINDOW, d), F32), pltpu.VMEM((WINDOW, LANES), F32), pltpu.VMEM((WINDOW, LANES), F32)],
                 sem=("arbitrary", "arbitrary"), comm=comm)


def _conv_tile(s):
    return min(256, s)


def _halo_specs(ts, width, s):
    per = ts // CONV_HALO
    prev = pl.BlockSpec((None, CONV_HALO, width), lambda b, i: (b, jnp.maximum(i * per - 1, 0), 0))
    nxt = pl.BlockSpec((None, CONV_HALO, width), lambda b, i: (b, jnp.minimum((i + 1) * per, s // CONV_HALO - 1), 0))
    cur = pl.BlockSpec((None, ts, width), lambda b, i: (b, i, 0))
    return prev, cur, nxt


def _glu(u, d):
    return u[:, :d] * jax.nn.sigmoid(u[:, d:])


def _store_shifted(shifted, value):
    rows = value.shape[0]
    shifted[0] = value
    for b in range(1, SUBLANES):
        shifted[b] = pltpu.roll(value, rows - b, axis=0)


def _at(shifted, base, offset, n_rows, lanes):
    return shifted[offset % SUBLANES, pl.ds(base + (offset - offset % SUBLANES), n_rows), lanes]


def _taps(w_ref, shifted, out_ref, ts, d, offset):
    for r0 in range(0, ts, CONV_ROWS):
        for l0 in range(0, d, LANES):
            lanes = pl.ds(l0, LANES)
            acc = jnp.zeros((CONV_ROWS, LANES), F32)
            for j in range(CONV_WIDTH):
                acc = acc + w_ref[pl.ds(j, 1), lanes] * _at(shifted, r0, offset(j), CONV_ROWS, lanes)
            out_ref[pl.ds(r0, CONV_ROWS), lanes] = acc


def _conv_fwd(u, w_dw, b_dw, ln_g, ln_b, *, name, comm=None):
    bsz, s, d2 = u.shape
    d = d2 // 2
    ts = _conv_tile(s)

    def body(up_ref, uc_ref, w_ref, bdw_ref, g_ref, b_ref, z_ref, v_ref, gbuf):
        halo = jnp.where(pl.program_id(1) > 0, _glu(up_ref[...], d), 0.0)
        _store_shifted(gbuf, jnp.concatenate([halo, _glu(uc_ref[...], d)], axis=0))
        _taps(w_ref, gbuf, v_ref, ts, d, lambda j: CONV_HALO - (CONV_WIDTH - 1) + j)
        v = v_ref[...] + bdw_ref[...]
        v_ref[...] = v
        mu = jnp.mean(v, axis=-1, keepdims=True)
        cen = v - mu
        rstd = lax.rsqrt(jnp.mean(cen * cen, axis=-1, keepdims=True) + NORM_EPS)
        ln = cen * rstd * g_ref[...] + b_ref[...]
        z_ref[...] = (ln * jax.nn.sigmoid(ln)).astype(BF16)

    prev, cur, _ = _halo_specs(ts, d2, s)
    one = pl.BlockSpec((1, d), lambda b, i: (0, 0))
    row = pl.BlockSpec((None, ts, d), lambda b, i: (b, i, 0))
    return _call(body, name=name, out_shape=(jax.ShapeDtypeStruct((bsz, s, d), BF16), jax.ShapeDtypeStruct((bsz, s, d), F32)),
                 operands=[u, u, w_dw, b_dw, ln_g, ln_b], grid=(bsz, s // ts),
                 in_specs=[prev, cur, pl.BlockSpec((CONV_HALO, d), lambda b, i: (0, 0)), one, one, one],
                 out_specs=(row, row), scratch_shapes=[pltpu.VMEM((SUBLANES, ts + CONV_HALO, d), F32)],
                 sem=("parallel", "parallel"), comm=comm)


def _conv_bwd_ln(dz, v, ln_g, ln_b, *, name):
    bsz, s, d = v.shape
    ts = _row_tile(s, d * 4, 2 << 20)

    def body(dz_ref, v_ref, g_ref, b_ref, dv_ref, dg_ref, db_ref, dbdw_ref):
        v_v = v_ref[...]
        mu = jnp.mean(v_v, axis=-1, keepdims=True)
        cen = v_v - mu
        rstd = lax.rsqrt(jnp.mean(cen * cen, axis=-1, keepdims=True) + NORM_EPS)
        vhat = cen * rstd
        ln = vhat * g_ref[...] + b_ref[...]
        sig = jax.nn.sigmoid(ln)
        dln = dz_ref[...] * (sig * (1.0 + ln * (1.0 - sig)))
        dvhat = dln * g_ref[...]
        dv = rstd * (dvhat - jnp.mean(dvhat, axis=-1, keepdims=True)
                     - vhat * jnp.mean(dvhat * vhat, axis=-1, keepdims=True))
        dv_ref[...] = dv

        @pl.when((pl.program_id(0) == 0) & (pl.program_id(1) == 0))
        def _():
            dg_ref[...] = jnp.zeros_like(dg_ref)
            db_ref[...] = jnp.zeros_like(db_ref)
            dbdw_ref[...] = jnp.zeros_like(dbdw_ref)

        dg_ref[...] += jnp.sum(dln * vhat, axis=0, keepdims=True)
        db_ref[...] += jnp.sum(dln, axis=0, keepdims=True)
        dbdw_ref[...] += jnp.sum(dv, axis=0, keepdims=True)

    row = pl.BlockSpec((None, ts, d), lambda b, i: (b, i, 0))
    one = pl.BlockSpec((1, d), lambda b, i: (0, 0))
    vec = jax.ShapeDtypeStruct((1, d), F32)
    return pl.pallas_call(body, out_shape=(jax.ShapeDtypeStruct(v.shape, F32), vec, vec, vec), grid=(bsz, s // ts),
                          in_specs=[row, row, one, one], out_specs=(row, one, one, one), name=name,
                          compiler_params=_params("arbitrary", "arbitrary"))(dz, v, ln_g, ln_b)


def _conv_bwd_taps(dv, u, w_dw, *, name, comm=None):
    bsz, s, d = dv.shape
    ts = _conv_tile(s)
    n_tiles = s // ts

    def body(dvc_ref, dvn_ref, uc_ref, w_ref, du_ref, dbu_ref, dw_ref, dvbuf, dglu, glu):
        i = pl.program_id(1)

        @pl.when((pl.program_id(0) == 0) & (i == 0))
        def _():
            dbu_ref[...] = jnp.zeros_like(dbu_ref)
            dw_ref[...] = jnp.zeros_like(dw_ref)

        u_v = uc_ref[...]
        a, sig = u_v[:, :d], jax.nn.sigmoid(u_v[:, d:])
        glu[...] = a * sig
        ahead = jnp.where(i < n_tiles - 1, dvn_ref[...], 0.0)
        _store_shifted(dvbuf, jnp.concatenate([dvc_ref[...], ahead], axis=0))
        _taps(w_ref, dvbuf, dglu, ts, d, lambda j: CONV_WIDTH - 1 - j)
        for l0 in range(0, d, LANES):
            lanes = pl.ds(l0, LANES)
            for j in range(CONV_WIDTH):
                acc = jnp.zeros((SUBLANES, LANES), F32)
                for r0 in range(0, ts, CONV_ROWS):
                    prod = glu[pl.ds(r0, CONV_ROWS), lanes] * _at(dvbuf, r0, CONV_WIDTH - 1 - j, CONV_ROWS, lanes)
                    for k in range(0, CONV_ROWS, SUBLANES):
                        acc = acc + prod[k:k + SUBLANES]
                dw_ref[j, :, lanes] += acc
        dg_v = dglu[...]
        da = dg_v * sig
        dgate = dg_v * a * sig * (1.0 - sig)
        du_ref[:, pl.ds(0, d)] = da.astype(BF16)
        du_ref[:, pl.ds(d, d)] = dgate.astype(BF16)
        dbu_ref[:, pl.ds(0, d)] += jnp.sum(da, axis=0, keepdims=True)
        dbu_ref[:, pl.ds(d, d)] += jnp.sum(dgate, axis=0, keepdims=True)

    _, dv_cur, dv_next = _halo_specs(ts, d, s)
    _, u_cur, _ = _halo_specs(ts, 2 * d, s)
    out_shape = (jax.ShapeDtypeStruct((bsz, s, 2 * d), BF16), jax.ShapeDtypeStruct((1, 2 * d), F32),
                 jax.ShapeDtypeStruct((CONV_HALO, SUBLANES, d), F32))
    out_specs = (pl.BlockSpec((None, ts, 2 * d), lambda b, i: (b, i, 0)), pl.BlockSpec((1, 2 * d), lambda b, i: (0, 0)),
                 pl.BlockSpec((CONV_HALO, SUBLANES, d), lambda b, i: (0, 0, 0)))
    return _call(body, name=name, out_shape=out_shape, operands=[dv, dv, u, w_dw], grid=(bsz, n_tiles),
                 in_specs=[dv_cur, dv_next, u_cur, pl.BlockSpec((CONV_HALO, d), lambda b, i: (0, 0))],
                 out_specs=out_specs,
                 scratch_shapes=[pltpu.VMEM((SUBLANES, ts + CONV_HALO, d), F32), pltpu.VMEM((ts, d), F32),
                                 pltpu.VMEM((ts, d), F32)],
                 sem=("arbitrary", "arbitrary"), comm=comm)


def _mod_fwd(c_all, w_mod, b_mod, *, name):
    n_layers, d, n = w_mod.shape
    rows = c_all.shape[0]

    def body(c_ref, w_ref, b_ref, o_ref):
        cv = c_ref[...]
        cs = (cv * jax.nn.sigmoid(cv)).astype(BF16)
        o_ref[...] = jnp.dot(cs, w_ref[...].astype(BF16), preferred_element_type=F32) + b_ref[...]

    return pl.pallas_call(body, out_shape=jax.ShapeDtypeStruct((n_layers, rows, n), F32), grid=(n_layers,),
                          in_specs=[pl.BlockSpec((rows, d), lambda l: (0, 0)), pl.BlockSpec((None, d, n), lambda l: (l, 0, 0)),
                                    pl.BlockSpec((None, 1, n), lambda l: (l, 0, 0))],
                          out_specs=pl.BlockSpec((None, rows, n), lambda l: (l, 0, 0)), name=name,
                          compiler_params=_params("parallel"))(c_all, w_mod, b_mod)


def _mod_bwd(c_all, dmod, *, name):
    n_layers, rows, n = dmod.shape
    d = c_all.shape[1]

    def body(c_ref, g_ref, o_ref):
        cv = c_ref[...]
        cs = (cv * jax.nn.sigmoid(cv)).astype(BF16)
        o_ref[...] = lax.dot_general(cs, g_ref[...].astype(BF16), (((0,), (0,)), ((), ())), preferred_element_type=F32)

    return pl.pallas_call(body, out_shape=jax.ShapeDtypeStruct((n_layers, d, n), F32), grid=(n_layers,),
                          in_specs=[pl.BlockSpec((rows, d), lambda l: (0, 0)), pl.BlockSpec((None, rows, n), lambda l: (l, 0, 0))],
                          out_specs=pl.BlockSpec((None, d, n), lambda l: (l, 0, 0)), name=name,
                          compiler_params=_params("parallel"))(c_all, dmod)


def _add_half(g, recv, my_c, *, name):
    p, _, r, cdim = g.shape
    tr = _row_tile(r, cdim * 4)

    def body(c_ref, g_ref, r_ref, o_ref):
        o_ref[...] = (g_ref[...] + r_ref[...]).astype(BF16)

    grid_spec = pltpu.PrefetchScalarGridSpec(
        num_scalar_prefetch=1, grid=(p, r // tr),
        in_specs=[pl.BlockSpec((None, None, tr, cdim), lambda q, i, c_ref: (q, c_ref[0], i, 0)),
                  pl.BlockSpec((None, None, tr, cdim), lambda q, i, c_ref: (q, 0, i, 0))],
        out_specs=pl.BlockSpec((None, tr, cdim), lambda q, i, c_ref: (q, i, 0)))
    return pl.pallas_call(body, out_shape=jax.ShapeDtypeStruct((p, r, cdim), BF16), grid_spec=grid_spec, name=name,
                          compiler_params=_params("parallel", "parallel"))(my_c, g, recv)


def _add_pieces(chip, recv, my_qc, stacked, layer, n_layers, *, name):
    _, r, cdim = chip.shape
    tr = _row_tile(r, cdim * 4)

    def body(qc_ref, own_ref, r0_ref, r1_ref, r2_ref, *rest):
        f32 = lambda ref: ref[...].astype(F32)
        rest[-1][...] = ((f32(own_ref) + f32(r0_ref)) + f32(r1_ref)) + f32(r2_ref)

    piece = lambda k: pl.BlockSpec((None, tr, cdim), lambda i, qc_ref: (k, i, 0))
    in_specs = [pl.BlockSpec((None, tr, cdim), lambda i, qc_ref: (qc_ref[0], i, 0)), piece(0), piece(1), piece(2)]
    operands = [my_qc, chip, recv, recv, recv]
    aliases = {}
    if stacked is not None:
        in_specs.append(pl.BlockSpec(memory_space=pl.ANY))
        operands.append(stacked)
        aliases = {len(operands) - 1: 0}
    grid_spec = pltpu.PrefetchScalarGridSpec(
        num_scalar_prefetch=1, grid=(r // tr,), in_specs=in_specs,
        out_specs=pl.BlockSpec((None, None, tr, cdim), lambda i, qc_ref: (layer, qc_ref[1], i, 0)))
    return pl.pallas_call(body, out_shape=jax.ShapeDtypeStruct((n_layers, 2, r, cdim), F32), grid_spec=grid_spec,
                          input_output_aliases=aliases, name=name, compiler_params=_params("parallel"))(*operands)


def _sum_devices(parts, dmod, *, name):
    def body(p_ref, m_ref, o_ref, b_ref):
        acc = p_ref[0]
        for k in range(1, N_DEV):
            acc = acc + p_ref[k]
        o_ref[...] = acc
        tot = None
        for k in range(N_DEV):
            for e in range(dmod.shape[2]):
                tot = m_ref[k, :, e] if tot is None else tot + m_ref[k, :, e]
        b_ref[...] = tot

    out_shape = (jax.ShapeDtypeStruct(parts.shape[1:], F32),
                 jax.ShapeDtypeStruct((dmod.shape[1],) + dmod.shape[3:], F32))
    return pl.pallas_call(body, out_shape=out_shape, in_specs=[VMEM_SPEC, VMEM_SPEC], out_specs=(VMEM_SPEC, VMEM_SPEC),
                          name=name, compiler_params=_params())(parts, dmod)


def _adamw(w, g, m, v, *, name, comm=None):
    r, cdim = w.shape
    tr = _row_tile(r, cdim * 4, 2 << 20)

    def body(w_ref, g_ref, m_ref, v_ref, d_ref, nm_ref, nv_ref):
        gv = g_ref[...]
        nm = ADAM_B1 * m_ref[...] + (1.0 - ADAM_B1) * gv
        nv = ADAM_B2 * v_ref[...] + (1.0 - ADAM_B2) * (gv * gv)
        m_hat = nm / (1.0 - ADAM_B1 ** ADAM_STEP)
        v_hat = nv / (1.0 - ADAM_B2 ** ADAM_STEP)
        d_ref[...] = -ADAM_LR * (m_hat / (jnp.sqrt(v_hat) + ADAM_EPS) + ADAM_WD * w_ref[...])
        nm_ref[...] = nm
        nv_ref[...] = nv

    row = pl.BlockSpec((tr, cdim), lambda i: (i, 0))
    shape = jax.ShapeDtypeStruct((r, cdim), F32)
    return _call(body, name=name, out_shape=(shape, shape, shape), operands=[w, g, m, v], grid=(r // tr,),
                 in_specs=[row] * 4, out_specs=(row, row, row), sem=("parallel",), comm=comm)


def _all_gather_small(block, *, name, comm=None):
    m_per, n = block.shape

    def body(x_ref, out_ref, send_sems, recv_sems, local_sem):
        x, y, c, _ = _position()
        me, sibling = (x, y, c), (x, y, 1 - c)
        chips = [_peer(k, x, y, c)[:2] for k in CHIP_KINDS]

        def rows(px, py, pc):
            return out_ref.at[pl.ds((4 * px + 2 * py + pc) * m_per, m_per), :]

        def copy(k, blk, to, src=None):
            return pltpu.make_async_remote_copy(src_ref=rows(*blk) if src is None else src, dst_ref=rows(*blk),
                                                send_sem=send_sems.at[k], recv_sem=recv_sems.at[k],
                                                device_id=to, device_id_type=MESH)

        mine = pltpu.make_async_copy(x_ref, rows(*me), local_sem)
        mine.start()
        first = [copy(0, me, sibling, src=x_ref)]
        first += [copy(1 + j, me, (*chip, c), src=x_ref) for j, chip in enumerate(chips)]
        for cp in first:
            cp.start()
        passed = [copy(4 + j, (*chip, c), sibling) for j, chip in enumerate(chips)]
        for j, chip in enumerate(chips):
            copy(1 + j, (*chip, c), me).wait_recv()
            passed[j].start()
        copy(0, sibling, me).wait_recv()
        for j, chip in enumerate(chips):
            copy(4 + j, (*chip, 1 - c), me).wait_recv()
        for cp in first + passed:
            cp.wait_send()
        mine.wait()

    return _call(body, name=name, out_shape=jax.ShapeDtypeStruct((N_DEV * m_per, n), block.dtype), operands=[block],
                 in_specs=[VMEM_SPEC], out_specs=VMEM_SPEC,
                 scratch_shapes=[pltpu.SemaphoreType.DMA((7,)), pltpu.SemaphoreType.DMA((7,)), pltpu.SemaphoreType.DMA],
                 comm=comm)


def _pack(arrays, width):
    flat = jnp.concatenate([a.reshape(-1).astype(F32) for a in arrays])
    rows = -(-flat.shape[0] // width)
    rows = -(-rows // SUBLANES) * SUBLANES
    return jnp.pad(flat, (0, rows * width - flat.shape[0])).reshape(rows, width)


def _unpack(packed, shapes):
    flat, out, off = packed.reshape(-1), [], 0
    for shp in shapes:
        size = 1
        for dim in shp:
            size *= dim
        out.append(flat[off:off + size].reshape(shp))
        off += size
    return out


def _adamw_packed(ws, gs, ms, vs, width, *, name):
    shapes = [w.shape for w in ws]
    res = _adamw(_pack(ws, width), _pack(gs, width), _pack(ms, width), _pack(vs, width), name=name)
    return [_unpack(r, shapes) for r in res]


MB = float(1 << 20)


def _nbytes(shape, dtype):
    size = jnp.dtype(dtype).itemsize
    for dim in shape:
        size *= dim
    return size


def _gather_plans(shard, side_by_side, on_ready):
    r, cdim = shard.shape
    half = r // 2
    if side_by_side:
        assert cdim % LANES == 0
        buf = jax.ShapeDtypeStruct((r, N_CHIPS * cdim), shard.dtype)
        rows = lambda ref, q, h: ref.at[pl.ds(h * half, half), pl.ds(q * cdim, cdim)]
        whole = lambda ref, q: ref.at[:, pl.ds(q * cdim, cdim)]
        src, src_rows = shard, (lambda ref, h: ref.at[pl.ds(h * half, half)])
    else:
        buf = jax.ShapeDtypeStruct((N_CHIPS, r, cdim), shard.dtype)
        rows = lambda ref, q, h: ref.at[pl.ds(q, 1), pl.ds(h * half, half)]
        whole = lambda ref, q: ref.at[pl.ds(q, 1)]
        src, src_rows = shard[None], (lambda ref, h: ref.at[:, pl.ds(h * half, half)])
    there = lambda kind, pos: _peer_shard(kind, pos[0], pos[1])
    cost = 3 * _nbytes((half, cdim), shard.dtype) / MB

    def forward(outs):
        copies = [(lambda ins, outs, pos, kind=kind: rows(ins[0], there(kind, pos), pos[2]),
                   lambda ins, outs, pos, kind=kind: rows(outs[0], there(kind, pos), pos[2]),
                   lambda ins, outs, pos, kind=kind: rows(outs[0], there(kind, pos), 1 - pos[2]), "c") for kind in CHIP_KINDS]
        return _Plan("d2d", cost, [outs[0]], [jax.ShapeDtypeStruct(buf.shape, buf.dtype)], copies, {0: 0},
                     lambda done: on_ready(done[0]))

    copies = [(lambda ins, outs, pos: src_rows(ins[0], pos[2]),
               lambda ins, outs, pos: rows(outs[0], pos[3], pos[2]),
               lambda ins, outs, pos, kind=kind: rows(outs[0], there(kind, pos), pos[2]), kind) for kind in CHIP_KINDS]
    own = lambda ins, outs, pos: whole(outs[0], pos[3])
    copies.append((lambda ins, outs, pos: ins[0], own, own, "c"))
    return _Plan("ici", cost, [src], [buf], copies), forward


class _Exchanges:
    def __init__(self):
        self.queue = []

    def add(self, plan, front=False):
        if front:
            self.queue.insert(0, plan)
        else:
            self.queue.append(plan)

    def take(self, budget_mb, at_least_one=False):
        chosen, spent = [p for p in self.queue if p.link == "d2d"], 0.0
        for p in self.queue:
            if p.link == "ici" and (spent + p.cost <= budget_mb or (at_least_one and spent == 0.0)):
                chosen.append(p)
                spent += p.cost
        if not chosen:
            return None
        self.queue = [p for p in self.queue if all(p is not ch for ch in chosen)]
        return _merge(chosen)

    def flush(self, budget_mb, until=lambda: False):
        while self.queue and not until():
            _exchange(self.take(budget_mb, at_least_one=True), name="exchange")


def kernel(x, c, w_mod, b_mod, norm_mix, norm_mlp, w_qkv, b_qkv, w_o, b_o, sinks, w_pw1, b_pw1, w_dw, b_dw, conv_ln_g, conv_ln_b, w_pw2, b_pw2, w_up, w_down, final_norm, loss_target, m_w_mod, m_b_mod, m_norm_mix, m_norm_mlp, m_w_qkv, m_b_qkv, m_w_o, m_b_o, m_sinks, m_w_pw1, m_b_pw1, m_w_dw, m_b_dw, m_conv_ln_g, m_conv_ln_b, m_w_pw2, m_b_pw2, m_w_up, m_w_down, m_final_norm, v_w_mod, v_b_mod, v_norm_mix, v_norm_mlp, v_w_qkv, v_b_qkv, v_w_o, v_b_o, v_sinks, v_w_pw1, v_b_pw1, v_w_dw, v_b_dw, v_conv_ln_g, v_conv_ln_b, v_w_pw2, v_b_pw2, v_w_up, v_w_down, v_final_norm):
    bsz, s, d = x.shape
    t = bsz * s
    depth = w_mod.shape[0]
    n_attn, n_conv = w_qkv.shape[0], w_pw1.shape[0]
    qkv_dim = d + 2 * N_KV_HEADS * HEAD_DIM
    mx, my, mc, mq = _position()
    me = 4 * mx + 2 * my + mc
    my_c = jnp.reshape(mc, (1,)).astype(jnp.int32)
    my_qc = jnp.stack([mq, mc]).astype(jnp.int32)
    pending = _Exchanges()
    SMALL, MEDIUM, LARGE = 2.5, 3.5, 6.5

    weights = {}
    order = []
    for i in range(depth):
        j = i // 2
        order += ([(("qkv", j), w_qkv[j], True), (("o", j), w_o[j], False)] if i % 2 == 0 else
                  [(("pw1", j), w_pw1[j], True), (("pw2", j), w_pw2[j], False)])
        order += [(("up", i), w_up[i], True), (("down", i), w_down[i], False)]
    for key, shard, by_cols in order:
        side_by_side = by_cols and shard.shape[1] % LANES == 0

        def ready(buf, key=key, by_cols=by_cols):
            if buf.ndim == 3:
                buf = jnp.transpose(buf, (1, 0, 2)).reshape(buf.shape[1], -1) if by_cols else buf.reshape(-1, buf.shape[2])
            weights[key] = buf
        ici, forward = _gather_plans(shard.astype(BF16), side_by_side, ready)
        ici.then = lambda outs, forward=forward: pending.add(forward(outs), front=True)
        pending.add(ici)

    def weight(key):
        pending.flush(SMALL, until=lambda: key in weights)
        return weights[key]

    small_sharded = [b_pw1, w_dw, b_dw, conv_ln_g, conv_ln_b, b_pw2]
    c_pad = jnp.pad(c, ((0, SUBLANES - bsz), (0, 0)))
    gathered = _all_gather_small(jnp.concatenate([c_pad, _pack(small_sharded, d)], axis=0), name="gather_c",
                                 comm=pending.take(SMALL, True))
    gathered = gathered.reshape(N_DEV, -1, d)
    c_all = gathered[:, :bsz].reshape(N_DEV * bsz, d)
    per_chip = [_unpack(gathered[2 * q, SUBLANES:], [a.shape for a in small_sharded]) for q in range(N_CHIPS)]
    b_pw1_f, w_dw_f, b_dw_f, ln_g_f, ln_b_f, b_pw2_f = [jnp.concatenate([per_chip[q][k] for q in range(N_CHIPS)], axis=-1)
                                                           for k in range(len(small_sharded))]
    w_dw_f = jnp.pad(w_dw_f, ((0, 0), (0, CONV_HALO - CONV_WIDTH), (0, 0)))

    n_mod = w_mod.shape[2]
    b_mod_cols = lax.dynamic_slice_in_dim(b_mod, mq * n_mod, n_mod, axis=1).reshape(depth, 1, n_mod)
    c_all8 = gathered[:, :SUBLANES].reshape(N_DEV * SUBLANES, d)
    mod_part = _mod_fwd(c_all8, w_mod, b_mod_cols, name="mod_fwd")
    peer_rows = lambda kind, pos: pl.ds(SUBLANES * (2 * _peer_shard(kind, pos[0], pos[1]) + pos[2]), SUBLANES)
    got = {}
    pending.add(_Plan("ici", 0.01, [mod_part],[jax.ShapeDtypeStruct((len(CHIP_KINDS), depth, SUBLANES, n_mod), F32)],
                      [(lambda ins, outs, pos, kind=kind: ins[0].at[:, peer_rows(kind, pos)],
                        lambda ins, outs, pos, k=k: outs[0].at[k], lambda ins, outs, pos, k=k: outs[0].at[k], kind)
                       for k, kind in enumerate(CHIP_KINDS)], then=lambda outs: got.update(rows=outs[0])), front=True)
    _exchange(pending.take(0.0, at_least_one=True), name="mod_exchange")
    own = lax.dynamic_slice_in_dim(mod_part, me * SUBLANES, SUBLANES, axis=1)
    parts = [own, got["rows"][0], got["rows"][1], got["rows"][2]]
    part_of = (0, 2, 1, 3)

    def shard(sidx):
        dist, out = jnp.bitwise_xor(mq, sidx), parts[0]
        for distance in (1, 2, 3):
            out = jnp.where(dist == distance, parts[part_of[distance]], out)
        return out

    mod = jnp.stack([shard(sidx) for sidx in range(N_CHIPS)])[:, :, :bsz]
    mod = jnp.transpose(mod, (1, 2, 0, 3)).reshape(depth, bsz, N_MOD, 1, d)
    mods = [[mod[i][:, k] for k in range(N_MOD)] for i in range(depth)]

    saved = []
    xc = x
    h1 = _normmod(xc, norm_mix[0][None], mods[0][1], mods[0][0], name="normmod")
    for i in range(depth):
        j = i // 2
        sh1, sc1, g1, sh2, sc2, g2 = mods[i]
        mlp_norm = (norm_mlp[i][None], sc2, sh2)
        if i % 2 == 0:
            wq, wo = weight(("qkv", j)), weight(("o", j))
            qkv = _mm(h1.reshape(t, d), wq, bias=b_qkv[j], tn=qkv_dim, name="mm_qkv",
                      comm=pending.take(SMALL)).reshape(bsz, s, qkv_dim)
            mix = _attn_fwd(qkv, sinks[j], name="attn_fwd", comm=pending.take(LARGE, True))
            y1, x1, h2 = _mm(mix.reshape(t, d), wo, bias=b_o[j], epi="resid", resid=xc.reshape(t, d), gate=g1, seq=s,
                             norm=mlp_norm, tn=d, name="mm_out", comm=pending.take(SMALL))
            extra = (qkv, mix)
        else:
            wp1, wp2 = weight(("pw1", j)), weight(("pw2", j))
            u = _mm(h1.reshape(t, d), wp1, bias=b_pw1_f[j], out_dtype=F32, tn=d, name="mm_pw1",
                    comm=pending.take(MEDIUM)).reshape(bsz, s, 2 * d)
            mix, conv_v = _conv_fwd(u, w_dw_f[j], b_dw_f[j][None], ln_g_f[j][None], ln_b_f[j][None], name="conv_fwd",
                                    comm=pending.take(LARGE, True))
            y1, x1, h2 = _mm(mix.reshape(t, d), wp2, bias=b_pw2_f[j], epi="resid", resid=xc.reshape(t, d), gate=g1, seq=s,
                             norm=mlp_norm, tn=d, name="mm_out", comm=pending.take(SMALL))
            extra = (u, mix, conv_v)
        act, slope = _mm(h2, weight(("up", i)), epi="relu2", tm=1024, tn=2 * d, name="mm_up", comm=pending.take(LARGE, True))
        mix_norm = (norm_mix[i + 1][None], mods[i + 1][1], mods[i + 1][0]) if i + 1 < depth else None
        y2, x2, *h_next = _mm(act, weight(("down", i)), epi="resid", resid=x1, gate=g2, seq=s, norm=mix_norm,
                              tm=512, tn=d, name="mm_down", comm=pending.take(LARGE, True))
        saved.append((xc, h1, extra, y1, x1.reshape(bsz, s, d), h2, act, y2, slope))
        xc = x2.reshape(bsz, s, d)
        h1 = h_next[0] if h_next else None
    pending.flush(LARGE)

    last_y2, last_g2 = saved[-1][7].reshape(bsz, s, d), mods[-1][5]
    dx, loss_cols, d_final, dyb, dg2, _ = _loss_head(xc, loss_target, final_norm[None], (last_y2, last_g2), name="loss_head")
    loss = lax.psum(0.5 / d * jnp.sum(loss_cols), ("x", "y", "c"))

    totals = {}

    def reduce_scatter(name, layer, n_layers, grad):
        view = grad.reshape(N_CHIPS, 2, grad.shape[1] // 2, grad.shape[2])
        half_shape = (N_CHIPS, 1) + view.shape[2:]

        def scatter(outs):
            chip = _add_half(view, outs[0], my_c, name="rs_chipsum")
            copies = [(lambda ins, outs, pos, kind=kind: ins[0].at[pl.ds(_peer_shard(kind, pos[0], pos[1]), 1)],
                       lambda ins, outs, pos, k=k: outs[0].at[pl.ds(k, 1)],
                       lambda ins, outs, pos, k=k: outs[0].at[pl.ds(k, 1)], kind) for k, kind in enumerate(CHIP_KINDS)]

            def total(outs):
                totals[name] = _add_pieces(chip, outs[0], my_qc, totals.get(name), layer, n_layers, name="rs_total")

            pending.add(_Plan("ici", 3 * _nbytes(chip.shape[1:], BF16) / MB, [chip],
                              [jax.ShapeDtypeStruct((3,) + chip.shape[1:], BF16)], copies, then=total))

        pending.add(_Plan("d2d", _nbytes(half_shape, F32) / MB, [view], [jax.ShapeDtypeStruct(half_shape, F32)],
                          [(lambda ins, outs, pos: ins[0].at[:, pl.ds(1 - pos[2], 1)],
                            lambda ins, outs, pos: outs[0], lambda ins, outs, pos: outs[0], "c")], then=scatter))

    dmods, small = [None] * depth, {}
    for i in reversed(range(depth)):
        j = i // 2
        xin, h1, extra, y1, x1, h2, act, _, slope = saved[i]
        sh1, sc1, g1, sh2, sc2, g2 = mods[i]
        dyb = dyb.reshape(t, d)
        reduce_scatter("down", i, depth,
                       _mm_tn(act, dyb, name="dw_down", comm=pending.take(LARGE)).reshape(N_CHIPS, -1, d))
        dup = _mm(dyb, weights["down", i], nt=True, epi="dact", act=slope, tm=1024, tn=2 * d, name="mm_dact",
                  comm=pending.take(LARGE))
        reduce_scatter("up", i, depth,
                       _mm_tn(h2.reshape(t, d), dup, col_shards=N_CHIPS, name="dw_up", comm=pending.take(LARGE)))
        dh2 = _mm(dup, weights["up", i], nt=True, tm=512, tn=d, name="mm_dh2",
                  comm=pending.take(LARGE)).reshape(bsz, s, d)
        dx1, p2, dsh2, dyb, dg1, sdx = _normmod_bwd(x1, dh2, dx, norm_mlp[i][None], sc2, (y1.reshape(bsz, s, d), g1),
                                                    name="normmod_bwd", comm=pending.take(SMALL))
        dyb = dyb.reshape(t, d)
        d_bias_out = jnp.sum(g1 * sdx, axis=(0, 1))
        if i % 2 == 0:
            qkv, mix = extra
            small["b_o", j] = d_bias_out
            reduce_scatter("o", j, n_attn,
                           _mm_tn(mix.reshape(t, d), dyb, name="dw_sq", comm=pending.take(SMALL)).reshape(N_CHIPS, -1, d))
            dmix = _mm(dyb, weights["o", j], nt=True, tn=d, name="mm_dmix", comm=pending.take(SMALL)).reshape(bsz, s, d)
            dqkv, d_bqkv, d_sink = _attn_bwd(qkv, dmix, sinks[j], name="attn_bwd", comm=pending.take(2 * LARGE))
            small["b_qkv", j], small["sinks", j] = d_bqkv[0], d_sink[:, 0]
            dqkv = dqkv.reshape(t, qkv_dim)
            dwq = _mm_tn(h1.reshape(t, d), dqkv, tn=qkv_dim, name="dw_qkv", comm=pending.take(SMALL))
            reduce_scatter("qkv", j, n_attn, jnp.transpose(dwq.reshape(d, N_CHIPS, -1), (1, 0, 2)))
            dh1 = _mm(dqkv, weights["qkv", j], nt=True, tn=d, name="mm_dh1a", comm=pending.take(SMALL))
        else:
            u, mix, conv_v = extra
            small["b_pw2", j] = d_bias_out
            reduce_scatter("pw2", j, n_conv,
                           _mm_tn(mix.reshape(t, d), dyb, name="dw_sq", comm=pending.take(SMALL)).reshape(N_CHIPS, -1, d))
            dz = _mm(dyb, weights["pw2", j], nt=True, out_dtype=F32, tn=d, name="mm_dz", comm=pending.take(SMALL)).reshape(bsz, s, d)
            dv, d_lng, d_lnb, d_bdw = _conv_bwd_ln(dz, conv_v, ln_g_f[j][None], ln_b_f[j][None], name="conv_bwd_ln")
            du, d_bpw1, d_wdw = _conv_bwd_taps(dv, u, w_dw_f[j], name="conv_bwd_taps", comm=pending.take(2 * LARGE))
            small["ln_g", j], small["ln_b", j], small["b_dw", j] = d_lng[0], d_lnb[0], d_bdw[0]
            small["b_pw1", j], small["w_dw", j] = d_bpw1[0], jnp.sum(d_wdw[:CONV_WIDTH], axis=1)
            du = du.reshape(t, 2 * d)
            reduce_scatter("pw1", j, n_conv,
                           _mm_tn(h1.reshape(t, d), du, col_shards=N_CHIPS, name="dw_pw1", comm=pending.take(MEDIUM)))
            dh1 = _mm(du, weights["pw1", j], nt=True, tn=d, name="mm_dh1c", comm=pending.take(MEDIUM))
        below = (saved[i - 1][7].reshape(bsz, s, d), mods[i - 1][5]) if i > 0 else None
        dx, p1, dsh1, *gate_grads = _normmod_bwd(xin, dh1.reshape(bsz, s, d), dx1, norm_mix[i][None], sc1, below,
                                                 name="normmod_bwd", comm=pending.take(SMALL, i == 0))
        small["norm_mix", i] = jnp.sum((1.0 + sc1) * p1, axis=(0, 1))
        small["norm_mlp", i] = jnp.sum((1.0 + sc2) * p2, axis=(0, 1))
        dmods[i] = jnp.concatenate([dsh1, norm_mix[i] * p1, dg1, dsh2, norm_mlp[i] * p2, dg2], axis=1)
        if i > 0:
            dyb, dg2, _ = gate_grads
    grad_x = dx

    small["final_norm"] = d_final[0]
    small_names = ([("norm_mix", i) for i in range(depth)] + [("norm_mlp", i) for i in range(depth)]
                   + [("b_o", j) for j in range(n_attn)]
                   + [(nm, j) for nm in ("b_pw1", "w_dw", "b_dw", "ln_g", "ln_b", "b_pw2") for j in range(n_conv)]
                   + ["final_norm"] + [(nm, j) for nm in ("b_qkv", "sinks") for j in range(n_attn)])
    small_list = [small[k] for k in small_names]
    small_pack = _pack(small_list, d)
    dmod_rows = jnp.stack(dmods).reshape(depth * bsz * N_MOD, d)
    n_small = small_pack.shape[0]
    pending.flush(LARGE)
    big_names = ["qkv", "o", "pw1", "pw2", "up", "down"]
    bufs = [totals[nm] for nm in big_names]
    copies = []
    for a, buf in enumerate(bufs):
        for layer in range(buf.shape[0]):
            half = lambda ref, h, layer=layer: ref.at[pl.ds(layer, 1), pl.ds(h, 1)]
            copies.append((lambda ins, outs, pos, a=a, half=half: half(ins[a], pos[2]),
                           lambda ins, outs, pos, a=a, half=half: half(outs[a], pos[2]),
                           lambda ins, outs, pos, a=a, half=half: half(outs[a], 1 - pos[2]), "c"))
    shared = {}
    share = _Plan("d2d", 0.0, bufs, [jax.ShapeDtypeStruct(b.shape, F32) for b in bufs], copies,
                  {a: a for a in range(len(bufs))}, lambda outs: shared.update(zip(big_names, outs)))
    gathered = _all_gather_small(jnp.concatenate([small_pack, _pack([dmod_rows], d)], axis=0), name="gather_small",
                                 comm=share)
    gathered = gathered.reshape(N_DEV, -1, d)
    dmod_all = gathered[:, n_small:n_small + depth * bsz * N_MOD].reshape(N_DEV, depth, bsz, N_MOD, d)
    small_sum, g_b_mod = _sum_devices(gathered[:, :n_small], dmod_all, name="sum_devices")
    small_tot = dict(zip(small_names, _unpack(small_sum, [a.shape for a in small_list])))
    stacked = lambda nm, count: jnp.stack([small_tot[nm, k] for k in range(count)])
    g_norm_mix, g_norm_mlp = stacked("norm_mix", depth), stacked("norm_mlp", depth)
    g_b_qkv, g_b_o, g_sinks = stacked("b_qkv", n_attn), stacked("b_o", n_attn), stacked("sinks", n_attn)
    g_final = small_tot["final_norm"]
    g_b_mod = g_b_mod.reshape(depth, N_MOD * d)
    shard_cols = lambda g: lax.dynamic_slice_in_dim(g, mq * (g.shape[-1] // N_CHIPS), g.shape[-1] // N_CHIPS, axis=g.ndim - 1)
    g_b_pw1, g_w_dw, g_b_dw, g_ln_g, g_ln_b, g_b_pw2 = [shard_cols(stacked(nm, n_conv))
                                                        for nm in ("b_pw1", "w_dw", "b_dw", "ln_g", "ln_b", "b_pw2")]

    dmod_cols = jnp.transpose(dmod_all, (1, 0, 2, 3, 4)).reshape(depth, N_DEV * bsz, N_MOD * d)
    dmod_cols = lax.dynamic_slice_in_dim(dmod_cols, mq * n_mod, n_mod, axis=2)
    g_w_mod = _mod_bwd(c_all, dmod_cols, name="mod_bwd")

    def adam(w, g, m, v, name):
        two_d = lambda a: a.reshape(-1, a.shape[-1])
        return [r.reshape(w.shape) for r in _adamw(two_d(w), two_d(g), two_d(m), two_d(v), name=name)]

    results = {"w_mod": (g_w_mod,) + tuple(adam(w_mod, g_w_mod, m_w_mod, v_w_mod, "adamw"))}

    g_w_qkv, g_w_o, g_w_pw1, g_w_pw2, g_w_up, g_w_down = [
        shared[nm].reshape(shared[nm].shape[0], -1, shared[nm].shape[3]) for nm in big_names]
    for nm, w, g, m, v in (("w_qkv", w_qkv, g_w_qkv, m_w_qkv, v_w_qkv),
                           ("w_o", w_o, g_w_o, m_w_o, v_w_o), ("w_pw1", w_pw1, g_w_pw1, m_w_pw1, v_w_pw1),
                           ("w_pw2", w_pw2, g_w_pw2, m_w_pw2, v_w_pw2), ("w_up", w_up, g_w_up, m_w_up, v_w_up),
                           ("w_down", w_down, g_w_down, m_w_down, v_w_down)):
        results[nm] = (g,) + tuple(adam(w, g, m, v, "adamw"))
    small_w = dict(b_mod=(b_mod, g_b_mod, m_b_mod, v_b_mod), norm_mix=(norm_mix, g_norm_mix, m_norm_mix, v_norm_mix),
                   norm_mlp=(norm_mlp, g_norm_mlp, m_norm_mlp, v_norm_mlp), b_qkv=(b_qkv, g_b_qkv, m_b_qkv, v_b_qkv),
                   b_o=(b_o, g_b_o, m_b_o, v_b_o), sinks=(sinks, g_sinks, m_sinks, v_sinks),
                   b_pw1=(b_pw1, g_b_pw1, m_b_pw1, v_b_pw1), w_dw=(w_dw, g_w_dw, m_w_dw, v_w_dw),
                   b_dw=(b_dw, g_b_dw, m_b_dw, v_b_dw), conv_ln_g=(conv_ln_g, g_ln_g, m_conv_ln_g, v_conv_ln_g),
                   conv_ln_b=(conv_ln_b, g_ln_b, m_conv_ln_b, v_conv_ln_b), b_pw2=(b_pw2, g_b_pw2, m_b_pw2, v_b_pw2),
                   final_norm=(final_norm, g_final, m_final_norm, v_final_norm))
    names = list(small_w)
    deltas, new_ms, new_vs = _adamw_packed(*[[small_w[nm][k] for nm in names] for k in range(4)], d, name="adamw_small")
    for k, nm in enumerate(names):
        results[nm] = (small_w[nm][1], deltas[k], new_ms[k], new_vs[k])

    weight_order = ["w_mod", "b_mod", "norm_mix", "norm_mlp", "w_qkv", "b_qkv", "w_o", "b_o", "sinks", "w_pw1", "b_pw1",
                    "w_dw", "b_dw", "conv_ln_g", "conv_ln_b", "w_pw2", "b_pw2", "w_up", "w_down", "final_norm"]
    return (loss, grad_x, *[results[nm][0] for nm in weight_order], *[results[nm][1] for nm in weight_order],
            *[results[nm][2] for nm in weight_order], *[results[nm][3] for nm in weight_order])
```

```python
import functools

import jax
import jax.numpy as jnp
from jax import lax
from jax.experimental import pallas as pl
from jax.experimental.pallas import tpu as pltpu

F32, BF16 = jnp.float32, jnp.bfloat16
MESH = pl.DeviceIdType.MESH
N_CHIPS = 4
N_DEV = 8
LANES = 128
SUBLANES = 8
VMEM_LIMIT = 48 * 1024 * 1024

NORM_EPS = 1e-6
HEAD_DIM = 64
N_KV_HEADS = 2
WINDOW = 128
CONV_WIDTH = 31
CONV_HALO = 32
CONV_ROWS = 32
N_MOD = 6

ADAM_LR, ADAM_B1, ADAM_B2, ADAM_EPS, ADAM_WD, ADAM_STEP = 0.001, 0.9, 0.999, 1e-08, 0.01, 10

HBM_SPEC = pl.BlockSpec(memory_space=pltpu.HBM)
VMEM_SPEC = pl.BlockSpec(memory_space=pltpu.VMEM)
SMEM_SPEC = pl.BlockSpec(memory_space=pltpu.SMEM)


def _params(*sem):
    return pltpu.CompilerParams(dimension_semantics=sem or None, vmem_limit_bytes=VMEM_LIMIT)


def _row_tile(rows, width_bytes, target=2 << 20):
    t = rows
    while t % 2 == 0 and t > SUBLANES and t * width_bytes > target:
        t //= 2
    return t


CHIP_KINDS = ("x", "y", "xy")


def _position():
    x, y, c = lax.axis_index("x"), lax.axis_index("y"), lax.axis_index("c")
    return x, y, c, 2 * x + y


def _peer(kind, x, y, c):
    return {"c": (x, y, 1 - c), "x": (1 - x, y, c), "y": (x, 1 - y, c), "xy": (1 - x, 1 - y, c)}[kind]


def _peer_shard(kind, x, y):
    px, py, _ = _peer(kind, x, y, 0)
    return 2 * px + py


class _Plan:
    def __init__(self, link, cost, operands, out_shapes, copies, aliases=None, then=None):
        self.link, self.cost = link, cost
        self.operands, self.out_shapes, self.copies = list(operands), list(out_shapes), list(copies)
        self.aliases, self.then = dict(aliases or {}), then


def _merge(plans):
    operands, out_shapes, copies, aliases, thens = [], [], [], {}, []
    for p in plans:
        i0, o0 = len(operands), len(out_shapes)
        i1, o1 = i0 + len(p.operands), o0 + len(p.out_shapes)

        def shifted(f, i0=i0, i1=i1, o0=o0, o1=o1):
            return lambda ins, outs, pos: f(ins[i0:i1], outs[o0:o1], pos)

        copies += [(shifted(src), shifted(dst), shifted(land), kind) for src, dst, land, kind in p.copies]
        aliases.update({i0 + k: o0 + v for k, v in p.aliases.items()})
        operands += p.operands
        out_shapes += p.out_shapes
        thens.append((p.then, o0, o1))

    def then(outs):
        for f, o0, o1 in thens:
            if f is not None:
                f(outs[o0:o1])

    return _Plan("mixed", sum(p.cost for p in plans), operands, out_shapes, copies, aliases, then)


def _call(body, *, name, out_shape, operands, grid=(), in_specs=(), out_specs=(), scratch_shapes=(), sem=(), comm=None):
    single = not isinstance(out_shape, (tuple, list))
    out_shape = [out_shape] if single else list(out_shape)
    out_specs = [out_specs] if single else list(out_specs)
    if comm is None:
        res = pl.pallas_call(body, out_shape=out_shape, grid=grid, in_specs=list(in_specs), out_specs=out_specs,
                             scratch_shapes=list(scratch_shapes), name=name, compiler_params=_params(*sem))(*operands)
        return res[0] if single else res
    n_in, n_out, n_scr = len(operands), len(out_shape), len(scratch_shapes)
    c_in, c_out, n_cp = len(comm.operands), len(comm.out_shapes), len(comm.copies)

    def wrapped(*refs):
        ins, refs = refs[:n_in], refs[n_in:]
        cins, refs = refs[:c_in], refs[c_in:]
        outs, refs = refs[:n_out], refs[n_out:]
        couts, refs = refs[:c_out], refs[c_out:]
        scr, (send_sems, recv_sems) = refs[:n_scr], refs[n_scr:]

        def descriptors():
            pos = _position()
            sends, lands = [], []
            for k, (src, dst, landing, kind) in enumerate(comm.copies):
                common = dict(send_sem=send_sems.at[k], recv_sem=recv_sems.at[k],
                              device_id=_peer(kind, *pos[:3]), device_id_type=MESH)
                sends.append(pltpu.make_async_remote_copy(src_ref=src(cins, couts, pos), dst_ref=dst(cins, couts, pos), **common))
                lands.append(pltpu.make_async_remote_copy(src_ref=src(cins, couts, pos), dst_ref=landing(cins, couts, pos), **common))
            return sends, lands

        def start():
            for cp in descriptors()[0]:
                cp.start()

        def finish():
            sends, lands = descriptors()
            for cp in lands:
                cp.wait_recv()
            for cp in sends:
                cp.wait_send()

        if not grid:
            start()
            body(*ins, *outs, *scr)
            finish()
        else:
            ids = [pl.program_id(ax) for ax in range(len(grid))]
            first, last = ids[0] == 0, ids[0] == grid[0] - 1
            for ax in range(1, len(grid)):
                first, last = first & (ids[ax] == 0), last & (ids[ax] == grid[ax] - 1)
            pl.when(first)(start)
            body(*ins, *outs, *scr)
            pl.when(last)(finish)

    res = pl.pallas_call(wrapped, out_shape=out_shape + comm.out_shapes, grid=grid,
                         in_specs=list(in_specs) + [HBM_SPEC] * c_in, out_specs=out_specs + [HBM_SPEC] * c_out,
                         scratch_shapes=list(scratch_shapes) + [pltpu.SemaphoreType.DMA((n_cp,)), pltpu.SemaphoreType.DMA((n_cp,))],
                         input_output_aliases={n_in + k: n_out + v for k, v in comm.aliases.items()},
                         name=name, compiler_params=_params(*["arbitrary"] * len(grid)))(*operands, *comm.operands)
    if comm.then is not None:
        comm.then(res[n_out:])
    return res[0] if single else res[:n_out]


def _exchange(plan, *, name):
    _call(lambda: None, name=name, out_shape=[], operands=[], comm=plan)


def _mm(a, b, *, name, nt=False, tm=1024, tn=512, epi="plain", out_dtype=BF16,
        bias=None, act=None, resid=None, gate=None, seq=None, norm=None, comm=None):
    m, k = a.shape
    n = b.shape[0] if nt else b.shape[1]
    tm, tn = min(tm, m, seq or m), min(tn, n)
    assert m % tm == 0 and n % tn == 0
    dims = (((1,), (1,)), ((), ())) if nt else (((1,), (0,)), ((), ()))
    tile = pl.BlockSpec((tm, tn), lambda j, i: (i, j))
    operands = [a, b]
    in_specs = [pl.BlockSpec((tm, k), lambda j, i: (i, 0)),
                pl.BlockSpec((tn, k), lambda j, i: (j, 0)) if nt else pl.BlockSpec((k, tn), lambda j, i: (0, j))]
    if bias is not None:
        operands.append(bias.reshape(1, n))
        in_specs.append(pl.BlockSpec((1, tn), lambda j, i: (0, j)))
    if epi == "dact":
        operands.append(act)
        in_specs.append(tile)
    if epi == "resid":
        assert seq % tm == 0
        per_ex = pl.BlockSpec((None, 1, tn), lambda j, i: (i * tm // seq, 0, j))
        operands += [resid, gate]
        in_specs += [tile, per_ex]
        out_shape = (jax.ShapeDtypeStruct((m, n), BF16), jax.ShapeDtypeStruct((m, n), F32))
        out_specs = (tile, tile)
        if norm is not None:
            assert tn == n
            operands += list(norm)
            in_specs += [pl.BlockSpec((1, tn), lambda j, i: (0, j)), per_ex, per_ex]
            out_shape += (jax.ShapeDtypeStruct((m, n), BF16),)
            out_specs += (tile,)
    elif epi == "relu2":
        out_shape = (jax.ShapeDtypeStruct((m, n), BF16), jax.ShapeDtypeStruct((m, n), BF16))
        out_specs = (tile, tile)
    else:
        out_shape = jax.ShapeDtypeStruct((m, n), out_dtype)
        out_specs = tile

    def body(*refs):
        it = iter(refs)
        a_ref, b_ref = next(it), next(it)
        acc = lax.dot_general(a_ref[...], b_ref[...], dims, preferred_element_type=F32)
        if bias is not None:
            acc = acc + next(it)[...]
        if epi == "plain":
            next(it)[...] = acc.astype(out_dtype)
        elif epi == "relu2":
            r = jnp.maximum(acc, 0.0)
            next(it)[...] = (r * r).astype(BF16)
            next(it)[...] = (2.0 * r).astype(BF16)
        elif epi == "dact":
            slope_ref = next(it)
            next(it)[...] = (acc * slope_ref[...].astype(F32)).astype(out_dtype)
        else:
            resid_ref, gate_ref = next(it), next(it)
            norm_refs = [next(it) for _ in (norm or ())]
            y_ref, x_ref = next(it), next(it)
            y_ref[...] = acc.astype(BF16)
            x_new = resid_ref[...] + gate_ref[...] * acc
            x_ref[...] = x_new
            if norm is not None:
                g_ref, sc_ref, sh_ref = norm_refs
                r = lax.rsqrt(jnp.mean(x_new * x_new, axis=-1, keepdims=True) + NORM_EPS)
                next(it)[...] = (x_new * r * g_ref[...] * (1.0 + sc_ref[...]) + sh_ref[...]).astype(BF16)

    return _call(body, name=name, out_shape=out_shape, operands=operands, grid=(n // tn, m // tm), in_specs=in_specs,
                 out_specs=out_specs, sem=("parallel", "parallel"), comm=comm)


def _mm_tn(a, b, *, name, tm=1024, tn=1024, tk=2048, col_shards=None, comm=None):
    t, m = a.shape
    n = b.shape[1]
    tm, tk = min(tm, m), min(tk, t)
    if col_shards is None:
        tn = min(tn, n)
        out_shape = jax.ShapeDtypeStruct((m, n), F32)
        out_spec = pl.BlockSpec((tm, tn), lambda i, j, k: (i, j))
    else:
        per = n // col_shards
        tn = min(tn, per)
        assert per % tn == 0
        out_shape = jax.ShapeDtypeStruct((col_shards, m, per), F32)
        out_spec = pl.BlockSpec((None, tm, tn), lambda i, j, k: (j // (per // tn), i, j % (per // tn)))
    assert m % tm == 0 and n % tn == 0 and t % tk == 0

    def body(a_ref, b_ref, o_ref):
        @pl.when(pl.program_id(2) == 0)
        def _():
            o_ref[...] = jnp.zeros_like(o_ref)

        o_ref[...] += lax.dot_general(a_ref[...], b_ref[...], (((0,), (0,)), ((), ())),
                                      preferred_element_type=F32)

    return _call(body, name=name, out_shape=out_shape, operands=[a, b], grid=(m // tm, n // tn, t // tk),
                 in_specs=[pl.BlockSpec((tk, tm), lambda i, j, k: (k, i)), pl.BlockSpec((tk, tn), lambda i, j, k: (k, j))],
                 out_specs=out_spec, sem=("parallel", "parallel", "arbitrary"), comm=comm)


def _normmod(x, gamma, sc, sh, *, name):
    bsz, s, d = x.shape
    ts = _row_tile(s, d * 4, 2 << 20)

    def body(x_ref, g_ref, sc_ref, sh_ref, o_ref):
        xf = x_ref[...]
        r = lax.rsqrt(jnp.mean(xf * xf, axis=-1, keepdims=True) + NORM_EPS)
        o_ref[...] = (xf * r * g_ref[...] * (1.0 + sc_ref[...]) + sh_ref[...]).astype(BF16)

    row = pl.BlockSpec((None, ts, d), lambda b, i: (b, i, 0))
    per_ex = pl.BlockSpec((None, 1, d), lambda b, i: (b, 0, 0))
    return pl.pallas_call(body, out_shape=jax.ShapeDtypeStruct(x.shape, BF16), grid=(bsz, s // ts),
                          in_specs=[row, pl.BlockSpec((1, d), lambda b, i: (0, 0)), per_ex, per_ex],
                          out_specs=row, name=name, compiler_params=_params("parallel", "parallel"))(x, gamma, sc, sh)


def _gate_grads(dxv, y_ref, gate_ref, dy_ref, dg_ref, sdx_ref):
    dy_ref[...] = (gate_ref[...] * dxv).astype(BF16)

    @pl.when(pl.program_id(1) == 0)
    def _():
        dg_ref[...] = jnp.zeros_like(dg_ref)
        sdx_ref[...] = jnp.zeros_like(sdx_ref)

    dg_ref[...] += jnp.sum(dxv * y_ref[...].astype(F32), axis=0, keepdims=True)
    sdx_ref[...] += jnp.sum(dxv, axis=0, keepdims=True)


def _normmod_bwd(x, dh, dres, gamma, sc, producer=None, *, name, comm=None):
    bsz, s, d = x.shape
    ts = _row_tile(s, d * 4, 2 << 20)

    def body(x_ref, dh_ref, dres_ref, g_ref, sc_ref, *rest):
        dx_ref, p_ref, dsh_ref = rest[-6:-3] if producer else rest
        xf = x_ref[...]
        r = lax.rsqrt(jnp.mean(xf * xf, axis=-1, keepdims=True) + NORM_EPS)
        xhat = xf * r
        dh_v = dh_ref[...].astype(F32)
        dxhat = dh_v * (g_ref[...] * (1.0 + sc_ref[...]))
        dxv = dres_ref[...] + r * (dxhat - xhat * jnp.mean(dxhat * xhat, axis=-1, keepdims=True))
        dx_ref[...] = dxv

        @pl.when(pl.program_id(1) == 0)
        def _():
            p_ref[...] = jnp.zeros_like(p_ref)
            dsh_ref[...] = jnp.zeros_like(dsh_ref)

        p_ref[...] += jnp.sum(dh_v * xhat, axis=0, keepdims=True)
        dsh_ref[...] += jnp.sum(dh_v, axis=0, keepdims=True)
        if producer:
            _gate_grads(dxv, rest[0], rest[1], *rest[-3:])

    row = pl.BlockSpec((None, ts, d), lambda b, i: (b, i, 0))
    per_ex = pl.BlockSpec((None, 1, d), lambda b, i: (b, 0, 0))
    vec = jax.ShapeDtypeStruct((bsz, 1, d), F32)
    gate_in, gate_out, gate_specs = (), (), ()
    if producer:
        gate_in, gate_out, gate_specs = (row, per_ex), (jax.ShapeDtypeStruct(x.shape, BF16), vec, vec), (row, per_ex, per_ex)
    return _call(body, name=name, out_shape=(jax.ShapeDtypeStruct(x.shape, F32), vec, vec) + gate_out,
                 operands=[x, dh, dres, gamma, sc, *(producer or ())], grid=(bsz, s // ts),
                 in_specs=[row, row, row, pl.BlockSpec((1, d), lambda b, i: (0, 0)), per_ex, *gate_in],
                 out_specs=(row, per_ex, per_ex) + gate_specs, sem=("parallel", "arbitrary"), comm=comm)


def _loss_head(x, target, gamma, producer, *, name):
    bsz, s, d = x.shape
    ts = _row_tile(s, d * 4, 2 << 20)

    def body(x_ref, t_ref, g_ref, y_ref, gate_ref, dx_ref, loss_ref, dg_ref, *gate_refs):
        xf = x_ref[...]
        r = lax.rsqrt(jnp.mean(xf * xf, axis=-1, keepdims=True) + NORM_EPS)
        xhat = xf * r
        err = xhat * g_ref[...] - t_ref[...]
        dy = err * (1.0 / d)
        dxhat = dy * g_ref[...]
        dxv = r * (dxhat - xhat * jnp.mean(dxhat * xhat, axis=-1, keepdims=True))
        dx_ref[...] = dxv

        @pl.when((pl.program_id(0) == 0) & (pl.program_id(1) == 0))
        def _():
            loss_ref[...] = jnp.zeros_like(loss_ref)
            dg_ref[...] = jnp.zeros_like(dg_ref)

        loss_ref[...] += jnp.sum(err * err, axis=0, keepdims=True)
        dg_ref[...] += jnp.sum(dy * xhat, axis=0, keepdims=True)
        _gate_grads(dxv, y_ref, gate_ref, *gate_refs)

    row = pl.BlockSpec((None, ts, d), lambda b, i: (b, i, 0))
    one = pl.BlockSpec((1, d), lambda b, i: (0, 0))
    per_ex = pl.BlockSpec((None, 1, d), lambda b, i: (b, 0, 0))
    vec, ex_vec = jax.ShapeDtypeStruct((1, d), F32), jax.ShapeDtypeStruct((bsz, 1, d), F32)
    return pl.pallas_call(body, out_shape=(jax.ShapeDtypeStruct(x.shape, F32), vec, vec,
                                           jax.ShapeDtypeStruct(x.shape, BF16), ex_vec, ex_vec), grid=(bsz, s // ts),
                          in_specs=[row, row, one, row, per_ex], out_specs=(row, one, one, row, per_ex, per_ex), name=name,
                          compiler_params=_params("arbitrary", "arbitrary"))(x, target, gamma, *producer)


def _alibi_slope(h, n_heads):
    return 2.0 ** (-8.0 * (h + 1) / n_heads)


def _attn_masks(first_block):
    qi = lax.broadcasted_iota(jnp.int32, (WINDOW, 2 * WINDOW), 0)
    ki = lax.broadcasted_iota(jnp.int32, (WINDOW, 2 * WINDOW), 1)
    dist = qi + WINDOW - ki
    first_key = jnp.where(first_block, WINDOW, 0)
    valid = (dist >= 0) & (dist < WINDOW) & (ki >= first_key)
    return dist.astype(F32), valid


def _dup_halves(span, kv, left):
    f = span.astype(F32)
    rolled = pltpu.roll(f, HEAD_DIM, axis=1)
    out = jnp.where(left, f, rolled) if kv == 0 else jnp.where(left, rolled, f)
    return out.astype(BF16)


def _attn_probs(s, h, n_heads, distf, valid, sink):
    s = s * (HEAD_DIM ** -0.5) - _alibi_slope(h, n_heads) * distf
    s = jnp.where(valid, s, -1e30)
    m = jnp.maximum(jnp.max(s, axis=-1, keepdims=True), sink)
    e = jnp.exp(s - m)
    e_sink = jnp.exp(sink - m)
    inv = 1.0 / (jnp.sum(e, axis=-1, keepdims=True) + e_sink)
    return e * inv, e_sink * inv


def _attn_specs(d, n_blocks, clamp):
    kcol = d // LANES
    cur = (lambda i: jnp.minimum(i, n_blocks - 1)) if clamp else (lambda i: i)
    prev = lambda i: jnp.maximum(cur(i) - 1, 0)
    kv = lambda col, blk: pl.BlockSpec((None, WINDOW, LANES), lambda b, i: (b, blk(i), col))
    return [pl.BlockSpec((None, WINDOW, d), lambda b, i: (b, cur(i), 0)),
            kv(kcol, prev), kv(kcol, cur), kv(kcol + 1, prev), kv(kcol + 1, cur)]


def _attn_fwd(qkv, sinks, *, name, comm=None):
    bsz, s, qkv_dim = qkv.shape
    d = qkv_dim - 2 * N_KV_HEADS * HEAD_DIM
    n_heads = d // HEAD_DIM
    group = n_heads // N_KV_HEADS
    pairs = group // 2
    n_blocks = s // WINDOW

    def body(q_ref, kp_ref, kc_ref, vp_ref, vc_ref, sink_ref, o_ref):
        left = lax.broadcasted_iota(jnp.int32, (1, LANES), 1) < HEAD_DIM
        distf, valid = _attn_masks(pl.program_id(1) == 0)
        kspan = jnp.concatenate([kp_ref[...], kc_ref[...]], axis=0)
        vspan = jnp.concatenate([vp_ref[...], vc_ref[...]], axis=0)
        for kv in range(N_KV_HEADS):
            kdup, vdup = _dup_halves(kspan, kv, left), _dup_halves(vspan, kv, left)
            res = []
            for par in range(2):
                keep = left if par == 0 else jnp.logical_not(left)
                cols = [pl.ds((kv * pairs + p) * LANES, LANES) for p in range(pairs)]
                lhs = jnp.concatenate([jnp.where(keep, q_ref[:, cl], jnp.zeros((), BF16)) for cl in cols], axis=0)
                sc = lax.dot_general(lhs, kdup, (((1,), (1,)), ((), ())), preferred_element_type=F32)
                probs = []
                for p in range(pairs):
                    h = kv * group + 2 * p + par
                    pr, _ = _attn_probs(sc[p * WINDOW:(p + 1) * WINDOW], h, n_heads, distf, valid, sink_ref[h])
                    probs.append(pr.astype(BF16))
                res.append(jnp.dot(jnp.concatenate(probs, axis=0), vdup, preferred_element_type=F32))
            for p in range(pairs):
                rows = slice(p * WINDOW, (p + 1) * WINDOW)
                o_ref[:, pl.ds((kv * pairs + p) * LANES, LANES)] = jnp.where(left, res[0][rows], res[1][rows]).astype(BF16)

    return _call(body, name=name, out_shape=jax.ShapeDtypeStruct((bsz, s, d), BF16), operands=[qkv] * 5 + [sinks],
                 grid=(bsz, n_blocks), in_specs=_attn_specs(d, n_blocks, False) + [SMEM_SPEC],
                 out_specs=pl.BlockSpec((None, WINDOW, d), lambda b, i: (b, i, 0)), sem=("parallel", "parallel"), comm=comm)


def _attn_bwd(qkv, do, sinks, *, name, comm=None):
    bsz, s, qkv_dim = qkv.shape
    d = qkv_dim - 2 * N_KV_HEADS * HEAD_DIM
    n_heads = d // HEAD_DIM
    group = n_heads // N_KV_HEADS
    pairs = group // 2
    n_blocks = s // WINDOW
    tn_dims = (((0,), (0,)), ((), ()))

    def body(q_ref, kp_ref, kc_ref, vp_ref, vc_ref, do_ref, sink_ref, dqkv_ref, colsum_ref, dsink_ref,
             dq_prev, dk_carry, dv_carry):
        b, i = pl.program_id(0), pl.program_id(1)
        left = lax.broadcasted_iota(jnp.int32, (1, LANES), 1) < HEAD_DIM

        @pl.when((b == 0) & (i == 0))
        def _():
            colsum_ref[...] = jnp.zeros_like(colsum_ref)
            dsink_ref[...] = jnp.zeros_like(dsink_ref)

        @pl.when(i == 0)
        def _():
            dqkv_ref[...] = jnp.zeros_like(dqkv_ref)
            dk_carry[...] = jnp.zeros_like(dk_carry)
            dv_carry[...] = jnp.zeros_like(dv_carry)

        @pl.when(i > 0)
        def _():
            dq_v = dq_prev[...]
            dqkv_ref[:, pl.ds(0, d)] = dq_v.astype(BF16)
            colsum_ref[:, pl.ds(0, d)] += jnp.sum(dq_v, axis=0, keepdims=True)

        @pl.when(i < n_blocks)
        def _():
            distf, valid = _attn_masks(i == 0)
            kspan = jnp.concatenate([kp_ref[...], kc_ref[...]], axis=0)
            vspan = jnp.concatenate([vp_ref[...], vc_ref[...]], axis=0)
            dk_blk, dv_blk = [], []
            for kv in range(N_KV_HEADS):
                kdup, vdup = _dup_halves(kspan, kv, left), _dup_halves(vspan, kv, left)
                dq_res, dk_sum, dv_sum = [], None, None
                for par in range(2):
                    keep = left if par == 0 else jnp.logical_not(left)
                    cols = [pl.ds((kv * pairs + p) * LANES, LANES) for p in range(pairs)]
                    zero = jnp.zeros((), BF16)
                    lhs = jnp.concatenate([jnp.where(keep, q_ref[:, cl], zero) for cl in cols], axis=0)
                    dol = jnp.concatenate([jnp.where(keep, do_ref[:, cl], zero) for cl in cols], axis=0)
                    sc = lax.dot_general(lhs, kdup, (((1,), (1,)), ((), ())), preferred_element_type=F32)
                    dp = lax.dot_general(dol, vdup, (((1,), (1,)), ((), ())), preferred_element_type=F32)
                    probs, dscores = [], []
                    for p in range(pairs):
                        h = kv * group + 2 * p + par
                        rows = slice(p * WINDOW, (p + 1) * WINDOW)
                        pr, p_sink = _attn_probs(sc[rows], h, n_heads, distf, valid, sink_ref[h])
                        delta = jnp.sum(pr * dp[rows], axis=-1, keepdims=True)
                        dscores.append((pr * (dp[rows] - delta) * (HEAD_DIM ** -0.5)).astype(BF16))
                        probs.append(pr.astype(BF16))
                        dsink_ref[pl.ds(h, 1), :] += jnp.zeros((1, LANES), F32) - jnp.sum(p_sink * delta)
                    ds_all = jnp.concatenate(dscores, axis=0)
                    p_all = jnp.concatenate(probs, axis=0)
                    dq_res.append(jnp.dot(ds_all, kdup, preferred_element_type=F32))
                    dk_par = lax.dot_general(ds_all, lhs, tn_dims, preferred_element_type=F32)
                    dv_par = lax.dot_general(p_all, dol, tn_dims, preferred_element_type=F32)
                    dk_sum = dk_par if dk_sum is None else dk_sum + dk_par
                    dv_sum = dv_par if dv_sum is None else dv_sum + dv_par
                for p in range(pairs):
                    rows = slice(p * WINDOW, (p + 1) * WINDOW)
                    dq_prev[:, pl.ds((kv * pairs + p) * LANES, LANES)] = jnp.where(left, dq_res[0][rows], dq_res[1][rows])
                dk_blk.append(dk_sum + pltpu.roll(dk_sum, HEAD_DIM, axis=1))
                dv_blk.append(dv_sum + pltpu.roll(dv_sum, HEAD_DIM, axis=1))
            dk_span = jnp.where(left, dk_blk[0], dk_blk[1])
            dv_span = jnp.where(left, dv_blk[0], dv_blk[1])
            dk_done = dk_carry[...] + dk_span[:WINDOW]
            dv_done = dv_carry[...] + dv_span[:WINDOW]
            dk_carry[...] = dk_span[WINDOW:]
            dv_carry[...] = dv_span[WINDOW:]

            @pl.when(i > 0)
            def _():
                dqkv_ref[:, pl.ds(d, LANES)] = dk_done.astype(BF16)
                dqkv_ref[:, pl.ds(d + LANES, LANES)] = dv_done.astype(BF16)
                colsum_ref[:, pl.ds(d, LANES)] += jnp.sum(dk_done, axis=0, keepdims=True)
                colsum_ref[:, pl.ds(d + LANES, LANES)] += jnp.sum(dv_done, axis=0, keepdims=True)

        @pl.when(i == n_blocks)
        def _():
            dk_done, dv_done = dk_carry[...], dv_carry[...]
            dqkv_ref[:, pl.ds(d, LANES)] = dk_done.astype(BF16)
            dqkv_ref[:, pl.ds(d + LANES, LANES)] = dv_done.astype(BF16)
            colsum_ref[:, pl.ds(d, LANES)] += jnp.sum(dk_done, axis=0, keepdims=True)
            colsum_ref[:, pl.ds(d + LANES, LANES)] += jnp.sum(dv_done, axis=0, keepdims=True)

    do_spec = pl.BlockSpec((None, WINDOW, d), lambda b, i: (b, jnp.minimum(i, n_blocks - 1), 0))
    out_shape = (jax.ShapeDtypeStruct((bsz, s, qkv_dim), BF16), jax.ShapeDtypeStruct((1, qkv_dim), F32),
                 jax.ShapeDtypeStruct((n_heads, LANES), F32))
    out_specs = (pl.BlockSpec((None, WINDOW, qkv_dim), lambda b, i: (b, jnp.maximum(i - 1, 0), 0)),
                 pl.BlockSpec((1, qkv_dim), lambda b, i: (0, 0)),
                 pl.BlockSpec((n_heads, LANES), lambda b, i: (0, 0)))
    return _call(body, name=name, out_shape=out_shape, operands=[qkv] * 5 + [do, sinks], grid=(bsz, n_blocks + 1),
                 in_specs=_attn_specs(d, n_blocks, True) + [do_spec, SMEM_SPEC], out_specs=out_specs,
                 scratch_shapes=[pltpu.VMEM((WINDOW, d), F32), pltpu.VMEM((WINDOW, LANES), F32), pltpu.VMEM((WINDOW, LANES), F32)],
                 sem=("arbitrary", "arbitrary"), comm=comm)


def _conv_tile(s):
    return min(256, s)


def _halo_specs(ts, width, s):
    per = ts // CONV_HALO
    prev = pl.BlockSpec((None, CONV_HALO, width), lambda b, i: (b, jnp.maximum(i * per - 1, 0), 0))
    nxt = pl.BlockSpec((None, CONV_HALO, width), lambda b, i: (b, jnp.minimum((i + 1) * per, s // CONV_HALO - 1), 0))
    cur = pl.BlockSpec((None, ts, width), lambda b, i: (b, i, 0))
    return prev, cur, nxt


def _glu(u, d):
    return u[:, :d] * jax.nn.sigmoid(u[:, d:])


def _store_shifted(shifted, value):
    rows = value.shape[0]
    shifted[0] = value
    for b in range(1, SUBLANES):
        shifted[b] = pltpu.roll(value, rows - b, axis=0)


def _at(shifted, base, offset, n_rows, lanes):
    return shifted[offset % SUBLANES, pl.ds(base + (offset - offset % SUBLANES), n_rows), lanes]


def _taps(w_ref, shifted, out_ref, ts, d, offset):
    for r0 in range(0, ts, CONV_ROWS):
        for l0 in range(0, d, LANES):
            lanes = pl.ds(l0, LANES)
            acc = jnp.zeros((CONV_ROWS, LANES), F32)
            for j in range(CONV_WIDTH):
                acc = acc + w_ref[pl.ds(j, 1), lanes] * _at(shifted, r0, offset(j), CONV_ROWS, lanes)
            out_ref[pl.ds(r0, CONV_ROWS), lanes] = acc


def _conv_fwd(u, w_dw, b_dw, ln_g, ln_b, *, name, comm=None):
    bsz, s, d2 = u.shape
    d = d2 // 2
    ts = _conv_tile(s)

    def body(up_ref, uc_ref, w_ref, bdw_ref, g_ref, b_ref, z_ref, v_ref, gbuf):
        halo = jnp.where(pl.program_id(1) > 0, _glu(up_ref[...], d), 0.0)
        _store_shifted(gbuf, jnp.concatenate([halo, _glu(uc_ref[...], d)], axis=0))
        _taps(w_ref, gbuf, v_ref, ts, d, lambda j: CONV_HALO - (CONV_WIDTH - 1) + j)
        v = v_ref[...] + bdw_ref[...]
        v_ref[...] = v
        mu = jnp.mean(v, axis=-1, keepdims=True)
        cen = v - mu
        rstd = lax.rsqrt(jnp.mean(cen * cen, axis=-1, keepdims=True) + NORM_EPS)
        ln = cen * rstd * g_ref[...] + b_ref[...]
        z_ref[...] = (ln * jax.nn.sigmoid(ln)).astype(BF16)

    prev, cur, _ = _halo_specs(ts, d2, s)
    one = pl.BlockSpec((1, d), lambda b, i: (0, 0))
    row = pl.BlockSpec((None, ts, d), lambda b, i: (b, i, 0))
    return _call(body, name=name, out_shape=(jax.ShapeDtypeStruct((bsz, s, d), BF16), jax.ShapeDtypeStruct((bsz, s, d), F32)),
                 operands=[u, u, w_dw, b_dw, ln_g, ln_b], grid=(bsz, s // ts),
                 in_specs=[prev, cur, pl.BlockSpec((CONV_HALO, d), lambda b, i: (0, 0)), one, one, one],
                 out_specs=(row, row), scratch_shapes=[pltpu.VMEM((SUBLANES, ts + CONV_HALO, d), F32)],
                 sem=("parallel", "parallel"), comm=comm)


def _conv_bwd_ln(dz, v, ln_g, ln_b, *, name):
    bsz, s, d = v.shape
    ts = _row_tile(s, d * 4, 2 << 20)

    def body(dz_ref, v_ref, g_ref, b_ref, dv_ref, dg_ref, db_ref, dbdw_ref):
        v_v = v_ref[...]
        mu = jnp.mean(v_v, axis=-1, keepdims=True)
        cen = v_v - mu
        rstd = lax.rsqrt(jnp.mean(cen * cen, axis=-1, keepdims=True) + NORM_EPS)
        vhat = cen * rstd
        ln = vhat * g_ref[...] + b_ref[...]
        sig = jax.nn.sigmoid(ln)
        dln = dz_ref[...] * (sig * (1.0 + ln * (1.0 - sig)))
        dvhat = dln * g_ref[...]
        dv = rstd * (dvhat - jnp.mean(dvhat, axis=-1, keepdims=True)
                     - vhat * jnp.mean(dvhat * vhat, axis=-1, keepdims=True))
        dv_ref[...] = dv

        @pl.when((pl.program_id(0) == 0) & (pl.program_id(1) == 0))
        def _():
            dg_ref[...] = jnp.zeros_like(dg_ref)
            db_ref[...] = jnp.zeros_like(db_ref)
            dbdw_ref[...] = jnp.zeros_like(dbdw_ref)

        dg_ref[...] += jnp.sum(dln * vhat, axis=0, keepdims=True)
        db_ref[...] += jnp.sum(dln, axis=0, keepdims=True)
        dbdw_ref[...] += jnp.sum(dv, axis=0, keepdims=True)

    row = pl.BlockSpec((None, ts, d), lambda b, i: (b, i, 0))
    one = pl.BlockSpec((1, d), lambda b, i: (0, 0))
    vec = jax.ShapeDtypeStruct((1, d), F32)
    return pl.pallas_call(body, out_shape=(jax.ShapeDtypeStruct(v.shape, F32), vec, vec, vec), grid=(bsz, s // ts),
                          in_specs=[row, row, one, one], out_specs=(row, one, one, one), name=name,
                          compiler_params=_params("arbitrary", "arbitrary"))(dz, v, ln_g, ln_b)


def _conv_bwd_taps(dv, u, w_dw, *, name, comm=None):
    bsz, s, d = dv.shape
    ts = _conv_tile(s)
    n_tiles = s // ts

    def body(dvc_ref, dvn_ref, uc_ref, w_ref, du_ref, dbu_ref, dw_ref, dvbuf, dglu, glu):
        i = pl.program_id(1)

        @pl.when((pl.program_id(0) == 0) & (i == 0))
        def _():
            dbu_ref[...] = jnp.zeros_like(dbu_ref)
            dw_ref[...] = jnp.zeros_like(dw_ref)

        u_v = uc_ref[...]
        a, sig = u_v[:, :d], jax.nn.sigmoid(u_v[:, d:])
        glu[...] = a * sig
        ahead = jnp.where(i < n_tiles - 1, dvn_ref[...], 0.0)
        _store_shifted(dvbuf, jnp.concatenate([dvc_ref[...], ahead], axis=0))
        _taps(w_ref, dvbuf, dglu, ts, d, lambda j: CONV_WIDTH - 1 - j)
        for l0 in range(0, d, LANES):
            lanes = pl.ds(l0, LANES)
            for j in range(CONV_WIDTH):
                acc = jnp.zeros((SUBLANES, LANES), F32)
                for r0 in range(0, ts, CONV_ROWS):
                    prod = glu[pl.ds(r0, CONV_ROWS), lanes] * _at(dvbuf, r0, CONV_WIDTH - 1 - j, CONV_ROWS, lanes)
                    for k in range(0, CONV_ROWS, SUBLANES):
                        acc = acc + prod[k:k + SUBLANES]
                dw_ref[j, :, lanes] += acc
        dg_v = dglu[...]
        da = dg_v * sig
        dgate = dg_v * a * sig * (1.0 - sig)
        du_ref[:, pl.ds(0, d)] = da.astype(BF16)
        du_ref[:, pl.ds(d, d)] = dgate.astype(BF16)
        dbu_ref[:, pl.ds(0, d)] += jnp.sum(da, axis=0, keepdims=True)
        dbu_ref[:, pl.ds(d, d)] += jnp.sum(dgate, axis=0, keepdims=True)

    _, dv_cur, dv_next = _halo_specs(ts, d, s)
    _, u_cur, _ = _halo_specs(ts, 2 * d, s)
    out_shape = (jax.ShapeDtypeStruct((bsz, s, 2 * d), BF16), jax.ShapeDtypeStruct((1, 2 * d), F32),
                 jax.ShapeDtypeStruct((CONV_HALO, SUBLANES, d), F32))
    out_specs = (pl.BlockSpec((None, ts, 2 * d), lambda b, i: (b, i, 0)), pl.BlockSpec((1, 2 * d), lambda b, i: (0, 0)),
                 pl.BlockSpec((CONV_HALO, SUBLANES, d), lambda b, i: (0, 0, 0)))
    return _call(body, name=name, out_shape=out_shape, operands=[dv, dv, u, w_dw], grid=(bsz, n_tiles),
                 in_specs=[dv_cur, dv_next, u_cur, pl.BlockSpec((CONV_HALO, d), lambda b, i: (0, 0))],
                 out_specs=out_specs,
                 scratch_shapes=[pltpu.VMEM((SUBLANES, ts + CONV_HALO, d), F32), pltpu.VMEM((ts, d), F32),
                                 pltpu.VMEM((ts, d), F32)],
                 sem=("arbitrary", "arbitrary"), comm=comm)


def _mod_fwd(c_all, w_mod, b_mod, *, name):
    n_layers, d, n = w_mod.shape
    rows = c_all.shape[0]

    def body(c_ref, w_ref, b_ref, o_ref):
        cv = c_ref[...]
        cs = (cv * jax.nn.sigmoid(cv)).astype(BF16)
        o_ref[...] = jnp.dot(cs, w_ref[...].astype(BF16), preferred_element_type=F32) + b_ref[...]

    return pl.pallas_call(body, out_shape=jax.ShapeDtypeStruct((n_layers, rows, n), F32), grid=(n_layers,),
                          in_specs=[pl.BlockSpec((rows, d), lambda l: (0, 0)), pl.BlockSpec((None, d, n), lambda l: (l, 0, 0)),
                                    pl.BlockSpec((None, 1, n), lambda l: (l, 0, 0))],
                          out_specs=pl.BlockSpec((None, rows, n), lambda l: (l, 0, 0)), name=name,
                          compiler_params=_params("parallel"))(c_all, w_mod, b_mod)


def _mod_bwd(c_all, dmod, *, name):
    n_layers, rows, n = dmod.shape
    d = c_all.shape[1]

    def body(c_ref, g_ref, o_ref):
        cv = c_ref[...]
        cs = (cv * jax.nn.sigmoid(cv)).astype(BF16)
        o_ref[...] = lax.dot_general(cs, g_ref[...].astype(BF16), (((0,), (0,)), ((), ())), preferred_element_type=F32)

    return pl.pallas_call(body, out_shape=jax.ShapeDtypeStruct((n_layers, d, n), F32), grid=(n_layers,),
                          in_specs=[pl.BlockSpec((rows, d), lambda l: (0, 0)), pl.BlockSpec((None, rows, n), lambda l: (l, 0, 0))],
                          out_specs=pl.BlockSpec((None, d, n), lambda l: (l, 0, 0)), name=name,
                          compiler_params=_params("parallel"))(c_all, dmod)


def _add_half(g, recv, my_c, *, name):
    p, _, r, cdim = g.shape
    tr = _row_tile(r, cdim * 4)

    def body(c_ref, g_ref, r_ref, o_ref):
        o_ref[...] = (g_ref[...] + r_ref[...]).astype(BF16)

    grid_spec = pltpu.PrefetchScalarGridSpec(
        num_scalar_prefetch=1, grid=(p, r // tr),
        in_specs=[pl.BlockSpec((None, None, tr, cdim), lambda q, i, c_ref: (q, c_ref[0], i, 0)),
                  pl.BlockSpec((None, None, tr, cdim), lambda q, i, c_ref: (q, 0, i, 0))],
        out_specs=pl.BlockSpec((None, tr, cdim), lambda q, i, c_ref: (q, i, 0)))
    return pl.pallas_call(body, out_shape=jax.ShapeDtypeStruct((p, r, cdim), BF16), grid_spec=grid_spec, name=name,
                          compiler_params=_params("parallel", "parallel"))(my_c, g, recv)


def _add_pieces(chip, recv, my_qc, stacked, layer, n_layers, *, name):
    _, r, cdim = chip.shape
    tr = _row_tile(r, cdim * 4)

    def body(qc_ref, own_ref, r0_ref, r1_ref, r2_ref, *rest):
        f32 = lambda ref: ref[...].astype(F32)
        rest[-1][...] = ((f32(own_ref) + f32(r0_ref)) + f32(r1_ref)) + f32(r2_ref)

    piece = lambda k: pl.BlockSpec((None, tr, cdim), lambda i, qc_ref: (k, i, 0))
    in_specs = [pl.BlockSpec((None, tr, cdim), lambda i, qc_ref: (qc_ref[0], i, 0)), piece(0), piece(1), piece(2)]
    operands = [my_qc, chip, recv, recv, recv]
    aliases = {}
    if stacked is not None:
        in_specs.append(pl.BlockSpec(memory_space=pl.ANY))
        operands.append(stacked)
        aliases = {len(operands) - 1: 0}
    grid_spec = pltpu.PrefetchScalarGridSpec(
        num_scalar_prefetch=1, grid=(r // tr,), in_specs=in_specs,
        out_specs=pl.BlockSpec((None, None, tr, cdim), lambda i, qc_ref: (layer, qc_ref[1], i, 0)))
    return pl.pallas_call(body, out_shape=jax.ShapeDtypeStruct((n_layers, 2, r, cdim), F32), grid_spec=grid_spec,
                          input_output_aliases=aliases, name=name, compiler_params=_params("parallel"))(*operands)


def _sum_devices(parts, dmod, *, name):
    def body(p_ref, m_ref, o_ref, b_ref):
        acc = p_ref[0]
        for k in range(1, N_DEV):
            acc = acc + p_ref[k]
        o_ref[...] = acc
        tot = None
        for k in range(N_DEV):
            for e in range(dmod.shape[2]):
                tot = m_ref[k, :, e] if tot is None else tot + m_ref[k, :, e]
        b_ref[...] = tot

    out_shape = (jax.ShapeDtypeStruct(parts.shape[1:], F32),
                 jax.ShapeDtypeStruct((dmod.shape[1],) + dmod.shape[3:], F32))
    return pl.pallas_call(body, out_shape=out_shape, in_specs=[VMEM_SPEC, VMEM_SPEC], out_specs=(VMEM_SPEC, VMEM_SPEC),
                          name=name, compiler_params=_params())(parts, dmod)


def _adamw(w, g, m, v, *, name, emit_grad=False, comm=None):
    r, cdim = w.shape
    tr = _row_tile(r, cdim * 4, 2 << 20)

    def body(w_ref, g_ref, m_ref, v_ref, d_ref, nm_ref, nv_ref, *grad_out):
        gv = g_ref[...]
        for ref in grad_out:
            ref[...] = gv
        nm = ADAM_B1 * m_ref[...] + (1.0 - ADAM_B1) * gv
        nv = ADAM_B2 * v_ref[...] + (1.0 - ADAM_B2) * (gv * gv)
        m_hat = nm / (1.0 - ADAM_B1 ** ADAM_STEP)
        v_hat = nv / (1.0 - ADAM_B2 ** ADAM_STEP)
        d_ref[...] = -ADAM_LR * (m_hat / (jnp.sqrt(v_hat) + ADAM_EPS) + ADAM_WD * w_ref[...])
        nm_ref[...] = nm
        nv_ref[...] = nv

    row = pl.BlockSpec((tr, cdim), lambda i: (i, 0))
    shape = jax.ShapeDtypeStruct((r, cdim), F32)
    n_out = 4 if emit_grad else 3
    return _call(body, name=name, out_shape=(shape,) * n_out, operands=[w, g, m, v], grid=(r // tr,),
                 in_specs=[row] * 4, out_specs=(row,) * n_out, sem=("parallel",), comm=comm)


def _all_gather_small(block, *, name, comm=None):
    m_per, n = block.shape

    def body(x_ref, out_ref, send_sems, recv_sems, local_sem):
        x, y, c, _ = _position()
        me, sibling = (x, y, c), (x, y, 1 - c)
        chips = [_peer(k, x, y, c)[:2] for k in CHIP_KINDS]

        def rows(px, py, pc):
            return out_ref.at[pl.ds((4 * px + 2 * py + pc) * m_per, m_per), :]

        def copy(k, blk, to, src=None):
            return pltpu.make_async_remote_copy(src_ref=rows(*blk) if src is None else src, dst_ref=rows(*blk),
                                                send_sem=send_sems.at[k], recv_sem=recv_sems.at[k],
                                                device_id=to, device_id_type=MESH)

        mine = pltpu.make_async_copy(x_ref, rows(*me), local_sem)
        mine.start()
        first = [copy(0, me, sibling, src=x_ref)]
        first += [copy(1 + j, me, (*chip, c), src=x_ref) for j, chip in enumerate(chips)]
        for cp in first:
            cp.start()
        passed = [copy(4 + j, (*chip, c), sibling) for j, chip in enumerate(chips)]
        for j, chip in enumerate(chips):
            copy(1 + j, (*chip, c), me).wait_recv()
            passed[j].start()
        copy(0, sibling, me).wait_recv()
        for j, chip in enumerate(chips):
            copy(4 + j, (*chip, 1 - c), me).wait_recv()
        for cp in first + passed:
            cp.wait_send()
        mine.wait()

    return _call(body, name=name, out_shape=jax.ShapeDtypeStruct((N_DEV * m_per, n), block.dtype), operands=[block],
                 in_specs=[VMEM_SPEC], out_specs=VMEM_SPEC,
                 scratch_shapes=[pltpu.SemaphoreType.DMA((7,)), pltpu.SemaphoreType.DMA((7,)), pltpu.SemaphoreType.DMA],
                 comm=comm)


def _pack(arrays, width):
    flat = jnp.concatenate([a.reshape(-1).astype(F32) for a in arrays])
    rows = -(-flat.shape[0] // width)
    rows = -(-rows // SUBLANES) * SUBLANES
    return jnp.pad(flat, (0, rows * width - flat.shape[0])).reshape(rows, width)


def _unpack(packed, shapes):
    flat, out, off = packed.reshape(-1), [], 0
    for shp in shapes:
        size = 1
        for dim in shp:
            size *= dim
        out.append(flat[off:off + size].reshape(shp))
        off += size
    return out


def _adamw_packed(ws, gs, ms, vs, width, *, name):
    shapes = [w.shape for w in ws]
    res = _adamw(_pack(ws, width), _pack(gs, width), _pack(ms, width), _pack(vs, width), name=name)
    return [_unpack(r, shapes) for r in res]


MB = float(1 << 20)


def _nbytes(shape, dtype):
    size = jnp.dtype(dtype).itemsize
    for dim in shape:
        size *= dim
    return size


def _gather_plans(shard, side_by_side, on_ready):
    r, cdim = shard.shape
    half = r // 2
    if side_by_side:
        assert cdim % LANES == 0
        buf = jax.ShapeDtypeStruct((r, N_CHIPS * cdim), shard.dtype)
        rows = lambda ref, q, h: ref.at[pl.ds(h * half, half), pl.ds(q * cdim, cdim)]
        whole = lambda ref, q: ref.at[:, pl.ds(q * cdim, cdim)]
        src, src_rows = shard, (lambda ref, h: ref.at[pl.ds(h * half, half)])
    else:
        buf = jax.ShapeDtypeStruct((N_CHIPS, r, cdim), shard.dtype)
        rows = lambda ref, q, h: ref.at[pl.ds(q, 1), pl.ds(h * half, half)]
        whole = lambda ref, q: ref.at[pl.ds(q, 1)]
        src, src_rows = shard[None], (lambda ref, h: ref.at[:, pl.ds(h * half, half)])
    there = lambda kind, pos: _peer_shard(kind, pos[0], pos[1])
    cost = 3 * _nbytes((half, cdim), shard.dtype) / MB

    def forward(outs):
        copies = [(lambda ins, outs, pos, kind=kind: rows(ins[0], there(kind, pos), pos[2]),
                   lambda ins, outs, pos, kind=kind: rows(outs[0], there(kind, pos), pos[2]),
                   lambda ins, outs, pos, kind=kind: rows(outs[0], there(kind, pos), 1 - pos[2]), "c") for kind in CHIP_KINDS]
        return _Plan("d2d", cost, [outs[0]], [jax.ShapeDtypeStruct(buf.shape, buf.dtype)], copies, {0: 0},
                     lambda done: on_ready(done[0]))

    copies = [(lambda ins, outs, pos: src_rows(ins[0], pos[2]),
               lambda ins, outs, pos: rows(outs[0], pos[3], pos[2]),
               lambda ins, outs, pos, kind=kind: rows(outs[0], there(kind, pos), pos[2]), kind) for kind in CHIP_KINDS]
    own = lambda ins, outs, pos: whole(outs[0], pos[3])
    copies.append((lambda ins, outs, pos: ins[0], own, own, "c"))
    return _Plan("ici", cost, [src], [buf], copies), forward


class _Exchanges:
    def __init__(self):
        self.queue = []

    def add(self, plan, front=False):
        if front:
            self.queue.insert(0, plan)
        else:
            self.queue.append(plan)

    def take(self, budget_mb, at_least_one=False):
        chosen, spent = [p for p in self.queue if p.link == "d2d"], 0.0
        for p in self.queue:
            if p.link == "ici" and (spent + p.cost <= budget_mb or (at_least_one and spent == 0.0)):
                chosen.append(p)
                spent += p.cost
        if not chosen:
            return None
        self.queue = [p for p in self.queue if all(p is not ch for ch in chosen)]
        return _merge(chosen)

    def flush(self, budget_mb, until=lambda: False):
        while self.queue and not until():
            _exchange(self.take(budget_mb, at_least_one=True), name="exchange")


def kernel(x, c, w_mod, b_mod, norm_mix, norm_mlp, w_qkv, b_qkv, w_o, b_o, sinks, w_pw1, b_pw1, w_dw, b_dw, conv_ln_g, conv_ln_b, w_pw2, b_pw2, w_up, w_down, final_norm, loss_target, m_w_mod, m_b_mod, m_norm_mix, m_norm_mlp, m_w_qkv, m_b_qkv, m_w_o, m_b_o, m_sinks, m_w_pw1, m_b_pw1, m_w_dw, m_b_dw, m_conv_ln_g, m_conv_ln_b, m_w_pw2, m_b_pw2, m_w_up, m_w_down, m_final_norm, v_w_mod, v_b_mod, v_norm_mix, v_norm_mlp, v_w_qkv, v_b_qkv, v_w_o, v_b_o, v_sinks, v_w_pw1, v_b_pw1, v_w_dw, v_b_dw, v_conv_ln_g, v_conv_ln_b, v_w_pw2, v_b_pw2, v_w_up, v_w_down, v_final_norm):
    bsz, s, d = x.shape
    t = bsz * s
    depth = w_mod.shape[0]
    n_attn, n_conv = w_qkv.shape[0], w_pw1.shape[0]
    qkv_dim = d + 2 * N_KV_HEADS * HEAD_DIM
    mx, my, mc, mq = _position()
    me = 4 * mx + 2 * my + mc
    my_c = jnp.reshape(mc, (1,)).astype(jnp.int32)
    my_qc = jnp.stack([mq, mc]).astype(jnp.int32)
    pending = _Exchanges()
    SMALL, MEDIUM, LARGE = 2.5, 3.5, 6.5

    weights = {}
    order = []
    for i in range(depth):
        j = i // 2
        order += ([(("qkv", j), w_qkv[j], True), (("o", j), w_o[j], False)] if i % 2 == 0 else
                  [(("pw1", j), w_pw1[j], True), (("pw2", j), w_pw2[j], False)])
        order += [(("up", i), w_up[i], True), (("down", i), w_down[i], False)]
    for key, shard, by_cols in order:
        side_by_side = by_cols and shard.shape[1] % LANES == 0

        def ready(buf, key=key, by_cols=by_cols):
            if buf.ndim == 3:
                buf = jnp.transpose(buf, (1, 0, 2)).reshape(buf.shape[1], -1) if by_cols else buf.reshape(-1, buf.shape[2])
            weights[key] = buf
        ici, forward = _gather_plans(shard.astype(BF16), side_by_side, ready)
        ici.then = lambda outs, forward=forward: pending.add(forward(outs), front=True)
        pending.add(ici)

    def weight(key):
        pending.flush(SMALL, until=lambda: key in weights)
        return weights[key]

    small_sharded = [b_pw1, w_dw, b_dw, conv_ln_g, conv_ln_b, b_pw2]
    c_pad = jnp.pad(c, ((0, SUBLANES - bsz), (0, 0)))
    gathered = _all_gather_small(jnp.concatenate([c_pad, _pack(small_sharded, d)], axis=0), name="gather_c",
                                 comm=pending.take(SMALL, True))
    gathered = gathered.reshape(N_DEV, -1, d)
    c_all = gathered[:, :bsz].reshape(N_DEV * bsz, d)
    per_chip = [_unpack(gathered[2 * q, SUBLANES:], [a.shape for a in small_sharded]) for q in range(N_CHIPS)]
    b_pw1_f, w_dw_f, b_dw_f, ln_g_f, ln_b_f, b_pw2_f = [jnp.concatenate([per_chip[q][k] for q in range(N_CHIPS)], axis=-1)
                                                           for k in range(len(small_sharded))]
    w_dw_f = jnp.pad(w_dw_f, ((0, 0), (0, CONV_HALO - CONV_WIDTH), (0, 0)))

    n_mod = w_mod.shape[2]
    b_mod_cols = lax.dynamic_slice_in_dim(b_mod, mq * n_mod, n_mod, axis=1).reshape(depth, 1, n_mod)
    c_all8 = gathered[:, :SUBLANES].reshape(N_DEV * SUBLANES, d)
    mod_part = _mod_fwd(c_all8, w_mod, b_mod_cols, name="mod_fwd")
    peer_rows = lambda kind, pos: pl.ds(SUBLANES * (2 * _peer_shard(kind, pos[0], pos[1]) + pos[2]), SUBLANES)
    got = {}
    pending.add(_Plan("ici", 0.01, [mod_part],[jax.ShapeDtypeStruct((len(CHIP_KINDS), depth, SUBLANES, n_mod), F32)],
                      [(lambda ins, outs, pos, kind=kind: ins[0].at[:, peer_rows(kind, pos)],
                        lambda ins, outs, pos, k=k: outs[0].at[k], lambda ins, outs, pos, k=k: outs[0].at[k], kind)
                       for k, kind in enumerate(CHIP_KINDS)], then=lambda outs: got.update(rows=outs[0])), front=True)
    _exchange(pending.take(0.0, at_least_one=True), name="mod_exchange")
    own = lax.dynamic_slice_in_dim(mod_part, me * SUBLANES, SUBLANES, axis=1)
    parts = [own, got["rows"][0], got["rows"][1], got["rows"][2]]
    part_of = (0, 2, 1, 3)

    def shard(sidx):
        dist, out = jnp.bitwise_xor(mq, sidx), parts[0]
        for distance in (1, 2, 3):
            out = jnp.where(dist == distance, parts[part_of[distance]], out)
        return out

    mod = jnp.stack([shard(sidx) for sidx in range(N_CHIPS)])[:, :, :bsz]
    mod = jnp.transpose(mod, (1, 2, 0, 3)).reshape(depth, bsz, N_MOD, 1, d)
    mods = [[mod[i][:, k] for k in range(N_MOD)] for i in range(depth)]

    saved = []
    xc = x
    h1 = _normmod(xc, norm_mix[0][None], mods[0][1], mods[0][0], name="normmod")
    for i in range(depth):
        j = i // 2
        sh1, sc1, g1, sh2, sc2, g2 = mods[i]
        mlp_norm = (norm_mlp[i][None], sc2, sh2)
        if i % 2 == 0:
            wq, wo = weight(("qkv", j)), weight(("o", j))
            qkv = _mm(h1.reshape(t, d), wq, bias=b_qkv[j], tn=qkv_dim, name="mm_qkv",
                      comm=pending.take(SMALL)).reshape(bsz, s, qkv_dim)
            mix = _attn_fwd(qkv, sinks[j], name="attn_fwd", comm=pending.take(LARGE, True))
            y1, x1, h2 = _mm(mix.reshape(t, d), wo, bias=b_o[j], epi="resid", resid=xc.reshape(t, d), gate=g1, seq=s,
                             norm=mlp_norm, tn=d, name="mm_out", comm=pending.take(SMALL))
            extra = (qkv, mix)
        else:
            wp1, wp2 = weight(("pw1", j)), weight(("pw2", j))
            u = _mm(h1.reshape(t, d), wp1, bias=b_pw1_f[j], out_dtype=F32, tn=d, name="mm_pw1",
                    comm=pending.take(MEDIUM)).reshape(bsz, s, 2 * d)
            mix, conv_v = _conv_fwd(u, w_dw_f[j], b_dw_f[j][None], ln_g_f[j][None], ln_b_f[j][None], name="conv_fwd",
                                    comm=pending.take(LARGE, True))
            y1, x1, h2 = _mm(mix.reshape(t, d), wp2, bias=b_pw2_f[j], epi="resid", resid=xc.reshape(t, d), gate=g1, seq=s,
                             norm=mlp_norm, tn=d, name="mm_out", comm=pending.take(SMALL))
            extra = (u, mix, conv_v)
        act, slope = _mm(h2, weight(("up", i)), epi="relu2", tm=1024, tn=2 * d, name="mm_up", comm=pending.take(LARGE, True))
        mix_norm = (norm_mix[i + 1][None], mods[i + 1][1], mods[i + 1][0]) if i + 1 < depth else None
        y2, x2, *h_next = _mm(act, weight(("down", i)), epi="resid", resid=x1, gate=g2, seq=s, norm=mix_norm,
                              tm=512, tn=d, name="mm_down", comm=pending.take(LARGE, True))
        saved.append((xc, h1, extra, y1, x1.reshape(bsz, s, d), h2, act, y2, slope))
        xc = x2.reshape(bsz, s, d)
        h1 = h_next[0] if h_next else None
    pending.flush(LARGE)

    last_y2, last_g2 = saved[-1][7].reshape(bsz, s, d), mods[-1][5]
    dx, loss_cols, d_final, dyb, dg2, _ = _loss_head(xc, loss_target, final_norm[None], (last_y2, last_g2), name="loss_head")
    loss = lax.psum(0.5 / d * jnp.sum(loss_cols), ("x", "y", "c"))

    totals = {}

    def reduce_scatter(name, layer, n_layers, grad):
        view = grad.reshape(N_CHIPS, 2, grad.shape[1] // 2, grad.shape[2])
        half_shape = (N_CHIPS, 1) + view.shape[2:]

        def scatter(outs):
            chip = _add_half(view, outs[0], my_c, name="rs_chipsum")
            copies = [(lambda ins, outs, pos, kind=kind: ins[0].at[pl.ds(_peer_shard(kind, pos[0], pos[1]), 1)],
                       lambda ins, outs, pos, k=k: outs[0].at[pl.ds(k, 1)],
                       lambda ins, outs, pos, k=k: outs[0].at[pl.ds(k, 1)], kind) for k, kind in enumerate(CHIP_KINDS)]

            def total(outs):
                totals[name] = _add_pieces(chip, outs[0], my_qc, totals.get(name), layer, n_layers, name="rs_total")

            pending.add(_Plan("ici", 3 * _nbytes(chip.shape[1:], BF16) / MB, [chip],
                              [jax.ShapeDtypeStruct((3,) + chip.shape[1:], BF16)], copies, then=total))

        pending.add(_Plan("d2d", _nbytes(half_shape, F32) / MB, [view], [jax.ShapeDtypeStruct(half_shape, F32)],
                          [(lambda ins, outs, pos: ins[0].at[:, pl.ds(1 - pos[2], 1)],
                            lambda ins, outs, pos: outs[0], lambda ins, outs, pos: outs[0], "c")], then=scatter))

    dmods, small = [None] * depth, {}
    for i in reversed(range(depth)):
        j = i // 2
        xin, h1, extra, y1, x1, h2, act, _, slope = saved[i]
        sh1, sc1, g1, sh2, sc2, g2 = mods[i]
        dyb = dyb.reshape(t, d)
        reduce_scatter("down", i, depth,
                       _mm_tn(act, dyb, name="dw_down", comm=pending.take(LARGE)).reshape(N_CHIPS, -1, d))
        dup = _mm(dyb, weights["down", i], nt=True, epi="dact", act=slope, tm=1024, tn=2 * d, name="mm_dact",
                  comm=pending.take(LARGE))
        reduce_scatter("up", i, depth,
                       _mm_tn(h2.reshape(t, d), dup, col_shards=N_CHIPS, name="dw_up", comm=pending.take(LARGE)))
        dh2 = _mm(dup, weights["up", i], nt=True, tm=512, tn=d, name="mm_dh2",
                  comm=pending.take(LARGE)).reshape(bsz, s, d)
        dx1, p2, dsh2, dyb, dg1, sdx = _normmod_bwd(x1, dh2, dx, norm_mlp[i][None], sc2, (y1.reshape(bsz, s, d), g1),
                                                    name="normmod_bwd", comm=pending.take(SMALL) if i == 0 else None)
        dyb = dyb.reshape(t, d)
        d_bias_out = jnp.sum(g1 * sdx, axis=(0, 1))
        if i % 2 == 0:
            qkv, mix = extra
            small["b_o", j] = d_bias_out
            reduce_scatter("o", j, n_attn,
                           _mm_tn(mix.reshape(t, d), dyb, name="dw_sq", comm=pending.take(SMALL)).reshape(N_CHIPS, -1, d))
            dmix = _mm(dyb, weights["o", j], nt=True, tn=d, name="mm_dmix", comm=pending.take(SMALL)).reshape(bsz, s, d)
            dqkv, d_bqkv, d_sink = _attn_bwd(qkv, dmix, sinks[j], name="attn_bwd", comm=pending.take(2 * LARGE))
            small["b_qkv", j], small["sinks", j] = d_bqkv[0], d_sink[:, 0]
            dqkv = dqkv.reshape(t, qkv_dim)
            dwq = _mm_tn(h1.reshape(t, d), dqkv, tn=qkv_dim, name="dw_qkv", comm=pending.take(SMALL))
            reduce_scatter("qkv", j, n_attn, jnp.transpose(dwq.reshape(d, N_CHIPS, -1), (1, 0, 2)))
            dh1 = _mm(dqkv, weights["qkv", j], nt=True, tn=d, name="mm_dh1a", comm=pending.take(SMALL))
        else:
            u, mix, conv_v = extra
            small["b_pw2", j] = d_bias_out
            reduce_scatter("pw2", j, n_conv,
                           _mm_tn(mix.reshape(t, d), dyb, name="dw_sq", comm=pending.take(SMALL)).reshape(N_CHIPS, -1, d))
            dz = _mm(dyb, weights["pw2", j], nt=True, out_dtype=F32, tn=d, name="mm_dz", comm=pending.take(SMALL)).reshape(bsz, s, d)
            dv, d_lng, d_lnb, d_bdw = _conv_bwd_ln(dz, conv_v, ln_g_f[j][None], ln_b_f[j][None], name="conv_bwd_ln")
            du, d_bpw1, d_wdw = _conv_bwd_taps(dv, u, w_dw_f[j], name="conv_bwd_taps", comm=pending.take(2 * LARGE))
            small["ln_g", j], small["ln_b", j], small["b_dw", j] = d_lng[0], d_lnb[0], d_bdw[0]
            small["b_pw1", j], small["w_dw", j] = d_bpw1[0], jnp.sum(d_wdw[:CONV_WIDTH], axis=1)
            du = du.reshape(t, 2 * d)
            reduce_scatter("pw1", j, n_conv,
                           _mm_tn(h1.reshape(t, d), du, col_shards=N_CHIPS, name="dw_pw1", comm=pending.take(MEDIUM)))
            dh1 = _mm(du, weights["pw1", j], nt=True, tn=d, name="mm_dh1c", comm=pending.take(MEDIUM))
        below = (saved[i - 1][7].reshape(bsz, s, d), mods[i - 1][5]) if i > 0 else None
        dx, p1, dsh1, *gate_grads = _normmod_bwd(xin, dh1.reshape(bsz, s, d), dx1, norm_mix[i][None], sc1, below,
                                                 name="normmod_bwd", comm=pending.take(SMALL, True) if i == 0 else None)
        small["norm_mix", i] = jnp.sum((1.0 + sc1) * p1, axis=(0, 1))
        small["norm_mlp", i] = jnp.sum((1.0 + sc2) * p2, axis=(0, 1))
        dmods[i] = jnp.concatenate([dsh1, norm_mix[i] * p1, dg1, dsh2, norm_mlp[i] * p2, dg2], axis=1)
        if i > 0:
            dyb, dg2, _ = gate_grads
    grad_x = dx

    small["final_norm"] = d_final[0]
    small_names = ([("norm_mix", i) for i in range(depth)] + [("norm_mlp", i) for i in range(depth)]
                   + [("b_o", j) for j in range(n_attn)]
                   + [(nm, j) for nm in ("b_pw1", "w_dw", "b_dw", "ln_g", "ln_b", "b_pw2") for j in range(n_conv)]
                   + ["final_norm"] + [(nm, j) for nm in ("b_qkv", "sinks") for j in range(n_attn)])
    small_list = [small[k] for k in small_names]
    small_pack = _pack(small_list, d)
    dmod_rows = jnp.stack(dmods).reshape(depth * bsz * N_MOD, d)
    n_small = small_pack.shape[0]
    pending.flush(LARGE)
    big_names = ["qkv", "o", "pw1", "pw2", "up", "down"]
    bufs = [totals[nm] for nm in big_names]
    copies = []
    for a, buf in enumerate(bufs):
        for layer in range(buf.shape[0]):
            half = lambda ref, h, layer=layer: ref.at[pl.ds(layer, 1), pl.ds(h, 1)]
            copies.append((lambda ins, outs, pos, a=a, half=half: half(ins[a], pos[2]),
                           lambda ins, outs, pos, a=a, half=half: half(outs[a], pos[2]),
                           lambda ins, outs, pos, a=a, half=half: half(outs[a], 1 - pos[2]), "c"))
    shared = {}
    share = _Plan("d2d", 0.0, bufs, [jax.ShapeDtypeStruct(b.shape, F32) for b in bufs], copies,
                  {a: a for a in range(len(bufs))}, lambda outs: shared.update(zip(big_names, outs)))
    gathered = _all_gather_small(jnp.concatenate([small_pack, _pack([dmod_rows], d)], axis=0), name="gather_small",
                                 comm=share)
    gathered = gathered.reshape(N_DEV, -1, d)
    dmod_all = gathered[:, n_small:n_small + depth * bsz * N_MOD].reshape(N_DEV, depth, bsz, N_MOD, d)
    small_sum, g_b_mod = _sum_devices(gathered[:, :n_small], dmod_all, name="sum_devices")
    small_tot = dict(zip(small_names, _unpack(small_sum, [a.shape for a in small_list])))
    stacked = lambda nm, count: jnp.stack([small_tot[nm, k] for k in range(count)])
    g_norm_mix, g_norm_mlp = stacked("norm_mix", depth), stacked("norm_mlp", depth)
    g_b_qkv, g_b_o, g_sinks = stacked("b_qkv", n_attn), stacked("b_o", n_attn), stacked("sinks", n_attn)
    g_final = small_tot["final_norm"]
    g_b_mod = g_b_mod.reshape(depth, N_MOD * d)
    shard_cols = lambda g: lax.dynamic_slice_in_dim(g, mq * (g.shape[-1] // N_CHIPS), g.shape[-1] // N_CHIPS, axis=g.ndim - 1)
    g_b_pw1, g_w_dw, g_b_dw, g_ln_g, g_ln_b, g_b_pw2 = [shard_cols(stacked(nm, n_conv))
                                                        for nm in ("b_pw1", "w_dw", "b_dw", "ln_g", "ln_b", "b_pw2")]

    dmod_cols = jnp.transpose(dmod_all, (1, 0, 2, 3, 4)).reshape(depth, N_DEV * bsz, N_MOD * d)
    dmod_cols = lax.dynamic_slice_in_dim(dmod_cols, mq * n_mod, n_mod, axis=2)
    g_w_mod = _mod_bwd(c_all, dmod_cols, name="mod_bwd")

    def adam(w, g, m, v, name, emit_grad=False):
        two_d = lambda a: a.reshape(-1, a.shape[-1])
        return [r.reshape(w.shape) for r in _adamw(two_d(w), two_d(g), two_d(m), two_d(v), name=name, emit_grad=emit_grad)]

    results = {"w_mod": (g_w_mod,) + tuple(adam(w_mod, g_w_mod, m_w_mod, v_w_mod, "adamw"))}

    g_w_qkv, g_w_o, g_w_pw1, g_w_pw2, g_w_up, g_w_down = [
        shared[nm].reshape(shared[nm].shape[0], -1, shared[nm].shape[3]) for nm in big_names]
    for nm, w, g, m, v in (("w_qkv", w_qkv, g_w_qkv, m_w_qkv, v_w_qkv),
                           ("w_o", w_o, g_w_o, m_w_o, v_w_o), ("w_pw1", w_pw1, g_w_pw1, m_w_pw1, v_w_pw1),
                           ("w_pw2", w_pw2, g_w_pw2, m_w_pw2, v_w_pw2), ("w_up", w_up, g_w_up, m_w_up, v_w_up),
                           ("w_down", w_down, g_w_down, m_w_down, v_w_down)):
        delta, new_m, new_v, g_out = adam(w, g, m, v, "adamw", emit_grad=True)
        results[nm] = (g_out, delta, new_m, new_v)
    small_w = dict(b_mod=(b_mod, g_b_mod, m_b_mod, v_b_mod), norm_mix=(norm_mix, g_norm_mix, m_norm_mix, v_norm_mix),
                   norm_mlp=(norm_mlp, g_norm_mlp, m_norm_mlp, v_norm_mlp), b_qkv=(b_qkv, g_b_qkv, m_b_qkv, v_b_qkv),
                   b_o=(b_o, g_b_o, m_b_o, v_b_o), sinks=(sinks, g_sinks, m_sinks, v_sinks),
                   b_pw1=(b_pw1, g_b_pw1, m_b_pw1, v_b_pw1), w_dw=(w_dw, g_w_dw, m_w_dw, v_w_dw),
                   b_dw=(b_dw, g_b_dw, m_b_dw, v_b_dw), conv_ln_g=(conv_ln_g, g_ln_g, m_conv_ln_g, v_conv_ln_g),
                   conv_ln_b=(conv_ln_b, g_ln_b, m_conv_ln_b, v_conv_ln_b), b_pw2=(b_pw2, g_b_pw2, m_b_pw2, v_b_pw2),
                   final_norm=(final_norm, g_final, m_final_norm, v_final_norm))
    names = list(small_w)
    deltas, new_ms, new_vs = _adamw_packed(*[[small_w[nm][k] for nm in names] for k in range(4)], d, name="adamw_small")
    for k, nm in enumerate(names):
        results[nm] = (small_w[nm][1], deltas[k], new_ms[k], new_vs[k])

    weight_order = ["w_mod", "b_mod", "norm_mix", "norm_mlp", "w_qkv", "b_qkv", "w_o", "b_o", "sinks", "w_pw1", "b_pw1",
                    "w_dw", "b_dw", "conv_ln_g", "conv_ln_b", "w_pw2", "b_pw2", "w_up", "w_down", "final_norm"]
    return (loss, grad_x, *[results[nm][0] for nm in weight_order], *[results[nm][1] for nm in weight_order],
            *[results[nm][2] for nm in weight_order], *[results[nm][3] for nm in weight_order])
```

```python
import functools

import jax
import jax.numpy as jnp
from jax import lax
from jax.experimental import pallas as pl
from jax.experimental.pallas import tpu as pltpu

F32, BF16 = jnp.float32, jnp.bfloat16
MESH = pl.DeviceIdType.MESH
N_CHIPS = 4
N_DEV = 8
LANES = 128
SUBLANES = 8
VMEM_LIMIT = 48 * 1024 * 1024

NORM_EPS = 1e-6
HEAD_DIM = 64
N_KV_HEADS = 2
WINDOW = 128
CONV_WIDTH = 31
CONV_HALO = 32
CONV_ROWS = 32
N_MOD = 6

ADAM_LR, ADAM_B1, ADAM_B2, ADAM_EPS, ADAM_WD, ADAM_STEP = 0.001, 0.9, 0.999, 1e-08, 0.01, 10

HBM_SPEC = pl.BlockSpec(memory_space=pltpu.HBM)
VMEM_SPEC = pl.BlockSpec(memory_space=pltpu.VMEM)
SMEM_SPEC = pl.BlockSpec(memory_space=pltpu.SMEM)


def _params(*sem):
    return pltpu.CompilerParams(dimension_semantics=sem or None, vmem_limit_bytes=VMEM_LIMIT)


def _row_tile(rows, width_bytes, target=2 << 20):
    t = rows
    while t % 2 == 0 and t > SUBLANES and t * width_bytes > target:
        t //= 2
    return t


CHIP_KINDS = ("x", "y", "xy")


def _position():
    x, y, c = lax.axis_index("x"), lax.axis_index("y"), lax.axis_index("c")
    return x, y, c, 2 * x + y


def _peer(kind, x, y, c):
    return {"c": (x, y, 1 - c), "x": (1 - x, y, c), "y": (x, 1 - y, c), "xy": (1 - x, 1 - y, c)}[kind]


def _peer_shard(kind, x, y):
    px, py, _ = _peer(kind, x, y, 0)
    return 2 * px + py


class _Plan:
    def __init__(self, link, cost, operands, out_shapes, copies, aliases=None, then=None):
        self.link, self.cost = link, cost
        self.operands, self.out_shapes, self.copies = list(operands), list(out_shapes), list(copies)
        self.aliases, self.then = dict(aliases or {}), then


def _merge(plans):
    operands, out_shapes, copies, aliases, thens = [], [], [], {}, []
    for p in plans:
        i0, o0 = len(operands), len(out_shapes)
        i1, o1 = i0 + len(p.operands), o0 + len(p.out_shapes)

        def shifted(f, i0=i0, i1=i1, o0=o0, o1=o1):
            return lambda ins, outs, pos: f(ins[i0:i1], outs[o0:o1], pos)

        copies += [(shifted(src), shifted(dst), shifted(land), kind) for src, dst, land, kind in p.copies]
        aliases.update({i0 + k: o0 + v for k, v in p.aliases.items()})
        operands += p.operands
        out_shapes += p.out_shapes
        thens.append((p.then, o0, o1))

    def then(outs):
        for f, o0, o1 in thens:
            if f is not None:
                f(outs[o0:o1])

    return _Plan("mixed", sum(p.cost for p in plans), operands, out_shapes, copies, aliases, then)


def _call(body, *, name, out_shape, operands, grid=(), in_specs=(), out_specs=(), scratch_shapes=(), sem=(), comm=None):
    single = not isinstance(out_shape, (tuple, list))
    out_shape = [out_shape] if single else list(out_shape)
    out_specs = [out_specs] if single else list(out_specs)
    if comm is None:
        res = pl.pallas_call(body, out_shape=out_shape, grid=grid, in_specs=list(in_specs), out_specs=out_specs,
                             scratch_shapes=list(scratch_shapes), name=name, compiler_params=_params(*sem))(*operands)
        return res[0] if single else res
    n_in, n_out, n_scr = len(operands), len(out_shape), len(scratch_shapes)
    c_in, c_out, n_cp = len(comm.operands), len(comm.out_shapes), len(comm.copies)

    def wrapped(*refs):
        ins, refs = refs[:n_in], refs[n_in:]
        cins, refs = refs[:c_in], refs[c_in:]
        outs, refs = refs[:n_out], refs[n_out:]
        couts, refs = refs[:c_out], refs[c_out:]
        scr, (send_sems, recv_sems) = refs[:n_scr], refs[n_scr:]

        def descriptors():
            pos = _position()
            sends, lands = [], []
            for k, (src, dst, landing, kind) in enumerate(comm.copies):
                common = dict(send_sem=send_sems.at[k], recv_sem=recv_sems.at[k],
                              device_id=_peer(kind, *pos[:3]), device_id_type=MESH)
                sends.append(pltpu.make_async_remote_copy(src_ref=src(cins, couts, pos), dst_ref=dst(cins, couts, pos), **common))
                lands.append(pltpu.make_async_remote_copy(src_ref=src(cins, couts, pos), dst_ref=landing(cins, couts, pos), **common))
            return sends, lands

        def start():
            for cp in descriptors()[0]:
                cp.start()

        def finish():
            sends, lands = descriptors()
            for cp in lands:
                cp.wait_recv()
            for cp in sends:
                cp.wait_send()

        if not grid:
            start()
            body(*ins, *outs, *scr)
            finish()
        else:
            ids = [pl.program_id(ax) for ax in range(len(grid))]
            first, last = ids[0] == 0, ids[0] == grid[0] - 1
            for ax in range(1, len(grid)):
                first, last = first & (ids[ax] == 0), last & (ids[ax] == grid[ax] - 1)
            pl.when(first)(start)
            body(*ins, *outs, *scr)
            pl.when(last)(finish)

    res = pl.pallas_call(wrapped, out_shape=out_shape + comm.out_shapes, grid=grid,
                         in_specs=list(in_specs) + [HBM_SPEC] * c_in, out_specs=out_specs + [HBM_SPEC] * c_out,
                         scratch_shapes=list(scratch_shapes) + [pltpu.SemaphoreType.DMA((n_cp,)), pltpu.SemaphoreType.DMA((n_cp,))],
                         input_output_aliases={n_in + k: n_out + v for k, v in comm.aliases.items()},
                         name=name, compiler_params=_params(*["arbitrary"] * len(grid)))(*operands, *comm.operands)
    if comm.then is not None:
        comm.then(res[n_out:])
    return res[0] if single else res[:n_out]


def _exchange(plan, *, name):
    _call(lambda: None, name=name, out_shape=[], operands=[], comm=plan)


def _mm(a, b, *, name, nt=False, tm=1024, tn=512, epi="plain", out_dtype=BF16,
        bias=None, act=None, resid=None, gate=None, seq=None, norm=None, norm_bwd=None, comm=None):
    m, k = a.shape
    n = b.shape[0] if nt else b.shape[1]
    tm, tn = min(tm, m, seq or m), min(tn, n)
    assert m % tm == 0 and n % tn == 0
    dims = (((1,), (1,)), ((), ())) if nt else (((1,), (0,)), ((), ()))
    tile = pl.BlockSpec((tm, tn), lambda j, i: (i, j))
    operands = [a, b]
    in_specs = [pl.BlockSpec((tm, k), lambda j, i: (i, 0)),
                pl.BlockSpec((tn, k), lambda j, i: (j, 0)) if nt else pl.BlockSpec((k, tn), lambda j, i: (0, j))]
    if bias is not None:
        operands.append(bias.reshape(1, n))
        in_specs.append(pl.BlockSpec((1, tn), lambda j, i: (0, j)))
    if epi == "dact":
        operands.append(act)
        in_specs.append(tile)
    if epi == "resid":
        assert seq % tm == 0
        per_ex = pl.BlockSpec((None, 1, tn), lambda j, i: (i * tm // seq, 0, j))
        operands += [resid, gate]
        in_specs += [tile, per_ex]
        out_shape = (jax.ShapeDtypeStruct((m, n), BF16), jax.ShapeDtypeStruct((m, n), F32))
        out_specs = (tile, tile)
        if norm is not None:
            assert tn == n
            operands += list(norm)
            in_specs += [pl.BlockSpec((1, tn), lambda j, i: (0, j)), per_ex, per_ex]
            out_shape += (jax.ShapeDtypeStruct((m, n), BF16),)
            out_specs += (tile,)
    elif epi == "relu2":
        out_shape = (jax.ShapeDtypeStruct((m, n), BF16), jax.ShapeDtypeStruct((m, n), BF16))
        out_specs = (tile, tile)
    elif epi == "normbwd":
        assert tn == n and seq % tm == 0
        x_in, dres_in, gamma_in, sc_in, producer = norm_bwd
        per_ex = pl.BlockSpec((None, 1, tn), lambda j, i: (i * tm // seq, 0, j))
        vec = jax.ShapeDtypeStruct((m // seq, 1, n), F32)
        operands += [x_in, dres_in, gamma_in, sc_in, *(producer or ())]
        in_specs += [tile, tile, pl.BlockSpec((1, tn), lambda j, i: (0, j)), per_ex] + ([tile, per_ex] if producer else [])
        out_shape = (jax.ShapeDtypeStruct((m, n), F32), vec, vec)
        out_specs = (tile, per_ex, per_ex)
        if producer:
            out_shape += (jax.ShapeDtypeStruct((m, n), BF16), vec, vec)
            out_specs += (tile, per_ex, per_ex)
    else:
        out_shape = jax.ShapeDtypeStruct((m, n), out_dtype)
        out_specs = tile

    def body(*refs):
        it = iter(refs)
        a_ref, b_ref = next(it), next(it)
        acc = lax.dot_general(a_ref[...], b_ref[...], dims, preferred_element_type=F32)
        if bias is not None:
            acc = acc + next(it)[...]
        if epi == "plain":
            next(it)[...] = acc.astype(out_dtype)
        elif epi == "relu2":
            r = jnp.maximum(acc, 0.0)
            next(it)[...] = (r * r).astype(BF16)
            next(it)[...] = (2.0 * r).astype(BF16)
        elif epi == "dact":
            slope_ref = next(it)
            next(it)[...] = (acc * slope_ref[...].astype(F32)).astype(out_dtype)
        elif epi == "normbwd":
            x_ref, dres_ref, g_ref, sc_ref = next(it), next(it), next(it), next(it)
            prod_refs = [next(it), next(it)] if norm_bwd[4] else []
            dx_ref, p_ref, dsh_ref = next(it), next(it), next(it)
            first = (pl.program_id(1) * tm) % seq == 0
            xf = x_ref[...]
            r = lax.rsqrt(jnp.mean(xf * xf, axis=-1, keepdims=True) + NORM_EPS)
            xhat = xf * r
            dxhat = acc * (g_ref[...] * (1.0 + sc_ref[...]))
            dxv = dres_ref[...] + r * (dxhat - xhat * jnp.mean(dxhat * xhat, axis=-1, keepdims=True))
            dx_ref[...] = dxv

            @pl.when(first)
            def _():
                p_ref[...] = jnp.zeros_like(p_ref)
                dsh_ref[...] = jnp.zeros_like(dsh_ref)

            p_ref[...] += jnp.sum(acc * xhat, axis=0, keepdims=True)
            dsh_ref[...] += jnp.sum(acc, axis=0, keepdims=True)
            if prod_refs:
                _gate_grads(dxv, prod_refs[0], prod_refs[1], next(it), next(it), next(it), first=first)
        else:
            resid_ref, gate_ref = next(it), next(it)
            norm_refs = [next(it) for _ in (norm or ())]
            y_ref, x_ref = next(it), next(it)
            y_ref[...] = acc.astype(BF16)
            x_new = resid_ref[...] + gate_ref[...] * acc
            x_ref[...] = x_new
            if norm is not None:
                g_ref, sc_ref, sh_ref = norm_refs
                r = lax.rsqrt(jnp.mean(x_new * x_new, axis=-1, keepdims=True) + NORM_EPS)
                next(it)[...] = (x_new * r * g_ref[...] * (1.0 + sc_ref[...]) + sh_ref[...]).astype(BF16)

    sem = ("arbitrary", "arbitrary") if epi == "normbwd" else ("parallel", "parallel")
    return _call(body, name=name, out_shape=out_shape, operands=operands, grid=(n // tn, m // tm), in_specs=in_specs,
                 out_specs=out_specs, sem=sem, comm=comm)


def _mm_tn(a, b, *, name, tm=1024, tn=1024, tk=2048, col_shards=None, comm=None):
    t, m = a.shape
    n = b.shape[1]
    tm, tk = min(tm, m), min(tk, t)
    if col_shards is None:
        tn = min(tn, n)
        out_shape = jax.ShapeDtypeStruct((m, n), F32)
        out_spec = pl.BlockSpec((tm, tn), lambda i, j, k: (i, j))
    else:
        per = n // col_shards
        tn = min(tn, per)
        assert per % tn == 0
        out_shape = jax.ShapeDtypeStruct((col_shards, m, per), F32)
        out_spec = pl.BlockSpec((None, tm, tn), lambda i, j, k: (j // (per // tn), i, j % (per // tn)))
    assert m % tm == 0 and n % tn == 0 and t % tk == 0

    def body(a_ref, b_ref, o_ref):
        @pl.when(pl.program_id(2) == 0)
        def _():
            o_ref[...] = jnp.zeros_like(o_ref)

        o_ref[...] += lax.dot_general(a_ref[...], b_ref[...], (((0,), (0,)), ((), ())),
                                      preferred_element_type=F32)

    return _call(body, name=name, out_shape=out_shape, operands=[a, b], grid=(m // tm, n // tn, t // tk),
                 in_specs=[pl.BlockSpec((tk, tm), lambda i, j, k: (k, i)), pl.BlockSpec((tk, tn), lambda i, j, k: (k, j))],
                 out_specs=out_spec, sem=("parallel", "parallel", "arbitrary"), comm=comm)


def _normmod(x, gamma, sc, sh, *, name):
    bsz, s, d = x.shape
    ts = _row_tile(s, d * 4, 2 << 20)

    def body(x_ref, g_ref, sc_ref, sh_ref, o_ref):
        xf = x_ref[...]
        r = lax.rsqrt(jnp.mean(xf * xf, axis=-1, keepdims=True) + NORM_EPS)
        o_ref[...] = (xf * r * g_ref[...] * (1.0 + sc_ref[...]) + sh_ref[...]).astype(BF16)

    row = pl.BlockSpec((None, ts, d), lambda b, i: (b, i, 0))
    per_ex = pl.BlockSpec((None, 1, d), lambda b, i: (b, 0, 0))
    return pl.pallas_call(body, out_shape=jax.ShapeDtypeStruct(x.shape, BF16), grid=(bsz, s // ts),
                          in_specs=[row, pl.BlockSpec((1, d), lambda b, i: (0, 0)), per_ex, per_ex],
                          out_specs=row, name=name, compiler_params=_params("parallel", "parallel"))(x, gamma, sc, sh)


def _gate_grads(dxv, y_ref, gate_ref, dy_ref, dg_ref, sdx_ref, first=None):
    dy_ref[...] = (gate_ref[...] * dxv).astype(BF16)

    @pl.when(pl.program_id(1) == 0 if first is None else first)
    def _():
        dg_ref[...] = jnp.zeros_like(dg_ref)
        sdx_ref[...] = jnp.zeros_like(sdx_ref)

    dg_ref[...] += jnp.sum(dxv * y_ref[...].astype(F32), axis=0, keepdims=True)
    sdx_ref[...] += jnp.sum(dxv, axis=0, keepdims=True)


def _normmod_bwd(x, dh, dres, gamma, sc, producer=None, *, name, comm=None):
    bsz, s, d = x.shape
    ts = _row_tile(s, d * 4, 2 << 20)

    def body(x_ref, dh_ref, dres_ref, g_ref, sc_ref, *rest):
        dx_ref, p_ref, dsh_ref = rest[-6:-3] if producer else rest
        xf = x_ref[...]
        r = lax.rsqrt(jnp.mean(xf * xf, axis=-1, keepdims=True) + NORM_EPS)
        xhat = xf * r
        dh_v = dh_ref[...].astype(F32)
        dxhat = dh_v * (g_ref[...] * (1.0 + sc_ref[...]))
        dxv = dres_ref[...] + r * (dxhat - xhat * jnp.mean(dxhat * xhat, axis=-1, keepdims=True))
        dx_ref[...] = dxv

        @pl.when(pl.program_id(1) == 0)
        def _():
            p_ref[...] = jnp.zeros_like(p_ref)
            dsh_ref[...] = jnp.zeros_like(dsh_ref)

        p_ref[...] += jnp.sum(dh_v * xhat, axis=0, keepdims=True)
        dsh_ref[...] += jnp.sum(dh_v, axis=0, keepdims=True)
        if producer:
            _gate_grads(dxv, rest[0], rest[1], *rest[-3:])

    row = pl.BlockSpec((None, ts, d), lambda b, i: (b, i, 0))
    per_ex = pl.BlockSpec((None, 1, d), lambda b, i: (b, 0, 0))
    vec = jax.ShapeDtypeStruct((bsz, 1, d), F32)
    gate_in, gate_out, gate_specs = (), (), ()
    if producer:
        gate_in, gate_out, gate_specs = (row, per_ex), (jax.ShapeDtypeStruct(x.shape, BF16), vec, vec), (row, per_ex, per_ex)
    return _call(body, name=name, out_shape=(jax.ShapeDtypeStruct(x.shape, F32), vec, vec) + gate_out,
                 operands=[x, dh, dres, gamma, sc, *(producer or ())], grid=(bsz, s // ts),
                 in_specs=[row, row, row, pl.BlockSpec((1, d), lambda b, i: (0, 0)), per_ex, *gate_in],
                 out_specs=(row, per_ex, per_ex) + gate_specs, sem=("parallel", "arbitrary"), comm=comm)


def _loss_head(x, target, gamma, producer, *, name):
    bsz, s, d = x.shape
    ts = _row_tile(s, d * 4, 2 << 20)

    def body(x_ref, t_ref, g_ref, y_ref, gate_ref, dx_ref, loss_ref, dg_ref, *gate_refs):
        xf = x_ref[...]
        r = lax.rsqrt(jnp.mean(xf * xf, axis=-1, keepdims=True) + NORM_EPS)
        xhat = xf * r
        err = xhat * g_ref[...] - t_ref[...]
        dy = err * (1.0 / d)
        dxhat = dy * g_ref[...]
        dxv = r * (dxhat - xhat * jnp.mean(dxhat * xhat, axis=-1, keepdims=True))
        dx_ref[...] = dxv

        @pl.when((pl.program_id(0) == 0) & (pl.program_id(1) == 0))
        def _():
            loss_ref[...] = jnp.zeros_like(loss_ref)
            dg_ref[...] = jnp.zeros_like(dg_ref)

        loss_ref[...] += jnp.sum(err * err, axis=0, keepdims=True)
        dg_ref[...] += jnp.sum(dy * xhat, axis=0, keepdims=True)
        _gate_grads(dxv, y_ref, gate_ref, *gate_refs)

    row = pl.BlockSpec((None, ts, d), lambda b, i: (b, i, 0))
    one = pl.BlockSpec((1, d), lambda b, i: (0, 0))
    per_ex = pl.BlockSpec((None, 1, d), lambda b, i: (b, 0, 0))
    vec, ex_vec = jax.ShapeDtypeStruct((1, d), F32), jax.ShapeDtypeStruct((bsz, 1, d), F32)
    return pl.pallas_call(body, out_shape=(jax.ShapeDtypeStruct(x.shape, F32), vec, vec,
                                           jax.ShapeDtypeStruct(x.shape, BF16), ex_vec, ex_vec), grid=(bsz, s // ts),
                          in_specs=[row, row, one, row, per_ex], out_specs=(row, one, one, row, per_ex, per_ex), name=name,
                          compiler_params=_params("arbitrary", "arbitrary"))(x, target, gamma, *producer)


def _alibi_slope(h, n_heads):
    return 2.0 ** (-8.0 * (h + 1) / n_heads)


def _attn_masks(first_block):
    qi = lax.broadcasted_iota(jnp.int32, (WINDOW, 2 * WINDOW), 0)
    ki = lax.broadcasted_iota(jnp.int32, (WINDOW, 2 * WINDOW), 1)
    dist = qi + WINDOW - ki
    first_key = jnp.where(first_block, WINDOW, 0)
    valid = (dist >= 0) & (dist < WINDOW) & (ki >= first_key)
    return dist.astype(F32), valid


def _dup_halves(span, kv, left):
    f = span.astype(F32)
    rolled = pltpu.roll(f, HEAD_DIM, axis=1)
    out = jnp.where(left, f, rolled) if kv == 0 else jnp.where(left, rolled, f)
    return out.astype(BF16)


def _attn_probs(s, h, n_heads, distf, valid, sink):
    s = s * (HEAD_DIM ** -0.5) - _alibi_slope(h, n_heads) * distf
    s = jnp.where(valid, s, -1e30)
    m = jnp.maximum(jnp.max(s, axis=-1, keepdims=True), sink)
    e = jnp.exp(s - m)
    e_sink = jnp.exp(sink - m)
    inv = 1.0 / (jnp.sum(e, axis=-1, keepdims=True) + e_sink)
    return e * inv, e_sink * inv


def _attn_specs(d, n_blocks, clamp):
    kcol = d // LANES
    cur = (lambda i: jnp.minimum(i, n_blocks - 1)) if clamp else (lambda i: i)
    prev = lambda i: jnp.maximum(cur(i) - 1, 0)
    kv = lambda col, blk: pl.BlockSpec((None, WINDOW, LANES), lambda b, i: (b, blk(i), col))
    return [pl.BlockSpec((None, WINDOW, d), lambda b, i: (b, cur(i), 0)),
            kv(kcol, prev), kv(kcol, cur), kv(kcol + 1, prev), kv(kcol + 1, cur)]


def _attn_fwd(qkv, sinks, *, name, comm=None):
    bsz, s, qkv_dim = qkv.shape
    d = qkv_dim - 2 * N_KV_HEADS * HEAD_DIM
    n_heads = d // HEAD_DIM
    group = n_heads // N_KV_HEADS
    pairs = group // 2
    n_blocks = s // WINDOW

    def body(q_ref, kp_ref, kc_ref, vp_ref, vc_ref, sink_ref, o_ref):
        left = lax.broadcasted_iota(jnp.int32, (1, LANES), 1) < HEAD_DIM
        distf, valid = _attn_masks(pl.program_id(1) == 0)
        kspan = jnp.concatenate([kp_ref[...], kc_ref[...]], axis=0)
        vspan = jnp.concatenate([vp_ref[...], vc_ref[...]], axis=0)
        for kv in range(N_KV_HEADS):
            kdup, vdup = _dup_halves(kspan, kv, left), _dup_halves(vspan, kv, left)
            res = []
            for par in range(2):
                keep = left if par == 0 else jnp.logical_not(left)
                cols = [pl.ds((kv * pairs + p) * LANES, LANES) for p in range(pairs)]
                lhs = jnp.concatenate([jnp.where(keep, q_ref[:, cl], jnp.zeros((), BF16)) for cl in cols], axis=0)
                sc = lax.dot_general(lhs, kdup, (((1,), (1,)), ((), ())), preferred_element_type=F32)
                probs = []
                for p in range(pairs):
                    h = kv * group + 2 * p + par
                    pr, _ = _attn_probs(sc[p * WINDOW:(p + 1) * WINDOW], h, n_heads, distf, valid, sink_ref[h])
                    probs.append(pr.astype(BF16))
                res.append(jnp.dot(jnp.concatenate(probs, axis=0), vdup, preferred_element_type=F32))
            for p in range(pairs):
                rows = slice(p * WINDOW, (p + 1) * WINDOW)
                o_ref[:, pl.ds((kv * pairs + p) * LANES, LANES)] = jnp.where(left, res[0][rows], res[1][rows]).astype(BF16)

    return _call(body, name=name, out_shape=jax.ShapeDtypeStruct((bsz, s, d), BF16), operands=[qkv] * 5 + [sinks],
                 grid=(bsz, n_blocks), in_specs=_attn_specs(d, n_blocks, False) + [SMEM_SPEC],
                 out_specs=pl.BlockSpec((None, WINDOW, d), lambda b, i: (b, i, 0)), sem=("parallel", "parallel"), comm=comm)


def _attn_bwd(qkv, do, sinks, *, name, comm=None):
    bsz, s, qkv_dim = qkv.shape
    d = qkv_dim - 2 * N_KV_HEADS * HEAD_DIM
    n_heads = d // HEAD_DIM
    group = n_heads // N_KV_HEADS
    pairs = group // 2
    n_blocks = s // WINDOW
    tn_dims = (((0,), (0,)), ((), ()))

    def body(q_ref, kp_ref, kc_ref, vp_ref, vc_ref, do_ref, sink_ref, dqkv_ref, colsum_ref, dsink_ref,
             dq_prev, dk_carry, dv_carry):
        b, i = pl.program_id(0), pl.program_id(1)
        left = lax.broadcasted_iota(jnp.int32, (1, LANES), 1) < HEAD_DIM

        @pl.when((b == 0) & (i == 0))
        def _():
            colsum_ref[...] = jnp.zeros_like(colsum_ref)
            dsink_ref[...] = jnp.zeros_like(dsink_ref)

        @pl.when(i == 0)
        def _():
            dqkv_ref[...] = jnp.zeros_like(dqkv_ref)
            dk_carry[...] = jnp.zeros_like(dk_carry)
            dv_carry[...] = jnp.zeros_like(dv_carry)

        @pl.when(i > 0)
        def _():
            dq_v = dq_prev[...]
            dqkv_ref[:, pl.ds(0, d)] = dq_v.astype(BF16)
            colsum_ref[:, pl.ds(0, d)] += jnp.sum(dq_v, axis=0, keepdims=True)

        @pl.when(i < n_blocks)
        def _():
            distf, valid = _attn_masks(i == 0)
            kspan = jnp.concatenate([kp_ref[...], kc_ref[...]], axis=0)
            vspan = jnp.concatenate([vp_ref[...], vc_ref[...]], axis=0)
            dk_blk, dv_blk = [], []
            for kv in range(N_KV_HEADS):
                kdup, vdup = _dup_halves(kspan, kv, left), _dup_halves(vspan, kv, left)
                dq_res, dk_sum, dv_sum = [], None, None
                for par in range(2):
                    keep = left if par == 0 else jnp.logical_not(left)
                    cols = [pl.ds((kv * pairs + p) * LANES, LANES) for p in range(pairs)]
                    zero = jnp.zeros((), BF16)
                    lhs = jnp.concatenate([jnp.where(keep, q_ref[:, cl], zero) for cl in cols], axis=0)
                    dol = jnp.concatenate([jnp.where(keep, do_ref[:, cl], zero) for cl in cols], axis=0)
                    sc = lax.dot_general(lhs, kdup, (((1,), (1,)), ((), ())), preferred_element_type=F32)
                    dp = lax.dot_general(dol, vdup, (((1,), (1,)), ((), ())), preferred_element_type=F32)
                    probs, dscores = [], []
                    for p in range(pairs):
                        h = kv * group + 2 * p + par
                        rows = slice(p * WINDOW, (p + 1) * WINDOW)
                        pr, p_sink = _attn_probs(sc[rows], h, n_heads, distf, valid, sink_ref[h])
                        delta = jnp.sum(pr * dp[rows], axis=-1, keepdims=True)
                        dscores.append((pr * (dp[rows] - delta) * (HEAD_DIM ** -0.5)).astype(BF16))
                        probs.append(pr.astype(BF16))
                        dsink_ref[pl.ds(h, 1), :] += jnp.zeros((1, LANES), F32) - jnp.sum(p_sink * delta)
                    ds_all = jnp.concatenate(dscores, axis=0)
                    p_all = jnp.concatenate(probs, axis=0)
                    dq_res.append(jnp.dot(ds_all, kdup, preferred_element_type=F32))
                    dk_par = lax.dot_general(ds_all, lhs, tn_dims, preferred_element_type=F32)
                    dv_par = lax.dot_general(p_all, dol, tn_dims, preferred_element_type=F32)
                    dk_sum = dk_par if dk_sum is None else dk_sum + dk_par
                    dv_sum = dv_par if dv_sum is None else dv_sum + dv_par
                for p in range(pairs):
                    rows = slice(p * WINDOW, (p + 1) * WINDOW)
                    dq_prev[:, pl.ds((kv * pairs + p) * LANES, LANES)] = jnp.where(left, dq_res[0][rows], dq_res[1][rows])
                dk_blk.append(dk_sum + pltpu.roll(dk_sum, HEAD_DIM, axis=1))
                dv_blk.append(dv_sum + pltpu.roll(dv_sum, HEAD_DIM, axis=1))
            dk_span = jnp.where(left, dk_blk[0], dk_blk[1])
            dv_span = jnp.where(left, dv_blk[0], dv_blk[1])
            dk_done = dk_carry[...] + dk_span[:WINDOW]
            dv_done = dv_carry[...] + dv_span[:WINDOW]
            dk_carry[...] = dk_span[WINDOW:]
            dv_carry[...] = dv_span[WINDOW:]

            @pl.when(i > 0)
            def _():
                dqkv_ref[:, pl.ds(d, LANES)] = dk_done.astype(BF16)
                dqkv_ref[:, pl.ds(d + LANES, LANES)] = dv_done.astype(BF16)
                colsum_ref[:, pl.ds(d, LANES)] += jnp.sum(dk_done, axis=0, keepdims=True)
                colsum_ref[:, pl.ds(d + LANES, LANES)] += jnp.sum(dv_done, axis=0, keepdims=True)

        @pl.when(i == n_blocks)
        def _():
            dk_done, dv_done = dk_carry[...], dv_carry[...]
            dqkv_ref[:, pl.ds(d, LANES)] = dk_done.astype(BF16)
            dqkv_ref[:, pl.ds(d + LANES, LANES)] = dv_done.astype(BF16)
            colsum_ref[:, pl.ds(d, LANES)] += jnp.sum(dk_done, axis=0, keepdims=True)
            colsum_ref[:, pl.ds(d + LANES, LANES)] += jnp.sum(dv_done, axis=0, keepdims=True)

    do_spec = pl.BlockSpec((None, WINDOW, d), lambda b, i: (b, jnp.minimum(i, n_blocks - 1), 0))
    out_shape = (jax.ShapeDtypeStruct((bsz, s, qkv_dim), BF16), jax.ShapeDtypeStruct((1, qkv_dim), F32),
                 jax.ShapeDtypeStruct((n_heads, LANES), F32))
    out_specs = (pl.BlockSpec((None, WINDOW, qkv_dim), lambda b, i: (b, jnp.maximum(i - 1, 0), 0)),
                 pl.BlockSpec((1, qkv_dim), lambda b, i: (0, 0)),
                 pl.BlockSpec((n_heads, LANES), lambda b, i: (0, 0)))
    return _call(body, name=name, out_shape=out_shape, operands=[qkv] * 5 + [do, sinks], grid=(bsz, n_blocks + 1),
                 in_specs=_attn_specs(d, n_blocks, True) + [do_spec, SMEM_SPEC], out_specs=out_specs,
                 scratch_shapes=[pltpu.VMEM((WINDOW, d), F32), pltpu.VMEM((WINDOW, LANES), F32), pltpu.VMEM((WINDOW, LANES), F32)],
                 sem=("arbitrary", "arbitrary"), comm=comm)


def _conv_tile(s):
    return min(256, s)


def _halo_specs(ts, width, s):
    per = ts // CONV_HALO
    prev = pl.BlockSpec((None, CONV_HALO, width), lambda b, i: (b, jnp.maximum(i * per - 1, 0), 0))
    nxt = pl.BlockSpec((None, CONV_HALO, width), lambda b, i: (b, jnp.minimum((i + 1) * per, s // CONV_HALO - 1), 0))
    cur = pl.BlockSpec((None, ts, width), lambda b, i: (b, i, 0))
    return prev, cur, nxt


def _glu(u, d):
    return u[:, :d] * jax.nn.sigmoid(u[:, d:])


def _store_shifted(shifted, value):
    rows = value.shape[0]
    shifted[0] = value
    for b in range(1, SUBLANES):
        shifted[b] = pltpu.roll(value, rows - b, axis=0)


def _at(shifted, base, offset, n_rows, lanes):
    return shifted[offset % SUBLANES, pl.ds(base + (offset - offset % SUBLANES), n_rows), lanes]


def _taps(w_ref, shifted, out_ref, ts, d, offset):
    for r0 in range(0, ts, CONV_ROWS):
        for l0 in range(0, d, LANES):
            lanes = pl.ds(l0, LANES)
            acc = jnp.zeros((CONV_ROWS, LANES), F32)
            for j in range(CONV_WIDTH):
                acc = acc + w_ref[pl.ds(j, 1), lanes] * _at(shifted, r0, offset(j), CONV_ROWS, lanes)
            out_ref[pl.ds(r0, CONV_ROWS), lanes] = acc


def _conv_fwd(u, w_dw, b_dw, ln_g, ln_b, *, name, comm=None):
    bsz, s, d2 = u.shape
    d = d2 // 2
    ts = _conv_tile(s)

    def body(up_ref, uc_ref, w_ref, bdw_ref, g_ref, b_ref, z_ref, v_ref, gbuf):
        halo = jnp.where(pl.program_id(1) > 0, _glu(up_ref[...], d), 0.0)
        _store_shifted(gbuf, jnp.concatenate([halo, _glu(uc_ref[...], d)], axis=0))
        _taps(w_ref, gbuf, v_ref, ts, d, lambda j: CONV_HALO - (CONV_WIDTH - 1) + j)
        v = v_ref[...] + bdw_ref[...]
        v_ref[...] = v
        mu = jnp.mean(v, axis=-1, keepdims=True)
        cen = v - mu
        rstd = lax.rsqrt(jnp.mean(cen * cen, axis=-1, keepdims=True) + NORM_EPS)
        ln = cen * rstd * g_ref[...] + b_ref[...]
        z_ref[...] = (ln * jax.nn.sigmoid(ln)).astype(BF16)

    prev, cur, _ = _halo_specs(ts, d2, s)
    one = pl.BlockSpec((1, d), lambda b, i: (0, 0))
    row = pl.BlockSpec((None, ts, d), lambda b, i: (b, i, 0))
    return _call(body, name=name, out_shape=(jax.ShapeDtypeStruct((bsz, s, d), BF16), jax.ShapeDtypeStruct((bsz, s, d), F32)),
                 operands=[u, u, w_dw, b_dw, ln_g, ln_b], grid=(bsz, s // ts),
                 in_specs=[prev, cur, pl.BlockSpec((CONV_HALO, d), lambda b, i: (0, 0)), one, one, one],
                 out_specs=(row, row), scratch_shapes=[pltpu.VMEM((SUBLANES, ts + CONV_HALO, d), F32)],
                 sem=("parallel", "parallel"), comm=comm)


def _conv_bwd_ln(dz, v, ln_g, ln_b, *, name):
    bsz, s, d = v.shape
    ts = _row_tile(s, d * 4, 2 << 20)

    def body(dz_ref, v_ref, g_ref, b_ref, dv_ref, dg_ref, db_ref, dbdw_ref):
        v_v = v_ref[...]
        mu = jnp.mean(v_v, axis=-1, keepdims=True)
        cen = v_v - mu
        rstd = lax.rsqrt(jnp.mean(cen * cen, axis=-1, keepdims=True) + NORM_EPS)
        vhat = cen * rstd
        ln = vhat * g_ref[...] + b_ref[...]
        sig = jax.nn.sigmoid(ln)
        dln = dz_ref[...] * (sig * (1.0 + ln * (1.0 - sig)))
        dvhat = dln * g_ref[...]
        dv = rstd * (dvhat - jnp.mean(dvhat, axis=-1, keepdims=True)
                     - vhat * jnp.mean(dvhat * vhat, axis=-1, keepdims=True))
        dv_ref[...] = dv

        @pl.when((pl.program_id(0) == 0) & (pl.program_id(1) == 0))
        def _():
            dg_ref[...] = jnp.zeros_like(dg_ref)
            db_ref[...] = jnp.zeros_like(db_ref)
            dbdw_ref[...] = jnp.zeros_like(dbdw_ref)

        dg_ref[...] += jnp.sum(dln * vhat, axis=0, keepdims=True)
        db_ref[...] += jnp.sum(dln, axis=0, keepdims=True)
        dbdw_ref[...] += jnp.sum(dv, axis=0, keepdims=True)

    row = pl.BlockSpec((None, ts, d), lambda b, i: (b, i, 0))
    one = pl.BlockSpec((1, d), lambda b, i: (0, 0))
    vec = jax.ShapeDtypeStruct((1, d), F32)
    return pl.pallas_call(body, out_shape=(jax.ShapeDtypeStruct(v.shape, F32), vec, vec, vec), grid=(bsz, s // ts),
                          in_specs=[row, row, one, one], out_specs=(row, one, one, one), name=name,
                          compiler_params=_params("arbitrary", "arbitrary"))(dz, v, ln_g, ln_b)


def _conv_bwd_taps(dv, u, w_dw, *, name, comm=None):
    bsz, s, d = dv.shape
    ts = _conv_tile(s)
    n_tiles = s // ts

    def body(dvc_ref, dvn_ref, uc_ref, w_ref, du_ref, dbu_ref, dw_ref, dvbuf, dglu, glu):
        i = pl.program_id(1)

        @pl.when((pl.program_id(0) == 0) & (i == 0))
        def _():
            dbu_ref[...] = jnp.zeros_like(dbu_ref)
            dw_ref[...] = jnp.zeros_like(dw_ref)

        u_v = uc_ref[...]
        a, sig = u_v[:, :d], jax.nn.sigmoid(u_v[:, d:])
        glu[...] = a * sig
        ahead = jnp.where(i < n_tiles - 1, dvn_ref[...], 0.0)
        _store_shifted(dvbuf, jnp.concatenate([dvc_ref[...], ahead], axis=0))
        _taps(w_ref, dvbuf, dglu, ts, d, lambda j: CONV_WIDTH - 1 - j)
        for l0 in range(0, d, LANES):
            lanes = pl.ds(l0, LANES)
            for j in range(CONV_WIDTH):
                acc = jnp.zeros((SUBLANES, LANES), F32)
                for r0 in range(0, ts, CONV_ROWS):
                    prod = glu[pl.ds(r0, CONV_ROWS), lanes] * _at(dvbuf, r0, CONV_WIDTH - 1 - j, CONV_ROWS, lanes)
                    for k in range(0, CONV_ROWS, SUBLANES):
                        acc = acc + prod[k:k + SUBLANES]
                dw_ref[j, :, lanes] += acc
        dg_v = dglu[...]
        da = dg_v * sig
        dgate = dg_v * a * sig * (1.0 - sig)
        du_ref[:, pl.ds(0, d)] = da.astype(BF16)
        du_ref[:, pl.ds(d, d)] = dgate.astype(BF16)
        dbu_ref[:, pl.ds(0, d)] += jnp.sum(da, axis=0, keepdims=True)
        dbu_ref[:, pl.ds(d, d)] += jnp.sum(dgate, axis=0, keepdims=True)

    _, dv_cur, dv_next = _halo_specs(ts, d, s)
    _, u_cur, _ = _halo_specs(ts, 2 * d, s)
    out_shape = (jax.ShapeDtypeStruct((bsz, s, 2 * d), BF16), jax.ShapeDtypeStruct((1, 2 * d), F32),
                 jax.ShapeDtypeStruct((CONV_HALO, SUBLANES, d), F32))
    out_specs = (pl.BlockSpec((None, ts, 2 * d), lambda b, i: (b, i, 0)), pl.BlockSpec((1, 2 * d), lambda b, i: (0, 0)),
                 pl.BlockSpec((CONV_HALO, SUBLANES, d), lambda b, i: (0, 0, 0)))
    return _call(body, name=name, out_shape=out_shape, operands=[dv, dv, u, w_dw], grid=(bsz, n_tiles),
                 in_specs=[dv_cur, dv_next, u_cur, pl.BlockSpec((CONV_HALO, d), lambda b, i: (0, 0))],
                 out_specs=out_specs,
                 scratch_shapes=[pltpu.VMEM((SUBLANES, ts + CONV_HALO, d), F32), pltpu.VMEM((ts, d), F32),
                                 pltpu.VMEM((ts, d), F32)],
                 sem=("arbitrary", "arbitrary"), comm=comm)


def _mod_fwd(c_all, w_mod, b_mod, *, name):
    n_layers, d, n = w_mod.shape
    rows = c_all.shape[0]

    def body(c_ref, w_ref, b_ref, o_ref):
        cv = c_ref[...]
        cs = (cv * jax.nn.sigmoid(cv)).astype(BF16)
        o_ref[...] = jnp.dot(cs, w_ref[...].astype(BF16), preferred_element_type=F32) + b_ref[...]

    return pl.pallas_call(body, out_shape=jax.ShapeDtypeStruct((n_layers, rows, n), F32), grid=(n_layers,),
                          in_specs=[pl.BlockSpec((rows, d), lambda l: (0, 0)), pl.BlockSpec((None, d, n), lambda l: (l, 0, 0)),
                                    pl.BlockSpec((None, 1, n), lambda l: (l, 0, 0))],
                          out_specs=pl.BlockSpec((None, rows, n), lambda l: (l, 0, 0)), name=name,
                          compiler_params=_params("parallel"))(c_all, w_mod, b_mod)


def _mod_bwd(c_all, dmod, *, name):
    n_layers, rows, n = dmod.shape
    d = c_all.shape[1]

    def body(c_ref, g_ref, o_ref):
        cv = c_ref[...]
        cs = (cv * jax.nn.sigmoid(cv)).astype(BF16)
        o_ref[...] = lax.dot_general(cs, g_ref[...].astype(BF16), (((0,), (0,)), ((), ())), preferred_element_type=F32)

    return pl.pallas_call(body, out_shape=jax.ShapeDtypeStruct((n_layers, d, n), F32), grid=(n_layers,),
                          in_specs=[pl.BlockSpec((rows, d), lambda l: (0, 0)), pl.BlockSpec((None, rows, n), lambda l: (l, 0, 0))],
                          out_specs=pl.BlockSpec((None, d, n), lambda l: (l, 0, 0)), name=name,
                          compiler_params=_params("parallel"))(c_all, dmod)


def _add_half(g, recv, my_c, *, name):
    p, _, r, cdim = g.shape
    tr = _row_tile(r, cdim * 4)

    def body(c_ref, g_ref, r_ref, o_ref):
        o_ref[...] = (g_ref[...] + r_ref[...]).astype(BF16)

    grid_spec = pltpu.PrefetchScalarGridSpec(
        num_scalar_prefetch=1, grid=(p, r // tr),
        in_specs=[pl.BlockSpec((None, None, tr, cdim), lambda q, i, c_ref: (q, c_ref[0], i, 0)),
                  pl.BlockSpec((None, None, tr, cdim), lambda q, i, c_ref: (q, 0, i, 0))],
        out_specs=pl.BlockSpec((None, tr, cdim), lambda q, i, c_ref: (q, i, 0)))
    return pl.pallas_call(body, out_shape=jax.ShapeDtypeStruct((p, r, cdim), BF16), grid_spec=grid_spec, name=name,
                          compiler_params=_params("parallel", "parallel"))(my_c, g, recv)


def _add_pieces(chip, recv, my_qc, stacked, layer, n_layers, *, name):
    _, r, cdim = chip.shape
    tr = _row_tile(r, cdim * 4)

    def body(qc_ref, own_ref, r0_ref, r1_ref, r2_ref, *rest):
        f32 = lambda ref: ref[...].astype(F32)
        rest[-1][...] = ((f32(own_ref) + f32(r0_ref)) + f32(r1_ref)) + f32(r2_ref)

    piece = lambda k: pl.BlockSpec((None, tr, cdim), lambda i, qc_ref: (k, i, 0))
    in_specs = [pl.BlockSpec((None, tr, cdim), lambda i, qc_ref: (qc_ref[0], i, 0)), piece(0), piece(1), piece(2)]
    operands = [my_qc, chip, recv, recv, recv]
    aliases = {}
    if stacked is not None:
        in_specs.append(pl.BlockSpec(memory_space=pl.ANY))
        operands.append(stacked)
        aliases = {len(operands) - 1: 0}
    grid_spec = pltpu.PrefetchScalarGridSpec(
        num_scalar_prefetch=1, grid=(r // tr,), in_specs=in_specs,
        out_specs=pl.BlockSpec((None, None, tr, cdim), lambda i, qc_ref: (layer, qc_ref[1], i, 0)))
    return pl.pallas_call(body, out_shape=jax.ShapeDtypeStruct((n_layers, 2, r, cdim), F32), grid_spec=grid_spec,
                          input_output_aliases=aliases, name=name, compiler_params=_params("parallel"))(*operands)


def _sum_devices(parts, dmod, *, name):
    def body(p_ref, m_ref, o_ref, b_ref):
        acc = p_ref[0]
        for k in range(1, N_DEV):
            acc = acc + p_ref[k]
        o_ref[...] = acc
        tot = None
        for k in range(N_DEV):
            for e in range(dmod.shape[2]):
                tot = m_ref[k, :, e] if tot is None else tot + m_ref[k, :, e]
        b_ref[...] = tot

    out_shape = (jax.ShapeDtypeStruct(parts.shape[1:], F32),
                 jax.ShapeDtypeStruct((dmod.shape[1],) + dmod.shape[3:], F32))
    return pl.pallas_call(body, out_shape=out_shape, in_specs=[VMEM_SPEC, VMEM_SPEC], out_specs=(VMEM_SPEC, VMEM_SPEC),
                          name=name, compiler_params=_params())(parts, dmod)


def _adamw(w, g, m, v, *, name, comm=None):
    r, cdim = w.shape
    tr = _row_tile(r, cdim * 4, 2 << 20)

    def body(w_ref, g_ref, m_ref, v_ref, d_ref, nm_ref, nv_ref):
        gv = g_ref[...]
        nm = ADAM_B1 * m_ref[...] + (1.0 - ADAM_B1) * gv
        nv = ADAM_B2 * v_ref[...] + (1.0 - ADAM_B2) * (gv * gv)
        m_hat = nm / (1.0 - ADAM_B1 ** ADAM_STEP)
        v_hat = nv / (1.0 - ADAM_B2 ** ADAM_STEP)
        d_ref[...] = -ADAM_LR * (m_hat / (jnp.sqrt(v_hat) + ADAM_EPS) + ADAM_WD * w_ref[...])
        nm_ref[...] = nm
        nv_ref[...] = nv

    row = pl.BlockSpec((tr, cdim), lambda i: (i, 0))
    shape = jax.ShapeDtypeStruct((r, cdim), F32)
    return _call(body, name=name, out_shape=(shape, shape, shape), operands=[w, g, m, v], grid=(r // tr,),
                 in_specs=[row] * 4, out_specs=(row, row, row), sem=("parallel",), comm=comm)


def _all_gather_small(block, *, name, comm=None):
    m_per, n = block.shape

    def body(x_ref, out_ref, send_sems, recv_sems, local_sem):
        x, y, c, _ = _position()
        me, sibling = (x, y, c), (x, y, 1 - c)
        chips = [_peer(k, x, y, c)[:2] for k in CHIP_KINDS]

        def rows(px, py, pc):
            return out_ref.at[pl.ds((4 * px + 2 * py + pc) * m_per, m_per), :]

        def copy(k, blk, to, src=None):
            return pltpu.make_async_remote_copy(src_ref=rows(*blk) if src is None else src, dst_ref=rows(*blk),
                                                send_sem=send_sems.at[k], recv_sem=recv_sems.at[k],
                                                device_id=to, device_id_type=MESH)

        mine = pltpu.make_async_copy(x_ref, rows(*me), local_sem)
        mine.start()
        first = [copy(0, me, sibling, src=x_ref)]
        first += [copy(1 + j, me, (*chip, c), src=x_ref) for j, chip in enumerate(chips)]
        for cp in first:
            cp.start()
        passed = [copy(4 + j, (*chip, c), sibling) for j, chip in enumerate(chips)]
        for j, chip in enumerate(chips):
            copy(1 + j, (*chip, c), me).wait_recv()
            passed[j].start()
        copy(0, sibling, me).wait_recv()
        for j, chip in enumerate(chips):
            copy(4 + j, (*chip, 1 - c), me).wait_recv()
        for cp in first + passed:
            cp.wait_send()
        mine.wait()

    return _call(body, name=name, out_shape=jax.ShapeDtypeStruct((N_DEV * m_per, n), block.dtype), operands=[block],
                 in_specs=[VMEM_SPEC], out_specs=VMEM_SPEC,
                 scratch_shapes=[pltpu.SemaphoreType.DMA((7,)), pltpu.SemaphoreType.DMA((7,)), pltpu.SemaphoreType.DMA],
                 comm=comm)


def _pack(arrays, width):
    flat = jnp.concatenate([a.reshape(-1).astype(F32) for a in arrays])
    rows = -(-flat.shape[0] // width)
    rows = -(-rows // SUBLANES) * SUBLANES
    return jnp.pad(flat, (0, rows * width - flat.shape[0])).reshape(rows, width)


def _unpack(packed, shapes):
    flat, out, off = packed.reshape(-1), [], 0
    for shp in shapes:
        size = 1
        for dim in shp:
            size *= dim
        out.append(flat[off:off + size].reshape(shp))
        off += size
    return out


def _adamw_packed(ws, gs, ms, vs, width, *, name):
    shapes = [w.shape for w in ws]
    res = _adamw(_pack(ws, width), _pack(gs, width), _pack(ms, width), _pack(vs, width), name=name)
    return [_unpack(r, shapes) for r in res]


MB = float(1 << 20)


def _nbytes(shape, dtype):
    size = jnp.dtype(dtype).itemsize
    for dim in shape:
        size *= dim
    return size


def _gather_plans(shard, side_by_side, on_ready):
    r, cdim = shard.shape
    half = r // 2
    if side_by_side:
        assert cdim % LANES == 0
        buf = jax.ShapeDtypeStruct((r, N_CHIPS * cdim), shard.dtype)
        rows = lambda ref, q, h: ref.at[pl.ds(h * half, half), pl.ds(q * cdim, cdim)]
        whole = lambda ref, q: ref.at[:, pl.ds(q * cdim, cdim)]
        src, src_rows = shard, (lambda ref, h: ref.at[pl.ds(h * half, half)])
    else:
        buf = jax.ShapeDtypeStruct((N_CHIPS, r, cdim), shard.dtype)
        rows = lambda ref, q, h: ref.at[pl.ds(q, 1), pl.ds(h * half, half)]
        whole = lambda ref, q: ref.at[pl.ds(q, 1)]
        src, src_rows = shard[None], (lambda ref, h: ref.at[:, pl.ds(h * half, half)])
    there = lambda kind, pos: _peer_shard(kind, pos[0], pos[1])
    cost = 3 * _nbytes((half, cdim), shard.dtype) / MB

    def forward(outs):
        copies = [(lambda ins, outs, pos, kind=kind: rows(ins[0], there(kind, pos), pos[2]),
                   lambda ins, outs, pos, kind=kind: rows(outs[0], there(kind, pos), pos[2]),
                   lambda ins, outs, pos, kind=kind: rows(outs[0], there(kind, pos), 1 - pos[2]), "c") for kind in CHIP_KINDS]
        return _Plan("d2d", cost, [outs[0]], [jax.ShapeDtypeStruct(buf.shape, buf.dtype)], copies, {0: 0},
                     lambda done: on_ready(done[0]))

    copies = [(lambda ins, outs, pos: src_rows(ins[0], pos[2]),
               lambda ins, outs, pos: rows(outs[0], pos[3], pos[2]),
               lambda ins, outs, pos, kind=kind: rows(outs[0], there(kind, pos), pos[2]), kind) for kind in CHIP_KINDS]
    own = lambda ins, outs, pos: whole(outs[0], pos[3])
    copies.append((lambda ins, outs, pos: ins[0], own, own, "c"))
    return _Plan("ici", cost, [src], [buf], copies), forward


class _Exchanges:
    def __init__(self):
        self.queue = []

    def add(self, plan, front=False):
        if front:
            self.queue.insert(0, plan)
        else:
            self.queue.append(plan)

    def take(self, budget_mb, at_least_one=False):
        chosen, spent = [p for p in self.queue if p.link == "d2d"], 0.0
        for p in self.queue:
            if p.link == "ici" and (spent + p.cost <= budget_mb or (at_least_one and spent == 0.0)):
                chosen.append(p)
                spent += p.cost
        if not chosen:
            return None
        self.queue = [p for p in self.queue if all(p is not ch for ch in chosen)]
        return _merge(chosen)

    def flush(self, budget_mb, until=lambda: False):
        while self.queue and not until():
            _exchange(self.take(budget_mb, at_least_one=True), name="exchange")


def kernel(x, c, w_mod, b_mod, norm_mix, norm_mlp, w_qkv, b_qkv, w_o, b_o, sinks, w_pw1, b_pw1, w_dw, b_dw, conv_ln_g, conv_ln_b, w_pw2, b_pw2, w_up, w_down, final_norm, loss_target, m_w_mod, m_b_mod, m_norm_mix, m_norm_mlp, m_w_qkv, m_b_qkv, m_w_o, m_b_o, m_sinks, m_w_pw1, m_b_pw1, m_w_dw, m_b_dw, m_conv_ln_g, m_conv_ln_b, m_w_pw2, m_b_pw2, m_w_up, m_w_down, m_final_norm, v_w_mod, v_b_mod, v_norm_mix, v_norm_mlp, v_w_qkv, v_b_qkv, v_w_o, v_b_o, v_sinks, v_w_pw1, v_b_pw1, v_w_dw, v_b_dw, v_conv_ln_g, v_conv_ln_b, v_w_pw2, v_b_pw2, v_w_up, v_w_down, v_final_norm):
    bsz, s, d = x.shape
    t = bsz * s
    depth = w_mod.shape[0]
    n_attn, n_conv = w_qkv.shape[0], w_pw1.shape[0]
    qkv_dim = d + 2 * N_KV_HEADS * HEAD_DIM
    mx, my, mc, mq = _position()
    me = 4 * mx + 2 * my + mc
    my_c = jnp.reshape(mc, (1,)).astype(jnp.int32)
    my_qc = jnp.stack([mq, mc]).astype(jnp.int32)
    pending = _Exchanges()
    SMALL, MEDIUM, LARGE = 2.5, 3.5, 6.5

    weights = {}
    order = []
    for i in range(depth):
        j = i // 2
        order += ([(("qkv", j), w_qkv[j], True), (("o", j), w_o[j], False)] if i % 2 == 0 else
                  [(("pw1", j), w_pw1[j], True), (("pw2", j), w_pw2[j], False)])
        order += [(("up", i), w_up[i], True), (("down", i), w_down[i], False)]
    for key, shard, by_cols in order:
        side_by_side = by_cols and shard.shape[1] % LANES == 0

        def ready(buf, key=key, by_cols=by_cols):
            if buf.ndim == 3:
                buf = jnp.transpose(buf, (1, 0, 2)).reshape(buf.shape[1], -1) if by_cols else buf.reshape(-1, buf.shape[2])
            weights[key] = buf
        ici, forward = _gather_plans(shard.astype(BF16), side_by_side, ready)
        ici.then = lambda outs, forward=forward: pending.add(forward(outs), front=True)
        pending.add(ici)

    def weight(key):
        pending.flush(SMALL, until=lambda: key in weights)
        return weights[key]

    small_sharded = [b_pw1, w_dw, b_dw, conv_ln_g, conv_ln_b, b_pw2]
    c_pad = jnp.pad(c, ((0, SUBLANES - bsz), (0, 0)))
    gathered = _all_gather_small(jnp.concatenate([c_pad, _pack(small_sharded, d)], axis=0), name="gather_c",
                                 comm=pending.take(SMALL, True))
    gathered = gathered.reshape(N_DEV, -1, d)
    c_all = gathered[:, :bsz].reshape(N_DEV * bsz, d)
    per_chip = [_unpack(gathered[2 * q, SUBLANES:], [a.shape for a in small_sharded]) for q in range(N_CHIPS)]
    b_pw1_f, w_dw_f, b_dw_f, ln_g_f, ln_b_f, b_pw2_f = [jnp.concatenate([per_chip[q][k] for q in range(N_CHIPS)], axis=-1)
                                                           for k in range(len(small_sharded))]
    w_dw_f = jnp.pad(w_dw_f, ((0, 0), (0, CONV_HALO - CONV_WIDTH), (0, 0)))

    n_mod = w_mod.shape[2]
    b_mod_cols = lax.dynamic_slice_in_dim(b_mod, mq * n_mod, n_mod, axis=1).reshape(depth, 1, n_mod)
    c_all8 = gathered[:, :SUBLANES].reshape(N_DEV * SUBLANES, d)
    mod_part = _mod_fwd(c_all8, w_mod, b_mod_cols, name="mod_fwd")
    peer_rows = lambda kind, pos: pl.ds(SUBLANES * (2 * _peer_shard(kind, pos[0], pos[1]) + pos[2]), SUBLANES)
    got = {}
    pending.add(_Plan("ici", 0.01, [mod_part],[jax.ShapeDtypeStruct((len(CHIP_KINDS), depth, SUBLANES, n_mod), F32)],
                      [(lambda ins, outs, pos, kind=kind: ins[0].at[:, peer_rows(kind, pos)],
                        lambda ins, outs, pos, k=k: outs[0].at[k], lambda ins, outs, pos, k=k: outs[0].at[k], kind)
                       for k, kind in enumerate(CHIP_KINDS)], then=lambda outs: got.update(rows=outs[0])), front=True)
    _exchange(pending.take(0.0, at_least_one=True), name="mod_exchange")
    own = lax.dynamic_slice_in_dim(mod_part, me * SUBLANES, SUBLANES, axis=1)
    parts = [own, got["rows"][0], got["rows"][1], got["rows"][2]]
    part_of = (0, 2, 1, 3)

    def shard(sidx):
        dist, out = jnp.bitwise_xor(mq, sidx), parts[0]
        for distance in (1, 2, 3):
            out = jnp.where(dist == distance, parts[part_of[distance]], out)
        return out

    mod = jnp.stack([shard(sidx) for sidx in range(N_CHIPS)])[:, :, :bsz]
    mod = jnp.transpose(mod, (1, 2, 0, 3)).reshape(depth, bsz, N_MOD, 1, d)
    mods = [[mod[i][:, k] for k in range(N_MOD)] for i in range(depth)]

    saved = []
    xc = x
    h1 = _normmod(xc, norm_mix[0][None], mods[0][1], mods[0][0], name="normmod")
    for i in range(depth):
        j = i // 2
        sh1, sc1, g1, sh2, sc2, g2 = mods[i]
        mlp_norm = (norm_mlp[i][None], sc2, sh2)
        if i % 2 == 0:
            wq, wo = weight(("qkv", j)), weight(("o", j))
            qkv = _mm(h1.reshape(t, d), wq, bias=b_qkv[j], tn=qkv_dim, name="mm_qkv",
                      comm=pending.take(SMALL)).reshape(bsz, s, qkv_dim)
            mix = _attn_fwd(qkv, sinks[j], name="attn_fwd", comm=pending.take(LARGE, True))
            y1, x1, h2 = _mm(mix.reshape(t, d), wo, bias=b_o[j], epi="resid", resid=xc.reshape(t, d), gate=g1, seq=s,
                             norm=mlp_norm, tn=d, name="mm_out", comm=pending.take(SMALL))
            extra = (qkv, mix)
        else:
            wp1, wp2 = weight(("pw1", j)), weight(("pw2", j))
            u = _mm(h1.reshape(t, d), wp1, bias=b_pw1_f[j], out_dtype=F32, tn=d, name="mm_pw1",
                    comm=pending.take(MEDIUM)).reshape(bsz, s, 2 * d)
            mix, conv_v = _conv_fwd(u, w_dw_f[j], b_dw_f[j][None], ln_g_f[j][None], ln_b_f[j][None], name="conv_fwd",
                                    comm=pending.take(LARGE, True))
            y1, x1, h2 = _mm(mix.reshape(t, d), wp2, bias=b_pw2_f[j], epi="resid", resid=xc.reshape(t, d), gate=g1, seq=s,
                             norm=mlp_norm, tn=d, name="mm_out", comm=pending.take(SMALL))
            extra = (u, mix, conv_v)
        act, slope = _mm(h2, weight(("up", i)), epi="relu2", tm=1024, tn=2 * d, name="mm_up", comm=pending.take(LARGE, True))
        mix_norm = (norm_mix[i + 1][None], mods[i + 1][1], mods[i + 1][0]) if i + 1 < depth else None
        y2, x2, *h_next = _mm(act, weight(("down", i)), epi="resid", resid=x1, gate=g2, seq=s, norm=mix_norm,
                              tm=512, tn=d, name="mm_down", comm=pending.take(LARGE, True))
        saved.append((xc, h1, extra, y1, x1.reshape(bsz, s, d), h2, act, y2, slope))
        xc = x2.reshape(bsz, s, d)
        h1 = h_next[0] if h_next else None
    pending.flush(LARGE)

    last_y2, last_g2 = saved[-1][7].reshape(bsz, s, d), mods[-1][5]
    dx, loss_cols, d_final, dyb, dg2, _ = _loss_head(xc, loss_target, final_norm[None], (last_y2, last_g2), name="loss_head")
    loss = lax.psum(0.5 / d * jnp.sum(loss_cols), ("x", "y", "c"))

    totals = {}

    def reduce_scatter(name, layer, n_layers, grad):
        view = grad.reshape(N_CHIPS, 2, grad.shape[1] // 2, grad.shape[2])
        half_shape = (N_CHIPS, 1) + view.shape[2:]

        def scatter(outs):
            chip = _add_half(view, outs[0], my_c, name="rs_chipsum")
            copies = [(lambda ins, outs, pos, kind=kind: ins[0].at[pl.ds(_peer_shard(kind, pos[0], pos[1]), 1)],
                       lambda ins, outs, pos, k=k: outs[0].at[pl.ds(k, 1)],
                       lambda ins, outs, pos, k=k: outs[0].at[pl.ds(k, 1)], kind) for k, kind in enumerate(CHIP_KINDS)]

            def total(outs):
                totals[name] = _add_pieces(chip, outs[0], my_qc, totals.get(name), layer, n_layers, name="rs_total")

            pending.add(_Plan("ici", 3 * _nbytes(chip.shape[1:], BF16) / MB, [chip],
                              [jax.ShapeDtypeStruct((3,) + chip.shape[1:], BF16)], copies, then=total))

        pending.add(_Plan("d2d", _nbytes(half_shape, F32) / MB, [view], [jax.ShapeDtypeStruct(half_shape, F32)],
                          [(lambda ins, outs, pos: ins[0].at[:, pl.ds(1 - pos[2], 1)],
                            lambda ins, outs, pos: outs[0], lambda ins, outs, pos: outs[0], "c")], then=scatter))

    dmods, small = [None] * depth, {}
    for i in reversed(range(depth)):
        j = i // 2
        xin, h1, extra, y1, x1, h2, act, _, slope = saved[i]
        sh1, sc1, g1, sh2, sc2, g2 = mods[i]
        dyb = dyb.reshape(t, d)
        reduce_scatter("down", i, depth,
                       _mm_tn(act, dyb, name="dw_down", comm=pending.take(LARGE)).reshape(N_CHIPS, -1, d))
        dup = _mm(dyb, weights["down", i], nt=True, epi="dact", act=slope, tm=1024, tn=2 * d, name="mm_dact",
                  comm=pending.take(LARGE))
        reduce_scatter("up", i, depth,
                       _mm_tn(h2.reshape(t, d), dup, col_shards=N_CHIPS, name="dw_up", comm=pending.take(LARGE)))
        dh2 = _mm(dup, weights["up", i], nt=True, tm=512, tn=d, name="mm_dh2",
                  comm=pending.take(LARGE)).reshape(bsz, s, d)
        dx1, p2, dsh2, dyb, dg1, sdx = _normmod_bwd(x1, dh2, dx, norm_mlp[i][None], sc2, (y1.reshape(bsz, s, d), g1),
                                                    name="normmod_bwd", comm=pending.take(SMALL))
        dyb = dyb.reshape(t, d)
        d_bias_out = jnp.sum(g1 * sdx, axis=(0, 1))
        if i % 2 == 0:
            qkv, mix = extra
            small["b_o", j] = d_bias_out
            reduce_scatter("o", j, n_attn,
                           _mm_tn(mix.reshape(t, d), dyb, name="dw_sq", comm=pending.take(SMALL)).reshape(N_CHIPS, -1, d))
            dmix = _mm(dyb, weights["o", j], nt=True, tn=d, name="mm_dmix", comm=pending.take(SMALL)).reshape(bsz, s, d)
            dqkv, d_bqkv, d_sink = _attn_bwd(qkv, dmix, sinks[j], name="attn_bwd", comm=pending.take(2 * LARGE))
            small["b_qkv", j], small["sinks", j] = d_bqkv[0], d_sink[:, 0]
            dqkv = dqkv.reshape(t, qkv_dim)
            dwq = _mm_tn(h1.reshape(t, d), dqkv, tn=qkv_dim, name="dw_qkv", comm=pending.take(SMALL))
            reduce_scatter("qkv", j, n_attn, jnp.transpose(dwq.reshape(d, N_CHIPS, -1), (1, 0, 2)))
            dh_operands, dh_name = (dqkv, weights["qkv", j]), "mm_dh1a"
        else:
            u, mix, conv_v = extra
            small["b_pw2", j] = d_bias_out
            reduce_scatter("pw2", j, n_conv,
                           _mm_tn(mix.reshape(t, d), dyb, name="dw_sq", comm=pending.take(SMALL)).reshape(N_CHIPS, -1, d))
            dz = _mm(dyb, weights["pw2", j], nt=True, out_dtype=F32, tn=d, name="mm_dz", comm=pending.take(SMALL)).reshape(bsz, s, d)
            dv, d_lng, d_lnb, d_bdw = _conv_bwd_ln(dz, conv_v, ln_g_f[j][None], ln_b_f[j][None], name="conv_bwd_ln")
            du, d_bpw1, d_wdw = _conv_bwd_taps(dv, u, w_dw_f[j], name="conv_bwd_taps", comm=pending.take(2 * LARGE))
            small["ln_g", j], small["ln_b", j], small["b_dw", j] = d_lng[0], d_lnb[0], d_bdw[0]
            small["b_pw1", j], small["w_dw", j] = d_bpw1[0], jnp.sum(d_wdw[:CONV_WIDTH], axis=1)
            du = du.reshape(t, 2 * d)
            reduce_scatter("pw1", j, n_conv,
                           _mm_tn(h1.reshape(t, d), du, col_shards=N_CHIPS, name="dw_pw1", comm=pending.take(MEDIUM)))
            dh_operands, dh_name = (du, weights["pw1", j]), "mm_dh1c"
        below = (saved[i - 1][7], mods[i - 1][5]) if i > 0 else None
        dx, p1, dsh1, *gate_grads = _mm(*dh_operands, nt=True, tm=512, tn=d, epi="normbwd", seq=s, name=dh_name,
                                        norm_bwd=(xin.reshape(t, d), dx1.reshape(t, d), norm_mix[i][None], sc1, below),
                                        comm=pending.take(MEDIUM, i == 0))
        dx = dx.reshape(bsz, s, d)
        small["norm_mix", i] = jnp.sum((1.0 + sc1) * p1, axis=(0, 1))
        small["norm_mlp", i] = jnp.sum((1.0 + sc2) * p2, axis=(0, 1))
        dmods[i] = jnp.concatenate([dsh1, norm_mix[i] * p1, dg1, dsh2, norm_mlp[i] * p2, dg2], axis=1)
        if i > 0:
            dyb, dg2, _ = gate_grads
    grad_x = dx

    small["final_norm"] = d_final[0]
    small_names = ([("norm_mix", i) for i in range(depth)] + [("norm_mlp", i) for i in range(depth)]
                   + [("b_o", j) for j in range(n_attn)]
                   + [(nm, j) for nm in ("b_pw1", "w_dw", "b_dw", "ln_g", "ln_b", "b_pw2") for j in range(n_conv)]
                   + ["final_norm"] + [(nm, j) for nm in ("b_qkv", "sinks") for j in range(n_attn)])
    small_list = [small[k] for k in small_names]
    small_pack = _pack(small_list, d)
    dmod_rows = jnp.stack(dmods).reshape(depth * bsz * N_MOD, d)
    n_small = small_pack.shape[0]
    pending.flush(LARGE)
    big_names = ["qkv", "o", "pw1", "pw2", "up", "down"]
    bufs = [totals[nm] for nm in big_names]
    copies = []
    for a, buf in enumerate(bufs):
        for layer in range(buf.shape[0]):
            half = lambda ref, h, layer=layer: ref.at[pl.ds(layer, 1), pl.ds(h, 1)]
            copies.append((lambda ins, outs, pos, a=a, half=half: half(ins[a], pos[2]),
                           lambda ins, outs, pos, a=a, half=half: half(outs[a], pos[2]),
                           lambda ins, outs, pos, a=a, half=half: half(outs[a], 1 - pos[2]), "c"))
    shared = {}
    share = _Plan("d2d", 0.0, bufs, [jax.ShapeDtypeStruct(b.shape, F32) for b in bufs], copies,
                  {a: a for a in range(len(bufs))}, lambda outs: shared.update(zip(big_names, outs)))
    gathered = _all_gather_small(jnp.concatenate([small_pack, _pack([dmod_rows], d)], axis=0), name="gather_small",
                                 comm=share)
    gathered = gathered.reshape(N_DEV, -1, d)
    dmod_all = gathered[:, n_small:n_small + depth * bsz * N_MOD].reshape(N_DEV, depth, bsz, N_MOD, d)
    small_sum, g_b_mod = _sum_devices(gathered[:, :n_small], dmod_all, name="sum_devices")
    small_tot = dict(zip(small_names, _unpack(small_sum, [a.shape for a in small_list])))
    stacked = lambda nm, count: jnp.stack([small_tot[nm, k] for k in range(count)])
    g_norm_mix, g_norm_mlp = stacked("norm_mix", depth), stacked("norm_mlp", depth)
    g_b_qkv, g_b_o, g_sinks = stacked("b_qkv", n_attn), stacked("b_o", n_attn), stacked("sinks", n_attn)
    g_final = small_tot["final_norm"]
    g_b_mod = g_b_mod.reshape(depth, N_MOD * d)
    shard_cols = lambda g: lax.dynamic_slice_in_dim(g, mq * (g.shape[-1] // N_CHIPS), g.shape[-1] // N_CHIPS, axis=g.ndim - 1)
    g_b_pw1, g_w_dw, g_b_dw, g_ln_g, g_ln_b, g_b_pw2 = [shard_cols(stacked(nm, n_conv))
                                                        for nm in ("b_pw1", "w_dw", "b_dw", "ln_g", "ln_b", "b_pw2")]

    dmod_cols = jnp.transpose(dmod_all, (1, 0, 2, 3, 4)).reshape(depth, N_DEV * bsz, N_MOD * d)
    dmod_cols = lax.dynamic_slice_in_dim(dmod_cols, mq * n_mod, n_mod, axis=2)
    g_w_mod = _mod_bwd(c_all, dmod_cols, name="mod_bwd")

    def adam(w, g, m, v, name):
        two_d = lambda a: a.reshape(-1, a.shape[-1])
        return [r.reshape(w.shape) for r in _adamw(two_d(w), two_d(g), two_d(m), two_d(v), name=name)]

    results = {"w_mod": (g_w_mod,) + tuple(adam(w_mod, g_w_mod, m_w_mod, v_w_mod, "adamw"))}

    g_w_qkv, g_w_o, g_w_pw1, g_w_pw2, g_w_up, g_w_down = [
        shared[nm].reshape(shared[nm].shape[0], -1, shared[nm].shape[3]) for nm in big_names]
    for nm, w, g, m, v in (("w_qkv", w_qkv, g_w_qkv, m_w_qkv, v_w_qkv),
                           ("w_o", w_o, g_w_o, m_w_o, v_w_o), ("w_pw1", w_pw1, g_w_pw1, m_w_pw1, v_w_pw1),
                           ("w_pw2", w_pw2, g_w_pw2, m_w_pw2, v_w_pw2), ("w_up", w_up, g_w_up, m_w_up, v_w_up),
                           ("w_down", w_down, g_w_down, m_w_down, v_w_down)):
        results[nm] = (g,) + tuple(adam(w, g, m, v, "adamw"))
    small_w = dict(b_mod=(b_mod, g_b_mod, m_b_mod, v_b_mod), norm_mix=(norm_mix, g_norm_mix, m_norm_mix, v_norm_mix),
                   norm_mlp=(norm_mlp, g_norm_mlp, m_norm_mlp, v_norm_mlp), b_qkv=(b_qkv, g_b_qkv, m_b_qkv, v_b_qkv),
                   b_o=(b_o, g_b_o, m_b_o, v_b_o), sinks=(sinks, g_sinks, m_sinks, v_sinks),
                   b_pw1=(b_pw1, g_b_pw1, m_b_pw1, v_b_pw1), w_dw=(w_dw, g_w_dw, m_w_dw, v_w_dw),
                   b_dw=(b_dw, g_b_dw, m_b_dw, v_b_dw), conv_ln_g=(conv_ln_g, g_ln_g, m_conv_ln_g, v_conv_ln_g),
                   conv_ln_b=(conv_ln_b, g_ln_b, m_conv_ln_b, v_conv_ln_b), b_pw2=(b_pw2, g_b_pw2, m_b_pw2, v_b_pw2),
                   final_norm=(final_norm, g_final, m_final_norm, v_final_norm))
    names = list(small_w)
    deltas, new_ms, new_vs = _adamw_packed(*[[small_w[nm][k] for nm in names] for k in range(4)], d, name="adamw_small")
    for k, nm in enumerate(names):
        results[nm] = (small_w[nm][1], deltas[k], new_ms[k], new_vs[k])

    weight_order = ["w_mod", "b_mod", "norm_mix", "norm_mlp", "w_qkv", "b_qkv", "w_o", "b_o", "sinks", "w_pw1", "b_pw1",
                    "w_dw", "b_dw", "conv_ln_g", "conv_ln_b", "w_pw2", "b_pw2", "w_up", "w_down", "final_norm"]
    return (loss, grad_x, *[results[nm][0] for nm in weight_order], *[results[nm][1] for nm in weight_order],
            *[results[nm][2] for nm in weight_order], *[results[nm][3] for nm in weight_order])
```
